```python
import math
import jax, jax.numpy as jnp
from jax import lax
import numpy as np

D_MODEL = 1024
BATCH = 8
SEQ = 4096
DEPTH = 4

N_MIXERS = 2
SSM_EXPAND = 2
D_INNER = SSM_EXPAND * D_MODEL
HEAD_DIM = 64
N_SSM_HEADS = D_INNER // HEAD_DIM
N_GROUPS = 4
HEADS_PER_GROUP = N_SSM_HEADS // N_GROUPS
D_STATE = 128
SSM_CONV = 4
CONV_DIM = D_INNER + 2 * N_GROUPS * D_STATE
IN_PROJ_DIM = 2 * D_INNER + 2 * N_GROUPS * D_STATE + N_SSM_HEADS
CHUNK = 128
CF_KERNEL = 31
N_MEM = 256
XA_HEADS = 4
XA_HEAD_DIM = D_MODEL // XA_HEADS
D_FF = 2816
FFN_CONV = 3
EPS = 1e-6

kernel_name = "hybrid_ssd_conformer_memxattn_convffn"


def rmsnorm(x, g):
    xf = x.astype(jnp.float32)
    y = xf * lax.rsqrt(jnp.mean(xf * xf, axis=-1, keepdims=True) + EPS)
    return (y * g.astype(jnp.float32)).astype(x.dtype)


def layernorm(x, g, b):
    xf = x.astype(jnp.float32)
    mu = jnp.mean(xf, axis=-1, keepdims=True)
    var = jnp.mean(jnp.square(xf - mu), axis=-1, keepdims=True)
    y = (xf - mu) * lax.rsqrt(var + EPS)
    return (y * g.astype(jnp.float32) + b.astype(jnp.float32)).astype(x.dtype)


def causal_dwconv(x, w, b):
    K, C = w.shape
    y = lax.conv_general_dilated(
        x, w[:, None, :].astype(x.dtype), window_strides=(1,), padding=[(K - 1, 0)],
        dimension_numbers=("NWC", "WIO", "NWC"), feature_group_count=C)
    return y + b.astype(x.dtype)


def ssd_mixer(h, in_w, conv_w, conv_b, dt_bias, A_log, D_skip, norm_g, out_w):
    Bsz, L, _ = h.shape
    nc = L // CHUNK
    G, Hg, P, N, Q = N_GROUPS, HEADS_PER_GROUP, HEAD_DIM, D_STATE, CHUNK
    f32 = jnp.float32
    proj = h @ in_w
    z, xBC, dt_raw = jnp.split(proj, [D_INNER, D_INNER + CONV_DIM], axis=-1)
    xBC = jax.nn.silu(causal_dwconv(xBC, conv_w, conv_b))
    xs, Bm, Cm = jnp.split(xBC, [D_INNER, D_INNER + G * N], axis=-1)
    xs = xs.astype(f32).reshape(Bsz, nc, Q, G, Hg, P)
    Bm = Bm.astype(f32).reshape(Bsz, nc, Q, G, N)
    Cm = Cm.astype(f32).reshape(Bsz, nc, Q, G, N)
    dt = jax.nn.softplus(dt_raw.astype(f32) + dt_bias.astype(f32)).reshape(Bsz, nc, Q, G, Hg)
    A = -jnp.exp(A_log.astype(f32)).reshape(G, Hg)
    cs = jnp.cumsum(dt * A, axis=2)
    xdt = xs * dt[..., None]
    tri = jnp.tril(jnp.ones((Q, Q), dtype=bool))
    seg = cs[:, :, :, None] - cs[:, :, None, :]
    decay = jnp.exp(jnp.where(tri[None, None, :, :, None, None], seg, -jnp.inf))
    CB = jnp.einsum("bcign,bcjgn->bcijg", Cm, Bm)
    y_diag = jnp.einsum("bcijgh,bcjghp->bcighp", CB[..., None] * decay, xdt)
    decay_to_end = jnp.exp(cs[:, :, -1:] - cs)
    states = jnp.einsum("bcjgn,bcjgh,bcjghp->bcghpn", Bm, decay_to_end, xdt)
    chunk_decay = jnp.exp(cs[:, :, -1])

    def step(carry, inp):
        st, dec = inp
        return carry * dec[..., None, None] + st, carry

    h0 = jnp.zeros((Bsz, G, Hg, P, N), f32)
    _, prev = lax.scan(step, h0, (jnp.swapaxes(states, 0, 1), jnp.swapaxes(chunk_decay, 0, 1)))
    prev = jnp.swapaxes(prev, 0, 1)
    y_off = jnp.einsum("bcign,bcghpn,bcigh->bcighp", Cm, prev, jnp.exp(cs))
    y = y_diag + y_off + xs * D_skip.astype(f32).reshape(G, Hg)[:, :, None]
    y = y.reshape(Bsz, L, D_INNER).astype(h.dtype)
    y = rmsnorm(y * jax.nn.silu(z), norm_g)
    return y @ out_w


def conformer_conv(h, pw1_w, pw1_b, dw_w, dw_b, ln_g, ln_b, pw2_w, pw2_b):
    u = h @ pw1_w + pw1_b
    a, gt = jnp.split(u, 2, axis=-1)
    c = causal_dwconv(a * jax.nn.sigmoid(gt), dw_w, dw_b)
    c = jax.nn.silu(layernorm(c, ln_g, ln_b))
    return c @ pw2_w + pw2_b


def mem_xattn(h, mem, mem_g, q_w, kv_w, o_w):
    Bsz, L, _ = h.shape
    m = rmsnorm(mem, mem_g)
    q = (h @ q_w).reshape(Bsz, L, XA_HEADS, XA_HEAD_DIM)
    k, v = jnp.split(m @ kv_w, 2, axis=-1)
    k = k.reshape(Bsz, N_MEM, XA_HEADS, XA_HEAD_DIM)
    v = v.reshape(Bsz, N_MEM, XA_HEADS, XA_HEAD_DIM)
    s = jnp.einsum("blhd,bmhd->bhlm", q, k).astype(jnp.float32) * (XA_HEAD_DIM ** -0.5)
    p = jax.nn.softmax(s, axis=-1).astype(v.dtype)
    o = jnp.einsum("bhlm,bmhd->blhd", p, v).reshape(Bsz, L, D_MODEL)
    return o @ o_w


def conv_ffn(h, in_w, conv_w, conv_b, out_w):
    u = causal_dwconv(h @ in_w, conv_w, conv_b)
    g, v = jnp.split(u, 2, axis=-1)
    return (jax.nn.silu(g) * v) @ out_w


def _fwd_setup_inputs(seed: int = 0) -> dict:
    key = jax.random.key(seed)
    ks = jax.random.split(key, 32)
    nA = (DEPTH + 1) // 2
    nB = DEPTH // 2
    D = D_MODEL

    def w(k, shape, fan_in):
        return jax.random.normal(k, shape, jnp.float32) * (fan_in ** -0.5)

    def gain(k, shape):
        return 1.0 + 0.02 * jax.random.normal(k, shape, jnp.float32)

    def bias(k, shape):
        return 0.02 * jax.random.normal(k, shape, jnp.float32)

    log_dt = jax.random.uniform(ks[5], (nA, N_SSM_HEADS), jnp.float32, math.log(1e-3), math.log(1e-1))
    dt0 = jnp.exp(log_dt)
    return {
        "x": jax.random.normal(ks[0], (BATCH, SEQ, D), jnp.float32),
        "mem": jax.random.normal(ks[1], (BATCH, N_MEM, D), jnp.float32),
        "norm_g": gain(ks[2], (DEPTH, 6, D)),
        "ssm_in_w": w(ks[3], (nA, D, IN_PROJ_DIM), D),
        "ssm_conv_w": w(ks[4], (nA, SSM_CONV, CONV_DIM), SSM_CONV),
        "ssm_conv_b": bias(ks[6], (nA, CONV_DIM)),
        "ssm_dt_bias": dt0 + jnp.log(-jnp.expm1(-dt0)),
        "ssm_A_log": jnp.log(jax.random.uniform(ks[7], (nA, N_SSM_HEADS), jnp.float32, 1.0, 16.0)),
        "ssm_D": 1.0 + 0.1 * jax.random.normal(ks[8], (nA, N_SSM_HEADS), jnp.float32),
        "ssm_norm_g": gain(ks[9], (nA, D_INNER)),
        "ssm_out_w": w(ks[10], (nA, D_INNER, D), D_INNER),
        "cf_pw1_w": w(ks[11], (nB, D, 2 * D), D),
        "cf_pw1_b": bias(ks[12], (nB, 2 * D)),
        "cf_dw_w": w(ks[13], (nB, CF_KERNEL, D), CF_KERNEL),
        "cf_dw_b": bias(ks[14], (nB, D)),
        "cf_ln_g": gain(ks[15], (nB, D)),
        "cf_ln_b": bias(ks[16], (nB, D)),
        "cf_pw2_w": w(ks[17], (nB, D, D), D),
        "cf_pw2_b": bias(ks[18], (nB, D)),
        "xa_mem_g": gain(ks[19], (DEPTH, D)),
        "xa_q_w": w(ks[20], (DEPTH, D, D), D),
        "xa_kv_w": w(ks[21], (DEPTH, D, 2 * D), D),
        "xa_o_w": w(ks[22], (DEPTH, D, D), D),
        "ffn_in_w": w(ks[23], (DEPTH, D, 2 * D_FF), D),
        "ffn_conv_w": w(ks[24], (DEPTH, FFN_CONV, 2 * D_FF), FFN_CONV),
        "ffn_conv_b": bias(ks[25], (DEPTH, 2 * D_FF)),
        "ffn_out_w": w(ks[26], (DEPTH, D_FF, D), D_FF),
    }


def _fwd_reference(x, mem, norm_g,
              ssm_in_w, ssm_conv_w, ssm_conv_b, ssm_dt_bias, ssm_A_log, ssm_D, ssm_norm_g, ssm_out_w,
              cf_pw1_w, cf_pw1_b, cf_dw_w, cf_dw_b, cf_ln_g, cf_ln_b, cf_pw2_w, cf_pw2_b,
              xa_mem_g, xa_q_w, xa_kv_w, xa_o_w,
              ffn_in_w, ffn_conv_w, ffn_conv_b, ffn_out_w):
    for i in range(DEPTH):
        g = norm_g[i]
        j = i // N_MIXERS
        h = rmsnorm(x, g[0])
        if i % N_MIXERS == 0:
            mix = ssd_mixer(h, ssm_in_w[j], ssm_conv_w[j], ssm_conv_b[j], ssm_dt_bias[j],
                            ssm_A_log[j], ssm_D[j], ssm_norm_g[j], ssm_out_w[j])
        else:
            mix = conformer_conv(h, cf_pw1_w[j], cf_pw1_b[j], cf_dw_w[j], cf_dw_b[j],
                                 cf_ln_g[j], cf_ln_b[j], cf_pw2_w[j], cf_pw2_b[j])
        x = x + rmsnorm(mix, g[1])
        a = mem_xattn(rmsnorm(x, g[2]), mem, xa_mem_g[i], xa_q_w[i], xa_kv_w[i], xa_o_w[i])
        x = x + rmsnorm(a, g[3])
        f = conv_ffn(rmsnorm(x, g[4]), ffn_in_w[i], ffn_conv_w[i], ffn_conv_b[i], ffn_out_w[i])
        x = x + rmsnorm(f, g[5])
    return x


import jax as _jax
import jax.numpy as _jnp

TWIN_FORMAT = 'train_step'
FWD_PARAMS = ['x', 'mem', 'norm_g', 'ssm_in_w', 'ssm_conv_w', 'ssm_conv_b', 'ssm_dt_bias', 'ssm_A_log', 'ssm_D', 'ssm_norm_g', 'ssm_out_w', 'cf_pw1_w', 'cf_pw1_b', 'cf_dw_w', 'cf_dw_b', 'cf_ln_g', 'cf_ln_b', 'cf_pw2_w', 'cf_pw2_b', 'xa_mem_g', 'xa_q_w', 'xa_kv_w', 'xa_o_w', 'ffn_in_w', 'ffn_conv_w', 'ffn_conv_b', 'ffn_out_w']
TWIN_WEIGHTS = ['norm_g', 'ssm_in_w', 'ssm_conv_w', 'ssm_conv_b', 'ssm_dt_bias', 'ssm_A_log', 'ssm_D', 'ssm_norm_g', 'ssm_out_w', 'cf_pw1_w', 'cf_pw1_b', 'cf_dw_w', 'cf_dw_b', 'cf_ln_g', 'cf_ln_b', 'cf_pw2_w', 'cf_pw2_b', 'xa_mem_g', 'xa_q_w', 'xa_kv_w', 'xa_o_w', 'ffn_in_w', 'ffn_conv_w', 'ffn_conv_b', 'ffn_out_w']
TWIN_DIFF_INPUT = 'x'
TWIN_INPUTS = ['x', 'mem', 'norm_g', 'ssm_in_w', 'ssm_conv_w', 'ssm_conv_b', 'ssm_dt_bias', 'ssm_A_log', 'ssm_D', 'ssm_norm_g', 'ssm_out_w', 'cf_pw1_w', 'cf_pw1_b', 'cf_dw_w', 'cf_dw_b', 'cf_ln_g', 'cf_ln_b', 'cf_pw2_w', 'cf_pw2_b', 'xa_mem_g', 'xa_q_w', 'xa_kv_w', 'xa_o_w', 'ffn_in_w', 'ffn_conv_w', 'ffn_conv_b', 'ffn_out_w', 'loss_target', 'm_norm_g', 'm_ssm_in_w', 'm_ssm_conv_w', 'm_ssm_conv_b', 'm_ssm_dt_bias', 'm_ssm_A_log', 'm_ssm_D', 'm_ssm_norm_g', 'm_ssm_out_w', 'm_cf_pw1_w', 'm_cf_pw1_b', 'm_cf_dw_w', 'm_cf_dw_b', 'm_cf_ln_g', 'm_cf_ln_b', 'm_cf_pw2_w', 'm_cf_pw2_b', 'm_xa_mem_g', 'm_xa_q_w', 'm_xa_kv_w', 'm_xa_o_w', 'm_ffn_in_w', 'm_ffn_conv_w', 'm_ffn_conv_b', 'm_ffn_out_w', 'v_norm_g', 'v_ssm_in_w', 'v_ssm_conv_w', 'v_ssm_conv_b', 'v_ssm_dt_bias', 'v_ssm_A_log', 'v_ssm_D', 'v_ssm_norm_g', 'v_ssm_out_w', 'v_cf_pw1_w', 'v_cf_pw1_b', 'v_cf_dw_w', 'v_cf_dw_b', 'v_cf_ln_g', 'v_cf_ln_b', 'v_cf_pw2_w', 'v_cf_pw2_b', 'v_xa_mem_g', 'v_xa_q_w', 'v_xa_kv_w', 'v_xa_o_w', 'v_ffn_in_w', 'v_ffn_conv_w', 'v_ffn_conv_b', 'v_ffn_out_w']
TWIN_OUTPUTS = ['loss', 'grad_x', 'grad_norm_g', 'grad_ssm_in_w', 'grad_ssm_conv_w', 'grad_ssm_conv_b', 'grad_ssm_dt_bias', 'grad_ssm_A_log', 'grad_ssm_D', 'grad_ssm_norm_g', 'grad_ssm_out_w', 'grad_cf_pw1_w', 'grad_cf_pw1_b', 'grad_cf_dw_w', 'grad_cf_dw_b', 'grad_cf_ln_g', 'grad_cf_ln_b', 'grad_cf_pw2_w', 'grad_cf_pw2_b', 'grad_xa_mem_g', 'grad_xa_q_w', 'grad_xa_kv_w', 'grad_xa_o_w', 'grad_ffn_in_w', 'grad_ffn_conv_w', 'grad_ffn_conv_b', 'grad_ffn_out_w', 'delta_norm_g', 'delta_ssm_in_w', 'delta_ssm_conv_w', 'delta_ssm_conv_b', 'delta_ssm_dt_bias', 'delta_ssm_A_log', 'delta_ssm_D', 'delta_ssm_norm_g', 'delta_ssm_out_w', 'delta_cf_pw1_w', 'delta_cf_pw1_b', 'delta_cf_dw_w', 'delta_cf_dw_b', 'delta_cf_ln_g', 'delta_cf_ln_b', 'delta_cf_pw2_w', 'delta_cf_pw2_b', 'delta_xa_mem_g', 'delta_xa_q_w', 'delta_xa_kv_w', 'delta_xa_o_w', 'delta_ffn_in_w', 'delta_ffn_conv_w', 'delta_ffn_conv_b', 'delta_ffn_out_w', 'new_m_norm_g', 'new_m_ssm_in_w', 'new_m_ssm_conv_w', 'new_m_ssm_conv_b', 'new_m_ssm_dt_bias', 'new_m_ssm_A_log', 'new_m_ssm_D', 'new_m_ssm_norm_g', 'new_m_ssm_out_w', 'new_m_cf_pw1_w', 'new_m_cf_pw1_b', 'new_m_cf_dw_w', 'new_m_cf_dw_b', 'new_m_cf_ln_g', 'new_m_cf_ln_b', 'new_m_cf_pw2_w', 'new_m_cf_pw2_b', 'new_m_xa_mem_g', 'new_m_xa_q_w', 'new_m_xa_kv_w', 'new_m_xa_o_w', 'new_m_ffn_in_w', 'new_m_ffn_conv_w', 'new_m_ffn_conv_b', 'new_m_ffn_out_w', 'new_v_norm_g', 'new_v_ssm_in_w', 'new_v_ssm_conv_w', 'new_v_ssm_conv_b', 'new_v_ssm_dt_bias', 'new_v_ssm_A_log', 'new_v_ssm_D', 'new_v_ssm_norm_g', 'new_v_ssm_out_w', 'new_v_cf_pw1_w', 'new_v_cf_pw1_b', 'new_v_cf_dw_w', 'new_v_cf_dw_b', 'new_v_cf_ln_g', 'new_v_cf_ln_b', 'new_v_cf_pw2_w', 'new_v_cf_pw2_b', 'new_v_xa_mem_g', 'new_v_xa_q_w', 'new_v_xa_kv_w', 'new_v_xa_o_w', 'new_v_ffn_in_w', 'new_v_ffn_conv_w', 'new_v_ffn_conv_b', 'new_v_ffn_out_w']
TWIN_LEAF_KINDS = {'loss': 'loss', 'grad_x': 'grad_x', 'grad_norm_g': 'grad_w', 'grad_ssm_in_w': 'grad_w', 'grad_ssm_conv_w': 'grad_w', 'grad_ssm_conv_b': 'grad_w', 'grad_ssm_dt_bias': 'grad_w', 'grad_ssm_A_log': 'grad_w', 'grad_ssm_D': 'grad_w', 'grad_ssm_norm_g': 'grad_w', 'grad_ssm_out_w': 'grad_w', 'grad_cf_pw1_w': 'grad_w', 'grad_cf_pw1_b': 'grad_w', 'grad_cf_dw_w': 'grad_w', 'grad_cf_dw_b': 'grad_w', 'grad_cf_ln_g': 'grad_w', 'grad_cf_ln_b': 'grad_w', 'grad_cf_pw2_w': 'grad_w', 'grad_cf_pw2_b': 'grad_w', 'grad_xa_mem_g': 'grad_w', 'grad_xa_q_w': 'grad_w', 'grad_xa_kv_w': 'grad_w', 'grad_xa_o_w': 'grad_w', 'grad_ffn_in_w': 'grad_w', 'grad_ffn_conv_w': 'grad_w', 'grad_ffn_conv_b': 'grad_w', 'grad_ffn_out_w': 'grad_w', 'delta_norm_g': 'delta_w', 'delta_ssm_in_w': 'delta_w', 'delta_ssm_conv_w': 'delta_w', 'delta_ssm_conv_b': 'delta_w', 'delta_ssm_dt_bias': 'delta_w', 'delta_ssm_A_log': 'delta_w', 'delta_ssm_D': 'delta_w', 'delta_ssm_norm_g': 'delta_w', 'delta_ssm_out_w': 'delta_w', 'delta_cf_pw1_w': 'delta_w', 'delta_cf_pw1_b': 'delta_w', 'delta_cf_dw_w': 'delta_w', 'delta_cf_dw_b': 'delta_w', 'delta_cf_ln_g': 'delta_w', 'delta_cf_ln_b': 'delta_w', 'delta_cf_pw2_w': 'delta_w', 'delta_cf_pw2_b': 'delta_w', 'delta_xa_mem_g': 'delta_w', 'delta_xa_q_w': 'delta_w', 'delta_xa_kv_w': 'delta_w', 'delta_xa_o_w': 'delta_w', 'delta_ffn_in_w': 'delta_w', 'delta_ffn_conv_w': 'delta_w', 'delta_ffn_conv_b': 'delta_w', 'delta_ffn_out_w': 'delta_w', 'new_m_norm_g': 'new_m', 'new_m_ssm_in_w': 'new_m', 'new_m_ssm_conv_w': 'new_m', 'new_m_ssm_conv_b': 'new_m', 'new_m_ssm_dt_bias': 'new_m', 'new_m_ssm_A_log': 'new_m', 'new_m_ssm_D': 'new_m', 'new_m_ssm_norm_g': 'new_m', 'new_m_ssm_out_w': 'new_m', 'new_m_cf_pw1_w': 'new_m', 'new_m_cf_pw1_b': 'new_m', 'new_m_cf_dw_w': 'new_m', 'new_m_cf_dw_b': 'new_m', 'new_m_cf_ln_g': 'new_m', 'new_m_cf_ln_b': 'new_m', 'new_m_cf_pw2_w': 'new_m', 'new_m_cf_pw2_b': 'new_m', 'new_m_xa_mem_g': 'new_m', 'new_m_xa_q_w': 'new_m', 'new_m_xa_kv_w': 'new_m', 'new_m_xa_o_w': 'new_m', 'new_m_ffn_in_w': 'new_m', 'new_m_ffn_conv_w': 'new_m', 'new_m_ffn_conv_b': 'new_m', 'new_m_ffn_out_w': 'new_m', 'new_v_norm_g': 'new_v', 'new_v_ssm_in_w': 'new_v', 'new_v_ssm_conv_w': 'new_v', 'new_v_ssm_conv_b': 'new_v', 'new_v_ssm_dt_bias': 'new_v', 'new_v_ssm_A_log': 'new_v', 'new_v_ssm_D': 'new_v', 'new_v_ssm_norm_g': 'new_v', 'new_v_ssm_out_w': 'new_v', 'new_v_cf_pw1_w': 'new_v', 'new_v_cf_pw1_b': 'new_v', 'new_v_cf_dw_w': 'new_v', 'new_v_cf_dw_b': 'new_v', 'new_v_cf_ln_g': 'new_v', 'new_v_cf_ln_b': 'new_v', 'new_v_cf_pw2_w': 'new_v', 'new_v_cf_pw2_b': 'new_v', 'new_v_xa_mem_g': 'new_v', 'new_v_xa_q_w': 'new_v', 'new_v_xa_kv_w': 'new_v', 'new_v_xa_o_w': 'new_v', 'new_v_ffn_in_w': 'new_v', 'new_v_ffn_conv_w': 'new_v', 'new_v_ffn_conv_b': 'new_v', 'new_v_ffn_out_w': 'new_v'}


def _forward(args):
    return _fwd_reference(*[args[k] for k in FWD_PARAMS])


def _output_shape():
    out = _jax.eval_shape(lambda: _forward(_fwd_setup_inputs(0)))
    return out.shape, out.dtype

N_MICROBATCH = 1
ADAM_LR = 0.001
ADAM_B1 = 0.9
ADAM_B2 = 0.999
ADAM_EPS = 1e-08
ADAM_WD = 0.01
ADAM_STEP = 10
PER_EXAMPLE_BATCH_AXIS = {'x': 0, 'mem': 0, 'loss_target': 0}
SHARED_INPUTS = []
_WEIGHT_DTYPES = {'norm_g': _jnp.float32, 'ssm_in_w': _jnp.float32, 'ssm_conv_w': _jnp.float32, 'ssm_conv_b': _jnp.float32, 'ssm_dt_bias': _jnp.float32, 'ssm_A_log': _jnp.float32, 'ssm_D': _jnp.float32, 'ssm_norm_g': _jnp.float32, 'ssm_out_w': _jnp.float32, 'cf_pw1_w': _jnp.float32, 'cf_pw1_b': _jnp.float32, 'cf_dw_w': _jnp.float32, 'cf_dw_b': _jnp.float32, 'cf_ln_g': _jnp.float32, 'cf_ln_b': _jnp.float32, 'cf_pw2_w': _jnp.float32, 'cf_pw2_b': _jnp.float32, 'xa_mem_g': _jnp.float32, 'xa_q_w': _jnp.float32, 'xa_kv_w': _jnp.float32, 'xa_o_w': _jnp.float32, 'ffn_in_w': _jnp.float32, 'ffn_conv_w': _jnp.float32, 'ffn_conv_b': _jnp.float32, 'ffn_out_w': _jnp.float32}
MOMENT_SCALE = {'norm_g': 2.428932e+01, 'ssm_in_w': 2.025154e+00, 'ssm_conv_w': 3.020638e+00, 'ssm_conv_b': 8.824344e+00, 'ssm_dt_bias': 5.234118e+00, 'ssm_A_log': 2.870528e+01, 'ssm_D': 1.446590e+01, 'ssm_norm_g': 4.696339e+00, 'ssm_out_w': 7.067668e+00, 'cf_pw1_w': 5.139341e+00, 'cf_pw1_b': 1.826899e+01, 'cf_dw_w': 7.822332e+00, 'cf_dw_b': 4.394429e+01, 'cf_ln_g': 1.940732e+01, 'cf_ln_b': 2.590861e+01, 'cf_pw2_w': 1.314061e+01, 'cf_pw2_b': 5.227719e+01, 'xa_mem_g': 1.764334e+01, 'xa_q_w': 4.866671e+00, 'xa_kv_w': 1.200811e+01, 'xa_o_w': 1.644448e+01, 'ffn_in_w': 2.369047e+00, 'ffn_conv_w': 2.807876e+00, 'ffn_conv_b': 7.887134e+00, 'ffn_out_w': 4.792176e+00}


def _to_microbatches(a, axis):
    t = _jnp.moveaxis(a, axis, 0)
    t = t.reshape((N_MICROBATCH, t.shape[0] // N_MICROBATCH) + t.shape[1:])
    return _jnp.moveaxis(t, 1, axis + 1)


def setup_inputs(seed: int = 0) -> dict:
    inp = _fwd_setup_inputs(seed)
    key = _jax.random.fold_in(_jax.random.key(seed), 7919)
    shape, _ = _output_shape()
    out = dict(inp)
    out["loss_target"] = _jax.random.normal(_jax.random.fold_in(key, 0), shape, _jnp.float32)
    for i, name in enumerate(TWIN_WEIGHTS):
        w = inp[name].astype(_jnp.float32)
        if MOMENT_SCALE is None:
            s = _jnp.sqrt(_jnp.mean(_jnp.square(w)) + 1e-30)
        else:
            s = MOMENT_SCALE[name]
        km, kv = _jax.random.split(_jax.random.fold_in(key, i + 1))
        out[name] = w
        out["m_" + name] = s * _jax.random.normal(km, w.shape, _jnp.float32)
        out["v_" + name] = (s * s) * _jax.random.uniform(kv, w.shape, _jnp.float32, 0.5, 1.5)
    if N_MICROBATCH > 1:
        for name, axis in PER_EXAMPLE_BATCH_AXIS.items():
            out[name] = _to_microbatches(out[name], axis)
    return {'x': out['x'], 'mem': out['mem'], 'norm_g': out['norm_g'], 'ssm_in_w': out['ssm_in_w'], 'ssm_conv_w': out['ssm_conv_w'], 'ssm_conv_b': out['ssm_conv_b'], 'ssm_dt_bias': out['ssm_dt_bias'], 'ssm_A_log': out['ssm_A_log'], 'ssm_D': out['ssm_D'], 'ssm_norm_g': out['ssm_norm_g'], 'ssm_out_w': out['ssm_out_w'], 'cf_pw1_w': out['cf_pw1_w'], 'cf_pw1_b': out['cf_pw1_b'], 'cf_dw_w': out['cf_dw_w'], 'cf_dw_b': out['cf_dw_b'], 'cf_ln_g': out['cf_ln_g'], 'cf_ln_b': out['cf_ln_b'], 'cf_pw2_w': out['cf_pw2_w'], 'cf_pw2_b': out['cf_pw2_b'], 'xa_mem_g': out['xa_mem_g'], 'xa_q_w': out['xa_q_w'], 'xa_kv_w': out['xa_kv_w'], 'xa_o_w': out['xa_o_w'], 'ffn_in_w': out['ffn_in_w'], 'ffn_conv_w': out['ffn_conv_w'], 'ffn_conv_b': out['ffn_conv_b'], 'ffn_out_w': out['ffn_out_w'], 'loss_target': out['loss_target'], 'm_norm_g': out['m_norm_g'], 'm_ssm_in_w': out['m_ssm_in_w'], 'm_ssm_conv_w': out['m_ssm_conv_w'], 'm_ssm_conv_b': out['m_ssm_conv_b'], 'm_ssm_dt_bias': out['m_ssm_dt_bias'], 'm_ssm_A_log': out['m_ssm_A_log'], 'm_ssm_D': out['m_ssm_D'], 'm_ssm_norm_g': out['m_ssm_norm_g'], 'm_ssm_out_w': out['m_ssm_out_w'], 'm_cf_pw1_w': out['m_cf_pw1_w'], 'm_cf_pw1_b': out['m_cf_pw1_b'], 'm_cf_dw_w': out['m_cf_dw_w'], 'm_cf_dw_b': out['m_cf_dw_b'], 'm_cf_ln_g': out['m_cf_ln_g'], 'm_cf_ln_b': out['m_cf_ln_b'], 'm_cf_pw2_w': out['m_cf_pw2_w'], 'm_cf_pw2_b': out['m_cf_pw2_b'], 'm_xa_mem_g': out['m_xa_mem_g'], 'm_xa_q_w': out['m_xa_q_w'], 'm_xa_kv_w': out['m_xa_kv_w'], 'm_xa_o_w': out['m_xa_o_w'], 'm_ffn_in_w': out['m_ffn_in_w'], 'm_ffn_conv_w': out['m_ffn_conv_w'], 'm_ffn_conv_b': out['m_ffn_conv_b'], 'm_ffn_out_w': out['m_ffn_out_w'], 'v_norm_g': out['v_norm_g'], 'v_ssm_in_w': out['v_ssm_in_w'], 'v_ssm_conv_w': out['v_ssm_conv_w'], 'v_ssm_conv_b': out['v_ssm_conv_b'], 'v_ssm_dt_bias': out['v_ssm_dt_bias'], 'v_ssm_A_log': out['v_ssm_A_log'], 'v_ssm_D': out['v_ssm_D'], 'v_ssm_norm_g': out['v_ssm_norm_g'], 'v_ssm_out_w': out['v_ssm_out_w'], 'v_cf_pw1_w': out['v_cf_pw1_w'], 'v_cf_pw1_b': out['v_cf_pw1_b'], 'v_cf_dw_w': out['v_cf_dw_w'], 'v_cf_dw_b': out['v_cf_dw_b'], 'v_cf_ln_g': out['v_cf_ln_g'], 'v_cf_ln_b': out['v_cf_ln_b'], 'v_cf_pw2_w': out['v_cf_pw2_w'], 'v_cf_pw2_b': out['v_cf_pw2_b'], 'v_xa_mem_g': out['v_xa_mem_g'], 'v_xa_q_w': out['v_xa_q_w'], 'v_xa_kv_w': out['v_xa_kv_w'], 'v_xa_o_w': out['v_xa_o_w'], 'v_ffn_in_w': out['v_ffn_in_w'], 'v_ffn_conv_w': out['v_ffn_conv_w'], 'v_ffn_conv_b': out['v_ffn_conv_b'], 'v_ffn_out_w': out['v_ffn_out_w']}


def _loss(weights, diff, rest, loss_target):
    with _jax.named_scope("forward"):
        args = {**rest, TWIN_DIFF_INPUT: diff, **{k: w.astype(_WEIGHT_DTYPES[k]) for k, w in weights.items()}}
        y = _forward(args)
    with _jax.named_scope("loss_head"):
        err = _jnp.square(y.astype(_jnp.float32) - loss_target)
        return 0.5 * _jnp.sum(_jnp.mean(err, axis=-1)) if err.ndim else 0.5 * err


def _adamw(w, g, m, v):
    m = ADAM_B1 * m + (1.0 - ADAM_B1) * g
    v = ADAM_B2 * v + (1.0 - ADAM_B2) * _jnp.square(g)
    m_hat = m / (1.0 - ADAM_B1 ** ADAM_STEP)
    v_hat = v / (1.0 - ADAM_B2 ** ADAM_STEP)
    delta = -ADAM_LR * (m_hat / (_jnp.sqrt(v_hat) + ADAM_EPS) + ADAM_WD * w)
    return delta, m, v


def reference(x, mem, norm_g, ssm_in_w, ssm_conv_w, ssm_conv_b, ssm_dt_bias, ssm_A_log, ssm_D, ssm_norm_g, ssm_out_w, cf_pw1_w, cf_pw1_b, cf_dw_w, cf_dw_b, cf_ln_g, cf_ln_b, cf_pw2_w, cf_pw2_b, xa_mem_g, xa_q_w, xa_kv_w, xa_o_w, ffn_in_w, ffn_conv_w, ffn_conv_b, ffn_out_w, loss_target, m_norm_g, m_ssm_in_w, m_ssm_conv_w, m_ssm_conv_b, m_ssm_dt_bias, m_ssm_A_log, m_ssm_D, m_ssm_norm_g, m_ssm_out_w, m_cf_pw1_w, m_cf_pw1_b, m_cf_dw_w, m_cf_dw_b, m_cf_ln_g, m_cf_ln_b, m_cf_pw2_w, m_cf_pw2_b, m_xa_mem_g, m_xa_q_w, m_xa_kv_w, m_xa_o_w, m_ffn_in_w, m_ffn_conv_w, m_ffn_conv_b, m_ffn_out_w, v_norm_g, v_ssm_in_w, v_ssm_conv_w, v_ssm_conv_b, v_ssm_dt_bias, v_ssm_A_log, v_ssm_D, v_ssm_norm_g, v_ssm_out_w, v_cf_pw1_w, v_cf_pw1_b, v_cf_dw_w, v_cf_dw_b, v_cf_ln_g, v_cf_ln_b, v_cf_pw2_w, v_cf_pw2_b, v_xa_mem_g, v_xa_q_w, v_xa_kv_w, v_xa_o_w, v_ffn_in_w, v_ffn_conv_w, v_ffn_conv_b, v_ffn_out_w):
    given = dict(x=x, mem=mem, norm_g=norm_g, ssm_in_w=ssm_in_w, ssm_conv_w=ssm_conv_w, ssm_conv_b=ssm_conv_b, ssm_dt_bias=ssm_dt_bias, ssm_A_log=ssm_A_log, ssm_D=ssm_D, ssm_norm_g=ssm_norm_g, ssm_out_w=ssm_out_w, cf_pw1_w=cf_pw1_w, cf_pw1_b=cf_pw1_b, cf_dw_w=cf_dw_w, cf_dw_b=cf_dw_b, cf_ln_g=cf_ln_g, cf_ln_b=cf_ln_b, cf_pw2_w=cf_pw2_w, cf_pw2_b=cf_pw2_b, xa_mem_g=xa_mem_g, xa_q_w=xa_q_w, xa_kv_w=xa_kv_w, xa_o_w=xa_o_w, ffn_in_w=ffn_in_w, ffn_conv_w=ffn_conv_w, ffn_conv_b=ffn_conv_b, ffn_out_w=ffn_out_w, loss_target=loss_target, m_norm_g=m_norm_g, m_ssm_in_w=m_ssm_in_w, m_ssm_conv_w=m_ssm_conv_w, m_ssm_conv_b=m_ssm_conv_b, m_ssm_dt_bias=m_ssm_dt_bias, m_ssm_A_log=m_ssm_A_log, m_ssm_D=m_ssm_D, m_ssm_norm_g=m_ssm_norm_g, m_ssm_out_w=m_ssm_out_w, m_cf_pw1_w=m_cf_pw1_w, m_cf_pw1_b=m_cf_pw1_b, m_cf_dw_w=m_cf_dw_w, m_cf_dw_b=m_cf_dw_b, m_cf_ln_g=m_cf_ln_g, m_cf_ln_b=m_cf_ln_b, m_cf_pw2_w=m_cf_pw2_w, m_cf_pw2_b=m_cf_pw2_b, m_xa_mem_g=m_xa_mem_g, m_xa_q_w=m_xa_q_w, m_xa_kv_w=m_xa_kv_w, m_xa_o_w=m_xa_o_w, m_ffn_in_w=m_ffn_in_w, m_ffn_conv_w=m_ffn_conv_w, m_ffn_conv_b=m_ffn_conv_b, m_ffn_out_w=m_ffn_out_w, v_norm_g=v_norm_g, v_ssm_in_w=v_ssm_in_w, v_ssm_conv_w=v_ssm_conv_w, v_ssm_conv_b=v_ssm_conv_b, v_ssm_dt_bias=v_ssm_dt_bias, v_ssm_A_log=v_ssm_A_log, v_ssm_D=v_ssm_D, v_ssm_norm_g=v_ssm_norm_g, v_ssm_out_w=v_ssm_out_w, v_cf_pw1_w=v_cf_pw1_w, v_cf_pw1_b=v_cf_pw1_b, v_cf_dw_w=v_cf_dw_w, v_cf_dw_b=v_cf_dw_b, v_cf_ln_g=v_cf_ln_g, v_cf_ln_b=v_cf_ln_b, v_cf_pw2_w=v_cf_pw2_w, v_cf_pw2_b=v_cf_pw2_b, v_xa_mem_g=v_xa_mem_g, v_xa_q_w=v_xa_q_w, v_xa_kv_w=v_xa_kv_w, v_xa_o_w=v_xa_o_w, v_ffn_in_w=v_ffn_in_w, v_ffn_conv_w=v_ffn_conv_w, v_ffn_conv_b=v_ffn_conv_b, v_ffn_out_w=v_ffn_out_w)
    weights = {n: given[n] for n in TWIN_WEIGHTS}
    shared = {n: given[n] for n in SHARED_INPUTS}
    per_example = {n: given[n] for n in ['x', 'mem']}
    grad_fn = _jax.value_and_grad(_loss, argnums=(0, 1))

    def one_microbatch(ex, loss_target):
        ex = dict(ex)
        diff = ex.pop(TWIN_DIFF_INPUT)
        return grad_fn(weights, diff, {**shared, **ex}, loss_target)

    if N_MICROBATCH == 1:
        loss, (grad_w, grad_x) = one_microbatch(per_example, given["loss_target"])
    else:
        def body(carry, xs):
            loss_sum, grad_sum = carry
            l_k, (gw_k, gx_k) = one_microbatch(xs[0], xs[1])
            with _jax.named_scope("update"):
                return (loss_sum + l_k, _jax.tree.map(_jnp.add, grad_sum, gw_k)), gx_k

        init = (_jnp.zeros((), _jnp.float32), _jax.tree.map(_jnp.zeros_like, weights))
        (loss, grad_w), grad_x = _jax.lax.scan(body, init, (per_example, given["loss_target"]))
    with _jax.named_scope("update"):
        delta_w, new_m, new_v = {}, {}, {}
        for n in TWIN_WEIGHTS:
            delta_w[n], new_m[n], new_v[n] = _adamw(weights[n], grad_w[n], given["m_" + n], given["v_" + n])
    return (loss, grad_x, *[grad_w[n] for n in TWIN_WEIGHTS], *[delta_w[n] for n in TWIN_WEIGHTS],
            *[new_m[n] for n in TWIN_WEIGHTS], *[new_v[n] for n in TWIN_WEIGHTS])
```

```python
import functools
import math

import jax
import jax.numpy as jnp
from jax import lax
from jax.experimental import pallas as pl
from jax.experimental.pallas import tpu as pltpu

f32 = jnp.float32
bf16 = jnp.bfloat16
S = jax.ShapeDtypeStruct

D_MODEL = 1024
DEPTH = 4
D_INNER = 2048
HEAD_DIM = 64
N_GROUPS = 4
HEADS_PER_GROUP = 8
N_SSM_HEADS = 32
D_STATE = 128
CHUNK = 128
SSM_CONV = 4
CONV_DIM = 3072
CF_KERNEL = 31
N_MEM = 256
XA_HEADS = 4
XA_HEAD_DIM = 256
D_FF = 2816
FFN_CONV = 3
EPS = 1e-6
ADAM_LR, ADAM_B1, ADAM_B2, ADAM_EPS, ADAM_WD, ADAM_STEP = 0.001, 0.9, 0.999, 1e-08, 0.01, 10

LANE = 128
SUBLANE = 8
VMEM_LIMIT = 56 * 1024 * 1024
N_CHIPS = 4
PACK_COLS = 1024

WEIGHT_NAMES = ['norm_g', 'ssm_in_w', 'ssm_conv_w', 'ssm_conv_b', 'ssm_dt_bias', 'ssm_A_log', 'ssm_D', 'ssm_norm_g',
                'ssm_out_w', 'cf_pw1_w', 'cf_pw1_b', 'cf_dw_w', 'cf_dw_b', 'cf_ln_g', 'cf_ln_b', 'cf_pw2_w', 'cf_pw2_b',
                'xa_mem_g', 'xa_q_w', 'xa_kv_w', 'xa_o_w', 'ffn_in_w', 'ffn_conv_w', 'ffn_conv_b', 'ffn_out_w']
SHARD_AXIS = {'norm_g': 2, 'ssm_in_w': 2, 'ssm_conv_w': 2, 'ssm_conv_b': None, 'ssm_dt_bias': None, 'ssm_A_log': None,
              'ssm_D': None, 'ssm_norm_g': None, 'ssm_out_w': 1, 'cf_pw1_w': 2, 'cf_pw1_b': 1, 'cf_dw_w': 2, 'cf_dw_b': 1,
              'cf_ln_g': 1, 'cf_ln_b': 1, 'cf_pw2_w': 1, 'cf_pw2_b': 1, 'xa_mem_g': None, 'xa_q_w': 1, 'xa_kv_w': 2,
              'xa_o_w': 1, 'ffn_in_w': 2, 'ffn_conv_w': 2, 'ffn_conv_b': None, 'ffn_out_w': 1}
MATMUL_WEIGHTS = ('ssm_in_w', 'ssm_out_w', 'cf_pw1_w', 'cf_pw2_w', 'xa_q_w', 'xa_kv_w', 'xa_o_w', 'ffn_in_w', 'ffn_out_w')


def _cp(*sem):
    return pltpu.CompilerParams(dimension_semantics=tuple(sem), vmem_limit_bytes=VMEM_LIMIT)


def _pick(dim, pref):
    if dim <= pref:
        return dim
    best = None
    for t in range(LANE, pref + 1, LANE):
        if dim % t == 0:
            best = t
    assert best is not None, (dim, pref)
    return best


def _sigmoid(x):
    return 1.0 / (1.0 + jnp.exp(-x))


def _silu(x):
    return x * _sigmoid(x)


def _dsilu(x):
    s = _sigmoid(x)
    return s * (1.0 + x * (1.0 - s))


def _softplus(x):
    return jnp.maximum(x, 0.0) + jnp.log(1.0 + jnp.exp(-jnp.abs(x)))


_DN = {"nn": (((1,), (0,)), ((), ())), "nt": (((1,), (1,)), ((), ())), "tn": (((0,), (0,)), ((), ()))}


def _mm(a, b, mode, *, name, out_dtype=f32, bias=None, add=None):
    if mode == "nn":
        (M, K), (K2, N) = a.shape, b.shape
    elif mode == "nt":
        (M, K), (N, K2) = a.shape, b.shape
    else:
        (K, M), (K2, N) = a.shape, b.shape
    assert K == K2, (a.shape, b.shape, mode)
    tm, tn, tk = _pick(M, 1024), _pick(N, 1408), _pick(K, 1408)
    nk = K // tk
    a_spec = {"nn": pl.BlockSpec((tm, tk), lambda i, j, k: (i, k)), "nt": pl.BlockSpec((tm, tk), lambda i, j, k: (i, k)),
              "tn": pl.BlockSpec((tk, tm), lambda i, j, k: (k, i))}[mode]
    b_spec = {"nn": pl.BlockSpec((tk, tn), lambda i, j, k: (k, j)), "nt": pl.BlockSpec((tn, tk), lambda i, j, k: (j, k)),
              "tn": pl.BlockSpec((tk, tn), lambda i, j, k: (k, j))}[mode]
    in_specs, args = [a_spec, b_spec], [a, b]
    if bias is not None:
        in_specs.append(pl.BlockSpec((1, tn), lambda i, j, k: (0, j)))
        args.append(bias)
    if add is not None:
        in_specs.append(pl.BlockSpec((tm, tn), lambda i, j, k: (i, j)))
        args.append(add)
    dn = _DN[mode]
    has_bias, has_add = bias is not None, add is not None

    def body(a_ref, b_ref, *rest):
        rest = list(rest)
        bias_ref = rest.pop(0) if has_bias else None
        add_ref = rest.pop(0) if has_add else None
        o_ref, acc_ref = rest
        k = pl.program_id(2)

        @pl.when(k == 0)
        def _():
            acc_ref[...] = jnp.zeros_like(acc_ref)

        acc_ref[...] += lax.dot_general(a_ref[...].astype(bf16), b_ref[...].astype(bf16), dn, preferred_element_type=f32)

        @pl.when(k == nk - 1)
        def _():
            r = acc_ref[...]
            if has_bias:
                r = r + bias_ref[...]
            if has_add:
                r = r + add_ref[...].astype(f32)
            o_ref[...] = r.astype(out_dtype)

    return pl.pallas_call(
        body, name=name, out_shape=S((M, N), out_dtype), grid=(M // tm, N // tn, nk),
        in_specs=in_specs, out_specs=pl.BlockSpec((tm, tn), lambda i, j, k: (i, j)),
        scratch_shapes=[pltpu.VMEM((tm, tn), f32)],
        compiler_params=_cp("parallel", "parallel", "arbitrary"))(*args)


def _rows(tm, C):
    return pl.BlockSpec((tm, C), lambda i: (i, 0))


def _const(shape):
    return pl.BlockSpec(shape, lambda i: tuple(0 for _ in shape))


def _rms_val(x, g):
    r = lax.rsqrt(jnp.mean(x * x, axis=-1, keepdims=True) + EPS)
    return x * r * g


def _rms_bwd_val(x, g, dy):
    r = lax.rsqrt(jnp.mean(x * x, axis=-1, keepdims=True) + EPS)
    xn = x * r
    dxh = dy * g
    dx = r * (dxh - xn * jnp.mean(dxh * xn, axis=-1, keepdims=True))
    return dx, jnp.sum(dy * xn, axis=0, keepdims=True)


def _rmsnorm_fwd(x, g, *, name):
    L, C = x.shape
    tm = _pick(L, 512)

    def body(x_ref, g_ref, o_ref):
        o_ref[...] = _rms_val(x_ref[...], g_ref[...]).astype(bf16)

    return pl.pallas_call(body, name=name, out_shape=S((L, C), bf16), grid=(L // tm,),
                          in_specs=[_rows(tm, C), _const((1, C))], out_specs=_rows(tm, C),
                          compiler_params=_cp("parallel"))(x, g)


def _resid_norm_fwd(x, mix, g_post, g_next, *, name):
    L, C = x.shape
    tm = _pick(L, 512)
    want_h = g_next is not None

    def body(x_ref, m_ref, gp_ref, *rest):
        xn = x_ref[...] + _rms_val(m_ref[...], gp_ref[...])
        if want_h:
            gn_ref, xo_ref, h_ref = rest
            h_ref[...] = _rms_val(xn, gn_ref[...]).astype(bf16)
        else:
            (xo_ref,) = rest
        xo_ref[...] = xn

    in_specs = [_rows(tm, C), _rows(tm, C), _const((1, C))]
    args = [x, mix, g_post]
    out_shape, out_specs = [S((L, C), f32)], [_rows(tm, C)]
    if want_h:
        in_specs.append(_const((1, C)))
        args.append(g_next)
        out_shape.append(S((L, C), bf16))
        out_specs.append(_rows(tm, C))
    out = pl.pallas_call(body, name=name, out_shape=tuple(out_shape), grid=(L // tm,), in_specs=in_specs,
                         out_specs=tuple(out_specs), compiler_params=_cp("parallel"))(*args)
    return (out[0], out[1]) if want_h else (out[0], None)


def _norm_bwd(x, g, dy, *, name, add=None, out_dtype=f32):
    L, C = x.shape
    tm = _pick(L, 512)
    has_add = add is not None

    def body(x_ref, g_ref, dy_ref, *rest):
        rest = list(rest)
        add_ref = rest.pop(0) if has_add else None
        dx_ref, dg_ref, cs_ref = rest
        i = pl.program_id(0)

        @pl.when(i == 0)
        def _():
            dg_ref[...] = jnp.zeros_like(dg_ref)
            cs_ref[...] = jnp.zeros_like(cs_ref)

        dx, dg = _rms_bwd_val(x_ref[...], g_ref[...], dy_ref[...].astype(f32))
        dg_ref[...] += dg
        cs_ref[...] += jnp.sum(dx, axis=0, keepdims=True)
        if has_add:
            dx = dx + add_ref[...]
        dx_ref[...] = dx.astype(out_dtype)

    in_specs = [_rows(tm, C), _const((1, C)), _rows(tm, C)]
    args = [x, g, dy]
    if has_add:
        in_specs.append(_rows(tm, C))
        args.append(add)
    return pl.pallas_call(body, name=name, out_shape=(S((L, C), out_dtype), S((1, C), f32), S((1, C), f32)),
                          grid=(L // tm,), in_specs=in_specs,
                          out_specs=(_rows(tm, C), _const((1, C)), _const((1, C))),
                          compiler_params=_cp("arbitrary"))(*args)


def _loss_fwd_bwd(y, target, *, name):
    L, C = y.shape
    tm = _pick(L, 512)

    def body(y_ref, t_ref, acc_ref, dy_ref):
        i = pl.program_id(0)

        @pl.when(i == 0)
        def _():
            acc_ref[...] = jnp.zeros_like(acc_ref)

        e = y_ref[...] - t_ref[...]
        rs = jnp.sum(e * e, axis=-1, keepdims=True)
        acc_ref[...] += jnp.broadcast_to(jnp.sum(rs, axis=0, keepdims=True), (1, LANE))
        dy_ref[...] = e * (1.0 / C)

    return pl.pallas_call(body, name=name, out_shape=(S((1, LANE), f32), S((L, C), f32)), grid=(L // tm,),
                          in_specs=[_rows(tm, C), _rows(tm, C)], out_specs=(_const((1, LANE)), _rows(tm, C)),
                          compiler_params=_cp("arbitrary"))(y, target)


def _halo_rows(K):
    return SUBLANE if K - 1 <= SUBLANE else 32


def _prev_halo_spec(tm, H, C):
    return pl.BlockSpec((H, C), lambda i: (jnp.maximum(i * (tm // H) - 1, 0), 0))


def _next_halo_spec(tm, H, C, L):
    return pl.BlockSpec((H, C), lambda i: (jnp.minimum((i + 1) * (tm // H), L // H - 1), 0))


def _shift_down(ext, s):
    return ext if s == 0 else pltpu.roll(ext, s, axis=0)


def _shift_up(ext, s):
    return ext if s == 0 else pltpu.roll(ext, ext.shape[0] - s, axis=0)


def _causal_conv(ext, H, w_ref, K):
    acc = None
    for k in range(K):
        term = _shift_down(ext, K - 1 - k)[H:, :] * w_ref[k:k + 1, :]
        acc = term if acc is None else acc + term
    return acc


def _anticausal_conv(ext, tm, w_ref, K):
    acc = None
    for k in range(K):
        term = _shift_up(ext, K - 1 - k)[:tm, :] * w_ref[k:k + 1, :]
        acc = term if acc is None else acc + term
    return acc


def _tap_grads(dw_ref, d_cur, x_ext, H, K):
    for k in range(K):
        dw_ref[k:k + 1, :] += jnp.sum(d_cur * _shift_down(x_ext, K - 1 - k)[H:, :], axis=0, keepdims=True)


def _pad_taps(w, K):
    return jnp.pad(w, ((0, _halo_rows(K) - K), (0, 0)))


def _conv_act_fwd(x, w, b, *, K, act, name, out_dtype, tm_pref=256):
    L, C = x.shape
    H = _halo_rows(K)
    tm = _pick(L, tm_pref)
    Co = C if act == "silu" else C // 2

    def body(h_ref, x_ref, w_ref, b_ref, o_ref):
        i = pl.program_id(0)
        halo = jnp.where(i > 0, h_ref[...], 0.0)
        ext = jnp.concatenate([halo, x_ref[...]], axis=0)
        u = _causal_conv(ext, H, w_ref, K) + b_ref[...]
        if act == "silu":
            o_ref[...] = _silu(u).astype(out_dtype)
        else:
            o_ref[...] = (_silu(u[:, :Co]) * u[:, Co:]).astype(out_dtype)

    return pl.pallas_call(body, name=name, out_shape=S((L, Co), out_dtype), grid=(L // tm,),
                          in_specs=[_prev_halo_spec(tm, H, C), _rows(tm, C), _const((H, C)), _const((1, C))],
                          out_specs=_rows(tm, Co), compiler_params=_cp("parallel"))(x, x, w, b)


def _conv_act_bwd(x, dparts, w, b, *, K, act, name, tm_pref=256):
    L, C = x.shape
    H = _halo_rows(K)
    tm = _pick(L, tm_pref)
    Co = C if act == "silu" else C // 2
    nparts = len(dparts)

    def body(h_ref, x_ref, w_ref, b_ref, *rest):
        d_refs, (du_ref, dw_ref, db_ref) = rest[:nparts], rest[nparts:]
        i = pl.program_id(0)

        @pl.when(i == 0)
        def _():
            dw_ref[...] = jnp.zeros_like(dw_ref)
            db_ref[...] = jnp.zeros_like(db_ref)

        halo = jnp.where(i > 0, h_ref[...], 0.0)
        ext = jnp.concatenate([halo, x_ref[...]], axis=0)
        u = _causal_conv(ext, H, w_ref, K) + b_ref[...]
        d = [r[...].astype(f32) for r in d_refs]
        d = d[0] if nparts == 1 else jnp.concatenate(d, axis=1)
        if act == "silu":
            du = d * _dsilu(u)
        else:
            g, v = u[:, :Co], u[:, Co:]
            du = jnp.concatenate([d * v * _dsilu(g), d * _silu(g)], axis=1)
        du_ref[...] = du
        db_ref[...] += jnp.sum(du, axis=0, keepdims=True)
        _tap_grads(dw_ref, du, ext, H, K)

    in_specs = [_prev_halo_spec(tm, H, C), _rows(tm, C), _const((H, C)), _const((1, C))]
    in_specs += [_rows(tm, p.shape[1]) for p in dparts]
    return pl.pallas_call(body, name=name, out_shape=(S((L, C), f32), S((H, C), f32), S((1, C), f32)), grid=(L // tm,),
                          in_specs=in_specs, out_specs=(_rows(tm, C), _const((H, C)), _const((1, C))),
                          compiler_params=_cp("arbitrary"))(x, x, w, b, *dparts)


def _conv_transpose(d, w, *, K, name, tm_pref=256):
    L, C = d.shape
    H = _halo_rows(K)
    tm = _pick(L, tm_pref)
    nb = L // tm

    def body(d_ref, h_ref, w_ref, o_ref):
        i = pl.program_id(0)
        halo = jnp.where(i < nb - 1, h_ref[...], 0.0)
        ext = jnp.concatenate([d_ref[...], halo], axis=0)
        o_ref[...] = _anticausal_conv(ext, tm, w_ref, K).astype(bf16)

    return pl.pallas_call(body, name=name, out_shape=S((L, C), bf16), grid=(nb,),
                          in_specs=[_rows(tm, C), _next_halo_spec(tm, H, C, L), _const((H, C))],
                          out_specs=_rows(tm, C), compiler_params=_cp("parallel"))(d, d, w)


def _cf_fwd(u, dw_w, dw_b, ln_g, ln_b, *, name):
    L, C2 = u.shape
    C = C2 // 2
    K, H = CF_KERNEL, _halo_rows(CF_KERNEL)
    tm = _pick(L, 256)

    def body(h_ref, u_ref, w_ref, b_ref, g_ref, lb_ref, c_ref, s_ref):
        i = pl.program_id(0)
        halo = jnp.where(i > 0, h_ref[...], 0.0)
        ext = jnp.concatenate([halo, u_ref[...]], axis=0)
        glu = ext[:, :C] * _sigmoid(ext[:, C:])
        c = _causal_conv(glu, H, w_ref, K) + b_ref[...]
        c_ref[...] = c
        mu = jnp.mean(c, axis=-1, keepdims=True)
        xc = c - mu
        var = jnp.mean(xc * xc, axis=-1, keepdims=True)
        ln = xc * lax.rsqrt(var + EPS) * g_ref[...] + lb_ref[...]
        s_ref[...] = _silu(ln).astype(bf16)

    return pl.pallas_call(body, name=name, out_shape=(S((L, C), f32), S((L, C), bf16)), grid=(L // tm,),
                          in_specs=[_prev_halo_spec(tm, H, C2), _rows(tm, C2), _const((H, C)), _const((1, C)),
                                    _const((1, C)), _const((1, C))],
                          out_specs=(_rows(tm, C), _rows(tm, C)), compiler_params=_cp("parallel"))(u, u, dw_w, dw_b, ln_g, ln_b)


def _cf_ln_bwd(c, ln_g, ln_b, ds, *, name):
    L, C = c.shape
    tm = _pick(L, 512)

    def body(c_ref, g_ref, lb_ref, ds_ref, dc_ref, dg_ref, db_ref):
        i = pl.program_id(0)

        @pl.when(i == 0)
        def _():
            dg_ref[...] = jnp.zeros_like(dg_ref)
            db_ref[...] = jnp.zeros_like(db_ref)

        c = c_ref[...]
        mu = jnp.mean(c, axis=-1, keepdims=True)
        xc = c - mu
        r = lax.rsqrt(jnp.mean(xc * xc, axis=-1, keepdims=True) + EPS)
        xh = xc * r
        ln = xh * g_ref[...] + lb_ref[...]
        dln = ds_ref[...].astype(f32) * _dsilu(ln)
        dg_ref[...] += jnp.sum(dln * xh, axis=0, keepdims=True)
        db_ref[...] += jnp.sum(dln, axis=0, keepdims=True)
        dxh = dln * g_ref[...]
        dc_ref[...] = r * (dxh - jnp.mean(dxh, axis=-1, keepdims=True) - xh * jnp.mean(dxh * xh, axis=-1, keepdims=True))

    return pl.pallas_call(body, name=name, out_shape=(S((L, C), f32), S((1, C), f32), S((1, C), f32)), grid=(L // tm,),
                          in_specs=[_rows(tm, C), _const((1, C)), _const((1, C)), _rows(tm, C)],
                          out_specs=(_rows(tm, C), _const((1, C)), _const((1, C))),
                          compiler_params=_cp("arbitrary"))(c, ln_g, ln_b, ds)


def _cf_glu_bwd(u, dc, dw_w, *, name):
    L, C2 = u.shape
    C = C2 // 2
    K, H = CF_KERNEL, _halo_rows(CF_KERNEL)
    tm = _pick(L, 256)
    nb = L // tm

    def body(uh_ref, u_ref, dc_ref, dch_ref, w_ref, du_ref, dw_ref, db_ref, dus_ref):
        i = pl.program_id(0)

        @pl.when(i == 0)
        def _():
            dw_ref[...] = jnp.zeros_like(dw_ref)
            db_ref[...] = jnp.zeros_like(db_ref)
            dus_ref[...] = jnp.zeros_like(dus_ref)

        halo = jnp.where(i > 0, uh_ref[...], 0.0)
        ext = jnp.concatenate([halo, u_ref[...]], axis=0)
        sg = _sigmoid(ext[:, C:])
        glu = ext[:, :C] * sg
        dc = dc_ref[...]
        dnext = jnp.where(i < nb - 1, dch_ref[...], 0.0)
        dglu = _anticausal_conv(jnp.concatenate([dc, dnext], axis=0), tm, w_ref, K)
        a_cur, sg_cur = ext[H:, :C], sg[H:, :]
        du = jnp.concatenate([dglu * sg_cur, dglu * a_cur * sg_cur * (1.0 - sg_cur)], axis=1)
        du_ref[...] = du.astype(bf16)
        dus_ref[...] += jnp.sum(du, axis=0, keepdims=True)
        db_ref[...] += jnp.sum(dc, axis=0, keepdims=True)
        _tap_grads(dw_ref, dc, glu, H, K)

    return pl.pallas_call(body, name=name,
                          out_shape=(S((L, C2), bf16), S((H, C), f32), S((1, C), f32), S((1, C2), f32)), grid=(nb,),
                          in_specs=[_prev_halo_spec(tm, H, C2), _rows(tm, C2), _rows(tm, C), _next_halo_spec(tm, H, C, L),
                                    _const((H, C))],
                          out_specs=(_rows(tm, C2), _const((H, C)), _const((1, C)), _const((1, C2))),
                          compiler_params=_cp("arbitrary"))(u, u, dc, dc, dw_w)


def _gated_norm_fwd(y, z, g, *, name):
    L, C = y.shape
    tm = _pick(L, 256)

    def body(y_ref, z_ref, g_ref, o_ref):
        o_ref[...] = _rms_val(y_ref[...] * _silu(z_ref[...]), g_ref[...]).astype(bf16)

    return pl.pallas_call(body, name=name, out_shape=S((L, C), bf16), grid=(L // tm,),
                          in_specs=[_rows(tm, C), _rows(tm, C), _const((1, C))], out_specs=_rows(tm, C),
                          compiler_params=_cp("parallel"))(y, z, g)


def _gated_norm_bwd(y, z, g, dyn, *, name):
    L, C = y.shape
    tm = _pick(L, 256)

    def body(y_ref, z_ref, g_ref, d_ref, dy_ref, dz_ref, dg_ref):
        i = pl.program_id(0)

        @pl.when(i == 0)
        def _():
            dg_ref[...] = jnp.zeros_like(dg_ref)

        y, z = y_ref[...], z_ref[...]
        sz = _silu(z)
        du, dg = _rms_bwd_val(y * sz, g_ref[...], d_ref[...].astype(f32))
        dg_ref[...] += dg
        dy_ref[...] = du * sz
        dz_ref[...] = (du * y * _dsilu(z)).astype(bf16)

    return pl.pallas_call(body, name=name, out_shape=(S((L, C), f32), S((L, C), bf16), S((1, C), f32)), grid=(L // tm,),
                          in_specs=[_rows(tm, C), _rows(tm, C), _const((1, C)), _rows(tm, C)],
                          out_specs=(_rows(tm, C), _rows(tm, C), _const((1, C))),
                          compiler_params=_cp("arbitrary"))(y, z, g, dyn)


_XA_SCALE = XA_HEAD_DIM ** -0.5


def _attn_fwd(q, kv, *, name):
    L, C = q.shape
    tm = _pick(L, 512)
    Dh = XA_HEAD_DIM

    def body(q_ref, kv_ref, o_ref):
        for h in range(XA_HEADS):
            qh = q_ref[:, h * Dh:(h + 1) * Dh]
            kh = kv_ref[:, h * Dh:(h + 1) * Dh]
            vh = kv_ref[:, C + h * Dh:C + (h + 1) * Dh]
            s = lax.dot_general(qh, kh, _DN["nt"], preferred_element_type=f32) * _XA_SCALE
            e = jnp.exp(s - jnp.max(s, axis=-1, keepdims=True))
            p = e / jnp.sum(e, axis=-1, keepdims=True)
            o_ref[:, h * Dh:(h + 1) * Dh] = jnp.dot(p.astype(bf16), vh, preferred_element_type=f32).astype(bf16)

    return pl.pallas_call(body, name=name, out_shape=S((L, C), bf16), grid=(L // tm,),
                          in_specs=[_rows(tm, C), _const((N_MEM, 2 * C))], out_specs=_rows(tm, C),
                          compiler_params=_cp("parallel"))(q, kv)


def _attn_bwd(q, kv, do, *, name):
    L, C = q.shape
    tm = _pick(L, 512)
    Dh = XA_HEAD_DIM

    def body(q_ref, kv_ref, do_ref, dq_ref, dkv_ref):
        i = pl.program_id(0)

        @pl.when(i == 0)
        def _():
            dkv_ref[...] = jnp.zeros_like(dkv_ref)

        for h in range(XA_HEADS):
            qh = q_ref[:, h * Dh:(h + 1) * Dh]
            kh = kv_ref[:, h * Dh:(h + 1) * Dh]
            vh = kv_ref[:, C + h * Dh:C + (h + 1) * Dh]
            doh = do_ref[:, h * Dh:(h + 1) * Dh]
            s = lax.dot_general(qh, kh, _DN["nt"], preferred_element_type=f32) * _XA_SCALE
            e = jnp.exp(s - jnp.max(s, axis=-1, keepdims=True))
            p = e / jnp.sum(e, axis=-1, keepdims=True)
            pb = p.astype(bf16)
            dkv_ref[:, C + h * Dh:C + (h + 1) * Dh] += lax.dot_general(pb, doh, _DN["tn"], preferred_element_type=f32)
            dp = lax.dot_general(doh, vh, _DN["nt"], preferred_element_type=f32)
            ds = (p * (dp - jnp.sum(dp * p, axis=-1, keepdims=True)) * _XA_SCALE).astype(bf16)
            dq_ref[:, h * Dh:(h + 1) * Dh] = jnp.dot(ds, kh, preferred_element_type=f32).astype(bf16)
            dkv_ref[:, h * Dh:(h + 1) * Dh] += lax.dot_general(ds, qh, _DN["tn"], preferred_element_type=f32)

    return pl.pallas_call(body, name=name, out_shape=(S((L, C), bf16), S((N_MEM, 2 * C), f32)), grid=(L // tm,),
                          in_specs=[_rows(tm, C), _const((N_MEM, 2 * C)), _rows(tm, C)],
                          out_specs=(_rows(tm, C), _const((N_MEM, 2 * C))),
                          compiler_params=_cp("arbitrary"))(q, kv, do)


Q = CHUNK
PAIRS = HEADS_PER_GROUP // 2
GW = HEADS_PER_GROUP * HEAD_DIM


def _split3(x):
    x1 = x.astype(bf16)
    r1 = x - x1.astype(f32)
    x2 = r1.astype(bf16)
    return x1, x2, (r1 - x2.astype(f32)).astype(bf16)


def _sel_right(x, sel, mode="nn"):
    return sum(lax.dot_general(p, sel, _DN[mode], preferred_element_type=f32) for p in _split3(x))


def _sel_left(sel, x):
    return sum(lax.dot_general(sel, p, _DN["nn"], preferred_element_type=f32) for p in _split3(x))


def _ssd_common(dt_ref, hp_ref):
    dt_pre = dt_ref[...] + hp_ref[0:1, :]
    dt = _softplus(dt_pre)
    A = -jnp.exp(hp_ref[1:2, :])
    a = dt * A
    row = lax.broadcasted_iota(jnp.int32, (Q, Q), 0)
    col = lax.broadcasted_iota(jnp.int32, (Q, Q), 1)
    tri = row >= col
    cs = _sel_left(tri.astype(bf16), a)
    T = cs[Q - 1:Q, :]
    return dict(dt_pre=dt_pre, dt=dt, A=A, cs=cs, csT=cs.T, T=T, ecs=jnp.exp(cs), eend=jnp.exp(T - cs), eT=jnp.exp(T),
                tri=tri, row=row, col=col)


def _pair_expand(v, jj, lo):
    return jnp.where(lo, v[:, 2 * jj:2 * jj + 1], v[:, 2 * jj + 1:2 * jj + 2])


def _decay(cm, h):
    seg = cm["cs"][:, h:h + 1] - cm["csT"][h:h + 1, :]
    return jnp.where(cm["tri"], jnp.exp(jnp.where(cm["tri"], seg, 0.0)), 0.0)


def _ssd_fwd(act, dtp, hp, *, name):
    L = act.shape[0]
    nc = L // Q

    def body(xs_ref, b_ref, c_ref, dt_ref, hp_ref, y_ref, hs_ref, h_scr):
        c = pl.program_id(1)

        @pl.when(c == 0)
        def _():
            h_scr[...] = jnp.zeros_like(h_scr)

        cm = _ssd_common(dt_ref, hp_ref)
        Bb, Cb = b_ref[...].astype(bf16), c_ref[...].astype(bf16)
        CB = lax.dot_general(Cb, Bb, _DN["nt"], preferred_element_type=f32)
        lo = lax.broadcasted_iota(jnp.int32, (Q, LANE), 1) < HEAD_DIM
        top = lax.broadcasted_iota(jnp.int32, (LANE, LANE), 0) < HEAD_DIM
        Drow = hp_ref[2:3, :]
        for jj in range(PAIRS):
            hA, hB = 2 * jj, 2 * jj + 1
            dtx, ecsx, eendx = (_pair_expand(cm[k], jj, lo) for k in ("dt", "ecs", "eend"))
            xs_p = xs_ref[:, jj * LANE:(jj + 1) * LANE]
            Xd = xs_p * dtx
            Y = None
            for h, Xm in ((hA, jnp.where(lo, Xd, 0.0)), (hB, jnp.where(lo, 0.0, Xd))):
                W = (CB * _decay(cm, h)).astype(bf16)
                t = jnp.dot(W, Xm.astype(bf16), preferred_element_type=f32)
                Y = t if Y is None else Y + t
            Hp = h_scr[jj]
            hs_ref[0, jj] = Hp
            Yoff = lax.dot_general(Cb, Hp.astype(bf16), _DN["nt"], preferred_element_type=f32) * ecsx
            Dx = jnp.where(lo[0:1, :], Drow[:, hA:hA + 1], Drow[:, hB:hB + 1])
            y_ref[:, jj * LANE:(jj + 1) * LANE] = Y + Yoff + xs_p * Dx
            Snew = lax.dot_general((Xd * eendx).astype(bf16), Bb, _DN["tn"], preferred_element_type=f32)
            eTx = jnp.where(top, cm["eT"][:, hA:hA + 1], cm["eT"][:, hB:hB + 1])
            h_scr[jj] = Hp * eTx + Snew

    return pl.pallas_call(
        body, name=name, out_shape=(S((L, D_INNER), f32), S((nc, N_SSM_HEADS // 2, LANE, D_STATE), f32)),
        grid=(N_GROUPS, nc),
        in_specs=[pl.BlockSpec((Q, GW), lambda g, c: (c, g)),
                  pl.BlockSpec((Q, D_STATE), lambda g, c: (c, D_INNER // D_STATE + g)),
                  pl.BlockSpec((Q, D_STATE), lambda g, c: (c, D_INNER // D_STATE + N_GROUPS + g)),
                  pl.BlockSpec((Q, LANE), lambda g, c: (c, g)),
                  pl.BlockSpec((SUBLANE, LANE), lambda g, c: (0, g))],
        out_specs=(pl.BlockSpec((Q, GW), lambda g, c: (c, g)),
                   pl.BlockSpec((1, PAIRS, LANE, D_STATE), lambda g, c: (c, g, 0, 0))),
        scratch_shapes=[pltpu.VMEM((PAIRS, LANE, D_STATE), f32)],
        compiler_params=_cp("arbitrary", "arbitrary"))(act, act, act, dtp, hp)


def _ssd_bwd(act, dtp, hp, dy, hs, *, name):
    L = act.shape[0]
    nc = L // Q

    def body(xs_ref, b_ref, c_ref, dt_ref, hp_ref, dy_ref, hs_ref, dxs_ref, db_ref, dc_ref, ddt_ref, dhp_ref, dh_scr):
        c = pl.program_id(1)

        @pl.when(c == 0)
        def _():
            dh_scr[...] = jnp.zeros_like(dh_scr)
            dhp_ref[...] = jnp.zeros_like(dhp_ref)

        cm = _ssd_common(dt_ref, hp_ref)
        Bb, Cb = b_ref[...].astype(bf16), c_ref[...].astype(bf16)
        CB = lax.dot_general(Cb, Bb, _DN["nt"], preferred_element_type=f32)
        lane = lax.broadcasted_iota(jnp.int32, (Q, LANE), 1)
        sub = lax.broadcasted_iota(jnp.int32, (LANE, LANE), 0)
        lo = lane < HEAD_DIM
        top = sub < HEAD_DIM
        Drow = hp_ref[2:3, :]
        zero = jnp.zeros((Q, LANE), f32)
        dcs, ddtx, dC, dB, dCB = zero, zero, zero, zero, jnp.zeros((Q, Q), f32)
        dD_row = jnp.zeros((1, LANE), f32)
        dT_row = jnp.zeros((1, LANE), f32)
        for jj in range(PAIRS):
            hA, hB = 2 * jj, 2 * jj + 1
            Pj = (lane == jnp.where(top, hA, hB)).astype(bf16)
            dtx, ecsx, eendx = (_pair_expand(cm[k], jj, lo) for k in ("dt", "ecs", "eend"))
            xs_p = xs_ref[:, jj * LANE:(jj + 1) * LANE]
            dY_p = dy_ref[:, jj * LANE:(jj + 1) * LANE]
            Xd = xs_p * dtx
            Xdb = Xd.astype(bf16)
            Hp, dHn = hs_ref[0, jj], dh_scr[jj]
            Hb, dHb = Hp.astype(bf16), dHn.astype(bf16)
            EdYb = (dY_p * ecsx).astype(bf16)
            YoffN = lax.dot_general(Cb, Hb, _DN["nt"], preferred_element_type=f32)
            dC = dC + jnp.dot(EdYb, Hb, preferred_element_type=f32)
            dH_off = lax.dot_general(EdYb, Cb, _DN["tn"], preferred_element_type=f32)
            R = lax.dot_general(Bb, dHb, _DN["nt"], preferred_element_type=f32)
            Xe = Xd * eendx
            dB = dB + jnp.dot(Xe.astype(bf16), dHb, preferred_element_type=f32)
            dXd = R * eendx
            V2 = _sel_right(R * Xe, Pj)
            dcs = dcs + _sel_right(dY_p * YoffN * ecsx, Pj) - V2
            dT_row = dT_row + jnp.sum(V2, axis=0, keepdims=True) \
                + jnp.sum(_sel_right(dHn * Hp, Pj, "tn"), axis=0, keepdims=True) * cm["eT"]
            for h, keep in ((hA, lo), (hB, jnp.logical_not(lo))):
                M = _decay(cm, h)
                Wf = CB * M
                dYm = jnp.where(keep, dY_p, 0.0).astype(bf16)
                dW = lax.dot_general(dYm, Xdb, _DN["nt"], preferred_element_type=f32)
                dXd = dXd + lax.dot_general(Wf.astype(bf16), dYm, _DN["tn"], preferred_element_type=f32)
                Z = dW * Wf
                onesh = (lane == h).astype(bf16)
                dcs = dcs + _sel_right(Z, onesh) - _sel_right(Z, onesh, "tn")
                dCB = dCB + dW * M
            Dx = jnp.where(lo[0:1, :], Drow[:, hA:hA + 1], Drow[:, hB:hB + 1])
            dxs_ref[:, jj * LANE:(jj + 1) * LANE] = dXd * dtx + dY_p * Dx
            ddtx = ddtx + _sel_right(dXd * xs_p, Pj)
            dD_row = dD_row + jnp.sum(_sel_right(dY_p * xs_p, Pj), axis=0, keepdims=True)
            eTx = jnp.where(top, cm["eT"][:, hA:hA + 1], cm["eT"][:, hB:hB + 1])
            dh_scr[jj] = dHn * eTx + dH_off
        dCBb = dCB.astype(bf16)
        dc_ref[...] = dC + jnp.dot(dCBb, Bb, preferred_element_type=f32)
        db_ref[...] = dB + lax.dot_general(dCBb, Cb, _DN["tn"], preferred_element_type=f32)
        dcs = dcs + jnp.where(lax.broadcasted_iota(jnp.int32, (Q, LANE), 0) == Q - 1, dT_row, 0.0)
        da = _sel_left((cm["row"] <= cm["col"]).astype(bf16), dcs)
        ddt_pre = (da * cm["A"] + ddtx) * _sigmoid(cm["dt_pre"])
        ddt_ref[...] = ddt_pre
        r8 = lax.broadcasted_iota(jnp.int32, (SUBLANE, LANE), 0)
        dhp_ref[...] += jnp.where(r8 == 0, jnp.sum(ddt_pre, axis=0, keepdims=True),
                                  jnp.where(r8 == 1, jnp.sum(da * cm["dt"], axis=0, keepdims=True) * cm["A"],
                                            jnp.where(r8 == 2, dD_row, 0.0)))

    rev = lambda c: nc - 1 - c
    return pl.pallas_call(
        body, name=name,
        out_shape=(S((L, D_INNER), f32), S((L, N_GROUPS * D_STATE), f32), S((L, N_GROUPS * D_STATE), f32),
                   S((L, N_GROUPS * LANE), f32), S((SUBLANE, N_GROUPS * LANE), f32)),
        grid=(N_GROUPS, nc),
        in_specs=[pl.BlockSpec((Q, GW), lambda g, c: (rev(c), g)),
                  pl.BlockSpec((Q, D_STATE), lambda g, c: (rev(c), D_INNER // D_STATE + g)),
                  pl.BlockSpec((Q, D_STATE), lambda g, c: (rev(c), D_INNER // D_STATE + N_GROUPS + g)),
                  pl.BlockSpec((Q, LANE), lambda g, c: (rev(c), g)),
                  pl.BlockSpec((SUBLANE, LANE), lambda g, c: (0, g)),
                  pl.BlockSpec((Q, GW), lambda g, c: (rev(c), g)),
                  pl.BlockSpec((1, PAIRS, LANE, D_STATE), lambda g, c: (rev(c), g, 0, 0))],
        out_specs=(pl.BlockSpec((Q, GW), lambda g, c: (rev(c), g)),
                   pl.BlockSpec((Q, D_STATE), lambda g, c: (rev(c), g)),
                   pl.BlockSpec((Q, D_STATE), lambda g, c: (rev(c), g)),
                   pl.BlockSpec((Q, LANE), lambda g, c: (rev(c), g)),
                   pl.BlockSpec((SUBLANE, LANE), lambda g, c: (0, g))),
        scratch_shapes=[pltpu.VMEM((PAIRS, LANE, D_STATE), f32)],
        compiler_params=_cp("arbitrary", "arbitrary"))(act, act, act, dtp, hp, dy, hs)


def _group_pad_cols(w):
    lead = w.shape[:-1]
    w = w.reshape(lead + (N_GROUPS, HEADS_PER_GROUP))
    w = jnp.pad(w, [(0, 0)] * len(lead) + [(0, 0), (0, LANE - HEADS_PER_GROUP)])
    return w.reshape(lead + (N_GROUPS * LANE,))


def _group_unpad_cols(w):
    lead = w.shape[:-1]
    return w.reshape(lead + (N_GROUPS, LANE))[..., :HEADS_PER_GROUP].reshape(lead + (N_SSM_HEADS,))


def _row(v):
    return v.reshape(1, -1)


def _device_step(x, mem, target, W):
    ng = W['norm_g']
    lw = []
    for i in range(DEPTH):
        j = i // 2
        p = {}
        if i % 2 == 0:
            win = W['ssm_in_w'][j]
            p['wz'], p['wx'] = win[:, :D_INNER], win[:, D_INNER:D_INNER + CONV_DIM]
            p['wdt'] = _group_pad_cols(win[:, D_INNER + CONV_DIM:])
            p['cw'] = _pad_taps(W['ssm_conv_w'][j], SSM_CONV)
            p['cb'] = _row(W['ssm_conv_b'][j])
            hp = jnp.stack([_group_pad_cols(W['ssm_dt_bias'][j]), _group_pad_cols(W['ssm_A_log'][j]),
                            _group_pad_cols(W['ssm_D'][j])])
            p['hp'] = jnp.pad(hp, ((0, SUBLANE - 3), (0, 0)))
            p['sng'] = _row(W['ssm_norm_g'][j])
            p['wout'] = W['ssm_out_w'][j]
        else:
            p['pw1'], p['pw1b'] = W['cf_pw1_w'][j], _row(W['cf_pw1_b'][j])
            p['dww'], p['dwb'] = _pad_taps(W['cf_dw_w'][j], CF_KERNEL), _row(W['cf_dw_b'][j])
            p['lng'], p['lnb'] = _row(W['cf_ln_g'][j]), _row(W['cf_ln_b'][j])
            p['pw2'], p['pw2b'] = W['cf_pw2_w'][j], _row(W['cf_pw2_b'][j])
        p['memg'] = _row(W['xa_mem_g'][i])
        p['wq'], p['wkv'], p['wo'] = W['xa_q_w'][i], W['xa_kv_w'][i], W['xa_o_w'][i]
        p['fin'], p['fout'] = W['ffn_in_w'][i], W['ffn_out_w'][i]
        p['fcw'], p['fcb'] = _pad_taps(W['ffn_conv_w'][i], FFN_CONV), _row(W['ffn_conv_b'][i])
        p['g'] = [_row(ng[i, s]) for s in range(6)]
        lw.append(p)

    saved = []
    X = x
    h = _rmsnorm_fwd(X, lw[0]['g'][0], name="norm_in")
    for i in range(DEPTH):
        p, sv = lw[i], {}
        sv['X0'], sv['h'] = X, h
        if i % 2 == 0:
            z = _mm(h, p['wz'], "nn", name="ssm_z")
            xbc = _mm(h, p['wx'], "nn", name="ssm_xbc")
            dtp = _mm(h, p['wdt'], "nn", name="ssm_dt")
            act = _conv_act_fwd(xbc, p['cw'], p['cb'], K=SSM_CONV, act="silu", name="ssm_conv_fwd", out_dtype=f32)
            y, hs = _ssd_fwd(act, dtp, p['hp'], name="ssd_fwd")
            yn = _gated_norm_fwd(y, z, p['sng'], name="ssm_gnorm_fwd")
            mix = _mm(yn, p['wout'], "nn", name="ssm_out")
            sv.update(z=z, xbc=xbc, dtp=dtp, act=act, y=y, hs=hs, yn=yn)
        else:
            u = _mm(h, p['pw1'], "nn", name="cf_pw1", bias=p['pw1b'])
            c, s = _cf_fwd(u, p['dww'], p['dwb'], p['lng'], p['lnb'], name="cf_conv_fwd")
            mix = _mm(s, p['pw2'], "nn", name="cf_pw2", bias=p['pw2b'])
            sv.update(u=u, c=c, s=s)
        X1, h2 = _resid_norm_fwd(X, mix, p['g'][1], p['g'][2], name="resid_norm_a")
        q = _mm(h2, p['wq'], "nn", name="xa_q", out_dtype=bf16)
        m = _rmsnorm_fwd(mem, p['memg'], name="xa_mem_norm")
        kv = _mm(m, p['wkv'], "nn", name="xa_kv", out_dtype=bf16)
        o = _attn_fwd(q, kv, name="xa_attn_fwd")
        a = _mm(o, p['wo'], "nn", name="xa_o")
        X2, h3 = _resid_norm_fwd(X1, a, p['g'][3], p['g'][4], name="resid_norm_b")
        u0 = _mm(h3, p['fin'], "nn", name="ffn_in")
        fact = _conv_act_fwd(u0, p['fcw'], p['fcb'], K=FFN_CONV, act="swiglu", name="ffn_conv_fwd", out_dtype=bf16)
        f = _mm(fact, p['fout'], "nn", name="ffn_out")
        g_next = lw[i + 1]['g'][0] if i + 1 < DEPTH else None
        X3, hn = _resid_norm_fwd(X2, f, p['g'][5], g_next, name="resid_norm_c" if g_next is not None else "resid_norm_last")
        sv.update(mix=mix, X1=X1, h2=h2, q=q, m=m, kv=kv, o=o, a=a, X2=X2, h3=h3, u0=u0, fact=fact, f=f)
        saved.append(sv)
        X, h = X3, hn

    sse, G = _loss_fwd_bwd(X, target, name="loss")

    gr = {n: [None] * W[n].shape[0] for n in WEIGHT_NAMES}
    dng = [[None] * 6 for _ in range(DEPTH)]
    for i in reversed(range(DEPTH)):
        p, sv, j = lw[i], saved[i], i // 2
        df, dng[i][5], _ = _norm_bwd(sv['f'], p['g'][5], G, name="nb_f", out_dtype=bf16)
        gr['ffn_out_w'][i] = _mm(sv['fact'], df, "tn", name="ffn_out_dw")
        dfact = _mm(df, p['fout'], "nt", name="ffn_out_dx")
        du, dcw, dcb = _conv_act_bwd(sv['u0'], [dfact], p['fcw'], p['fcb'], K=FFN_CONV, act="swiglu", name="ffn_conv_bwd")
        gr['ffn_conv_w'][i], gr['ffn_conv_b'][i] = dcw[:FFN_CONV], dcb[0]
        du0 = _conv_transpose(du, p['fcw'], K=FFN_CONV, name="ffn_conv_bwd_x")
        gr['ffn_in_w'][i] = _mm(sv['h3'], du0, "tn", name="ffn_in_dw")
        dh3 = _mm(du0, p['fin'], "nt", name="ffn_in_dx")
        G, dng[i][4], _ = _norm_bwd(sv['X2'], p['g'][4], dh3, name="nb_x2", add=G)
        da, dng[i][3], _ = _norm_bwd(sv['a'], p['g'][3], G, name="nb_a", out_dtype=bf16)
        gr['xa_o_w'][i] = _mm(sv['o'], da, "tn", name="xa_o_dw")
        do = _mm(da, p['wo'], "nt", name="xa_o_dx", out_dtype=bf16)
        dq, dkv = _attn_bwd(sv['q'], sv['kv'], do, name="xa_attn_bwd")
        gr['xa_q_w'][i] = _mm(sv['h2'], dq, "tn", name="xa_q_dw")
        dh2 = _mm(dq, p['wq'], "nt", name="xa_q_dx")
        gr['xa_kv_w'][i] = _mm(sv['m'], dkv, "tn", name="xa_kv_dw")
        dm = _mm(dkv, p['wkv'], "nt", name="xa_kv_dx")
        _, dmg, _ = _norm_bwd(mem, p['memg'], dm, name="nb_mem")
        gr['xa_mem_g'][i] = dmg[0]
        G, dng[i][2], _ = _norm_bwd(sv['X1'], p['g'][2], dh2, name="nb_x1", add=G)
        dmix, dng[i][1], dmix_sum = _norm_bwd(sv['mix'], p['g'][1], G, name="nb_mix", out_dtype=bf16)
        if i % 2 == 0:
            gr['ssm_out_w'][j] = _mm(sv['yn'], dmix, "tn", name="ssm_out_dw")
            dyn = _mm(dmix, p['wout'], "nt", name="ssm_out_dx")
            dy, dz, dsng = _gated_norm_bwd(sv['y'], sv['z'], p['sng'], dyn, name="ssm_gnorm_bwd")
            gr['ssm_norm_g'][j] = dsng[0]
            dxs, dB, dC, ddtp, dhp = _ssd_bwd(sv['act'], sv['dtp'], p['hp'], dy, sv['hs'], name="ssd_bwd")
            gr['ssm_dt_bias'][j], gr['ssm_A_log'][j], gr['ssm_D'][j] = (_group_unpad_cols(dhp[r]) for r in range(3))
            dpre, dcw, dcb = _conv_act_bwd(sv['xbc'], [dxs, dB, dC], p['cw'], p['cb'], K=SSM_CONV, act="silu",
                                           name="ssm_conv_bwd")
            gr['ssm_conv_w'][j], gr['ssm_conv_b'][j] = dcw[:SSM_CONV], dcb[0]
            dxbc = _conv_transpose(dpre, p['cw'], K=SSM_CONV, name="ssm_conv_bwd_x")
            hh = sv['h']
            dwz = _mm(hh, dz, "tn", name="ssm_z_dw")
            dwx = _mm(hh, dxbc, "tn", name="ssm_xbc_dw")
            dwdt = _mm(hh, ddtp, "tn", name="ssm_dt_dw")
            gr['ssm_in_w'][j] = jnp.concatenate([dwz, dwx, _group_unpad_cols(dwdt)], axis=1)
            dh = _mm(dz, p['wz'], "nt", name="ssm_z_dx")
            dh = _mm(dxbc, p['wx'], "nt", name="ssm_xbc_dx", add=dh)
            dh = _mm(ddtp, p['wdt'], "nt", name="ssm_dt_dx", add=dh)
        else:
            gr['cf_pw2_w'][j] = _mm(sv['s'], dmix, "tn", name="cf_pw2_dw")
            gr['cf_pw2_b'][j] = dmix_sum[0]
            ds = _mm(dmix, p['pw2'], "nt", name="cf_pw2_dx")
            dc, dlg, dlb = _cf_ln_bwd(sv['c'], p['lng'], p['lnb'], ds, name="cf_ln_bwd")
            gr['cf_ln_g'][j], gr['cf_ln_b'][j] = dlg[0], dlb[0]
            du, ddw, ddb, dus = _cf_glu_bwd(sv['u'], dc, p['dww'], name="cf_glu_bwd")
            gr['cf_dw_w'][j], gr['cf_dw_b'][j], gr['cf_pw1_b'][j] = ddw[:CF_KERNEL], ddb[0], dus[0]
            gr['cf_pw1_w'][j] = _mm(sv['h'], du, "tn", name="cf_pw1_dw")
            dh = _mm(du, p['pw1'], "nt", name="cf_pw1_dx")
        G, dng[i][0], _ = _norm_bwd(sv['X0'], p['g'][0], dh, name="nb_x0", add=G)
    gr['norm_g'] = [jnp.concatenate(dng[i], axis=0) for i in range(DEPTH)]
    grads = {n: jnp.stack(gr[n]) for n in WEIGHT_NAMES}
    return sse, G, grads


MESH = pl.DeviceIdType.MESH
HBM_SPEC = pl.BlockSpec(memory_space=pltpu.HBM)


def _chip_peers(x, y):
    return [(1 - x, y), (x, 1 - y), (1 - x, 1 - y)]


def _all_gather_chips(buf, *, name):
    R, C = buf.shape

    def body(in_ref, out_ref, send_sems, recv_sems, local_sem):
        x, y, c = lax.axis_index("x"), lax.axis_index("y"), lax.axis_index("c")
        me = 2 * x + y
        mine = pltpu.make_async_copy(in_ref, out_ref.at[me], local_sem)
        mine.start()
        peers = _chip_peers(x, y)
        sends = []
        for k, (px, py) in enumerate(peers):
            cp = pltpu.make_async_remote_copy(src_ref=in_ref, dst_ref=out_ref.at[me], send_sem=send_sems.at[k],
                                              recv_sem=recv_sems.at[k], device_id=(px, py, c), device_id_type=MESH)
            cp.start()
            sends.append(cp)
        for k, (px, py) in enumerate(peers):
            pltpu.make_async_remote_copy(src_ref=in_ref, dst_ref=out_ref.at[2 * px + py], send_sem=send_sems.at[k],
                                         recv_sem=recv_sems.at[k], device_id=(px, py, c), device_id_type=MESH).wait_recv()
        for cp in sends:
            cp.wait_send()
        mine.wait()

    return pl.pallas_call(body, name=name, out_shape=S((N_CHIPS, R, C), buf.dtype), in_specs=[HBM_SPEC], out_specs=HBM_SPEC,
                          scratch_shapes=[pltpu.SemaphoreType.DMA((3,)), pltpu.SemaphoreType.DMA((3,)),
                                          pltpu.SemaphoreType.DMA(())])(buf)


def _scatter_chips(buf, *, name):
    _, R, C = buf.shape

    def body(in_ref, out_ref, send_sems, recv_sems, local_sem):
        x, y, c = lax.axis_index("x"), lax.axis_index("y"), lax.axis_index("c")
        me = 2 * x + y
        mine = pltpu.make_async_copy(in_ref.at[me], out_ref.at[me], local_sem)
        mine.start()
        peers = _chip_peers(x, y)
        sends = []
        for k, (px, py) in enumerate(peers):
            cp = pltpu.make_async_remote_copy(src_ref=in_ref.at[2 * px + py], dst_ref=out_ref.at[me],
                                              send_sem=send_sems.at[k], recv_sem=recv_sems.at[k],
                                              device_id=(px, py, c), device_id_type=MESH)
            cp.start()
            sends.append(cp)
        for k, (px, py) in enumerate(peers):
            pltpu.make_async_remote_copy(src_ref=in_ref.at[me], dst_ref=out_ref.at[2 * px + py], send_sem=send_sems.at[k],
                                         recv_sem=recv_sems.at[k], device_id=(px, py, c), device_id_type=MESH).wait_recv()
        for cp in sends:
            cp.wait_send()
        mine.wait()

    return pl.pallas_call(body, name=name, out_shape=S((N_CHIPS, R, C), buf.dtype), in_specs=[HBM_SPEC], out_specs=HBM_SPEC,
                          scratch_shapes=[pltpu.SemaphoreType.DMA((3,)), pltpu.SemaphoreType.DMA((3,)),
                                          pltpu.SemaphoreType.DMA(())])(buf)


def _sum_chips(buf, *, name):
    _, R, C = buf.shape
    tr = _pick(R, 512)
    assert R % tr == 0

    def body(a_ref, b_ref, c_ref, d_ref, o_ref):
        o_ref[...] = ((a_ref[0] + b_ref[0]) + c_ref[0]) + d_ref[0]

    specs = [pl.BlockSpec((1, tr, C), functools.partial(lambda s, i: (s, i, 0), s)) for s in range(N_CHIPS)]
    return pl.pallas_call(body, name=name, out_shape=S((R, C), buf.dtype), grid=(R // tr,), in_specs=specs,
                          out_specs=pl.BlockSpec((tr, C), lambda i: (i, 0)), compiler_params=_cp("parallel"))(buf, buf, buf, buf)


def _swap_sibling(buf, *, name):
    def body(in_ref, out_ref, send_sem, recv_sem):
        x, y, c = lax.axis_index("x"), lax.axis_index("y"), lax.axis_index("c")
        cp = pltpu.make_async_remote_copy(src_ref=in_ref, dst_ref=out_ref, send_sem=send_sem, recv_sem=recv_sem,
                                          device_id=(x, y, 1 - c), device_id_type=MESH)
        cp.start()
        cp.wait()

    return pl.pallas_call(body, name=name, out_shape=S(buf.shape, buf.dtype), in_specs=[HBM_SPEC], out_specs=HBM_SPEC,
                          scratch_shapes=[pltpu.SemaphoreType.DMA(()), pltpu.SemaphoreType.DMA(())])(buf)


def _adamw(w, ga, gb, m, v, *, name):
    shape = w.shape
    C = shape[-1]
    Rr = math.prod(shape[:-1])
    to2 = lambda t: t.reshape(Rr, C)
    tr = Rr
    if Rr * C * 4 > (1 << 20):
        tr = max(t for t in range(SUBLANE, Rr + 1, SUBLANE) if Rr % t == 0 and t * C * 4 <= (1 << 20))
    c1 = 1.0 / (1.0 - ADAM_B1 ** ADAM_STEP)
    c2 = 1.0 / (1.0 - ADAM_B2 ** ADAM_STEP)

    def body(w_ref, ga_ref, gb_ref, m_ref, v_ref, g_ref, d_ref, mo_ref, vo_ref):
        g = ga_ref[...] + gb_ref[...]
        mn = ADAM_B1 * m_ref[...] + (1.0 - ADAM_B1) * g
        vn = ADAM_B2 * v_ref[...] + (1.0 - ADAM_B2) * (g * g)
        g_ref[...] = g
        mo_ref[...] = mn
        vo_ref[...] = vn
        d_ref[...] = -ADAM_LR * ((mn * c1) / (jnp.sqrt(vn * c2) + ADAM_EPS) + ADAM_WD * w_ref[...])

    spec = pl.BlockSpec((tr, C), lambda i: (i, 0))
    out = pl.pallas_call(body, name=name, out_shape=tuple(S((Rr, C), f32) for _ in range(4)), grid=(Rr // tr,),
                         in_specs=[spec] * 5, out_specs=(spec,) * 4,
                         compiler_params=_cp("parallel"))(to2(w), to2(ga), to2(gb), to2(m), to2(v))
    return tuple(o.reshape(shape) for o in out)


def _pack_rows(parts, dtype):
    flat = jnp.concatenate([p.reshape(-1).astype(dtype) for p in parts])
    n = flat.shape[0]
    unit = PACK_COLS * 512
    padded = -(-n // unit) * unit
    return jnp.pad(flat, (0, padded - n)).reshape(padded // PACK_COLS, PACK_COLS)


def _unpack_rows(flat2d, shapes):
    flat = flat2d.reshape(-1)
    out, off = [], 0
    for shp in shapes:
        n = math.prod(shp)
        out.append(flat[off:off + n].reshape(shp))
        off += n
    return out


def _gather_weights(local, names, dtype, *, name):
    shapes = [local[n].shape for n in names]
    got = _all_gather_chips(_pack_rows([local[n] for n in names], dtype), name=name)
    per_chip = [_unpack_rows(got[s], shapes) for s in range(N_CHIPS)]
    return {n: jnp.concatenate([per_chip[s][k] for s in range(N_CHIPS)], axis=SHARD_AXIS[n]) for k, n in enumerate(names)}


def kernel(x, mem, norm_g, ssm_in_w, ssm_conv_w, ssm_conv_b, ssm_dt_bias, ssm_A_log, ssm_D, ssm_norm_g, ssm_out_w, cf_pw1_w, cf_pw1_b, cf_dw_w, cf_dw_b, cf_ln_g, cf_ln_b, cf_pw2_w, cf_pw2_b, xa_mem_g, xa_q_w, xa_kv_w, xa_o_w, ffn_in_w, ffn_conv_w, ffn_conv_b, ffn_out_w, loss_target, m_norm_g, m_ssm_in_w, m_ssm_conv_w, m_ssm_conv_b, m_ssm_dt_bias, m_ssm_A_log, m_ssm_D, m_ssm_norm_g, m_ssm_out_w, m_cf_pw1_w, m_cf_pw1_b, m_cf_dw_w, m_cf_dw_b, m_cf_ln_g, m_cf_ln_b, m_cf_pw2_w, m_cf_pw2_b, m_xa_mem_g, m_xa_q_w, m_xa_kv_w, m_xa_o_w, m_ffn_in_w, m_ffn_conv_w, m_ffn_conv_b, m_ffn_out_w, v_norm_g, v_ssm_in_w, v_ssm_conv_w, v_ssm_conv_b, v_ssm_dt_bias, v_ssm_A_log, v_ssm_D, v_ssm_norm_g, v_ssm_out_w, v_cf_pw1_w, v_cf_pw1_b, v_cf_dw_w, v_cf_dw_b, v_cf_ln_g, v_cf_ln_b, v_cf_pw2_w, v_cf_pw2_b, v_xa_mem_g, v_xa_q_w, v_xa_kv_w, v_xa_o_w, v_ffn_in_w, v_ffn_conv_w, v_ffn_conv_b, v_ffn_out_w):
    w_local = dict(zip(WEIGHT_NAMES, (norm_g, ssm_in_w, ssm_conv_w, ssm_conv_b, ssm_dt_bias, ssm_A_log, ssm_D, ssm_norm_g,
                                      ssm_out_w, cf_pw1_w, cf_pw1_b, cf_dw_w, cf_dw_b, cf_ln_g, cf_ln_b, cf_pw2_w, cf_pw2_b,
                                      xa_mem_g, xa_q_w, xa_kv_w, xa_o_w, ffn_in_w, ffn_conv_w, ffn_conv_b, ffn_out_w)))
    m_local = dict(zip(WEIGHT_NAMES, (m_norm_g, m_ssm_in_w, m_ssm_conv_w, m_ssm_conv_b, m_ssm_dt_bias, m_ssm_A_log, m_ssm_D,
                                      m_ssm_norm_g, m_ssm_out_w, m_cf_pw1_w, m_cf_pw1_b, m_cf_dw_w, m_cf_dw_b, m_cf_ln_g,
                                      m_cf_ln_b, m_cf_pw2_w, m_cf_pw2_b, m_xa_mem_g, m_xa_q_w, m_xa_kv_w, m_xa_o_w,
                                      m_ffn_in_w, m_ffn_conv_w, m_ffn_conv_b, m_ffn_out_w)))
    v_local = dict(zip(WEIGHT_NAMES, (v_norm_g, v_ssm_in_w, v_ssm_conv_w, v_ssm_conv_b, v_ssm_dt_bias, v_ssm_A_log, v_ssm_D,
                                      v_ssm_norm_g, v_ssm_out_w, v_cf_pw1_w, v_cf_pw1_b, v_cf_dw_w, v_cf_dw_b, v_cf_ln_g,
                                      v_cf_ln_b, v_cf_pw2_w, v_cf_pw2_b, v_xa_mem_g, v_xa_q_w, v_xa_kv_w, v_xa_o_w,
                                      v_ffn_in_w, v_ffn_conv_w, v_ffn_conv_b, v_ffn_out_w)))

    small_sharded = [n for n in WEIGHT_NAMES if SHARD_AXIS[n] is not None and n not in MATMUL_WEIGHTS]
    W = {n: w_local[n] for n in WEIGHT_NAMES if SHARD_AXIS[n] is None}
    W.update(_gather_weights(w_local, list(MATMUL_WEIGHTS), bf16, name="gather_matmul_weights"))
    W.update(_gather_weights(w_local, small_sharded, f32, name="gather_small_weights"))

    sse, gx, grads = _device_step(x[0], mem[0], loss_target[0], W)

    loss = lax.psum(0.5 * sse[0, 0] / D_MODEL, ("x", "y", "c"))

    shard_shapes = [w_local[n].shape for n in WEIGHT_NAMES]
    blocks = []
    for s in range(N_CHIPS):
        parts = []
        for n in WEIGHT_NAMES:
            ax = SHARD_AXIS[n]
            if ax is None:
                parts.append(grads[n])
            else:
                width = w_local[n].shape[ax]
                parts.append(lax.slice_in_dim(grads[n], s * width, (s + 1) * width, axis=ax))
        blocks.append(_pack_rows(parts, f32))
    gbuf = jnp.stack(blocks)
    part = _sum_chips(_scatter_chips(gbuf, name="grad_scatter_chips"), name="grad_sum_chips")
    other = _swap_sibling(part, name="grad_swap_sibling")
    g_mine = _unpack_rows(part, shard_shapes)
    g_other = _unpack_rows(other, shard_shapes)

    outs_g, outs_d, outs_m, outs_v = [], [], [], []
    for k, n in enumerate(WEIGHT_NAMES):
        g, d, mn, vn = _adamw(w_local[n], g_mine[k], g_other[k], m_local[n], v_local[n], name="adamw_" + n)
        outs_g.append(g)
        outs_d.append(d)
        outs_m.append(mn)
        outs_v.append(vn)
    return (loss, gx[None], *outs_g, *outs_d, *outs_m, *outs_v)
```

```python
import functools
import math

import jax
import jax.numpy as jnp
from jax import lax
from jax.experimental import pallas as pl
from jax.experimental.pallas import tpu as pltpu

f32 = jnp.float32
bf16 = jnp.bfloat16
S = jax.ShapeDtypeStruct

D_MODEL = 1024
DEPTH = 4
D_INNER = 2048
HEAD_DIM = 64
N_GROUPS = 4
HEADS_PER_GROUP = 8
N_SSM_HEADS = 32
D_STATE = 128
CHUNK = 128
SSM_CONV = 4
CONV_DIM = 3072
CF_KERNEL = 31
N_MEM = 256
XA_HEADS = 4
XA_HEAD_DIM = 256
D_FF = 2816
FFN_CONV = 3
EPS = 1e-6
ADAM_LR, ADAM_B1, ADAM_B2, ADAM_EPS, ADAM_WD, ADAM_STEP = 0.001, 0.9, 0.999, 1e-08, 0.01, 10

LANE = 128
SUBLANE = 8
VMEM_LIMIT = 56 * 1024 * 1024
N_CHIPS = 4
PACK_COLS = 1024

WEIGHT_NAMES = ['norm_g', 'ssm_in_w', 'ssm_conv_w', 'ssm_conv_b', 'ssm_dt_bias', 'ssm_A_log', 'ssm_D', 'ssm_norm_g',
                'ssm_out_w', 'cf_pw1_w', 'cf_pw1_b', 'cf_dw_w', 'cf_dw_b', 'cf_ln_g', 'cf_ln_b', 'cf_pw2_w', 'cf_pw2_b',
                'xa_mem_g', 'xa_q_w', 'xa_kv_w', 'xa_o_w', 'ffn_in_w', 'ffn_conv_w', 'ffn_conv_b', 'ffn_out_w']
SHARD_AXIS = {'norm_g': 2, 'ssm_in_w': 2, 'ssm_conv_w': 2, 'ssm_conv_b': None, 'ssm_dt_bias': None, 'ssm_A_log': None,
              'ssm_D': None, 'ssm_norm_g': None, 'ssm_out_w': 1, 'cf_pw1_w': 2, 'cf_pw1_b': 1, 'cf_dw_w': 2, 'cf_dw_b': 1,
              'cf_ln_g': 1, 'cf_ln_b': 1, 'cf_pw2_w': 1, 'cf_pw2_b': 1, 'xa_mem_g': None, 'xa_q_w': 1, 'xa_kv_w': 2,
              'xa_o_w': 1, 'ffn_in_w': 2, 'ffn_conv_w': 2, 'ffn_conv_b': None, 'ffn_out_w': 1}
MATMUL_WEIGHTS = ('ssm_in_w', 'ssm_out_w', 'cf_pw1_w', 'cf_pw2_w', 'xa_q_w', 'xa_kv_w', 'xa_o_w', 'ffn_in_w', 'ffn_out_w')


def _cp(*sem):
    return pltpu.CompilerParams(dimension_semantics=tuple(sem), vmem_limit_bytes=VMEM_LIMIT)


def _pick(dim, pref):
    if dim <= pref:
        return dim
    best = None
    for t in range(LANE, pref + 1, LANE):
        if dim % t == 0:
            best = t
    assert best is not None, (dim, pref)
    return best


def _sigmoid(x):
    return 1.0 / (1.0 + jnp.exp(-x))


def _silu(x):
    return x * _sigmoid(x)


def _dsilu(x):
    s = _sigmoid(x)
    return s * (1.0 + x * (1.0 - s))


def _softplus(x):
    return jnp.maximum(x, 0.0) + jnp.log(1.0 + jnp.exp(-jnp.abs(x)))


_DN = {"nn": (((1,), (0,)), ((), ())), "nt": (((1,), (1,)), ((), ())), "tn": (((0,), (0,)), ((), ()))}


def _mm(a, b, mode, *, name, out_dtype=f32, bias=None, add=None, b_layer=None, b_shards=False, out_into=None):
    bshape = b.shape[-2:] if b_layer is not None else b.shape
    if b_shards:
        bshape = (bshape[0], bshape[1] * N_CHIPS)
    if mode == "nn":
        (M, K), (K2, N) = a.shape, bshape
    elif mode == "nt":
        (M, K), (N, K2) = a.shape, bshape
    else:
        (K, M), (K2, N) = a.shape, bshape
    assert K == K2, (a.shape, b.shape, mode)
    out_cols = out_into is not None and out_into[0] == "cols"
    n_unit = N // N_CHIPS if ((b_shards and mode == "nn") or out_cols) else N
    k_unit = K // N_CHIPS if (b_shards and mode == "nt") else K
    tm, tn, tk = _pick(M, 1024), _pick(n_unit, 1408), _pick(k_unit, 1408)
    nk, nj_u, nk_u = K // tk, n_unit // tn, k_unit // tk
    l = b_layer
    a_spec = {"nn": pl.BlockSpec((tm, tk), lambda i, j, k: (i, k)), "nt": pl.BlockSpec((tm, tk), lambda i, j, k: (i, k)),
              "tn": pl.BlockSpec((tk, tm), lambda i, j, k: (k, i))}[mode]
    if b_layer is None:
        b_spec = {"nn": pl.BlockSpec((tk, tn), lambda i, j, k: (k, j)), "nt": pl.BlockSpec((tn, tk), lambda i, j, k: (j, k)),
                  "tn": pl.BlockSpec((tk, tn), lambda i, j, k: (k, j))}[mode]
    elif not b_shards:
        b_spec = {"nn": pl.BlockSpec((None, tk, tn), lambda i, j, k: (l, k, j)),
                  "nt": pl.BlockSpec((None, tn, tk), lambda i, j, k: (l, j, k))}[mode]
    else:
        b_spec = {"nn": pl.BlockSpec((None, None, tk, tn), lambda i, j, k: (l, j // nj_u, k, j % nj_u)),
                  "nt": pl.BlockSpec((None, None, tn, tk), lambda i, j, k: (l, k // nk_u, j, k % nk_u))}[mode]
    in_specs, args = [a_spec, b_spec], [a, b]
    if bias is not None:
        in_specs.append(pl.BlockSpec((1, tn), lambda i, j, k: (0, j)))
        args.append(bias)
    if add is not None:
        in_specs.append(pl.BlockSpec((tm, tn), lambda i, j, k: (i, j)))
        args.append(add)
    aliases = {}
    if out_into is None:
        out_shape, out_spec = S((M, N), out_dtype), pl.BlockSpec((tm, tn), lambda i, j, k: (i, j))
    else:
        kind, buf, ol, layers = out_into
        if kind == "rows":
            out_shape = S((layers, M, N), out_dtype)
            out_spec = pl.BlockSpec((None, tm, tn), lambda i, j, k: (ol, i, j))
        else:
            out_shape = S((layers, N_CHIPS, M, n_unit), out_dtype)
            out_spec = pl.BlockSpec((None, None, tm, tn), lambda i, j, k: (ol, j // nj_u, i, j % nj_u))
        if buf is not None:
            aliases = {len(args): 0}
            in_specs.append(pl.BlockSpec(memory_space=pl.ANY))
            args.append(buf)
    dn = _DN[mode]
    has_bias, has_add, has_buf = bias is not None, add is not None, bool(aliases)

    def body(a_ref, b_ref, *rest):
        rest = list(rest)
        bias_ref = rest.pop(0) if has_bias else None
        add_ref = rest.pop(0) if has_add else None
        if has_buf:
            rest.pop(0)
        o_ref, acc_ref = rest
        k = pl.program_id(2)

        @pl.when(k == 0)
        def _():
            acc_ref[...] = jnp.zeros_like(acc_ref)

        acc_ref[...] += lax.dot_general(a_ref[...].astype(bf16), b_ref[...].astype(bf16), dn, preferred_element_type=f32)

        @pl.when(k == nk - 1)
        def _():
            r = acc_ref[...]
            if has_bias:
                r = r + bias_ref[...]
            if has_add:
                r = r + add_ref[...].astype(f32)
            o_ref[...] = r.astype(out_dtype)

    return pl.pallas_call(
        body, name=name, out_shape=out_shape, grid=(M // tm, N // tn, nk),
        in_specs=in_specs, out_specs=out_spec, scratch_shapes=[pltpu.VMEM((tm, tn), f32)],
        input_output_aliases=aliases,
        compiler_params=_cp("parallel", "parallel", "arbitrary"))(*args)


def _rows(tm, C):
    return pl.BlockSpec((tm, C), lambda i: (i, 0))


def _const(shape):
    return pl.BlockSpec(shape, lambda i: tuple(0 for _ in shape))


def _rms_val(x, g):
    r = lax.rsqrt(jnp.mean(x * x, axis=-1, keepdims=True) + EPS)
    return x * r * g


def _rms_bwd_val(x, g, dy):
    r = lax.rsqrt(jnp.mean(x * x, axis=-1, keepdims=True) + EPS)
    xn = x * r
    dxh = dy * g
    dx = r * (dxh - xn * jnp.mean(dxh * xn, axis=-1, keepdims=True))
    return dx, jnp.sum(dy * xn, axis=0, keepdims=True)


def _rmsnorm_fwd(x, g, *, name):
    L, C = x.shape
    tm = _pick(L, 512)

    def body(x_ref, g_ref, o_ref):
        o_ref[...] = _rms_val(x_ref[...], g_ref[...]).astype(bf16)

    return pl.pallas_call(body, name=name, out_shape=S((L, C), bf16), grid=(L // tm,),
                          in_specs=[_rows(tm, C), _const((1, C))], out_specs=_rows(tm, C),
                          compiler_params=_cp("parallel"))(x, g)


def _resid_norm_fwd(x, mix, g_post, g_next, *, name):
    L, C = x.shape
    tm = _pick(L, 512)
    want_h = g_next is not None

    def body(x_ref, m_ref, gp_ref, *rest):
        xn = x_ref[...] + _rms_val(m_ref[...], gp_ref[...])
        if want_h:
            gn_ref, xo_ref, h_ref = rest
            h_ref[...] = _rms_val(xn, gn_ref[...]).astype(bf16)
        else:
            (xo_ref,) = rest
        xo_ref[...] = xn

    in_specs = [_rows(tm, C), _rows(tm, C), _const((1, C))]
    args = [x, mix, g_post]
    out_shape, out_specs = [S((L, C), f32)], [_rows(tm, C)]
    if want_h:
        in_specs.append(_const((1, C)))
        args.append(g_next)
        out_shape.append(S((L, C), bf16))
        out_specs.append(_rows(tm, C))
    out = pl.pallas_call(body, name=name, out_shape=tuple(out_shape), grid=(L // tm,), in_specs=in_specs,
                         out_specs=tuple(out_specs), compiler_params=_cp("parallel"))(*args)
    return (out[0], out[1]) if want_h else (out[0], None)


def _norm_bwd(x, g, dy, *, name, add=None, out_dtype=f32):
    L, C = x.shape
    tm = _pick(L, 512)
    has_add = add is not None

    def body(x_ref, g_ref, dy_ref, *rest):
        rest = list(rest)
        add_ref = rest.pop(0) if has_add else None
        dx_ref, dg_ref, cs_ref = rest
        i = pl.program_id(0)

        @pl.when(i == 0)
        def _():
            dg_ref[...] = jnp.zeros_like(dg_ref)
            cs_ref[...] = jnp.zeros_like(cs_ref)

        dx, dg = _rms_bwd_val(x_ref[...], g_ref[...], dy_ref[...].astype(f32))
        dg_ref[...] += dg
        cs_ref[...] += jnp.sum(dx, axis=0, keepdims=True)
        if has_add:
            dx = dx + add_ref[...]
        dx_ref[...] = dx.astype(out_dtype)

    in_specs = [_rows(tm, C), _const((1, C)), _rows(tm, C)]
    args = [x, g, dy]
    if has_add:
        in_specs.append(_rows(tm, C))
        args.append(add)
    return pl.pallas_call(body, name=name, out_shape=(S((L, C), out_dtype), S((1, C), f32), S((1, C), f32)),
                          grid=(L // tm,), in_specs=in_specs,
                          out_specs=(_rows(tm, C), _const((1, C)), _const((1, C))),
                          compiler_params=_cp("arbitrary"))(*args)


def _loss_fwd_bwd(y, target, *, name):
    L, C = y.shape
    tm = _pick(L, 512)

    def body(y_ref, t_ref, acc_ref, dy_ref):
        i = pl.program_id(0)

        @pl.when(i == 0)
        def _():
            acc_ref[...] = jnp.zeros_like(acc_ref)

        e = y_ref[...] - t_ref[...]
        rs = jnp.sum(e * e, axis=-1, keepdims=True)
        acc_ref[...] += jnp.broadcast_to(jnp.sum(rs, axis=0, keepdims=True), (1, LANE))
        dy_ref[...] = e * (1.0 / C)

    return pl.pallas_call(body, name=name, out_shape=(S((1, LANE), f32), S((L, C), f32)), grid=(L // tm,),
                          in_specs=[_rows(tm, C), _rows(tm, C)], out_specs=(_const((1, LANE)), _rows(tm, C)),
                          compiler_params=_cp("arbitrary"))(y, target)


def _halo_rows(K):
    return SUBLANE if K - 1 <= SUBLANE else 32


def _prev_halo_spec(tm, H, C):
    return pl.BlockSpec((H, C), lambda i: (jnp.maximum(i * (tm // H) - 1, 0), 0))


def _next_halo_spec(tm, H, C, L):
    return pl.BlockSpec((H, C), lambda i: (jnp.minimum((i + 1) * (tm // H), L // H - 1), 0))


def _shift_down(ext, s):
    return ext if s == 0 else pltpu.roll(ext, s, axis=0)


def _shift_up(ext, s):
    return ext if s == 0 else pltpu.roll(ext, ext.shape[0] - s, axis=0)


def _causal_conv(ext, H, w_ref, K):
    acc = None
    for k in range(K):
        term = _shift_down(ext, K - 1 - k)[H:, :] * w_ref[k:k + 1, :]
        acc = term if acc is None else acc + term
    return acc


def _anticausal_conv(ext, tm, w_ref, K):
    acc = None
    for k in range(K):
        term = _shift_up(ext, K - 1 - k)[:tm, :] * w_ref[k:k + 1, :]
        acc = term if acc is None else acc + term
    return acc


def _tap_grads(dw_ref, d_cur, x_ext, H, K):
    for k in range(K):
        dw_ref[k:k + 1, :] += jnp.sum(d_cur * _shift_down(x_ext, K - 1 - k)[H:, :], axis=0, keepdims=True)


def _pad_taps(w, K):
    return jnp.pad(w, ((0, _halo_rows(K) - K), (0, 0)))


def _conv_act_fwd(x, w, b, *, K, act, name, out_dtype, tm_pref=256):
    L, C = x.shape
    H = _halo_rows(K)
    tm = _pick(L, tm_pref)
    Co = C if act == "silu" else C // 2

    def body(h_ref, x_ref, w_ref, b_ref, o_ref):
        i = pl.program_id(0)
        halo = jnp.where(i > 0, h_ref[...], 0.0)
        ext = jnp.concatenate([halo, x_ref[...]], axis=0)
        u = _causal_conv(ext, H, w_ref, K) + b_ref[...]
        if act == "silu":
            o_ref[...] = _silu(u).astype(out_dtype)
        else:
            o_ref[...] = (_silu(u[:, :Co]) * u[:, Co:]).astype(out_dtype)

    return pl.pallas_call(body, name=name, out_shape=S((L, Co), out_dtype), grid=(L // tm,),
                          in_specs=[_prev_halo_spec(tm, H, C), _rows(tm, C), _const((H, C)), _const((1, C))],
                          out_specs=_rows(tm, Co), compiler_params=_cp("parallel"))(x, x, w, b)


def _conv_act_bwd(x, dparts, w, b, *, K, act, name, tm_pref=256):
    L, C = x.shape
    H = _halo_rows(K)
    tm = _pick(L, tm_pref)
    Co = C if act == "silu" else C // 2
    nparts = len(dparts)

    def body(h_ref, x_ref, w_ref, b_ref, *rest):
        d_refs, (du_ref, dw_ref, db_ref) = rest[:nparts], rest[nparts:]
        i = pl.program_id(0)

        @pl.when(i == 0)
        def _():
            dw_ref[...] = jnp.zeros_like(dw_ref)
            db_ref[...] = jnp.zeros_like(db_ref)

        halo = jnp.where(i > 0, h_ref[...], 0.0)
        ext = jnp.concatenate([halo, x_ref[...]], axis=0)
        u = _causal_conv(ext, H, w_ref, K) + b_ref[...]
        d = [r[...].astype(f32) for r in d_refs]
        d = d[0] if nparts == 1 else jnp.concatenate(d, axis=1)
        if act == "silu":
            du = d * _dsilu(u)
        else:
            g, v = u[:, :Co], u[:, Co:]
            du = jnp.concatenate([d * v * _dsilu(g), d * _silu(g)], axis=1)
        du_ref[...] = du
        db_ref[...] += jnp.sum(du, axis=0, keepdims=True)
        _tap_grads(dw_ref, du, ext, H, K)

    in_specs = [_prev_halo_spec(tm, H, C), _rows(tm, C), _const((H, C)), _const((1, C))]
    in_specs += [_rows(tm, p.shape[1]) for p in dparts]
    return pl.pallas_call(body, name=name, out_shape=(S((L, C), f32), S((H, C), f32), S((1, C), f32)), grid=(L // tm,),
                          in_specs=in_specs, out_specs=(_rows(tm, C), _const((H, C)), _const((1, C))),
                          compiler_params=_cp("arbitrary"))(x, x, w, b, *dparts)


def _conv_transpose(d, w, *, K, name, tm_pref=256):
    L, C = d.shape
    H = _halo_rows(K)
    tm = _pick(L, tm_pref)
    nb = L // tm

    def body(d_ref, h_ref, w_ref, o_ref):
        i = pl.program_id(0)
        halo = jnp.where(i < nb - 1, h_ref[...], 0.0)
        ext = jnp.concatenate([d_ref[...], halo], axis=0)
        o_ref[...] = _anticausal_conv(ext, tm, w_ref, K).astype(bf16)

    return pl.pallas_call(body, name=name, out_shape=S((L, C), bf16), grid=(nb,),
                          in_specs=[_rows(tm, C), _next_halo_spec(tm, H, C, L), _const((H, C))],
                          out_specs=_rows(tm, C), compiler_params=_cp("parallel"))(d, d, w)


def _cf_fwd(u, dw_w, dw_b, ln_g, ln_b, *, name):
    L, C2 = u.shape
    C = C2 // 2
    K, H = CF_KERNEL, _halo_rows(CF_KERNEL)
    tm = _pick(L, 256)

    def body(h_ref, u_ref, w_ref, b_ref, g_ref, lb_ref, c_ref, s_ref):
        i = pl.program_id(0)
        halo = jnp.where(i > 0, h_ref[...], 0.0)
        ext = jnp.concatenate([halo, u_ref[...]], axis=0)
        glu = ext[:, :C] * _sigmoid(ext[:, C:])
        c = _causal_conv(glu, H, w_ref, K) + b_ref[...]
        c_ref[...] = c
        mu = jnp.mean(c, axis=-1, keepdims=True)
        xc = c - mu
        var = jnp.mean(xc * xc, axis=-1, keepdims=True)
        ln = xc * lax.rsqrt(var + EPS) * g_ref[...] + lb_ref[...]
        s_ref[...] = _silu(ln).astype(bf16)

    return pl.pallas_call(body, name=name, out_shape=(S((L, C), f32), S((L, C), bf16)), grid=(L // tm,),
                          in_specs=[_prev_halo_spec(tm, H, C2), _rows(tm, C2), _const((H, C)), _const((1, C)),
                                    _const((1, C)), _const((1, C))],
                          out_specs=(_rows(tm, C), _rows(tm, C)), compiler_params=_cp("parallel"))(u, u, dw_w, dw_b, ln_g, ln_b)


def _cf_ln_bwd(c, ln_g, ln_b, ds, *, name):
    L, C = c.shape
    tm = _pick(L, 512)

    def body(c_ref, g_ref, lb_ref, ds_ref, dc_ref, dg_ref, db_ref):
        i = pl.program_id(0)

        @pl.when(i == 0)
        def _():
            dg_ref[...] = jnp.zeros_like(dg_ref)
            db_ref[...] = jnp.zeros_like(db_ref)

        c = c_ref[...]
        mu = jnp.mean(c, axis=-1, keepdims=True)
        xc = c - mu
        r = lax.rsqrt(jnp.mean(xc * xc, axis=-1, keepdims=True) + EPS)
        xh = xc * r
        ln = xh * g_ref[...] + lb_ref[...]
        dln = ds_ref[...].astype(f32) * _dsilu(ln)
        dg_ref[...] += jnp.sum(dln * xh, axis=0, keepdims=True)
        db_ref[...] += jnp.sum(dln, axis=0, keepdims=True)
        dxh = dln * g_ref[...]
        dc_ref[...] = r * (dxh - jnp.mean(dxh, axis=-1, keepdims=True) - xh * jnp.mean(dxh * xh, axis=-1, keepdims=True))

    return pl.pallas_call(body, name=name, out_shape=(S((L, C), f32), S((1, C), f32), S((1, C), f32)), grid=(L // tm,),
                          in_specs=[_rows(tm, C), _const((1, C)), _const((1, C)), _rows(tm, C)],
                          out_specs=(_rows(tm, C), _const((1, C)), _const((1, C))),
                          compiler_params=_cp("arbitrary"))(c, ln_g, ln_b, ds)


def _cf_glu_bwd(u, dc, dw_w, *, name):
    L, C2 = u.shape
    C = C2 // 2
    K, H = CF_KERNEL, _halo_rows(CF_KERNEL)
    tm = _pick(L, 256)
    nb = L // tm

    def body(uh_ref, u_ref, dc_ref, dch_ref, w_ref, du_ref, dw_ref, db_ref, dus_ref):
        i = pl.program_id(0)

        @pl.when(i == 0)
        def _():
            dw_ref[...] = jnp.zeros_like(dw_ref)
            db_ref[...] = jnp.zeros_like(db_ref)
            dus_ref[...] = jnp.zeros_like(dus_ref)

        halo = jnp.where(i > 0, uh_ref[...], 0.0)
        ext = jnp.concatenate([halo, u_ref[...]], axis=0)
        sg = _sigmoid(ext[:, C:])
        glu = ext[:, :C] * sg
        dc = dc_ref[...]
        dnext = jnp.where(i < nb - 1, dch_ref[...], 0.0)
        dglu = _anticausal_conv(jnp.concatenate([dc, dnext], axis=0), tm, w_ref, K)
        a_cur, sg_cur = ext[H:, :C], sg[H:, :]
        du = jnp.concatenate([dglu * sg_cur, dglu * a_cur * sg_cur * (1.0 - sg_cur)], axis=1)
        du_ref[...] = du.astype(bf16)
        dus_ref[...] += jnp.sum(du, axis=0, keepdims=True)
        db_ref[...] += jnp.sum(dc, axis=0, keepdims=True)
        _tap_grads(dw_ref, dc, glu, H, K)

    return pl.pallas_call(body, name=name,
                          out_shape=(S((L, C2), bf16), S((H, C), f32), S((1, C), f32), S((1, C2), f32)), grid=(nb,),
                          in_specs=[_prev_halo_spec(tm, H, C2), _rows(tm, C2), _rows(tm, C), _next_halo_spec(tm, H, C, L),
                                    _const((H, C))],
                          out_specs=(_rows(tm, C2), _const((H, C)), _const((1, C)), _const((1, C2))),
                          compiler_params=_cp("arbitrary"))(u, u, dc, dc, dw_w)


def _gated_norm_fwd(y, z, g, *, name):
    L, C = y.shape
    tm = _pick(L, 256)

    def body(y_ref, z_ref, g_ref, o_ref):
        o_ref[...] = _rms_val(y_ref[...] * _silu(z_ref[...]), g_ref[...]).astype(bf16)

    return pl.pallas_call(body, name=name, out_shape=S((L, C), bf16), grid=(L // tm,),
                          in_specs=[_rows(tm, C), _rows(tm, C), _const((1, C))], out_specs=_rows(tm, C),
                          compiler_params=_cp("parallel"))(y, z, g)


def _gated_norm_bwd(y, z, g, dyn, *, name):
    L, C = y.shape
    tm = _pick(L, 256)

    def body(y_ref, z_ref, g_ref, d_ref, dy_ref, dz_ref, dg_ref):
        i = pl.program_id(0)

        @pl.when(i == 0)
        def _():
            dg_ref[...] = jnp.zeros_like(dg_ref)

        y, z = y_ref[...], z_ref[...]
        sz = _silu(z)
        du, dg = _rms_bwd_val(y * sz, g_ref[...], d_ref[...].astype(f32))
        dg_ref[...] += dg
        dy_ref[...] = du * sz
        dz_ref[...] = (du * y * _dsilu(z)).astype(bf16)

    return pl.pallas_call(body, name=name, out_shape=(S((L, C), f32), S((L, C), bf16), S((1, C), f32)), grid=(L // tm,),
                          in_specs=[_rows(tm, C), _rows(tm, C), _const((1, C)), _rows(tm, C)],
                          out_specs=(_rows(tm, C), _rows(tm, C), _const((1, C))),
                          compiler_params=_cp("arbitrary"))(y, z, g, dyn)


_XA_SCALE = XA_HEAD_DIM ** -0.5


def _attn_fwd(q, kv, *, name):
    L, C = q.shape
    tm = _pick(L, 512)
    Dh = XA_HEAD_DIM

    def body(q_ref, kv_ref, o_ref):
        for h in range(XA_HEADS):
            qh = q_ref[:, h * Dh:(h + 1) * Dh]
            kh = kv_ref[:, h * Dh:(h + 1) * Dh]
            vh = kv_ref[:, C + h * Dh:C + (h + 1) * Dh]
            s = lax.dot_general(qh, kh, _DN["nt"], preferred_element_type=f32) * _XA_SCALE
            e = jnp.exp(s - jnp.max(s, axis=-1, keepdims=True))
            p = e / jnp.sum(e, axis=-1, keepdims=True)
            o_ref[:, h * Dh:(h + 1) * Dh] = jnp.dot(p.astype(bf16), vh, preferred_element_type=f32).astype(bf16)

    return pl.pallas_call(body, name=name, out_shape=S((L, C), bf16), grid=(L // tm,),
                          in_specs=[_rows(tm, C), _const((N_MEM, 2 * C))], out_specs=_rows(tm, C),
                          compiler_params=_cp("parallel"))(q, kv)


def _attn_bwd(q, kv, do, *, name):
    L, C = q.shape
    tm = _pick(L, 512)
    Dh = XA_HEAD_DIM

    def body(q_ref, kv_ref, do_ref, dq_ref, dkv_ref):
        i = pl.program_id(0)

        @pl.when(i == 0)
        def _():
            dkv_ref[...] = jnp.zeros_like(dkv_ref)

        for h in range(XA_HEADS):
            qh = q_ref[:, h * Dh:(h + 1) * Dh]
            kh = kv_ref[:, h * Dh:(h + 1) * Dh]
            vh = kv_ref[:, C + h * Dh:C + (h + 1) * Dh]
            doh = do_ref[:, h * Dh:(h + 1) * Dh]
            s = lax.dot_general(qh, kh, _DN["nt"], preferred_element_type=f32) * _XA_SCALE
            e = jnp.exp(s - jnp.max(s, axis=-1, keepdims=True))
            p = e / jnp.sum(e, axis=-1, keepdims=True)
            pb = p.astype(bf16)
            dkv_ref[:, C + h * Dh:C + (h + 1) * Dh] += lax.dot_general(pb, doh, _DN["tn"], preferred_element_type=f32)
            dp = lax.dot_general(doh, vh, _DN["nt"], preferred_element_type=f32)
            ds = (p * (dp - jnp.sum(dp * p, axis=-1, keepdims=True)) * _XA_SCALE).astype(bf16)
            dq_ref[:, h * Dh:(h + 1) * Dh] = jnp.dot(ds, kh, preferred_element_type=f32).astype(bf16)
            dkv_ref[:, h * Dh:(h + 1) * Dh] += lax.dot_general(ds, qh, _DN["tn"], preferred_element_type=f32)

    return pl.pallas_call(body, name=name, out_shape=(S((L, C), bf16), S((N_MEM, 2 * C), f32)), grid=(L // tm,),
                          in_specs=[_rows(tm, C), _const((N_MEM, 2 * C)), _rows(tm, C)],
                          out_specs=(_rows(tm, C), _const((N_MEM, 2 * C))),
                          compiler_params=_cp("arbitrary"))(q, kv, do)


Q = CHUNK
PAIRS = HEADS_PER_GROUP // 2
GW = HEADS_PER_GROUP * HEAD_DIM


def _split3(x):
    x1 = x.astype(bf16)
    r1 = x - x1.astype(f32)
    x2 = r1.astype(bf16)
    return x1, x2, (r1 - x2.astype(f32)).astype(bf16)


def _sel_right(x, sel, mode="nn"):
    return sum(lax.dot_general(p, sel, _DN[mode], preferred_element_type=f32) for p in _split3(x))


def _sel_left(sel, x):
    return sum(lax.dot_general(sel, p, _DN["nn"], preferred_element_type=f32) for p in _split3(x))


def _ssd_common(dt_ref, hp_ref):
    dt_pre = dt_ref[...] + hp_ref[0:1, :]
    dt = _softplus(dt_pre)
    A = -jnp.exp(hp_ref[1:2, :])
    a = dt * A
    row = lax.broadcasted_iota(jnp.int32, (Q, Q), 0)
    col = lax.broadcasted_iota(jnp.int32, (Q, Q), 1)
    tri = row >= col
    cs = _sel_left(tri.astype(bf16), a)
    T = cs[Q - 1:Q, :]
    return dict(dt_pre=dt_pre, dt=dt, A=A, cs=cs, csT=cs.T, T=T, ecs=jnp.exp(cs), eend=jnp.exp(T - cs), eT=jnp.exp(T),
                tri=tri, row=row, col=col)


def _pair_expand(v, jj, lo):
    return jnp.where(lo, v[:, 2 * jj:2 * jj + 1], v[:, 2 * jj + 1:2 * jj + 2])


def _decay(cm, h):
    seg = cm["cs"][:, h:h + 1] - cm["csT"][h:h + 1, :]
    return jnp.where(cm["tri"], jnp.exp(jnp.where(cm["tri"], seg, 0.0)), 0.0)


def _ssd_fwd(act, dtp, hp, *, name):
    L = act.shape[0]
    nc = L // Q

    def body(xs_ref, b_ref, c_ref, dt_ref, hp_ref, y_ref, hs_ref, h_scr):
        c = pl.program_id(1)

        @pl.when(c == 0)
        def _():
            h_scr[...] = jnp.zeros_like(h_scr)

        cm = _ssd_common(dt_ref, hp_ref)
        Bb, Cb = b_ref[...].astype(bf16), c_ref[...].astype(bf16)
        CB = lax.dot_general(Cb, Bb, _DN["nt"], preferred_element_type=f32)
        lo = lax.broadcasted_iota(jnp.int32, (Q, LANE), 1) < HEAD_DIM
        top = lax.broadcasted_iota(jnp.int32, (LANE, LANE), 0) < HEAD_DIM
        Drow = hp_ref[2:3, :]
        for jj in range(PAIRS):
            hA, hB = 2 * jj, 2 * jj + 1
            dtx, ecsx, eendx = (_pair_expand(cm[k], jj, lo) for k in ("dt", "ecs", "eend"))
            xs_p = xs_ref[:, jj * LANE:(jj + 1) * LANE]
            Xd = xs_p * dtx
            Y = None
            for h, Xm in ((hA, jnp.where(lo, Xd, 0.0)), (hB, jnp.where(lo, 0.0, Xd))):
                W = (CB * _decay(cm, h)).astype(bf16)
                t = jnp.dot(W, Xm.astype(bf16), preferred_element_type=f32)
                Y = t if Y is None else Y + t
            Hp = h_scr[jj]
            hs_ref[0, jj] = Hp
            Yoff = lax.dot_general(Cb, Hp.astype(bf16), _DN["nt"], preferred_element_type=f32) * ecsx
            Dx = jnp.where(lo[0:1, :], Drow[:, hA:hA + 1], Drow[:, hB:hB + 1])
            y_ref[:, jj * LANE:(jj + 1) * LANE] = Y + Yoff + xs_p * Dx
            Snew = lax.dot_general((Xd * eendx).astype(bf16), Bb, _DN["tn"], preferred_element_type=f32)
            eTx = jnp.where(top, cm["eT"][:, hA:hA + 1], cm["eT"][:, hB:hB + 1])
            h_scr[jj] = Hp * eTx + Snew

    return pl.pallas_call(
        body, name=name, out_shape=(S((L, D_INNER), f32), S((nc, N_SSM_HEADS // 2, LANE, D_STATE), f32)),
        grid=(N_GROUPS, nc),
        in_specs=[pl.BlockSpec((Q, GW), lambda g, c: (c, g)),
                  pl.BlockSpec((Q, D_STATE), lambda g, c: (c, D_INNER // D_STATE + g)),
                  pl.BlockSpec((Q, D_STATE), lambda g, c: (c, D_INNER // D_STATE + N_GROUPS + g)),
                  pl.BlockSpec((Q, LANE), lambda g, c: (c, g)),
                  pl.BlockSpec((SUBLANE, LANE), lambda g, c: (0, g))],
        out_specs=(pl.BlockSpec((Q, GW), lambda g, c: (c, g)),
                   pl.BlockSpec((1, PAIRS, LANE, D_STATE), lambda g, c: (c, g, 0, 0))),
        scratch_shapes=[pltpu.VMEM((PAIRS, LANE, D_STATE), f32)],
        compiler_params=_cp("arbitrary", "arbitrary"))(act, act, act, dtp, hp)


def _ssd_bwd(act, dtp, hp, dy, hs, *, name):
    L = act.shape[0]
    nc = L // Q

    def body(xs_ref, b_ref, c_ref, dt_ref, hp_ref, dy_ref, hs_ref, dxs_ref, db_ref, dc_ref, ddt_ref, dhp_ref, dh_scr):
        c = pl.program_id(1)

        @pl.when(c == 0)
        def _():
            dh_scr[...] = jnp.zeros_like(dh_scr)
            dhp_ref[...] = jnp.zeros_like(dhp_ref)

        cm = _ssd_common(dt_ref, hp_ref)
        Bb, Cb = b_ref[...].astype(bf16), c_ref[...].astype(bf16)
        CB = lax.dot_general(Cb, Bb, _DN["nt"], preferred_element_type=f32)
        lane = lax.broadcasted_iota(jnp.int32, (Q, LANE), 1)
        sub = lax.broadcasted_iota(jnp.int32, (LANE, LANE), 0)
        lo = lane < HEAD_DIM
        top = sub < HEAD_DIM
        Drow = hp_ref[2:3, :]
        zero = jnp.zeros((Q, LANE), f32)
        dcs, ddtx, dC, dB, dCB = zero, zero, zero, zero, jnp.zeros((Q, Q), f32)
        dD_row = jnp.zeros((1, LANE), f32)
        dT_row = jnp.zeros((1, LANE), f32)
        for jj in range(PAIRS):
            hA, hB = 2 * jj, 2 * jj + 1
            Pj = (lane == jnp.where(top, hA, hB)).astype(bf16)
            dtx, ecsx, eendx = (_pair_expand(cm[k], jj, lo) for k in ("dt", "ecs", "eend"))
            xs_p = xs_ref[:, jj * LANE:(jj + 1) * LANE]
            dY_p = dy_ref[:, jj * LANE:(jj + 1) * LANE]
            Xd = xs_p * dtx
            Xdb = Xd.astype(bf16)
            Hp, dHn = hs_ref[0, jj], dh_scr[jj]
            Hb, dHb = Hp.astype(bf16), dHn.astype(bf16)
            EdYb = (dY_p * ecsx).astype(bf16)
            YoffN = lax.dot_general(Cb, Hb, _DN["nt"], preferred_element_type=f32)
            dC = dC + jnp.dot(EdYb, Hb, preferred_element_type=f32)
            dH_off = lax.dot_general(EdYb, Cb, _DN["tn"], preferred_element_type=f32)
            R = lax.dot_general(Bb, dHb, _DN["nt"], preferred_element_type=f32)
            Xe = Xd * eendx
            dB = dB + jnp.dot(Xe.astype(bf16), dHb, preferred_element_type=f32)
            dXd = R * eendx
            V2 = _sel_right(R * Xe, Pj)
            dcs = dcs + _sel_right(dY_p * YoffN * ecsx, Pj) - V2
            dT_row = dT_row + jnp.sum(V2, axis=0, keepdims=True) \
                + jnp.sum(_sel_right(dHn * Hp, Pj, "tn"), axis=0, keepdims=True) * cm["eT"]
            for h, keep in ((hA, lo), (hB, jnp.logical_not(lo))):
                M = _decay(cm, h)
                Wf = CB * M
                dYm = jnp.where(keep, dY_p, 0.0).astype(bf16)
                dW = lax.dot_general(dYm, Xdb, _DN["nt"], preferred_element_type=f32)
                dXd = dXd + lax.dot_general(Wf.astype(bf16), dYm, _DN["tn"], preferred_element_type=f32)
                Z = dW * Wf
                onesh = (lane == h).astype(bf16)
                dcs = dcs + _sel_right(Z, onesh) - _sel_right(Z, onesh, "tn")
                dCB = dCB + dW * M
            Dx = jnp.where(lo[0:1, :], Drow[:, hA:hA + 1], Drow[:, hB:hB + 1])
            dxs_ref[:, jj * LANE:(jj + 1) * LANE] = dXd * dtx + dY_p * Dx
            ddtx = ddtx + _sel_right(dXd * xs_p, Pj)
            dD_row = dD_row + jnp.sum(_sel_right(dY_p * xs_p, Pj), axis=0, keepdims=True)
            eTx = jnp.where(top, cm["eT"][:, hA:hA + 1], cm["eT"][:, hB:hB + 1])
            dh_scr[jj] = dHn * eTx + dH_off
        dCBb = dCB.astype(bf16)
        dc_ref[...] = dC + jnp.dot(dCBb, Bb, preferred_element_type=f32)
        db_ref[...] = dB + lax.dot_general(dCBb, Cb, _DN["tn"], preferred_element_type=f32)
        dcs = dcs + jnp.where(lax.broadcasted_iota(jnp.int32, (Q, LANE), 0) == Q - 1, dT_row, 0.0)
        da = _sel_left((cm["row"] <= cm["col"]).astype(bf16), dcs)
        ddt_pre = (da * cm["A"] + ddtx) * _sigmoid(cm["dt_pre"])
        ddt_ref[...] = ddt_pre
        r8 = lax.broadcasted_iota(jnp.int32, (SUBLANE, LANE), 0)
        dhp_ref[...] += jnp.where(r8 == 0, jnp.sum(ddt_pre, axis=0, keepdims=True),
                                  jnp.where(r8 == 1, jnp.sum(da * cm["dt"], axis=0, keepdims=True) * cm["A"],
                                            jnp.where(r8 == 2, dD_row, 0.0)))

    rev = lambda c: nc - 1 - c
    return pl.pallas_call(
        body, name=name,
        out_shape=(S((L, D_INNER), f32), S((L, N_GROUPS * D_STATE), f32), S((L, N_GROUPS * D_STATE), f32),
                   S((L, N_GROUPS * LANE), f32), S((SUBLANE, N_GROUPS * LANE), f32)),
        grid=(N_GROUPS, nc),
        in_specs=[pl.BlockSpec((Q, GW), lambda g, c: (rev(c), g)),
                  pl.BlockSpec((Q, D_STATE), lambda g, c: (rev(c), D_INNER // D_STATE + g)),
                  pl.BlockSpec((Q, D_STATE), lambda g, c: (rev(c), D_INNER // D_STATE + N_GROUPS + g)),
                  pl.BlockSpec((Q, LANE), lambda g, c: (rev(c), g)),
                  pl.BlockSpec((SUBLANE, LANE), lambda g, c: (0, g)),
                  pl.BlockSpec((Q, GW), lambda g, c: (rev(c), g)),
                  pl.BlockSpec((1, PAIRS, LANE, D_STATE), lambda g, c: (rev(c), g, 0, 0))],
        out_specs=(pl.BlockSpec((Q, GW), lambda g, c: (rev(c), g)),
                   pl.BlockSpec((Q, D_STATE), lambda g, c: (rev(c), g)),
                   pl.BlockSpec((Q, D_STATE), lambda g, c: (rev(c), g)),
                   pl.BlockSpec((Q, LANE), lambda g, c: (rev(c), g)),
                   pl.BlockSpec((SUBLANE, LANE), lambda g, c: (0, g))),
        scratch_shapes=[pltpu.VMEM((PAIRS, LANE, D_STATE), f32)],
        compiler_params=_cp("arbitrary", "arbitrary"))(act, act, act, dtp, hp, dy, hs)


def _group_pad_cols(w):
    lead = w.shape[:-1]
    w = w.reshape(lead + (N_GROUPS, HEADS_PER_GROUP))
    w = jnp.pad(w, [(0, 0)] * len(lead) + [(0, 0), (0, LANE - HEADS_PER_GROUP)])
    return w.reshape(lead + (N_GROUPS * LANE,))


def _group_unpad_cols(w):
    lead = w.shape[:-1]
    return w.reshape(lead + (N_GROUPS, LANE))[..., :HEADS_PER_GROUP].reshape(lead + (N_SSM_HEADS,))


def _row(v):
    return v.reshape(1, -1)


ROW_SHARDED = ('ssm_out_w', 'cf_pw2_w', 'xa_q_w', 'xa_o_w', 'ffn_out_w')
COL_SHARDED = ('cf_pw1_w', 'xa_kv_w', 'ffn_in_w')


def _device_step(x, mem, target, W):
    ng = W['norm_g']
    lw = []
    for i in range(DEPTH):
        j = i // 2
        p = {}
        if i % 2 == 0:
            win = jnp.concatenate([W['ssm_in_w'][j, s] for s in range(N_CHIPS)], axis=1)
            p['wz'], p['wx'] = win[:, :D_INNER], win[:, D_INNER:D_INNER + CONV_DIM]
            p['wdt'] = _group_pad_cols(win[:, D_INNER + CONV_DIM:])
            p['cw'] = _pad_taps(W['ssm_conv_w'][j], SSM_CONV)
            p['cb'] = _row(W['ssm_conv_b'][j])
            hp = jnp.stack([_group_pad_cols(W['ssm_dt_bias'][j]), _group_pad_cols(W['ssm_A_log'][j]),
                            _group_pad_cols(W['ssm_D'][j])])
            p['hp'] = jnp.pad(hp, ((0, SUBLANE - 3), (0, 0)))
            p['sng'] = _row(W['ssm_norm_g'][j])
        else:
            p['pw1b'] = _row(W['cf_pw1_b'][j])
            p['dww'], p['dwb'] = _pad_taps(W['cf_dw_w'][j], CF_KERNEL), _row(W['cf_dw_b'][j])
            p['lng'], p['lnb'] = _row(W['cf_ln_g'][j]), _row(W['cf_ln_b'][j])
            p['pw2b'] = _row(W['cf_pw2_b'][j])
        p['memg'] = _row(W['xa_mem_g'][i])
        p['fcw'], p['fcb'] = _pad_taps(W['ffn_conv_w'][i], FFN_CONV), _row(W['ffn_conv_b'][i])
        p['g'] = [_row(ng[i, s]) for s in range(6)]
        lw.append(p)

    def wmm(a, wname, layer, mode, **kw):
        return _mm(a, W[wname], mode, b_layer=layer, b_shards=wname in COL_SHARDED, **kw)

    saved = []
    X = x
    h = _rmsnorm_fwd(X, lw[0]['g'][0], name="norm_in")
    for i in range(DEPTH):
        p, sv = lw[i], {}
        sv['X0'], sv['h'] = X, h
        if i % 2 == 0:
            z = _mm(h, p['wz'], "nn", name="ssm_z")
            xbc = _mm(h, p['wx'], "nn", name="ssm_xbc")
            dtp = _mm(h, p['wdt'], "nn", name="ssm_dt")
            act = _conv_act_fwd(xbc, p['cw'], p['cb'], K=SSM_CONV, act="silu", name="ssm_conv_fwd", out_dtype=f32)
            y, hs = _ssd_fwd(act, dtp, p['hp'], name="ssd_fwd")
            yn = _gated_norm_fwd(y, z, p['sng'], name="ssm_gnorm_fwd")
            mix = wmm(yn, 'ssm_out_w', i // 2, "nn", name="ssm_out")
            sv.update(z=z, xbc=xbc, dtp=dtp, act=act, y=y, hs=hs, yn=yn)
        else:
            u = wmm(h, 'cf_pw1_w', i // 2, "nn", name="cf_pw1", bias=p['pw1b'])
            c, s = _cf_fwd(u, p['dww'], p['dwb'], p['lng'], p['lnb'], name="cf_conv_fwd")
            mix = wmm(s, 'cf_pw2_w', i // 2, "nn", name="cf_pw2", bias=p['pw2b'])
            sv.update(u=u, c=c, s=s)
        X1, h2 = _resid_norm_fwd(X, mix, p['g'][1], p['g'][2], name="resid_norm_a")
        q = wmm(h2, 'xa_q_w', i, "nn", name="xa_q", out_dtype=bf16)
        m = _rmsnorm_fwd(mem, p['memg'], name="xa_mem_norm")
        kv = wmm(m, 'xa_kv_w', i, "nn", name="xa_kv", out_dtype=bf16)
        o = _attn_fwd(q, kv, name="xa_attn_fwd")
        a = wmm(o, 'xa_o_w', i, "nn", name="xa_o")
        X2, h3 = _resid_norm_fwd(X1, a, p['g'][3], p['g'][4], name="resid_norm_b")
        u0 = wmm(h3, 'ffn_in_w', i, "nn", name="ffn_in")
        fact = _conv_act_fwd(u0, p['fcw'], p['fcb'], K=FFN_CONV, act="swiglu", name="ffn_conv_fwd", out_dtype=bf16)
        f = wmm(fact, 'ffn_out_w', i, "nn", name="ffn_out")
        g_next = lw[i + 1]['g'][0] if i + 1 < DEPTH else None
        X3, hn = _resid_norm_fwd(X2, f, p['g'][5], g_next, name="resid_norm_c" if g_next is not None else "resid_norm_last")
        sv.update(mix=mix, X1=X1, h2=h2, q=q, m=m, kv=kv, o=o, a=a, X2=X2, h3=h3, u0=u0, fact=fact, f=f)
        saved.append(sv)
        X, h = X3, hn

    sse, G = _loss_fwd_bwd(X, target, name="loss")

    small = [n for n in WEIGHT_NAMES if n not in MATMUL_WEIGHTS]
    gr = {n: [None] * W[n].shape[0] for n in small}
    gbig = {n: None for n in ROW_SHARDED + COL_SHARDED}
    g_in = [None] * (DEPTH // 2)

    def dwmm(a, d, wname, layer, *, name):
        kind = "cols" if wname in COL_SHARDED else "rows"
        gbig[wname] = _mm(a, d, "tn", name=name, out_dtype=bf16, out_into=(kind, gbig[wname], layer, W[wname].shape[0]))

    dng = [[None] * 6 for _ in range(DEPTH)]
    for i in reversed(range(DEPTH)):
        p, sv, j = lw[i], saved[i], i // 2
        df, dng[i][5], _ = _norm_bwd(sv['f'], p['g'][5], G, name="nb_f", out_dtype=bf16)
        dwmm(sv['fact'], df, 'ffn_out_w', i, name="ffn_out_dw")
        dfact = wmm(df, 'ffn_out_w', i, "nt", name="ffn_out_dx")
        du, dcw, dcb = _conv_act_bwd(sv['u0'], [dfact], p['fcw'], p['fcb'], K=FFN_CONV, act="swiglu", name="ffn_conv_bwd")
        gr['ffn_conv_w'][i], gr['ffn_conv_b'][i] = dcw[:FFN_CONV], dcb[0]
        du0 = _conv_transpose(du, p['fcw'], K=FFN_CONV, name="ffn_conv_bwd_x")
        dwmm(sv['h3'], du0, 'ffn_in_w', i, name="ffn_in_dw")
        dh3 = wmm(du0, 'ffn_in_w', i, "nt", name="ffn_in_dx")
        G, dng[i][4], _ = _norm_bwd(sv['X2'], p['g'][4], dh3, name="nb_x2", add=G)
        da, dng[i][3], _ = _norm_bwd(sv['a'], p['g'][3], G, name="nb_a", out_dtype=bf16)
        dwmm(sv['o'], da, 'xa_o_w', i, name="xa_o_dw")
        do = wmm(da, 'xa_o_w', i, "nt", name="xa_o_dx", out_dtype=bf16)
        dq, dkv = _attn_bwd(sv['q'], sv['kv'], do, name="xa_attn_bwd")
        dwmm(sv['h2'], dq, 'xa_q_w', i, name="xa_q_dw")
        dh2 = wmm(dq, 'xa_q_w', i, "nt", name="xa_q_dx")
        dwmm(sv['m'], dkv, 'xa_kv_w', i, name="xa_kv_dw")
        dm = wmm(dkv, 'xa_kv_w', i, "nt", name="xa_kv_dx")
        _, dmg, _ = _norm_bwd(mem, p['memg'], dm, name="nb_mem")
        gr['xa_mem_g'][i] = dmg[0]
        G, dng[i][2], _ = _norm_bwd(sv['X1'], p['g'][2], dh2, name="nb_x1", add=G)
        dmix, dng[i][1], dmix_sum = _norm_bwd(sv['mix'], p['g'][1], G, name="nb_mix", out_dtype=bf16)
        if i % 2 == 0:
            dwmm(sv['yn'], dmix, 'ssm_out_w', j, name="ssm_out_dw")
            dyn = wmm(dmix, 'ssm_out_w', j, "nt", name="ssm_out_dx")
            dy, dz, dsng = _gated_norm_bwd(sv['y'], sv['z'], p['sng'], dyn, name="ssm_gnorm_bwd")
            gr['ssm_norm_g'][j] = dsng[0]
            dxs, dB, dC, ddtp, dhp = _ssd_bwd(sv['act'], sv['dtp'], p['hp'], dy, sv['hs'], name="ssd_bwd")
            gr['ssm_dt_bias'][j], gr['ssm_A_log'][j], gr['ssm_D'][j] = (_group_unpad_cols(dhp[r]) for r in range(3))
            dpre, dcw, dcb = _conv_act_bwd(sv['xbc'], [dxs, dB, dC], p['cw'], p['cb'], K=SSM_CONV, act="silu",
                                           name="ssm_conv_bwd")
            gr['ssm_conv_w'][j], gr['ssm_conv_b'][j] = dcw[:SSM_CONV], dcb[0]
            dxbc = _conv_transpose(dpre, p['cw'], K=SSM_CONV, name="ssm_conv_bwd_x")
            hh = sv['h']
            dwz = _mm(hh, dz, "tn", name="ssm_z_dw", out_dtype=bf16)
            dwx = _mm(hh, dxbc, "tn", name="ssm_xbc_dw", out_dtype=bf16)
            dwdt = _mm(hh, ddtp, "tn", name="ssm_dt_dw", out_dtype=bf16)
            din = jnp.concatenate([dwz, dwx, _group_unpad_cols(dwdt)], axis=1)
            g_in[j] = jnp.stack(jnp.split(din, N_CHIPS, axis=1))
            dh = _mm(dz, p['wz'], "nt", name="ssm_z_dx")
            dh = _mm(dxbc, p['wx'], "nt", name="ssm_xbc_dx", add=dh)
            dh = _mm(ddtp, p['wdt'], "nt", name="ssm_dt_dx", add=dh)
        else:
            dwmm(sv['s'], dmix, 'cf_pw2_w', j, name="cf_pw2_dw")
            gr['cf_pw2_b'][j] = dmix_sum[0]
            ds = wmm(dmix, 'cf_pw2_w', j, "nt", name="cf_pw2_dx")
            dc, dlg, dlb = _cf_ln_bwd(sv['c'], p['lng'], p['lnb'], ds, name="cf_ln_bwd")
            gr['cf_ln_g'][j], gr['cf_ln_b'][j] = dlg[0], dlb[0]
            du, ddw, ddb, dus = _cf_glu_bwd(sv['u'], dc, p['dww'], name="cf_glu_bwd")
            gr['cf_dw_w'][j], gr['cf_dw_b'][j], gr['cf_pw1_b'][j] = ddw[:CF_KERNEL], ddb[0], dus[0]
            dwmm(sv['h'], du, 'cf_pw1_w', j, name="cf_pw1_dw")
            dh = wmm(du, 'cf_pw1_w', j, "nt", name="cf_pw1_dx")
        G, dng[i][0], _ = _norm_bwd(sv['X0'], p['g'][0], dh, name="nb_x0", add=G)
    gr['norm_g'] = [jnp.concatenate(dng[i], axis=0) for i in range(DEPTH)]
    gsmall = {n: jnp.stack(gr[n]) for n in small}
    for n in ROW_SHARDED:
        layers, rows, cols = gbig[n].shape
        gbig[n] = gbig[n].reshape(layers, N_CHIPS, rows // N_CHIPS, cols)
    gbig['ssm_in_w'] = jnp.stack(g_in)
    return sse, G, gbig, gsmall


MESH = pl.DeviceIdType.MESH
HBM_SPEC = pl.BlockSpec(memory_space=pltpu.HBM)


def _chip_peers(x, y):
    return [(1 - x, y), (x, 1 - y), (1 - x, 1 - y)]


def _all_gather_chips(buf, *, name):
    R, C = buf.shape

    def body(in_ref, out_ref, send_sems, recv_sems, local_sem):
        x, y, c = lax.axis_index("x"), lax.axis_index("y"), lax.axis_index("c")
        me = 2 * x + y
        mine = pltpu.make_async_copy(in_ref, out_ref.at[me], local_sem)
        mine.start()
        peers = _chip_peers(x, y)
        sends = []
        for k, (px, py) in enumerate(peers):
            cp = pltpu.make_async_remote_copy(src_ref=in_ref, dst_ref=out_ref.at[me], send_sem=send_sems.at[k],
                                              recv_sem=recv_sems.at[k], device_id=(px, py, c), device_id_type=MESH)
            cp.start()
            sends.append(cp)
        for k, (px, py) in enumerate(peers):
            pltpu.make_async_remote_copy(src_ref=in_ref, dst_ref=out_ref.at[2 * px + py], send_sem=send_sems.at[k],
                                         recv_sem=recv_sems.at[k], device_id=(px, py, c), device_id_type=MESH).wait_recv()
        for cp in sends:
            cp.wait_send()
        mine.wait()

    return pl.pallas_call(body, name=name, out_shape=S((N_CHIPS, R, C), buf.dtype), in_specs=[HBM_SPEC], out_specs=HBM_SPEC,
                          scratch_shapes=[pltpu.SemaphoreType.DMA((3,)), pltpu.SemaphoreType.DMA((3,)),
                                          pltpu.SemaphoreType.DMA(())])(buf)


def _remote(src, dst, send_sem, recv_sem, device):
    return pltpu.make_async_remote_copy(src_ref=src, dst_ref=dst, send_sem=send_sem, recv_sem=recv_sem,
                                        device_id=device, device_id_type=MESH)


def _gather_matmul_weights(shards, *, name):
    n = len(shards)

    def body(*refs):
        ins, outs = refs[:n], refs[n:2 * n]
        send, recv, fsend, frecv, lsem = refs[2 * n:]
        x, y, c = lax.axis_index("x"), lax.axis_index("y"), lax.axis_index("c")
        me, sib = 2 * x + y, (x, y, 1 - c)
        peers = _chip_peers(x, y)
        started, local = [], []
        for w in range(n):
            cp = pltpu.make_async_copy(ins[w], outs[w].at[:, me], lsem.at[w])
            cp.start()
            local.append(cp)
            for k, (px, py) in enumerate(peers):
                cp = _remote(ins[w].at[:, c], outs[w].at[:, me, c], send.at[w, k], recv.at[w, k], (px, py, c))
                cp.start()
                started.append(cp)
        for w in range(n):
            for k, (px, py) in enumerate(peers):
                landed = outs[w].at[:, 2 * px + py, c]
                _remote(ins[w].at[:, c], landed, send.at[w, k], recv.at[w, k], (px, py, c)).wait_recv()
                cp = _remote(landed, landed, fsend.at[w, k], frecv.at[w, k], sib)
                cp.start()
                started.append(cp)
        for w in range(n):
            for k, (px, py) in enumerate(peers):
                _remote(ins[w].at[:, c], outs[w].at[:, 2 * px + py, 1 - c], fsend.at[w, k], frecv.at[w, k], sib).wait_recv()
        for cp in started:
            cp.wait_send()
        for cp in local:
            cp.wait()

    out_shape = tuple(S((s.shape[0], N_CHIPS) + s.shape[1:], s.dtype) for s in shards)
    sems = [pltpu.SemaphoreType.DMA((n, 3)) for _ in range(4)] + [pltpu.SemaphoreType.DMA((n,))]
    return pl.pallas_call(body, name=name, out_shape=out_shape, in_specs=[HBM_SPEC] * n, out_specs=(HBM_SPEC,) * n,
                          scratch_shapes=sems)(*shards)


def _exchange_matmul_grads(grads, *, name):
    n = len(grads)

    def body(*refs):
        ins, own, sibb = refs[:n], refs[n:2 * n], refs[2 * n:3 * n]
        send, recv, dsend, drecv, lsem = refs[3 * n:]
        x, y, c = lax.axis_index("x"), lax.axis_index("y"), lax.axis_index("c")
        me, sib = 2 * x + y, (x, y, 1 - c)
        peers = _chip_peers(x, y)
        started, local = [], []
        for w in range(n):
            cp = pltpu.make_async_copy(ins[w].at[:, me], own[w].at[me], lsem.at[w])
            cp.start()
            local.append(cp)
            cp = _remote(ins[w].at[:, me], sibb[w].at[me], dsend.at[w, 3], drecv.at[w, 3], sib)
            cp.start()
            started.append(cp)
            for k, (px, py) in enumerate(peers):
                cp = _remote(ins[w].at[:, 2 * px + py], own[w].at[me], send.at[w, k], recv.at[w, k], (px, py, c))
                cp.start()
                started.append(cp)
        for w in range(n):
            for k, (px, py) in enumerate(peers):
                pk = 2 * px + py
                _remote(ins[w].at[:, me], own[w].at[pk], send.at[w, k], recv.at[w, k], (px, py, c)).wait_recv()
                cp = _remote(own[w].at[pk], sibb[w].at[pk], dsend.at[w, k], drecv.at[w, k], sib)
                cp.start()
                started.append(cp)
        for w in range(n):
            _remote(ins[w].at[:, me], sibb[w].at[me], dsend.at[w, 3], drecv.at[w, 3], sib).wait_recv()
            for k, (px, py) in enumerate(peers):
                _remote(ins[w].at[:, me], sibb[w].at[2 * px + py], dsend.at[w, k], drecv.at[w, k], sib).wait_recv()
        for cp in started:
            cp.wait_send()
        for cp in local:
            cp.wait()

    slot_shapes = tuple(S((N_CHIPS, g.shape[0]) + g.shape[2:], g.dtype) for g in grads)
    sems = [pltpu.SemaphoreType.DMA((n, 3)), pltpu.SemaphoreType.DMA((n, 3)), pltpu.SemaphoreType.DMA((n, 4)),
            pltpu.SemaphoreType.DMA((n, 4)), pltpu.SemaphoreType.DMA((n,))]
    out = pl.pallas_call(body, name=name, out_shape=slot_shapes + slot_shapes, in_specs=[HBM_SPEC] * n,
                         out_specs=(HBM_SPEC,) * (2 * n), scratch_shapes=sems)(*grads)
    return out[:n], out[n:]


N_DEVICES = 8


def _allgather_devices(buf, *, name):
    R, C = buf.shape

    def body(in_ref, out_ref, send, recv, lsem):
        x, y, c = lax.axis_index("x"), lax.axis_index("y"), lax.axis_index("c")
        me = 4 * x + 2 * y + c
        mine = pltpu.make_async_copy(in_ref, out_ref.at[me], lsem)
        mine.start()
        flips = [(d >> 2 & 1, d >> 1 & 1, d & 1) for d in range(1, N_DEVICES)]
        peers = [(1 - x if fx else x, 1 - y if fy else y, 1 - c if fc else c) for fx, fy, fc in flips]
        sends = []
        for k, peer in enumerate(peers):
            cp = _remote(in_ref, out_ref.at[me], send.at[k], recv.at[k], peer)
            cp.start()
            sends.append(cp)
        for k, (px, py, pc) in enumerate(peers):
            _remote(in_ref, out_ref.at[4 * px + 2 * py + pc], send.at[k], recv.at[k], (px, py, pc)).wait_recv()
        for cp in sends:
            cp.wait_send()
        mine.wait()

    return pl.pallas_call(body, name=name, out_shape=S((N_DEVICES, R, C), buf.dtype), in_specs=[HBM_SPEC], out_specs=HBM_SPEC,
                          scratch_shapes=[pltpu.SemaphoreType.DMA((N_DEVICES - 1,)), pltpu.SemaphoreType.DMA((N_DEVICES - 1,)),
                                          pltpu.SemaphoreType.DMA(())])(buf)


def _sum_slots(buf, *, name):
    ns, R, C = buf.shape
    tr = _pick(R, 512)
    assert R % tr == 0

    def body(*refs):
        acc = refs[0][...]
        for r in refs[1:ns]:
            acc = acc + r[...]
        refs[ns][...] = acc

    specs = [pl.BlockSpec((None, tr, C), functools.partial(lambda s, i: (s, i, 0), s)) for s in range(ns)]
    return pl.pallas_call(body, name=name, out_shape=S((R, C), buf.dtype), grid=(R // tr,), in_specs=specs,
                          out_specs=pl.BlockSpec((tr, C), lambda i: (i, 0)), compiler_params=_cp("parallel"))(*([buf] * ns))


ADAMW_BLOCK_BYTES = 1 << 20


def _adamw(w, m, v, groups, *, name):
    shape = w.shape
    C = shape[-1]
    Rr = math.prod(shape[:-1])
    to2 = lambda t: t.reshape(Rr, C)
    tr = Rr
    if Rr * C * 4 > ADAMW_BLOCK_BYTES:
        tr = max(t for t in range(2 * SUBLANE, Rr + 1, 2 * SUBLANE) if Rr % t == 0 and t * C * 4 <= ADAMW_BLOCK_BYTES)
    c1 = 1.0 / (1.0 - ADAM_B1 ** ADAM_STEP)
    c2 = 1.0 / (1.0 - ADAM_B2 ** ADAM_STEP)
    spec = pl.BlockSpec((tr, C), lambda i: (i, 0))
    g_specs, g_args, sizes = [], [], []
    for grp in groups:
        sizes.append(len(grp))
        for term in grp:
            if isinstance(term, tuple):
                arr, slot = term
                g_specs.append(pl.BlockSpec((None, tr, C), functools.partial(lambda s, i: (s, i, 0), slot)))
                g_args.append(arr.reshape(arr.shape[0], Rr, C))
            else:
                g_specs.append(spec)
                g_args.append(to2(term))
    nterms = len(g_args)

    def body(w_ref, m_ref, v_ref, *rest):
        t_refs, (g_ref, d_ref, mo_ref, vo_ref) = rest[:nterms], rest[nterms:]
        g, pos = None, 0
        for size in sizes:
            part = None
            for r in t_refs[pos:pos + size]:
                t = r[...].astype(f32)
                part = t if part is None else part + t
            pos += size
            g = part if g is None else g + part
        mn = ADAM_B1 * m_ref[...] + (1.0 - ADAM_B1) * g
        vn = ADAM_B2 * v_ref[...] + (1.0 - ADAM_B2) * (g * g)
        g_ref[...] = g
        mo_ref[...] = mn
        vo_ref[...] = vn
        d_ref[...] = -ADAM_LR * ((mn * c1) / (jnp.sqrt(vn * c2) + ADAM_EPS) + ADAM_WD * w_ref[...])

    out = pl.pallas_call(body, name=name, out_shape=tuple(S((Rr, C), f32) for _ in range(4)), grid=(Rr // tr,),
                         in_specs=[spec] * 3 + g_specs, out_specs=(spec,) * 4,
                         compiler_params=_cp("parallel"))(to2(w), to2(m), to2(v), *g_args)
    return tuple(o.reshape(shape) for o in out)


def _pack_rows(parts, dtype):
    flat = jnp.concatenate([p.reshape(-1).astype(dtype) for p in parts])
    n = flat.shape[0]
    unit = PACK_COLS * 2 * SUBLANE
    padded = -(-n // unit) * unit
    return jnp.pad(flat, (0, padded - n)).reshape(padded // PACK_COLS, PACK_COLS)


def _unpack_rows(flat2d, shapes):
    flat = flat2d.reshape(-1)
    out, off = [], 0
    for shp in shapes:
        n = math.prod(shp)
        out.append(flat[off:off + n].reshape(shp))
        off += n
    return out


def _gather_weights(local, names, dtype, *, name):
    shapes = [local[n].shape for n in names]
    got = _all_gather_chips(_pack_rows([local[n] for n in names], dtype), name=name)
    per_chip = [_unpack_rows(got[s], shapes) for s in range(N_CHIPS)]
    return {n: jnp.concatenate([per_chip[s][k] for s in range(N_CHIPS)], axis=SHARD_AXIS[n]) for k, n in enumerate(names)}


def kernel(x, mem, norm_g, ssm_in_w, ssm_conv_w, ssm_conv_b, ssm_dt_bias, ssm_A_log, ssm_D, ssm_norm_g, ssm_out_w, cf_pw1_w, cf_pw1_b, cf_dw_w, cf_dw_b, cf_ln_g, cf_ln_b, cf_pw2_w, cf_pw2_b, xa_mem_g, xa_q_w, xa_kv_w, xa_o_w, ffn_in_w, ffn_conv_w, ffn_conv_b, ffn_out_w, loss_target, m_norm_g, m_ssm_in_w, m_ssm_conv_w, m_ssm_conv_b, m_ssm_dt_bias, m_ssm_A_log, m_ssm_D, m_ssm_norm_g, m_ssm_out_w, m_cf_pw1_w, m_cf_pw1_b, m_cf_dw_w, m_cf_dw_b, m_cf_ln_g, m_cf_ln_b, m_cf_pw2_w, m_cf_pw2_b, m_xa_mem_g, m_xa_q_w, m_xa_kv_w, m_xa_o_w, m_ffn_in_w, m_ffn_conv_w, m_ffn_conv_b, m_ffn_out_w, v_norm_g, v_ssm_in_w, v_ssm_conv_w, v_ssm_conv_b, v_ssm_dt_bias, v_ssm_A_log, v_ssm_D, v_ssm_norm_g, v_ssm_out_w, v_cf_pw1_w, v_cf_pw1_b, v_cf_dw_w, v_cf_dw_b, v_cf_ln_g, v_cf_ln_b, v_cf_pw2_w, v_cf_pw2_b, v_xa_mem_g, v_xa_q_w, v_xa_kv_w, v_xa_o_w, v_ffn_in_w, v_ffn_conv_w, v_ffn_conv_b, v_ffn_out_w):
    w_local = dict(zip(WEIGHT_NAMES, (norm_g, ssm_in_w, ssm_conv_w, ssm_conv_b, ssm_dt_bias, ssm_A_log, ssm_D, ssm_norm_g,
                                      ssm_out_w, cf_pw1_w, cf_pw1_b, cf_dw_w, cf_dw_b, cf_ln_g, cf_ln_b, cf_pw2_w, cf_pw2_b,
                                      xa_mem_g, xa_q_w, xa_kv_w, xa_o_w, ffn_in_w, ffn_conv_w, ffn_conv_b, ffn_out_w)))
    m_local = dict(zip(WEIGHT_NAMES, (m_norm_g, m_ssm_in_w, m_ssm_conv_w, m_ssm_conv_b, m_ssm_dt_bias, m_ssm_A_log, m_ssm_D,
                                      m_ssm_norm_g, m_ssm_out_w, m_cf_pw1_w, m_cf_pw1_b, m_cf_dw_w, m_cf_dw_b, m_cf_ln_g,
                                      m_cf_ln_b, m_cf_pw2_w, m_cf_pw2_b, m_xa_mem_g, m_xa_q_w, m_xa_kv_w, m_xa_o_w,
                                      m_ffn_in_w, m_ffn_conv_w, m_ffn_conv_b, m_ffn_out_w)))
    v_local = dict(zip(WEIGHT_NAMES, (v_norm_g, v_ssm_in_w, v_ssm_conv_w, v_ssm_conv_b, v_ssm_dt_bias, v_ssm_A_log, v_ssm_D,
                                      v_ssm_norm_g, v_ssm_out_w, v_cf_pw1_w, v_cf_pw1_b, v_cf_dw_w, v_cf_dw_b, v_cf_ln_g,
                                      v_cf_ln_b, v_cf_pw2_w, v_cf_pw2_b, v_xa_mem_g, v_xa_q_w, v_xa_kv_w, v_xa_o_w,
                                      v_ffn_in_w, v_ffn_conv_w, v_ffn_conv_b, v_ffn_out_w)))

    big = list(MATMUL_WEIGHTS)
    small = [n for n in WEIGHT_NAMES if n not in MATMUL_WEIGHTS]
    small_sharded = [n for n in small if SHARD_AXIS[n] is not None]
    W = {n: w_local[n] for n in small if SHARD_AXIS[n] is None}
    W.update(_gather_weights(w_local, small_sharded, f32, name="gather_small_weights"))
    halves = []
    for n in big:
        layers, rows, cols = w_local[n].shape
        halves.append(w_local[n].astype(bf16).reshape(layers, 2, rows // 2, cols))
    for n, got in zip(big, _gather_matmul_weights(halves, name="gather_matmul_weights")):
        layers, _, _, hr, cols = got.shape
        W[n] = got.reshape(layers, N_CHIPS * 2 * hr, cols) if n in ROW_SHARDED else got.reshape(layers, N_CHIPS, 2 * hr, cols)

    sse, gx, gbig, gsmall = _device_step(x[0], mem[0], loss_target[0], W)

    loss = lax.psum(0.5 * sse[0, 0] / D_MODEL, ("x", "y", "c"))

    own, sib = _exchange_matmul_grads([gbig[n] for n in big], name="exchange_matmul_grads")
    small_shapes = [gsmall[n].shape for n in small]
    slots = _allgather_devices(_pack_rows([gsmall[n] for n in small], f32), name="allgather_small_grads")
    gsum = dict(zip(small, _unpack_rows(_sum_slots(slots, name="sum_small_grads"), small_shapes)))
    chip = 2 * lax.axis_index("x") + lax.axis_index("y")

    res = {}
    for k, n in enumerate(big):
        groups = [[(own[k], s) for s in range(N_CHIPS)], [(sib[k], s) for s in range(N_CHIPS)]]
        res[n] = _adamw(w_local[n], m_local[n], v_local[n], groups, name="adamw_" + n)
    for n in small:
        g, ax = gsum[n], SHARD_AXIS[n]
        if ax is not None:
            width = w_local[n].shape[ax]
            g = lax.dynamic_slice_in_dim(g, chip * width, width, axis=ax)
        res[n] = _adamw(w_local[n], m_local[n], v_local[n], [[g]], name="adamw_" + n)
    return (loss, gx[None], *[res[n][0] for n in WEIGHT_NAMES], *[res[n][1] for n in WEIGHT_NAMES],
            *[res[n][2] for n in WEIGHT_NAMES], *[res[n][3] for n in WEIGHT_NAMES])
```

```python
import functools
import math

import jax
import jax.numpy as jnp
from jax import lax
from jax.experimental import pallas as pl
from jax.experimental.pallas import tpu as pltpu

f32 = jnp.float32
bf16 = jnp.bfloat16
S = jax.ShapeDtypeStruct

D_MODEL = 1024
DEPTH = 4
D_INNER = 2048
HEAD_DIM = 64
N_GROUPS = 4
HEADS_PER_GROUP = 8
N_SSM_HEADS = 32
D_STATE = 128
CHUNK = 128
SSM_CONV = 4
CONV_DIM = 3072
CF_KERNEL = 31
N_MEM = 256
XA_HEADS = 4
XA_HEAD_DIM = 256
D_FF = 2816
FFN_CONV = 3
EPS = 1e-6
ADAM_LR, ADAM_B1, ADAM_B2, ADAM_EPS, ADAM_WD, ADAM_STEP = 0.001, 0.9, 0.999, 1e-08, 0.01, 10

LANE = 128
SUBLANE = 8
VMEM_LIMIT = 56 * 1024 * 1024
N_CHIPS = 4
PACK_COLS = 1024

WEIGHT_NAMES = ['norm_g', 'ssm_in_w', 'ssm_conv_w', 'ssm_conv_b', 'ssm_dt_bias', 'ssm_A_log', 'ssm_D', 'ssm_norm_g',
                'ssm_out_w', 'cf_pw1_w', 'cf_pw1_b', 'cf_dw_w', 'cf_dw_b', 'cf_ln_g', 'cf_ln_b', 'cf_pw2_w', 'cf_pw2_b',
                'xa_mem_g', 'xa_q_w', 'xa_kv_w', 'xa_o_w', 'ffn_in_w', 'ffn_conv_w', 'ffn_conv_b', 'ffn_out_w']
SHARD_AXIS = {'norm_g': 2, 'ssm_in_w': 2, 'ssm_conv_w': 2, 'ssm_conv_b': None, 'ssm_dt_bias': None, 'ssm_A_log': None,
              'ssm_D': None, 'ssm_norm_g': None, 'ssm_out_w': 1, 'cf_pw1_w': 2, 'cf_pw1_b': 1, 'cf_dw_w': 2, 'cf_dw_b': 1,
              'cf_ln_g': 1, 'cf_ln_b': 1, 'cf_pw2_w': 1, 'cf_pw2_b': 1, 'xa_mem_g': None, 'xa_q_w': 1, 'xa_kv_w': 2,
              'xa_o_w': 1, 'ffn_in_w': 2, 'ffn_conv_w': 2, 'ffn_conv_b': None, 'ffn_out_w': 1}
MATMUL_WEIGHTS = ('ssm_in_w', 'ssm_out_w', 'cf_pw1_w', 'cf_pw2_w', 'xa_q_w', 'xa_kv_w', 'xa_o_w', 'ffn_in_w', 'ffn_out_w')


def _cp(*sem):
    return pltpu.CompilerParams(dimension_semantics=tuple(sem), vmem_limit_bytes=VMEM_LIMIT)


def _pick(dim, pref):
    if dim <= pref:
        return dim
    best = None
    for t in range(LANE, pref + 1, LANE):
        if dim % t == 0:
            best = t
    assert best is not None, (dim, pref)
    return best


def _sigmoid(x):
    return 1.0 / (1.0 + jnp.exp(-x))


def _silu(x):
    return x * _sigmoid(x)


def _dsilu(x):
    s = _sigmoid(x)
    return s * (1.0 + x * (1.0 - s))


def _softplus(x):
    return jnp.maximum(x, 0.0) + jnp.log(1.0 + jnp.exp(-jnp.abs(x)))


_DN = {"nn": (((1,), (0,)), ((), ())), "nt": (((1,), (1,)), ((), ())), "tn": (((0,), (0,)), ((), ()))}


def _mm(a, b, mode, *, name, out_dtype=f32, bias=None, add=None, b_shards=False, out_shards=False):
    bshape = (b.shape[1], b.shape[2] * N_CHIPS) if b_shards else b.shape
    if mode == "nn":
        (M, K), (K2, N) = a.shape, bshape
    elif mode == "nt":
        (M, K), (N, K2) = a.shape, bshape
    else:
        (K, M), (K2, N) = a.shape, bshape
    assert K == K2, (a.shape, b.shape, mode)
    n_unit = N // N_CHIPS if ((b_shards and mode == "nn") or out_shards) else N
    k_unit = K // N_CHIPS if (b_shards and mode == "nt") else K
    tm, tn, tk = _pick(M, 1024), _pick(n_unit, 1408), _pick(k_unit, 1408)
    nk, nj_u, nk_u = K // tk, n_unit // tn, k_unit // tk
    a_spec = {"nn": pl.BlockSpec((tm, tk), lambda i, j, k: (i, k)), "nt": pl.BlockSpec((tm, tk), lambda i, j, k: (i, k)),
              "tn": pl.BlockSpec((tk, tm), lambda i, j, k: (k, i))}[mode]
    if not b_shards:
        b_spec = {"nn": pl.BlockSpec((tk, tn), lambda i, j, k: (k, j)), "nt": pl.BlockSpec((tn, tk), lambda i, j, k: (j, k)),
                  "tn": pl.BlockSpec((tk, tn), lambda i, j, k: (k, j))}[mode]
    else:
        b_spec = {"nn": pl.BlockSpec((None, tk, tn), lambda i, j, k: (j // nj_u, k, j % nj_u)),
                  "nt": pl.BlockSpec((None, tn, tk), lambda i, j, k: (k // nk_u, j, k % nk_u))}[mode]
    in_specs, args = [a_spec, b_spec], [a, b]
    if bias is not None:
        in_specs.append(pl.BlockSpec((1, tn), lambda i, j, k: (0, j)))
        args.append(bias)
    if add is not None:
        in_specs.append(pl.BlockSpec((tm, tn), lambda i, j, k: (i, j)))
        args.append(add)
    if not out_shards:
        out_shape, out_spec = S((M, N), out_dtype), pl.BlockSpec((tm, tn), lambda i, j, k: (i, j))
    else:
        out_shape = S((N_CHIPS, M, n_unit), out_dtype)
        out_spec = pl.BlockSpec((None, tm, tn), lambda i, j, k: (j // nj_u, i, j % nj_u))
    dn = _DN[mode]
    has_bias, has_add = bias is not None, add is not None

    def body(a_ref, b_ref, *rest):
        rest = list(rest)
        bias_ref = rest.pop(0) if has_bias else None
        add_ref = rest.pop(0) if has_add else None
        o_ref, acc_ref = rest
        k = pl.program_id(2)

        @pl.when(k == 0)
        def _():
            acc_ref[...] = jnp.zeros_like(acc_ref)

        acc_ref[...] += lax.dot_general(a_ref[...].astype(bf16), b_ref[...].astype(bf16), dn, preferred_element_type=f32)

        @pl.when(k == nk - 1)
        def _():
            r = acc_ref[...]
            if has_bias:
                r = r + bias_ref[...]
            if has_add:
                r = r + add_ref[...].astype(f32)
            o_ref[...] = r.astype(out_dtype)

    return pl.pallas_call(
        body, name=name, out_shape=out_shape, grid=(M // tm, N // tn, nk),
        in_specs=in_specs, out_specs=out_spec, scratch_shapes=[pltpu.VMEM((tm, tn), f32)],
        compiler_params=_cp("parallel", "parallel", "arbitrary"))(*args)


def _rows(tm, C):
    return pl.BlockSpec((tm, C), lambda i: (i, 0))


def _const(shape):
    return pl.BlockSpec(shape, lambda i: tuple(0 for _ in shape))


def _rms_val(x, g):
    r = lax.rsqrt(jnp.mean(x * x, axis=-1, keepdims=True) + EPS)
    return x * r * g


def _rms_bwd_val(x, g, dy):
    r = lax.rsqrt(jnp.mean(x * x, axis=-1, keepdims=True) + EPS)
    xn = x * r
    dxh = dy * g
    dx = r * (dxh - xn * jnp.mean(dxh * xn, axis=-1, keepdims=True))
    return dx, jnp.sum(dy * xn, axis=0, keepdims=True)


ANY_SPEC = pl.BlockSpec(memory_space=pl.ANY)


def _rmsnorm_fwd(x, g, *, name, after=()):
    L, C = x.shape
    tm = _pick(L, 512)

    def body(x_ref, g_ref, *rest):
        rest[-1][...] = _rms_val(x_ref[...], g_ref[...]).astype(bf16)

    return pl.pallas_call(body, name=name, out_shape=S((L, C), bf16), grid=(L // tm,),
                          in_specs=[_rows(tm, C), _const((1, C))] + [ANY_SPEC] * len(after), out_specs=_rows(tm, C),
                          compiler_params=_cp("parallel"))(x, g, *after)


def _resid_norm_fwd(x, mix, g_post, g_next, *, name):
    L, C = x.shape
    tm = _pick(L, 512)
    want_h = g_next is not None

    def body(x_ref, m_ref, gp_ref, *rest):
        xn = x_ref[...] + _rms_val(m_ref[...], gp_ref[...])
        if want_h:
            gn_ref, xo_ref, h_ref = rest
            h_ref[...] = _rms_val(xn, gn_ref[...]).astype(bf16)
        else:
            (xo_ref,) = rest
        xo_ref[...] = xn

    in_specs = [_rows(tm, C), _rows(tm, C), _const((1, C))]
    args = [x, mix, g_post]
    out_shape, out_specs = [S((L, C), f32)], [_rows(tm, C)]
    if want_h:
        in_specs.append(_const((1, C)))
        args.append(g_next)
        out_shape.append(S((L, C), bf16))
        out_specs.append(_rows(tm, C))
    out = pl.pallas_call(body, name=name, out_shape=tuple(out_shape), grid=(L // tm,), in_specs=in_specs,
                         out_specs=tuple(out_specs), compiler_params=_cp("parallel"))(*args)
    return (out[0], out[1]) if want_h else (out[0], None)


def _norm_bwd(x, g, dy, *, name, add=None, out_dtype=f32, after=()):
    L, C = x.shape
    tm = _pick(L, 512)
    has_add = add is not None

    def body(x_ref, g_ref, dy_ref, *rest):
        rest = list(rest)
        add_ref = rest.pop(0) if has_add else None
        dx_ref, dg_ref, cs_ref = rest[-3:]
        i = pl.program_id(0)

        @pl.when(i == 0)
        def _():
            dg_ref[...] = jnp.zeros_like(dg_ref)
            cs_ref[...] = jnp.zeros_like(cs_ref)

        dx, dg = _rms_bwd_val(x_ref[...], g_ref[...], dy_ref[...].astype(f32))
        dg_ref[...] += dg
        cs_ref[...] += jnp.sum(dx, axis=0, keepdims=True)
        if has_add:
            dx = dx + add_ref[...]
        dx_ref[...] = dx.astype(out_dtype)

    in_specs = [_rows(tm, C), _const((1, C)), _rows(tm, C)]
    args = [x, g, dy]
    if has_add:
        in_specs.append(_rows(tm, C))
        args.append(add)
    in_specs += [ANY_SPEC] * len(after)
    args += list(after)
    return pl.pallas_call(body, name=name, out_shape=(S((L, C), out_dtype), S((1, C), f32), S((1, C), f32)),
                          grid=(L // tm,), in_specs=in_specs,
                          out_specs=(_rows(tm, C), _const((1, C)), _const((1, C))),
                          compiler_params=_cp("arbitrary"))(*args)


def _loss_fwd_bwd(y, target, *, name):
    L, C = y.shape
    tm = _pick(L, 512)

    def body(y_ref, t_ref, acc_ref, dy_ref):
        i = pl.program_id(0)

        @pl.when(i == 0)
        def _():
            acc_ref[...] = jnp.zeros_like(acc_ref)

        e = y_ref[...] - t_ref[...]
        rs = jnp.sum(e * e, axis=-1, keepdims=True)
        acc_ref[...] += jnp.broadcast_to(jnp.sum(rs, axis=0, keepdims=True), (1, LANE))
        dy_ref[...] = e * (1.0 / C)

    return pl.pallas_call(body, name=name, out_shape=(S((1, LANE), f32), S((L, C), f32)), grid=(L // tm,),
                          in_specs=[_rows(tm, C), _rows(tm, C)], out_specs=(_const((1, LANE)), _rows(tm, C)),
                          compiler_params=_cp("arbitrary"))(y, target)


def _halo_rows(K):
    return SUBLANE if K - 1 <= SUBLANE else 32


def _prev_halo_spec(tm, H, C):
    return pl.BlockSpec((H, C), lambda i: (jnp.maximum(i * (tm // H) - 1, 0), 0))


def _next_halo_spec(tm, H, C, L):
    return pl.BlockSpec((H, C), lambda i: (jnp.minimum((i + 1) * (tm // H), L // H - 1), 0))


def _shift_down(ext, s):
    return ext if s == 0 else pltpu.roll(ext, s, axis=0)


def _shift_up(ext, s):
    return ext if s == 0 else pltpu.roll(ext, ext.shape[0] - s, axis=0)


def _causal_conv(ext, H, w_ref, K):
    acc = None
    for k in range(K):
        term = _shift_down(ext, K - 1 - k)[H:, :] * w_ref[k:k + 1, :]
        acc = term if acc is None else acc + term
    return acc


def _anticausal_conv(ext, tm, w_ref, K):
    acc = None
    for k in range(K):
        term = _shift_up(ext, K - 1 - k)[:tm, :] * w_ref[k:k + 1, :]
        acc = term if acc is None else acc + term
    return acc


def _tap_grads(dw_ref, d_cur, x_ext, H, K):
    for k in range(K):
        dw_ref[k:k + 1, :] += jnp.sum(d_cur * _shift_down(x_ext, K - 1 - k)[H:, :], axis=0, keepdims=True)


def _pad_taps(w, K):
    return jnp.pad(w, ((0, _halo_rows(K) - K), (0, 0)))


def _conv_act_fwd(x, w, b, *, K, act, name, out_dtype, tm_pref=256):
    L, C = x.shape
    H = _halo_rows(K)
    tm = _pick(L, tm_pref)
    Co = C if act == "silu" else C // 2

    def body(h_ref, x_ref, w_ref, b_ref, o_ref):
        i = pl.program_id(0)
        halo = jnp.where(i > 0, h_ref[...], 0.0)
        ext = jnp.concatenate([halo, x_ref[...]], axis=0)
        u = _causal_conv(ext, H, w_ref, K) + b_ref[...]
        if act == "silu":
            o_ref[...] = _silu(u).astype(out_dtype)
        else:
            o_ref[...] = (_silu(u[:, :Co]) * u[:, Co:]).astype(out_dtype)

    return pl.pallas_call(body, name=name, out_shape=S((L, Co), out_dtype), grid=(L // tm,),
                          in_specs=[_prev_halo_spec(tm, H, C), _rows(tm, C), _const((H, C)), _const((1, C))],
                          out_specs=_rows(tm, Co), compiler_params=_cp("parallel"))(x, x, w, b)


def _conv_act_bwd(x, dparts, w, b, *, K, act, name, tm_pref=256):
    L, C = x.shape
    H = _halo_rows(K)
    tm = _pick(L, tm_pref)
    Co = C if act == "silu" else C // 2
    nparts = len(dparts)

    def body(h_ref, x_ref, w_ref, b_ref, *rest):
        d_refs, (du_ref, dw_ref, db_ref) = rest[:nparts], rest[nparts:]
        i = pl.program_id(0)

        @pl.when(i == 0)
        def _():
            dw_ref[...] = jnp.zeros_like(dw_ref)
            db_ref[...] = jnp.zeros_like(db_ref)

        halo = jnp.where(i > 0, h_ref[...], 0.0)
        ext = jnp.concatenate([halo, x_ref[...]], axis=0)
        u = _causal_conv(ext, H, w_ref, K) + b_ref[...]
        d = [r[...].astype(f32) for r in d_refs]
        d = d[0] if nparts == 1 else jnp.concatenate(d, axis=1)
        if act == "silu":
            du = d * _dsilu(u)
        else:
            g, v = u[:, :Co], u[:, Co:]
            du = jnp.concatenate([d * v * _dsilu(g), d * _silu(g)], axis=1)
        du_ref[...] = du
        db_ref[...] += jnp.sum(du, axis=0, keepdims=True)
        _tap_grads(dw_ref, du, ext, H, K)

    in_specs = [_prev_halo_spec(tm, H, C), _rows(tm, C), _const((H, C)), _const((1, C))]
    in_specs += [_rows(tm, p.shape[1]) for p in dparts]
    return pl.pallas_call(body, name=name, out_shape=(S((L, C), f32), S((H, C), f32), S((1, C), f32)), grid=(L // tm,),
                          in_specs=in_specs, out_specs=(_rows(tm, C), _const((H, C)), _const((1, C))),
                          compiler_params=_cp("arbitrary"))(x, x, w, b, *dparts)


def _conv_transpose(d, w, *, K, name, tm_pref=256):
    L, C = d.shape
    H = _halo_rows(K)
    tm = _pick(L, tm_pref)
    nb = L // tm

    def body(d_ref, h_ref, w_ref, o_ref):
        i = pl.program_id(0)
        halo = jnp.where(i < nb - 1, h_ref[...], 0.0)
        ext = jnp.concatenate([d_ref[...], halo], axis=0)
        o_ref[...] = _anticausal_conv(ext, tm, w_ref, K).astype(bf16)

    return pl.pallas_call(body, name=name, out_shape=S((L, C), bf16), grid=(nb,),
                          in_specs=[_rows(tm, C), _next_halo_spec(tm, H, C, L), _const((H, C))],
                          out_specs=_rows(tm, C), compiler_params=_cp("parallel"))(d, d, w)


def _cf_fwd(u, dw_w, dw_b, ln_g, ln_b, *, name):
    L, C2 = u.shape
    C = C2 // 2
    K, H = CF_KERNEL, _halo_rows(CF_KERNEL)
    tm = _pick(L, 256)

    def body(h_ref, u_ref, w_ref, b_ref, g_ref, lb_ref, c_ref, s_ref):
        i = pl.program_id(0)
        halo = jnp.where(i > 0, h_ref[...], 0.0)
        ext = jnp.concatenate([halo, u_ref[...]], axis=0)
        glu = ext[:, :C] * _sigmoid(ext[:, C:])
        c = _causal_conv(glu, H, w_ref, K) + b_ref[...]
        c_ref[...] = c
        mu = jnp.mean(c, axis=-1, keepdims=True)
        xc = c - mu
        var = jnp.mean(xc * xc, axis=-1, keepdims=True)
        ln = xc * lax.rsqrt(var + EPS) * g_ref[...] + lb_ref[...]
        s_ref[...] = _silu(ln).astype(bf16)

    return pl.pallas_call(body, name=name, out_shape=(S((L, C), f32), S((L, C), bf16)), grid=(L // tm,),
                          in_specs=[_prev_halo_spec(tm, H, C2), _rows(tm, C2), _const((H, C)), _const((1, C)),
                                    _const((1, C)), _const((1, C))],
                          out_specs=(_rows(tm, C), _rows(tm, C)), compiler_params=_cp("parallel"))(u, u, dw_w, dw_b, ln_g, ln_b)


def _cf_ln_bwd(c, ln_g, ln_b, ds, *, name):
    L, C = c.shape
    tm = _pick(L, 512)

    def body(c_ref, g_ref, lb_ref, ds_ref, dc_ref, dg_ref, db_ref):
        i = pl.program_id(0)

        @pl.when(i == 0)
        def _():
            dg_ref[...] = jnp.zeros_like(dg_ref)
            db_ref[...] = jnp.zeros_like(db_ref)

        c = c_ref[...]
        mu = jnp.mean(c, axis=-1, keepdims=True)
        xc = c - mu
        r = lax.rsqrt(jnp.mean(xc * xc, axis=-1, keepdims=True) + EPS)
        xh = xc * r
        ln = xh * g_ref[...] + lb_ref[...]
        dln = ds_ref[...].astype(f32) * _dsilu(ln)
        dg_ref[...] += jnp.sum(dln * xh, axis=0, keepdims=True)
        db_ref[...] += jnp.sum(dln, axis=0, keepdims=True)
        dxh = dln * g_ref[...]
        dc_ref[...] = r * (dxh - jnp.mean(dxh, axis=-1, keepdims=True) - xh * jnp.mean(dxh * xh, axis=-1, keepdims=True))

    return pl.pallas_call(body, name=name, out_shape=(S((L, C), f32), S((1, C), f32), S((1, C), f32)), grid=(L // tm,),
                          in_specs=[_rows(tm, C), _const((1, C)), _const((1, C)), _rows(tm, C)],
                          out_specs=(_rows(tm, C), _const((1, C)), _const((1, C))),
                          compiler_params=_cp("arbitrary"))(c, ln_g, ln_b, ds)


def _cf_glu_bwd(u, dc, dw_w, *, name):
    L, C2 = u.shape
    C = C2 // 2
    K, H = CF_KERNEL, _halo_rows(CF_KERNEL)
    tm = _pick(L, 256)
    nb = L // tm

    def body(uh_ref, u_ref, dc_ref, dch_ref, w_ref, du_ref, dw_ref, db_ref, dus_ref):
        i = pl.program_id(0)

        @pl.when(i == 0)
        def _():
            dw_ref[...] = jnp.zeros_like(dw_ref)
            db_ref[...] = jnp.zeros_like(db_ref)
            dus_ref[...] = jnp.zeros_like(dus_ref)

        halo = jnp.where(i > 0, uh_ref[...], 0.0)
        ext = jnp.concatenate([halo, u_ref[...]], axis=0)
        sg = _sigmoid(ext[:, C:])
        glu = ext[:, :C] * sg
        dc = dc_ref[...]
        dnext = jnp.where(i < nb - 1, dch_ref[...], 0.0)
        dglu = _anticausal_conv(jnp.concatenate([dc, dnext], axis=0), tm, w_ref, K)
        a_cur, sg_cur = ext[H:, :C], sg[H:, :]
        du = jnp.concatenate([dglu * sg_cur, dglu * a_cur * sg_cur * (1.0 - sg_cur)], axis=1)
        du_ref[...] = du.astype(bf16)
        dus_ref[...] += jnp.sum(du, axis=0, keepdims=True)
        db_ref[...] += jnp.sum(dc, axis=0, keepdims=True)
        _tap_grads(dw_ref, dc, glu, H, K)

    return pl.pallas_call(body, name=name,
                          out_shape=(S((L, C2), bf16), S((H, C), f32), S((1, C), f32), S((1, C2), f32)), grid=(nb,),
                          in_specs=[_prev_halo_spec(tm, H, C2), _rows(tm, C2), _rows(tm, C), _next_halo_spec(tm, H, C, L),
                                    _const((H, C))],
                          out_specs=(_rows(tm, C2), _const((H, C)), _const((1, C)), _const((1, C2))),
                          compiler_params=_cp("arbitrary"))(u, u, dc, dc, dw_w)


def _gated_norm_fwd(y, z, g, *, name):
    L, C = y.shape
    tm = _pick(L, 256)

    def body(y_ref, z_ref, g_ref, o_ref):
        o_ref[...] = _rms_val(y_ref[...] * _silu(z_ref[...]), g_ref[...]).astype(bf16)

    return pl.pallas_call(body, name=name, out_shape=S((L, C), bf16), grid=(L // tm,),
                          in_specs=[_rows(tm, C), _rows(tm, C), _const((1, C))], out_specs=_rows(tm, C),
                          compiler_params=_cp("parallel"))(y, z, g)


def _gated_norm_bwd(y, z, g, dyn, *, name):
    L, C = y.shape
    tm = _pick(L, 256)

    def body(y_ref, z_ref, g_ref, d_ref, dy_ref, dz_ref, dg_ref):
        i = pl.program_id(0)

        @pl.when(i == 0)
        def _():
            dg_ref[...] = jnp.zeros_like(dg_ref)

        y, z = y_ref[...], z_ref[...]
        sz = _silu(z)
        du, dg = _rms_bwd_val(y * sz, g_ref[...], d_ref[...].astype(f32))
        dg_ref[...] += dg
        dy_ref[...] = du * sz
        dz_ref[...] = (du * y * _dsilu(z)).astype(bf16)

    return pl.pallas_call(body, name=name, out_shape=(S((L, C), f32), S((L, C), bf16), S((1, C), f32)), grid=(L // tm,),
                          in_specs=[_rows(tm, C), _rows(tm, C), _const((1, C)), _rows(tm, C)],
                          out_specs=(_rows(tm, C), _rows(tm, C), _const((1, C))),
                          compiler_params=_cp("arbitrary"))(y, z, g, dyn)


_XA_SCALE = XA_HEAD_DIM ** -0.5


def _attn_fwd(q, kv, *, name):
    L, C = q.shape
    tm = _pick(L, 512)
    Dh = XA_HEAD_DIM

    def body(q_ref, kv_ref, o_ref):
        for h in range(XA_HEADS):
            qh = q_ref[:, h * Dh:(h + 1) * Dh]
            kh = kv_ref[:, h * Dh:(h + 1) * Dh]
            vh = kv_ref[:, C + h * Dh:C + (h + 1) * Dh]
            s = lax.dot_general(qh, kh, _DN["nt"], preferred_element_type=f32) * _XA_SCALE
            e = jnp.exp(s - jnp.max(s, axis=-1, keepdims=True))
            p = e / jnp.sum(e, axis=-1, keepdims=True)
            o_ref[:, h * Dh:(h + 1) * Dh] = jnp.dot(p.astype(bf16), vh, preferred_element_type=f32).astype(bf16)

    return pl.pallas_call(body, name=name, out_shape=S((L, C), bf16), grid=(L // tm,),
                          in_specs=[_rows(tm, C), _const((N_MEM, 2 * C))], out_specs=_rows(tm, C),
                          compiler_params=_cp("parallel"))(q, kv)


def _attn_bwd(q, kv, do, *, name):
    L, C = q.shape
    tm = _pick(L, 512)
    Dh = XA_HEAD_DIM

    def body(q_ref, kv_ref, do_ref, dq_ref, dkv_ref):
        i = pl.program_id(0)

        @pl.when(i == 0)
        def _():
            dkv_ref[...] = jnp.zeros_like(dkv_ref)

        for h in range(XA_HEADS):
            qh = q_ref[:, h * Dh:(h + 1) * Dh]
            kh = kv_ref[:, h * Dh:(h + 1) * Dh]
            vh = kv_ref[:, C + h * Dh:C + (h + 1) * Dh]
            doh = do_ref[:, h * Dh:(h + 1) * Dh]
            s = lax.dot_general(qh, kh, _DN["nt"], preferred_element_type=f32) * _XA_SCALE
            e = jnp.exp(s - jnp.max(s, axis=-1, keepdims=True))
            p = e / jnp.sum(e, axis=-1, keepdims=True)
            pb = p.astype(bf16)
            dkv_ref[:, C + h * Dh:C + (h + 1) * Dh] += lax.dot_general(pb, doh, _DN["tn"], preferred_element_type=f32)
            dp = lax.dot_general(doh, vh, _DN["nt"], preferred_element_type=f32)
            ds = (p * (dp - jnp.sum(dp * p, axis=-1, keepdims=True)) * _XA_SCALE).astype(bf16)
            dq_ref[:, h * Dh:(h + 1) * Dh] = jnp.dot(ds, kh, preferred_element_type=f32).astype(bf16)
            dkv_ref[:, h * Dh:(h + 1) * Dh] += lax.dot_general(ds, qh, _DN["tn"], preferred_element_type=f32)

    return pl.pallas_call(body, name=name, out_shape=(S((L, C), bf16), S((N_MEM, 2 * C), f32)), grid=(L // tm,),
                          in_specs=[_rows(tm, C), _const((N_MEM, 2 * C)), _rows(tm, C)],
                          out_specs=(_rows(tm, C), _const((N_MEM, 2 * C))),
                          compiler_params=_cp("arbitrary"))(q, kv, do)


Q = CHUNK
PAIRS = HEADS_PER_GROUP // 2
GW = HEADS_PER_GROUP * HEAD_DIM


def _split3(x):
    x1 = x.astype(bf16)
    r1 = x - x1.astype(f32)
    x2 = r1.astype(bf16)
    return x1, x2, (r1 - x2.astype(f32)).astype(bf16)


def _sel_right(x, sel, mode="nn"):
    return sum(lax.dot_general(p, sel, _DN[mode], preferred_element_type=f32) for p in _split3(x))


def _sel_left(sel, x):
    return sum(lax.dot_general(sel, p, _DN["nn"], preferred_element_type=f32) for p in _split3(x))


def _ssd_common(dt_ref, hp_ref):
    dt_pre = dt_ref[...] + hp_ref[0:1, :]
    dt = _softplus(dt_pre)
    A = -jnp.exp(hp_ref[1:2, :])
    a = dt * A
    row = lax.broadcasted_iota(jnp.int32, (Q, Q), 0)
    col = lax.broadcasted_iota(jnp.int32, (Q, Q), 1)
    tri = row >= col
    cs = _sel_left(tri.astype(bf16), a)
    T = cs[Q - 1:Q, :]
    return dict(dt_pre=dt_pre, dt=dt, A=A, cs=cs, csT=cs.T, T=T, ecs=jnp.exp(cs), eend=jnp.exp(T - cs), eT=jnp.exp(T),
                tri=tri, row=row, col=col)


def _pair_expand(v, jj, lo):
    return jnp.where(lo, v[:, 2 * jj:2 * jj + 1], v[:, 2 * jj + 1:2 * jj + 2])


def _decay(cm, h):
    seg = cm["cs"][:, h:h + 1] - cm["csT"][h:h + 1, :]
    return jnp.where(cm["tri"], jnp.exp(jnp.where(cm["tri"], seg, 0.0)), 0.0)


def _ssd_fwd(act, dtp, hp, *, name):
    L = act.shape[0]
    nc = L // Q

    def body(xs_ref, b_ref, c_ref, dt_ref, hp_ref, y_ref, hs_ref, h_scr):
        c = pl.program_id(1)

        @pl.when(c == 0)
        def _():
            h_scr[...] = jnp.zeros_like(h_scr)

        cm = _ssd_common(dt_ref, hp_ref)
        Bb, Cb = b_ref[...].astype(bf16), c_ref[...].astype(bf16)
        CB = lax.dot_general(Cb, Bb, _DN["nt"], preferred_element_type=f32)
        lo = lax.broadcasted_iota(jnp.int32, (Q, LANE), 1) < HEAD_DIM
        top = lax.broadcasted_iota(jnp.int32, (LANE, LANE), 0) < HEAD_DIM
        Drow = hp_ref[2:3, :]
        for jj in range(PAIRS):
            hA, hB = 2 * jj, 2 * jj + 1
            dtx, ecsx, eendx = (_pair_expand(cm[k], jj, lo) for k in ("dt", "ecs", "eend"))
            xs_p = xs_ref[:, jj * LANE:(jj + 1) * LANE]
            Xd = xs_p * dtx
            Y = None
            for h, Xm in ((hA, jnp.where(lo, Xd, 0.0)), (hB, jnp.where(lo, 0.0, Xd))):
                W = (CB * _decay(cm, h)).astype(bf16)
                t = jnp.dot(W, Xm.astype(bf16), preferred_element_type=f32)
                Y = t if Y is None else Y + t
            Hp = h_scr[jj]
            hs_ref[0, jj] = Hp
            Yoff = lax.dot_general(Cb, Hp.astype(bf16), _DN["nt"], preferred_element_type=f32) * ecsx
            Dx = jnp.where(lo[0:1, :], Drow[:, hA:hA + 1], Drow[:, hB:hB + 1])
            y_ref[:, jj * LANE:(jj + 1) * LANE] = Y + Yoff + xs_p * Dx
            Snew = lax.dot_general((Xd * eendx).astype(bf16), Bb, _DN["tn"], preferred_element_type=f32)
            eTx = jnp.where(top, cm["eT"][:, hA:hA + 1], cm["eT"][:, hB:hB + 1])
            h_scr[jj] = Hp * eTx + Snew

    return pl.pallas_call(
        body, name=name, out_shape=(S((L, D_INNER), f32), S((nc, N_SSM_HEADS // 2, LANE, D_STATE), f32)),
        grid=(N_GROUPS, nc),
        in_specs=[pl.BlockSpec((Q, GW), lambda g, c: (c, g)),
                  pl.BlockSpec((Q, D_STATE), lambda g, c: (c, D_INNER // D_STATE + g)),
                  pl.BlockSpec((Q, D_STATE), lambda g, c: (c, D_INNER // D_STATE + N_GROUPS + g)),
                  pl.BlockSpec((Q, LANE), lambda g, c: (c, g)),
                  pl.BlockSpec((SUBLANE, LANE), lambda g, c: (0, g))],
        out_specs=(pl.BlockSpec((Q, GW), lambda g, c: (c, g)),
                   pl.BlockSpec((1, PAIRS, LANE, D_STATE), lambda g, c: (c, g, 0, 0))),
        scratch_shapes=[pltpu.VMEM((PAIRS, LANE, D_STATE), f32)],
        compiler_params=_cp("arbitrary", "arbitrary"))(act, act, act, dtp, hp)


def _ssd_bwd(act, dtp, hp, dy, hs, *, name):
    L = act.shape[0]
    nc = L // Q

    def body(xs_ref, b_ref, c_ref, dt_ref, hp_ref, dy_ref, hs_ref, dxs_ref, db_ref, dc_ref, ddt_ref, dhp_ref, dh_scr):
        c = pl.program_id(1)

        @pl.when(c == 0)
        def _():
            dh_scr[...] = jnp.zeros_like(dh_scr)
            dhp_ref[...] = jnp.zeros_like(dhp_ref)

        cm = _ssd_common(dt_ref, hp_ref)
        Bb, Cb = b_ref[...].astype(bf16), c_ref[...].astype(bf16)
        CB = lax.dot_general(Cb, Bb, _DN["nt"], preferred_element_type=f32)
        lane = lax.broadcasted_iota(jnp.int32, (Q, LANE), 1)
        sub = lax.broadcasted_iota(jnp.int32, (LANE, LANE), 0)
        lo = lane < HEAD_DIM
        top = sub < HEAD_DIM
        Drow = hp_ref[2:3, :]
        zero = jnp.zeros((Q, LANE), f32)
        dcs, ddtx, dC, dB, dCB = zero, zero, zero, zero, jnp.zeros((Q, Q), f32)
        dD_row = jnp.zeros((1, LANE), f32)
        dT_row = jnp.zeros((1, LANE), f32)
        for jj in range(PAIRS):
            hA, hB = 2 * jj, 2 * jj + 1
            Pj = (lane == jnp.where(top, hA, hB)).astype(bf16)
            dtx, ecsx, eendx = (_pair_expand(cm[k], jj, lo) for k in ("dt", "ecs", "eend"))
            xs_p = xs_ref[:, jj * LANE:(jj + 1) * LANE]
            dY_p = dy_ref[:, jj * LANE:(jj + 1) * LANE]
            Xd = xs_p * dtx
            Xdb = Xd.astype(bf16)
            Hp, dHn = hs_ref[0, jj], dh_scr[jj]
            Hb, dHb = Hp.astype(bf16), dHn.astype(bf16)
            EdYb = (dY_p * ecsx).astype(bf16)
            YoffN = lax.dot_general(Cb, Hb, _DN["nt"], preferred_element_type=f32)
            dC = dC + jnp.dot(EdYb, Hb, preferred_element_type=f32)
            dH_off = lax.dot_general(EdYb, Cb, _DN["tn"], preferred_element_type=f32)
            R = lax.dot_general(Bb, dHb, _DN["nt"], preferred_element_type=f32)
            Xe = Xd * eendx
            dB = dB + jnp.dot(Xe.astype(bf16), dHb, preferred_element_type=f32)
            dXd = R * eendx
            V2 = _sel_right(R * Xe, Pj)
            dcs = dcs + _sel_right(dY_p * YoffN * ecsx, Pj) - V2
            dT_row = dT_row + jnp.sum(V2, axis=0, keepdims=True) \
                + jnp.sum(_sel_right(dHn * Hp, Pj, "tn"), axis=0, keepdims=True) * cm["eT"]
            for h, keep in ((hA, lo), (hB, jnp.logical_not(lo))):
                M = _decay(cm, h)
                Wf = CB * M
                dYm = jnp.where(keep, dY_p, 0.0).astype(bf16)
                dW = lax.dot_general(dYm, Xdb, _DN["nt"], preferred_element_type=f32)
                dXd = dXd + lax.dot_general(Wf.astype(bf16), dYm, _DN["tn"], preferred_element_type=f32)
                Z = dW * Wf
                onesh = (lane == h).astype(bf16)
                dcs = dcs + _sel_right(Z, onesh) - _sel_right(Z, onesh, "tn")
                dCB = dCB + dW * M
            Dx = jnp.where(lo[0:1, :], Drow[:, hA:hA + 1], Drow[:, hB:hB + 1])
            dxs_ref[:, jj * LANE:(jj + 1) * LANE] = dXd * dtx + dY_p * Dx
            ddtx = ddtx + _sel_right(dXd * xs_p, Pj)
            dD_row = dD_row + jnp.sum(_sel_right(dY_p * xs_p, Pj), axis=0, keepdims=True)
            eTx = jnp.where(top, cm["eT"][:, hA:hA + 1], cm["eT"][:, hB:hB + 1])
            dh_scr[jj] = dHn * eTx + dH_off
        dCBb = dCB.astype(bf16)
        dc_ref[...] = dC + jnp.dot(dCBb, Bb, preferred_element_type=f32)
        db_ref[...] = dB + lax.dot_general(dCBb, Cb, _DN["tn"], preferred_element_type=f32)
        dcs = dcs + jnp.where(lax.broadcasted_iota(jnp.int32, (Q, LANE), 0) == Q - 1, dT_row, 0.0)
        da = _sel_left((cm["row"] <= cm["col"]).astype(bf16), dcs)
        ddt_pre = (da * cm["A"] + ddtx) * _sigmoid(cm["dt_pre"])
        ddt_ref[...] = ddt_pre
        r8 = lax.broadcasted_iota(jnp.int32, (SUBLANE, LANE), 0)
        dhp_ref[...] += jnp.where(r8 == 0, jnp.sum(ddt_pre, axis=0, keepdims=True),
                                  jnp.where(r8 == 1, jnp.sum(da * cm["dt"], axis=0, keepdims=True) * cm["A"],
                                            jnp.where(r8 == 2, dD_row, 0.0)))

    rev = lambda c: nc - 1 - c
    return pl.pallas_call(
        body, name=name,
        out_shape=(S((L, D_INNER), f32), S((L, N_GROUPS * D_STATE), f32), S((L, N_GROUPS * D_STATE), f32),
                   S((L, N_GROUPS * LANE), f32), S((SUBLANE, N_GROUPS * LANE), f32)),
        grid=(N_GROUPS, nc),
        in_specs=[pl.BlockSpec((Q, GW), lambda g, c: (rev(c), g)),
                  pl.BlockSpec((Q, D_STATE), lambda g, c: (rev(c), D_INNER // D_STATE + g)),
                  pl.BlockSpec((Q, D_STATE), lambda g, c: (rev(c), D_INNER // D_STATE + N_GROUPS + g)),
                  pl.BlockSpec((Q, LANE), lambda g, c: (rev(c), g)),
                  pl.BlockSpec((SUBLANE, LANE), lambda g, c: (0, g)),
                  pl.BlockSpec((Q, GW), lambda g, c: (rev(c), g)),
                  pl.BlockSpec((1, PAIRS, LANE, D_STATE), lambda g, c: (rev(c), g, 0, 0))],
        out_specs=(pl.BlockSpec((Q, GW), lambda g, c: (rev(c), g)),
                   pl.BlockSpec((Q, D_STATE), lambda g, c: (rev(c), g)),
                   pl.BlockSpec((Q, D_STATE), lambda g, c: (rev(c), g)),
                   pl.BlockSpec((Q, LANE), lambda g, c: (rev(c), g)),
                   pl.BlockSpec((SUBLANE, LANE), lambda g, c: (0, g))),
        scratch_shapes=[pltpu.VMEM((PAIRS, LANE, D_STATE), f32)],
        compiler_params=_cp("arbitrary", "arbitrary"))(act, act, act, dtp, hp, dy, hs)


def _group_pad_cols(w):
    lead = w.shape[:-1]
    w = w.reshape(lead + (N_GROUPS, HEADS_PER_GROUP))
    w = jnp.pad(w, [(0, 0)] * len(lead) + [(0, 0), (0, LANE - HEADS_PER_GROUP)])
    return w.reshape(lead + (N_GROUPS * LANE,))


def _group_unpad_cols(w):
    lead = w.shape[:-1]
    return w.reshape(lead + (N_GROUPS, LANE))[..., :HEADS_PER_GROUP].reshape(lead + (N_SSM_HEADS,))


def _row(v):
    return v.reshape(1, -1)


ROW_SHARDED = ('ssm_out_w', 'cf_pw2_w', 'xa_q_w', 'xa_o_w', 'ffn_out_w')
COL_SHARDED = ('cf_pw1_w', 'xa_kv_w', 'ffn_in_w')


def _layer_matmul_weights(i):
    mixer = ('ssm_in_w', 'ssm_out_w') if i % 2 == 0 else ('cf_pw1_w', 'cf_pw2_w')
    return mixer + ('xa_q_w', 'xa_kv_w', 'xa_o_w', 'ffn_in_w', 'ffn_out_w')


def _device_step(x, mem, target, W, layer_weights, layer_grads, start_after=()):
    ng = W['norm_g']
    lw = []
    for i in range(DEPTH):
        j = i // 2
        p = {}
        if i % 2 == 0:
            p['cw'] = _pad_taps(W['ssm_conv_w'][j], SSM_CONV)
            p['cb'] = _row(W['ssm_conv_b'][j])
            hp = jnp.stack([_group_pad_cols(W['ssm_dt_bias'][j]), _group_pad_cols(W['ssm_A_log'][j]),
                            _group_pad_cols(W['ssm_D'][j])])
            p['hp'] = jnp.pad(hp, ((0, SUBLANE - 3), (0, 0)))
            p['sng'] = _row(W['ssm_norm_g'][j])
        else:
            p['pw1b'] = _row(W['cf_pw1_b'][j])
            p['dww'], p['dwb'] = _pad_taps(W['cf_dw_w'][j], CF_KERNEL), _row(W['cf_dw_b'][j])
            p['lng'], p['lnb'] = _row(W['cf_ln_g'][j]), _row(W['cf_ln_b'][j])
            p['pw2b'] = _row(W['cf_pw2_b'][j])
        p['memg'] = _row(W['xa_mem_g'][i])
        p['fcw'], p['fcb'] = _pad_taps(W['ffn_conv_w'][i], FFN_CONV), _row(W['ffn_conv_b'][i])
        p['g'] = [_row(ng[i, s]) for s in range(6)]
        lw.append(p)

    def wmm(a, wl, wname, mode, **kw):
        return _mm(a, wl[wname], mode, b_shards=wname in COL_SHARDED, **kw)

    saved = []
    X = x
    h = _rmsnorm_fwd(X, lw[0]['g'][0], name="norm_in", after=start_after)
    for i in range(DEPTH):
        p, sv = lw[i], {}
        wl = dict(layer_weights(i, X))
        sv['X0'], sv['h'], sv['wl'] = X, h, wl
        if i % 2 == 0:
            win = jnp.concatenate([wl['ssm_in_w'][s] for s in range(N_CHIPS)], axis=1)
            wl['wz'], wl['wx'] = win[:, :D_INNER], win[:, D_INNER:D_INNER + CONV_DIM]
            wl['wdt'] = _group_pad_cols(win[:, D_INNER + CONV_DIM:])
            z = _mm(h, wl['wz'], "nn", name="ssm_z")
            xbc = _mm(h, wl['wx'], "nn", name="ssm_xbc")
            dtp = _mm(h, wl['wdt'], "nn", name="ssm_dt")
            act = _conv_act_fwd(xbc, p['cw'], p['cb'], K=SSM_CONV, act="silu", name="ssm_conv_fwd", out_dtype=f32)
            y, hs = _ssd_fwd(act, dtp, p['hp'], name="ssd_fwd")
            yn = _gated_norm_fwd(y, z, p['sng'], name="ssm_gnorm_fwd")
            mix = wmm(yn, wl, 'ssm_out_w', "nn", name="ssm_out")
            sv.update(z=z, xbc=xbc, dtp=dtp, act=act, y=y, hs=hs, yn=yn)
        else:
            u = wmm(h, wl, 'cf_pw1_w', "nn", name="cf_pw1", bias=p['pw1b'])
            c, s = _cf_fwd(u, p['dww'], p['dwb'], p['lng'], p['lnb'], name="cf_conv_fwd")
            mix = wmm(s, wl, 'cf_pw2_w', "nn", name="cf_pw2", bias=p['pw2b'])
            sv.update(u=u, c=c, s=s)
        X1, h2 = _resid_norm_fwd(X, mix, p['g'][1], p['g'][2], name="resid_norm_a")
        q = wmm(h2, wl, 'xa_q_w', "nn", name="xa_q", out_dtype=bf16)
        m = _rmsnorm_fwd(mem, p['memg'], name="xa_mem_norm")
        kv = wmm(m, wl, 'xa_kv_w', "nn", name="xa_kv", out_dtype=bf16)
        o = _attn_fwd(q, kv, name="xa_attn_fwd")
        a = wmm(o, wl, 'xa_o_w', "nn", name="xa_o")
        X2, h3 = _resid_norm_fwd(X1, a, p['g'][3], p['g'][4], name="resid_norm_b")
        u0 = wmm(h3, wl, 'ffn_in_w', "nn", name="ffn_in")
        fact = _conv_act_fwd(u0, p['fcw'], p['fcb'], K=FFN_CONV, act="swiglu", name="ffn_conv_fwd", out_dtype=bf16)
        f = wmm(fact, wl, 'ffn_out_w', "nn", name="ffn_out")
        g_next = lw[i + 1]['g'][0] if i + 1 < DEPTH else None
        X3, hn = _resid_norm_fwd(X2, f, p['g'][5], g_next, name="resid_norm_c" if g_next is not None else "resid_norm_last")
        sv.update(mix=mix, X1=X1, h2=h2, q=q, m=m, kv=kv, o=o, a=a, X2=X2, h3=h3, u0=u0, fact=fact, f=f)
        saved.append(sv)
        X, h = X3, hn

    sse, G = _loss_fwd_bwd(X, target, name="loss")

    small = [n for n in WEIGHT_NAMES if n not in MATMUL_WEIGHTS]
    gr = {n: [None] * W[n].shape[0] for n in small}

    def dwmm(gl, a, d, wname, *, name):
        if wname in COL_SHARDED:
            gl[wname] = _mm(a, d, "tn", name=name, out_dtype=bf16, out_shards=True)
        else:
            g = _mm(a, d, "tn", name=name, out_dtype=bf16)
            gl[wname] = g.reshape(N_CHIPS, g.shape[0] // N_CHIPS, g.shape[1])

    dng = [[None] * 6 for _ in range(DEPTH)]
    behind = ()
    for i in reversed(range(DEPTH)):
        p, sv, j = lw[i], saved[i], i // 2
        wl, gl = sv['wl'], {}
        df, dng[i][5], _ = _norm_bwd(sv['f'], p['g'][5], G, name="nb_f", out_dtype=bf16, after=behind)
        dwmm(gl, sv['fact'], df, 'ffn_out_w', name="ffn_out_dw")
        dfact = wmm(df, wl, 'ffn_out_w', "nt", name="ffn_out_dx")
        du, dcw, dcb = _conv_act_bwd(sv['u0'], [dfact], p['fcw'], p['fcb'], K=FFN_CONV, act="swiglu", name="ffn_conv_bwd")
        gr['ffn_conv_w'][i], gr['ffn_conv_b'][i] = dcw[:FFN_CONV], dcb[0]
        du0 = _conv_transpose(du, p['fcw'], K=FFN_CONV, name="ffn_conv_bwd_x")
        dwmm(gl, sv['h3'], du0, 'ffn_in_w', name="ffn_in_dw")
        dh3 = wmm(du0, wl, 'ffn_in_w', "nt", name="ffn_in_dx")
        G, dng[i][4], _ = _norm_bwd(sv['X2'], p['g'][4], dh3, name="nb_x2", add=G)
        da, dng[i][3], _ = _norm_bwd(sv['a'], p['g'][3], G, name="nb_a", out_dtype=bf16)
        dwmm(gl, sv['o'], da, 'xa_o_w', name="xa_o_dw")
        do = wmm(da, wl, 'xa_o_w', "nt", name="xa_o_dx", out_dtype=bf16)
        dq, dkv = _attn_bwd(sv['q'], sv['kv'], do, name="xa_attn_bwd")
        dwmm(gl, sv['h2'], dq, 'xa_q_w', name="xa_q_dw")
        dh2 = wmm(dq, wl, 'xa_q_w', "nt", name="xa_q_dx")
        dwmm(gl, sv['m'], dkv, 'xa_kv_w', name="xa_kv_dw")
        dm = wmm(dkv, wl, 'xa_kv_w', "nt", name="xa_kv_dx")
        _, dmg, _ = _norm_bwd(mem, p['memg'], dm, name="nb_mem")
        gr['xa_mem_g'][i] = dmg[0]
        G, dng[i][2], _ = _norm_bwd(sv['X1'], p['g'][2], dh2, name="nb_x1", add=G)
        dmix, dng[i][1], dmix_sum = _norm_bwd(sv['mix'], p['g'][1], G, name="nb_mix", out_dtype=bf16)
        if i % 2 == 0:
            dwmm(gl, sv['yn'], dmix, 'ssm_out_w', name="ssm_out_dw")
            dyn = wmm(dmix, wl, 'ssm_out_w', "nt", name="ssm_out_dx")
            dy, dz, dsng = _gated_norm_bwd(sv['y'], sv['z'], p['sng'], dyn, name="ssm_gnorm_bwd")
            gr['ssm_norm_g'][j] = dsng[0]
            dxs, dB, dC, ddtp, dhp = _ssd_bwd(sv['act'], sv['dtp'], p['hp'], dy, sv['hs'], name="ssd_bwd")
            gr['ssm_dt_bias'][j], gr['ssm_A_log'][j], gr['ssm_D'][j] = (_group_unpad_cols(dhp[r]) for r in range(3))
            dpre, dcw, dcb = _conv_act_bwd(sv['xbc'], [dxs, dB, dC], p['cw'], p['cb'], K=SSM_CONV, act="silu",
                                           name="ssm_conv_bwd")
            gr['ssm_conv_w'][j], gr['ssm_conv_b'][j] = dcw[:SSM_CONV], dcb[0]
            dxbc = _conv_transpose(dpre, p['cw'], K=SSM_CONV, name="ssm_conv_bwd_x")
            hh = sv['h']
            dwz = _mm(hh, dz, "tn", name="ssm_z_dw", out_dtype=bf16)
            dwx = _mm(hh, dxbc, "tn", name="ssm_xbc_dw", out_dtype=bf16)
            dwdt = _mm(hh, ddtp, "tn", name="ssm_dt_dw", out_dtype=bf16)
            din = jnp.concatenate([dwz, dwx, _group_unpad_cols(dwdt)], axis=1)
            gl['ssm_in_w'] = jnp.stack(jnp.split(din, N_CHIPS, axis=1))
            dh = _mm(dz, wl['wz'], "nt", name="ssm_z_dx")
            dh = _mm(dxbc, wl['wx'], "nt", name="ssm_xbc_dx", add=dh)
            dh = _mm(ddtp, wl['wdt'], "nt", name="ssm_dt_dx", add=dh)
        else:
            dwmm(gl, sv['s'], dmix, 'cf_pw2_w', name="cf_pw2_dw")
            gr['cf_pw2_b'][j] = dmix_sum[0]
            ds = wmm(dmix, wl, 'cf_pw2_w', "nt", name="cf_pw2_dx")
            dc, dlg, dlb = _cf_ln_bwd(sv['c'], p['lng'], p['lnb'], ds, name="cf_ln_bwd")
            gr['cf_ln_g'][j], gr['cf_ln_b'][j] = dlg[0], dlb[0]
            du, ddw, ddb, dus = _cf_glu_bwd(sv['u'], dc, p['dww'], name="cf_glu_bwd")
            gr['cf_dw_w'][j], gr['cf_dw_b'][j], gr['cf_pw1_b'][j] = ddw[:CF_KERNEL], ddb[0], dus[0]
            dwmm(gl, sv['h'], du, 'cf_pw1_w', name="cf_pw1_dw")
            dh = wmm(du, wl, 'cf_pw1_w', "nt", name="cf_pw1_dx")
        behind = tuple(layer_grads(i, gl))
        G, dng[i][0], _ = _norm_bwd(sv['X0'], p['g'][0], dh, name="nb_x0", add=G, after=behind)
    gr['norm_g'] = [jnp.concatenate(dng[i], axis=0) for i in range(DEPTH)]
    gsmall = {n: jnp.stack(gr[n]) for n in small}
    return sse, G, gsmall


MESH = pl.DeviceIdType.MESH
HBM_SPEC = pl.BlockSpec(memory_space=pltpu.HBM)


def _chip_peers(x, y):
    return [(1 - x, y), (x, 1 - y), (1 - x, 1 - y)]


def _all_gather_chips(buf, *, name):
    R, C = buf.shape

    def body(in_ref, out_ref, send_sems, recv_sems, local_sem):
        x, y, c = lax.axis_index("x"), lax.axis_index("y"), lax.axis_index("c")
        me = 2 * x + y
        mine = pltpu.make_async_copy(in_ref, out_ref.at[me], local_sem)
        mine.start()
        peers = _chip_peers(x, y)
        sends = []
        for k, (px, py) in enumerate(peers):
            cp = pltpu.make_async_remote_copy(src_ref=in_ref, dst_ref=out_ref.at[me], send_sem=send_sems.at[k],
                                              recv_sem=recv_sems.at[k], device_id=(px, py, c), device_id_type=MESH)
            cp.start()
            sends.append(cp)
        for k, (px, py) in enumerate(peers):
            pltpu.make_async_remote_copy(src_ref=in_ref, dst_ref=out_ref.at[2 * px + py], send_sem=send_sems.at[k],
                                         recv_sem=recv_sems.at[k], device_id=(px, py, c), device_id_type=MESH).wait_recv()
        for cp in sends:
            cp.wait_send()
        mine.wait()

    return pl.pallas_call(body, name=name, out_shape=S((N_CHIPS, R, C), buf.dtype), in_specs=[HBM_SPEC], out_specs=HBM_SPEC,
                          scratch_shapes=[pltpu.SemaphoreType.DMA((3,)), pltpu.SemaphoreType.DMA((3,)),
                                          pltpu.SemaphoreType.DMA(())])(buf)


def _remote(src, dst, send_sem, recv_sem, device):
    return pltpu.make_async_remote_copy(src_ref=src, dst_ref=dst, send_sem=send_sem, recv_sem=recv_sem,
                                        device_id=device, device_id_type=MESH)


def _gather_matmul_weights(shards, *, name):
    n = len(shards)

    def body(*refs):
        ins, outs = refs[:n], refs[n:2 * n]
        send, recv, fsend, frecv, lsem = refs[2 * n:]
        x, y, c = lax.axis_index("x"), lax.axis_index("y"), lax.axis_index("c")
        me, sib = 2 * x + y, (x, y, 1 - c)
        peers = _chip_peers(x, y)
        started, local = [], []
        for w in range(n):
            cp = pltpu.make_async_copy(ins[w], outs[w].at[:, me], lsem.at[w])
            cp.start()
            local.append(cp)
            for k, (px, py) in enumerate(peers):
                cp = _remote(ins[w].at[:, c], outs[w].at[:, me, c], send.at[w, k], recv.at[w, k], (px, py, c))
                cp.start()
                started.append(cp)
        for w in range(n):
            for k, (px, py) in enumerate(peers):
                landed = outs[w].at[:, 2 * px + py, c]
                _remote(ins[w].at[:, c], landed, send.at[w, k], recv.at[w, k], (px, py, c)).wait_recv()
                cp = _remote(landed, landed, fsend.at[w, k], frecv.at[w, k], sib)
                cp.start()
                started.append(cp)
        for w in range(n):
            for k, (px, py) in enumerate(peers):
                _remote(ins[w].at[:, c], outs[w].at[:, 2 * px + py, 1 - c], fsend.at[w, k], frecv.at[w, k], sib).wait_recv()
        for cp in started:
            cp.wait_send()
        for cp in local:
            cp.wait()

    out_shape = tuple(S((s.shape[0], N_CHIPS) + s.shape[1:], s.dtype) for s in shards)
    sems = [pltpu.SemaphoreType.DMA((n, 3)) for _ in range(4)] + [pltpu.SemaphoreType.DMA((n,))]
    return pl.pallas_call(body, name=name, out_shape=out_shape, in_specs=[HBM_SPEC] * n, out_specs=(HBM_SPEC,) * n,
                          scratch_shapes=sems)(*shards)


SEM_SPEC = pl.BlockSpec(memory_space=pltpu.SEMAPHORE)
VMEM_SPEC = pl.BlockSpec(memory_space=pltpu.VMEM)


def _in_hbm(a):
    return pltpu.with_memory_space_constraint(a, pltpu.HBM)


def _spread_start(srcs, scatter, *, name):
    n = len(srcs)
    lands = [lax.empty((N_CHIPS,) + (s.shape[1:] if scatter else s.shape), s.dtype) for s in srcs]

    def body(*refs):
        src, land, send, recv, token = refs[:n], refs[n:2 * n], refs[2 * n], refs[2 * n + 1], refs[-1]
        x, y, c = lax.axis_index("x"), lax.axis_index("y"), lax.axis_index("c")
        me = 2 * x + y
        for w in range(n):
            for k, (px, py) in enumerate(_chip_peers(x, y)):
                block = src[w].at[2 * px + py] if scatter else src[w]
                _remote(block, land[w].at[me], send.at[3 * w + k], recv.at[3 * w + k], (px, py, c)).start()
        token[...] = jnp.zeros_like(token)

    thru = tuple(pltpu.HBM(a.shape, a.dtype) for a in list(srcs) + lands)
    out = pl.pallas_call(
        body, name=name,
        out_shape=(pltpu.SemaphoreType.DMA((3 * n,)), pltpu.SemaphoreType.DMA((3 * n,))) + thru + (S((SUBLANE, LANE), f32),),
        in_specs=[HBM_SPEC] * (2 * n), out_specs=(SEM_SPEC, SEM_SPEC) + (HBM_SPEC,) * (2 * n) + (VMEM_SPEC,),
        input_output_aliases={i: 2 + i for i in range(2 * n)},
        compiler_params=pltpu.CompilerParams(has_side_effects=pltpu.SideEffectType.DATAFLOW_SIDE_EFFECTING),
    )(*[_in_hbm(a) for a in list(srcs) + lands])
    return out[0], out[1], out[2:2 + n], out[2 + n:2 + 2 * n], out[-1]


def _spread_wait(send, recv, srcs, lands, after, scatter, *, name):
    n = len(srcs)

    def body(*refs):
        src, land, send, recv = refs[:n], refs[n:2 * n], refs[2 * n], refs[2 * n + 1]
        x, y, c = lax.axis_index("x"), lax.axis_index("y"), lax.axis_index("c")
        me = 2 * x + y
        for w in range(n):
            for k, (px, py) in enumerate(_chip_peers(x, y)):
                block = src[w].at[me] if scatter else src[w]
                cp = _remote(block, land[w].at[2 * px + py], send.at[3 * w + k], recv.at[3 * w + k], (px, py, c))
                cp.wait_send()
                cp.wait_recv()

    thru = tuple(pltpu.HBM(a.shape, a.dtype) for a in list(srcs) + list(lands))
    out = pl.pallas_call(
        body, name=name, out_shape=thru,
        in_specs=[HBM_SPEC] * (2 * n) + [SEM_SPEC, SEM_SPEC] + [ANY_SPEC] * len(after), out_specs=(HBM_SPEC,) * (2 * n),
        input_output_aliases={i: i for i in range(2 * n)},
        compiler_params=pltpu.CompilerParams(has_side_effects=pltpu.SideEffectType.DATAFLOW_SIDE_EFFECTING),
    )(*srcs, *lands, send, recv, *after)
    return out[:n], out[n:]


def _place_own(lands, srcs, scatter, *, name):
    n = len(lands)

    def body(*refs):
        src, out, sem = refs[n:2 * n], refs[2 * n:3 * n], refs[-1]
        me = 2 * lax.axis_index("x") + lax.axis_index("y")
        copies = [pltpu.make_async_copy(src[w].at[me] if scatter else src[w], out[w].at[me], sem.at[w]) for w in range(n)]
        for cp in copies:
            cp.start()
        for cp in copies:
            cp.wait()

    return pl.pallas_call(body, name=name, out_shape=tuple(S(a.shape, a.dtype) for a in lands),
                          in_specs=[HBM_SPEC] * (2 * n), out_specs=(HBM_SPEC,) * n,
                          input_output_aliases={i: i for i in range(n)},
                          scratch_shapes=[pltpu.SemaphoreType.DMA((n,))])(*lands, *srcs)


def _swap_sibling(bufs, *, name):
    n = len(bufs)

    def body(*refs):
        src, out, send, recv = refs[:n], refs[n:2 * n], refs[-2], refs[-1]
        sib = (lax.axis_index("x"), lax.axis_index("y"), 1 - lax.axis_index("c"))
        copies = [_remote(src[w], out[w], send.at[w], recv.at[w], sib) for w in range(n)]
        for cp in copies:
            cp.start()
        for cp in copies:
            cp.wait()

    return pl.pallas_call(body, name=name, out_shape=tuple(S(a.shape, a.dtype) for a in bufs),
                          in_specs=[HBM_SPEC] * n, out_specs=(HBM_SPEC,) * n,
                          scratch_shapes=[pltpu.SemaphoreType.DMA((n,)), pltpu.SemaphoreType.DMA((n,))])(*bufs)


N_DEVICES = 8


def _allgather_devices(buf, *, name):
    R, C = buf.shape

    def body(in_ref, out_ref, send, recv, lsem):
        x, y, c = lax.axis_index("x"), lax.axis_index("y"), lax.axis_index("c")
        me = 4 * x + 2 * y + c
        mine = pltpu.make_async_copy(in_ref, out_ref.at[me], lsem)
        mine.start()
        flips = [(d >> 2 & 1, d >> 1 & 1, d & 1) for d in range(1, N_DEVICES)]
        peers = [(1 - x if fx else x, 1 - y if fy else y, 1 - c if fc else c) for fx, fy, fc in flips]
        sends = []
        for k, peer in enumerate(peers):
            cp = _remote(in_ref, out_ref.at[me], send.at[k], recv.at[k], peer)
            cp.start()
            sends.append(cp)
        for k, (px, py, pc) in enumerate(peers):
            _remote(in_ref, out_ref.at[4 * px + 2 * py + pc], send.at[k], recv.at[k], (px, py, pc)).wait_recv()
        for cp in sends:
            cp.wait_send()
        mine.wait()

    return pl.pallas_call(body, name=name, out_shape=S((N_DEVICES, R, C), buf.dtype), in_specs=[HBM_SPEC], out_specs=HBM_SPEC,
                          scratch_shapes=[pltpu.SemaphoreType.DMA((N_DEVICES - 1,)), pltpu.SemaphoreType.DMA((N_DEVICES - 1,)),
                                          pltpu.SemaphoreType.DMA(())])(buf)


def _sum_slots(buf, *, name):
    ns, R, C = buf.shape
    tr = _pick(R, 512)
    assert R % tr == 0

    def body(*refs):
        acc = refs[0][...]
        for r in refs[1:ns]:
            acc = acc + r[...]
        refs[ns][...] = acc

    specs = [pl.BlockSpec((None, tr, C), functools.partial(lambda s, i: (s, i, 0), s)) for s in range(ns)]
    return pl.pallas_call(body, name=name, out_shape=S((R, C), buf.dtype), grid=(R // tr,), in_specs=specs,
                          out_specs=pl.BlockSpec((tr, C), lambda i: (i, 0)), compiler_params=_cp("parallel"))(*([buf] * ns))


ADAMW_BLOCK_BYTES = 1 << 20


def _adamw(w, m, v, groups, *, name, layer=None, prev=None):
    shape = w.shape if layer is None else w.shape[1:]
    C = shape[-1]
    Rr = math.prod(shape[:-1])
    tr = Rr
    if Rr * C * 4 > ADAMW_BLOCK_BYTES:
        tr = max(t for t in range(2 * SUBLANE, Rr + 1, 2 * SUBLANE) if Rr % t == 0 and t * C * 4 <= ADAMW_BLOCK_BYTES)
    c1 = 1.0 / (1.0 - ADAM_B1 ** ADAM_STEP)
    c2 = 1.0 / (1.0 - ADAM_B2 ** ADAM_STEP)
    if layer is None:
        to2 = lambda t: t.reshape(Rr, C)
        spec = pl.BlockSpec((tr, C), lambda i: (i, 0))
        res_shape = S((Rr, C), f32)
    else:
        to2 = lambda t: t.reshape(layer[1], Rr, C)
        spec = pl.BlockSpec((None, tr, C), functools.partial(lambda l, i: (l, i, 0), layer[0]))
        res_shape = S((layer[1], Rr, C), f32)
    wspec, spec = spec, pl.BlockSpec((tr, C), lambda i: (i, 0))
    g_specs, g_args, sizes = [], [], []
    for grp in groups:
        sizes.append(len(grp))
        for term in grp:
            if isinstance(term, tuple):
                arr, slot = term
                g_specs.append(pl.BlockSpec((None, tr, C), functools.partial(lambda s, i: (s, i, 0), slot)))
                g_args.append(arr.reshape(arr.shape[0], Rr, C))
            else:
                g_specs.append(spec)
                g_args.append(term.reshape(Rr, C))
    nterms = len(g_args)
    prev = () if prev is None else tuple(to2(t) for t in prev)

    def body(w_ref, m_ref, v_ref, *rest):
        t_refs, (g_ref, d_ref, mo_ref, vo_ref) = rest[:nterms], rest[-4:]
        g, pos = None, 0
        for size in sizes:
            part = None
            for r in t_refs[pos:pos + size]:
                t = r[...].astype(f32)
                part = t if part is None else part + t
            pos += size
            g = part if g is None else g + part
        mn = ADAM_B1 * m_ref[...] + (1.0 - ADAM_B1) * g
        vn = ADAM_B2 * v_ref[...] + (1.0 - ADAM_B2) * (g * g)
        g_ref[...] = g
        mo_ref[...] = mn
        vo_ref[...] = vn
        d_ref[...] = -ADAM_LR * ((mn * c1) / (jnp.sqrt(vn * c2) + ADAM_EPS) + ADAM_WD * w_ref[...])

    out = pl.pallas_call(body, name=name, out_shape=(res_shape,) * 4, grid=(Rr // tr,),
                         in_specs=[wspec] * 3 + g_specs + [ANY_SPEC] * len(prev), out_specs=(wspec,) * 4,
                         input_output_aliases={3 + nterms + k: k for k in range(len(prev))},
                         compiler_params=_cp("parallel"))(to2(w), to2(m), to2(v), *g_args, *prev)
    return tuple(o.reshape(w.shape) for o in out)


def _pack_rows(parts, dtype):
    flat = jnp.concatenate([p.reshape(-1).astype(dtype) for p in parts])
    n = flat.shape[0]
    unit = PACK_COLS * 2 * SUBLANE
    padded = -(-n // unit) * unit
    return jnp.pad(flat, (0, padded - n)).reshape(padded // PACK_COLS, PACK_COLS)


def _unpack_rows(flat2d, shapes):
    flat = flat2d.reshape(-1)
    out, off = [], 0
    for shp in shapes:
        n = math.prod(shp)
        out.append(flat[off:off + n].reshape(shp))
        off += n
    return out


def _gather_weights(local, names, dtype, *, name):
    shapes = [local[n].shape for n in names]
    got = _all_gather_chips(_pack_rows([local[n] for n in names], dtype), name=name)
    per_chip = [_unpack_rows(got[s], shapes) for s in range(N_CHIPS)]
    return {n: jnp.concatenate([per_chip[s][k] for s in range(N_CHIPS)], axis=SHARD_AXIS[n]) for k, n in enumerate(names)}


def kernel(x, mem, norm_g, ssm_in_w, ssm_conv_w, ssm_conv_b, ssm_dt_bias, ssm_A_log, ssm_D, ssm_norm_g, ssm_out_w, cf_pw1_w, cf_pw1_b, cf_dw_w, cf_dw_b, cf_ln_g, cf_ln_b, cf_pw2_w, cf_pw2_b, xa_mem_g, xa_q_w, xa_kv_w, xa_o_w, ffn_in_w, ffn_conv_w, ffn_conv_b, ffn_out_w, loss_target, m_norm_g, m_ssm_in_w, m_ssm_conv_w, m_ssm_conv_b, m_ssm_dt_bias, m_ssm_A_log, m_ssm_D, m_ssm_norm_g, m_ssm_out_w, m_cf_pw1_w, m_cf_pw1_b, m_cf_dw_w, m_cf_dw_b, m_cf_ln_g, m_cf_ln_b, m_cf_pw2_w, m_cf_pw2_b, m_xa_mem_g, m_xa_q_w, m_xa_kv_w, m_xa_o_w, m_ffn_in_w, m_ffn_conv_w, m_ffn_conv_b, m_ffn_out_w, v_norm_g, v_ssm_in_w, v_ssm_conv_w, v_ssm_conv_b, v_ssm_dt_bias, v_ssm_A_log, v_ssm_D, v_ssm_norm_g, v_ssm_out_w, v_cf_pw1_w, v_cf_pw1_b, v_cf_dw_w, v_cf_dw_b, v_cf_ln_g, v_cf_ln_b, v_cf_pw2_w, v_cf_pw2_b, v_xa_mem_g, v_xa_q_w, v_xa_kv_w, v_xa_o_w, v_ffn_in_w, v_ffn_conv_w, v_ffn_conv_b, v_ffn_out_w):
    w_local = dict(zip(WEIGHT_NAMES, (norm_g, ssm_in_w, ssm_conv_w, ssm_conv_b, ssm_dt_bias, ssm_A_log, ssm_D, ssm_norm_g,
                                      ssm_out_w, cf_pw1_w, cf_pw1_b, cf_dw_w, cf_dw_b, cf_ln_g, cf_ln_b, cf_pw2_w, cf_pw2_b,
                                      xa_mem_g, xa_q_w, xa_kv_w, xa_o_w, ffn_in_w, ffn_conv_w, ffn_conv_b, ffn_out_w)))
    m_local = dict(zip(WEIGHT_NAMES, (m_norm_g, m_ssm_in_w, m_ssm_conv_w, m_ssm_conv_b, m_ssm_dt_bias, m_ssm_A_log, m_ssm_D,
                                      m_ssm_norm_g, m_ssm_out_w, m_cf_pw1_w, m_cf_pw1_b, m_cf_dw_w, m_cf_dw_b, m_cf_ln_g,
                                      m_cf_ln_b, m_cf_pw2_w, m_cf_pw2_b, m_xa_mem_g, m_xa_q_w, m_xa_kv_w, m_xa_o_w,
                                      m_ffn_in_w, m_ffn_conv_w, m_ffn_conv_b, m_ffn_out_w)))
    v_local = dict(zip(WEIGHT_NAMES, (v_norm_g, v_ssm_in_w, v_ssm_conv_w, v_ssm_conv_b, v_ssm_dt_bias, v_ssm_A_log, v_ssm_D,
                                      v_ssm_norm_g, v_ssm_out_w, v_cf_pw1_w, v_cf_pw1_b, v_cf_dw_w, v_cf_dw_b, v_cf_ln_g,
                                      v_cf_ln_b, v_cf_pw2_w, v_cf_pw2_b, v_xa_mem_g, v_xa_q_w, v_xa_kv_w, v_xa_o_w,
                                      v_ffn_in_w, v_ffn_conv_w, v_ffn_conv_b, v_ffn_out_w)))

    small = [n for n in WEIGHT_NAMES if n not in MATMUL_WEIGHTS]
    small_sharded = [n for n in small if SHARD_AXIS[n] is not None]
    W = {n: w_local[n] for n in small if SHARD_AXIS[n] is None}
    W.update(_gather_weights(w_local, small_sharded, f32, name="gather_small_weights"))

    def layer_index(n, i):
        return i // 2 if n in ('ssm_in_w', 'ssm_out_w', 'cf_pw1_w', 'cf_pw2_w') else i

    def layer_shards(i):
        return [w_local[n][layer_index(n, i)].astype(bf16) for n in _layer_matmul_weights(i)]

    halves0 = [s.reshape(1, 2, s.shape[0] // 2, s.shape[1]) for s in layer_shards(0)]
    got0 = _gather_matmul_weights(halves0, name="gather_layer0")
    gathers, tokens = {}, []
    for i in range(1, DEPTH):
        send, recv, srcs, lands, token = _spread_start(layer_shards(i), False, name="gather_start_%d" % i)
        gathers[i] = (send, recv, srcs, lands)
        tokens.append(token)

    def layer_weights(i, after):
        names = _layer_matmul_weights(i)
        if i == 0:
            lands = [g.reshape((N_CHIPS, 2 * g.shape[3], g.shape[4])) for g in got0]
        else:
            srcs, lands = _spread_wait(*gathers[i], (after,), False, name="gather_wait_%d" % i)
            lands = _place_own(lands, srcs, False, name="gather_own_%d" % i)
        return {n: (a.reshape(N_CHIPS * a.shape[1], a.shape[2]) if n in ROW_SHARDED else a) for n, a in zip(names, lands)}

    scatters = {}

    def layer_grads(i, gl):
        send, recv, srcs, lands, token = _spread_start([gl[n] for n in _layer_matmul_weights(i)], True,
                                                       name="grads_start_%d" % i)
        scatters[i] = (send, recv, srcs, lands)
        return (token,)

    sse, gx, gsmall = _device_step(x[0], mem[0], loss_target[0], W, layer_weights, layer_grads, tuple(tokens))

    loss = lax.psum(0.5 * sse[0, 0] / D_MODEL, ("x", "y", "c"))

    own = {}
    for i in reversed(range(DEPTH)):
        srcs, lands = _spread_wait(*scatters[i], (gx,), True, name="grads_wait_%d" % i)
        lands = _place_own(lands, srcs, True, name="grads_own_%d" % i)
        for n, a in zip(_layer_matmul_weights(i), lands):
            own[n, layer_index(n, i)] = a
    keys = sorted(own)
    sib = dict(zip(keys, _swap_sibling([own[k] for k in keys], name="grads_swap_sibling")))
    small_shapes = [gsmall[n].shape for n in small]
    slots = _allgather_devices(_pack_rows([gsmall[n] for n in small], f32), name="allgather_small_grads")
    gsum = dict(zip(small, _unpack_rows(_sum_slots(slots, name="sum_small_grads"), small_shapes)))
    chip = 2 * lax.axis_index("x") + lax.axis_index("y")

    res = {}
    for n in MATMUL_WEIGHTS:
        layers, out = w_local[n].shape[0], None
        for l in range(layers):
            groups = [[(own[n, l], s) for s in range(N_CHIPS)], [(sib[n, l], s) for s in range(N_CHIPS)]]
            out = _adamw(w_local[n], m_local[n], v_local[n], groups, name="adamw_%s_%d" % (n, l), layer=(l, layers), prev=out)
        res[n] = out
    for n in small:
        g, ax = gsum[n], SHARD_AXIS[n]
        if ax is not None:
            width = w_local[n].shape[ax]
            g = lax.dynamic_slice_in_dim(g, chip * width, width, axis=ax)
        res[n] = _adamw(w_local[n], m_local[n], v_local[n], [[g]], name="adamw_" + n)
    return (loss, gx[None], *[res[n][0] for n in WEIGHT_NAMES], *[res[n][1] for n in WEIGHT_NAMES],
            *[res[n][2] for n in WEIGHT_NAMES], *[res[n][3] for n in WEIGHT_NAMES])
```

```python
import functools
import math

import jax
import jax.numpy as jnp
from jax import lax
from jax.experimental import pallas as pl
from jax.experimental.pallas import tpu as pltpu

f32 = jnp.float32
bf16 = jnp.bfloat16
S = jax.ShapeDtypeStruct

D_MODEL = 1024
DEPTH = 4
D_INNER = 2048
HEAD_DIM = 64
N_GROUPS = 4
HEADS_PER_GROUP = 8
N_SSM_HEADS = 32
D_STATE = 128
CHUNK = 128
SSM_CONV = 4
CONV_DIM = 3072
CF_KERNEL = 31
N_MEM = 256
XA_HEADS = 4
XA_HEAD_DIM = 256
D_FF = 2816
FFN_CONV = 3
EPS = 1e-6
ADAM_LR, ADAM_B1, ADAM_B2, ADAM_EPS, ADAM_WD, ADAM_STEP = 0.001, 0.9, 0.999, 1e-08, 0.01, 10

LANE = 128
SUBLANE = 8
VMEM_LIMIT = 56 * 1024 * 1024
N_CHIPS = 4
PACK_COLS = 1024

WEIGHT_NAMES = ['norm_g', 'ssm_in_w', 'ssm_conv_w', 'ssm_conv_b', 'ssm_dt_bias', 'ssm_A_log', 'ssm_D', 'ssm_norm_g',
                'ssm_out_w', 'cf_pw1_w', 'cf_pw1_b', 'cf_dw_w', 'cf_dw_b', 'cf_ln_g', 'cf_ln_b', 'cf_pw2_w', 'cf_pw2_b',
                'xa_mem_g', 'xa_q_w', 'xa_kv_w', 'xa_o_w', 'ffn_in_w', 'ffn_conv_w', 'ffn_conv_b', 'ffn_out_w']
SHARD_AXIS = {'norm_g': 2, 'ssm_in_w': 2, 'ssm_conv_w': 2, 'ssm_conv_b': None, 'ssm_dt_bias': None, 'ssm_A_log': None,
              'ssm_D': None, 'ssm_norm_g': None, 'ssm_out_w': 1, 'cf_pw1_w': 2, 'cf_pw1_b': 1, 'cf_dw_w': 2, 'cf_dw_b': 1,
              'cf_ln_g': 1, 'cf_ln_b': 1, 'cf_pw2_w': 1, 'cf_pw2_b': 1, 'xa_mem_g': None, 'xa_q_w': 1, 'xa_kv_w': 2,
              'xa_o_w': 1, 'ffn_in_w': 2, 'ffn_conv_w': 2, 'ffn_conv_b': None, 'ffn_out_w': 1}
MATMUL_WEIGHTS = ('ssm_in_w', 'ssm_out_w', 'cf_pw1_w', 'cf_pw2_w', 'xa_q_w', 'xa_kv_w', 'xa_o_w', 'ffn_in_w', 'ffn_out_w')


def _cp(*sem):
    return pltpu.CompilerParams(dimension_semantics=tuple(sem), vmem_limit_bytes=VMEM_LIMIT)


def _pick(dim, pref):
    if dim <= pref:
        return dim
    best = None
    for t in range(LANE, pref + 1, LANE):
        if dim % t == 0:
            best = t
    assert best is not None, (dim, pref)
    return best


def _sigmoid(x):
    return 1.0 / (1.0 + jnp.exp(-x))


def _silu(x):
    return x * _sigmoid(x)


def _dsilu(x):
    s = _sigmoid(x)
    return s * (1.0 + x * (1.0 - s))


def _softplus(x):
    return jnp.maximum(x, 0.0) + jnp.log(1.0 + jnp.exp(-jnp.abs(x)))


_DN = {"nn": (((1,), (0,)), ((), ())), "nt": (((1,), (1,)), ((), ())), "tn": (((0,), (0,)), ((), ()))}


def _mm(a, b, mode, *, name, out_dtype=f32, bias=None, add=None, b_shards=False, out_shards=False):
    bshape = (b.shape[1], b.shape[2] * N_CHIPS) if b_shards else b.shape
    if mode == "nn":
        (M, K), (K2, N) = a.shape, bshape
    elif mode == "nt":
        (M, K), (N, K2) = a.shape, bshape
    else:
        (K, M), (K2, N) = a.shape, bshape
    assert K == K2, (a.shape, b.shape, mode)
    n_unit = N // N_CHIPS if ((b_shards and mode == "nn") or out_shards) else N
    k_unit = K // N_CHIPS if (b_shards and mode == "nt") else K
    tm, tn, tk = _pick(M, 1024), _pick(n_unit, 1408), _pick(k_unit, 1408)
    nk, nj_u, nk_u = K // tk, n_unit // tn, k_unit // tk
    a_spec = {"nn": pl.BlockSpec((tm, tk), lambda i, j, k: (i, k)), "nt": pl.BlockSpec((tm, tk), lambda i, j, k: (i, k)),
              "tn": pl.BlockSpec((tk, tm), lambda i, j, k: (k, i))}[mode]
    if not b_shards:
        b_spec = {"nn": pl.BlockSpec((tk, tn), lambda i, j, k: (k, j)), "nt": pl.BlockSpec((tn, tk), lambda i, j, k: (j, k)),
                  "tn": pl.BlockSpec((tk, tn), lambda i, j, k: (k, j))}[mode]
    else:
        b_spec = {"nn": pl.BlockSpec((None, tk, tn), lambda i, j, k: (j // nj_u, k, j % nj_u)),
                  "nt": pl.BlockSpec((None, tn, tk), lambda i, j, k: (k // nk_u, j, k % nk_u))}[mode]
    in_specs, args = [a_spec, b_spec], [a, b]
    if bias is not None:
        in_specs.append(pl.BlockSpec((1, tn), lambda i, j, k: (0, j)))
        args.append(bias)
    if add is not None:
        in_specs.append(pl.BlockSpec((tm, tn), lambda i, j, k: (i, j)))
        args.append(add)
    if not out_shards:
        out_shape, out_spec = S((M, N), out_dtype), pl.BlockSpec((tm, tn), lambda i, j, k: (i, j))
    else:
        out_shape = S((N_CHIPS, M, n_unit), out_dtype)
        out_spec = pl.BlockSpec((None, tm, tn), lambda i, j, k: (j // nj_u, i, j % nj_u))
    dn = _DN[mode]
    has_bias, has_add = bias is not None, add is not None

    def body(a_ref, b_ref, *rest):
        rest = list(rest)
        bias_ref = rest.pop(0) if has_bias else None
        add_ref = rest.pop(0) if has_add else None
        o_ref, acc_ref = rest
        k = pl.program_id(2)

        @pl.when(k == 0)
        def _():
            acc_ref[...] = jnp.zeros_like(acc_ref)

        acc_ref[...] += lax.dot_general(a_ref[...].astype(bf16), b_ref[...].astype(bf16), dn, preferred_element_type=f32)

        @pl.when(k == nk - 1)
        def _():
            r = acc_ref[...]
            if has_bias:
                r = r + bias_ref[...]
            if has_add:
                r = r + add_ref[...].astype(f32)
            o_ref[...] = r.astype(out_dtype)

    return pl.pallas_call(
        body, name=name, out_shape=out_shape, grid=(M // tm, N // tn, nk),
        in_specs=in_specs, out_specs=out_spec, scratch_shapes=[pltpu.VMEM((tm, tn), f32)],
        compiler_params=_cp("parallel", "parallel", "arbitrary"))(*args)


def _rows(tm, C):
    return pl.BlockSpec((tm, C), lambda i: (i, 0))


def _const(shape):
    return pl.BlockSpec(shape, lambda i: tuple(0 for _ in shape))


def _rms_val(x, g):
    r = lax.rsqrt(jnp.mean(x * x, axis=-1, keepdims=True) + EPS)
    return x * r * g


def _rms_bwd_val(x, g, dy):
    r = lax.rsqrt(jnp.mean(x * x, axis=-1, keepdims=True) + EPS)
    xn = x * r
    dxh = dy * g
    dx = r * (dxh - xn * jnp.mean(dxh * xn, axis=-1, keepdims=True))
    return dx, jnp.sum(dy * xn, axis=0, keepdims=True)


ANY_SPEC = pl.BlockSpec(memory_space=pl.ANY)


def _rmsnorm_fwd(x, g, *, name, after=()):
    L, C = x.shape
    tm = _pick(L, 512)

    def body(x_ref, g_ref, *rest):
        rest[-1][...] = _rms_val(x_ref[...], g_ref[...]).astype(bf16)

    return pl.pallas_call(body, name=name, out_shape=S((L, C), bf16), grid=(L // tm,),
                          in_specs=[_rows(tm, C), _const((1, C))] + [ANY_SPEC] * len(after), out_specs=_rows(tm, C),
                          compiler_params=_cp("parallel"))(x, g, *after)


def _resid_norm_fwd(x, mix, g_post, g_next, *, name):
    L, C = x.shape
    tm = _pick(L, 512)
    want_h = g_next is not None

    def body(x_ref, m_ref, gp_ref, *rest):
        xn = x_ref[...] + _rms_val(m_ref[...], gp_ref[...])
        if want_h:
            gn_ref, xo_ref, h_ref = rest
            h_ref[...] = _rms_val(xn, gn_ref[...]).astype(bf16)
        else:
            (xo_ref,) = rest
        xo_ref[...] = xn

    in_specs = [_rows(tm, C), _rows(tm, C), _const((1, C))]
    args = [x, mix, g_post]
    out_shape, out_specs = [S((L, C), f32)], [_rows(tm, C)]
    if want_h:
        in_specs.append(_const((1, C)))
        args.append(g_next)
        out_shape.append(S((L, C), bf16))
        out_specs.append(_rows(tm, C))
    out = pl.pallas_call(body, name=name, out_shape=tuple(out_shape), grid=(L // tm,), in_specs=in_specs,
                         out_specs=tuple(out_specs), compiler_params=_cp("parallel"))(*args)
    return (out[0], out[1]) if want_h else (out[0], None)


def _norm_bwd(x, g, dy, *, name, add=None, out_dtype=f32, after=()):
    L, C = x.shape
    tm = _pick(L, 512)
    has_add = add is not None

    def body(x_ref, g_ref, dy_ref, *rest):
        rest = list(rest)
        add_ref = rest.pop(0) if has_add else None
        dx_ref, dg_ref, cs_ref = rest[-3:]
        i = pl.program_id(0)

        @pl.when(i == 0)
        def _():
            dg_ref[...] = jnp.zeros_like(dg_ref)
            cs_ref[...] = jnp.zeros_like(cs_ref)

        dx, dg = _rms_bwd_val(x_ref[...], g_ref[...], dy_ref[...].astype(f32))
        dg_ref[...] += dg
        cs_ref[...] += jnp.sum(dx, axis=0, keepdims=True)
        if has_add:
            dx = dx + add_ref[...]
        dx_ref[...] = dx.astype(out_dtype)

    in_specs = [_rows(tm, C), _const((1, C)), _rows(tm, C)]
    args = [x, g, dy]
    if has_add:
        in_specs.append(_rows(tm, C))
        args.append(add)
    in_specs += [ANY_SPEC] * len(after)
    args += list(after)
    return pl.pallas_call(body, name=name, out_shape=(S((L, C), out_dtype), S((1, C), f32), S((1, C), f32)),
                          grid=(L // tm,), in_specs=in_specs,
                          out_specs=(_rows(tm, C), _const((1, C)), _const((1, C))),
                          compiler_params=_cp("arbitrary"))(*args)


def _loss_fwd_bwd(y, target, *, name):
    L, C = y.shape
    tm = _pick(L, 512)

    def body(y_ref, t_ref, acc_ref, dy_ref):
        i = pl.program_id(0)

        @pl.when(i == 0)
        def _():
            acc_ref[...] = jnp.zeros_like(acc_ref)

        e = y_ref[...] - t_ref[...]
        rs = jnp.sum(e * e, axis=-1, keepdims=True)
        acc_ref[...] += jnp.broadcast_to(jnp.sum(rs, axis=0, keepdims=True), (1, LANE))
        dy_ref[...] = e * (1.0 / C)

    return pl.pallas_call(body, name=name, out_shape=(S((1, LANE), f32), S((L, C), f32)), grid=(L // tm,),
                          in_specs=[_rows(tm, C), _rows(tm, C)], out_specs=(_const((1, LANE)), _rows(tm, C)),
                          compiler_params=_cp("arbitrary"))(y, target)


def _halo_rows(K):
    return SUBLANE if K - 1 <= SUBLANE else 32


def _prev_halo_spec(tm, H, C):
    return pl.BlockSpec((H, C), lambda i: (jnp.maximum(i * (tm // H) - 1, 0), 0))


def _next_halo_spec(tm, H, C, L):
    return pl.BlockSpec((H, C), lambda i: (jnp.minimum((i + 1) * (tm // H), L // H - 1), 0))


def _shift_down(ext, s):
    return ext if s == 0 else pltpu.roll(ext, s, axis=0)


def _shift_up(ext, s):
    return ext if s == 0 else pltpu.roll(ext, ext.shape[0] - s, axis=0)


def _causal_conv(ext, H, w_ref, K):
    acc = None
    for k in range(K):
        term = _shift_down(ext, K - 1 - k)[H:, :] * w_ref[k:k + 1, :]
        acc = term if acc is None else acc + term
    return acc


def _anticausal_conv(ext, tm, w_ref, K):
    acc = None
    for k in range(K):
        term = _shift_up(ext, K - 1 - k)[:tm, :] * w_ref[k:k + 1, :]
        acc = term if acc is None else acc + term
    return acc


def _tap_grads(dw_ref, d_cur, x_ext, H, K):
    for k in range(K):
        dw_ref[k:k + 1, :] += jnp.sum(d_cur * _shift_down(x_ext, K - 1 - k)[H:, :], axis=0, keepdims=True)


def _pad_taps(w, K):
    return jnp.pad(w, ((0, _halo_rows(K) - K), (0, 0)))


def _conv_act_fwd(x, w, b, *, K, act, name, out_dtype, tm_pref=256):
    L, C = x.shape
    H = _halo_rows(K)
    tm = _pick(L, tm_pref)
    Co = C if act == "silu" else C // 2

    def body(h_ref, x_ref, w_ref, b_ref, o_ref):
        i = pl.program_id(0)
        halo = jnp.where(i > 0, h_ref[...], 0.0)
        ext = jnp.concatenate([halo, x_ref[...]], axis=0)
        u = _causal_conv(ext, H, w_ref, K) + b_ref[...]
        if act == "silu":
            o_ref[...] = _silu(u).astype(out_dtype)
        else:
            o_ref[...] = (_silu(u[:, :Co]) * u[:, Co:]).astype(out_dtype)

    return pl.pallas_call(body, name=name, out_shape=S((L, Co), out_dtype), grid=(L // tm,),
                          in_specs=[_prev_halo_spec(tm, H, C), _rows(tm, C), _const((H, C)), _const((1, C))],
                          out_specs=_rows(tm, Co), compiler_params=_cp("parallel"))(x, x, w, b)


def _conv_act_bwd(x, dparts, w, b, *, K, act, name, tm_pref=256):
    L, C = x.shape
    H = _halo_rows(K)
    tm = _pick(L, tm_pref)
    Co = C if act == "silu" else C // 2
    nparts = len(dparts)

    def body(h_ref, x_ref, w_ref, b_ref, *rest):
        d_refs, (du_ref, dw_ref, db_ref) = rest[:nparts], rest[nparts:]
        i = pl.program_id(0)

        @pl.when(i == 0)
        def _():
            dw_ref[...] = jnp.zeros_like(dw_ref)
            db_ref[...] = jnp.zeros_like(db_ref)

        halo = jnp.where(i > 0, h_ref[...], 0.0)
        ext = jnp.concatenate([halo, x_ref[...]], axis=0)
        u = _causal_conv(ext, H, w_ref, K) + b_ref[...]
        d = [r[...].astype(f32) for r in d_refs]
        d = d[0] if nparts == 1 else jnp.concatenate(d, axis=1)
        if act == "silu":
            du = d * _dsilu(u)
        else:
            g, v = u[:, :Co], u[:, Co:]
            du = jnp.concatenate([d * v * _dsilu(g), d * _silu(g)], axis=1)
        du_ref[...] = du
        db_ref[...] += jnp.sum(du, axis=0, keepdims=True)
        _tap_grads(dw_ref, du, ext, H, K)

    in_specs = [_prev_halo_spec(tm, H, C), _rows(tm, C), _const((H, C)), _const((1, C))]
    in_specs += [_rows(tm, p.shape[1]) for p in dparts]
    return pl.pallas_call(body, name=name, out_shape=(S((L, C), f32), S((H, C), f32), S((1, C), f32)), grid=(L // tm,),
                          in_specs=in_specs, out_specs=(_rows(tm, C), _const((H, C)), _const((1, C))),
                          compiler_params=_cp("arbitrary"))(x, x, w, b, *dparts)


def _conv_transpose(d, w, *, K, name, tm_pref=256):
    L, C = d.shape
    H = _halo_rows(K)
    tm = _pick(L, tm_pref)
    nb = L // tm

    def body(d_ref, h_ref, w_ref, o_ref):
        i = pl.program_id(0)
        halo = jnp.where(i < nb - 1, h_ref[...], 0.0)
        ext = jnp.concatenate([d_ref[...], halo], axis=0)
        o_ref[...] = _anticausal_conv(ext, tm, w_ref, K).astype(bf16)

    return pl.pallas_call(body, name=name, out_shape=S((L, C), bf16), grid=(nb,),
                          in_specs=[_rows(tm, C), _next_halo_spec(tm, H, C, L), _const((H, C))],
                          out_specs=_rows(tm, C), compiler_params=_cp("parallel"))(d, d, w)


def _cf_fwd(u, dw_w, dw_b, ln_g, ln_b, *, name):
    L, C2 = u.shape
    C = C2 // 2
    K, H = CF_KERNEL, _halo_rows(CF_KERNEL)
    tm = _pick(L, 256)

    def body(h_ref, u_ref, w_ref, b_ref, g_ref, lb_ref, c_ref, s_ref):
        i = pl.program_id(0)
        halo = jnp.where(i > 0, h_ref[...], 0.0)
        ext = jnp.concatenate([halo, u_ref[...]], axis=0)
        glu = ext[:, :C] * _sigmoid(ext[:, C:])
        c = _causal_conv(glu, H, w_ref, K) + b_ref[...]
        c_ref[...] = c
        mu = jnp.mean(c, axis=-1, keepdims=True)
        xc = c - mu
        var = jnp.mean(xc * xc, axis=-1, keepdims=True)
        ln = xc * lax.rsqrt(var + EPS) * g_ref[...] + lb_ref[...]
        s_ref[...] = _silu(ln).astype(bf16)

    return pl.pallas_call(body, name=name, out_shape=(S((L, C), f32), S((L, C), bf16)), grid=(L // tm,),
                          in_specs=[_prev_halo_spec(tm, H, C2), _rows(tm, C2), _const((H, C)), _const((1, C)),
                                    _const((1, C)), _const((1, C))],
                          out_specs=(_rows(tm, C), _rows(tm, C)), compiler_params=_cp("parallel"))(u, u, dw_w, dw_b, ln_g, ln_b)


def _cf_ln_bwd(c, ln_g, ln_b, ds, *, name):
    L, C = c.shape
    tm = _pick(L, 512)

    def body(c_ref, g_ref, lb_ref, ds_ref, dc_ref, dg_ref, db_ref):
        i = pl.program_id(0)

        @pl.when(i == 0)
        def _():
            dg_ref[...] = jnp.zeros_like(dg_ref)
            db_ref[...] = jnp.zeros_like(db_ref)

        c = c_ref[...]
        mu = jnp.mean(c, axis=-1, keepdims=True)
        xc = c - mu
        r = lax.rsqrt(jnp.mean(xc * xc, axis=-1, keepdims=True) + EPS)
        xh = xc * r
        ln = xh * g_ref[...] + lb_ref[...]
        dln = ds_ref[...].astype(f32) * _dsilu(ln)
        dg_ref[...] += jnp.sum(dln * xh, axis=0, keepdims=True)
        db_ref[...] += jnp.sum(dln, axis=0, keepdims=True)
        dxh = dln * g_ref[...]
        dc_ref[...] = r * (dxh - jnp.mean(dxh, axis=-1, keepdims=True) - xh * jnp.mean(dxh * xh, axis=-1, keepdims=True))

    return pl.pallas_call(body, name=name, out_shape=(S((L, C), f32), S((1, C), f32), S((1, C), f32)), grid=(L // tm,),
                          in_specs=[_rows(tm, C), _const((1, C)), _const((1, C)), _rows(tm, C)],
                          out_specs=(_rows(tm, C), _const((1, C)), _const((1, C))),
                          compiler_params=_cp("arbitrary"))(c, ln_g, ln_b, ds)


def _cf_glu_bwd(u, dc, dw_w, *, name):
    L, C2 = u.shape
    C = C2 // 2
    K, H = CF_KERNEL, _halo_rows(CF_KERNEL)
    tm = _pick(L, 256)
    nb = L // tm

    def body(uh_ref, u_ref, dc_ref, dch_ref, w_ref, du_ref, dw_ref, db_ref, dus_ref):
        i = pl.program_id(0)

        @pl.when(i == 0)
        def _():
            dw_ref[...] = jnp.zeros_like(dw_ref)
            db_ref[...] = jnp.zeros_like(db_ref)
            dus_ref[...] = jnp.zeros_like(dus_ref)

        halo = jnp.where(i > 0, uh_ref[...], 0.0)
        ext = jnp.concatenate([halo, u_ref[...]], axis=0)
        sg = _sigmoid(ext[:, C:])
        glu = ext[:, :C] * sg
        dc = dc_ref[...]
        dnext = jnp.where(i < nb - 1, dch_ref[...], 0.0)
        dglu = _anticausal_conv(jnp.concatenate([dc, dnext], axis=0), tm, w_ref, K)
        a_cur, sg_cur = ext[H:, :C], sg[H:, :]
        du = jnp.concatenate([dglu * sg_cur, dglu * a_cur * sg_cur * (1.0 - sg_cur)], axis=1)
        du_ref[...] = du.astype(bf16)
        dus_ref[...] += jnp.sum(du, axis=0, keepdims=True)
        db_ref[...] += jnp.sum(dc, axis=0, keepdims=True)
        _tap_grads(dw_ref, dc, glu, H, K)

    return pl.pallas_call(body, name=name,
                          out_shape=(S((L, C2), bf16), S((H, C), f32), S((1, C), f32), S((1, C2), f32)), grid=(nb,),
                          in_specs=[_prev_halo_spec(tm, H, C2), _rows(tm, C2), _rows(tm, C), _next_halo_spec(tm, H, C, L),
                                    _const((H, C))],
                          out_specs=(_rows(tm, C2), _const((H, C)), _const((1, C)), _const((1, C2))),
                          compiler_params=_cp("arbitrary"))(u, u, dc, dc, dw_w)


def _gated_norm_fwd(y, z, g, *, name):
    L, C = y.shape
    tm = _pick(L, 256)

    def body(y_ref, z_ref, g_ref, o_ref):
        o_ref[...] = _rms_val(y_ref[...] * _silu(z_ref[...]), g_ref[...]).astype(bf16)

    return pl.pallas_call(body, name=name, out_shape=S((L, C), bf16), grid=(L // tm,),
                          in_specs=[_rows(tm, C), _rows(tm, C), _const((1, C))], out_specs=_rows(tm, C),
                          compiler_params=_cp("parallel"))(y, z, g)


def _gated_norm_bwd(y, z, g, dyn, *, name):
    L, C = y.shape
    tm = _pick(L, 256)

    def body(y_ref, z_ref, g_ref, d_ref, dy_ref, dz_ref, dg_ref):
        i = pl.program_id(0)

        @pl.when(i == 0)
        def _():
            dg_ref[...] = jnp.zeros_like(dg_ref)

        y, z = y_ref[...], z_ref[...]
        sz = _silu(z)
        du, dg = _rms_bwd_val(y * sz, g_ref[...], d_ref[...].astype(f32))
        dg_ref[...] += dg
        dy_ref[...] = du * sz
        dz_ref[...] = (du * y * _dsilu(z)).astype(bf16)

    return pl.pallas_call(body, name=name, out_shape=(S((L, C), f32), S((L, C), bf16), S((1, C), f32)), grid=(L // tm,),
                          in_specs=[_rows(tm, C), _rows(tm, C), _const((1, C)), _rows(tm, C)],
                          out_specs=(_rows(tm, C), _rows(tm, C), _const((1, C))),
                          compiler_params=_cp("arbitrary"))(y, z, g, dyn)


_XA_SCALE = XA_HEAD_DIM ** -0.5


def _attn_fwd(q, kv, *, name):
    L, C = q.shape
    tm = _pick(L, 512)
    Dh = XA_HEAD_DIM

    def body(q_ref, kv_ref, o_ref):
        for h in range(XA_HEADS):
            qh = q_ref[:, h * Dh:(h + 1) * Dh]
            kh = kv_ref[:, h * Dh:(h + 1) * Dh]
            vh = kv_ref[:, C + h * Dh:C + (h + 1) * Dh]
            s = lax.dot_general(qh, kh, _DN["nt"], preferred_element_type=f32) * _XA_SCALE
            e = jnp.exp(s - jnp.max(s, axis=-1, keepdims=True))
            p = e / jnp.sum(e, axis=-1, keepdims=True)
            o_ref[:, h * Dh:(h + 1) * Dh] = jnp.dot(p.astype(bf16), vh, preferred_element_type=f32).astype(bf16)

    return pl.pallas_call(body, name=name, out_shape=S((L, C), bf16), grid=(L // tm,),
                          in_specs=[_rows(tm, C), _const((N_MEM, 2 * C))], out_specs=_rows(tm, C),
                          compiler_params=_cp("parallel"))(q, kv)


def _attn_bwd(q, kv, do, *, name):
    L, C = q.shape
    tm = _pick(L, 512)
    Dh = XA_HEAD_DIM

    def body(q_ref, kv_ref, do_ref, dq_ref, dkv_ref):
        i = pl.program_id(0)

        @pl.when(i == 0)
        def _():
            dkv_ref[...] = jnp.zeros_like(dkv_ref)

        for h in range(XA_HEADS):
            qh = q_ref[:, h * Dh:(h + 1) * Dh]
            kh = kv_ref[:, h * Dh:(h + 1) * Dh]
            vh = kv_ref[:, C + h * Dh:C + (h + 1) * Dh]
            doh = do_ref[:, h * Dh:(h + 1) * Dh]
            s = lax.dot_general(qh, kh, _DN["nt"], preferred_element_type=f32) * _XA_SCALE
            e = jnp.exp(s - jnp.max(s, axis=-1, keepdims=True))
            p = e / jnp.sum(e, axis=-1, keepdims=True)
            pb = p.astype(bf16)
            dkv_ref[:, C + h * Dh:C + (h + 1) * Dh] += lax.dot_general(pb, doh, _DN["tn"], preferred_element_type=f32)
            dp = lax.dot_general(doh, vh, _DN["nt"], preferred_element_type=f32)
            ds = (p * (dp - jnp.sum(dp * p, axis=-1, keepdims=True)) * _XA_SCALE).astype(bf16)
            dq_ref[:, h * Dh:(h + 1) * Dh] = jnp.dot(ds, kh, preferred_element_type=f32).astype(bf16)
            dkv_ref[:, h * Dh:(h + 1) * Dh] += lax.dot_general(ds, qh, _DN["tn"], preferred_element_type=f32)

    return pl.pallas_call(body, name=name, out_shape=(S((L, C), bf16), S((N_MEM, 2 * C), f32)), grid=(L // tm,),
                          in_specs=[_rows(tm, C), _const((N_MEM, 2 * C)), _rows(tm, C)],
                          out_specs=(_rows(tm, C), _const((N_MEM, 2 * C))),
                          compiler_params=_cp("arbitrary"))(q, kv, do)


Q = CHUNK
PAIRS = HEADS_PER_GROUP // 2
GW = HEADS_PER_GROUP * HEAD_DIM


def _split3(x):
    x1 = x.astype(bf16)
    r1 = x - x1.astype(f32)
    x2 = r1.astype(bf16)
    return x1, x2, (r1 - x2.astype(f32)).astype(bf16)


def _sel_right(x, sel, mode="nn"):
    return sum(lax.dot_general(p, sel, _DN[mode], preferred_element_type=f32) for p in _split3(x))


def _sel_left(sel, x):
    return sum(lax.dot_general(sel, p, _DN["nn"], preferred_element_type=f32) for p in _split3(x))


def _ssd_common(dt_ref, hp_ref):
    dt_pre = dt_ref[...] + hp_ref[0:1, :]
    dt = _softplus(dt_pre)
    A = -jnp.exp(hp_ref[1:2, :])
    a = dt * A
    row = lax.broadcasted_iota(jnp.int32, (Q, Q), 0)
    col = lax.broadcasted_iota(jnp.int32, (Q, Q), 1)
    tri = row >= col
    cs = _sel_left(tri.astype(bf16), a)
    T = cs[Q - 1:Q, :]
    return dict(dt_pre=dt_pre, dt=dt, A=A, cs=cs, csT=cs.T, T=T, ecs=jnp.exp(cs), eend=jnp.exp(T - cs), eT=jnp.exp(T),
                tri=tri, row=row, col=col)


def _pair_expand(v, jj, lo):
    return jnp.where(lo, v[:, 2 * jj:2 * jj + 1], v[:, 2 * jj + 1:2 * jj + 2])


def _decay(cm, h):
    seg = cm["cs"][:, h:h + 1] - cm["csT"][h:h + 1, :]
    return jnp.where(cm["tri"], jnp.exp(jnp.where(cm["tri"], seg, 0.0)), 0.0)


def _ssd_fwd(act, dtp, hp, *, name):
    L = act.shape[0]
    nc = L // Q

    def body(xs_ref, b_ref, c_ref, dt_ref, hp_ref, y_ref, hs_ref, h_scr):
        c = pl.program_id(1)

        @pl.when(c == 0)
        def _():
            h_scr[...] = jnp.zeros_like(h_scr)

        cm = _ssd_common(dt_ref, hp_ref)
        Bb, Cb = b_ref[...].astype(bf16), c_ref[...].astype(bf16)
        CB = lax.dot_general(Cb, Bb, _DN["nt"], preferred_element_type=f32)
        lo = lax.broadcasted_iota(jnp.int32, (Q, LANE), 1) < HEAD_DIM
        top = lax.broadcasted_iota(jnp.int32, (LANE, LANE), 0) < HEAD_DIM
        Drow = hp_ref[2:3, :]
        for jj in range(PAIRS):
            hA, hB = 2 * jj, 2 * jj + 1
            dtx, ecsx, eendx = (_pair_expand(cm[k], jj, lo) for k in ("dt", "ecs", "eend"))
            xs_p = xs_ref[:, jj * LANE:(jj + 1) * LANE]
            Xd = xs_p * dtx
            Y = None
            for h, Xm in ((hA, jnp.where(lo, Xd, 0.0)), (hB, jnp.where(lo, 0.0, Xd))):
                W = (CB * _decay(cm, h)).astype(bf16)
                t = jnp.dot(W, Xm.astype(bf16), preferred_element_type=f32)
                Y = t if Y is None else Y + t
            Hp = h_scr[jj]
            hs_ref[0, jj] = Hp
            Yoff = lax.dot_general(Cb, Hp.astype(bf16), _DN["nt"], preferred_element_type=f32) * ecsx
            Dx = jnp.where(lo[0:1, :], Drow[:, hA:hA + 1], Drow[:, hB:hB + 1])
            y_ref[:, jj * LANE:(jj + 1) * LANE] = Y + Yoff + xs_p * Dx
            Snew = lax.dot_general((Xd * eendx).astype(bf16), Bb, _DN["tn"], preferred_element_type=f32)
            eTx = jnp.where(top, cm["eT"][:, hA:hA + 1], cm["eT"][:, hB:hB + 1])
            h_scr[jj] = Hp * eTx + Snew

    return pl.pallas_call(
        body, name=name, out_shape=(S((L, D_INNER), f32), S((nc, N_SSM_HEADS // 2, LANE, D_STATE), f32)),
        grid=(N_GROUPS, nc),
        in_specs=[pl.BlockSpec((Q, GW), lambda g, c: (c, g)),
                  pl.BlockSpec((Q, D_STATE), lambda g, c: (c, D_INNER // D_STATE + g)),
                  pl.BlockSpec((Q, D_STATE), lambda g, c: (c, D_INNER // D_STATE + N_GROUPS + g)),
                  pl.BlockSpec((Q, LANE), lambda g, c: (c, g)),
                  pl.BlockSpec((SUBLANE, LANE), lambda g, c: (0, g))],
        out_specs=(pl.BlockSpec((Q, GW), lambda g, c: (c, g)),
                   pl.BlockSpec((1, PAIRS, LANE, D_STATE), lambda g, c: (c, g, 0, 0))),
        scratch_shapes=[pltpu.VMEM((PAIRS, LANE, D_STATE), f32)],
        compiler_params=_cp("arbitrary", "arbitrary"))(act, act, act, dtp, hp)


def _ssd_bwd(act, dtp, hp, dy, hs, *, name):
    L = act.shape[0]
    nc = L // Q

    def body(xs_ref, b_ref, c_ref, dt_ref, hp_ref, dy_ref, hs_ref, dxs_ref, db_ref, dc_ref, ddt_ref, dhp_ref, dh_scr):
        c = pl.program_id(1)

        @pl.when(c == 0)
        def _():
            dh_scr[...] = jnp.zeros_like(dh_scr)
            dhp_ref[...] = jnp.zeros_like(dhp_ref)

        cm = _ssd_common(dt_ref, hp_ref)
        Bb, Cb = b_ref[...].astype(bf16), c_ref[...].astype(bf16)
        CB = lax.dot_general(Cb, Bb, _DN["nt"], preferred_element_type=f32)
        lane = lax.broadcasted_iota(jnp.int32, (Q, LANE), 1)
        sub = lax.broadcasted_iota(jnp.int32, (LANE, LANE), 0)
        lo = lane < HEAD_DIM
        top = sub < HEAD_DIM
        Drow = hp_ref[2:3, :]
        zero = jnp.zeros((Q, LANE), f32)
        dcs, ddtx, dC, dB, dCB = zero, zero, zero, zero, jnp.zeros((Q, Q), f32)
        dD_row = jnp.zeros((1, LANE), f32)
        dT_row = jnp.zeros((1, LANE), f32)
        for jj in range(PAIRS):
            hA, hB = 2 * jj, 2 * jj + 1
            Pj = (lane == jnp.where(top, hA, hB)).astype(bf16)
            dtx, ecsx, eendx = (_pair_expand(cm[k], jj, lo) for k in ("dt", "ecs", "eend"))
            xs_p = xs_ref[:, jj * LANE:(jj + 1) * LANE]
            dY_p = dy_ref[:, jj * LANE:(jj + 1) * LANE]
            Xd = xs_p * dtx
            Xdb = Xd.astype(bf16)
            Hp, dHn = hs_ref[0, jj], dh_scr[jj]
            Hb, dHb = Hp.astype(bf16), dHn.astype(bf16)
            EdYb = (dY_p * ecsx).astype(bf16)
            YoffN = lax.dot_general(Cb, Hb, _DN["nt"], preferred_element_type=f32)
            dC = dC + jnp.dot(EdYb, Hb, preferred_element_type=f32)
            dH_off = lax.dot_general(EdYb, Cb, _DN["tn"], preferred_element_type=f32)
            R = lax.dot_general(Bb, dHb, _DN["nt"], preferred_element_type=f32)
            Xe = Xd * eendx
            dB = dB + jnp.dot(Xe.astype(bf16), dHb, preferred_element_type=f32)
            dXd = R * eendx
            V2 = _sel_right(R * Xe, Pj)
            dcs = dcs + _sel_right(dY_p * YoffN * ecsx, Pj) - V2
            dT_row = dT_row + jnp.sum(V2, axis=0, keepdims=True) \
                + jnp.sum(_sel_right(dHn * Hp, Pj, "tn"), axis=0, keepdims=True) * cm["eT"]
            for h, keep in ((hA, lo), (hB, jnp.logical_not(lo))):
                M = _decay(cm, h)
                Wf = CB * M
                dYm = jnp.where(keep, dY_p, 0.0).astype(bf16)
                dW = lax.dot_general(dYm, Xdb, _DN["nt"], preferred_element_type=f32)
                dXd = dXd + lax.dot_general(Wf.astype(bf16), dYm, _DN["tn"], preferred_element_type=f32)
                Z = dW * Wf
                onesh = (lane == h).astype(bf16)
                dcs = dcs + _sel_right(Z, onesh) - _sel_right(Z, onesh, "tn")
                dCB = dCB + dW * M
            Dx = jnp.where(lo[0:1, :], Drow[:, hA:hA + 1], Drow[:, hB:hB + 1])
            dxs_ref[:, jj * LANE:(jj + 1) * LANE] = dXd * dtx + dY_p * Dx
            ddtx = ddtx + _sel_right(dXd * xs_p, Pj)
            dD_row = dD_row + jnp.sum(_sel_right(dY_p * xs_p, Pj), axis=0, keepdims=True)
            eTx = jnp.where(top, cm["eT"][:, hA:hA + 1], cm["eT"][:, hB:hB + 1])
            dh_scr[jj] = dHn * eTx + dH_off
        dCBb = dCB.astype(bf16)
        dc_ref[...] = dC + jnp.dot(dCBb, Bb, preferred_element_type=f32)
        db_ref[...] = dB + lax.dot_general(dCBb, Cb, _DN["tn"], preferred_element_type=f32)
        dcs = dcs + jnp.where(lax.broadcasted_iota(jnp.int32, (Q, LANE), 0) == Q - 1, dT_row, 0.0)
        da = _sel_left((cm["row"] <= cm["col"]).astype(bf16), dcs)
        ddt_pre = (da * cm["A"] + ddtx) * _sigmoid(cm["dt_pre"])
        ddt_ref[...] = ddt_pre
        r8 = lax.broadcasted_iota(jnp.int32, (SUBLANE, LANE), 0)
        dhp_ref[...] += jnp.where(r8 == 0, jnp.sum(ddt_pre, axis=0, keepdims=True),
                                  jnp.where(r8 == 1, jnp.sum(da * cm["dt"], axis=0, keepdims=True) * cm["A"],
                                            jnp.where(r8 == 2, dD_row, 0.0)))

    rev = lambda c: nc - 1 - c
    return pl.pallas_call(
        body, name=name,
        out_shape=(S((L, D_INNER), f32), S((L, N_GROUPS * D_STATE), f32), S((L, N_GROUPS * D_STATE), f32),
                   S((L, N_GROUPS * LANE), f32), S((SUBLANE, N_GROUPS * LANE), f32)),
        grid=(N_GROUPS, nc),
        in_specs=[pl.BlockSpec((Q, GW), lambda g, c: (rev(c), g)),
                  pl.BlockSpec((Q, D_STATE), lambda g, c: (rev(c), D_INNER // D_STATE + g)),
                  pl.BlockSpec((Q, D_STATE), lambda g, c: (rev(c), D_INNER // D_STATE + N_GROUPS + g)),
                  pl.BlockSpec((Q, LANE), lambda g, c: (rev(c), g)),
                  pl.BlockSpec((SUBLANE, LANE), lambda g, c: (0, g)),
                  pl.BlockSpec((Q, GW), lambda g, c: (rev(c), g)),
                  pl.BlockSpec((1, PAIRS, LANE, D_STATE), lambda g, c: (rev(c), g, 0, 0))],
        out_specs=(pl.BlockSpec((Q, GW), lambda g, c: (rev(c), g)),
                   pl.BlockSpec((Q, D_STATE), lambda g, c: (rev(c), g)),
                   pl.BlockSpec((Q, D_STATE), lambda g, c: (rev(c), g)),
                   pl.BlockSpec((Q, LANE), lambda g, c: (rev(c), g)),
                   pl.BlockSpec((SUBLANE, LANE), lambda g, c: (0, g))),
        scratch_shapes=[pltpu.VMEM((PAIRS, LANE, D_STATE), f32)],
        compiler_params=_cp("arbitrary", "arbitrary"))(act, act, act, dtp, hp, dy, hs)


def _group_pad_cols(w):
    lead = w.shape[:-1]
    w = w.reshape(lead + (N_GROUPS, HEADS_PER_GROUP))
    w = jnp.pad(w, [(0, 0)] * len(lead) + [(0, 0), (0, LANE - HEADS_PER_GROUP)])
    return w.reshape(lead + (N_GROUPS * LANE,))


def _group_unpad_cols(w):
    lead = w.shape[:-1]
    return w.reshape(lead + (N_GROUPS, LANE))[..., :HEADS_PER_GROUP].reshape(lead + (N_SSM_HEADS,))


def _row(v):
    return v.reshape(1, -1)


ROW_SHARDED = ('ssm_out_w', 'cf_pw2_w', 'xa_q_w', 'xa_o_w', 'ffn_out_w')
COL_SHARDED = ('cf_pw1_w', 'xa_kv_w', 'ffn_in_w')


def _layer_matmul_weights(i):
    mixer = ('ssm_in_w', 'ssm_out_w') if i % 2 == 0 else ('cf_pw1_w', 'cf_pw2_w')
    return mixer + ('xa_q_w', 'xa_kv_w', 'xa_o_w', 'ffn_in_w', 'ffn_out_w')


def _device_step(x, mem, target, W, layer_weights, layer_grads, start_after=()):
    ng = W['norm_g']
    lw = []
    for i in range(DEPTH):
        j = i // 2
        p = {}
        if i % 2 == 0:
            p['cw'] = _pad_taps(W['ssm_conv_w'][j], SSM_CONV)
            p['cb'] = _row(W['ssm_conv_b'][j])
            hp = jnp.stack([_group_pad_cols(W['ssm_dt_bias'][j]), _group_pad_cols(W['ssm_A_log'][j]),
                            _group_pad_cols(W['ssm_D'][j])])
            p['hp'] = jnp.pad(hp, ((0, SUBLANE - 3), (0, 0)))
            p['sng'] = _row(W['ssm_norm_g'][j])
        else:
            p['pw1b'] = _row(W['cf_pw1_b'][j])
            p['dww'], p['dwb'] = _pad_taps(W['cf_dw_w'][j], CF_KERNEL), _row(W['cf_dw_b'][j])
            p['lng'], p['lnb'] = _row(W['cf_ln_g'][j]), _row(W['cf_ln_b'][j])
            p['pw2b'] = _row(W['cf_pw2_b'][j])
        p['memg'] = _row(W['xa_mem_g'][i])
        p['fcw'], p['fcb'] = _pad_taps(W['ffn_conv_w'][i], FFN_CONV), _row(W['ffn_conv_b'][i])
        p['g'] = [_row(ng[i, s]) for s in range(6)]
        lw.append(p)

    def wmm(a, wl, wname, mode, **kw):
        return _mm(a, wl[wname], mode, b_shards=wname in COL_SHARDED, **kw)

    saved = []
    X = x
    h = _rmsnorm_fwd(X, lw[0]['g'][0], name="norm_in", after=start_after)
    for i in range(DEPTH):
        p, sv = lw[i], {}
        wl = dict(layer_weights(i, X))
        sv['X0'], sv['h'], sv['wl'] = X, h, wl
        if i % 2 == 0:
            win = jnp.concatenate([wl['ssm_in_w'][s] for s in range(N_CHIPS)], axis=1)
            wl['wz'], wl['wx'] = win[:, :D_INNER], win[:, D_INNER:D_INNER + CONV_DIM]
            wl['wdt'] = _group_pad_cols(win[:, D_INNER + CONV_DIM:])
            z = _mm(h, wl['wz'], "nn", name="ssm_z")
            xbc = _mm(h, wl['wx'], "nn", name="ssm_xbc")
            dtp = _mm(h, wl['wdt'], "nn", name="ssm_dt")
            act = _conv_act_fwd(xbc, p['cw'], p['cb'], K=SSM_CONV, act="silu", name="ssm_conv_fwd", out_dtype=f32)
            y, hs = _ssd_fwd(act, dtp, p['hp'], name="ssd_fwd")
            yn = _gated_norm_fwd(y, z, p['sng'], name="ssm_gnorm_fwd")
            mix = wmm(yn, wl, 'ssm_out_w', "nn", name="ssm_out")
            sv.update(z=z, xbc=xbc, dtp=dtp, act=act, y=y, hs=hs, yn=yn)
        else:
            u = wmm(h, wl, 'cf_pw1_w', "nn", name="cf_pw1", bias=p['pw1b'])
            c, s = _cf_fwd(u, p['dww'], p['dwb'], p['lng'], p['lnb'], name="cf_conv_fwd")
            mix = wmm(s, wl, 'cf_pw2_w', "nn", name="cf_pw2", bias=p['pw2b'])
            sv.update(u=u, c=c, s=s)
        X1, h2 = _resid_norm_fwd(X, mix, p['g'][1], p['g'][2], name="resid_norm_a")
        q = wmm(h2, wl, 'xa_q_w', "nn", name="xa_q", out_dtype=bf16)
        m = _rmsnorm_fwd(mem, p['memg'], name="xa_mem_norm")
        kv = wmm(m, wl, 'xa_kv_w', "nn", name="xa_kv", out_dtype=bf16)
        o = _attn_fwd(q, kv, name="xa_attn_fwd")
        a = wmm(o, wl, 'xa_o_w', "nn", name="xa_o")
        X2, h3 = _resid_norm_fwd(X1, a, p['g'][3], p['g'][4], name="resid_norm_b")
        u0 = wmm(h3, wl, 'ffn_in_w', "nn", name="ffn_in")
        fact = _conv_act_fwd(u0, p['fcw'], p['fcb'], K=FFN_CONV, act="swiglu", name="ffn_conv_fwd", out_dtype=bf16)
        f = wmm(fact, wl, 'ffn_out_w', "nn", name="ffn_out")
        g_next = lw[i + 1]['g'][0] if i + 1 < DEPTH else None
        X3, hn = _resid_norm_fwd(X2, f, p['g'][5], g_next, name="resid_norm_c" if g_next is not None else "resid_norm_last")
        sv.update(mix=mix, X1=X1, h2=h2, q=q, m=m, kv=kv, o=o, a=a, X2=X2, h3=h3, u0=u0, fact=fact, f=f)
        saved.append(sv)
        X, h = X3, hn

    sse, G = _loss_fwd_bwd(X, target, name="loss")

    small = [n for n in WEIGHT_NAMES if n not in MATMUL_WEIGHTS]
    gr = {n: [None] * W[n].shape[0] for n in small}

    def dwmm(gl, a, d, wname, *, name):
        if wname in COL_SHARDED:
            gl[wname] = _mm(a, d, "tn", name=name, out_dtype=bf16, out_shards=True)
        else:
            g = _mm(a, d, "tn", name=name, out_dtype=bf16)
            gl[wname] = g.reshape(N_CHIPS, g.shape[0] // N_CHIPS, g.shape[1])

    dng = [[None] * 6 for _ in range(DEPTH)]
    behind = ()
    for i in reversed(range(DEPTH)):
        p, sv, j = lw[i], saved[i], i // 2
        wl, gl = sv['wl'], {}
        df, dng[i][5], _ = _norm_bwd(sv['f'], p['g'][5], G, name="nb_f", out_dtype=bf16, after=behind)
        dwmm(gl, sv['fact'], df, 'ffn_out_w', name="ffn_out_dw")
        dfact = wmm(df, wl, 'ffn_out_w', "nt", name="ffn_out_dx")
        du, dcw, dcb = _conv_act_bwd(sv['u0'], [dfact], p['fcw'], p['fcb'], K=FFN_CONV, act="swiglu", name="ffn_conv_bwd")
        gr['ffn_conv_w'][i], gr['ffn_conv_b'][i] = dcw[:FFN_CONV], dcb[0]
        du0 = _conv_transpose(du, p['fcw'], K=FFN_CONV, name="ffn_conv_bwd_x")
        dwmm(gl, sv['h3'], du0, 'ffn_in_w', name="ffn_in_dw")
        dh3 = wmm(du0, wl, 'ffn_in_w', "nt", name="ffn_in_dx")
        G, dng[i][4], _ = _norm_bwd(sv['X2'], p['g'][4], dh3, name="nb_x2", add=G)
        da, dng[i][3], _ = _norm_bwd(sv['a'], p['g'][3], G, name="nb_a", out_dtype=bf16)
        dwmm(gl, sv['o'], da, 'xa_o_w', name="xa_o_dw")
        do = wmm(da, wl, 'xa_o_w', "nt", name="xa_o_dx", out_dtype=bf16)
        dq, dkv = _attn_bwd(sv['q'], sv['kv'], do, name="xa_attn_bwd")
        dwmm(gl, sv['h2'], dq, 'xa_q_w', name="xa_q_dw")
        dh2 = wmm(dq, wl, 'xa_q_w', "nt", name="xa_q_dx")
        dwmm(gl, sv['m'], dkv, 'xa_kv_w', name="xa_kv_dw")
        dm = wmm(dkv, wl, 'xa_kv_w', "nt", name="xa_kv_dx")
        _, dmg, _ = _norm_bwd(mem, p['memg'], dm, name="nb_mem")
        gr['xa_mem_g'][i] = dmg[0]
        G, dng[i][2], _ = _norm_bwd(sv['X1'], p['g'][2], dh2, name="nb_x1", add=G)
        dmix, dng[i][1], dmix_sum = _norm_bwd(sv['mix'], p['g'][1], G, name="nb_mix", out_dtype=bf16)
        if i % 2 == 0:
            dwmm(gl, sv['yn'], dmix, 'ssm_out_w', name="ssm_out_dw")
            dyn = wmm(dmix, wl, 'ssm_out_w', "nt", name="ssm_out_dx")
            dy, dz, dsng = _gated_norm_bwd(sv['y'], sv['z'], p['sng'], dyn, name="ssm_gnorm_bwd")
            gr['ssm_norm_g'][j] = dsng[0]
            dxs, dB, dC, ddtp, dhp = _ssd_bwd(sv['act'], sv['dtp'], p['hp'], dy, sv['hs'], name="ssd_bwd")
            gr['ssm_dt_bias'][j], gr['ssm_A_log'][j], gr['ssm_D'][j] = (_group_unpad_cols(dhp[r]) for r in range(3))
            dpre, dcw, dcb = _conv_act_bwd(sv['xbc'], [dxs, dB, dC], p['cw'], p['cb'], K=SSM_CONV, act="silu",
                                           name="ssm_conv_bwd")
            gr['ssm_conv_w'][j], gr['ssm_conv_b'][j] = dcw[:SSM_CONV], dcb[0]
            dxbc = _conv_transpose(dpre, p['cw'], K=SSM_CONV, name="ssm_conv_bwd_x")
            hh = sv['h']
            dwz = _mm(hh, dz, "tn", name="ssm_z_dw", out_dtype=bf16)
            dwx = _mm(hh, dxbc, "tn", name="ssm_xbc_dw", out_dtype=bf16)
            dwdt = _mm(hh, ddtp, "tn", name="ssm_dt_dw", out_dtype=bf16)
            din = jnp.concatenate([dwz, dwx, _group_unpad_cols(dwdt)], axis=1)
            gl['ssm_in_w'] = jnp.stack(jnp.split(din, N_CHIPS, axis=1))
            dh = _mm(dz, wl['wz'], "nt", name="ssm_z_dx")
            dh = _mm(dxbc, wl['wx'], "nt", name="ssm_xbc_dx", add=dh)
            dh = _mm(ddtp, wl['wdt'], "nt", name="ssm_dt_dx", add=dh)
        else:
            dwmm(gl, sv['s'], dmix, 'cf_pw2_w', name="cf_pw2_dw")
            gr['cf_pw2_b'][j] = dmix_sum[0]
            ds = wmm(dmix, wl, 'cf_pw2_w', "nt", name="cf_pw2_dx")
            dc, dlg, dlb = _cf_ln_bwd(sv['c'], p['lng'], p['lnb'], ds, name="cf_ln_bwd")
            gr['cf_ln_g'][j], gr['cf_ln_b'][j] = dlg[0], dlb[0]
            du, ddw, ddb, dus = _cf_glu_bwd(sv['u'], dc, p['dww'], name="cf_glu_bwd")
            gr['cf_dw_w'][j], gr['cf_dw_b'][j], gr['cf_pw1_b'][j] = ddw[:CF_KERNEL], ddb[0], dus[0]
            dwmm(gl, sv['h'], du, 'cf_pw1_w', name="cf_pw1_dw")
            dh = wmm(du, wl, 'cf_pw1_w', "nt", name="cf_pw1_dx")
        behind = tuple(layer_grads(i, gl))
        G, dng[i][0], _ = _norm_bwd(sv['X0'], p['g'][0], dh, name="nb_x0", add=G, after=behind)
    gr['norm_g'] = [jnp.concatenate(dng[i], axis=0) for i in range(DEPTH)]
    gsmall = {n: jnp.stack(gr[n]) for n in small}
    return sse, G, gsmall


MESH = pl.DeviceIdType.MESH
HBM_SPEC = pl.BlockSpec(memory_space=pltpu.HBM)


def _chip_peers(x, y):
    return [(1 - x, y), (x, 1 - y), (1 - x, 1 - y)]


def _all_gather_chips(buf, *, name):
    R, C = buf.shape

    def body(in_ref, out_ref, send_sems, recv_sems, local_sem):
        x, y, c = lax.axis_index("x"), lax.axis_index("y"), lax.axis_index("c")
        me = 2 * x + y
        mine = pltpu.make_async_copy(in_ref, out_ref.at[me], local_sem)
        mine.start()
        peers = _chip_peers(x, y)
        sends = []
        for k, (px, py) in enumerate(peers):
            cp = pltpu.make_async_remote_copy(src_ref=in_ref, dst_ref=out_ref.at[me], send_sem=send_sems.at[k],
                                              recv_sem=recv_sems.at[k], device_id=(px, py, c), device_id_type=MESH)
            cp.start()
            sends.append(cp)
        for k, (px, py) in enumerate(peers):
            pltpu.make_async_remote_copy(src_ref=in_ref, dst_ref=out_ref.at[2 * px + py], send_sem=send_sems.at[k],
                                         recv_sem=recv_sems.at[k], device_id=(px, py, c), device_id_type=MESH).wait_recv()
        for cp in sends:
            cp.wait_send()
        mine.wait()

    return pl.pallas_call(body, name=name, out_shape=S((N_CHIPS, R, C), buf.dtype), in_specs=[HBM_SPEC], out_specs=HBM_SPEC,
                          scratch_shapes=[pltpu.SemaphoreType.DMA((3,)), pltpu.SemaphoreType.DMA((3,)),
                                          pltpu.SemaphoreType.DMA(())])(buf)


def _remote(src, dst, send_sem, recv_sem, device):
    return pltpu.make_async_remote_copy(src_ref=src, dst_ref=dst, send_sem=send_sem, recv_sem=recv_sem,
                                        device_id=device, device_id_type=MESH)


def _gather_matmul_weights(shards, *, name):
    n = len(shards)

    def body(*refs):
        ins, outs = refs[:n], refs[n:2 * n]
        send, recv, fsend, frecv, lsem = refs[2 * n:]
        x, y, c = lax.axis_index("x"), lax.axis_index("y"), lax.axis_index("c")
        me, sib = 2 * x + y, (x, y, 1 - c)
        peers = _chip_peers(x, y)
        started, local = [], []
        for w in range(n):
            cp = pltpu.make_async_copy(ins[w], outs[w].at[:, me], lsem.at[w])
            cp.start()
            local.append(cp)
            for k, (px, py) in enumerate(peers):
                cp = _remote(ins[w].at[:, c], outs[w].at[:, me, c], send.at[w, k], recv.at[w, k], (px, py, c))
                cp.start()
                started.append(cp)
        for w in range(n):
            for k, (px, py) in enumerate(peers):
                landed = outs[w].at[:, 2 * px + py, c]
                _remote(ins[w].at[:, c], landed, send.at[w, k], recv.at[w, k], (px, py, c)).wait_recv()
                cp = _remote(landed, landed, fsend.at[w, k], frecv.at[w, k], sib)
                cp.start()
                started.append(cp)
        for w in range(n):
            for k, (px, py) in enumerate(peers):
                _remote(ins[w].at[:, c], outs[w].at[:, 2 * px + py, 1 - c], fsend.at[w, k], frecv.at[w, k], sib).wait_recv()
        for cp in started:
            cp.wait_send()
        for cp in local:
            cp.wait()

    out_shape = tuple(S((s.shape[0], N_CHIPS) + s.shape[1:], s.dtype) for s in shards)
    sems = [pltpu.SemaphoreType.DMA((n, 3)) for _ in range(4)] + [pltpu.SemaphoreType.DMA((n,))]
    return pl.pallas_call(body, name=name, out_shape=out_shape, in_specs=[HBM_SPEC] * n, out_specs=(HBM_SPEC,) * n,
                          scratch_shapes=sems)(*shards)


SEM_SPEC = pl.BlockSpec(memory_space=pltpu.SEMAPHORE)
VMEM_SPEC = pl.BlockSpec(memory_space=pltpu.VMEM)


def _in_hbm(a):
    return pltpu.with_memory_space_constraint(a, pltpu.HBM)


def _chip_targets(x, y):
    return [(x, y), (1 - x, y), (x, 1 - y), (1 - x, 1 - y)]


def _spread_start(srcs, scatter, *, name, after=()):
    n = len(srcs)
    lands = [lax.empty((N_CHIPS,) + (s.shape[1:] if scatter else s.shape), s.dtype) for s in srcs]

    def body(*refs):
        src, land = refs[:n], refs[n:2 * n]
        send, recv, token = refs[2 * n + len(after)], refs[2 * n + len(after) + 1], refs[-1]
        x, y, c = lax.axis_index("x"), lax.axis_index("y"), lax.axis_index("c")
        me = 2 * x + y
        for w in range(n):
            for k, (px, py) in enumerate(_chip_targets(x, y)):
                block = src[w].at[2 * px + py] if scatter else src[w]
                _remote(block, land[w].at[me], send.at[N_CHIPS * w + k], recv.at[N_CHIPS * w + k], (px, py, c)).start()
        token[...] = jnp.zeros_like(token)

    thru = tuple(pltpu.HBM(a.shape, a.dtype) for a in list(srcs) + lands)
    sems = (pltpu.SemaphoreType.DMA((N_CHIPS * n,)), pltpu.SemaphoreType.DMA((N_CHIPS * n,)))
    out = pl.pallas_call(
        body, name=name, out_shape=sems + thru + (S((SUBLANE, LANE), f32),),
        in_specs=[HBM_SPEC] * (2 * n) + [ANY_SPEC] * len(after),
        out_specs=(SEM_SPEC, SEM_SPEC) + (HBM_SPEC,) * (2 * n) + (VMEM_SPEC,),
        input_output_aliases={i: 2 + i for i in range(2 * n)},
        compiler_params=pltpu.CompilerParams(has_side_effects=pltpu.SideEffectType.DATAFLOW_SIDE_EFFECTING),
    )(*[_in_hbm(a) for a in list(srcs) + lands], *after)
    return out[0], out[1], out[2:2 + n], out[2 + n:2 + 2 * n], out[-1]


def _spread_wait(send, recv, srcs, lands, after, scatter, *, name):
    n = len(srcs)

    def body(*refs):
        src, land, send, recv = refs[:n], refs[n:2 * n], refs[2 * n], refs[2 * n + 1]
        x, y, c = lax.axis_index("x"), lax.axis_index("y"), lax.axis_index("c")
        me = 2 * x + y
        for w in range(n):
            for k, (px, py) in enumerate(_chip_targets(x, y)):
                block = src[w].at[me] if scatter else src[w]
                cp = _remote(block, land[w].at[2 * px + py], send.at[N_CHIPS * w + k], recv.at[N_CHIPS * w + k], (px, py, c))
                cp.wait_send()
                cp.wait_recv()

    thru = tuple(pltpu.HBM(a.shape, a.dtype) for a in list(srcs) + list(lands))
    out = pl.pallas_call(
        body, name=name, out_shape=thru,
        in_specs=[HBM_SPEC] * (2 * n) + [SEM_SPEC, SEM_SPEC] + [ANY_SPEC] * len(after), out_specs=(HBM_SPEC,) * (2 * n),
        input_output_aliases={i: i for i in range(2 * n)},
        compiler_params=pltpu.CompilerParams(has_side_effects=pltpu.SideEffectType.DATAFLOW_SIDE_EFFECTING),
    )(*srcs, *lands, send, recv, *after)
    return out[:n], out[n:]


def _swap_sibling(bufs, *, name):
    n = len(bufs)

    def body(*refs):
        src, out, send, recv = refs[:n], refs[n:2 * n], refs[-2], refs[-1]
        sib = (lax.axis_index("x"), lax.axis_index("y"), 1 - lax.axis_index("c"))
        copies = [_remote(src[w], out[w], send.at[w], recv.at[w], sib) for w in range(n)]
        for cp in copies:
            cp.start()
        for cp in copies:
            cp.wait()

    return pl.pallas_call(body, name=name, out_shape=tuple(S(a.shape, a.dtype) for a in bufs),
                          in_specs=[HBM_SPEC] * n, out_specs=(HBM_SPEC,) * n,
                          scratch_shapes=[pltpu.SemaphoreType.DMA((n,)), pltpu.SemaphoreType.DMA((n,))])(*bufs)


N_DEVICES = 8


def _allgather_devices(buf, *, name):
    R, C = buf.shape

    def body(in_ref, out_ref, send, recv, lsem):
        x, y, c = lax.axis_index("x"), lax.axis_index("y"), lax.axis_index("c")
        me = 4 * x + 2 * y + c
        mine = pltpu.make_async_copy(in_ref, out_ref.at[me], lsem)
        mine.start()
        flips = [(d >> 2 & 1, d >> 1 & 1, d & 1) for d in range(1, N_DEVICES)]
        peers = [(1 - x if fx else x, 1 - y if fy else y, 1 - c if fc else c) for fx, fy, fc in flips]
        sends = []
        for k, peer in enumerate(peers):
            cp = _remote(in_ref, out_ref.at[me], send.at[k], recv.at[k], peer)
            cp.start()
            sends.append(cp)
        for k, (px, py, pc) in enumerate(peers):
            _remote(in_ref, out_ref.at[4 * px + 2 * py + pc], send.at[k], recv.at[k], (px, py, pc)).wait_recv()
        for cp in sends:
            cp.wait_send()
        mine.wait()

    return pl.pallas_call(body, name=name, out_shape=S((N_DEVICES, R, C), buf.dtype), in_specs=[HBM_SPEC], out_specs=HBM_SPEC,
                          scratch_shapes=[pltpu.SemaphoreType.DMA((N_DEVICES - 1,)), pltpu.SemaphoreType.DMA((N_DEVICES - 1,)),
                                          pltpu.SemaphoreType.DMA(())])(buf)


def _sum_slots(buf, *, name):
    ns, R, C = buf.shape
    tr = _pick(R, 512)
    assert R % tr == 0

    def body(*refs):
        acc = refs[0][...]
        for r in refs[1:ns]:
            acc = acc + r[...]
        refs[ns][...] = acc

    specs = [pl.BlockSpec((None, tr, C), functools.partial(lambda s, i: (s, i, 0), s)) for s in range(ns)]
    return pl.pallas_call(body, name=name, out_shape=S((R, C), buf.dtype), grid=(R // tr,), in_specs=specs,
                          out_specs=pl.BlockSpec((tr, C), lambda i: (i, 0)), compiler_params=_cp("parallel"))(*([buf] * ns))


ADAMW_BLOCK_BYTES = 1 << 20


def _adamw(w, m, v, groups, *, name, layer=None, prev=None):
    shape = w.shape if layer is None else w.shape[1:]
    C = shape[-1]
    Rr = math.prod(shape[:-1])
    tr = Rr
    if Rr * C * 4 > ADAMW_BLOCK_BYTES:
        tr = max(t for t in range(2 * SUBLANE, Rr + 1, 2 * SUBLANE) if Rr % t == 0 and t * C * 4 <= ADAMW_BLOCK_BYTES)
    c1 = 1.0 / (1.0 - ADAM_B1 ** ADAM_STEP)
    c2 = 1.0 / (1.0 - ADAM_B2 ** ADAM_STEP)
    if layer is None:
        to2 = lambda t: t.reshape(Rr, C)
        spec = pl.BlockSpec((tr, C), lambda i: (i, 0))
        res_shape = S((Rr, C), f32)
    else:
        to2 = lambda t: t.reshape(layer[1], Rr, C)
        spec = pl.BlockSpec((None, tr, C), functools.partial(lambda l, i: (l, i, 0), layer[0]))
        res_shape = S((layer[1], Rr, C), f32)
    wspec, spec = spec, pl.BlockSpec((tr, C), lambda i: (i, 0))
    g_specs, g_args, sizes = [], [], []
    for grp in groups:
        sizes.append(len(grp))
        for term in grp:
            if isinstance(term, tuple):
                arr, slot = term
                g_specs.append(pl.BlockSpec((None, tr, C), functools.partial(lambda s, i: (s, i, 0), slot)))
                g_args.append(arr.reshape(arr.shape[0], Rr, C))
            else:
                g_specs.append(spec)
                g_args.append(term.reshape(Rr, C))
    nterms = len(g_args)
    prev = () if prev is None else tuple(to2(t) for t in prev)

    def body(w_ref, m_ref, v_ref, *rest):
        t_refs, (g_ref, d_ref, mo_ref, vo_ref) = rest[:nterms], rest[-4:]
        g, pos = None, 0
        for size in sizes:
            part = None
            for r in t_refs[pos:pos + size]:
                t = r[...].astype(f32)
                part = t if part is None else part + t
            pos += size
            g = part if g is None else g + part
        mn = ADAM_B1 * m_ref[...] + (1.0 - ADAM_B1) * g
        vn = ADAM_B2 * v_ref[...] + (1.0 - ADAM_B2) * (g * g)
        g_ref[...] = g
        mo_ref[...] = mn
        vo_ref[...] = vn
        d_ref[...] = -ADAM_LR * ((mn * c1) / (jnp.sqrt(vn * c2) + ADAM_EPS) + ADAM_WD * w_ref[...])

    out = pl.pallas_call(body, name=name, out_shape=(res_shape,) * 4, grid=(Rr // tr,),
                         in_specs=[wspec] * 3 + g_specs + [ANY_SPEC] * len(prev), out_specs=(wspec,) * 4,
                         input_output_aliases={3 + nterms + k: k for k in range(len(prev))},
                         compiler_params=_cp("parallel"))(to2(w), to2(m), to2(v), *g_args, *prev)
    return tuple(o.reshape(w.shape) for o in out)


def _pack_rows(parts, dtype):
    flat = jnp.concatenate([p.reshape(-1).astype(dtype) for p in parts])
    n = flat.shape[0]
    unit = PACK_COLS * 2 * SUBLANE
    padded = -(-n // unit) * unit
    return jnp.pad(flat, (0, padded - n)).reshape(padded // PACK_COLS, PACK_COLS)


def _unpack_rows(flat2d, shapes):
    flat = flat2d.reshape(-1)
    out, off = [], 0
    for shp in shapes:
        n = math.prod(shp)
        out.append(flat[off:off + n].reshape(shp))
        off += n
    return out


def _gather_weights(local, names, dtype, *, name):
    shapes = [local[n].shape for n in names]
    got = _all_gather_chips(_pack_rows([local[n] for n in names], dtype), name=name)
    per_chip = [_unpack_rows(got[s], shapes) for s in range(N_CHIPS)]
    return {n: jnp.concatenate([per_chip[s][k] for s in range(N_CHIPS)], axis=SHARD_AXIS[n]) for k, n in enumerate(names)}


def kernel(x, mem, norm_g, ssm_in_w, ssm_conv_w, ssm_conv_b, ssm_dt_bias, ssm_A_log, ssm_D, ssm_norm_g, ssm_out_w, cf_pw1_w, cf_pw1_b, cf_dw_w, cf_dw_b, cf_ln_g, cf_ln_b, cf_pw2_w, cf_pw2_b, xa_mem_g, xa_q_w, xa_kv_w, xa_o_w, ffn_in_w, ffn_conv_w, ffn_conv_b, ffn_out_w, loss_target, m_norm_g, m_ssm_in_w, m_ssm_conv_w, m_ssm_conv_b, m_ssm_dt_bias, m_ssm_A_log, m_ssm_D, m_ssm_norm_g, m_ssm_out_w, m_cf_pw1_w, m_cf_pw1_b, m_cf_dw_w, m_cf_dw_b, m_cf_ln_g, m_cf_ln_b, m_cf_pw2_w, m_cf_pw2_b, m_xa_mem_g, m_xa_q_w, m_xa_kv_w, m_xa_o_w, m_ffn_in_w, m_ffn_conv_w, m_ffn_conv_b, m_ffn_out_w, v_norm_g, v_ssm_in_w, v_ssm_conv_w, v_ssm_conv_b, v_ssm_dt_bias, v_ssm_A_log, v_ssm_D, v_ssm_norm_g, v_ssm_out_w, v_cf_pw1_w, v_cf_pw1_b, v_cf_dw_w, v_cf_dw_b, v_cf_ln_g, v_cf_ln_b, v_cf_pw2_w, v_cf_pw2_b, v_xa_mem_g, v_xa_q_w, v_xa_kv_w, v_xa_o_w, v_ffn_in_w, v_ffn_conv_w, v_ffn_conv_b, v_ffn_out_w):
    w_local = dict(zip(WEIGHT_NAMES, (norm_g, ssm_in_w, ssm_conv_w, ssm_conv_b, ssm_dt_bias, ssm_A_log, ssm_D, ssm_norm_g,
                                      ssm_out_w, cf_pw1_w, cf_pw1_b, cf_dw_w, cf_dw_b, cf_ln_g, cf_ln_b, cf_pw2_w, cf_pw2_b,
                                      xa_mem_g, xa_q_w, xa_kv_w, xa_o_w, ffn_in_w, ffn_conv_w, ffn_conv_b, ffn_out_w)))
    m_local = dict(zip(WEIGHT_NAMES, (m_norm_g, m_ssm_in_w, m_ssm_conv_w, m_ssm_conv_b, m_ssm_dt_bias, m_ssm_A_log, m_ssm_D,
                                      m_ssm_norm_g, m_ssm_out_w, m_cf_pw1_w, m_cf_pw1_b, m_cf_dw_w, m_cf_dw_b, m_cf_ln_g,
                                      m_cf_ln_b, m_cf_pw2_w, m_cf_pw2_b, m_xa_mem_g, m_xa_q_w, m_xa_kv_w, m_xa_o_w,
                                      m_ffn_in_w, m_ffn_conv_w, m_ffn_conv_b, m_ffn_out_w)))
    v_local = dict(zip(WEIGHT_NAMES, (v_norm_g, v_ssm_in_w, v_ssm_conv_w, v_ssm_conv_b, v_ssm_dt_bias, v_ssm_A_log, v_ssm_D,
                                      v_ssm_norm_g, v_ssm_out_w, v_cf_pw1_w, v_cf_pw1_b, v_cf_dw_w, v_cf_dw_b, v_cf_ln_g,
                                      v_cf_ln_b, v_cf_pw2_w, v_cf_pw2_b, v_xa_mem_g, v_xa_q_w, v_xa_kv_w, v_xa_o_w,
                                      v_ffn_in_w, v_ffn_conv_w, v_ffn_conv_b, v_ffn_out_w)))

    small = [n for n in WEIGHT_NAMES if n not in MATMUL_WEIGHTS]
    small_sharded = [n for n in small if SHARD_AXIS[n] is not None]
    W = {n: w_local[n] for n in small if SHARD_AXIS[n] is None}
    W.update(_gather_weights(w_local, small_sharded, f32, name="gather_small_weights"))

    def layer_index(n, i):
        return i // 2 if n in ('ssm_in_w', 'ssm_out_w', 'cf_pw1_w', 'cf_pw2_w') else i

    def layer_shards(i):
        return [w_local[n][layer_index(n, i)].astype(bf16) for n in _layer_matmul_weights(i)]

    halves0 = [s.reshape(1, 2, s.shape[0] // 2, s.shape[1]) for s in layer_shards(0)]
    got0 = _gather_matmul_weights(halves0, name="gather_layer0")
    gathers, tokens = {}, []
    for i in range(1, DEPTH):
        send, recv, srcs, lands, token = _spread_start(layer_shards(i), False, name="gather_start_%d" % i, after=(got0[0],))
        gathers[i] = (send, recv, srcs, lands)
        tokens.append(token)

    def layer_weights(i, after):
        names = _layer_matmul_weights(i)
        if i == 0:
            lands = [g.reshape((N_CHIPS, 2 * g.shape[3], g.shape[4])) for g in got0]
        else:
            _, lands = _spread_wait(*gathers[i], (after,), False, name="gather_wait_%d" % i)
        return {n: (a.reshape(N_CHIPS * a.shape[1], a.shape[2]) if n in ROW_SHARDED else a) for n, a in zip(names, lands)}

    scatters = {}

    def layer_grads(i, gl):
        send, recv, srcs, lands, token = _spread_start([gl[n] for n in _layer_matmul_weights(i)], True,
                                                       name="grads_start_%d" % i)
        scatters[i] = (send, recv, srcs, lands)
        return (token,)

    sse, gx, gsmall = _device_step(x[0], mem[0], loss_target[0], W, layer_weights, layer_grads, tuple(tokens))

    loss = lax.psum(0.5 * sse[0, 0] / D_MODEL, ("x", "y", "c"))

    own = {}
    for i in reversed(range(DEPTH)):
        _, lands = _spread_wait(*scatters[i], (gx,), True, name="grads_wait_%d" % i)
        for n, a in zip(_layer_matmul_weights(i), lands):
            own[n, layer_index(n, i)] = a
    keys = sorted(own)
    sib = dict(zip(keys, _swap_sibling([own[k] for k in keys], name="grads_swap_sibling")))
    small_shapes = [gsmall[n].shape for n in small]
    slots = _allgather_devices(_pack_rows([gsmall[n] for n in small], f32), name="allgather_small_grads")
    gsum = dict(zip(small, _unpack_rows(_sum_slots(slots, name="sum_small_grads"), small_shapes)))
    chip = 2 * lax.axis_index("x") + lax.axis_index("y")

    res = {}
    for n in MATMUL_WEIGHTS:
        layers, out = w_local[n].shape[0], None
        for l in range(layers):
            groups = [[(own[n, l], s) for s in range(N_CHIPS)], [(sib[n, l], s) for s in range(N_CHIPS)]]
            out = _adamw(w_local[n], m_local[n], v_local[n], groups, name="adamw_%s_%d" % (n, l), layer=(l, layers), prev=out)
        res[n] = out
    for n in small:
        g, ax = gsum[n], SHARD_AXIS[n]
        if ax is not None:
            width = w_local[n].shape[ax]
            g = lax.dynamic_slice_in_dim(g, chip * width, width, axis=ax)
        res[n] = _adamw(w_local[n], m_local[n], v_local[n], [[g]], name="adamw_" + n)
    return (loss, gx[None], *[res[n][0] for n in WEIGHT_NAMES], *[res[n][1] for n in WEIGHT_NAMES],
            *[res[n][2] for n in WEIGHT_NAMES], *[res[n][3] for n in WEIGHT_NAMES])
```

```python
import functools
import math

import jax
import jax.numpy as jnp
from jax import lax
from jax.experimental import pallas as pl
from jax.experimental.pallas import tpu as pltpu

f32 = jnp.float32
bf16 = jnp.bfloat16
S = jax.ShapeDtypeStruct

D_MODEL = 1024
DEPTH = 4
D_INNER = 2048
HEAD_DIM = 64
N_GROUPS = 4
HEADS_PER_GROUP = 8
N_SSM_HEADS = 32
D_STATE = 128
CHUNK = 128
SSM_CONV = 4
CONV_DIM = 3072
CF_KERNEL = 31
N_MEM = 256
XA_HEADS = 4
XA_HEAD_DIM = 256
D_FF = 2816
FFN_CONV = 3
EPS = 1e-6
ADAM_LR, ADAM_B1, ADAM_B2, ADAM_EPS, ADAM_WD, ADAM_STEP = 0.001, 0.9, 0.999, 1e-08, 0.01, 10

LANE = 128
SUBLANE = 8
VMEM_LIMIT = 56 * 1024 * 1024
N_CHIPS = 4
PACK_COLS = 1024

WEIGHT_NAMES = ['norm_g', 'ssm_in_w', 'ssm_conv_w', 'ssm_conv_b', 'ssm_dt_bias', 'ssm_A_log', 'ssm_D', 'ssm_norm_g',
                'ssm_out_w', 'cf_pw1_w', 'cf_pw1_b', 'cf_dw_w', 'cf_dw_b', 'cf_ln_g', 'cf_ln_b', 'cf_pw2_w', 'cf_pw2_b',
                'xa_mem_g', 'xa_q_w', 'xa_kv_w', 'xa_o_w', 'ffn_in_w', 'ffn_conv_w', 'ffn_conv_b', 'ffn_out_w']
SHARD_AXIS = {'norm_g': 2, 'ssm_in_w': 2, 'ssm_conv_w': 2, 'ssm_conv_b': None, 'ssm_dt_bias': None, 'ssm_A_log': None,
              'ssm_D': None, 'ssm_norm_g': None, 'ssm_out_w': 1, 'cf_pw1_w': 2, 'cf_pw1_b': 1, 'cf_dw_w': 2, 'cf_dw_b': 1,
              'cf_ln_g': 1, 'cf_ln_b': 1, 'cf_pw2_w': 1, 'cf_pw2_b': 1, 'xa_mem_g': None, 'xa_q_w': 1, 'xa_kv_w': 2,
              'xa_o_w': 1, 'ffn_in_w': 2, 'ffn_conv_w': 2, 'ffn_conv_b': None, 'ffn_out_w': 1}
MATMUL_WEIGHTS = ('ssm_in_w', 'ssm_out_w', 'cf_pw1_w', 'cf_pw2_w', 'xa_q_w', 'xa_kv_w', 'xa_o_w', 'ffn_in_w', 'ffn_out_w')


def _cp(*sem):
    return pltpu.CompilerParams(dimension_semantics=tuple(sem), vmem_limit_bytes=VMEM_LIMIT)


def _pick(dim, pref):
    if dim <= pref:
        return dim
    best = None
    for t in range(LANE, pref + 1, LANE):
        if dim % t == 0:
            best = t
    assert best is not None, (dim, pref)
    return best


def _sigmoid(x):
    return 1.0 / (1.0 + jnp.exp(-x))


def _silu(x):
    return x * _sigmoid(x)


def _dsilu(x):
    s = _sigmoid(x)
    return s * (1.0 + x * (1.0 - s))


def _softplus(x):
    return jnp.maximum(x, 0.0) + jnp.log(1.0 + jnp.exp(-jnp.abs(x)))


_DN = {"nn": (((1,), (0,)), ((), ())), "nt": (((1,), (1,)), ((), ())), "tn": (((0,), (0,)), ((), ()))}


def _mm(a, b, mode, *, name, out_dtype=f32, bias=None, add=None, b_shards=False, out_shards=False):
    bshape = (b.shape[1], b.shape[2] * N_CHIPS) if b_shards else b.shape
    if mode == "nn":
        (M, K), (K2, N) = a.shape, bshape
    elif mode == "nt":
        (M, K), (N, K2) = a.shape, bshape
    else:
        (K, M), (K2, N) = a.shape, bshape
    assert K == K2, (a.shape, b.shape, mode)
    n_unit = N // N_CHIPS if ((b_shards and mode == "nn") or out_shards) else N
    k_unit = K // N_CHIPS if (b_shards and mode == "nt") else K
    tm, tn, tk = _pick(M, 1024), _pick(n_unit, 1408), _pick(k_unit, 1408)
    nk, nj_u, nk_u = K // tk, n_unit // tn, k_unit // tk
    a_spec = {"nn": pl.BlockSpec((tm, tk), lambda i, j, k: (i, k)), "nt": pl.BlockSpec((tm, tk), lambda i, j, k: (i, k)),
              "tn": pl.BlockSpec((tk, tm), lambda i, j, k: (k, i))}[mode]
    if not b_shards:
        b_spec = {"nn": pl.BlockSpec((tk, tn), lambda i, j, k: (k, j)), "nt": pl.BlockSpec((tn, tk), lambda i, j, k: (j, k)),
                  "tn": pl.BlockSpec((tk, tn), lambda i, j, k: (k, j))}[mode]
    else:
        b_spec = {"nn": pl.BlockSpec((None, tk, tn), lambda i, j, k: (j // nj_u, k, j % nj_u)),
                  "nt": pl.BlockSpec((None, tn, tk), lambda i, j, k: (k // nk_u, j, k % nk_u))}[mode]
    in_specs, args = [a_spec, b_spec], [a, b]
    if bias is not None:
        in_specs.append(pl.BlockSpec((1, tn), lambda i, j, k: (0, j)))
        args.append(bias)
    if add is not None:
        in_specs.append(pl.BlockSpec((tm, tn), lambda i, j, k: (i, j)))
        args.append(add)
    if not out_shards:
        out_shape, out_spec = S((M, N), out_dtype), pl.BlockSpec((tm, tn), lambda i, j, k: (i, j))
    else:
        out_shape = S((N_CHIPS, M, n_unit), out_dtype)
        out_spec = pl.BlockSpec((None, tm, tn), lambda i, j, k: (j // nj_u, i, j % nj_u))
    dn = _DN[mode]
    has_bias, has_add = bias is not None, add is not None

    def body(a_ref, b_ref, *rest):
        rest = list(rest)
        bias_ref = rest.pop(0) if has_bias else None
        add_ref = rest.pop(0) if has_add else None
        o_ref = rest[0]

        def finish(r):
            if has_bias:
                r = r + bias_ref[...]
            if has_add:
                r = r + add_ref[...].astype(f32)
            o_ref[...] = r.astype(out_dtype)

        part = lax.dot_general(a_ref[...].astype(bf16), b_ref[...].astype(bf16), dn, preferred_element_type=f32)
        if nk == 1:
            finish(part)
            return
        acc_ref = rest[1]
        k = pl.program_id(2)

        @pl.when(k == 0)
        def _():
            acc_ref[...] = part

        @pl.when(k > 0)
        def _():
            acc_ref[...] += part

        @pl.when(k == nk - 1)
        def _():
            finish(acc_ref[...])

    return pl.pallas_call(
        body, name=name, out_shape=out_shape, grid=(M // tm, N // tn, nk),
        in_specs=in_specs, out_specs=out_spec, scratch_shapes=[pltpu.VMEM((tm, tn), f32)] if nk > 1 else [],
        compiler_params=_cp("parallel", "parallel", "arbitrary"))(*args)


def _rows(tm, C):
    return pl.BlockSpec((tm, C), lambda i: (i, 0))


def _const(shape):
    return pl.BlockSpec(shape, lambda i: tuple(0 for _ in shape))


def _rms_val(x, g):
    r = lax.rsqrt(jnp.mean(x * x, axis=-1, keepdims=True) + EPS)
    return x * r * g


def _rms_bwd_val(x, g, dy):
    r = lax.rsqrt(jnp.mean(x * x, axis=-1, keepdims=True) + EPS)
    xn = x * r
    dxh = dy * g
    dx = r * (dxh - xn * jnp.mean(dxh * xn, axis=-1, keepdims=True))
    return dx, jnp.sum(dy * xn, axis=0, keepdims=True)


ANY_SPEC = pl.BlockSpec(memory_space=pl.ANY)


def _rmsnorm_fwd(x, g, *, name, after=()):
    L, C = x.shape
    tm = _pick(L, 512)

    def body(x_ref, g_ref, *rest):
        rest[-1][...] = _rms_val(x_ref[...], g_ref[...]).astype(bf16)

    return pl.pallas_call(body, name=name, out_shape=S((L, C), bf16), grid=(L // tm,),
                          in_specs=[_rows(tm, C), _const((1, C))] + [ANY_SPEC] * len(after), out_specs=_rows(tm, C),
                          compiler_params=_cp("parallel"))(x, g, *after)


def _resid_norm_fwd(x, mix, g_post, g_next, *, name):
    L, C = x.shape
    tm = _pick(L, 512)
    want_h = g_next is not None

    def body(x_ref, m_ref, gp_ref, *rest):
        xn = x_ref[...] + _rms_val(m_ref[...], gp_ref[...])
        if want_h:
            gn_ref, xo_ref, h_ref = rest
            h_ref[...] = _rms_val(xn, gn_ref[...]).astype(bf16)
        else:
            (xo_ref,) = rest
        xo_ref[...] = xn

    in_specs = [_rows(tm, C), _rows(tm, C), _const((1, C))]
    args = [x, mix, g_post]
    out_shape, out_specs = [S((L, C), f32)], [_rows(tm, C)]
    if want_h:
        in_specs.append(_const((1, C)))
        args.append(g_next)
        out_shape.append(S((L, C), bf16))
        out_specs.append(_rows(tm, C))
    out = pl.pallas_call(body, name=name, out_shape=tuple(out_shape), grid=(L // tm,), in_specs=in_specs,
                         out_specs=tuple(out_specs), compiler_params=_cp("parallel"))(*args)
    return (out[0], out[1]) if want_h else (out[0], None)


def _norm_bwd(x, g, dy, *, name, add=None, out_dtype=f32, after=()):
    L, C = x.shape
    tm = _pick(L, 512)
    has_add = add is not None

    def body(x_ref, g_ref, dy_ref, *rest):
        rest = list(rest)
        add_ref = rest.pop(0) if has_add else None
        dx_ref, dg_ref, cs_ref = rest[-3:]
        i = pl.program_id(0)

        @pl.when(i == 0)
        def _():
            dg_ref[...] = jnp.zeros_like(dg_ref)
            cs_ref[...] = jnp.zeros_like(cs_ref)

        dx, dg = _rms_bwd_val(x_ref[...], g_ref[...], dy_ref[...].astype(f32))
        dg_ref[...] += dg
        cs_ref[...] += jnp.sum(dx, axis=0, keepdims=True)
        if has_add:
            dx = dx + add_ref[...]
        dx_ref[...] = dx.astype(out_dtype)

    in_specs = [_rows(tm, C), _const((1, C)), _rows(tm, C)]
    args = [x, g, dy]
    if has_add:
        in_specs.append(_rows(tm, C))
        args.append(add)
    in_specs += [ANY_SPEC] * len(after)
    args += list(after)
    return pl.pallas_call(body, name=name, out_shape=(S((L, C), out_dtype), S((1, C), f32), S((1, C), f32)),
                          grid=(L // tm,), in_specs=in_specs,
                          out_specs=(_rows(tm, C), _const((1, C)), _const((1, C))),
                          compiler_params=_cp("arbitrary"))(*args)


def _loss_fwd_bwd(y, target, *, name):
    L, C = y.shape
    tm = _pick(L, 512)

    def body(y_ref, t_ref, acc_ref, dy_ref):
        i = pl.program_id(0)

        @pl.when(i == 0)
        def _():
            acc_ref[...] = jnp.zeros_like(acc_ref)

        e = y_ref[...] - t_ref[...]
        rs = jnp.sum(e * e, axis=-1, keepdims=True)
        acc_ref[...] += jnp.broadcast_to(jnp.sum(rs, axis=0, keepdims=True), (1, LANE))
        dy_ref[...] = e * (1.0 / C)

    return pl.pallas_call(body, name=name, out_shape=(S((1, LANE), f32), S((L, C), f32)), grid=(L // tm,),
                          in_specs=[_rows(tm, C), _rows(tm, C)], out_specs=(_const((1, LANE)), _rows(tm, C)),
                          compiler_params=_cp("arbitrary"))(y, target)


def _halo_rows(K):
    return SUBLANE if K - 1 <= SUBLANE else 32


def _prev_halo_spec(tm, H, C):
    return pl.BlockSpec((H, C), lambda i: (jnp.maximum(i * (tm // H) - 1, 0), 0))


def _next_halo_spec(tm, H, C, L):
    return pl.BlockSpec((H, C), lambda i: (jnp.minimum((i + 1) * (tm // H), L // H - 1), 0))


def _shift_down(ext, s):
    return ext if s == 0 else pltpu.roll(ext, s, axis=0)


def _shift_up(ext, s):
    return ext if s == 0 else pltpu.roll(ext, ext.shape[0] - s, axis=0)


def _causal_conv(ext, H, w_ref, K):
    acc = None
    for k in range(K):
        term = _shift_down(ext, K - 1 - k)[H:, :] * w_ref[k:k + 1, :]
        acc = term if acc is None else acc + term
    return acc


def _anticausal_conv(ext, tm, w_ref, K):
    acc = None
    for k in range(K):
        term = _shift_up(ext, K - 1 - k)[:tm, :] * w_ref[k:k + 1, :]
        acc = term if acc is None else acc + term
    return acc


def _tap_grads(dw_ref, d_cur, x_ext, H, K):
    for k in range(K):
        dw_ref[k:k + 1, :] += jnp.sum(d_cur * _shift_down(x_ext, K - 1 - k)[H:, :], axis=0, keepdims=True)


def _pad_taps(w, K):
    return jnp.pad(w, ((0, _halo_rows(K) - K), (0, 0)))


def _conv_act_fwd(x, w, b, *, K, act, name, out_dtype, tm_pref=256):
    L, C = x.shape
    H = _halo_rows(K)
    tm = _pick(L, tm_pref)
    Co = C if act == "silu" else C // 2

    def body(h_ref, x_ref, w_ref, b_ref, o_ref):
        i = pl.program_id(0)
        halo = jnp.where(i > 0, h_ref[...], 0.0)
        ext = jnp.concatenate([halo, x_ref[...]], axis=0)
        u = _causal_conv(ext, H, w_ref, K) + b_ref[...]
        if act == "silu":
            o_ref[...] = _silu(u).astype(out_dtype)
        else:
            o_ref[...] = (_silu(u[:, :Co]) * u[:, Co:]).astype(out_dtype)

    return pl.pallas_call(body, name=name, out_shape=S((L, Co), out_dtype), grid=(L // tm,),
                          in_specs=[_prev_halo_spec(tm, H, C), _rows(tm, C), _const((H, C)), _const((1, C))],
                          out_specs=_rows(tm, Co), compiler_params=_cp("parallel"))(x, x, w, b)


def _conv_act_bwd(x, dparts, w, b, *, K, act, name, tm_pref=256):
    L, C = x.shape
    H = _halo_rows(K)
    tm = _pick(L, tm_pref)
    Co = C if act == "silu" else C // 2
    nparts = len(dparts)

    def body(h_ref, x_ref, w_ref, b_ref, *rest):
        d_refs, (du_ref, dw_ref, db_ref) = rest[:nparts], rest[nparts:]
        i = pl.program_id(0)

        @pl.when(i == 0)
        def _():
            dw_ref[...] = jnp.zeros_like(dw_ref)
            db_ref[...] = jnp.zeros_like(db_ref)

        halo = jnp.where(i > 0, h_ref[...], 0.0)
        ext = jnp.concatenate([halo, x_ref[...]], axis=0)
        u = _causal_conv(ext, H, w_ref, K) + b_ref[...]
        d = [r[...].astype(f32) for r in d_refs]
        d = d[0] if nparts == 1 else jnp.concatenate(d, axis=1)
        if act == "silu":
            du = d * _dsilu(u)
        else:
            g, v = u[:, :Co], u[:, Co:]
            du = jnp.concatenate([d * v * _dsilu(g), d * _silu(g)], axis=1)
        du_ref[...] = du
        db_ref[...] += jnp.sum(du, axis=0, keepdims=True)
        _tap_grads(dw_ref, du, ext, H, K)

    in_specs = [_prev_halo_spec(tm, H, C), _rows(tm, C), _const((H, C)), _const((1, C))]
    in_specs += [_rows(tm, p.shape[1]) for p in dparts]
    return pl.pallas_call(body, name=name, out_shape=(S((L, C), f32), S((H, C), f32), S((1, C), f32)), grid=(L // tm,),
                          in_specs=in_specs, out_specs=(_rows(tm, C), _const((H, C)), _const((1, C))),
                          compiler_params=_cp("arbitrary"))(x, x, w, b, *dparts)


def _conv_transpose(d, w, *, K, name, tm_pref=256):
    L, C = d.shape
    H = _halo_rows(K)
    tm = _pick(L, tm_pref)
    nb = L // tm

    def body(d_ref, h_ref, w_ref, o_ref):
        i = pl.program_id(0)
        halo = jnp.where(i < nb - 1, h_ref[...], 0.0)
        ext = jnp.concatenate([d_ref[...], halo], axis=0)
        o_ref[...] = _anticausal_conv(ext, tm, w_ref, K).astype(bf16)

    return pl.pallas_call(body, name=name, out_shape=S((L, C), bf16), grid=(nb,),
                          in_specs=[_rows(tm, C), _next_halo_spec(tm, H, C, L), _const((H, C))],
                          out_specs=_rows(tm, C), compiler_params=_cp("parallel"))(d, d, w)


def _cf_fwd(u, dw_w, dw_b, ln_g, ln_b, *, name):
    L, C2 = u.shape
    C = C2 // 2
    K, H = CF_KERNEL, _halo_rows(CF_KERNEL)
    tm = _pick(L, 256)

    def body(h_ref, u_ref, w_ref, b_ref, g_ref, lb_ref, c_ref, s_ref):
        i = pl.program_id(0)
        halo = jnp.where(i > 0, h_ref[...], 0.0)
        ext = jnp.concatenate([halo, u_ref[...]], axis=0)
        glu = ext[:, :C] * _sigmoid(ext[:, C:])
        c = _causal_conv(glu, H, w_ref, K) + b_ref[...]
        c_ref[...] = c
        mu = jnp.mean(c, axis=-1, keepdims=True)
        xc = c - mu
        var = jnp.mean(xc * xc, axis=-1, keepdims=True)
        ln = xc * lax.rsqrt(var + EPS) * g_ref[...] + lb_ref[...]
        s_ref[...] = _silu(ln).astype(bf16)

    return pl.pallas_call(body, name=name, out_shape=(S((L, C), f32), S((L, C), bf16)), grid=(L // tm,),
                          in_specs=[_prev_halo_spec(tm, H, C2), _rows(tm, C2), _const((H, C)), _const((1, C)),
                                    _const((1, C)), _const((1, C))],
                          out_specs=(_rows(tm, C), _rows(tm, C)), compiler_params=_cp("parallel"))(u, u, dw_w, dw_b, ln_g, ln_b)


def _cf_ln_bwd(c, ln_g, ln_b, ds, *, name):
    L, C = c.shape
    tm = _pick(L, 512)

    def body(c_ref, g_ref, lb_ref, ds_ref, dc_ref, dg_ref, db_ref):
        i = pl.program_id(0)

        @pl.when(i == 0)
        def _():
            dg_ref[...] = jnp.zeros_like(dg_ref)
            db_ref[...] = jnp.zeros_like(db_ref)

        c = c_ref[...]
        mu = jnp.mean(c, axis=-1, keepdims=True)
        xc = c - mu
        r = lax.rsqrt(jnp.mean(xc * xc, axis=-1, keepdims=True) + EPS)
        xh = xc * r
        ln = xh * g_ref[...] + lb_ref[...]
        dln = ds_ref[...].astype(f32) * _dsilu(ln)
        dg_ref[...] += jnp.sum(dln * xh, axis=0, keepdims=True)
        db_ref[...] += jnp.sum(dln, axis=0, keepdims=True)
        dxh = dln * g_ref[...]
        dc_ref[...] = r * (dxh - jnp.mean(dxh, axis=-1, keepdims=True) - xh * jnp.mean(dxh * xh, axis=-1, keepdims=True))

    return pl.pallas_call(body, name=name, out_shape=(S((L, C), f32), S((1, C), f32), S((1, C), f32)), grid=(L // tm,),
                          in_specs=[_rows(tm, C), _const((1, C)), _const((1, C)), _rows(tm, C)],
                          out_specs=(_rows(tm, C), _const((1, C)), _const((1, C))),
                          compiler_params=_cp("arbitrary"))(c, ln_g, ln_b, ds)


def _cf_glu_bwd(u, dc, dw_w, *, name):
    L, C2 = u.shape
    C = C2 // 2
    K, H = CF_KERNEL, _halo_rows(CF_KERNEL)
    tm = _pick(L, 256)
    nb = L // tm

    def body(uh_ref, u_ref, dc_ref, dch_ref, w_ref, du_ref, dw_ref, db_ref, dus_ref):
        i = pl.program_id(0)

        @pl.when(i == 0)
        def _():
            dw_ref[...] = jnp.zeros_like(dw_ref)
            db_ref[...] = jnp.zeros_like(db_ref)
            dus_ref[...] = jnp.zeros_like(dus_ref)

        halo = jnp.where(i > 0, uh_ref[...], 0.0)
        ext = jnp.concatenate([halo, u_ref[...]], axis=0)
        sg = _sigmoid(ext[:, C:])
        glu = ext[:, :C] * sg
        dc = dc_ref[...]
        dnext = jnp.where(i < nb - 1, dch_ref[...], 0.0)
        dglu = _anticausal_conv(jnp.concatenate([dc, dnext], axis=0), tm, w_ref, K)
        a_cur, sg_cur = ext[H:, :C], sg[H:, :]
        du = jnp.concatenate([dglu * sg_cur, dglu * a_cur * sg_cur * (1.0 - sg_cur)], axis=1)
        du_ref[...] = du.astype(bf16)
        dus_ref[...] += jnp.sum(du, axis=0, keepdims=True)
        db_ref[...] += jnp.sum(dc, axis=0, keepdims=True)
        _tap_grads(dw_ref, dc, glu, H, K)

    return pl.pallas_call(body, name=name,
                          out_shape=(S((L, C2), bf16), S((H, C), f32), S((1, C), f32), S((1, C2), f32)), grid=(nb,),
                          in_specs=[_prev_halo_spec(tm, H, C2), _rows(tm, C2), _rows(tm, C), _next_halo_spec(tm, H, C, L),
                                    _const((H, C))],
                          out_specs=(_rows(tm, C2), _const((H, C)), _const((1, C)), _const((1, C2))),
                          compiler_params=_cp("arbitrary"))(u, u, dc, dc, dw_w)


def _gated_norm_fwd(y, z, g, *, name):
    L, C = y.shape
    tm = _pick(L, 256)

    def body(y_ref, z_ref, g_ref, o_ref):
        o_ref[...] = _rms_val(y_ref[...] * _silu(z_ref[...]), g_ref[...]).astype(bf16)

    return pl.pallas_call(body, name=name, out_shape=S((L, C), bf16), grid=(L // tm,),
                          in_specs=[_rows(tm, C), _rows(tm, C), _const((1, C))], out_specs=_rows(tm, C),
                          compiler_params=_cp("parallel"))(y, z, g)


def _gated_norm_bwd(y, z, g, dyn, *, name):
    L, C = y.shape
    tm = _pick(L, 256)

    def body(y_ref, z_ref, g_ref, d_ref, dy_ref, dz_ref, dg_ref):
        i = pl.program_id(0)

        @pl.when(i == 0)
        def _():
            dg_ref[...] = jnp.zeros_like(dg_ref)

        y, z = y_ref[...], z_ref[...]
        sz = _silu(z)
        du, dg = _rms_bwd_val(y * sz, g_ref[...], d_ref[...].astype(f32))
        dg_ref[...] += dg
        dy_ref[...] = du * sz
        dz_ref[...] = (du * y * _dsilu(z)).astype(bf16)

    return pl.pallas_call(body, name=name, out_shape=(S((L, C), f32), S((L, C), bf16), S((1, C), f32)), grid=(L // tm,),
                          in_specs=[_rows(tm, C), _rows(tm, C), _const((1, C)), _rows(tm, C)],
                          out_specs=(_rows(tm, C), _rows(tm, C), _const((1, C))),
                          compiler_params=_cp("arbitrary"))(y, z, g, dyn)


_XA_SCALE = XA_HEAD_DIM ** -0.5


def _attn_fwd(q, kv, *, name):
    L, C = q.shape
    tm = _pick(L, 512)
    Dh = XA_HEAD_DIM

    def body(q_ref, kv_ref, o_ref):
        for h in range(XA_HEADS):
            qh = q_ref[:, h * Dh:(h + 1) * Dh]
            kh = kv_ref[:, h * Dh:(h + 1) * Dh]
            vh = kv_ref[:, C + h * Dh:C + (h + 1) * Dh]
            s = lax.dot_general(qh, kh, _DN["nt"], preferred_element_type=f32) * _XA_SCALE
            e = jnp.exp(s - jnp.max(s, axis=-1, keepdims=True))
            p = e / jnp.sum(e, axis=-1, keepdims=True)
            o_ref[:, h * Dh:(h + 1) * Dh] = jnp.dot(p.astype(bf16), vh, preferred_element_type=f32).astype(bf16)

    return pl.pallas_call(body, name=name, out_shape=S((L, C), bf16), grid=(L // tm,),
                          in_specs=[_rows(tm, C), _const((N_MEM, 2 * C))], out_specs=_rows(tm, C),
                          compiler_params=_cp("parallel"))(q, kv)


def _attn_bwd(q, kv, do, *, name):
    L, C = q.shape
    tm = _pick(L, 512)
    Dh = XA_HEAD_DIM

    def body(q_ref, kv_ref, do_ref, dq_ref, dkv_ref):
        i = pl.program_id(0)

        @pl.when(i == 0)
        def _():
            dkv_ref[...] = jnp.zeros_like(dkv_ref)

        for h in range(XA_HEADS):
            qh = q_ref[:, h * Dh:(h + 1) * Dh]
            kh = kv_ref[:, h * Dh:(h + 1) * Dh]
            vh = kv_ref[:, C + h * Dh:C + (h + 1) * Dh]
            doh = do_ref[:, h * Dh:(h + 1) * Dh]
            s = lax.dot_general(qh, kh, _DN["nt"], preferred_element_type=f32) * _XA_SCALE
            e = jnp.exp(s - jnp.max(s, axis=-1, keepdims=True))
            p = e / jnp.sum(e, axis=-1, keepdims=True)
            pb = p.astype(bf16)
            dkv_ref[:, C + h * Dh:C + (h + 1) * Dh] += lax.dot_general(pb, doh, _DN["tn"], preferred_element_type=f32)
            dp = lax.dot_general(doh, vh, _DN["nt"], preferred_element_type=f32)
            ds = (p * (dp - jnp.sum(dp * p, axis=-1, keepdims=True)) * _XA_SCALE).astype(bf16)
            dq_ref[:, h * Dh:(h + 1) * Dh] = jnp.dot(ds, kh, preferred_element_type=f32).astype(bf16)
            dkv_ref[:, h * Dh:(h + 1) * Dh] += lax.dot_general(ds, qh, _DN["tn"], preferred_element_type=f32)

    return pl.pallas_call(body, name=name, out_shape=(S((L, C), bf16), S((N_MEM, 2 * C), f32)), grid=(L // tm,),
                          in_specs=[_rows(tm, C), _const((N_MEM, 2 * C)), _rows(tm, C)],
                          out_specs=(_rows(tm, C), _const((N_MEM, 2 * C))),
                          compiler_params=_cp("arbitrary"))(q, kv, do)


Q = CHUNK
PAIRS = HEADS_PER_GROUP // 2
GW = HEADS_PER_GROUP * HEAD_DIM


def _split3(x):
    x1 = x.astype(bf16)
    r1 = x - x1.astype(f32)
    x2 = r1.astype(bf16)
    return x1, x2, (r1 - x2.astype(f32)).astype(bf16)


def _sel_right(x, sel, mode="nn"):
    return sum(lax.dot_general(p, sel, _DN[mode], preferred_element_type=f32) for p in _split3(x))


def _sel_left(sel, x):
    return sum(lax.dot_general(sel, p, _DN["nn"], preferred_element_type=f32) for p in _split3(x))


def _ssd_common(dt_ref, hp_ref):
    dt_pre = dt_ref[...] + hp_ref[0:1, :]
    dt = _softplus(dt_pre)
    A = -jnp.exp(hp_ref[1:2, :])
    a = dt * A
    row = lax.broadcasted_iota(jnp.int32, (Q, Q), 0)
    col = lax.broadcasted_iota(jnp.int32, (Q, Q), 1)
    tri = row >= col
    cs = _sel_left(tri.astype(bf16), a)
    T = cs[Q - 1:Q, :]
    return dict(dt_pre=dt_pre, dt=dt, A=A, cs=cs, csT=cs.T, T=T, ecs=jnp.exp(cs), eend=jnp.exp(T - cs), eT=jnp.exp(T),
                tri=tri, row=row, col=col)


def _pair_expand(v, jj, lo):
    return jnp.where(lo, v[:, 2 * jj:2 * jj + 1], v[:, 2 * jj + 1:2 * jj + 2])


def _decay(cm, h):
    seg = cm["cs"][:, h:h + 1] - cm["csT"][h:h + 1, :]
    return jnp.where(cm["tri"], jnp.exp(jnp.where(cm["tri"], seg, 0.0)), 0.0)


def _ssd_fwd(act, dtp, hp, *, name):
    L = act.shape[0]
    nc = L // Q

    def body(xs_ref, b_ref, c_ref, dt_ref, hp_ref, y_ref, hs_ref, h_scr):
        c = pl.program_id(1)

        @pl.when(c == 0)
        def _():
            h_scr[...] = jnp.zeros_like(h_scr)

        cm = _ssd_common(dt_ref, hp_ref)
        Bb, Cb = b_ref[...].astype(bf16), c_ref[...].astype(bf16)
        CB = lax.dot_general(Cb, Bb, _DN["nt"], preferred_element_type=f32)
        lo = lax.broadcasted_iota(jnp.int32, (Q, LANE), 1) < HEAD_DIM
        top = lax.broadcasted_iota(jnp.int32, (LANE, LANE), 0) < HEAD_DIM
        Drow = hp_ref[2:3, :]
        for jj in range(PAIRS):
            hA, hB = 2 * jj, 2 * jj + 1
            dtx, ecsx, eendx = (_pair_expand(cm[k], jj, lo) for k in ("dt", "ecs", "eend"))
            xs_p = xs_ref[:, jj * LANE:(jj + 1) * LANE]
            Xd = xs_p * dtx
            Y = None
            for h, Xm in ((hA, jnp.where(lo, Xd, 0.0)), (hB, jnp.where(lo, 0.0, Xd))):
                W = (CB * _decay(cm, h)).astype(bf16)
                t = jnp.dot(W, Xm.astype(bf16), preferred_element_type=f32)
                Y = t if Y is None else Y + t
            Hp = h_scr[jj]
            hs_ref[0, jj] = Hp
            Yoff = lax.dot_general(Cb, Hp.astype(bf16), _DN["nt"], preferred_element_type=f32) * ecsx
            Dx = jnp.where(lo[0:1, :], Drow[:, hA:hA + 1], Drow[:, hB:hB + 1])
            y_ref[:, jj * LANE:(jj + 1) * LANE] = Y + Yoff + xs_p * Dx
            Snew = lax.dot_general((Xd * eendx).astype(bf16), Bb, _DN["tn"], preferred_element_type=f32)
            eTx = jnp.where(top, cm["eT"][:, hA:hA + 1], cm["eT"][:, hB:hB + 1])
            h_scr[jj] = Hp * eTx + Snew

    return pl.pallas_call(
        body, name=name, out_shape=(S((L, D_INNER), f32), S((nc, N_SSM_HEADS // 2, LANE, D_STATE), f32)),
        grid=(N_GROUPS, nc),
        in_specs=[pl.BlockSpec((Q, GW), lambda g, c: (c, g)),
                  pl.BlockSpec((Q, D_STATE), lambda g, c: (c, D_INNER // D_STATE + g)),
                  pl.BlockSpec((Q, D_STATE), lambda g, c: (c, D_INNER // D_STATE + N_GROUPS + g)),
                  pl.BlockSpec((Q, LANE), lambda g, c: (c, g)),
                  pl.BlockSpec((SUBLANE, LANE), lambda g, c: (0, g))],
        out_specs=(pl.BlockSpec((Q, GW), lambda g, c: (c, g)),
                   pl.BlockSpec((1, PAIRS, LANE, D_STATE), lambda g, c: (c, g, 0, 0))),
        scratch_shapes=[pltpu.VMEM((PAIRS, LANE, D_STATE), f32)],
        compiler_params=_cp("arbitrary", "arbitrary"))(act, act, act, dtp, hp)


def _ssd_bwd(act, dtp, hp, dy, hs, *, name):
    L = act.shape[0]
    nc = L // Q

    def body(xs_ref, b_ref, c_ref, dt_ref, hp_ref, dy_ref, hs_ref, dxs_ref, db_ref, dc_ref, ddt_ref, dhp_ref, dh_scr):
        c = pl.program_id(1)

        @pl.when(c == 0)
        def _():
            dh_scr[...] = jnp.zeros_like(dh_scr)
            dhp_ref[...] = jnp.zeros_like(dhp_ref)

        cm = _ssd_common(dt_ref, hp_ref)
        Bb, Cb = b_ref[...].astype(bf16), c_ref[...].astype(bf16)
        CB = lax.dot_general(Cb, Bb, _DN["nt"], preferred_element_type=f32)
        lane = lax.broadcasted_iota(jnp.int32, (Q, LANE), 1)
        sub = lax.broadcasted_iota(jnp.int32, (LANE, LANE), 0)
        lo = lane < HEAD_DIM
        top = sub < HEAD_DIM
        Drow = hp_ref[2:3, :]
        zero = jnp.zeros((Q, LANE), f32)
        dcs, ddtx, dC, dB, dCB = zero, zero, zero, zero, jnp.zeros((Q, Q), f32)
        dD_row = jnp.zeros((1, LANE), f32)
        dT_row = jnp.zeros((1, LANE), f32)
        for jj in range(PAIRS):
            hA, hB = 2 * jj, 2 * jj + 1
            Pj = (lane == jnp.where(top, hA, hB)).astype(bf16)
            dtx, ecsx, eendx = (_pair_expand(cm[k], jj, lo) for k in ("dt", "ecs", "eend"))
            xs_p = xs_ref[:, jj * LANE:(jj + 1) * LANE]
            dY_p = dy_ref[:, jj * LANE:(jj + 1) * LANE]
            Xd = xs_p * dtx
            Xdb = Xd.astype(bf16)
            Hp, dHn = hs_ref[0, jj], dh_scr[jj]
            Hb, dHb = Hp.astype(bf16), dHn.astype(bf16)
            EdYb = (dY_p * ecsx).astype(bf16)
            YoffN = lax.dot_general(Cb, Hb, _DN["nt"], preferred_element_type=f32)
            dC = dC + jnp.dot(EdYb, Hb, preferred_element_type=f32)
            dH_off = lax.dot_general(EdYb, Cb, _DN["tn"], preferred_element_type=f32)
            R = lax.dot_general(Bb, dHb, _DN["nt"], preferred_element_type=f32)
            Xe = Xd * eendx
            dB = dB + jnp.dot(Xe.astype(bf16), dHb, preferred_element_type=f32)
            dXd = R * eendx
            V2 = _sel_right(R * Xe, Pj)
            dcs = dcs + _sel_right(dY_p * YoffN * ecsx, Pj) - V2
            dT_row = dT_row + jnp.sum(V2, axis=0, keepdims=True) \
                + jnp.sum(_sel_right(dHn * Hp, Pj, "tn"), axis=0, keepdims=True) * cm["eT"]
            for h, keep in ((hA, lo), (hB, jnp.logical_not(lo))):
                M = _decay(cm, h)
                Wf = CB * M
                dYm = jnp.where(keep, dY_p, 0.0).astype(bf16)
                dW = lax.dot_general(dYm, Xdb, _DN["nt"], preferred_element_type=f32)
                dXd = dXd + lax.dot_general(Wf.astype(bf16), dYm, _DN["tn"], preferred_element_type=f32)
                Z = dW * Wf
                onesh = (lane == h).astype(bf16)
                dcs = dcs + _sel_right(Z, onesh) - _sel_right(Z, onesh, "tn")
                dCB = dCB + dW * M
            Dx = jnp.where(lo[0:1, :], Drow[:, hA:hA + 1], Drow[:, hB:hB + 1])
            dxs_ref[:, jj * LANE:(jj + 1) * LANE] = dXd * dtx + dY_p * Dx
            ddtx = ddtx + _sel_right(dXd * xs_p, Pj)
            dD_row = dD_row + jnp.sum(_sel_right(dY_p * xs_p, Pj), axis=0, keepdims=True)
            eTx = jnp.where(top, cm["eT"][:, hA:hA + 1], cm["eT"][:, hB:hB + 1])
            dh_scr[jj] = dHn * eTx + dH_off
        dCBb = dCB.astype(bf16)
        dc_ref[...] = dC + jnp.dot(dCBb, Bb, preferred_element_type=f32)
        db_ref[...] = dB + lax.dot_general(dCBb, Cb, _DN["tn"], preferred_element_type=f32)
        dcs = dcs + jnp.where(lax.broadcasted_iota(jnp.int32, (Q, LANE), 0) == Q - 1, dT_row, 0.0)
        da = _sel_left((cm["row"] <= cm["col"]).astype(bf16), dcs)
        ddt_pre = (da * cm["A"] + ddtx) * _sigmoid(cm["dt_pre"])
        ddt_ref[...] = ddt_pre
        r8 = lax.broadcasted_iota(jnp.int32, (SUBLANE, LANE), 0)
        dhp_ref[...] += jnp.where(r8 == 0, jnp.sum(ddt_pre, axis=0, keepdims=True),
                                  jnp.where(r8 == 1, jnp.sum(da * cm["dt"], axis=0, keepdims=True) * cm["A"],
                                            jnp.where(r8 == 2, dD_row, 0.0)))

    rev = lambda c: nc - 1 - c
    return pl.pallas_call(
        body, name=name,
        out_shape=(S((L, D_INNER), f32), S((L, N_GROUPS * D_STATE), f32), S((L, N_GROUPS * D_STATE), f32),
                   S((L, N_GROUPS * LANE), f32), S((SUBLANE, N_GROUPS * LANE), f32)),
        grid=(N_GROUPS, nc),
        in_specs=[pl.BlockSpec((Q, GW), lambda g, c: (rev(c), g)),
                  pl.BlockSpec((Q, D_STATE), lambda g, c: (rev(c), D_INNER // D_STATE + g)),
                  pl.BlockSpec((Q, D_STATE), lambda g, c: (rev(c), D_INNER // D_STATE + N_GROUPS + g)),
                  pl.BlockSpec((Q, LANE), lambda g, c: (rev(c), g)),
                  pl.BlockSpec((SUBLANE, LANE), lambda g, c: (0, g)),
                  pl.BlockSpec((Q, GW), lambda g, c: (rev(c), g)),
                  pl.BlockSpec((1, PAIRS, LANE, D_STATE), lambda g, c: (rev(c), g, 0, 0))],
        out_specs=(pl.BlockSpec((Q, GW), lambda g, c: (rev(c), g)),
                   pl.BlockSpec((Q, D_STATE), lambda g, c: (rev(c), g)),
                   pl.BlockSpec((Q, D_STATE), lambda g, c: (rev(c), g)),
                   pl.BlockSpec((Q, LANE), lambda g, c: (rev(c), g)),
                   pl.BlockSpec((SUBLANE, LANE), lambda g, c: (0, g))),
        scratch_shapes=[pltpu.VMEM((PAIRS, LANE, D_STATE), f32)],
        compiler_params=_cp("arbitrary", "arbitrary"))(act, act, act, dtp, hp, dy, hs)


def _group_pad_cols(w):
    lead = w.shape[:-1]
    w = w.reshape(lead + (N_GROUPS, HEADS_PER_GROUP))
    w = jnp.pad(w, [(0, 0)] * len(lead) + [(0, 0), (0, LANE - HEADS_PER_GROUP)])
    return w.reshape(lead + (N_GROUPS * LANE,))


def _group_unpad_cols(w):
    lead = w.shape[:-1]
    return w.reshape(lead + (N_GROUPS, LANE))[..., :HEADS_PER_GROUP].reshape(lead + (N_SSM_HEADS,))


def _row(v):
    return v.reshape(1, -1)


ROW_SHARDED = ('ssm_out_w', 'cf_pw2_w', 'xa_q_w', 'xa_o_w', 'ffn_out_w')
COL_SHARDED = ('cf_pw1_w', 'xa_kv_w', 'ffn_in_w')


MIXER_WEIGHTS = ('ssm_in_w', 'ssm_out_w', 'cf_pw1_w', 'cf_pw2_w')


def _layer_matmul_weights(i, part):
    if part == "mixer":
        return ('ssm_in_w', 'ssm_out_w') if i % 2 == 0 else ('cf_pw1_w', 'cf_pw2_w')
    return ('xa_q_w', 'xa_kv_w', 'xa_o_w', 'ffn_in_w', 'ffn_out_w')


def _device_step(x, mem, target, W, layer_weights, layer_grads, start_after=()):
    ng = W['norm_g']
    lw = []
    for i in range(DEPTH):
        j = i // 2
        p = {}
        if i % 2 == 0:
            p['cw'] = _pad_taps(W['ssm_conv_w'][j], SSM_CONV)
            p['cb'] = _row(W['ssm_conv_b'][j])
            hp = jnp.stack([_group_pad_cols(W['ssm_dt_bias'][j]), _group_pad_cols(W['ssm_A_log'][j]),
                            _group_pad_cols(W['ssm_D'][j])])
            p['hp'] = jnp.pad(hp, ((0, SUBLANE - 3), (0, 0)))
            p['sng'] = _row(W['ssm_norm_g'][j])
        else:
            p['pw1b'] = _row(W['cf_pw1_b'][j])
            p['dww'], p['dwb'] = _pad_taps(W['cf_dw_w'][j], CF_KERNEL), _row(W['cf_dw_b'][j])
            p['lng'], p['lnb'] = _row(W['cf_ln_g'][j]), _row(W['cf_ln_b'][j])
            p['pw2b'] = _row(W['cf_pw2_b'][j])
        p['memg'] = _row(W['xa_mem_g'][i])
        p['fcw'], p['fcb'] = _pad_taps(W['ffn_conv_w'][i], FFN_CONV), _row(W['ffn_conv_b'][i])
        p['g'] = [_row(ng[i, s]) for s in range(6)]
        lw.append(p)

    def wmm(a, wl, wname, mode, **kw):
        return _mm(a, wl[wname], mode, b_shards=wname in COL_SHARDED, **kw)

    saved = []
    X = x
    h = _rmsnorm_fwd(X, lw[0]['g'][0], name="norm_in", after=start_after)
    for i in range(DEPTH):
        p, sv = lw[i], {}
        wl = dict(layer_weights(i, "mixer", X))
        sv['X0'], sv['h'], sv['wl'] = X, h, wl
        if i % 2 == 0:
            win = jnp.concatenate([wl['ssm_in_w'][s] for s in range(N_CHIPS)], axis=1)
            wl['wz'], wl['wx'] = win[:, :D_INNER], win[:, D_INNER:D_INNER + CONV_DIM]
            wl['wdt'] = _group_pad_cols(win[:, D_INNER + CONV_DIM:])
            z = _mm(h, wl['wz'], "nn", name="ssm_z")
            xbc = _mm(h, wl['wx'], "nn", name="ssm_xbc")
            dtp = _mm(h, wl['wdt'], "nn", name="ssm_dt")
            act = _conv_act_fwd(xbc, p['cw'], p['cb'], K=SSM_CONV, act="silu", name="ssm_conv_fwd", out_dtype=f32)
            y, hs = _ssd_fwd(act, dtp, p['hp'], name="ssd_fwd")
            yn = _gated_norm_fwd(y, z, p['sng'], name="ssm_gnorm_fwd")
            mix = wmm(yn, wl, 'ssm_out_w', "nn", name="ssm_out")
            sv.update(z=z, xbc=xbc, dtp=dtp, act=act, y=y, hs=hs, yn=yn)
        else:
            u = wmm(h, wl, 'cf_pw1_w', "nn", name="cf_pw1", bias=p['pw1b'])
            c, s = _cf_fwd(u, p['dww'], p['dwb'], p['lng'], p['lnb'], name="cf_conv_fwd")
            mix = wmm(s, wl, 'cf_pw2_w', "nn", name="cf_pw2", bias=p['pw2b'])
            sv.update(u=u, c=c, s=s)
        wl.update(layer_weights(i, "rest", mix))
        X1, h2 = _resid_norm_fwd(X, mix, p['g'][1], p['g'][2], name="resid_norm_a")
        q = wmm(h2, wl, 'xa_q_w', "nn", name="xa_q", out_dtype=bf16)
        m = _rmsnorm_fwd(mem, p['memg'], name="xa_mem_norm")
        kv = wmm(m, wl, 'xa_kv_w', "nn", name="xa_kv", out_dtype=bf16)
        o = _attn_fwd(q, kv, name="xa_attn_fwd")
        a = wmm(o, wl, 'xa_o_w', "nn", name="xa_o")
        X2, h3 = _resid_norm_fwd(X1, a, p['g'][3], p['g'][4], name="resid_norm_b")
        u0 = wmm(h3, wl, 'ffn_in_w', "nn", name="ffn_in")
        fact = _conv_act_fwd(u0, p['fcw'], p['fcb'], K=FFN_CONV, act="swiglu", name="ffn_conv_fwd", out_dtype=bf16)
        f = wmm(fact, wl, 'ffn_out_w', "nn", name="ffn_out")
        g_next = lw[i + 1]['g'][0] if i + 1 < DEPTH else None
        X3, hn = _resid_norm_fwd(X2, f, p['g'][5], g_next, name="resid_norm_c" if g_next is not None else "resid_norm_last")
        sv.update(mix=mix, X1=X1, h2=h2, q=q, m=m, kv=kv, o=o, a=a, X2=X2, h3=h3, u0=u0, fact=fact, f=f)
        saved.append(sv)
        X, h = X3, hn

    sse, G = _loss_fwd_bwd(X, target, name="loss")

    small = [n for n in WEIGHT_NAMES if n not in MATMUL_WEIGHTS]
    gr = {n: [None] * W[n].shape[0] for n in small}

    def dwmm(gl, a, d, wname, *, name):
        if wname in COL_SHARDED:
            gl[wname] = _mm(a, d, "tn", name=name, out_dtype=bf16, out_shards=True)
        else:
            g = _mm(a, d, "tn", name=name, out_dtype=bf16)
            gl[wname] = g.reshape(N_CHIPS, g.shape[0] // N_CHIPS, g.shape[1])

    dng = [[None] * 6 for _ in range(DEPTH)]
    for i in reversed(range(DEPTH)):
        p, sv, j = lw[i], saved[i], i // 2
        wl, gl = sv['wl'], {}
        df, dng[i][5], _ = _norm_bwd(sv['f'], p['g'][5], G, name="nb_f", out_dtype=bf16)
        dwmm(gl, sv['fact'], df, 'ffn_out_w', name="ffn_out_dw")
        dfact = wmm(df, wl, 'ffn_out_w', "nt", name="ffn_out_dx")
        du, dcw, dcb = _conv_act_bwd(sv['u0'], [dfact], p['fcw'], p['fcb'], K=FFN_CONV, act="swiglu", name="ffn_conv_bwd")
        gr['ffn_conv_w'][i], gr['ffn_conv_b'][i] = dcw[:FFN_CONV], dcb[0]
        du0 = _conv_transpose(du, p['fcw'], K=FFN_CONV, name="ffn_conv_bwd_x")
        dwmm(gl, sv['h3'], du0, 'ffn_in_w', name="ffn_in_dw")
        dh3 = wmm(du0, wl, 'ffn_in_w', "nt", name="ffn_in_dx")
        G, dng[i][4], _ = _norm_bwd(sv['X2'], p['g'][4], dh3, name="nb_x2", add=G)
        da, dng[i][3], _ = _norm_bwd(sv['a'], p['g'][3], G, name="nb_a", out_dtype=bf16)
        dwmm(gl, sv['o'], da, 'xa_o_w', name="xa_o_dw")
        do = wmm(da, wl, 'xa_o_w', "nt", name="xa_o_dx", out_dtype=bf16)
        dq, dkv = _attn_bwd(sv['q'], sv['kv'], do, name="xa_attn_bwd")
        dwmm(gl, sv['h2'], dq, 'xa_q_w', name="xa_q_dw")
        dh2 = wmm(dq, wl, 'xa_q_w', "nt", name="xa_q_dx")
        dwmm(gl, sv['m'], dkv, 'xa_kv_w', name="xa_kv_dw")
        dm = wmm(dkv, wl, 'xa_kv_w', "nt", name="xa_kv_dx")
        _, dmg, _ = _norm_bwd(mem, p['memg'], dm, name="nb_mem")
        gr['xa_mem_g'][i] = dmg[0]
        G, dng[i][2], _ = _norm_bwd(sv['X1'], p['g'][2], dh2, name="nb_x1", add=G)
        behind = tuple(layer_grads(i, "rest", gl))
        dmix, dng[i][1], dmix_sum = _norm_bwd(sv['mix'], p['g'][1], G, name="nb_mix", out_dtype=bf16, after=behind)
        if i % 2 == 0:
            dwmm(gl, sv['yn'], dmix, 'ssm_out_w', name="ssm_out_dw")
            dyn = wmm(dmix, wl, 'ssm_out_w', "nt", name="ssm_out_dx")
            dy, dz, dsng = _gated_norm_bwd(sv['y'], sv['z'], p['sng'], dyn, name="ssm_gnorm_bwd")
            gr['ssm_norm_g'][j] = dsng[0]
            dxs, dB, dC, ddtp, dhp = _ssd_bwd(sv['act'], sv['dtp'], p['hp'], dy, sv['hs'], name="ssd_bwd")
            gr['ssm_dt_bias'][j], gr['ssm_A_log'][j], gr['ssm_D'][j] = (_group_unpad_cols(dhp[r]) for r in range(3))
            dpre, dcw, dcb = _conv_act_bwd(sv['xbc'], [dxs, dB, dC], p['cw'], p['cb'], K=SSM_CONV, act="silu",
                                           name="ssm_conv_bwd")
            gr['ssm_conv_w'][j], gr['ssm_conv_b'][j] = dcw[:SSM_CONV], dcb[0]
            dxbc = _conv_transpose(dpre, p['cw'], K=SSM_CONV, name="ssm_conv_bwd_x")
            hh = sv['h']
            dwz = _mm(hh, dz, "tn", name="ssm_z_dw", out_dtype=bf16)
            dwx = _mm(hh, dxbc, "tn", name="ssm_xbc_dw", out_dtype=bf16)
            dwdt = _mm(hh, ddtp, "tn", name="ssm_dt_dw", out_dtype=bf16)
            din = jnp.concatenate([dwz, dwx, _group_unpad_cols(dwdt)], axis=1)
            gl['ssm_in_w'] = jnp.stack(jnp.split(din, N_CHIPS, axis=1))
            dh = _mm(dz, wl['wz'], "nt", name="ssm_z_dx")
            dh = _mm(dxbc, wl['wx'], "nt", name="ssm_xbc_dx", add=dh)
            dh = _mm(ddtp, wl['wdt'], "nt", name="ssm_dt_dx", add=dh)
        else:
            dwmm(gl, sv['s'], dmix, 'cf_pw2_w', name="cf_pw2_dw")
            gr['cf_pw2_b'][j] = dmix_sum[0]
            ds = wmm(dmix, wl, 'cf_pw2_w', "nt", name="cf_pw2_dx")
            dc, dlg, dlb = _cf_ln_bwd(sv['c'], p['lng'], p['lnb'], ds, name="cf_ln_bwd")
            gr['cf_ln_g'][j], gr['cf_ln_b'][j] = dlg[0], dlb[0]
            du, ddw, ddb, dus = _cf_glu_bwd(sv['u'], dc, p['dww'], name="cf_glu_bwd")
            gr['cf_dw_w'][j], gr['cf_dw_b'][j], gr['cf_pw1_b'][j] = ddw[:CF_KERNEL], ddb[0], dus[0]
            dwmm(gl, sv['h'], du, 'cf_pw1_w', name="cf_pw1_dw")
            dh = wmm(du, wl, 'cf_pw1_w', "nt", name="cf_pw1_dx")
        behind = tuple(layer_grads(i, "mixer", gl))
        G, dng[i][0], _ = _norm_bwd(sv['X0'], p['g'][0], dh, name="nb_x0", add=G, after=behind)
    gr['norm_g'] = [jnp.concatenate(dng[i], axis=0) for i in range(DEPTH)]
    gsmall = {n: jnp.stack(gr[n]) for n in small}
    return sse, G, gsmall


MESH = pl.DeviceIdType.MESH
HBM_SPEC = pl.BlockSpec(memory_space=pltpu.HBM)


def _chip_peers(x, y):
    return [(1 - x, y), (x, 1 - y), (1 - x, 1 - y)]


def _all_gather_chips(buf, *, name):
    R, C = buf.shape

    def body(in_ref, out_ref, send_sems, recv_sems, local_sem):
        x, y, c = lax.axis_index("x"), lax.axis_index("y"), lax.axis_index("c")
        me = 2 * x + y
        mine = pltpu.make_async_copy(in_ref, out_ref.at[me], local_sem)
        mine.start()
        peers = _chip_peers(x, y)
        sends = []
        for k, (px, py) in enumerate(peers):
            cp = pltpu.make_async_remote_copy(src_ref=in_ref, dst_ref=out_ref.at[me], send_sem=send_sems.at[k],
                                              recv_sem=recv_sems.at[k], device_id=(px, py, c), device_id_type=MESH)
            cp.start()
            sends.append(cp)
        for k, (px, py) in enumerate(peers):
            pltpu.make_async_remote_copy(src_ref=in_ref, dst_ref=out_ref.at[2 * px + py], send_sem=send_sems.at[k],
                                         recv_sem=recv_sems.at[k], device_id=(px, py, c), device_id_type=MESH).wait_recv()
        for cp in sends:
            cp.wait_send()
        mine.wait()

    return pl.pallas_call(body, name=name, out_shape=S((N_CHIPS, R, C), buf.dtype), in_specs=[HBM_SPEC], out_specs=HBM_SPEC,
                          scratch_shapes=[pltpu.SemaphoreType.DMA((3,)), pltpu.SemaphoreType.DMA((3,)),
                                          pltpu.SemaphoreType.DMA(())])(buf)


def _remote(src, dst, send_sem, recv_sem, device):
    return pltpu.make_async_remote_copy(src_ref=src, dst_ref=dst, send_sem=send_sem, recv_sem=recv_sem,
                                        device_id=device, device_id_type=MESH)


def _gather_matmul_weights(shards, *, name):
    n = len(shards)

    def body(*refs):
        ins, outs = refs[:n], refs[n:2 * n]
        send, recv, fsend, frecv, lsem = refs[2 * n:]
        x, y, c = lax.axis_index("x"), lax.axis_index("y"), lax.axis_index("c")
        me, sib = 2 * x + y, (x, y, 1 - c)
        peers = _chip_peers(x, y)
        started, local = [], []
        for w in range(n):
            cp = pltpu.make_async_copy(ins[w], outs[w].at[:, me], lsem.at[w])
            cp.start()
            local.append(cp)
            for k, (px, py) in enumerate(peers):
                cp = _remote(ins[w].at[:, c], outs[w].at[:, me, c], send.at[w, k], recv.at[w, k], (px, py, c))
                cp.start()
                started.append(cp)
        for w in range(n):
            for k, (px, py) in enumerate(peers):
                landed = outs[w].at[:, 2 * px + py, c]
                _remote(ins[w].at[:, c], landed, send.at[w, k], recv.at[w, k], (px, py, c)).wait_recv()
                cp = _remote(landed, landed, fsend.at[w, k], frecv.at[w, k], sib)
                cp.start()
                started.append(cp)
        for w in range(n):
            for k, (px, py) in enumerate(peers):
                _remote(ins[w].at[:, c], outs[w].at[:, 2 * px + py, 1 - c], fsend.at[w, k], frecv.at[w, k], sib).wait_recv()
        for cp in started:
            cp.wait_send()
        for cp in local:
            cp.wait()

    out_shape = tuple(S((s.shape[0], N_CHIPS) + s.shape[1:], s.dtype) for s in shards)
    sems = [pltpu.SemaphoreType.DMA((n, 3)) for _ in range(4)] + [pltpu.SemaphoreType.DMA((n,))]
    return pl.pallas_call(body, name=name, out_shape=out_shape, in_specs=[HBM_SPEC] * n, out_specs=(HBM_SPEC,) * n,
                          scratch_shapes=sems)(*shards)


SEM_SPEC = pl.BlockSpec(memory_space=pltpu.SEMAPHORE)
VMEM_SPEC = pl.BlockSpec(memory_space=pltpu.VMEM)


def _in_hbm(a):
    return pltpu.with_memory_space_constraint(a, pltpu.HBM)


def _chip_targets(x, y):
    return [(x, y), (1 - x, y), (x, 1 - y), (1 - x, 1 - y)]


def _spread_start(srcs, scatter, *, name, after=()):
    n = len(srcs)
    lands = [lax.empty((N_CHIPS,) + (s.shape[1:] if scatter else s.shape), s.dtype) for s in srcs]

    def body(*refs):
        src, land = refs[:n], refs[n:2 * n]
        send, recv, token = refs[2 * n + len(after)], refs[2 * n + len(after) + 1], refs[-1]
        x, y, c = lax.axis_index("x"), lax.axis_index("y"), lax.axis_index("c")
        me = 2 * x + y
        for w in range(n):
            for k, (px, py) in enumerate(_chip_targets(x, y)):
                block = src[w].at[2 * px + py] if scatter else src[w]
                _remote(block, land[w].at[me], send.at[N_CHIPS * w + k], recv.at[N_CHIPS * w + k], (px, py, c)).start()
        token[...] = jnp.zeros_like(token)

    thru = tuple(pltpu.HBM(a.shape, a.dtype) for a in list(srcs) + lands)
    sems = (pltpu.SemaphoreType.DMA((N_CHIPS * n,)), pltpu.SemaphoreType.DMA((N_CHIPS * n,)))
    out = pl.pallas_call(
        body, name=name, out_shape=sems + thru + (S((SUBLANE, LANE), f32),),
        in_specs=[HBM_SPEC] * (2 * n) + [ANY_SPEC] * len(after),
        out_specs=(SEM_SPEC, SEM_SPEC) + (HBM_SPEC,) * (2 * n) + (VMEM_SPEC,),
        input_output_aliases={i: 2 + i for i in range(2 * n)},
        compiler_params=pltpu.CompilerParams(has_side_effects=pltpu.SideEffectType.DATAFLOW_SIDE_EFFECTING),
    )(*[_in_hbm(a) for a in list(srcs) + lands], *after)
    return out[0], out[1], out[2:2 + n], out[2 + n:2 + 2 * n], out[-1]


def _spread_wait(send, recv, srcs, lands, after, scatter, *, name):
    n = len(srcs)

    def body(*refs):
        src, land, send, recv = refs[:n], refs[n:2 * n], refs[2 * n], refs[2 * n + 1]
        x, y, c = lax.axis_index("x"), lax.axis_index("y"), lax.axis_index("c")
        me = 2 * x + y
        for w in range(n):
            for k, (px, py) in enumerate(_chip_targets(x, y)):
                block = src[w].at[me] if scatter else src[w]
                cp = _remote(block, land[w].at[2 * px + py], send.at[N_CHIPS * w + k], recv.at[N_CHIPS * w + k], (px, py, c))
                cp.wait_send()
                cp.wait_recv()

    thru = tuple(pltpu.HBM(a.shape, a.dtype) for a in list(srcs) + list(lands))
    out = pl.pallas_call(
        body, name=name, out_shape=thru,
        in_specs=[HBM_SPEC] * (2 * n) + [SEM_SPEC, SEM_SPEC] + [ANY_SPEC] * len(after), out_specs=(HBM_SPEC,) * (2 * n),
        input_output_aliases={i: i for i in range(2 * n)},
        compiler_params=pltpu.CompilerParams(has_side_effects=pltpu.SideEffectType.DATAFLOW_SIDE_EFFECTING),
    )(*srcs, *lands, send, recv, *after)
    return out[:n], out[n:]


def _swap_sibling(bufs, *, name):
    n = len(bufs)

    def body(*refs):
        src, out, send, recv = refs[:n], refs[n:2 * n], refs[-2], refs[-1]
        sib = (lax.axis_index("x"), lax.axis_index("y"), 1 - lax.axis_index("c"))
        copies = [_remote(src[w], out[w], send.at[w], recv.at[w], sib) for w in range(n)]
        for cp in copies:
            cp.start()
        for cp in copies:
            cp.wait()

    return pl.pallas_call(body, name=name, out_shape=tuple(S(a.shape, a.dtype) for a in bufs),
                          in_specs=[HBM_SPEC] * n, out_specs=(HBM_SPEC,) * n,
                          scratch_shapes=[pltpu.SemaphoreType.DMA((n,)), pltpu.SemaphoreType.DMA((n,))])(*bufs)


def _swap_start(bufs, *, name):
    n = len(bufs)
    lands = [lax.empty(b.shape, b.dtype) for b in bufs]

    def body(*refs):
        src, land, send, recv, token = refs[:n], refs[n:2 * n], refs[2 * n], refs[2 * n + 1], refs[-1]
        sib = (lax.axis_index("x"), lax.axis_index("y"), 1 - lax.axis_index("c"))
        for w in range(n):
            _remote(src[w], land[w], send.at[w], recv.at[w], sib).start()
        token[...] = jnp.zeros_like(token)

    thru = tuple(pltpu.HBM(a.shape, a.dtype) for a in list(bufs) + lands)
    out = pl.pallas_call(
        body, name=name,
        out_shape=(pltpu.SemaphoreType.DMA((n,)), pltpu.SemaphoreType.DMA((n,))) + thru + (S((SUBLANE, LANE), f32),),
        in_specs=[HBM_SPEC] * (2 * n), out_specs=(SEM_SPEC, SEM_SPEC) + (HBM_SPEC,) * (2 * n) + (VMEM_SPEC,),
        input_output_aliases={i: 2 + i for i in range(2 * n)},
        compiler_params=pltpu.CompilerParams(has_side_effects=pltpu.SideEffectType.DATAFLOW_SIDE_EFFECTING),
    )(*[_in_hbm(a) for a in list(bufs) + lands])
    return out[0], out[1], out[2:2 + n], out[2 + n:2 + 2 * n], out[-1]


def _swap_wait(send, recv, bufs, lands, after, *, name):
    n = len(bufs)

    def body(*refs):
        src, land, send, recv = refs[:n], refs[n:2 * n], refs[2 * n], refs[2 * n + 1]
        sib = (lax.axis_index("x"), lax.axis_index("y"), 1 - lax.axis_index("c"))
        for w in range(n):
            cp = _remote(src[w], land[w], send.at[w], recv.at[w], sib)
            cp.wait_send()
            cp.wait_recv()

    thru = tuple(pltpu.HBM(a.shape, a.dtype) for a in list(bufs) + list(lands))
    out = pl.pallas_call(
        body, name=name, out_shape=thru,
        in_specs=[HBM_SPEC] * (2 * n) + [SEM_SPEC, SEM_SPEC] + [ANY_SPEC] * len(after), out_specs=(HBM_SPEC,) * (2 * n),
        input_output_aliases={i: i for i in range(2 * n)},
        compiler_params=pltpu.CompilerParams(has_side_effects=pltpu.SideEffectType.DATAFLOW_SIDE_EFFECTING),
    )(*bufs, *lands, send, recv, *after)
    return out[:n], out[n:]


N_DEVICES = 8


def _allgather_devices(buf, *, name):
    R, C = buf.shape

    def body(in_ref, out_ref, send, recv, lsem):
        x, y, c = lax.axis_index("x"), lax.axis_index("y"), lax.axis_index("c")
        me = 4 * x + 2 * y + c
        mine = pltpu.make_async_copy(in_ref, out_ref.at[me], lsem)
        mine.start()
        flips = [(d >> 2 & 1, d >> 1 & 1, d & 1) for d in range(1, N_DEVICES)]
        peers = [(1 - x if fx else x, 1 - y if fy else y, 1 - c if fc else c) for fx, fy, fc in flips]
        sends = []
        for k, peer in enumerate(peers):
            cp = _remote(in_ref, out_ref.at[me], send.at[k], recv.at[k], peer)
            cp.start()
            sends.append(cp)
        for k, (px, py, pc) in enumerate(peers):
            _remote(in_ref, out_ref.at[4 * px + 2 * py + pc], send.at[k], recv.at[k], (px, py, pc)).wait_recv()
        for cp in sends:
            cp.wait_send()
        mine.wait()

    return pl.pallas_call(body, name=name, out_shape=S((N_DEVICES, R, C), buf.dtype), in_specs=[HBM_SPEC], out_specs=HBM_SPEC,
                          scratch_shapes=[pltpu.SemaphoreType.DMA((N_DEVICES - 1,)), pltpu.SemaphoreType.DMA((N_DEVICES - 1,)),
                                          pltpu.SemaphoreType.DMA(())])(buf)


def _sum_slots(buf, *, name):
    ns, R, C = buf.shape
    tr = _pick(R, 512)
    assert R % tr == 0

    def body(*refs):
        acc = refs[0][...]
        for r in refs[1:ns]:
            acc = acc + r[...]
        refs[ns][...] = acc

    specs = [pl.BlockSpec((None, tr, C), functools.partial(lambda s, i: (s, i, 0), s)) for s in range(ns)]
    return pl.pallas_call(body, name=name, out_shape=S((R, C), buf.dtype), grid=(R // tr,), in_specs=specs,
                          out_specs=pl.BlockSpec((tr, C), lambda i: (i, 0)), compiler_params=_cp("parallel"))(*([buf] * ns))


ADAMW_BLOCK_BYTES = 1 << 20


def _adamw(w, m, v, groups, *, name, layer=None, prev=None):
    shape = w.shape if layer is None else w.shape[1:]
    C = shape[-1]
    Rr = math.prod(shape[:-1])
    tr = Rr
    if Rr * C * 4 > ADAMW_BLOCK_BYTES:
        tr = max(t for t in range(2 * SUBLANE, Rr + 1, 2 * SUBLANE) if Rr % t == 0 and t * C * 4 <= ADAMW_BLOCK_BYTES)
    c1 = 1.0 / (1.0 - ADAM_B1 ** ADAM_STEP)
    c2 = 1.0 / (1.0 - ADAM_B2 ** ADAM_STEP)
    if layer is None:
        to2 = lambda t: t.reshape(Rr, C)
        spec = pl.BlockSpec((tr, C), lambda i: (i, 0))
        res_shape = S((Rr, C), f32)
    else:
        to2 = lambda t: t.reshape(layer[1], Rr, C)
        spec = pl.BlockSpec((None, tr, C), functools.partial(lambda l, i: (l, i, 0), layer[0]))
        res_shape = S((layer[1], Rr, C), f32)
    wspec, spec = spec, pl.BlockSpec((tr, C), lambda i: (i, 0))
    g_specs, g_args, sizes = [], [], []
    for grp in groups:
        sizes.append(len(grp))
        for term in grp:
            if isinstance(term, tuple):
                arr, slot = term
                g_specs.append(pl.BlockSpec((None, tr, C), functools.partial(lambda s, i: (s, i, 0), slot)))
                g_args.append(arr.reshape(arr.shape[0], Rr, C))
            else:
                g_specs.append(spec)
                g_args.append(term.reshape(Rr, C))
    nterms = len(g_args)
    prev = () if prev is None else tuple(to2(t) for t in prev)

    def body(w_ref, m_ref, v_ref, *rest):
        t_refs, (g_ref, d_ref, mo_ref, vo_ref) = rest[:nterms], rest[-4:]
        g, pos = None, 0
        for size in sizes:
            part = None
            for r in t_refs[pos:pos + size]:
                t = r[...].astype(f32)
                part = t if part is None else part + t
            pos += size
            g = part if g is None else g + part
        mn = ADAM_B1 * m_ref[...] + (1.0 - ADAM_B1) * g
        vn = ADAM_B2 * v_ref[...] + (1.0 - ADAM_B2) * (g * g)
        g_ref[...] = g
        mo_ref[...] = mn
        vo_ref[...] = vn
        d_ref[...] = -ADAM_LR * ((mn * c1) / (jnp.sqrt(vn * c2) + ADAM_EPS) + ADAM_WD * w_ref[...])

    out = pl.pallas_call(body, name=name, out_shape=(res_shape,) * 4, grid=(Rr // tr,),
                         in_specs=[wspec] * 3 + g_specs + [ANY_SPEC] * len(prev), out_specs=(wspec,) * 4,
                         input_output_aliases={3 + nterms + k: k for k in range(len(prev))},
                         compiler_params=_cp("parallel"))(to2(w), to2(m), to2(v), *g_args, *prev)
    return tuple(o.reshape(w.shape) for o in out)


def _pack_rows(parts, dtype):
    flat = jnp.concatenate([p.reshape(-1).astype(dtype) for p in parts])
    n = flat.shape[0]
    unit = PACK_COLS * 2 * SUBLANE
    padded = -(-n // unit) * unit
    return jnp.pad(flat, (0, padded - n)).reshape(padded // PACK_COLS, PACK_COLS)


def _unpack_rows(flat2d, shapes):
    flat = flat2d.reshape(-1)
    out, off = [], 0
    for shp in shapes:
        n = math.prod(shp)
        out.append(flat[off:off + n].reshape(shp))
        off += n
    return out


def _gather_weights(local, names, dtype, *, name):
    shapes = [local[n].shape for n in names]
    got = _all_gather_chips(_pack_rows([local[n] for n in names], dtype), name=name)
    per_chip = [_unpack_rows(got[s], shapes) for s in range(N_CHIPS)]
    return {n: jnp.concatenate([per_chip[s][k] for s in range(N_CHIPS)], axis=SHARD_AXIS[n]) for k, n in enumerate(names)}


def kernel(x, mem, norm_g, ssm_in_w, ssm_conv_w, ssm_conv_b, ssm_dt_bias, ssm_A_log, ssm_D, ssm_norm_g, ssm_out_w, cf_pw1_w, cf_pw1_b, cf_dw_w, cf_dw_b, cf_ln_g, cf_ln_b, cf_pw2_w, cf_pw2_b, xa_mem_g, xa_q_w, xa_kv_w, xa_o_w, ffn_in_w, ffn_conv_w, ffn_conv_b, ffn_out_w, loss_target, m_norm_g, m_ssm_in_w, m_ssm_conv_w, m_ssm_conv_b, m_ssm_dt_bias, m_ssm_A_log, m_ssm_D, m_ssm_norm_g, m_ssm_out_w, m_cf_pw1_w, m_cf_pw1_b, m_cf_dw_w, m_cf_dw_b, m_cf_ln_g, m_cf_ln_b, m_cf_pw2_w, m_cf_pw2_b, m_xa_mem_g, m_xa_q_w, m_xa_kv_w, m_xa_o_w, m_ffn_in_w, m_ffn_conv_w, m_ffn_conv_b, m_ffn_out_w, v_norm_g, v_ssm_in_w, v_ssm_conv_w, v_ssm_conv_b, v_ssm_dt_bias, v_ssm_A_log, v_ssm_D, v_ssm_norm_g, v_ssm_out_w, v_cf_pw1_w, v_cf_pw1_b, v_cf_dw_w, v_cf_dw_b, v_cf_ln_g, v_cf_ln_b, v_cf_pw2_w, v_cf_pw2_b, v_xa_mem_g, v_xa_q_w, v_xa_kv_w, v_xa_o_w, v_ffn_in_w, v_ffn_conv_w, v_ffn_conv_b, v_ffn_out_w):
    w_local = dict(zip(WEIGHT_NAMES, (norm_g, ssm_in_w, ssm_conv_w, ssm_conv_b, ssm_dt_bias, ssm_A_log, ssm_D, ssm_norm_g,
                                      ssm_out_w, cf_pw1_w, cf_pw1_b, cf_dw_w, cf_dw_b, cf_ln_g, cf_ln_b, cf_pw2_w, cf_pw2_b,
                                      xa_mem_g, xa_q_w, xa_kv_w, xa_o_w, ffn_in_w, ffn_conv_w, ffn_conv_b, ffn_out_w)))
    m_local = dict(zip(WEIGHT_NAMES, (m_norm_g, m_ssm_in_w, m_ssm_conv_w, m_ssm_conv_b, m_ssm_dt_bias, m_ssm_A_log, m_ssm_D,
                                      m_ssm_norm_g, m_ssm_out_w, m_cf_pw1_w, m_cf_pw1_b, m_cf_dw_w, m_cf_dw_b, m_cf_ln_g,
                                      m_cf_ln_b, m_cf_pw2_w, m_cf_pw2_b, m_xa_mem_g, m_xa_q_w, m_xa_kv_w, m_xa_o_w,
                                      m_ffn_in_w, m_ffn_conv_w, m_ffn_conv_b, m_ffn_out_w)))
    v_local = dict(zip(WEIGHT_NAMES, (v_norm_g, v_ssm_in_w, v_ssm_conv_w, v_ssm_conv_b, v_ssm_dt_bias, v_ssm_A_log, v_ssm_D,
                                      v_ssm_norm_g, v_ssm_out_w, v_cf_pw1_w, v_cf_pw1_b, v_cf_dw_w, v_cf_dw_b, v_cf_ln_g,
                                      v_cf_ln_b, v_cf_pw2_w, v_cf_pw2_b, v_xa_mem_g, v_xa_q_w, v_xa_kv_w, v_xa_o_w,
                                      v_ffn_in_w, v_ffn_conv_w, v_ffn_conv_b, v_ffn_out_w)))

    small = [n for n in WEIGHT_NAMES if n not in MATMUL_WEIGHTS]
    small_sharded = [n for n in small if SHARD_AXIS[n] is not None]
    W = {n: w_local[n] for n in small if SHARD_AXIS[n] is None}
    W.update(_gather_weights(w_local, small_sharded, f32, name="gather_small_weights"))

    def layer_index(n, i):
        return i // 2 if n in MIXER_WEIGHTS else i

    def keys_of(i, parts):
        return [(n, layer_index(n, i)) for part in parts for n in _layer_matmul_weights(i, part)]

    def shards(keys):
        return [w_local[n][l].astype(bf16) for n, l in keys]

    def usable(n, a):
        return a.reshape(N_CHIPS * a.shape[1], a.shape[2]) if n in ROW_SHARDED else a

    mixer0 = keys_of(0, ("mixer",))
    got0 = _gather_matmul_weights([s.reshape(1, 2, s.shape[0] // 2, s.shape[1]) for s in shards(mixer0)],
                                  name="gather_layer0_mixer")
    gather_groups = {(0, "rest"): keys_of(0, ("rest",))}
    gather_groups.update({(i, "mixer"): keys_of(i, ("mixer", "rest")) for i in range(1, DEPTH)})
    gathers, tokens, landed = {}, [], {}
    for gkey in sorted(gather_groups):
        send, recv, srcs, lands, token = _spread_start(shards(gather_groups[gkey]), False, name="gather_start_%d_%s" % gkey,
                                                       after=(got0[0], W[small_sharded[0]]))
        gathers[gkey] = (send, recv, srcs, lands)
        tokens.append(token)

    def layer_weights(i, part, after):
        if (i, part) == (0, "mixer"):
            landed.update({k: g.reshape((N_CHIPS, 2 * g.shape[3], g.shape[4])) for k, g in zip(mixer0, got0)})
        elif (i, part) in gathers:
            _, lands = _spread_wait(*gathers[i, part], (after,), False, name="gather_wait_%d_%s" % (i, part))
            landed.update(zip(gather_groups[i, part], lands))
        return {n: usable(n, landed[n, layer_index(n, i)]) for n in _layer_matmul_weights(i, part)}

    pending, scatters, swaps, own, sib = {}, {}, {}, {}, {}

    def scatter_start(gkey, keys, gl):
        send, recv, srcs, lands, token = _spread_start([gl[k] for k in keys], True, name="grads_start_%d_%s" % gkey)
        scatters[gkey] = (keys, send, recv, srcs, lands)
        return token

    def layer_grads(i, part, gl):
        grads = {(n, layer_index(n, i)): gl[n] for n in _layer_matmul_weights(i, part)}
        behind = []
        if part == "rest":
            if i + 1 < DEPTH:
                keys, send, recv, srcs, lands = scatters.pop((i + 1, "mixer"))
                _, lands = _spread_wait(send, recv, srcs, lands, (grads['xa_kv_w', i],), True,
                                        name="grads_wait_%d" % (i + 1))
                send, recv, srcs, lands, token = _swap_start(lands, name="grads_swap_start_%d" % (i + 1))
                swaps[i + 1] = (keys, send, recv, srcs, lands)
                behind.append(token)
            if i == 0:
                behind.append(scatter_start((0, "rest"), list(grads), grads))
            else:
                pending.update(grads)
        else:
            pending.update(grads)
            behind.append(scatter_start((i, "mixer"), list(pending), dict(pending)))
            pending.clear()
        return behind

    sse, gx, gsmall = _device_step(x[0], mem[0], loss_target[0], W, layer_weights, layer_grads, tuple(tokens))

    loss = lax.psum(0.5 * sse[0, 0] / D_MODEL, ("x", "y", "c"))

    last_keys, last_lands = [], []
    for gkey in sorted(scatters):
        keys, send, recv, srcs, lands = scatters[gkey]
        _, lands = _spread_wait(send, recv, srcs, lands, (gx,), True, name="grads_wait_%d_%s" % gkey)
        last_keys += keys
        last_lands += list(lands)
    own.update(zip(last_keys, last_lands))
    sib.update(zip(last_keys, _swap_sibling(last_lands, name="grads_swap_last")))
    for i in sorted(swaps):
        keys, send, recv, srcs, lands = swaps[i]
        mine, theirs = _swap_wait(send, recv, srcs, lands, (gx,), name="grads_swap_wait_%d" % i)
        own.update(zip(keys, mine))
        sib.update(zip(keys, theirs))
    small_shapes = [gsmall[n].shape for n in small]
    slots = _allgather_devices(_pack_rows([gsmall[n] for n in small], f32), name="allgather_small_grads")
    gsum = dict(zip(small, _unpack_rows(_sum_slots(slots, name="sum_small_grads"), small_shapes)))
    chip = 2 * lax.axis_index("x") + lax.axis_index("y")

    res = {}
    for n in MATMUL_WEIGHTS:
        layers, out = w_local[n].shape[0], None
        for l in range(layers):
            groups = [[(own[n, l], s) for s in range(N_CHIPS)], [(sib[n, l], s) for s in range(N_CHIPS)]]
            out = _adamw(w_local[n], m_local[n], v_local[n], groups, name="adamw_%s_%d" % (n, l), layer=(l, layers), prev=out)
        res[n] = out
    for n in small:
        g, ax = gsum[n], SHARD_AXIS[n]
        if ax is not None:
            width = w_local[n].shape[ax]
            g = lax.dynamic_slice_in_dim(g, chip * width, width, axis=ax)
        res[n] = _adamw(w_local[n], m_local[n], v_local[n], [[g]], name="adamw_" + n)
    return (loss, gx[None], *[res[n][0] for n in WEIGHT_NAMES], *[res[n][1] for n in WEIGHT_NAMES],
            *[res[n][2] for n in WEIGHT_NAMES], *[res[n][3] for n in WEIGHT_NAMES])
```

```python
import functools
import math

import jax
import jax.numpy as jnp
from jax import lax
from jax.experimental import pallas as pl
from jax.experimental.pallas import tpu as pltpu

f32 = jnp.float32
bf16 = jnp.bfloat16
S = jax.ShapeDtypeStruct

D_MODEL = 1024
DEPTH = 4
D_INNER = 2048
HEAD_DIM = 64
N_GROUPS = 4
HEADS_PER_GROUP = 8
N_SSM_HEADS = 32
D_STATE = 128
CHUNK = 128
SSM_CONV = 4
CONV_DIM = 3072
CF_KERNEL = 31
N_MEM = 256
XA_HEADS = 4
XA_HEAD_DIM = 256
D_FF = 2816
FFN_CONV = 3
EPS = 1e-6
ADAM_LR, ADAM_B1, ADAM_B2, ADAM_EPS, ADAM_WD, ADAM_STEP = 0.001, 0.9, 0.999, 1e-08, 0.01, 10

LANE = 128
SUBLANE = 8
ROW_SUB = 64
VMEM_LIMIT = 56 * 1024 * 1024
N_CHIPS = 4
PACK_COLS = 1024

WEIGHT_NAMES = ['norm_g', 'ssm_in_w', 'ssm_conv_w', 'ssm_conv_b', 'ssm_dt_bias', 'ssm_A_log', 'ssm_D', 'ssm_norm_g',
                'ssm_out_w', 'cf_pw1_w', 'cf_pw1_b', 'cf_dw_w', 'cf_dw_b', 'cf_ln_g', 'cf_ln_b', 'cf_pw2_w', 'cf_pw2_b',
                'xa_mem_g', 'xa_q_w', 'xa_kv_w', 'xa_o_w', 'ffn_in_w', 'ffn_conv_w', 'ffn_conv_b', 'ffn_out_w']
SHARD_AXIS = {'norm_g': 2, 'ssm_in_w': 2, 'ssm_conv_w': 2, 'ssm_conv_b': None, 'ssm_dt_bias': None, 'ssm_A_log': None,
              'ssm_D': None, 'ssm_norm_g': None, 'ssm_out_w': 1, 'cf_pw1_w': 2, 'cf_pw1_b': 1, 'cf_dw_w': 2, 'cf_dw_b': 1,
              'cf_ln_g': 1, 'cf_ln_b': 1, 'cf_pw2_w': 1, 'cf_pw2_b': 1, 'xa_mem_g': None, 'xa_q_w': 1, 'xa_kv_w': 2,
              'xa_o_w': 1, 'ffn_in_w': 2, 'ffn_conv_w': 2, 'ffn_conv_b': None, 'ffn_out_w': 1}
MATMUL_WEIGHTS = ('ssm_in_w', 'ssm_out_w', 'cf_pw1_w', 'cf_pw2_w', 'xa_q_w', 'xa_kv_w', 'xa_o_w', 'ffn_in_w', 'ffn_out_w')


def _cp(*sem):
    return pltpu.CompilerParams(dimension_semantics=tuple(sem), vmem_limit_bytes=VMEM_LIMIT)


def _pick(dim, pref):
    if dim <= pref:
        return dim
    best = None
    for t in range(LANE, pref + 1, LANE):
        if dim % t == 0:
            best = t
    assert best is not None, (dim, pref)
    return best


def _sigmoid(x):
    return 1.0 / (1.0 + jnp.exp(-x))


def _silu(x):
    return x * _sigmoid(x)


def _dsilu(x):
    s = _sigmoid(x)
    return s * (1.0 + x * (1.0 - s))


def _softplus(x):
    return jnp.maximum(x, 0.0) + jnp.log(1.0 + jnp.exp(-jnp.abs(x)))


_DN = {"nn": (((1,), (0,)), ((), ())), "nt": (((1,), (1,)), ((), ())), "tn": (((0,), (0,)), ((), ()))}


def _mm(a, b, mode, *, name, out_dtype=f32, bias=None, add=None, b_shards=False, out_shards=False):
    bshape = (b.shape[1], b.shape[2] * N_CHIPS) if b_shards else b.shape
    if mode == "nn":
        (M, K), (K2, N) = a.shape, bshape
    elif mode == "nt":
        (M, K), (N, K2) = a.shape, bshape
    else:
        (K, M), (K2, N) = a.shape, bshape
    assert K == K2, (a.shape, b.shape, mode)
    n_unit = N // N_CHIPS if ((b_shards and mode == "nn") or out_shards) else N
    k_unit = K // N_CHIPS if (b_shards and mode == "nt") else K
    tm, tn, tk = _pick(M, 1024), _pick(n_unit, 1408), _pick(k_unit, 1408)
    nk, nj_u, nk_u = K // tk, n_unit // tn, k_unit // tk
    a_spec = {"nn": pl.BlockSpec((tm, tk), lambda i, j, k: (i, k)), "nt": pl.BlockSpec((tm, tk), lambda i, j, k: (i, k)),
              "tn": pl.BlockSpec((tk, tm), lambda i, j, k: (k, i))}[mode]
    if not b_shards:
        b_spec = {"nn": pl.BlockSpec((tk, tn), lambda i, j, k: (k, j)), "nt": pl.BlockSpec((tn, tk), lambda i, j, k: (j, k)),
                  "tn": pl.BlockSpec((tk, tn), lambda i, j, k: (k, j))}[mode]
    else:
        b_spec = {"nn": pl.BlockSpec((None, tk, tn), lambda i, j, k: (j // nj_u, k, j % nj_u)),
                  "nt": pl.BlockSpec((None, tn, tk), lambda i, j, k: (k // nk_u, j, k % nk_u))}[mode]
    in_specs, args = [a_spec, b_spec], [a, b]
    if bias is not None:
        in_specs.append(pl.BlockSpec((1, tn), lambda i, j, k: (0, j)))
        args.append(bias)
    if add is not None:
        in_specs.append(pl.BlockSpec((tm, tn), lambda i, j, k: (i, j)))
        args.append(add)
    if not out_shards:
        out_shape, out_spec = S((M, N), out_dtype), pl.BlockSpec((tm, tn), lambda i, j, k: (i, j))
    else:
        out_shape = S((N_CHIPS, M, n_unit), out_dtype)
        out_spec = pl.BlockSpec((None, tm, tn), lambda i, j, k: (j // nj_u, i, j % nj_u))
    dn = _DN[mode]
    has_bias, has_add = bias is not None, add is not None

    def body(a_ref, b_ref, *rest):
        rest = list(rest)
        bias_ref = rest.pop(0) if has_bias else None
        add_ref = rest.pop(0) if has_add else None
        o_ref = rest[0]

        def finish(r):
            if has_bias:
                r = r + bias_ref[...]
            if has_add:
                r = r + add_ref[...].astype(f32)
            o_ref[...] = r.astype(out_dtype)

        part = lax.dot_general(a_ref[...].astype(bf16), b_ref[...].astype(bf16), dn, preferred_element_type=f32)
        if nk == 1:
            finish(part)
            return
        acc_ref = rest[1]
        k = pl.program_id(2)

        @pl.when(k == 0)
        def _():
            acc_ref[...] = part

        @pl.when(k > 0)
        def _():
            acc_ref[...] += part

        @pl.when(k == nk - 1)
        def _():
            finish(acc_ref[...])

    return pl.pallas_call(
        body, name=name, out_shape=out_shape, grid=(M // tm, N // tn, nk),
        in_specs=in_specs, out_specs=out_spec, scratch_shapes=[pltpu.VMEM((tm, tn), f32)] if nk > 1 else [],
        compiler_params=_cp("parallel", "parallel", "arbitrary"))(*args)


def _rows(tm, C):
    return pl.BlockSpec((tm, C), lambda i: (i, 0))


def _const(shape):
    return pl.BlockSpec(shape, lambda i: tuple(0 for _ in shape))


def _rms_val(x, g):
    r = lax.rsqrt(jnp.mean(x * x, axis=-1, keepdims=True) + EPS)
    return x * r * g


def _rms_bwd_val(x, g, dy):
    r = lax.rsqrt(jnp.mean(x * x, axis=-1, keepdims=True) + EPS)
    xn = x * r
    dxh = dy * g
    dx = r * (dxh - xn * jnp.mean(dxh * xn, axis=-1, keepdims=True))
    return dx, jnp.sum(dy * xn, axis=0, keepdims=True)


ANY_SPEC = pl.BlockSpec(memory_space=pl.ANY)


def _rmsnorm_fwd(x, g, *, name, after=()):
    L, C = x.shape
    tm = _pick(L, 512)

    def body(x_ref, g_ref, *rest):
        rest[-1][...] = _rms_val(x_ref[...], g_ref[...]).astype(bf16)

    return pl.pallas_call(body, name=name, out_shape=S((L, C), bf16), grid=(L // tm,),
                          in_specs=[_rows(tm, C), _const((1, C))] + [ANY_SPEC] * len(after), out_specs=_rows(tm, C),
                          compiler_params=_cp("parallel"))(x, g, *after)


def _resid_norm_fwd(x, mix, g_post, g_next, *, name):
    L, C = x.shape
    tm = _pick(L, 512)
    want_h = g_next is not None

    def body(x_ref, m_ref, gp_ref, *rest):
        xn = x_ref[...] + _rms_val(m_ref[...], gp_ref[...])
        if want_h:
            gn_ref, xo_ref, h_ref = rest
            h_ref[...] = _rms_val(xn, gn_ref[...]).astype(bf16)
        else:
            (xo_ref,) = rest
        xo_ref[...] = xn

    in_specs = [_rows(tm, C), _rows(tm, C), _const((1, C))]
    args = [x, mix, g_post]
    out_shape, out_specs = [S((L, C), f32)], [_rows(tm, C)]
    if want_h:
        in_specs.append(_const((1, C)))
        args.append(g_next)
        out_shape.append(S((L, C), bf16))
        out_specs.append(_rows(tm, C))
    out = pl.pallas_call(body, name=name, out_shape=tuple(out_shape), grid=(L // tm,), in_specs=in_specs,
                         out_specs=tuple(out_specs), compiler_params=_cp("parallel"))(*args)
    return (out[0], out[1]) if want_h else (out[0], None)


def _norm_bwd(x, g, dy, *, name, add=None, out_dtype=f32, after=()):
    L, C = x.shape
    tm = _pick(L, 512)
    has_add = add is not None

    def body(x_ref, g_ref, dy_ref, *rest):
        rest = list(rest)
        add_ref = rest.pop(0) if has_add else None
        dx_ref, dg_ref, cs_ref = rest[-3:]
        i = pl.program_id(0)

        @pl.when(i == 0)
        def _():
            dg_ref[...] = jnp.zeros_like(dg_ref)
            cs_ref[...] = jnp.zeros_like(cs_ref)

        dx, dg = _rms_bwd_val(x_ref[...], g_ref[...], dy_ref[...].astype(f32))
        dg_ref[...] += dg
        cs_ref[...] += jnp.sum(dx, axis=0, keepdims=True)
        if has_add:
            dx = dx + add_ref[...]
        dx_ref[...] = dx.astype(out_dtype)

    in_specs = [_rows(tm, C), _const((1, C)), _rows(tm, C)]
    args = [x, g, dy]
    if has_add:
        in_specs.append(_rows(tm, C))
        args.append(add)
    in_specs += [ANY_SPEC] * len(after)
    args += list(after)
    return pl.pallas_call(body, name=name, out_shape=(S((L, C), out_dtype), S((1, C), f32), S((1, C), f32)),
                          grid=(L // tm,), in_specs=in_specs,
                          out_specs=(_rows(tm, C), _const((1, C)), _const((1, C))),
                          compiler_params=_cp("arbitrary"))(*args)


def _loss_fwd_bwd(y, target, *, name):
    L, C = y.shape
    tm = _pick(L, 512)

    def body(y_ref, t_ref, acc_ref, dy_ref):
        i = pl.program_id(0)

        @pl.when(i == 0)
        def _():
            acc_ref[...] = jnp.zeros_like(acc_ref)

        e = y_ref[...] - t_ref[...]
        rs = jnp.sum(e * e, axis=-1, keepdims=True)
        acc_ref[...] += jnp.broadcast_to(jnp.sum(rs, axis=0, keepdims=True), (1, LANE))
        dy_ref[...] = e * (1.0 / C)

    return pl.pallas_call(body, name=name, out_shape=(S((1, LANE), f32), S((L, C), f32)), grid=(L // tm,),
                          in_specs=[_rows(tm, C), _rows(tm, C)], out_specs=(_const((1, LANE)), _rows(tm, C)),
                          compiler_params=_cp("arbitrary"))(y, target)


def _halo_rows(K):
    return SUBLANE if K - 1 <= SUBLANE else 32


def _prev_halo_spec(tm, H, C):
    return pl.BlockSpec((H, C), lambda i: (jnp.maximum(i * (tm // H) - 1, 0), 0))


def _next_halo_spec(tm, H, C, L):
    return pl.BlockSpec((H, C), lambda i: (jnp.minimum((i + 1) * (tm // H), L // H - 1), 0))


def _shift_down(ext, s):
    return ext if s == 0 else pltpu.roll(ext, s, axis=0)


def _shift_up(ext, s):
    return ext if s == 0 else pltpu.roll(ext, ext.shape[0] - s, axis=0)


def _causal_conv(ext, H, w_ref, K):
    acc = None
    for k in range(K):
        term = _shift_down(ext, K - 1 - k)[H:, :] * w_ref[k:k + 1, :]
        acc = term if acc is None else acc + term
    return acc


def _anticausal_conv(ext, tm, w_ref, K):
    acc = None
    for k in range(K):
        term = _shift_up(ext, K - 1 - k)[:tm, :] * w_ref[k:k + 1, :]
        acc = term if acc is None else acc + term
    return acc


def _tap_grads(dw_ref, d_cur, x_ext, H, K):
    for k in range(K):
        dw_ref[k:k + 1, :] += jnp.sum(d_cur * _shift_down(x_ext, K - 1 - k)[H:, :], axis=0, keepdims=True)


def _pad_taps(w, K):
    return jnp.pad(w, ((0, _halo_rows(K) - K), (0, 0)))


def _conv_act_fwd(x, w, b, *, K, act, name, out_dtype, tm_pref=256):
    L, C = x.shape
    H = _halo_rows(K)
    tm = _pick(L, tm_pref)
    Co = C if act == "silu" else C // 2

    rs = min(ROW_SUB, tm)

    def body(h_ref, x_ref, w_ref, b_ref, o_ref, ext_scr):
        i = pl.program_id(0)
        ext_scr[0:H] = jnp.where(i > 0, h_ref[...], 0.0)
        ext_scr[H:] = x_ref[...]
        for j in range(Co // LANE):
            lanes = [j] if act == "silu" else [j, j + Co // LANE]
            wb = [(w_ref[:, c * LANE:(c + 1) * LANE], b_ref[:, c * LANE:(c + 1) * LANE]) for c in lanes]
            for r0 in range(0, tm, rs):
                us = [_causal_conv(ext_scr[r0:r0 + rs + H, c * LANE:(c + 1) * LANE], H, wc, K) + bc
                      for c, (wc, bc) in zip(lanes, wb)]
                y = _silu(us[0]) if act == "silu" else _silu(us[0]) * us[1]
                o_ref[r0:r0 + rs, j * LANE:(j + 1) * LANE] = y.astype(out_dtype)

    return pl.pallas_call(body, name=name, out_shape=S((L, Co), out_dtype), grid=(L // tm,),
                          in_specs=[_prev_halo_spec(tm, H, C), _rows(tm, C), _const((H, C)), _const((1, C))],
                          out_specs=_rows(tm, Co), scratch_shapes=[pltpu.VMEM((H + tm, C), f32)],
                          compiler_params=_cp("parallel"))(x, x, w, b)


def _conv_act_bwd(x, dparts, w, b, *, K, act, name, tm_pref=256):
    L, C = x.shape
    H = _halo_rows(K)
    tm = _pick(L, tm_pref)
    nb = L // tm
    Co = C if act == "silu" else C // 2
    nparts = len(dparts)

    rs = min(ROW_SUB, tm)

    def body(hp_ref, x_ref, hn_ref, w_ref, b_ref, *rest):
        d_refs, dn_refs = rest[:nparts], rest[nparts:2 * nparts]
        dx_ref, dw_ref, db_ref, ext_scr, d_scr = rest[2 * nparts:]
        i = pl.program_id(0)

        @pl.when(i == 0)
        def _():
            dw_ref[...] = jnp.zeros_like(dw_ref)
            db_ref[...] = jnp.zeros_like(db_ref)

        ext_scr[0:H] = jnp.where(i > 0, hp_ref[...], 0.0)
        ext_scr[H:H + tm] = x_ref[...]
        ext_scr[H + tm:] = jnp.where(i < nb - 1, hn_ref[...], 0.0)
        off = 0
        for r, rn in zip(d_refs, dn_refs):
            d_scr[0:tm, off:off + r.shape[1]] = r[...].astype(f32)
            d_scr[tm:, off:off + r.shape[1]] = jnp.where(i < nb - 1, rn[...].astype(f32), 0.0)
            off += r.shape[1]
        for j in range(Co // LANE):
            lanes = [j] if act == "silu" else [j, j + Co // LANE]
            wb = [(w_ref[:, c * LANE:(c + 1) * LANE], b_ref[:, c * LANE:(c + 1) * LANE]) for c in lanes]
            db_acc = [jnp.zeros((1, LANE), f32) for _ in lanes]
            dw_acc = [[jnp.zeros((1, LANE), f32) for _ in range(K)] for _ in lanes]
            for r0 in range(0, tm, rs):
                exts = [ext_scr[r0:r0 + rs + 2 * H, c * LANE:(c + 1) * LANE] for c in lanes]
                us = [_causal_conv(e, H, wc, K) + bc for e, (wc, bc) in zip(exts, wb)]
                d = d_scr[r0:r0 + rs + H, j * LANE:(j + 1) * LANE]
                dus = [d * _dsilu(us[0])] if act == "silu" else [d * us[1] * _dsilu(us[0]), d * _silu(us[0])]
                for q, (c, e, du, (wc, _)) in enumerate(zip(lanes, exts, dus, wb)):
                    dx_ref[r0:r0 + rs, c * LANE:(c + 1) * LANE] = _anticausal_conv(du, rs, wc, K).astype(bf16)
                    du_cur = du[:rs]
                    db_acc[q] = db_acc[q] + jnp.sum(du_cur, axis=0, keepdims=True)
                    for k in range(K):
                        dw_acc[q][k] = dw_acc[q][k] + jnp.sum(du_cur * _shift_down(e[:H + rs], K - 1 - k)[H:], axis=0,
                                                              keepdims=True)
            for q, c in enumerate(lanes):
                db_ref[:, c * LANE:(c + 1) * LANE] += db_acc[q]
                for k in range(K):
                    dw_ref[k:k + 1, c * LANE:(c + 1) * LANE] += dw_acc[q][k]

    in_specs = [_prev_halo_spec(tm, H, C), _rows(tm, C), _next_halo_spec(tm, H, C, L), _const((H, C)), _const((1, C))]
    in_specs += [_rows(tm, p.shape[1]) for p in dparts] + [_next_halo_spec(tm, H, p.shape[1], L) for p in dparts]
    return pl.pallas_call(body, name=name, out_shape=(S((L, C), bf16), S((H, C), f32), S((1, C), f32)), grid=(nb,),
                          in_specs=in_specs, out_specs=(_rows(tm, C), _const((H, C)), _const((1, C))),
                          scratch_shapes=[pltpu.VMEM((tm + 2 * H, C), f32), pltpu.VMEM((tm + H, Co), f32)],
                          compiler_params=_cp("arbitrary"))(x, x, x, w, b, *dparts, *dparts)


def _cf_fwd(u, dw_w, dw_b, ln_g, ln_b, *, name):
    L, C2 = u.shape
    C = C2 // 2
    K, H = CF_KERNEL, _halo_rows(CF_KERNEL)
    tm = _pick(L, 256)

    def body(h_ref, u_ref, w_ref, b_ref, g_ref, lb_ref, c_ref, s_ref):
        i = pl.program_id(0)
        halo = jnp.where(i > 0, h_ref[...], 0.0)
        ext = jnp.concatenate([halo, u_ref[...]], axis=0)
        glu = ext[:, :C] * _sigmoid(ext[:, C:])
        c = _causal_conv(glu, H, w_ref, K) + b_ref[...]
        c_ref[...] = c
        mu = jnp.mean(c, axis=-1, keepdims=True)
        xc = c - mu
        var = jnp.mean(xc * xc, axis=-1, keepdims=True)
        ln = xc * lax.rsqrt(var + EPS) * g_ref[...] + lb_ref[...]
        s_ref[...] = _silu(ln).astype(bf16)

    return pl.pallas_call(body, name=name, out_shape=(S((L, C), f32), S((L, C), bf16)), grid=(L // tm,),
                          in_specs=[_prev_halo_spec(tm, H, C2), _rows(tm, C2), _const((H, C)), _const((1, C)),
                                    _const((1, C)), _const((1, C))],
                          out_specs=(_rows(tm, C), _rows(tm, C)), compiler_params=_cp("parallel"))(u, u, dw_w, dw_b, ln_g, ln_b)


def _cf_ln_bwd(c, ln_g, ln_b, ds, *, name):
    L, C = c.shape
    tm = _pick(L, 512)

    def body(c_ref, g_ref, lb_ref, ds_ref, dc_ref, dg_ref, db_ref):
        i = pl.program_id(0)

        @pl.when(i == 0)
        def _():
            dg_ref[...] = jnp.zeros_like(dg_ref)
            db_ref[...] = jnp.zeros_like(db_ref)

        c = c_ref[...]
        mu = jnp.mean(c, axis=-1, keepdims=True)
        xc = c - mu
        r = lax.rsqrt(jnp.mean(xc * xc, axis=-1, keepdims=True) + EPS)
        xh = xc * r
        ln = xh * g_ref[...] + lb_ref[...]
        dln = ds_ref[...].astype(f32) * _dsilu(ln)
        dg_ref[...] += jnp.sum(dln * xh, axis=0, keepdims=True)
        db_ref[...] += jnp.sum(dln, axis=0, keepdims=True)
        dxh = dln * g_ref[...]
        dc_ref[...] = r * (dxh - jnp.mean(dxh, axis=-1, keepdims=True) - xh * jnp.mean(dxh * xh, axis=-1, keepdims=True))

    return pl.pallas_call(body, name=name, out_shape=(S((L, C), f32), S((1, C), f32), S((1, C), f32)), grid=(L // tm,),
                          in_specs=[_rows(tm, C), _const((1, C)), _const((1, C)), _rows(tm, C)],
                          out_specs=(_rows(tm, C), _const((1, C)), _const((1, C))),
                          compiler_params=_cp("arbitrary"))(c, ln_g, ln_b, ds)


def _cf_glu_bwd(u, dc, dw_w, *, name):
    L, C2 = u.shape
    C = C2 // 2
    K, H = CF_KERNEL, _halo_rows(CF_KERNEL)
    tm = _pick(L, 256)
    nb = L // tm

    def body(uh_ref, u_ref, dc_ref, dch_ref, w_ref, du_ref, dw_ref, db_ref, dus_ref):
        i = pl.program_id(0)

        @pl.when(i == 0)
        def _():
            dw_ref[...] = jnp.zeros_like(dw_ref)
            db_ref[...] = jnp.zeros_like(db_ref)
            dus_ref[...] = jnp.zeros_like(dus_ref)

        halo = jnp.where(i > 0, uh_ref[...], 0.0)
        ext = jnp.concatenate([halo, u_ref[...]], axis=0)
        sg = _sigmoid(ext[:, C:])
        glu = ext[:, :C] * sg
        dc = dc_ref[...]
        dnext = jnp.where(i < nb - 1, dch_ref[...], 0.0)
        dglu = _anticausal_conv(jnp.concatenate([dc, dnext], axis=0), tm, w_ref, K)
        a_cur, sg_cur = ext[H:, :C], sg[H:, :]
        du = jnp.concatenate([dglu * sg_cur, dglu * a_cur * sg_cur * (1.0 - sg_cur)], axis=1)
        du_ref[...] = du.astype(bf16)
        dus_ref[...] += jnp.sum(du, axis=0, keepdims=True)
        db_ref[...] += jnp.sum(dc, axis=0, keepdims=True)
        _tap_grads(dw_ref, dc, glu, H, K)

    return pl.pallas_call(body, name=name,
                          out_shape=(S((L, C2), bf16), S((H, C), f32), S((1, C), f32), S((1, C2), f32)), grid=(nb,),
                          in_specs=[_prev_halo_spec(tm, H, C2), _rows(tm, C2), _rows(tm, C), _next_halo_spec(tm, H, C, L),
                                    _const((H, C))],
                          out_specs=(_rows(tm, C2), _const((H, C)), _const((1, C)), _const((1, C2))),
                          compiler_params=_cp("arbitrary"))(u, u, dc, dc, dw_w)


def _gated_norm_fwd(y, z, g, *, name):
    L, C = y.shape
    tm = _pick(L, 256)

    def body(y_ref, z_ref, g_ref, o_ref):
        o_ref[...] = _rms_val(y_ref[...] * _silu(z_ref[...]), g_ref[...]).astype(bf16)

    return pl.pallas_call(body, name=name, out_shape=S((L, C), bf16), grid=(L // tm,),
                          in_specs=[_rows(tm, C), _rows(tm, C), _const((1, C))], out_specs=_rows(tm, C),
                          compiler_params=_cp("parallel"))(y, z, g)


def _gated_norm_bwd(y, z, g, dyn, *, name):
    L, C = y.shape
    tm = _pick(L, 256)

    def body(y_ref, z_ref, g_ref, d_ref, dy_ref, dz_ref, dg_ref):
        i = pl.program_id(0)

        @pl.when(i == 0)
        def _():
            dg_ref[...] = jnp.zeros_like(dg_ref)

        y, z = y_ref[...], z_ref[...]
        sz = _silu(z)
        du, dg = _rms_bwd_val(y * sz, g_ref[...], d_ref[...].astype(f32))
        dg_ref[...] += dg
        dy_ref[...] = du * sz
        dz_ref[...] = (du * y * _dsilu(z)).astype(bf16)

    return pl.pallas_call(body, name=name, out_shape=(S((L, C), f32), S((L, C), bf16), S((1, C), f32)), grid=(L // tm,),
                          in_specs=[_rows(tm, C), _rows(tm, C), _const((1, C)), _rows(tm, C)],
                          out_specs=(_rows(tm, C), _rows(tm, C), _const((1, C))),
                          compiler_params=_cp("arbitrary"))(y, z, g, dyn)


_XA_SCALE = XA_HEAD_DIM ** -0.5


def _attn_fwd(q, kv, *, name):
    L, C = q.shape
    tm = _pick(L, 512)
    Dh = XA_HEAD_DIM

    def body(q_ref, kv_ref, o_ref):
        for h in range(XA_HEADS):
            qh = q_ref[:, h * Dh:(h + 1) * Dh]
            kh = kv_ref[:, h * Dh:(h + 1) * Dh]
            vh = kv_ref[:, C + h * Dh:C + (h + 1) * Dh]
            s = lax.dot_general(qh, kh, _DN["nt"], preferred_element_type=f32) * _XA_SCALE
            e = jnp.exp(s - jnp.max(s, axis=-1, keepdims=True))
            p = e / jnp.sum(e, axis=-1, keepdims=True)
            o_ref[:, h * Dh:(h + 1) * Dh] = jnp.dot(p.astype(bf16), vh, preferred_element_type=f32).astype(bf16)

    return pl.pallas_call(body, name=name, out_shape=S((L, C), bf16), grid=(L // tm,),
                          in_specs=[_rows(tm, C), _const((N_MEM, 2 * C))], out_specs=_rows(tm, C),
                          compiler_params=_cp("parallel"))(q, kv)


def _attn_bwd(q, kv, do, *, name):
    L, C = q.shape
    tm = _pick(L, 512)
    Dh = XA_HEAD_DIM

    def body(q_ref, kv_ref, do_ref, dq_ref, dkv_ref):
        i = pl.program_id(0)

        @pl.when(i == 0)
        def _():
            dkv_ref[...] = jnp.zeros_like(dkv_ref)

        for h in range(XA_HEADS):
            qh = q_ref[:, h * Dh:(h + 1) * Dh]
            kh = kv_ref[:, h * Dh:(h + 1) * Dh]
            vh = kv_ref[:, C + h * Dh:C + (h + 1) * Dh]
            doh = do_ref[:, h * Dh:(h + 1) * Dh]
            s = lax.dot_general(qh, kh, _DN["nt"], preferred_element_type=f32) * _XA_SCALE
            e = jnp.exp(s - jnp.max(s, axis=-1, keepdims=True))
            p = e / jnp.sum(e, axis=-1, keepdims=True)
            pb = p.astype(bf16)
            dkv_ref[:, C + h * Dh:C + (h + 1) * Dh] += lax.dot_general(pb, doh, _DN["tn"], preferred_element_type=f32)
            dp = lax.dot_general(doh, vh, _DN["nt"], preferred_element_type=f32)
            ds = (p * (dp - jnp.sum(dp * p, axis=-1, keepdims=True)) * _XA_SCALE).astype(bf16)
            dq_ref[:, h * Dh:(h + 1) * Dh] = jnp.dot(ds, kh, preferred_element_type=f32).astype(bf16)
            dkv_ref[:, h * Dh:(h + 1) * Dh] += lax.dot_general(ds, qh, _DN["tn"], preferred_element_type=f32)

    return pl.pallas_call(body, name=name, out_shape=(S((L, C), bf16), S((N_MEM, 2 * C), f32)), grid=(L // tm,),
                          in_specs=[_rows(tm, C), _const((N_MEM, 2 * C)), _rows(tm, C)],
                          out_specs=(_rows(tm, C), _const((N_MEM, 2 * C))),
                          compiler_params=_cp("arbitrary"))(q, kv, do)


Q = CHUNK
PAIRS = HEADS_PER_GROUP // 2
GW = HEADS_PER_GROUP * HEAD_DIM


def _split(x, pieces):
    out = []
    for _ in range(pieces - 1):
        p = x.astype(bf16)
        out.append(p)
        x = x - p.astype(f32)
    return out + [x.astype(bf16)]


def _sel_right(x, sel, mode="nn", pieces=2):
    return sum(lax.dot_general(p, sel, _DN[mode], preferred_element_type=f32) for p in _split(x, pieces))


def _sel_left(sel, x, pieces=3):
    return sum(lax.dot_general(sel, p, _DN["nn"], preferred_element_type=f32) for p in _split(x, pieces))


def _ssd_common(dt_ref, hp_ref):
    dt_pre = dt_ref[...] + hp_ref[0:1, :]
    dt = _softplus(dt_pre)
    A = -jnp.exp(hp_ref[1:2, :])
    a = dt * A
    row = lax.broadcasted_iota(jnp.int32, (Q, Q), 0)
    col = lax.broadcasted_iota(jnp.int32, (Q, Q), 1)
    tri = row >= col
    cs = _sel_left(tri.astype(bf16), a)
    T = cs[Q - 1:Q, :]
    return dict(dt_pre=dt_pre, dt=dt, A=A, cs=cs, csT=cs.T, T=T, ecs=jnp.exp(cs), eend=jnp.exp(T - cs), eT=jnp.exp(T),
                tri=tri, row=row, col=col)


def _pair_expand(v, jj, lo):
    return jnp.where(lo, v[:, 2 * jj:2 * jj + 1], v[:, 2 * jj + 1:2 * jj + 2])


def _decay(cm, h):
    seg = cm["cs"][:, h:h + 1] - cm["csT"][h:h + 1, :]
    return jnp.where(cm["tri"], jnp.exp(jnp.where(cm["tri"], seg, 0.0)), 0.0)


def _ssd_fwd(act, dtp, hp, *, name):
    L = act.shape[0]
    nc = L // Q

    def body(xs_ref, b_ref, c_ref, dt_ref, hp_ref, y_ref, hs_ref, h_scr):
        c = pl.program_id(1)

        @pl.when(c == 0)
        def _():
            h_scr[...] = jnp.zeros_like(h_scr)

        cm = _ssd_common(dt_ref, hp_ref)
        Bb, Cb = b_ref[...].astype(bf16), c_ref[...].astype(bf16)
        CB = lax.dot_general(Cb, Bb, _DN["nt"], preferred_element_type=f32)
        lo = lax.broadcasted_iota(jnp.int32, (Q, LANE), 1) < HEAD_DIM
        top = lax.broadcasted_iota(jnp.int32, (LANE, LANE), 0) < HEAD_DIM
        Drow = hp_ref[2:3, :]
        for jj in range(PAIRS):
            hA, hB = 2 * jj, 2 * jj + 1
            dtx, ecsx, eendx = (_pair_expand(cm[k], jj, lo) for k in ("dt", "ecs", "eend"))
            xs_p = xs_ref[:, jj * LANE:(jj + 1) * LANE]
            Xd = xs_p * dtx
            Y = None
            for h, Xm in ((hA, jnp.where(lo, Xd, 0.0)), (hB, jnp.where(lo, 0.0, Xd))):
                W = (CB * _decay(cm, h)).astype(bf16)
                t = jnp.dot(W, Xm.astype(bf16), preferred_element_type=f32)
                Y = t if Y is None else Y + t
            Hp = h_scr[jj]
            hs_ref[0, jj] = Hp
            Yoff = lax.dot_general(Cb, Hp.astype(bf16), _DN["nt"], preferred_element_type=f32) * ecsx
            Dx = jnp.where(lo[0:1, :], Drow[:, hA:hA + 1], Drow[:, hB:hB + 1])
            y_ref[:, jj * LANE:(jj + 1) * LANE] = Y + Yoff + xs_p * Dx
            Snew = lax.dot_general((Xd * eendx).astype(bf16), Bb, _DN["tn"], preferred_element_type=f32)
            eTx = jnp.where(top, cm["eT"][:, hA:hA + 1], cm["eT"][:, hB:hB + 1])
            h_scr[jj] = Hp * eTx + Snew

    return pl.pallas_call(
        body, name=name, out_shape=(S((L, D_INNER), f32), S((nc, N_SSM_HEADS // 2, LANE, D_STATE), f32)),
        grid=(N_GROUPS, nc),
        in_specs=[pl.BlockSpec((Q, GW), lambda g, c: (c, g)),
                  pl.BlockSpec((Q, D_STATE), lambda g, c: (c, D_INNER // D_STATE + g)),
                  pl.BlockSpec((Q, D_STATE), lambda g, c: (c, D_INNER // D_STATE + N_GROUPS + g)),
                  pl.BlockSpec((Q, LANE), lambda g, c: (c, g)),
                  pl.BlockSpec((SUBLANE, LANE), lambda g, c: (0, g))],
        out_specs=(pl.BlockSpec((Q, GW), lambda g, c: (c, g)),
                   pl.BlockSpec((1, PAIRS, LANE, D_STATE), lambda g, c: (c, g, 0, 0))),
        scratch_shapes=[pltpu.VMEM((PAIRS, LANE, D_STATE), f32)],
        compiler_params=_cp("arbitrary", "arbitrary"))(act, act, act, dtp, hp)


def _ssd_bwd(act, dtp, hp, dy, hs, *, name):
    L = act.shape[0]
    nc = L // Q

    def body(xs_ref, b_ref, c_ref, dt_ref, hp_ref, dy_ref, hs_ref, dxs_ref, db_ref, dc_ref, ddt_ref, dhp_ref, dh_scr):
        c = pl.program_id(1)

        @pl.when(c == 0)
        def _():
            dh_scr[...] = jnp.zeros_like(dh_scr)
            dhp_ref[...] = jnp.zeros_like(dhp_ref)

        cm = _ssd_common(dt_ref, hp_ref)
        Bb, Cb = b_ref[...].astype(bf16), c_ref[...].astype(bf16)
        CB = lax.dot_general(Cb, Bb, _DN["nt"], preferred_element_type=f32)
        lane = lax.broadcasted_iota(jnp.int32, (Q, LANE), 1)
        sub = lax.broadcasted_iota(jnp.int32, (LANE, LANE), 0)
        lo = lane < HEAD_DIM
        top = sub < HEAD_DIM
        Drow = hp_ref[2:3, :]
        zero = jnp.zeros((Q, LANE), f32)
        dcs, ddtx, dC, dB, dCB = zero, zero, zero, zero, jnp.zeros((Q, Q), f32)
        dD_row = jnp.zeros((1, LANE), f32)
        dT_row = jnp.zeros((1, LANE), f32)
        for jj in range(PAIRS):
            hA, hB = 2 * jj, 2 * jj + 1
            Pj = (lane == jnp.where(top, hA, hB)).astype(bf16)
            dtx, ecsx, eendx = (_pair_expand(cm[k], jj, lo) for k in ("dt", "ecs", "eend"))
            xs_p = xs_ref[:, jj * LANE:(jj + 1) * LANE]
            dY_p = dy_ref[:, jj * LANE:(jj + 1) * LANE]
            Xd = xs_p * dtx
            Xdb = Xd.astype(bf16)
            Hp, dHn = hs_ref[0, jj], dh_scr[jj]
            Hb, dHb = Hp.astype(bf16), dHn.astype(bf16)
            EdYb = (dY_p * ecsx).astype(bf16)
            YoffN = lax.dot_general(Cb, Hb, _DN["nt"], preferred_element_type=f32)
            dC = dC + jnp.dot(EdYb, Hb, preferred_element_type=f32)
            dH_off = lax.dot_general(EdYb, Cb, _DN["tn"], preferred_element_type=f32)
            R = lax.dot_general(Bb, dHb, _DN["nt"], preferred_element_type=f32)
            Xe = Xd * eendx
            dB = dB + jnp.dot(Xe.astype(bf16), dHb, preferred_element_type=f32)
            dXd = R * eendx
            V2 = _sel_right(R * Xe, Pj)
            dcs = dcs + _sel_right(dY_p * YoffN * ecsx, Pj) - V2
            HH = dHn * Hp
            hh = [jnp.sum(jnp.sum(HH[r0:r0 + HEAD_DIM], axis=0, keepdims=True), axis=1, keepdims=True) for r0 in (0, HEAD_DIM)]
            dT_row = dT_row + jnp.sum(V2, axis=0, keepdims=True) \
                + (jnp.where(lane[0:1] == hA, hh[0], 0.0) + jnp.where(lane[0:1] == hB, hh[1], 0.0)) * cm["eT"]
            for h, keep in ((hA, lo), (hB, jnp.logical_not(lo))):
                M = _decay(cm, h)
                Wf = CB * M
                dYm = jnp.where(keep, dY_p, 0.0).astype(bf16)
                dW = lax.dot_general(dYm, Xdb, _DN["nt"], preferred_element_type=f32)
                dXd = dXd + lax.dot_general(Wf.astype(bf16), dYm, _DN["tn"], preferred_element_type=f32)
                Z = dW * Wf
                onesh = (lane == h).astype(bf16)
                dcs = dcs + _sel_right(Z, onesh) - _sel_right(Z, onesh, "tn")
                dCB = dCB + dW * M
            Dx = jnp.where(lo[0:1, :], Drow[:, hA:hA + 1], Drow[:, hB:hB + 1])
            dxs_ref[:, jj * LANE:(jj + 1) * LANE] = dXd * dtx + dY_p * Dx
            ddtx = ddtx + _sel_right(dXd * xs_p, Pj)
            dD_cols = jnp.broadcast_to(jnp.sum(dY_p * xs_p, axis=0, keepdims=True), (SUBLANE, LANE))
            dD_row = dD_row + _sel_right(dD_cols, Pj, pieces=3)[0:1]
            eTx = jnp.where(top, cm["eT"][:, hA:hA + 1], cm["eT"][:, hB:hB + 1])
            dh_scr[jj] = dHn * eTx + dH_off
        dCBb = dCB.astype(bf16)
        dc_ref[...] = dC + jnp.dot(dCBb, Bb, preferred_element_type=f32)
        db_ref[...] = dB + lax.dot_general(dCBb, Cb, _DN["tn"], preferred_element_type=f32)
        dcs = dcs + jnp.where(lax.broadcasted_iota(jnp.int32, (Q, LANE), 0) == Q - 1, dT_row, 0.0)
        da = _sel_left((cm["row"] <= cm["col"]).astype(bf16), dcs)
        ddt_pre = (da * cm["A"] + ddtx) * _sigmoid(cm["dt_pre"])
        ddt_ref[...] = ddt_pre
        r8 = lax.broadcasted_iota(jnp.int32, (SUBLANE, LANE), 0)
        dhp_ref[...] += jnp.where(r8 == 0, jnp.sum(ddt_pre, axis=0, keepdims=True),
                                  jnp.where(r8 == 1, jnp.sum(da * cm["dt"], axis=0, keepdims=True) * cm["A"],
                                            jnp.where(r8 == 2, dD_row, 0.0)))

    rev = lambda c: nc - 1 - c
    return pl.pallas_call(
        body, name=name,
        out_shape=(S((L, D_INNER), f32), S((L, N_GROUPS * D_STATE), f32), S((L, N_GROUPS * D_STATE), f32),
                   S((L, N_GROUPS * LANE), f32), S((SUBLANE, N_GROUPS * LANE), f32)),
        grid=(N_GROUPS, nc),
        in_specs=[pl.BlockSpec((Q, GW), lambda g, c: (rev(c), g)),
                  pl.BlockSpec((Q, D_STATE), lambda g, c: (rev(c), D_INNER // D_STATE + g)),
                  pl.BlockSpec((Q, D_STATE), lambda g, c: (rev(c), D_INNER // D_STATE + N_GROUPS + g)),
                  pl.BlockSpec((Q, LANE), lambda g, c: (rev(c), g)),
                  pl.BlockSpec((SUBLANE, LANE), lambda g, c: (0, g)),
                  pl.BlockSpec((Q, GW), lambda g, c: (rev(c), g)),
                  pl.BlockSpec((1, PAIRS, LANE, D_STATE), lambda g, c: (rev(c), g, 0, 0))],
        out_specs=(pl.BlockSpec((Q, GW), lambda g, c: (rev(c), g)),
                   pl.BlockSpec((Q, D_STATE), lambda g, c: (rev(c), g)),
                   pl.BlockSpec((Q, D_STATE), lambda g, c: (rev(c), g)),
                   pl.BlockSpec((Q, LANE), lambda g, c: (rev(c), g)),
                   pl.BlockSpec((SUBLANE, LANE), lambda g, c: (0, g))),
        scratch_shapes=[pltpu.VMEM((PAIRS, LANE, D_STATE), f32)],
        compiler_params=_cp("arbitrary", "arbitrary"))(act, act, act, dtp, hp, dy, hs)


def _group_pad_cols(w):
    lead = w.shape[:-1]
    w = w.reshape(lead + (N_GROUPS, HEADS_PER_GROUP))
    w = jnp.pad(w, [(0, 0)] * len(lead) + [(0, 0), (0, LANE - HEADS_PER_GROUP)])
    return w.reshape(lead + (N_GROUPS * LANE,))


def _group_unpad_cols(w):
    lead = w.shape[:-1]
    return w.reshape(lead + (N_GROUPS, LANE))[..., :HEADS_PER_GROUP].reshape(lead + (N_SSM_HEADS,))


def _row(v):
    return v.reshape(1, -1)


ROW_SHARDED = ('ssm_out_w', 'cf_pw2_w', 'xa_q_w', 'xa_o_w', 'ffn_out_w')
COL_SHARDED = ('cf_pw1_w', 'xa_kv_w', 'ffn_in_w')


MIXER_WEIGHTS = ('ssm_in_w', 'ssm_out_w', 'cf_pw1_w', 'cf_pw2_w')


def _layer_matmul_weights(i, part):
    if part == "mixer":
        return ('ssm_in_w', 'ssm_out_w') if i % 2 == 0 else ('cf_pw1_w', 'cf_pw2_w')
    return ('xa_q_w', 'xa_kv_w', 'xa_o_w', 'ffn_in_w', 'ffn_out_w')


def _device_step(x, mem, target, W, layer_weights, layer_grads, start_after=()):
    ng = W['norm_g']
    lw = []
    for i in range(DEPTH):
        j = i // 2
        p = {}
        if i % 2 == 0:
            p['cw'] = _pad_taps(W['ssm_conv_w'][j], SSM_CONV)
            p['cb'] = _row(W['ssm_conv_b'][j])
            hp = jnp.stack([_group_pad_cols(W['ssm_dt_bias'][j]), _group_pad_cols(W['ssm_A_log'][j]),
                            _group_pad_cols(W['ssm_D'][j])])
            p['hp'] = jnp.pad(hp, ((0, SUBLANE - 3), (0, 0)))
            p['sng'] = _row(W['ssm_norm_g'][j])
        else:
            p['pw1b'] = _row(W['cf_pw1_b'][j])
            p['dww'], p['dwb'] = _pad_taps(W['cf_dw_w'][j], CF_KERNEL), _row(W['cf_dw_b'][j])
            p['lng'], p['lnb'] = _row(W['cf_ln_g'][j]), _row(W['cf_ln_b'][j])
            p['pw2b'] = _row(W['cf_pw2_b'][j])
        p['memg'] = _row(W['xa_mem_g'][i])
        p['fcw'], p['fcb'] = _pad_taps(W['ffn_conv_w'][i], FFN_CONV), _row(W['ffn_conv_b'][i])
        p['g'] = [_row(ng[i, s]) for s in range(6)]
        lw.append(p)

    def wmm(a, wl, wname, mode, **kw):
        return _mm(a, wl[wname], mode, b_shards=wname in COL_SHARDED, **kw)

    saved = []
    X = x
    h = _rmsnorm_fwd(X, lw[0]['g'][0], name="norm_in", after=start_after)
    for i in range(DEPTH):
        p, sv = lw[i], {}
        wl = dict(layer_weights(i, "mixer", X))
        sv['X0'], sv['h'], sv['wl'] = X, h, wl
        if i % 2 == 0:
            win = jnp.concatenate([wl['ssm_in_w'][s] for s in range(N_CHIPS)], axis=1)
            wl['wz'], wl['wx'] = win[:, :D_INNER], win[:, D_INNER:D_INNER + CONV_DIM]
            wl['wdt'] = _group_pad_cols(win[:, D_INNER + CONV_DIM:])
            z = _mm(h, wl['wz'], "nn", name="ssm_z")
            xbc = _mm(h, wl['wx'], "nn", name="ssm_xbc")
            dtp = _mm(h, wl['wdt'], "nn", name="ssm_dt")
            act = _conv_act_fwd(xbc, p['cw'], p['cb'], K=SSM_CONV, act="silu", name="ssm_conv_fwd", out_dtype=f32)
            y, hs = _ssd_fwd(act, dtp, p['hp'], name="ssd_fwd")
            yn = _gated_norm_fwd(y, z, p['sng'], name="ssm_gnorm_fwd")
            mix = wmm(yn, wl, 'ssm_out_w', "nn", name="ssm_out")
            sv.update(z=z, xbc=xbc, dtp=dtp, act=act, y=y, hs=hs, yn=yn)
        else:
            u = wmm(h, wl, 'cf_pw1_w', "nn", name="cf_pw1", bias=p['pw1b'])
            c, s = _cf_fwd(u, p['dww'], p['dwb'], p['lng'], p['lnb'], name="cf_conv_fwd")
            mix = wmm(s, wl, 'cf_pw2_w', "nn", name="cf_pw2", bias=p['pw2b'])
            sv.update(u=u, c=c, s=s)
        wl.update(layer_weights(i, "rest", mix))
        X1, h2 = _resid_norm_fwd(X, mix, p['g'][1], p['g'][2], name="resid_norm_a")
        q = wmm(h2, wl, 'xa_q_w', "nn", name="xa_q", out_dtype=bf16)
        m = _rmsnorm_fwd(mem, p['memg'], name="xa_mem_norm")
        kv = wmm(m, wl, 'xa_kv_w', "nn", name="xa_kv", out_dtype=bf16)
        o = _attn_fwd(q, kv, name="xa_attn_fwd")
        a = wmm(o, wl, 'xa_o_w', "nn", name="xa_o")
        X2, h3 = _resid_norm_fwd(X1, a, p['g'][3], p['g'][4], name="resid_norm_b")
        u0 = wmm(h3, wl, 'ffn_in_w', "nn", name="ffn_in")
        fact = _conv_act_fwd(u0, p['fcw'], p['fcb'], K=FFN_CONV, act="swiglu", name="ffn_conv_fwd", out_dtype=bf16)
        f = wmm(fact, wl, 'ffn_out_w', "nn", name="ffn_out")
        g_next = lw[i + 1]['g'][0] if i + 1 < DEPTH else None
        X3, hn = _resid_norm_fwd(X2, f, p['g'][5], g_next, name="resid_norm_c" if g_next is not None else "resid_norm_last")
        sv.update(mix=mix, X1=X1, h2=h2, q=q, m=m, kv=kv, o=o, a=a, X2=X2, h3=h3, u0=u0, fact=fact, f=f)
        saved.append(sv)
        X, h = X3, hn

    sse, G = _loss_fwd_bwd(X, target, name="loss")

    small = [n for n in WEIGHT_NAMES if n not in MATMUL_WEIGHTS]
    gr = {n: [None] * W[n].shape[0] for n in small}

    def dwmm(gl, a, d, wname, *, name):
        if wname in COL_SHARDED:
            gl[wname] = _mm(a, d, "tn", name=name, out_dtype=bf16, out_shards=True)
        else:
            g = _mm(a, d, "tn", name=name, out_dtype=bf16)
            gl[wname] = g.reshape(N_CHIPS, g.shape[0] // N_CHIPS, g.shape[1])

    dng = [[None] * 6 for _ in range(DEPTH)]
    for i in reversed(range(DEPTH)):
        p, sv, j = lw[i], saved[i], i // 2
        wl, gl = sv['wl'], {}
        df, dng[i][5], _ = _norm_bwd(sv['f'], p['g'][5], G, name="nb_f", out_dtype=bf16)
        dwmm(gl, sv['fact'], df, 'ffn_out_w', name="ffn_out_dw")
        dfact = wmm(df, wl, 'ffn_out_w', "nt", name="ffn_out_dx")
        du0, dcw, dcb = _conv_act_bwd(sv['u0'], [dfact], p['fcw'], p['fcb'], K=FFN_CONV, act="swiglu", name="ffn_conv_bwd")
        gr['ffn_conv_w'][i], gr['ffn_conv_b'][i] = dcw[:FFN_CONV], dcb[0]
        dwmm(gl, sv['h3'], du0, 'ffn_in_w', name="ffn_in_dw")
        dh3 = wmm(du0, wl, 'ffn_in_w', "nt", name="ffn_in_dx")
        G, dng[i][4], _ = _norm_bwd(sv['X2'], p['g'][4], dh3, name="nb_x2", add=G)
        da, dng[i][3], _ = _norm_bwd(sv['a'], p['g'][3], G, name="nb_a", out_dtype=bf16)
        dwmm(gl, sv['o'], da, 'xa_o_w', name="xa_o_dw")
        do = wmm(da, wl, 'xa_o_w', "nt", name="xa_o_dx", out_dtype=bf16)
        dq, dkv = _attn_bwd(sv['q'], sv['kv'], do, name="xa_attn_bwd")
        dwmm(gl, sv['h2'], dq, 'xa_q_w', name="xa_q_dw")
        dh2 = wmm(dq, wl, 'xa_q_w', "nt", name="xa_q_dx")
        dwmm(gl, sv['m'], dkv, 'xa_kv_w', name="xa_kv_dw")
        dm = wmm(dkv, wl, 'xa_kv_w', "nt", name="xa_kv_dx")
        _, dmg, _ = _norm_bwd(mem, p['memg'], dm, name="nb_mem")
        gr['xa_mem_g'][i] = dmg[0]
        G, dng[i][2], _ = _norm_bwd(sv['X1'], p['g'][2], dh2, name="nb_x1", add=G)
        behind = tuple(layer_grads(i, "rest", gl))
        dmix, dng[i][1], dmix_sum = _norm_bwd(sv['mix'], p['g'][1], G, name="nb_mix", out_dtype=bf16, after=behind)
        if i % 2 == 0:
            dwmm(gl, sv['yn'], dmix, 'ssm_out_w', name="ssm_out_dw")
            dyn = wmm(dmix, wl, 'ssm_out_w', "nt", name="ssm_out_dx")
            dy, dz, dsng = _gated_norm_bwd(sv['y'], sv['z'], p['sng'], dyn, name="ssm_gnorm_bwd")
            gr['ssm_norm_g'][j] = dsng[0]
            dxs, dB, dC, ddtp, dhp = _ssd_bwd(sv['act'], sv['dtp'], p['hp'], dy, sv['hs'], name="ssd_bwd")
            gr['ssm_dt_bias'][j], gr['ssm_A_log'][j], gr['ssm_D'][j] = (_group_unpad_cols(dhp[r]) for r in range(3))
            dxbc, dcw, dcb = _conv_act_bwd(sv['xbc'], [dxs, dB, dC], p['cw'], p['cb'], K=SSM_CONV, act="silu",
                                           name="ssm_conv_bwd")
            gr['ssm_conv_w'][j], gr['ssm_conv_b'][j] = dcw[:SSM_CONV], dcb[0]
            hh = sv['h']
            dwz = _mm(hh, dz, "tn", name="ssm_z_dw", out_dtype=bf16)
            dwx = _mm(hh, dxbc, "tn", name="ssm_xbc_dw", out_dtype=bf16)
            dwdt = _mm(hh, ddtp, "tn", name="ssm_dt_dw", out_dtype=bf16)
            din = jnp.concatenate([dwz, dwx, _group_unpad_cols(dwdt)], axis=1)
            gl['ssm_in_w'] = jnp.stack(jnp.split(din, N_CHIPS, axis=1))
            dh = _mm(dz, wl['wz'], "nt", name="ssm_z_dx")
            dh = _mm(dxbc, wl['wx'], "nt", name="ssm_xbc_dx", add=dh)
            dh = _mm(ddtp, wl['wdt'], "nt", name="ssm_dt_dx", add=dh)
        else:
            dwmm(gl, sv['s'], dmix, 'cf_pw2_w', name="cf_pw2_dw")
            gr['cf_pw2_b'][j] = dmix_sum[0]
            ds = wmm(dmix, wl, 'cf_pw2_w', "nt", name="cf_pw2_dx")
            dc, dlg, dlb = _cf_ln_bwd(sv['c'], p['lng'], p['lnb'], ds, name="cf_ln_bwd")
            gr['cf_ln_g'][j], gr['cf_ln_b'][j] = dlg[0], dlb[0]
            du, ddw, ddb, dus = _cf_glu_bwd(sv['u'], dc, p['dww'], name="cf_glu_bwd")
            gr['cf_dw_w'][j], gr['cf_dw_b'][j], gr['cf_pw1_b'][j] = ddw[:CF_KERNEL], ddb[0], dus[0]
            dwmm(gl, sv['h'], du, 'cf_pw1_w', name="cf_pw1_dw")
            dh = wmm(du, wl, 'cf_pw1_w', "nt", name="cf_pw1_dx")
        behind = tuple(layer_grads(i, "mixer", gl))
        G, dng[i][0], _ = _norm_bwd(sv['X0'], p['g'][0], dh, name="nb_x0", add=G, after=behind)
    gr['norm_g'] = [jnp.concatenate(dng[i], axis=0) for i in range(DEPTH)]
    gsmall = {n: jnp.stack(gr[n]) for n in small}
    return sse, G, gsmall


MESH = pl.DeviceIdType.MESH
HBM_SPEC = pl.BlockSpec(memory_space=pltpu.HBM)


def _chip_peers(x, y):
    return [(1 - x, y), (x, 1 - y), (1 - x, 1 - y)]


def _all_gather_chips(buf, *, name):
    R, C = buf.shape

    def body(in_ref, out_ref, send_sems, recv_sems, local_sem):
        x, y, c = lax.axis_index("x"), lax.axis_index("y"), lax.axis_index("c")
        me = 2 * x + y
        mine = pltpu.make_async_copy(in_ref, out_ref.at[me], local_sem)
        mine.start()
        peers = _chip_peers(x, y)
        sends = []
        for k, (px, py) in enumerate(peers):
            cp = pltpu.make_async_remote_copy(src_ref=in_ref, dst_ref=out_ref.at[me], send_sem=send_sems.at[k],
                                              recv_sem=recv_sems.at[k], device_id=(px, py, c), device_id_type=MESH)
            cp.start()
            sends.append(cp)
        for k, (px, py) in enumerate(peers):
            pltpu.make_async_remote_copy(src_ref=in_ref, dst_ref=out_ref.at[2 * px + py], send_sem=send_sems.at[k],
                                         recv_sem=recv_sems.at[k], device_id=(px, py, c), device_id_type=MESH).wait_recv()
        for cp in sends:
            cp.wait_send()
        mine.wait()

    return pl.pallas_call(body, name=name, out_shape=S((N_CHIPS, R, C), buf.dtype), in_specs=[HBM_SPEC], out_specs=HBM_SPEC,
                          scratch_shapes=[pltpu.SemaphoreType.DMA((3,)), pltpu.SemaphoreType.DMA((3,)),
                                          pltpu.SemaphoreType.DMA(())])(buf)


def _remote(src, dst, send_sem, recv_sem, device):
    return pltpu.make_async_remote_copy(src_ref=src, dst_ref=dst, send_sem=send_sem, recv_sem=recv_sem,
                                        device_id=device, device_id_type=MESH)


def _gather_matmul_weights(shards, *, name):
    n = len(shards)

    def body(*refs):
        ins, outs = refs[:n], refs[n:2 * n]
        send, recv, fsend, frecv, lsem = refs[2 * n:]
        x, y, c = lax.axis_index("x"), lax.axis_index("y"), lax.axis_index("c")
        me, sib = 2 * x + y, (x, y, 1 - c)
        peers = _chip_peers(x, y)
        started, local = [], []
        for w in range(n):
            cp = pltpu.make_async_copy(ins[w], outs[w].at[:, me], lsem.at[w])
            cp.start()
            local.append(cp)
            for k, (px, py) in enumerate(peers):
                cp = _remote(ins[w].at[:, c], outs[w].at[:, me, c], send.at[w, k], recv.at[w, k], (px, py, c))
                cp.start()
                started.append(cp)
        for w in range(n):
            for k, (px, py) in enumerate(peers):
                landed = outs[w].at[:, 2 * px + py, c]
                _remote(ins[w].at[:, c], landed, send.at[w, k], recv.at[w, k], (px, py, c)).wait_recv()
                cp = _remote(landed, landed, fsend.at[w, k], frecv.at[w, k], sib)
                cp.start()
                started.append(cp)
        for w in range(n):
            for k, (px, py) in enumerate(peers):
                _remote(ins[w].at[:, c], outs[w].at[:, 2 * px + py, 1 - c], fsend.at[w, k], frecv.at[w, k], sib).wait_recv()
        for cp in started:
            cp.wait_send()
        for cp in local:
            cp.wait()

    out_shape = tuple(S((s.shape[0], N_CHIPS) + s.shape[1:], s.dtype) for s in shards)
    sems = [pltpu.SemaphoreType.DMA((n, 3)) for _ in range(4)] + [pltpu.SemaphoreType.DMA((n,))]
    return pl.pallas_call(body, name=name, out_shape=out_shape, in_specs=[HBM_SPEC] * n, out_specs=(HBM_SPEC,) * n,
                          scratch_shapes=sems)(*shards)


SEM_SPEC = pl.BlockSpec(memory_space=pltpu.SEMAPHORE)
VMEM_SPEC = pl.BlockSpec(memory_space=pltpu.VMEM)


def _in_hbm(a):
    return pltpu.with_memory_space_constraint(a, pltpu.HBM)


def _chip_targets(x, y):
    return [(x, y), (1 - x, y), (x, 1 - y), (1 - x, 1 - y)]


def _spread_start(srcs, scatter, *, name, after=()):
    n = len(srcs)
    lands = [lax.empty((N_CHIPS,) + (s.shape[1:] if scatter else s.shape), s.dtype) for s in srcs]

    def body(*refs):
        src, land = refs[:n], refs[n:2 * n]
        send, recv, token = refs[2 * n + len(after)], refs[2 * n + len(after) + 1], refs[-1]
        x, y, c = lax.axis_index("x"), lax.axis_index("y"), lax.axis_index("c")
        me = 2 * x + y
        for w in range(n):
            for k, (px, py) in enumerate(_chip_targets(x, y)):
                block = src[w].at[2 * px + py] if scatter else src[w]
                _remote(block, land[w].at[me], send.at[N_CHIPS * w + k], recv.at[N_CHIPS * w + k], (px, py, c)).start()
        token[...] = jnp.zeros_like(token)

    thru = tuple(pltpu.HBM(a.shape, a.dtype) for a in list(srcs) + lands)
    sems = (pltpu.SemaphoreType.DMA((N_CHIPS * n,)), pltpu.SemaphoreType.DMA((N_CHIPS * n,)))
    out = pl.pallas_call(
        body, name=name, out_shape=sems + thru + (S((SUBLANE, LANE), f32),),
        in_specs=[HBM_SPEC] * (2 * n) + [ANY_SPEC] * len(after),
        out_specs=(SEM_SPEC, SEM_SPEC) + (HBM_SPEC,) * (2 * n) + (VMEM_SPEC,),
        input_output_aliases={i: 2 + i for i in range(2 * n)},
        compiler_params=pltpu.CompilerParams(has_side_effects=pltpu.SideEffectType.DATAFLOW_SIDE_EFFECTING),
    )(*[_in_hbm(a) for a in list(srcs) + lands], *after)
    return out[0], out[1], out[2:2 + n], out[2 + n:2 + 2 * n], out[-1]


def _spread_wait(send, recv, srcs, lands, after, scatter, *, name):
    n = len(srcs)

    def body(*refs):
        src, land, send, recv = refs[:n], refs[n:2 * n], refs[2 * n], refs[2 * n + 1]
        x, y, c = lax.axis_index("x"), lax.axis_index("y"), lax.axis_index("c")
        me = 2 * x + y
        for w in range(n):
            for k, (px, py) in enumerate(_chip_targets(x, y)):
                block = src[w].at[me] if scatter else src[w]
                cp = _remote(block, land[w].at[2 * px + py], send.at[N_CHIPS * w + k], recv.at[N_CHIPS * w + k], (px, py, c))
                cp.wait_send()
                cp.wait_recv()

    thru = tuple(pltpu.HBM(a.shape, a.dtype) for a in list(srcs) + list(lands))
    out = pl.pallas_call(
        body, name=name, out_shape=thru,
        in_specs=[HBM_SPEC] * (2 * n) + [SEM_SPEC, SEM_SPEC] + [ANY_SPEC] * len(after), out_specs=(HBM_SPEC,) * (2 * n),
        input_output_aliases={i: i for i in range(2 * n)},
        compiler_params=pltpu.CompilerParams(has_side_effects=pltpu.SideEffectType.DATAFLOW_SIDE_EFFECTING),
    )(*srcs, *lands, send, recv, *after)
    return out[:n], out[n:]


def _swap_sibling(bufs, *, name):
    n = len(bufs)

    def body(*refs):
        src, out, send, recv = refs[:n], refs[n:2 * n], refs[-2], refs[-1]
        sib = (lax.axis_index("x"), lax.axis_index("y"), 1 - lax.axis_index("c"))
        copies = [_remote(src[w], out[w], send.at[w], recv.at[w], sib) for w in range(n)]
        for cp in copies:
            cp.start()
        for cp in copies:
            cp.wait()

    return pl.pallas_call(body, name=name, out_shape=tuple(S(a.shape, a.dtype) for a in bufs),
                          in_specs=[HBM_SPEC] * n, out_specs=(HBM_SPEC,) * n,
                          scratch_shapes=[pltpu.SemaphoreType.DMA((n,)), pltpu.SemaphoreType.DMA((n,))])(*bufs)


def _swap_start(bufs, *, name):
    n = len(bufs)
    lands = [lax.empty(b.shape, b.dtype) for b in bufs]

    def body(*refs):
        src, land, send, recv, token = refs[:n], refs[n:2 * n], refs[2 * n], refs[2 * n + 1], refs[-1]
        sib = (lax.axis_index("x"), lax.axis_index("y"), 1 - lax.axis_index("c"))
        for w in range(n):
            _remote(src[w], land[w], send.at[w], recv.at[w], sib).start()
        token[...] = jnp.zeros_like(token)

    thru = tuple(pltpu.HBM(a.shape, a.dtype) for a in list(bufs) + lands)
    out = pl.pallas_call(
        body, name=name,
        out_shape=(pltpu.SemaphoreType.DMA((n,)), pltpu.SemaphoreType.DMA((n,))) + thru + (S((SUBLANE, LANE), f32),),
        in_specs=[HBM_SPEC] * (2 * n), out_specs=(SEM_SPEC, SEM_SPEC) + (HBM_SPEC,) * (2 * n) + (VMEM_SPEC,),
        input_output_aliases={i: 2 + i for i in range(2 * n)},
        compiler_params=pltpu.CompilerParams(has_side_effects=pltpu.SideEffectType.DATAFLOW_SIDE_EFFECTING),
    )(*[_in_hbm(a) for a in list(bufs) + lands])
    return out[0], out[1], out[2:2 + n], out[2 + n:2 + 2 * n], out[-1]


def _swap_wait(send, recv, bufs, lands, after, *, name):
    n = len(bufs)

    def body(*refs):
        src, land, send, recv = refs[:n], refs[n:2 * n], refs[2 * n], refs[2 * n + 1]
        sib = (lax.axis_index("x"), lax.axis_index("y"), 1 - lax.axis_index("c"))
        for w in range(n):
            cp = _remote(src[w], land[w], send.at[w], recv.at[w], sib)
            cp.wait_send()
            cp.wait_recv()

    thru = tuple(pltpu.HBM(a.shape, a.dtype) for a in list(bufs) + list(lands))
    out = pl.pallas_call(
        body, name=name, out_shape=thru,
        in_specs=[HBM_SPEC] * (2 * n) + [SEM_SPEC, SEM_SPEC] + [ANY_SPEC] * len(after), out_specs=(HBM_SPEC,) * (2 * n),
        input_output_aliases={i: i for i in range(2 * n)},
        compiler_params=pltpu.CompilerParams(has_side_effects=pltpu.SideEffectType.DATAFLOW_SIDE_EFFECTING),
    )(*bufs, *lands, send, recv, *after)
    return out[:n], out[n:]


N_DEVICES = 8


def _allgather_devices(buf, *, name):
    R, C = buf.shape

    def body(in_ref, out_ref, send, recv, lsem):
        x, y, c = lax.axis_index("x"), lax.axis_index("y"), lax.axis_index("c")
        me = 4 * x + 2 * y + c
        mine = pltpu.make_async_copy(in_ref, out_ref.at[me], lsem)
        mine.start()
        flips = [(d >> 2 & 1, d >> 1 & 1, d & 1) for d in range(1, N_DEVICES)]
        peers = [(1 - x if fx else x, 1 - y if fy else y, 1 - c if fc else c) for fx, fy, fc in flips]
        sends = []
        for k, peer in enumerate(peers):
            cp = _remote(in_ref, out_ref.at[me], send.at[k], recv.at[k], peer)
            cp.start()
            sends.append(cp)
        for k, (px, py, pc) in enumerate(peers):
            _remote(in_ref, out_ref.at[4 * px + 2 * py + pc], send.at[k], recv.at[k], (px, py, pc)).wait_recv()
        for cp in sends:
            cp.wait_send()
        mine.wait()

    return pl.pallas_call(body, name=name, out_shape=S((N_DEVICES, R, C), buf.dtype), in_specs=[HBM_SPEC], out_specs=HBM_SPEC,
                          scratch_shapes=[pltpu.SemaphoreType.DMA((N_DEVICES - 1,)), pltpu.SemaphoreType.DMA((N_DEVICES - 1,)),
                                          pltpu.SemaphoreType.DMA(())])(buf)


def _sum_slots(buf, *, name):
    ns, R, C = buf.shape
    tr = _pick(R, 512)
    assert R % tr == 0

    def body(*refs):
        acc = refs[0][...]
        for r in refs[1:ns]:
            acc = acc + r[...]
        refs[ns][...] = acc

    specs = [pl.BlockSpec((None, tr, C), functools.partial(lambda s, i: (s, i, 0), s)) for s in range(ns)]
    return pl.pallas_call(body, name=name, out_shape=S((R, C), buf.dtype), grid=(R // tr,), in_specs=specs,
                          out_specs=pl.BlockSpec((tr, C), lambda i: (i, 0)), compiler_params=_cp("parallel"))(*([buf] * ns))


ADAMW_BLOCK_BYTES = 1 << 20


def _adamw(w, m, v, groups, *, name, layer=None, prev=None):
    shape = w.shape if layer is None else w.shape[1:]
    C = shape[-1]
    Rr = math.prod(shape[:-1])
    tr = Rr
    if Rr * C * 4 > ADAMW_BLOCK_BYTES:
        tr = max(t for t in range(2 * SUBLANE, Rr + 1, 2 * SUBLANE) if Rr % t == 0 and t * C * 4 <= ADAMW_BLOCK_BYTES)
    c1 = 1.0 / (1.0 - ADAM_B1 ** ADAM_STEP)
    c2 = 1.0 / (1.0 - ADAM_B2 ** ADAM_STEP)
    if layer is None:
        to2 = lambda t: t.reshape(Rr, C)
        spec = pl.BlockSpec((tr, C), lambda i: (i, 0))
        res_shape = S((Rr, C), f32)
    else:
        to2 = lambda t: t.reshape(layer[1], Rr, C)
        spec = pl.BlockSpec((None, tr, C), functools.partial(lambda l, i: (l, i, 0), layer[0]))
        res_shape = S((layer[1], Rr, C), f32)
    wspec, spec = spec, pl.BlockSpec((tr, C), lambda i: (i, 0))
    g_specs, g_args, sizes = [], [], []
    for grp in groups:
        sizes.append(len(grp))
        for term in grp:
            if isinstance(term, tuple):
                arr, slot = term
                g_specs.append(pl.BlockSpec((None, tr, C), functools.partial(lambda s, i: (s, i, 0), slot)))
                g_args.append(arr.reshape(arr.shape[0], Rr, C))
            else:
                g_specs.append(spec)
                g_args.append(term.reshape(Rr, C))
    nterms = len(g_args)
    prev = () if prev is None else tuple(to2(t) for t in prev)

    def body(w_ref, m_ref, v_ref, *rest):
        t_refs, (g_ref, d_ref, mo_ref, vo_ref) = rest[:nterms], rest[-4:]
        g, pos = None, 0
        for size in sizes:
            part = None
            for r in t_refs[pos:pos + size]:
                t = r[...].astype(f32)
                part = t if part is None else part + t
            pos += size
            g = part if g is None else g + part
        mn = ADAM_B1 * m_ref[...] + (1.0 - ADAM_B1) * g
        vn = ADAM_B2 * v_ref[...] + (1.0 - ADAM_B2) * (g * g)
        g_ref[...] = g
        mo_ref[...] = mn
        vo_ref[...] = vn
        d_ref[...] = -ADAM_LR * ((mn * c1) / (jnp.sqrt(vn * c2) + ADAM_EPS) + ADAM_WD * w_ref[...])

    out = pl.pallas_call(body, name=name, out_shape=(res_shape,) * 4, grid=(Rr // tr,),
                         in_specs=[wspec] * 3 + g_specs + [ANY_SPEC] * len(prev), out_specs=(wspec,) * 4,
                         input_output_aliases={3 + nterms + k: k for k in range(len(prev))},
                         compiler_params=_cp("parallel"))(to2(w), to2(m), to2(v), *g_args, *prev)
    return tuple(o.reshape(w.shape) for o in out)


def _pack_rows(parts, dtype):
    flat = jnp.concatenate([p.reshape(-1).astype(dtype) for p in parts])
    n = flat.shape[0]
    unit = PACK_COLS * 2 * SUBLANE
    padded = -(-n // unit) * unit
    return jnp.pad(flat, (0, padded - n)).reshape(padded // PACK_COLS, PACK_COLS)


def _unpack_rows(flat2d, shapes):
    flat = flat2d.reshape(-1)
    out, off = [], 0
    for shp in shapes:
        n = math.prod(shp)
        out.append(flat[off:off + n].reshape(shp))
        off += n
    return out


def _gather_weights(local, names, dtype, *, name):
    shapes = [local[n].shape for n in names]
    got = _all_gather_chips(_pack_rows([local[n] for n in names], dtype), name=name)
    per_chip = [_unpack_rows(got[s], shapes) for s in range(N_CHIPS)]
    return {n: jnp.concatenate([per_chip[s][k] for s in range(N_CHIPS)], axis=SHARD_AXIS[n]) for k, n in enumerate(names)}


def kernel(x, mem, norm_g, ssm_in_w, ssm_conv_w, ssm_conv_b, ssm_dt_bias, ssm_A_log, ssm_D, ssm_norm_g, ssm_out_w, cf_pw1_w, cf_pw1_b, cf_dw_w, cf_dw_b, cf_ln_g, cf_ln_b, cf_pw2_w, cf_pw2_b, xa_mem_g, xa_q_w, xa_kv_w, xa_o_w, ffn_in_w, ffn_conv_w, ffn_conv_b, ffn_out_w, loss_target, m_norm_g, m_ssm_in_w, m_ssm_conv_w, m_ssm_conv_b, m_ssm_dt_bias, m_ssm_A_log, m_ssm_D, m_ssm_norm_g, m_ssm_out_w, m_cf_pw1_w, m_cf_pw1_b, m_cf_dw_w, m_cf_dw_b, m_cf_ln_g, m_cf_ln_b, m_cf_pw2_w, m_cf_pw2_b, m_xa_mem_g, m_xa_q_w, m_xa_kv_w, m_xa_o_w, m_ffn_in_w, m_ffn_conv_w, m_ffn_conv_b, m_ffn_out_w, v_norm_g, v_ssm_in_w, v_ssm_conv_w, v_ssm_conv_b, v_ssm_dt_bias, v_ssm_A_log, v_ssm_D, v_ssm_norm_g, v_ssm_out_w, v_cf_pw1_w, v_cf_pw1_b, v_cf_dw_w, v_cf_dw_b, v_cf_ln_g, v_cf_ln_b, v_cf_pw2_w, v_cf_pw2_b, v_xa_mem_g, v_xa_q_w, v_xa_kv_w, v_xa_o_w, v_ffn_in_w, v_ffn_conv_w, v_ffn_conv_b, v_ffn_out_w):
    w_local = dict(zip(WEIGHT_NAMES, (norm_g, ssm_in_w, ssm_conv_w, ssm_conv_b, ssm_dt_bias, ssm_A_log, ssm_D, ssm_norm_g,
                                      ssm_out_w, cf_pw1_w, cf_pw1_b, cf_dw_w, cf_dw_b, cf_ln_g, cf_ln_b, cf_pw2_w, cf_pw2_b,
                                      xa_mem_g, xa_q_w, xa_kv_w, xa_o_w, ffn_in_w, ffn_conv_w, ffn_conv_b, ffn_out_w)))
    m_local = dict(zip(WEIGHT_NAMES, (m_norm_g, m_ssm_in_w, m_ssm_conv_w, m_ssm_conv_b, m_ssm_dt_bias, m_ssm_A_log, m_ssm_D,
                                      m_ssm_norm_g, m_ssm_out_w, m_cf_pw1_w, m_cf_pw1_b, m_cf_dw_w, m_cf_dw_b, m_cf_ln_g,
                                      m_cf_ln_b, m_cf_pw2_w, m_cf_pw2_b, m_xa_mem_g, m_xa_q_w, m_xa_kv_w, m_xa_o_w,
                                      m_ffn_in_w, m_ffn_conv_w, m_ffn_conv_b, m_ffn_out_w)))
    v_local = dict(zip(WEIGHT_NAMES, (v_norm_g, v_ssm_in_w, v_ssm_conv_w, v_ssm_conv_b, v_ssm_dt_bias, v_ssm_A_log, v_ssm_D,
                                      v_ssm_norm_g, v_ssm_out_w, v_cf_pw1_w, v_cf_pw1_b, v_cf_dw_w, v_cf_dw_b, v_cf_ln_g,
                                      v_cf_ln_b, v_cf_pw2_w, v_cf_pw2_b, v_xa_mem_g, v_xa_q_w, v_xa_kv_w, v_xa_o_w,
                                      v_ffn_in_w, v_ffn_conv_w, v_ffn_conv_b, v_ffn_out_w)))

    small = [n for n in WEIGHT_NAMES if n not in MATMUL_WEIGHTS]
    small_sharded = [n for n in small if SHARD_AXIS[n] is not None]
    W = {n: w_local[n] for n in small if SHARD_AXIS[n] is None}
    W.update(_gather_weights(w_local, small_sharded, f32, name="gather_small_weights"))

    def layer_index(n, i):
        return i // 2 if n in MIXER_WEIGHTS else i

    def keys_of(i, parts):
        return [(n, layer_index(n, i)) for part in parts for n in _layer_matmul_weights(i, part)]

    def shards(keys):
        return [w_local[n][l].astype(bf16) for n, l in keys]

    def usable(n, a):
        return a.reshape(N_CHIPS * a.shape[1], a.shape[2]) if n in ROW_SHARDED else a

    mixer0 = keys_of(0, ("mixer",))
    got0 = _gather_matmul_weights([s.reshape(1, 2, s.shape[0] // 2, s.shape[1]) for s in shards(mixer0)],
                                  name="gather_layer0_mixer")
    gather_groups = {(0, "rest"): keys_of(0, ("rest",))}
    gather_groups.update({(i, "mixer"): keys_of(i, ("mixer", "rest")) for i in range(1, DEPTH)})
    gathers, tokens, landed = {}, [], {}
    for gkey in sorted(gather_groups):
        send, recv, srcs, lands, token = _spread_start(shards(gather_groups[gkey]), False, name="gather_start_%d_%s" % gkey,
                                                       after=(got0[0], W[small_sharded[0]]))
        gathers[gkey] = (send, recv, srcs, lands)
        tokens.append(token)

    def layer_weights(i, part, after):
        if (i, part) == (0, "mixer"):
            landed.update({k: g.reshape((N_CHIPS, 2 * g.shape[3], g.shape[4])) for k, g in zip(mixer0, got0)})
        elif (i, part) in gathers:
            _, lands = _spread_wait(*gathers[i, part], (after,), False, name="gather_wait_%d_%s" % (i, part))
            landed.update(zip(gather_groups[i, part], lands))
        return {n: usable(n, landed[n, layer_index(n, i)]) for n in _layer_matmul_weights(i, part)}

    pending, scatters, swaps, own, sib = {}, {}, {}, {}, {}

    def scatter_start(gkey, keys, gl):
        send, recv, srcs, lands, token = _spread_start([gl[k] for k in keys], True, name="grads_start_%d_%s" % gkey)
        scatters[gkey] = (keys, send, recv, srcs, lands)
        return token

    def layer_grads(i, part, gl):
        grads = {(n, layer_index(n, i)): gl[n] for n in _layer_matmul_weights(i, part)}
        behind = []
        if part == "rest":
            if i + 1 < DEPTH:
                keys, send, recv, srcs, lands = scatters.pop((i + 1, "mixer"))
                _, lands = _spread_wait(send, recv, srcs, lands, (grads['xa_kv_w', i],), True,
                                        name="grads_wait_%d" % (i + 1))
                send, recv, srcs, lands, token = _swap_start(lands, name="grads_swap_start_%d" % (i + 1))
                swaps[i + 1] = (keys, send, recv, srcs, lands)
                behind.append(token)
            if i == 0:
                behind.append(scatter_start((0, "rest"), list(grads), grads))
            else:
                pending.update(grads)
        else:
            pending.update(grads)
            behind.append(scatter_start((i, "mixer"), list(pending), dict(pending)))
            pending.clear()
        return behind

    sse, gx, gsmall = _device_step(x[0], mem[0], loss_target[0], W, layer_weights, layer_grads, tuple(tokens))

    loss = lax.psum(0.5 * sse[0, 0] / D_MODEL, ("x", "y", "c"))

    last_keys, last_lands = [], []
    for gkey in sorted(scatters):
        keys, send, recv, srcs, lands = scatters[gkey]
        _, lands = _spread_wait(send, recv, srcs, lands, (gx,), True, name="grads_wait_%d_%s" % gkey)
        last_keys += keys
        last_lands += list(lands)
    own.update(zip(last_keys, last_lands))
    sib.update(zip(last_keys, _swap_sibling(last_lands, name="grads_swap_last")))
    for i in sorted(swaps):
        keys, send, recv, srcs, lands = swaps[i]
        mine, theirs = _swap_wait(send, recv, srcs, lands, (gx,), name="grads_swap_wait_%d" % i)
        own.update(zip(keys, mine))
        sib.update(zip(keys, theirs))
    small_shapes = [gsmall[n].shape for n in small]
    slots = _allgather_devices(_pack_rows([gsmall[n] for n in small], f32), name="allgather_small_grads")
    gsum = dict(zip(small, _unpack_rows(_sum_slots(slots, name="sum_small_grads"), small_shapes)))
    chip = 2 * lax.axis_index("x") + lax.axis_index("y")

    res = {}
    for n in MATMUL_WEIGHTS:
        layers, out = w_local[n].shape[0], None
        for l in range(layers):
            groups = [[(own[n, l], s) for s in range(N_CHIPS)], [(sib[n, l], s) for s in range(N_CHIPS)]]
            out = _adamw(w_local[n], m_local[n], v_local[n], groups, name="adamw_%s_%d" % (n, l), layer=(l, layers), prev=out)
        res[n] = out
    for n in small:
        g, ax = gsum[n], SHARD_AXIS[n]
        if ax is not None:
            width = w_local[n].shape[ax]
            g = lax.dynamic_slice_in_dim(g, chip * width, width, axis=ax)
        res[n] = _adamw(w_local[n], m_local[n], v_local[n], [[g]], name="adamw_" + n)
    return (loss, gx[None], *[res[n][0] for n in WEIGHT_NAMES], *[res[n][1] for n in WEIGHT_NAMES],
            *[res[n][2] for n in WEIGHT_NAMES], *[res[n][3] for n in WEIGHT_NAMES])
```

```python
import functools
import math

import jax
import jax.numpy as jnp
from jax import lax
from jax.experimental import pallas as pl
from jax.experimental.pallas import tpu as pltpu

f32 = jnp.float32
bf16 = jnp.bfloat16
S = jax.ShapeDtypeStruct

D_MODEL = 1024
DEPTH = 4
D_INNER = 2048
HEAD_DIM = 64
N_GROUPS = 4
HEADS_PER_GROUP = 8
N_SSM_HEADS = 32
D_STATE = 128
CHUNK = 128
SSM_CONV = 4
CONV_DIM = 3072
CF_KERNEL = 31
N_MEM = 256
XA_HEADS = 4
XA_HEAD_DIM = 256
D_FF = 2816
FFN_CONV = 3
EPS = 1e-6
ADAM_LR, ADAM_B1, ADAM_B2, ADAM_EPS, ADAM_WD, ADAM_STEP = 0.001, 0.9, 0.999, 1e-08, 0.01, 10

LANE = 128
SUBLANE = 8
ROW_SUB = 64
VMEM_LIMIT = 56 * 1024 * 1024
N_CHIPS = 4
PACK_COLS = 1024

WEIGHT_NAMES = ['norm_g', 'ssm_in_w', 'ssm_conv_w', 'ssm_conv_b', 'ssm_dt_bias', 'ssm_A_log', 'ssm_D', 'ssm_norm_g',
                'ssm_out_w', 'cf_pw1_w', 'cf_pw1_b', 'cf_dw_w', 'cf_dw_b', 'cf_ln_g', 'cf_ln_b', 'cf_pw2_w', 'cf_pw2_b',
                'xa_mem_g', 'xa_q_w', 'xa_kv_w', 'xa_o_w', 'ffn_in_w', 'ffn_conv_w', 'ffn_conv_b', 'ffn_out_w']
SHARD_AXIS = {'norm_g': 2, 'ssm_in_w': 2, 'ssm_conv_w': 2, 'ssm_conv_b': None, 'ssm_dt_bias': None, 'ssm_A_log': None,
              'ssm_D': None, 'ssm_norm_g': None, 'ssm_out_w': 1, 'cf_pw1_w': 2, 'cf_pw1_b': 1, 'cf_dw_w': 2, 'cf_dw_b': 1,
              'cf_ln_g': 1, 'cf_ln_b': 1, 'cf_pw2_w': 1, 'cf_pw2_b': 1, 'xa_mem_g': None, 'xa_q_w': 1, 'xa_kv_w': 2,
              'xa_o_w': 1, 'ffn_in_w': 2, 'ffn_conv_w': 2, 'ffn_conv_b': None, 'ffn_out_w': 1}
MATMUL_WEIGHTS = ('ssm_in_w', 'ssm_out_w', 'cf_pw1_w', 'cf_pw2_w', 'xa_q_w', 'xa_kv_w', 'xa_o_w', 'ffn_in_w', 'ffn_out_w')


def _cp(*sem):
    return pltpu.CompilerParams(dimension_semantics=tuple(sem), vmem_limit_bytes=VMEM_LIMIT)


def _pick(dim, pref):
    if dim <= pref:
        return dim
    best = None
    for t in range(LANE, pref + 1, LANE):
        if dim % t == 0:
            best = t
    assert best is not None, (dim, pref)
    return best


def _sigmoid(x):
    return 1.0 / (1.0 + jnp.exp(-x))


def _silu(x):
    return x * _sigmoid(x)


def _dsilu(x):
    s = _sigmoid(x)
    return s * (1.0 + x * (1.0 - s))


def _softplus(x):
    return jnp.maximum(x, 0.0) + jnp.log(1.0 + jnp.exp(-jnp.abs(x)))


_DN = {"nn": (((1,), (0,)), ((), ())), "nt": (((1,), (1,)), ((), ())), "tn": (((0,), (0,)), ((), ()))}


def _mm(a, b, mode, *, name, out_dtype=f32, bias=None, add=None, b_shards=False, out_shards=False):
    bshape = (b.shape[1], b.shape[2] * N_CHIPS) if b_shards else b.shape
    if mode == "nn":
        (M, K), (K2, N) = a.shape, bshape
    elif mode == "nt":
        (M, K), (N, K2) = a.shape, bshape
    else:
        (K, M), (K2, N) = a.shape, bshape
    assert K == K2, (a.shape, b.shape, mode)
    n_unit = N // N_CHIPS if ((b_shards and mode == "nn") or out_shards) else N
    k_unit = K // N_CHIPS if (b_shards and mode == "nt") else K
    tm, tn, tk = _pick(M, 1024), _pick(n_unit, 1408), _pick(k_unit, 1408)
    nk, nj_u, nk_u = K // tk, n_unit // tn, k_unit // tk
    a_spec = {"nn": pl.BlockSpec((tm, tk), lambda i, j, k: (i, k)), "nt": pl.BlockSpec((tm, tk), lambda i, j, k: (i, k)),
              "tn": pl.BlockSpec((tk, tm), lambda i, j, k: (k, i))}[mode]
    if not b_shards:
        b_spec = {"nn": pl.BlockSpec((tk, tn), lambda i, j, k: (k, j)), "nt": pl.BlockSpec((tn, tk), lambda i, j, k: (j, k)),
                  "tn": pl.BlockSpec((tk, tn), lambda i, j, k: (k, j))}[mode]
    else:
        b_spec = {"nn": pl.BlockSpec((None, tk, tn), lambda i, j, k: (j // nj_u, k, j % nj_u)),
                  "nt": pl.BlockSpec((None, tn, tk), lambda i, j, k: (k // nk_u, j, k % nk_u))}[mode]
    in_specs, args = [a_spec, b_spec], [a, b]
    if bias is not None:
        in_specs.append(pl.BlockSpec((1, tn), lambda i, j, k: (0, j)))
        args.append(bias)
    if add is not None:
        in_specs.append(pl.BlockSpec((tm, tn), lambda i, j, k: (i, j)))
        args.append(add)
    if not out_shards:
        out_shape, out_spec = S((M, N), out_dtype), pl.BlockSpec((tm, tn), lambda i, j, k: (i, j))
    else:
        out_shape = S((N_CHIPS, M, n_unit), out_dtype)
        out_spec = pl.BlockSpec((None, tm, tn), lambda i, j, k: (j // nj_u, i, j % nj_u))
    dn = _DN[mode]
    has_bias, has_add = bias is not None, add is not None

    def body(a_ref, b_ref, *rest):
        rest = list(rest)
        bias_ref = rest.pop(0) if has_bias else None
        add_ref = rest.pop(0) if has_add else None
        o_ref = rest[0]

        def finish(r):
            if has_bias:
                r = r + bias_ref[...]
            if has_add:
                r = r + add_ref[...].astype(f32)
            o_ref[...] = r.astype(out_dtype)

        part = lax.dot_general(a_ref[...].astype(bf16), b_ref[...].astype(bf16), dn, preferred_element_type=f32)
        if nk == 1:
            finish(part)
            return
        acc_ref = rest[1]
        k = pl.program_id(2)

        @pl.when(k == 0)
        def _():
            acc_ref[...] = part

        @pl.when(k > 0)
        def _():
            acc_ref[...] += part

        @pl.when(k == nk - 1)
        def _():
            finish(acc_ref[...])

    return pl.pallas_call(
        body, name=name, out_shape=out_shape, grid=(M // tm, N // tn, nk),
        in_specs=in_specs, out_specs=out_spec, scratch_shapes=[pltpu.VMEM((tm, tn), f32)] if nk > 1 else [],
        compiler_params=_cp("parallel", "parallel", "arbitrary"))(*args)


def _rows(tm, C):
    return pl.BlockSpec((tm, C), lambda i: (i, 0))


def _const(shape):
    return pl.BlockSpec(shape, lambda i: tuple(0 for _ in shape))


def _rms_val(x, g):
    r = lax.rsqrt(jnp.mean(x * x, axis=-1, keepdims=True) + EPS)
    return x * r * g


def _rms_bwd_val(x, g, dy):
    r = lax.rsqrt(jnp.mean(x * x, axis=-1, keepdims=True) + EPS)
    xn = x * r
    dxh = dy * g
    dx = r * (dxh - xn * jnp.mean(dxh * xn, axis=-1, keepdims=True))
    return dx, jnp.sum(dy * xn, axis=0, keepdims=True)


ANY_SPEC = pl.BlockSpec(memory_space=pl.ANY)


def _rmsnorm_fwd(x, g, *, name, after=()):
    L, C = x.shape
    tm = _pick(L, 512)

    def body(x_ref, g_ref, *rest):
        rest[-1][...] = _rms_val(x_ref[...], g_ref[...]).astype(bf16)

    return pl.pallas_call(body, name=name, out_shape=S((L, C), bf16), grid=(L // tm,),
                          in_specs=[_rows(tm, C), _const((1, C))] + [ANY_SPEC] * len(after), out_specs=_rows(tm, C),
                          compiler_params=_cp("parallel"))(x, g, *after)


def _resid_norm_fwd(x, mix, g_post, g_next, *, name):
    L, C = x.shape
    tm = _pick(L, 512)
    want_h = g_next is not None

    def body(x_ref, m_ref, gp_ref, *rest):
        xn = x_ref[...] + _rms_val(m_ref[...], gp_ref[...])
        if want_h:
            gn_ref, xo_ref, h_ref = rest
            h_ref[...] = _rms_val(xn, gn_ref[...]).astype(bf16)
        else:
            (xo_ref,) = rest
        xo_ref[...] = xn

    in_specs = [_rows(tm, C), _rows(tm, C), _const((1, C))]
    args = [x, mix, g_post]
    out_shape, out_specs = [S((L, C), f32)], [_rows(tm, C)]
    if want_h:
        in_specs.append(_const((1, C)))
        args.append(g_next)
        out_shape.append(S((L, C), bf16))
        out_specs.append(_rows(tm, C))
    out = pl.pallas_call(body, name=name, out_shape=tuple(out_shape), grid=(L // tm,), in_specs=in_specs,
                         out_specs=tuple(out_specs), compiler_params=_cp("parallel"))(*args)
    return (out[0], out[1]) if want_h else (out[0], None)


def _norm_bwd(x, g, dy, *, name, add=None, out_dtype=f32, after=()):
    L, C = x.shape
    tm = _pick(L, 512)
    has_add = add is not None

    def body(x_ref, g_ref, dy_ref, *rest):
        rest = list(rest)
        add_ref = rest.pop(0) if has_add else None
        dx_ref, dg_ref, cs_ref = rest[-3:]
        i = pl.program_id(0)

        @pl.when(i == 0)
        def _():
            dg_ref[...] = jnp.zeros_like(dg_ref)
            cs_ref[...] = jnp.zeros_like(cs_ref)

        dx, dg = _rms_bwd_val(x_ref[...], g_ref[...], dy_ref[...].astype(f32))
        dg_ref[...] += dg
        cs_ref[...] += jnp.sum(dx, axis=0, keepdims=True)
        if has_add:
            dx = dx + add_ref[...]
        dx_ref[...] = dx.astype(out_dtype)

    in_specs = [_rows(tm, C), _const((1, C)), _rows(tm, C)]
    args = [x, g, dy]
    if has_add:
        in_specs.append(_rows(tm, C))
        args.append(add)
    in_specs += [ANY_SPEC] * len(after)
    args += list(after)
    return pl.pallas_call(body, name=name, out_shape=(S((L, C), out_dtype), S((1, C), f32), S((1, C), f32)),
                          grid=(L // tm,), in_specs=in_specs,
                          out_specs=(_rows(tm, C), _const((1, C)), _const((1, C))),
                          compiler_params=_cp("arbitrary"))(*args)


def _loss_fwd_bwd(y, target, *, name):
    L, C = y.shape
    tm = _pick(L, 512)

    def body(y_ref, t_ref, acc_ref, dy_ref):
        i = pl.program_id(0)

        @pl.when(i == 0)
        def _():
            acc_ref[...] = jnp.zeros_like(acc_ref)

        e = y_ref[...] - t_ref[...]
        rs = jnp.sum(e * e, axis=-1, keepdims=True)
        acc_ref[...] += jnp.broadcast_to(jnp.sum(rs, axis=0, keepdims=True), (1, LANE))
        dy_ref[...] = e * (1.0 / C)

    return pl.pallas_call(body, name=name, out_shape=(S((1, LANE), f32), S((L, C), f32)), grid=(L // tm,),
                          in_specs=[_rows(tm, C), _rows(tm, C)], out_specs=(_const((1, LANE)), _rows(tm, C)),
                          compiler_params=_cp("arbitrary"))(y, target)


def _halo_rows(K):
    return SUBLANE if K - 1 <= SUBLANE else 32


def _prev_halo_spec(tm, H, C):
    return pl.BlockSpec((H, C), lambda i: (jnp.maximum(i * (tm // H) - 1, 0), 0))


def _next_halo_spec(tm, H, C, L):
    return pl.BlockSpec((H, C), lambda i: (jnp.minimum((i + 1) * (tm // H), L // H - 1), 0))


def _shift_down(ext, s):
    return ext if s == 0 else pltpu.roll(ext, s, axis=0)


def _shift_up(ext, s):
    return ext if s == 0 else pltpu.roll(ext, ext.shape[0] - s, axis=0)


def _causal_conv(ext, H, w_ref, K):
    acc = None
    for k in range(K):
        term = _shift_down(ext, K - 1 - k)[H:, :] * w_ref[k:k + 1, :]
        acc = term if acc is None else acc + term
    return acc


def _anticausal_conv(ext, tm, w_ref, K):
    acc = None
    for k in range(K):
        term = _shift_up(ext, K - 1 - k)[:tm, :] * w_ref[k:k + 1, :]
        acc = term if acc is None else acc + term
    return acc


def _tap_grads(dw_ref, d_cur, x_ext, H, K):
    for k in range(K):
        dw_ref[k:k + 1, :] += jnp.sum(d_cur * _shift_down(x_ext, K - 1 - k)[H:, :], axis=0, keepdims=True)


def _pad_taps(w, K):
    return jnp.pad(w, ((0, _halo_rows(K) - K), (0, 0)))


def _conv_act_fwd(x, w, b, *, K, act, name, out_dtype, tm_pref=256):
    L, C = x.shape
    H = _halo_rows(K)
    tm = _pick(L, tm_pref)
    Co = C if act == "silu" else C // 2

    rs = min(ROW_SUB, tm)

    def body(h_ref, x_ref, w_ref, b_ref, o_ref, ext_scr):
        i = pl.program_id(0)
        ext_scr[0:H] = jnp.where(i > 0, h_ref[...], 0.0)
        ext_scr[H:] = x_ref[...]
        for j in range(Co // LANE):
            lanes = [j] if act == "silu" else [j, j + Co // LANE]
            wb = [(w_ref[:, c * LANE:(c + 1) * LANE], b_ref[:, c * LANE:(c + 1) * LANE]) for c in lanes]
            for r0 in range(0, tm, rs):
                us = [_causal_conv(ext_scr[r0:r0 + rs + H, c * LANE:(c + 1) * LANE], H, wc, K) + bc
                      for c, (wc, bc) in zip(lanes, wb)]
                y = _silu(us[0]) if act == "silu" else _silu(us[0]) * us[1]
                o_ref[r0:r0 + rs, j * LANE:(j + 1) * LANE] = y.astype(out_dtype)

    return pl.pallas_call(body, name=name, out_shape=S((L, Co), out_dtype), grid=(L // tm,),
                          in_specs=[_prev_halo_spec(tm, H, C), _rows(tm, C), _const((H, C)), _const((1, C))],
                          out_specs=_rows(tm, Co), scratch_shapes=[pltpu.VMEM((H + tm, C), f32)],
                          compiler_params=_cp("parallel"))(x, x, w, b)


def _conv_act_bwd(x, dparts, w, b, *, K, act, name, tm_pref=256):
    L, C = x.shape
    H = _halo_rows(K)
    tm = _pick(L, tm_pref)
    nb = L // tm
    Co = C if act == "silu" else C // 2
    nparts = len(dparts)

    rs = min(ROW_SUB, tm)

    def body(hp_ref, x_ref, hn_ref, w_ref, b_ref, *rest):
        d_refs, dn_refs = rest[:nparts], rest[nparts:2 * nparts]
        dx_ref, dw_ref, db_ref, ext_scr, d_scr = rest[2 * nparts:]
        i = pl.program_id(0)

        @pl.when(i == 0)
        def _():
            dw_ref[...] = jnp.zeros_like(dw_ref)
            db_ref[...] = jnp.zeros_like(db_ref)

        ext_scr[0:H] = jnp.where(i > 0, hp_ref[...], 0.0)
        ext_scr[H:H + tm] = x_ref[...]
        ext_scr[H + tm:] = jnp.where(i < nb - 1, hn_ref[...], 0.0)
        off = 0
        for r, rn in zip(d_refs, dn_refs):
            d_scr[0:tm, off:off + r.shape[1]] = r[...].astype(f32)
            d_scr[tm:, off:off + r.shape[1]] = jnp.where(i < nb - 1, rn[...].astype(f32), 0.0)
            off += r.shape[1]
        for j in range(Co // LANE):
            lanes = [j] if act == "silu" else [j, j + Co // LANE]
            wb = [(w_ref[:, c * LANE:(c + 1) * LANE], b_ref[:, c * LANE:(c + 1) * LANE]) for c in lanes]
            db_acc = [jnp.zeros((1, LANE), f32) for _ in lanes]
            dw_acc = [[jnp.zeros((1, LANE), f32) for _ in range(K)] for _ in lanes]
            for r0 in range(0, tm, rs):
                exts = [ext_scr[r0:r0 + rs + 2 * H, c * LANE:(c + 1) * LANE] for c in lanes]
                us = [_causal_conv(e, H, wc, K) + bc for e, (wc, bc) in zip(exts, wb)]
                d = d_scr[r0:r0 + rs + H, j * LANE:(j + 1) * LANE]
                dus = [d * _dsilu(us[0])] if act == "silu" else [d * us[1] * _dsilu(us[0]), d * _silu(us[0])]
                for q, (c, e, du, (wc, _)) in enumerate(zip(lanes, exts, dus, wb)):
                    dx_ref[r0:r0 + rs, c * LANE:(c + 1) * LANE] = _anticausal_conv(du, rs, wc, K).astype(bf16)
                    du_cur = du[:rs]
                    db_acc[q] = db_acc[q] + jnp.sum(du_cur, axis=0, keepdims=True)
                    for k in range(K):
                        dw_acc[q][k] = dw_acc[q][k] + jnp.sum(du_cur * _shift_down(e[:H + rs], K - 1 - k)[H:], axis=0,
                                                              keepdims=True)
            for q, c in enumerate(lanes):
                db_ref[:, c * LANE:(c + 1) * LANE] += db_acc[q]
                for k in range(K):
                    dw_ref[k:k + 1, c * LANE:(c + 1) * LANE] += dw_acc[q][k]

    in_specs = [_prev_halo_spec(tm, H, C), _rows(tm, C), _next_halo_spec(tm, H, C, L), _const((H, C)), _const((1, C))]
    in_specs += [_rows(tm, p.shape[1]) for p in dparts] + [_next_halo_spec(tm, H, p.shape[1], L) for p in dparts]
    return pl.pallas_call(body, name=name, out_shape=(S((L, C), bf16), S((H, C), f32), S((1, C), f32)), grid=(nb,),
                          in_specs=in_specs, out_specs=(_rows(tm, C), _const((H, C)), _const((1, C))),
                          scratch_shapes=[pltpu.VMEM((tm + 2 * H, C), f32), pltpu.VMEM((tm + H, Co), f32)],
                          compiler_params=_cp("arbitrary"))(x, x, x, w, b, *dparts, *dparts)


def _cf_fwd(u, dw_w, dw_b, ln_g, ln_b, *, name):
    L, C2 = u.shape
    C = C2 // 2
    K, H = CF_KERNEL, _halo_rows(CF_KERNEL)
    tm = _pick(L, 256)

    def body(h_ref, u_ref, w_ref, b_ref, g_ref, lb_ref, c_ref, s_ref):
        i = pl.program_id(0)
        halo = jnp.where(i > 0, h_ref[...], 0.0)
        ext = jnp.concatenate([halo, u_ref[...]], axis=0)
        glu = ext[:, :C] * _sigmoid(ext[:, C:])
        c = _causal_conv(glu, H, w_ref, K) + b_ref[...]
        c_ref[...] = c
        mu = jnp.mean(c, axis=-1, keepdims=True)
        xc = c - mu
        var = jnp.mean(xc * xc, axis=-1, keepdims=True)
        ln = xc * lax.rsqrt(var + EPS) * g_ref[...] + lb_ref[...]
        s_ref[...] = _silu(ln).astype(bf16)

    return pl.pallas_call(body, name=name, out_shape=(S((L, C), f32), S((L, C), bf16)), grid=(L // tm,),
                          in_specs=[_prev_halo_spec(tm, H, C2), _rows(tm, C2), _const((H, C)), _const((1, C)),
                                    _const((1, C)), _const((1, C))],
                          out_specs=(_rows(tm, C), _rows(tm, C)), compiler_params=_cp("parallel"))(u, u, dw_w, dw_b, ln_g, ln_b)


def _cf_ln_bwd(c, ln_g, ln_b, ds, *, name):
    L, C = c.shape
    tm = _pick(L, 512)

    def body(c_ref, g_ref, lb_ref, ds_ref, dc_ref, dg_ref, db_ref):
        i = pl.program_id(0)

        @pl.when(i == 0)
        def _():
            dg_ref[...] = jnp.zeros_like(dg_ref)
            db_ref[...] = jnp.zeros_like(db_ref)

        c = c_ref[...]
        mu = jnp.mean(c, axis=-1, keepdims=True)
        xc = c - mu
        r = lax.rsqrt(jnp.mean(xc * xc, axis=-1, keepdims=True) + EPS)
        xh = xc * r
        ln = xh * g_ref[...] + lb_ref[...]
        dln = ds_ref[...].astype(f32) * _dsilu(ln)
        dg_ref[...] += jnp.sum(dln * xh, axis=0, keepdims=True)
        db_ref[...] += jnp.sum(dln, axis=0, keepdims=True)
        dxh = dln * g_ref[...]
        dc_ref[...] = r * (dxh - jnp.mean(dxh, axis=-1, keepdims=True) - xh * jnp.mean(dxh * xh, axis=-1, keepdims=True))

    return pl.pallas_call(body, name=name, out_shape=(S((L, C), f32), S((1, C), f32), S((1, C), f32)), grid=(L // tm,),
                          in_specs=[_rows(tm, C), _const((1, C)), _const((1, C)), _rows(tm, C)],
                          out_specs=(_rows(tm, C), _const((1, C)), _const((1, C))),
                          compiler_params=_cp("arbitrary"))(c, ln_g, ln_b, ds)


def _cf_glu_bwd(u, dc, dw_w, *, name):
    L, C2 = u.shape
    C = C2 // 2
    K, H = CF_KERNEL, _halo_rows(CF_KERNEL)
    tm = _pick(L, 256)
    nb = L // tm

    def body(uh_ref, u_ref, dc_ref, dch_ref, w_ref, du_ref, dw_ref, db_ref, dus_ref):
        i = pl.program_id(0)

        @pl.when(i == 0)
        def _():
            dw_ref[...] = jnp.zeros_like(dw_ref)
            db_ref[...] = jnp.zeros_like(db_ref)
            dus_ref[...] = jnp.zeros_like(dus_ref)

        halo = jnp.where(i > 0, uh_ref[...], 0.0)
        ext = jnp.concatenate([halo, u_ref[...]], axis=0)
        sg = _sigmoid(ext[:, C:])
        glu = ext[:, :C] * sg
        dc = dc_ref[...]
        dnext = jnp.where(i < nb - 1, dch_ref[...], 0.0)
        dglu = _anticausal_conv(jnp.concatenate([dc, dnext], axis=0), tm, w_ref, K)
        a_cur, sg_cur = ext[H:, :C], sg[H:, :]
        du = jnp.concatenate([dglu * sg_cur, dglu * a_cur * sg_cur * (1.0 - sg_cur)], axis=1)
        du_ref[...] = du.astype(bf16)
        dus_ref[...] += jnp.sum(du, axis=0, keepdims=True)
        db_ref[...] += jnp.sum(dc, axis=0, keepdims=True)
        _tap_grads(dw_ref, dc, glu, H, K)

    return pl.pallas_call(body, name=name,
                          out_shape=(S((L, C2), bf16), S((H, C), f32), S((1, C), f32), S((1, C2), f32)), grid=(nb,),
                          in_specs=[_prev_halo_spec(tm, H, C2), _rows(tm, C2), _rows(tm, C), _next_halo_spec(tm, H, C, L),
                                    _const((H, C))],
                          out_specs=(_rows(tm, C2), _const((H, C)), _const((1, C)), _const((1, C2))),
                          compiler_params=_cp("arbitrary"))(u, u, dc, dc, dw_w)


def _gated_norm_fwd(y, z, g, *, name):
    L, C = y.shape
    tm = _pick(L, 256)

    def body(y_ref, z_ref, g_ref, o_ref):
        o_ref[...] = _rms_val(y_ref[...] * _silu(z_ref[...]), g_ref[...]).astype(bf16)

    return pl.pallas_call(body, name=name, out_shape=S((L, C), bf16), grid=(L // tm,),
                          in_specs=[_rows(tm, C), _rows(tm, C), _const((1, C))], out_specs=_rows(tm, C),
                          compiler_params=_cp("parallel"))(y, z, g)


def _gated_norm_bwd(y, z, g, dyn, *, name):
    L, C = y.shape
    tm = _pick(L, 256)

    def body(y_ref, z_ref, g_ref, d_ref, dy_ref, dz_ref, dg_ref):
        i = pl.program_id(0)

        @pl.when(i == 0)
        def _():
            dg_ref[...] = jnp.zeros_like(dg_ref)

        y, z = y_ref[...], z_ref[...]
        sz = _silu(z)
        du, dg = _rms_bwd_val(y * sz, g_ref[...], d_ref[...].astype(f32))
        dg_ref[...] += dg
        dy_ref[...] = du * sz
        dz_ref[...] = (du * y * _dsilu(z)).astype(bf16)

    return pl.pallas_call(body, name=name, out_shape=(S((L, C), f32), S((L, C), bf16), S((1, C), f32)), grid=(L // tm,),
                          in_specs=[_rows(tm, C), _rows(tm, C), _const((1, C)), _rows(tm, C)],
                          out_specs=(_rows(tm, C), _rows(tm, C), _const((1, C))),
                          compiler_params=_cp("arbitrary"))(y, z, g, dyn)


_XA_SCALE = XA_HEAD_DIM ** -0.5


def _attn_fwd(q, kv, *, name):
    L, C = q.shape
    tm = _pick(L, 512)
    Dh = XA_HEAD_DIM

    def body(q_ref, kv_ref, o_ref):
        for h in range(XA_HEADS):
            qh = q_ref[:, h * Dh:(h + 1) * Dh]
            kh = kv_ref[:, h * Dh:(h + 1) * Dh]
            vh = kv_ref[:, C + h * Dh:C + (h + 1) * Dh]
            s = lax.dot_general(qh, kh, _DN["nt"], preferred_element_type=f32) * _XA_SCALE
            e = jnp.exp(s - jnp.max(s, axis=-1, keepdims=True))
            p = e / jnp.sum(e, axis=-1, keepdims=True)
            o_ref[:, h * Dh:(h + 1) * Dh] = jnp.dot(p.astype(bf16), vh, preferred_element_type=f32).astype(bf16)

    return pl.pallas_call(body, name=name, out_shape=S((L, C), bf16), grid=(L // tm,),
                          in_specs=[_rows(tm, C), _const((N_MEM, 2 * C))], out_specs=_rows(tm, C),
                          compiler_params=_cp("parallel"))(q, kv)


def _attn_bwd(q, kv, do, *, name):
    L, C = q.shape
    tm = _pick(L, 512)
    Dh = XA_HEAD_DIM

    def body(q_ref, kv_ref, do_ref, dq_ref, dkv_ref):
        i = pl.program_id(0)

        @pl.when(i == 0)
        def _():
            dkv_ref[...] = jnp.zeros_like(dkv_ref)

        for h in range(XA_HEADS):
            qh = q_ref[:, h * Dh:(h + 1) * Dh]
            kh = kv_ref[:, h * Dh:(h + 1) * Dh]
            vh = kv_ref[:, C + h * Dh:C + (h + 1) * Dh]
            doh = do_ref[:, h * Dh:(h + 1) * Dh]
            s = lax.dot_general(qh, kh, _DN["nt"], preferred_element_type=f32) * _XA_SCALE
            e = jnp.exp(s - jnp.max(s, axis=-1, keepdims=True))
            p = e / jnp.sum(e, axis=-1, keepdims=True)
            pb = p.astype(bf16)
            dkv_ref[:, C + h * Dh:C + (h + 1) * Dh] += lax.dot_general(pb, doh, _DN["tn"], preferred_element_type=f32)
            dp = lax.dot_general(doh, vh, _DN["nt"], preferred_element_type=f32)
            ds = (p * (dp - jnp.sum(dp * p, axis=-1, keepdims=True)) * _XA_SCALE).astype(bf16)
            dq_ref[:, h * Dh:(h + 1) * Dh] = jnp.dot(ds, kh, preferred_element_type=f32).astype(bf16)
            dkv_ref[:, h * Dh:(h + 1) * Dh] += lax.dot_general(ds, qh, _DN["tn"], preferred_element_type=f32)

    return pl.pallas_call(body, name=name, out_shape=(S((L, C), bf16), S((N_MEM, 2 * C), f32)), grid=(L // tm,),
                          in_specs=[_rows(tm, C), _const((N_MEM, 2 * C)), _rows(tm, C)],
                          out_specs=(_rows(tm, C), _const((N_MEM, 2 * C))),
                          compiler_params=_cp("arbitrary"))(q, kv, do)


Q = CHUNK
PAIRS = HEADS_PER_GROUP // 2
GW = HEADS_PER_GROUP * HEAD_DIM


def _split(x, pieces):
    out = []
    for _ in range(pieces - 1):
        p = x.astype(bf16)
        out.append(p)
        x = x - p.astype(f32)
    return out + [x.astype(bf16)]


def _sel_right(x, sel, mode="nn", pieces=2):
    return sum(lax.dot_general(p, sel, _DN[mode], preferred_element_type=f32) for p in _split(x, pieces))


def _sel_left(sel, x, pieces=3):
    return sum(lax.dot_general(sel, p, _DN["nn"], preferred_element_type=f32) for p in _split(x, pieces))


def _ssd_common(dt_ref, hp_ref):
    dt_pre = dt_ref[...] + hp_ref[0:1, :]
    dt = _softplus(dt_pre)
    A = -jnp.exp(hp_ref[1:2, :])
    a = dt * A
    row = lax.broadcasted_iota(jnp.int32, (Q, Q), 0)
    col = lax.broadcasted_iota(jnp.int32, (Q, Q), 1)
    tri = row >= col
    cs = _sel_left(tri.astype(bf16), a)
    T = cs[Q - 1:Q, :]
    return dict(dt_pre=dt_pre, dt=dt, A=A, cs=cs, csT=cs.T, T=T, ecs=jnp.exp(cs), eend=jnp.exp(T - cs), eT=jnp.exp(T),
                tri=tri, row=row, col=col)


def _pair_expand(v, jj, lo):
    return jnp.where(lo, v[:, 2 * jj:2 * jj + 1], v[:, 2 * jj + 1:2 * jj + 2])


def _decay(cm, h):
    seg = cm["cs"][:, h:h + 1] - cm["csT"][h:h + 1, :]
    return jnp.where(cm["tri"], jnp.exp(jnp.where(cm["tri"], seg, 0.0)), 0.0)


def _decay_t(cm, h):
    keep = cm["row"] <= cm["col"]
    seg = cm["csT"][h:h + 1, :] - cm["cs"][:, h:h + 1]
    return jnp.where(keep, jnp.exp(jnp.where(keep, seg, 0.0)), 0.0)


def _ssd_fwd(act, dtp, hp, *, name):
    L = act.shape[0]
    nc = L // Q

    def body(xs_ref, b_ref, c_ref, dt_ref, hp_ref, y_ref, hs_ref, h_scr):
        c = pl.program_id(1)

        @pl.when(c == 0)
        def _():
            h_scr[...] = jnp.zeros_like(h_scr)

        cm = _ssd_common(dt_ref, hp_ref)
        Bb, Cb = b_ref[...].astype(bf16), c_ref[...].astype(bf16)
        CB = lax.dot_general(Cb, Bb, _DN["nt"], preferred_element_type=f32)
        lo = lax.broadcasted_iota(jnp.int32, (Q, LANE), 1) < HEAD_DIM
        top = lax.broadcasted_iota(jnp.int32, (LANE, LANE), 0) < HEAD_DIM
        Drow = hp_ref[2:3, :]
        for jj in range(PAIRS):
            hA, hB = 2 * jj, 2 * jj + 1
            dtx, ecsx, eendx = (_pair_expand(cm[k], jj, lo) for k in ("dt", "ecs", "eend"))
            xs_p = xs_ref[:, jj * LANE:(jj + 1) * LANE]
            Xd = xs_p * dtx
            Y = None
            for h, Xm in ((hA, jnp.where(lo, Xd, 0.0)), (hB, jnp.where(lo, 0.0, Xd))):
                W = (CB * _decay(cm, h)).astype(bf16)
                t = jnp.dot(W, Xm.astype(bf16), preferred_element_type=f32)
                Y = t if Y is None else Y + t
            Hp = h_scr[jj]
            hs_ref[0, jj] = Hp
            Yoff = lax.dot_general(Cb, Hp.astype(bf16), _DN["nt"], preferred_element_type=f32) * ecsx
            Dx = jnp.where(lo[0:1, :], Drow[:, hA:hA + 1], Drow[:, hB:hB + 1])
            y_ref[:, jj * LANE:(jj + 1) * LANE] = Y + Yoff + xs_p * Dx
            Snew = lax.dot_general((Xd * eendx).astype(bf16), Bb, _DN["tn"], preferred_element_type=f32)
            eTx = jnp.where(top, cm["eT"][:, hA:hA + 1], cm["eT"][:, hB:hB + 1])
            h_scr[jj] = Hp * eTx + Snew

    return pl.pallas_call(
        body, name=name, out_shape=(S((L, D_INNER), f32), S((nc, N_SSM_HEADS // 2, LANE, D_STATE), f32)),
        grid=(N_GROUPS, nc),
        in_specs=[pl.BlockSpec((Q, GW), lambda g, c: (c, g)),
                  pl.BlockSpec((Q, D_STATE), lambda g, c: (c, D_INNER // D_STATE + g)),
                  pl.BlockSpec((Q, D_STATE), lambda g, c: (c, D_INNER // D_STATE + N_GROUPS + g)),
                  pl.BlockSpec((Q, LANE), lambda g, c: (c, g)),
                  pl.BlockSpec((SUBLANE, LANE), lambda g, c: (0, g))],
        out_specs=(pl.BlockSpec((Q, GW), lambda g, c: (c, g)),
                   pl.BlockSpec((1, PAIRS, LANE, D_STATE), lambda g, c: (c, g, 0, 0))),
        scratch_shapes=[pltpu.VMEM((PAIRS, LANE, D_STATE), f32)],
        compiler_params=_cp("arbitrary", "arbitrary"))(act, act, act, dtp, hp)


def _ssd_bwd(act, dtp, hp, dy, hs, *, name):
    L = act.shape[0]
    nc = L // Q

    def body(xs_ref, b_ref, c_ref, dt_ref, hp_ref, dy_ref, hs_ref, dxs_ref, db_ref, dc_ref, ddt_ref, dhp_ref, dh_scr):
        c = pl.program_id(1)

        @pl.when(c == 0)
        def _():
            dh_scr[...] = jnp.zeros_like(dh_scr)
            dhp_ref[...] = jnp.zeros_like(dhp_ref)

        cm = _ssd_common(dt_ref, hp_ref)
        Bb, Cb = b_ref[...].astype(bf16), c_ref[...].astype(bf16)
        CB = lax.dot_general(Cb, Bb, _DN["nt"], preferred_element_type=f32)
        CBT = lax.dot_general(Bb, Cb, _DN["nt"], preferred_element_type=f32)
        lane = lax.broadcasted_iota(jnp.int32, (Q, LANE), 1)
        sub = lax.broadcasted_iota(jnp.int32, (LANE, LANE), 0)
        lo = lane < HEAD_DIM
        top = sub < HEAD_DIM
        Drow = hp_ref[2:3, :]
        zero = jnp.zeros((Q, LANE), f32)
        dcs, dcsT, ddtx, dC, dB, dCB = zero, zero, zero, zero, zero, jnp.zeros((Q, Q), f32)
        dD_row = jnp.zeros((1, LANE), f32)
        dT_row = jnp.zeros((1, LANE), f32)
        for jj in range(PAIRS):
            hA, hB = 2 * jj, 2 * jj + 1
            Pj = (lane == jnp.where(top, hA, hB)).astype(bf16)
            dtx, ecsx, eendx = (_pair_expand(cm[k], jj, lo) for k in ("dt", "ecs", "eend"))
            xs_p = xs_ref[:, jj * LANE:(jj + 1) * LANE]
            dY_p = dy_ref[:, jj * LANE:(jj + 1) * LANE]
            Xd = xs_p * dtx
            Xdb = Xd.astype(bf16)
            Hp, dHn = hs_ref[0, jj], dh_scr[jj]
            Hb, dHb = Hp.astype(bf16), dHn.astype(bf16)
            EdYb = (dY_p * ecsx).astype(bf16)
            YoffN = lax.dot_general(Cb, Hb, _DN["nt"], preferred_element_type=f32)
            dC = dC + jnp.dot(EdYb, Hb, preferred_element_type=f32)
            dH_off = lax.dot_general(EdYb, Cb, _DN["tn"], preferred_element_type=f32)
            R = lax.dot_general(Bb, dHb, _DN["nt"], preferred_element_type=f32)
            Xe = Xd * eendx
            dB = dB + jnp.dot(Xe.astype(bf16), dHb, preferred_element_type=f32)
            dXd = R * eendx
            RXe = R * Xe
            dcs = dcs + _sel_right(dY_p * YoffN * ecsx - RXe, Pj)
            HH = dHn * Hp
            hh = [jnp.sum(jnp.sum(HH[r0:r0 + HEAD_DIM], axis=0, keepdims=True), axis=1, keepdims=True) for r0 in (0, HEAD_DIM)]
            rxe_cols = jnp.broadcast_to(jnp.sum(RXe, axis=0, keepdims=True), (SUBLANE, LANE))
            dT_row = dT_row + _sel_right(rxe_cols, Pj, pieces=3)[0:1] \
                + (jnp.where(lane[0:1] == hA, hh[0], 0.0) + jnp.where(lane[0:1] == hB, hh[1], 0.0)) * cm["eT"]
            for h, keep in ((hA, lo), (hB, jnp.logical_not(lo))):
                M = _decay(cm, h)
                Wf = CB * M
                dYm = jnp.where(keep, dY_p, 0.0).astype(bf16)
                dW = lax.dot_general(dYm, Xdb, _DN["nt"], preferred_element_type=f32)
                WT = (CBT * _decay_t(cm, h)).astype(bf16)
                dXd = dXd + jnp.dot(WT, dYm, preferred_element_type=f32)
                Z = dW * Wf
                dcs = dcs + _sel_right(Z, (lane == h).astype(bf16))
                dcsT = dcsT + jnp.where(sub == h, jnp.sum(Z, axis=0, keepdims=True), 0.0)
                dCB = dCB + dW * M
            Dx = jnp.where(lo[0:1, :], Drow[:, hA:hA + 1], Drow[:, hB:hB + 1])
            dxs_ref[:, jj * LANE:(jj + 1) * LANE] = dXd * dtx + dY_p * Dx
            ddtx = ddtx + _sel_right(dXd * xs_p, Pj)
            dD_cols = jnp.broadcast_to(jnp.sum(dY_p * xs_p, axis=0, keepdims=True), (SUBLANE, LANE))
            dD_row = dD_row + _sel_right(dD_cols, Pj, pieces=3)[0:1]
            eTx = jnp.where(top, cm["eT"][:, hA:hA + 1], cm["eT"][:, hB:hB + 1])
            dh_scr[jj] = dHn * eTx + dH_off
        dCBb = dCB.astype(bf16)
        dc_ref[...] = dC + jnp.dot(dCBb, Bb, preferred_element_type=f32)
        db_ref[...] = dB + lax.dot_general(dCBb, Cb, _DN["tn"], preferred_element_type=f32)
        dcs = dcs - dcsT.T + jnp.where(lax.broadcasted_iota(jnp.int32, (Q, LANE), 0) == Q - 1, dT_row, 0.0)
        da = _sel_left((cm["row"] <= cm["col"]).astype(bf16), dcs)
        ddt_pre = (da * cm["A"] + ddtx) * _sigmoid(cm["dt_pre"])
        ddt_ref[...] = ddt_pre
        r8 = lax.broadcasted_iota(jnp.int32, (SUBLANE, LANE), 0)
        dhp_ref[...] += jnp.where(r8 == 0, jnp.sum(ddt_pre, axis=0, keepdims=True),
                                  jnp.where(r8 == 1, jnp.sum(da * cm["dt"], axis=0, keepdims=True) * cm["A"],
                                            jnp.where(r8 == 2, dD_row, 0.0)))

    rev = lambda c: nc - 1 - c
    return pl.pallas_call(
        body, name=name,
        out_shape=(S((L, D_INNER), f32), S((L, N_GROUPS * D_STATE), f32), S((L, N_GROUPS * D_STATE), f32),
                   S((L, N_GROUPS * LANE), f32), S((SUBLANE, N_GROUPS * LANE), f32)),
        grid=(N_GROUPS, nc),
        in_specs=[pl.BlockSpec((Q, GW), lambda g, c: (rev(c), g)),
                  pl.BlockSpec((Q, D_STATE), lambda g, c: (rev(c), D_INNER // D_STATE + g)),
                  pl.BlockSpec((Q, D_STATE), lambda g, c: (rev(c), D_INNER // D_STATE + N_GROUPS + g)),
                  pl.BlockSpec((Q, LANE), lambda g, c: (rev(c), g)),
                  pl.BlockSpec((SUBLANE, LANE), lambda g, c: (0, g)),
                  pl.BlockSpec((Q, GW), lambda g, c: (rev(c), g)),
                  pl.BlockSpec((1, PAIRS, LANE, D_STATE), lambda g, c: (rev(c), g, 0, 0))],
        out_specs=(pl.BlockSpec((Q, GW), lambda g, c: (rev(c), g)),
                   pl.BlockSpec((Q, D_STATE), lambda g, c: (rev(c), g)),
                   pl.BlockSpec((Q, D_STATE), lambda g, c: (rev(c), g)),
                   pl.BlockSpec((Q, LANE), lambda g, c: (rev(c), g)),
                   pl.BlockSpec((SUBLANE, LANE), lambda g, c: (0, g))),
        scratch_shapes=[pltpu.VMEM((PAIRS, LANE, D_STATE), f32)],
        compiler_params=_cp("arbitrary", "arbitrary"))(act, act, act, dtp, hp, dy, hs)


def _group_pad_cols(w):
    lead = w.shape[:-1]
    w = w.reshape(lead + (N_GROUPS, HEADS_PER_GROUP))
    w = jnp.pad(w, [(0, 0)] * len(lead) + [(0, 0), (0, LANE - HEADS_PER_GROUP)])
    return w.reshape(lead + (N_GROUPS * LANE,))


def _group_unpad_cols(w):
    lead = w.shape[:-1]
    return w.reshape(lead + (N_GROUPS, LANE))[..., :HEADS_PER_GROUP].reshape(lead + (N_SSM_HEADS,))


def _row(v):
    return v.reshape(1, -1)


ROW_SHARDED = ('ssm_out_w', 'cf_pw2_w', 'xa_q_w', 'xa_o_w', 'ffn_out_w')
COL_SHARDED = ('cf_pw1_w', 'xa_kv_w', 'ffn_in_w')


MIXER_WEIGHTS = ('ssm_in_w', 'ssm_out_w', 'cf_pw1_w', 'cf_pw2_w')


def _layer_matmul_weights(i, part):
    if part == "mixer":
        return ('ssm_in_w', 'ssm_out_w') if i % 2 == 0 else ('cf_pw1_w', 'cf_pw2_w')
    return ('xa_q_w', 'xa_kv_w', 'xa_o_w', 'ffn_in_w', 'ffn_out_w')


def _device_step(x, mem, target, W, layer_weights, layer_grads, start_after=()):
    ng = W['norm_g']
    lw = []
    for i in range(DEPTH):
        j = i // 2
        p = {}
        if i % 2 == 0:
            p['cw'] = _pad_taps(W['ssm_conv_w'][j], SSM_CONV)
            p['cb'] = _row(W['ssm_conv_b'][j])
            hp = jnp.stack([_group_pad_cols(W['ssm_dt_bias'][j]), _group_pad_cols(W['ssm_A_log'][j]),
                            _group_pad_cols(W['ssm_D'][j])])
            p['hp'] = jnp.pad(hp, ((0, SUBLANE - 3), (0, 0)))
            p['sng'] = _row(W['ssm_norm_g'][j])
        else:
            p['pw1b'] = _row(W['cf_pw1_b'][j])
            p['dww'], p['dwb'] = _pad_taps(W['cf_dw_w'][j], CF_KERNEL), _row(W['cf_dw_b'][j])
            p['lng'], p['lnb'] = _row(W['cf_ln_g'][j]), _row(W['cf_ln_b'][j])
            p['pw2b'] = _row(W['cf_pw2_b'][j])
        p['memg'] = _row(W['xa_mem_g'][i])
        p['fcw'], p['fcb'] = _pad_taps(W['ffn_conv_w'][i], FFN_CONV), _row(W['ffn_conv_b'][i])
        p['g'] = [_row(ng[i, s]) for s in range(6)]
        lw.append(p)

    def wmm(a, wl, wname, mode, **kw):
        return _mm(a, wl[wname], mode, b_shards=wname in COL_SHARDED, **kw)

    saved = []
    X = x
    h = _rmsnorm_fwd(X, lw[0]['g'][0], name="norm_in", after=start_after)
    for i in range(DEPTH):
        p, sv = lw[i], {}
        wl = dict(layer_weights(i, "mixer", X))
        sv['X0'], sv['h'], sv['wl'] = X, h, wl
        if i % 2 == 0:
            win = jnp.concatenate([wl['ssm_in_w'][s] for s in range(N_CHIPS)], axis=1)
            wl['wz'], wl['wx'] = win[:, :D_INNER], win[:, D_INNER:D_INNER + CONV_DIM]
            wl['wdt'] = _group_pad_cols(win[:, D_INNER + CONV_DIM:])
            z = _mm(h, wl['wz'], "nn", name="ssm_z")
            xbc = _mm(h, wl['wx'], "nn", name="ssm_xbc")
            dtp = _mm(h, wl['wdt'], "nn", name="ssm_dt")
            act = _conv_act_fwd(xbc, p['cw'], p['cb'], K=SSM_CONV, act="silu", name="ssm_conv_fwd", out_dtype=f32)
            y, hs = _ssd_fwd(act, dtp, p['hp'], name="ssd_fwd")
            yn = _gated_norm_fwd(y, z, p['sng'], name="ssm_gnorm_fwd")
            mix = wmm(yn, wl, 'ssm_out_w', "nn", name="ssm_out")
            sv.update(z=z, xbc=xbc, dtp=dtp, act=act, y=y, hs=hs, yn=yn)
        else:
            u = wmm(h, wl, 'cf_pw1_w', "nn", name="cf_pw1", bias=p['pw1b'])
            c, s = _cf_fwd(u, p['dww'], p['dwb'], p['lng'], p['lnb'], name="cf_conv_fwd")
            mix = wmm(s, wl, 'cf_pw2_w', "nn", name="cf_pw2", bias=p['pw2b'])
            sv.update(u=u, c=c, s=s)
        wl.update(layer_weights(i, "rest", mix))
        X1, h2 = _resid_norm_fwd(X, mix, p['g'][1], p['g'][2], name="resid_norm_a")
        q = wmm(h2, wl, 'xa_q_w', "nn", name="xa_q", out_dtype=bf16)
        m = _rmsnorm_fwd(mem, p['memg'], name="xa_mem_norm")
        kv = wmm(m, wl, 'xa_kv_w', "nn", name="xa_kv", out_dtype=bf16)
        o = _attn_fwd(q, kv, name="xa_attn_fwd")
        a = wmm(o, wl, 'xa_o_w', "nn", name="xa_o")
        X2, h3 = _resid_norm_fwd(X1, a, p['g'][3], p['g'][4], name="resid_norm_b")
        u0 = wmm(h3, wl, 'ffn_in_w', "nn", name="ffn_in")
        fact = _conv_act_fwd(u0, p['fcw'], p['fcb'], K=FFN_CONV, act="swiglu", name="ffn_conv_fwd", out_dtype=bf16)
        f = wmm(fact, wl, 'ffn_out_w', "nn", name="ffn_out")
        g_next = lw[i + 1]['g'][0] if i + 1 < DEPTH else None
        X3, hn = _resid_norm_fwd(X2, f, p['g'][5], g_next, name="resid_norm_c" if g_next is not None else "resid_norm_last")
        sv.update(mix=mix, X1=X1, h2=h2, q=q, m=m, kv=kv, o=o, a=a, X2=X2, h3=h3, u0=u0, fact=fact, f=f)
        saved.append(sv)
        X, h = X3, hn

    sse, G = _loss_fwd_bwd(X, target, name="loss")

    small = [n for n in WEIGHT_NAMES if n not in MATMUL_WEIGHTS]
    gr = {n: [None] * W[n].shape[0] for n in small}

    def dwmm(gl, a, d, wname, *, name):
        if wname in COL_SHARDED:
            gl[wname] = _mm(a, d, "tn", name=name, out_dtype=bf16, out_shards=True)
        else:
            g = _mm(a, d, "tn", name=name, out_dtype=bf16)
            gl[wname] = g.reshape(N_CHIPS, g.shape[0] // N_CHIPS, g.shape[1])

    dng = [[None] * 6 for _ in range(DEPTH)]
    for i in reversed(range(DEPTH)):
        p, sv, j = lw[i], saved[i], i // 2
        wl, gl = sv['wl'], {}
        df, dng[i][5], _ = _norm_bwd(sv['f'], p['g'][5], G, name="nb_f", out_dtype=bf16)
        dwmm(gl, sv['fact'], df, 'ffn_out_w', name="ffn_out_dw")
        dfact = wmm(df, wl, 'ffn_out_w', "nt", name="ffn_out_dx")
        du0, dcw, dcb = _conv_act_bwd(sv['u0'], [dfact], p['fcw'], p['fcb'], K=FFN_CONV, act="swiglu", name="ffn_conv_bwd")
        gr['ffn_conv_w'][i], gr['ffn_conv_b'][i] = dcw[:FFN_CONV], dcb[0]
        dwmm(gl, sv['h3'], du0, 'ffn_in_w', name="ffn_in_dw")
        dh3 = wmm(du0, wl, 'ffn_in_w', "nt", name="ffn_in_dx")
        G, dng[i][4], _ = _norm_bwd(sv['X2'], p['g'][4], dh3, name="nb_x2", add=G)
        da, dng[i][3], _ = _norm_bwd(sv['a'], p['g'][3], G, name="nb_a", out_dtype=bf16)
        dwmm(gl, sv['o'], da, 'xa_o_w', name="xa_o_dw")
        do = wmm(da, wl, 'xa_o_w', "nt", name="xa_o_dx", out_dtype=bf16)
        dq, dkv = _attn_bwd(sv['q'], sv['kv'], do, name="xa_attn_bwd")
        dwmm(gl, sv['h2'], dq, 'xa_q_w', name="xa_q_dw")
        dh2 = wmm(dq, wl, 'xa_q_w', "nt", name="xa_q_dx")
        dwmm(gl, sv['m'], dkv, 'xa_kv_w', name="xa_kv_dw")
        dm = wmm(dkv, wl, 'xa_kv_w', "nt", name="xa_kv_dx")
        _, dmg, _ = _norm_bwd(mem, p['memg'], dm, name="nb_mem")
        gr['xa_mem_g'][i] = dmg[0]
        G, dng[i][2], _ = _norm_bwd(sv['X1'], p['g'][2], dh2, name="nb_x1", add=G)
        behind = tuple(layer_grads(i, "rest", gl))
        dmix, dng[i][1], dmix_sum = _norm_bwd(sv['mix'], p['g'][1], G, name="nb_mix", out_dtype=bf16, after=behind)
        if i % 2 == 0:
            dwmm(gl, sv['yn'], dmix, 'ssm_out_w', name="ssm_out_dw")
            dyn = wmm(dmix, wl, 'ssm_out_w', "nt", name="ssm_out_dx")
            dy, dz, dsng = _gated_norm_bwd(sv['y'], sv['z'], p['sng'], dyn, name="ssm_gnorm_bwd")
            gr['ssm_norm_g'][j] = dsng[0]
            dxs, dB, dC, ddtp, dhp = _ssd_bwd(sv['act'], sv['dtp'], p['hp'], dy, sv['hs'], name="ssd_bwd")
            gr['ssm_dt_bias'][j], gr['ssm_A_log'][j], gr['ssm_D'][j] = (_group_unpad_cols(dhp[r]) for r in range(3))
            dxbc, dcw, dcb = _conv_act_bwd(sv['xbc'], [dxs, dB, dC], p['cw'], p['cb'], K=SSM_CONV, act="silu",
                                           name="ssm_conv_bwd")
            gr['ssm_conv_w'][j], gr['ssm_conv_b'][j] = dcw[:SSM_CONV], dcb[0]
            hh = sv['h']
            dwz = _mm(hh, dz, "tn", name="ssm_z_dw", out_dtype=bf16)
            dwx = _mm(hh, dxbc, "tn", name="ssm_xbc_dw", out_dtype=bf16)
            dwdt = _mm(hh, ddtp, "tn", name="ssm_dt_dw", out_dtype=bf16)
            din = jnp.concatenate([dwz, dwx, _group_unpad_cols(dwdt)], axis=1)
            gl['ssm_in_w'] = jnp.stack(jnp.split(din, N_CHIPS, axis=1))
            dh = _mm(dz, wl['wz'], "nt", name="ssm_z_dx")
            dh = _mm(dxbc, wl['wx'], "nt", name="ssm_xbc_dx", add=dh)
            dh = _mm(ddtp, wl['wdt'], "nt", name="ssm_dt_dx", add=dh)
        else:
            dwmm(gl, sv['s'], dmix, 'cf_pw2_w', name="cf_pw2_dw")
            gr['cf_pw2_b'][j] = dmix_sum[0]
            ds = wmm(dmix, wl, 'cf_pw2_w', "nt", name="cf_pw2_dx")
            dc, dlg, dlb = _cf_ln_bwd(sv['c'], p['lng'], p['lnb'], ds, name="cf_ln_bwd")
            gr['cf_ln_g'][j], gr['cf_ln_b'][j] = dlg[0], dlb[0]
            du, ddw, ddb, dus = _cf_glu_bwd(sv['u'], dc, p['dww'], name="cf_glu_bwd")
            gr['cf_dw_w'][j], gr['cf_dw_b'][j], gr['cf_pw1_b'][j] = ddw[:CF_KERNEL], ddb[0], dus[0]
            dwmm(gl, sv['h'], du, 'cf_pw1_w', name="cf_pw1_dw")
            dh = wmm(du, wl, 'cf_pw1_w', "nt", name="cf_pw1_dx")
        behind = tuple(layer_grads(i, "mixer", gl))
        G, dng[i][0], _ = _norm_bwd(sv['X0'], p['g'][0], dh, name="nb_x0", add=G, after=behind)
    gr['norm_g'] = [jnp.concatenate(dng[i], axis=0) for i in range(DEPTH)]
    gsmall = {n: jnp.stack(gr[n]) for n in small}
    return sse, G, gsmall


MESH = pl.DeviceIdType.MESH
HBM_SPEC = pl.BlockSpec(memory_space=pltpu.HBM)


def _chip_peers(x, y):
    return [(1 - x, y), (x, 1 - y), (1 - x, 1 - y)]


def _all_gather_chips(buf, *, name):
    R, C = buf.shape

    def body(in_ref, out_ref, send_sems, recv_sems, local_sem):
        x, y, c = lax.axis_index("x"), lax.axis_index("y"), lax.axis_index("c")
        me = 2 * x + y
        mine = pltpu.make_async_copy(in_ref, out_ref.at[me], local_sem)
        mine.start()
        peers = _chip_peers(x, y)
        sends = []
        for k, (px, py) in enumerate(peers):
            cp = pltpu.make_async_remote_copy(src_ref=in_ref, dst_ref=out_ref.at[me], send_sem=send_sems.at[k],
                                              recv_sem=recv_sems.at[k], device_id=(px, py, c), device_id_type=MESH)
            cp.start()
            sends.append(cp)
        for k, (px, py) in enumerate(peers):
            pltpu.make_async_remote_copy(src_ref=in_ref, dst_ref=out_ref.at[2 * px + py], send_sem=send_sems.at[k],
                                         recv_sem=recv_sems.at[k], device_id=(px, py, c), device_id_type=MESH).wait_recv()
        for cp in sends:
            cp.wait_send()
        mine.wait()

    return pl.pallas_call(body, name=name, out_shape=S((N_CHIPS, R, C), buf.dtype), in_specs=[HBM_SPEC], out_specs=HBM_SPEC,
                          scratch_shapes=[pltpu.SemaphoreType.DMA((3,)), pltpu.SemaphoreType.DMA((3,)),
                                          pltpu.SemaphoreType.DMA(())])(buf)


def _remote(src, dst, send_sem, recv_sem, device):
    return pltpu.make_async_remote_copy(src_ref=src, dst_ref=dst, send_sem=send_sem, recv_sem=recv_sem,
                                        device_id=device, device_id_type=MESH)


def _gather_matmul_weights(shards, *, name):
    n = len(shards)

    def body(*refs):
        ins, outs = refs[:n], refs[n:2 * n]
        send, recv, fsend, frecv, lsem = refs[2 * n:]
        x, y, c = lax.axis_index("x"), lax.axis_index("y"), lax.axis_index("c")
        me, sib = 2 * x + y, (x, y, 1 - c)
        peers = _chip_peers(x, y)
        started, local = [], []
        for w in range(n):
            cp = pltpu.make_async_copy(ins[w], outs[w].at[:, me], lsem.at[w])
            cp.start()
            local.append(cp)
            for k, (px, py) in enumerate(peers):
                cp = _remote(ins[w].at[:, c], outs[w].at[:, me, c], send.at[w, k], recv.at[w, k], (px, py, c))
                cp.start()
                started.append(cp)
        for w in range(n):
            for k, (px, py) in enumerate(peers):
                landed = outs[w].at[:, 2 * px + py, c]
                _remote(ins[w].at[:, c], landed, send.at[w, k], recv.at[w, k], (px, py, c)).wait_recv()
                cp = _remote(landed, landed, fsend.at[w, k], frecv.at[w, k], sib)
                cp.start()
                started.append(cp)
        for w in range(n):
            for k, (px, py) in enumerate(peers):
                _remote(ins[w].at[:, c], outs[w].at[:, 2 * px + py, 1 - c], fsend.at[w, k], frecv.at[w, k], sib).wait_recv()
        for cp in started:
            cp.wait_send()
        for cp in local:
            cp.wait()

    out_shape = tuple(S((s.shape[0], N_CHIPS) + s.shape[1:], s.dtype) for s in shards)
    sems = [pltpu.SemaphoreType.DMA((n, 3)) for _ in range(4)] + [pltpu.SemaphoreType.DMA((n,))]
    return pl.pallas_call(body, name=name, out_shape=out_shape, in_specs=[HBM_SPEC] * n, out_specs=(HBM_SPEC,) * n,
                          scratch_shapes=sems)(*shards)


SEM_SPEC = pl.BlockSpec(memory_space=pltpu.SEMAPHORE)
VMEM_SPEC = pl.BlockSpec(memory_space=pltpu.VMEM)


def _in_hbm(a):
    return pltpu.with_memory_space_constraint(a, pltpu.HBM)


def _chip_targets(x, y):
    return [(x, y), (1 - x, y), (x, 1 - y), (1 - x, 1 - y)]


def _spread_start(srcs, scatter, *, name, after=()):
    n = len(srcs)
    lands = [lax.empty((N_CHIPS,) + (s.shape[1:] if scatter else s.shape), s.dtype) for s in srcs]

    def body(*refs):
        src, land = refs[:n], refs[n:2 * n]
        send, recv, token = refs[2 * n + len(after)], refs[2 * n + len(after) + 1], refs[-1]
        x, y, c = lax.axis_index("x"), lax.axis_index("y"), lax.axis_index("c")
        me = 2 * x + y
        for w in range(n):
            for k, (px, py) in enumerate(_chip_targets(x, y)):
                block = src[w].at[2 * px + py] if scatter else src[w]
                _remote(block, land[w].at[me], send.at[N_CHIPS * w + k], recv.at[N_CHIPS * w + k], (px, py, c)).start()
        token[...] = jnp.zeros_like(token)

    thru = tuple(pltpu.HBM(a.shape, a.dtype) for a in list(srcs) + lands)
    sems = (pltpu.SemaphoreType.DMA((N_CHIPS * n,)), pltpu.SemaphoreType.DMA((N_CHIPS * n,)))
    out = pl.pallas_call(
        body, name=name, out_shape=sems + thru + (S((SUBLANE, LANE), f32),),
        in_specs=[HBM_SPEC] * (2 * n) + [ANY_SPEC] * len(after),
        out_specs=(SEM_SPEC, SEM_SPEC) + (HBM_SPEC,) * (2 * n) + (VMEM_SPEC,),
        input_output_aliases={i: 2 + i for i in range(2 * n)},
        compiler_params=pltpu.CompilerParams(has_side_effects=pltpu.SideEffectType.DATAFLOW_SIDE_EFFECTING),
    )(*[_in_hbm(a) for a in list(srcs) + lands], *after)
    return out[0], out[1], out[2:2 + n], out[2 + n:2 + 2 * n], out[-1]


def _spread_wait(send, recv, srcs, lands, after, scatter, *, name):
    n = len(srcs)

    def body(*refs):
        src, land, send, recv = refs[:n], refs[n:2 * n], refs[2 * n], refs[2 * n + 1]
        x, y, c = lax.axis_index("x"), lax.axis_index("y"), lax.axis_index("c")
        me = 2 * x + y
        for w in range(n):
            for k, (px, py) in enumerate(_chip_targets(x, y)):
                block = src[w].at[me] if scatter else src[w]
                cp = _remote(block, land[w].at[2 * px + py], send.at[N_CHIPS * w + k], recv.at[N_CHIPS * w + k], (px, py, c))
                cp.wait_send()
                cp.wait_recv()

    thru = tuple(pltpu.HBM(a.shape, a.dtype) for a in list(srcs) + list(lands))
    out = pl.pallas_call(
        body, name=name, out_shape=thru,
        in_specs=[HBM_SPEC] * (2 * n) + [SEM_SPEC, SEM_SPEC] + [ANY_SPEC] * len(after), out_specs=(HBM_SPEC,) * (2 * n),
        input_output_aliases={i: i for i in range(2 * n)},
        compiler_params=pltpu.CompilerParams(has_side_effects=pltpu.SideEffectType.DATAFLOW_SIDE_EFFECTING),
    )(*srcs, *lands, send, recv, *after)
    return out[:n], out[n:]


def _swap_sibling(bufs, *, name):
    n = len(bufs)

    def body(*refs):
        src, out, send, recv = refs[:n], refs[n:2 * n], refs[-2], refs[-1]
        sib = (lax.axis_index("x"), lax.axis_index("y"), 1 - lax.axis_index("c"))
        copies = [_remote(src[w], out[w], send.at[w], recv.at[w], sib) for w in range(n)]
        for cp in copies:
            cp.start()
        for cp in copies:
            cp.wait()

    return pl.pallas_call(body, name=name, out_shape=tuple(S(a.shape, a.dtype) for a in bufs),
                          in_specs=[HBM_SPEC] * n, out_specs=(HBM_SPEC,) * n,
                          scratch_shapes=[pltpu.SemaphoreType.DMA((n,)), pltpu.SemaphoreType.DMA((n,))])(*bufs)


def _swap_start(bufs, *, name):
    n = len(bufs)
    lands = [lax.empty(b.shape, b.dtype) for b in bufs]

    def body(*refs):
        src, land, send, recv, token = refs[:n], refs[n:2 * n], refs[2 * n], refs[2 * n + 1], refs[-1]
        sib = (lax.axis_index("x"), lax.axis_index("y"), 1 - lax.axis_index("c"))
        for w in range(n):
            _remote(src[w], land[w], send.at[w], recv.at[w], sib).start()
        token[...] = jnp.zeros_like(token)

    thru = tuple(pltpu.HBM(a.shape, a.dtype) for a in list(bufs) + lands)
    out = pl.pallas_call(
        body, name=name,
        out_shape=(pltpu.SemaphoreType.DMA((n,)), pltpu.SemaphoreType.DMA((n,))) + thru + (S((SUBLANE, LANE), f32),),
        in_specs=[HBM_SPEC] * (2 * n), out_specs=(SEM_SPEC, SEM_SPEC) + (HBM_SPEC,) * (2 * n) + (VMEM_SPEC,),
        input_output_aliases={i: 2 + i for i in range(2 * n)},
        compiler_params=pltpu.CompilerParams(has_side_effects=pltpu.SideEffectType.DATAFLOW_SIDE_EFFECTING),
    )(*[_in_hbm(a) for a in list(bufs) + lands])
    return out[0], out[1], out[2:2 + n], out[2 + n:2 + 2 * n], out[-1]


def _swap_wait(send, recv, bufs, lands, after, *, name):
    n = len(bufs)

    def body(*refs):
        src, land, send, recv = refs[:n], refs[n:2 * n], refs[2 * n], refs[2 * n + 1]
        sib = (lax.axis_index("x"), lax.axis_index("y"), 1 - lax.axis_index("c"))
        for w in range(n):
            cp = _remote(src[w], land[w], send.at[w], recv.at[w], sib)
            cp.wait_send()
            cp.wait_recv()

    thru = tuple(pltpu.HBM(a.shape, a.dtype) for a in list(bufs) + list(lands))
    out = pl.pallas_call(
        body, name=name, out_shape=thru,
        in_specs=[HBM_SPEC] * (2 * n) + [SEM_SPEC, SEM_SPEC] + [ANY_SPEC] * len(after), out_specs=(HBM_SPEC,) * (2 * n),
        input_output_aliases={i: i for i in range(2 * n)},
        compiler_params=pltpu.CompilerParams(has_side_effects=pltpu.SideEffectType.DATAFLOW_SIDE_EFFECTING),
    )(*bufs, *lands, send, recv, *after)
    return out[:n], out[n:]


N_DEVICES = 8


def _allgather_devices(buf, *, name):
    R, C = buf.shape

    def body(in_ref, out_ref, send, recv, lsem):
        x, y, c = lax.axis_index("x"), lax.axis_index("y"), lax.axis_index("c")
        me = 4 * x + 2 * y + c
        mine = pltpu.make_async_copy(in_ref, out_ref.at[me], lsem)
        mine.start()
        flips = [(d >> 2 & 1, d >> 1 & 1, d & 1) for d in range(1, N_DEVICES)]
        peers = [(1 - x if fx else x, 1 - y if fy else y, 1 - c if fc else c) for fx, fy, fc in flips]
        sends = []
        for k, peer in enumerate(peers):
            cp = _remote(in_ref, out_ref.at[me], send.at[k], recv.at[k], peer)
            cp.start()
            sends.append(cp)
        for k, (px, py, pc) in enumerate(peers):
            _remote(in_ref, out_ref.at[4 * px + 2 * py + pc], send.at[k], recv.at[k], (px, py, pc)).wait_recv()
        for cp in sends:
            cp.wait_send()
        mine.wait()

    return pl.pallas_call(body, name=name, out_shape=S((N_DEVICES, R, C), buf.dtype), in_specs=[HBM_SPEC], out_specs=HBM_SPEC,
                          scratch_shapes=[pltpu.SemaphoreType.DMA((N_DEVICES - 1,)), pltpu.SemaphoreType.DMA((N_DEVICES - 1,)),
                                          pltpu.SemaphoreType.DMA(())])(buf)


def _sum_slots(buf, *, name):
    ns, R, C = buf.shape
    tr = _pick(R, 512)
    assert R % tr == 0

    def body(*refs):
        acc = refs[0][...]
        for r in refs[1:ns]:
            acc = acc + r[...]
        refs[ns][...] = acc

    specs = [pl.BlockSpec((None, tr, C), functools.partial(lambda s, i: (s, i, 0), s)) for s in range(ns)]
    return pl.pallas_call(body, name=name, out_shape=S((R, C), buf.dtype), grid=(R // tr,), in_specs=specs,
                          out_specs=pl.BlockSpec((tr, C), lambda i: (i, 0)), compiler_params=_cp("parallel"))(*([buf] * ns))


ADAMW_BLOCK_BYTES = 1 << 20


def _adamw(w, m, v, groups, *, name, layer=None, prev=None):
    shape = w.shape if layer is None else w.shape[1:]
    C = shape[-1]
    Rr = math.prod(shape[:-1])
    tr = Rr
    if Rr * C * 4 > ADAMW_BLOCK_BYTES:
        tr = max(t for t in range(2 * SUBLANE, Rr + 1, 2 * SUBLANE) if Rr % t == 0 and t * C * 4 <= ADAMW_BLOCK_BYTES)
    c1 = 1.0 / (1.0 - ADAM_B1 ** ADAM_STEP)
    c2 = 1.0 / (1.0 - ADAM_B2 ** ADAM_STEP)
    if layer is None:
        to2 = lambda t: t.reshape(Rr, C)
        spec = pl.BlockSpec((tr, C), lambda i: (i, 0))
        res_shape = S((Rr, C), f32)
    else:
        to2 = lambda t: t.reshape(layer[1], Rr, C)
        spec = pl.BlockSpec((None, tr, C), functools.partial(lambda l, i: (l, i, 0), layer[0]))
        res_shape = S((layer[1], Rr, C), f32)
    wspec, spec = spec, pl.BlockSpec((tr, C), lambda i: (i, 0))
    g_specs, g_args, sizes = [], [], []
    for grp in groups:
        sizes.append(len(grp))
        for term in grp:
            if isinstance(term, tuple):
                arr, slot = term
                g_specs.append(pl.BlockSpec((None, tr, C), functools.partial(lambda s, i: (s, i, 0), slot)))
                g_args.append(arr.reshape(arr.shape[0], Rr, C))
            else:
                g_specs.append(spec)
                g_args.append(term.reshape(Rr, C))
    nterms = len(g_args)
    prev = () if prev is None else tuple(to2(t) for t in prev)

    def body(w_ref, m_ref, v_ref, *rest):
        t_refs, (g_ref, d_ref, mo_ref, vo_ref) = rest[:nterms], rest[-4:]
        g, pos = None, 0
        for size in sizes:
            part = None
            for r in t_refs[pos:pos + size]:
                t = r[...].astype(f32)
                part = t if part is None else part + t
            pos += size
            g = part if g is None else g + part
        mn = ADAM_B1 * m_ref[...] + (1.0 - ADAM_B1) * g
        vn = ADAM_B2 * v_ref[...] + (1.0 - ADAM_B2) * (g * g)
        g_ref[...] = g
        mo_ref[...] = mn
        vo_ref[...] = vn
        d_ref[...] = -ADAM_LR * ((mn * c1) / (jnp.sqrt(vn * c2) + ADAM_EPS) + ADAM_WD * w_ref[...])

    out = pl.pallas_call(body, name=name, out_shape=(res_shape,) * 4, grid=(Rr // tr,),
                         in_specs=[wspec] * 3 + g_specs + [ANY_SPEC] * len(prev), out_specs=(wspec,) * 4,
                         input_output_aliases={3 + nterms + k: k for k in range(len(prev))},
                         compiler_params=_cp("parallel"))(to2(w), to2(m), to2(v), *g_args, *prev)
    return tuple(o.reshape(w.shape) for o in out)


def _pack_rows(parts, dtype):
    flat = jnp.concatenate([p.reshape(-1).astype(dtype) for p in parts])
    n = flat.shape[0]
    unit = PACK_COLS * 2 * SUBLANE
    padded = -(-n // unit) * unit
    return jnp.pad(flat, (0, padded - n)).reshape(padded // PACK_COLS, PACK_COLS)


def _unpack_rows(flat2d, shapes):
    flat = flat2d.reshape(-1)
    out, off = [], 0
    for shp in shapes:
        n = math.prod(shp)
        out.append(flat[off:off + n].reshape(shp))
        off += n
    return out


def _gather_weights(local, names, dtype, *, name):
    shapes = [local[n].shape for n in names]
    got = _all_gather_chips(_pack_rows([local[n] for n in names], dtype), name=name)
    per_chip = [_unpack_rows(got[s], shapes) for s in range(N_CHIPS)]
    return {n: jnp.concatenate([per_chip[s][k] for s in range(N_CHIPS)], axis=SHARD_AXIS[n]) for k, n in enumerate(names)}


def kernel(x, mem, norm_g, ssm_in_w, ssm_conv_w, ssm_conv_b, ssm_dt_bias, ssm_A_log, ssm_D, ssm_norm_g, ssm_out_w, cf_pw1_w, cf_pw1_b, cf_dw_w, cf_dw_b, cf_ln_g, cf_ln_b, cf_pw2_w, cf_pw2_b, xa_mem_g, xa_q_w, xa_kv_w, xa_o_w, ffn_in_w, ffn_conv_w, ffn_conv_b, ffn_out_w, loss_target, m_norm_g, m_ssm_in_w, m_ssm_conv_w, m_ssm_conv_b, m_ssm_dt_bias, m_ssm_A_log, m_ssm_D, m_ssm_norm_g, m_ssm_out_w, m_cf_pw1_w, m_cf_pw1_b, m_cf_dw_w, m_cf_dw_b, m_cf_ln_g, m_cf_ln_b, m_cf_pw2_w, m_cf_pw2_b, m_xa_mem_g, m_xa_q_w, m_xa_kv_w, m_xa_o_w, m_ffn_in_w, m_ffn_conv_w, m_ffn_conv_b, m_ffn_out_w, v_norm_g, v_ssm_in_w, v_ssm_conv_w, v_ssm_conv_b, v_ssm_dt_bias, v_ssm_A_log, v_ssm_D, v_ssm_norm_g, v_ssm_out_w, v_cf_pw1_w, v_cf_pw1_b, v_cf_dw_w, v_cf_dw_b, v_cf_ln_g, v_cf_ln_b, v_cf_pw2_w, v_cf_pw2_b, v_xa_mem_g, v_xa_q_w, v_xa_kv_w, v_xa_o_w, v_ffn_in_w, v_ffn_conv_w, v_ffn_conv_b, v_ffn_out_w):
    w_local = dict(zip(WEIGHT_NAMES, (norm_g, ssm_in_w, ssm_conv_w, ssm_conv_b, ssm_dt_bias, ssm_A_log, ssm_D, ssm_norm_g,
                                      ssm_out_w, cf_pw1_w, cf_pw1_b, cf_dw_w, cf_dw_b, cf_ln_g, cf_ln_b, cf_pw2_w, cf_pw2_b,
                                      xa_mem_g, xa_q_w, xa_kv_w, xa_o_w, ffn_in_w, ffn_conv_w, ffn_conv_b, ffn_out_w)))
    m_local = dict(zip(WEIGHT_NAMES, (m_norm_g, m_ssm_in_w, m_ssm_conv_w, m_ssm_conv_b, m_ssm_dt_bias, m_ssm_A_log, m_ssm_D,
                                      m_ssm_norm_g, m_ssm_out_w, m_cf_pw1_w, m_cf_pw1_b, m_cf_dw_w, m_cf_dw_b, m_cf_ln_g,
                                      m_cf_ln_b, m_cf_pw2_w, m_cf_pw2_b, m_xa_mem_g, m_xa_q_w, m_xa_kv_w, m_xa_o_w,
                                      m_ffn_in_w, m_ffn_conv_w, m_ffn_conv_b, m_ffn_out_w)))
    v_local = dict(zip(WEIGHT_NAMES, (v_norm_g, v_ssm_in_w, v_ssm_conv_w, v_ssm_conv_b, v_ssm_dt_bias, v_ssm_A_log, v_ssm_D,
                                      v_ssm_norm_g, v_ssm_out_w, v_cf_pw1_w, v_cf_pw1_b, v_cf_dw_w, v_cf_dw_b, v_cf_ln_g,
                                      v_cf_ln_b, v_cf_pw2_w, v_cf_pw2_b, v_xa_mem_g, v_xa_q_w, v_xa_kv_w, v_xa_o_w,
                                      v_ffn_in_w, v_ffn_conv_w, v_ffn_conv_b, v_ffn_out_w)))

    small = [n for n in WEIGHT_NAMES if n not in MATMUL_WEIGHTS]
    small_sharded = [n for n in small if SHARD_AXIS[n] is not None]
    W = {n: w_local[n] for n in small if SHARD_AXIS[n] is None}
    W.update(_gather_weights(w_local, small_sharded, f32, name="gather_small_weights"))

    def layer_index(n, i):
        return i // 2 if n in MIXER_WEIGHTS else i

    def keys_of(i, parts):
        return [(n, layer_index(n, i)) for part in parts for n in _layer_matmul_weights(i, part)]

    def shards(keys):
        return [w_local[n][l].astype(bf16) for n, l in keys]

    def usable(n, a):
        return a.reshape(N_CHIPS * a.shape[1], a.shape[2]) if n in ROW_SHARDED else a

    mixer0 = keys_of(0, ("mixer",))
    got0 = _gather_matmul_weights([s.reshape(1, 2, s.shape[0] // 2, s.shape[1]) for s in shards(mixer0)],
                                  name="gather_layer0_mixer")
    gather_groups = {(0, "rest"): keys_of(0, ("rest",))}
    gather_groups.update({(i, "mixer"): keys_of(i, ("mixer", "rest")) for i in range(1, DEPTH)})
    gathers, tokens, landed = {}, [], {}
    for gkey in sorted(gather_groups):
        send, recv, srcs, lands, token = _spread_start(shards(gather_groups[gkey]), False, name="gather_start_%d_%s" % gkey,
                                                       after=(got0[0], W[small_sharded[0]]))
        gathers[gkey] = (send, recv, srcs, lands)
        tokens.append(token)

    def layer_weights(i, part, after):
        if (i, part) == (0, "mixer"):
            landed.update({k: g.reshape((N_CHIPS, 2 * g.shape[3], g.shape[4])) for k, g in zip(mixer0, got0)})
        elif (i, part) in gathers:
            _, lands = _spread_wait(*gathers[i, part], (after,), False, name="gather_wait_%d_%s" % (i, part))
            landed.update(zip(gather_groups[i, part], lands))
        return {n: usable(n, landed[n, layer_index(n, i)]) for n in _layer_matmul_weights(i, part)}

    pending, scatters, swaps, own, sib = {}, {}, {}, {}, {}

    def scatter_start(gkey, keys, gl):
        send, recv, srcs, lands, token = _spread_start([gl[k] for k in keys], True, name="grads_start_%d_%s" % gkey)
        scatters[gkey] = (keys, send, recv, srcs, lands)
        return token

    def layer_grads(i, part, gl):
        grads = {(n, layer_index(n, i)): gl[n] for n in _layer_matmul_weights(i, part)}
        behind = []
        if part == "rest":
            if i + 1 < DEPTH:
                keys, send, recv, srcs, lands = scatters.pop((i + 1, "mixer"))
                _, lands = _spread_wait(send, recv, srcs, lands, (grads['xa_kv_w', i],), True,
                                        name="grads_wait_%d" % (i + 1))
                send, recv, srcs, lands, token = _swap_start(lands, name="grads_swap_start_%d" % (i + 1))
                swaps[i + 1] = (keys, send, recv, srcs, lands)
                behind.append(token)
            if i == 0:
                behind.append(scatter_start((0, "rest"), list(grads), grads))
            else:
                pending.update(grads)
        else:
            pending.update(grads)
            behind.append(scatter_start((i, "mixer"), list(pending), dict(pending)))
            pending.clear()
        return behind

    sse, gx, gsmall = _device_step(x[0], mem[0], loss_target[0], W, layer_weights, layer_grads, tuple(tokens))

    loss = lax.psum(0.5 * sse[0, 0] / D_MODEL, ("x", "y", "c"))

    last_keys, last_lands = [], []
    for gkey in sorted(scatters):
        keys, send, recv, srcs, lands = scatters[gkey]
        _, lands = _spread_wait(send, recv, srcs, lands, (gx,), True, name="grads_wait_%d_%s" % gkey)
        last_keys += keys
        last_lands += list(lands)
    own.update(zip(last_keys, last_lands))
    sib.update(zip(last_keys, _swap_sibling(last_lands, name="grads_swap_last")))
    for i in sorted(swaps):
        keys, send, recv, srcs, lands = swaps[i]
        mine, theirs = _swap_wait(send, recv, srcs, lands, (gx,), name="grads_swap_wait_%d" % i)
        own.update(zip(keys, mine))
        sib.update(zip(keys, theirs))
    small_shapes = [gsmall[n].shape for n in small]
    slots = _allgather_devices(_pack_rows([gsmall[n] for n in small], f32), name="allgather_small_grads")
    gsum = dict(zip(small, _unpack_rows(_sum_slots(slots, name="sum_small_grads"), small_shapes)))
    chip = 2 * lax.axis_index("x") + lax.axis_index("y")

    res = {}
    for n in MATMUL_WEIGHTS:
        layers, out = w_local[n].shape[0], None
        for l in range(layers):
            groups = [[(own[n, l], s) for s in range(N_CHIPS)], [(sib[n, l], s) for s in range(N_CHIPS)]]
            out = _adamw(w_local[n], m_local[n], v_local[n], groups, name="adamw_%s_%d" % (n, l), layer=(l, layers), prev=out)
        res[n] = out
    for n in small:
        g, ax = gsum[n], SHARD_AXIS[n]
        if ax is not None:
            width = w_local[n].shape[ax]
            g = lax.dynamic_slice_in_dim(g, chip * width, width, axis=ax)
        res[n] = _adamw(w_local[n], m_local[n], v_local[n], [[g]], name="adamw_" + n)
    return (loss, gx[None], *[res[n][0] for n in WEIGHT_NAMES], *[res[n][1] for n in WEIGHT_NAMES],
            *[res[n][2] for n in WEIGHT_NAMES], *[res[n][3] for n in WEIGHT_NAMES])
```

```python
import functools
import math

import jax
import jax.numpy as jnp
from jax import lax
from jax.experimental import pallas as pl
from jax.experimental.pallas import tpu as pltpu

f32 = jnp.float32
bf16 = jnp.bfloat16
S = jax.ShapeDtypeStruct

D_MODEL = 1024
DEPTH = 4
D_INNER = 2048
HEAD_DIM = 64
N_GROUPS = 4
HEADS_PER_GROUP = 8
N_SSM_HEADS = 32
D_STATE = 128
CHUNK = 128
SSM_CONV = 4
CONV_DIM = 3072
CF_KERNEL = 31
N_MEM = 256
XA_HEADS = 4
XA_HEAD_DIM = 256
D_FF = 2816
FFN_CONV = 3
EPS = 1e-6
ADAM_LR, ADAM_B1, ADAM_B2, ADAM_EPS, ADAM_WD, ADAM_STEP = 0.001, 0.9, 0.999, 1e-08, 0.01, 10

LANE = 128
SUBLANE = 8
ROW_SUB = 64
VMEM_LIMIT = 56 * 1024 * 1024
N_CHIPS = 4
PACK_COLS = 1024

WEIGHT_NAMES = ['norm_g', 'ssm_in_w', 'ssm_conv_w', 'ssm_conv_b', 'ssm_dt_bias', 'ssm_A_log', 'ssm_D', 'ssm_norm_g',
                'ssm_out_w', 'cf_pw1_w', 'cf_pw1_b', 'cf_dw_w', 'cf_dw_b', 'cf_ln_g', 'cf_ln_b', 'cf_pw2_w', 'cf_pw2_b',
                'xa_mem_g', 'xa_q_w', 'xa_kv_w', 'xa_o_w', 'ffn_in_w', 'ffn_conv_w', 'ffn_conv_b', 'ffn_out_w']
SHARD_AXIS = {'norm_g': 2, 'ssm_in_w': 2, 'ssm_conv_w': 2, 'ssm_conv_b': None, 'ssm_dt_bias': None, 'ssm_A_log': None,
              'ssm_D': None, 'ssm_norm_g': None, 'ssm_out_w': 1, 'cf_pw1_w': 2, 'cf_pw1_b': 1, 'cf_dw_w': 2, 'cf_dw_b': 1,
              'cf_ln_g': 1, 'cf_ln_b': 1, 'cf_pw2_w': 1, 'cf_pw2_b': 1, 'xa_mem_g': None, 'xa_q_w': 1, 'xa_kv_w': 2,
              'xa_o_w': 1, 'ffn_in_w': 2, 'ffn_conv_w': 2, 'ffn_conv_b': None, 'ffn_out_w': 1}
MATMUL_WEIGHTS = ('ssm_in_w', 'ssm_out_w', 'cf_pw1_w', 'cf_pw2_w', 'xa_q_w', 'xa_kv_w', 'xa_o_w', 'ffn_in_w', 'ffn_out_w')


def _cp(*sem):
    return pltpu.CompilerParams(dimension_semantics=tuple(sem), vmem_limit_bytes=VMEM_LIMIT)


def _pick(dim, pref):
    if dim <= pref:
        return dim
    best = None
    for t in range(LANE, pref + 1, LANE):
        if dim % t == 0:
            best = t
    assert best is not None, (dim, pref)
    return best


def _sigmoid(x):
    return 1.0 / (1.0 + jnp.exp(-x))


def _silu(x):
    return x * _sigmoid(x)


def _dsilu(x):
    s = _sigmoid(x)
    return s * (1.0 + x * (1.0 - s))


def _softplus(x):
    return jnp.maximum(x, 0.0) + jnp.log(1.0 + jnp.exp(-jnp.abs(x)))


_DN = {"nn": (((1,), (0,)), ((), ())), "nt": (((1,), (1,)), ((), ())), "tn": (((0,), (0,)), ((), ()))}


def _mm(a, b, mode, *, name, out_dtype=f32, bias=None, add=None, b_shards=False, out_shards=False, after=()):
    bshape = (b.shape[1], b.shape[2] * N_CHIPS) if b_shards else b.shape
    if mode == "nn":
        (M, K), (K2, N) = a.shape, bshape
    elif mode == "nt":
        (M, K), (N, K2) = a.shape, bshape
    else:
        (K, M), (K2, N) = a.shape, bshape
    assert K == K2, (a.shape, b.shape, mode)
    n_unit = N // N_CHIPS if ((b_shards and mode == "nn") or out_shards) else N
    k_unit = K // N_CHIPS if (b_shards and mode == "nt") else K
    tm, tn, tk = _pick(M, 1024), _pick(n_unit, 1408), _pick(k_unit, 1408)
    nk, nj_u, nk_u = K // tk, n_unit // tn, k_unit // tk
    a_spec = {"nn": pl.BlockSpec((tm, tk), lambda i, j, k: (i, k)), "nt": pl.BlockSpec((tm, tk), lambda i, j, k: (i, k)),
              "tn": pl.BlockSpec((tk, tm), lambda i, j, k: (k, i))}[mode]
    if not b_shards:
        b_spec = {"nn": pl.BlockSpec((tk, tn), lambda i, j, k: (k, j)), "nt": pl.BlockSpec((tn, tk), lambda i, j, k: (j, k)),
                  "tn": pl.BlockSpec((tk, tn), lambda i, j, k: (k, j))}[mode]
    else:
        b_spec = {"nn": pl.BlockSpec((None, tk, tn), lambda i, j, k: (j // nj_u, k, j % nj_u)),
                  "nt": pl.BlockSpec((None, tn, tk), lambda i, j, k: (k // nk_u, j, k % nk_u))}[mode]
    in_specs, args = [a_spec, b_spec], [a, b]
    if bias is not None:
        in_specs.append(pl.BlockSpec((1, tn), lambda i, j, k: (0, j)))
        args.append(bias)
    if add is not None:
        in_specs.append(pl.BlockSpec((tm, tn), lambda i, j, k: (i, j)))
        args.append(add)
    in_specs += [pl.BlockSpec(memory_space=pl.ANY)] * len(after)
    args += list(after)
    if not out_shards:
        out_shape, out_spec = S((M, N), out_dtype), pl.BlockSpec((tm, tn), lambda i, j, k: (i, j))
    else:
        out_shape = S((N_CHIPS, M, n_unit), out_dtype)
        out_spec = pl.BlockSpec((None, tm, tn), lambda i, j, k: (j // nj_u, i, j % nj_u))
    dn = _DN[mode]
    has_bias, has_add = bias is not None, add is not None

    def body(a_ref, b_ref, *rest):
        rest = list(rest)
        bias_ref = rest.pop(0) if has_bias else None
        add_ref = rest.pop(0) if has_add else None
        rest = rest[len(after):]
        o_ref = rest[0]

        def finish(r):
            if has_bias:
                r = r + bias_ref[...]
            if has_add:
                r = r + add_ref[...].astype(f32)
            o_ref[...] = r.astype(out_dtype)

        part = lax.dot_general(a_ref[...].astype(bf16), b_ref[...].astype(bf16), dn, preferred_element_type=f32)
        if nk == 1:
            finish(part)
            return
        acc_ref = rest[1]
        k = pl.program_id(2)

        @pl.when(k == 0)
        def _():
            acc_ref[...] = part

        @pl.when(k > 0)
        def _():
            acc_ref[...] += part

        @pl.when(k == nk - 1)
        def _():
            finish(acc_ref[...])

    return pl.pallas_call(
        body, name=name, out_shape=out_shape, grid=(M // tm, N // tn, nk),
        in_specs=in_specs, out_specs=out_spec, scratch_shapes=[pltpu.VMEM((tm, tn), f32)] if nk > 1 else [],
        compiler_params=_cp("parallel", "parallel", "arbitrary"))(*args)


def _rows(tm, C):
    return pl.BlockSpec((tm, C), lambda i: (i, 0))


def _const(shape):
    return pl.BlockSpec(shape, lambda i: tuple(0 for _ in shape))


def _rms_val(x, g):
    r = lax.rsqrt(jnp.mean(x * x, axis=-1, keepdims=True) + EPS)
    return x * r * g


def _rms_bwd_val(x, g, dy):
    r = lax.rsqrt(jnp.mean(x * x, axis=-1, keepdims=True) + EPS)
    xn = x * r
    dxh = dy * g
    dx = r * (dxh - xn * jnp.mean(dxh * xn, axis=-1, keepdims=True))
    return dx, jnp.sum(dy * xn, axis=0, keepdims=True)


ANY_SPEC = pl.BlockSpec(memory_space=pl.ANY)


def _rmsnorm_fwd(x, g, *, name, after=()):
    L, C = x.shape
    tm = _pick(L, 512)

    def body(x_ref, g_ref, *rest):
        rest[-1][...] = _rms_val(x_ref[...], g_ref[...]).astype(bf16)

    return pl.pallas_call(body, name=name, out_shape=S((L, C), bf16), grid=(L // tm,),
                          in_specs=[_rows(tm, C), _const((1, C))] + [ANY_SPEC] * len(after), out_specs=_rows(tm, C),
                          compiler_params=_cp("parallel"))(x, g, *after)


def _resid_norm_fwd(x, mix, g_post, g_next, *, name):
    L, C = x.shape
    tm = _pick(L, 512)
    want_h = g_next is not None

    def body(x_ref, m_ref, gp_ref, *rest):
        xn = x_ref[...] + _rms_val(m_ref[...], gp_ref[...])
        if want_h:
            gn_ref, xo_ref, h_ref = rest
            h_ref[...] = _rms_val(xn, gn_ref[...]).astype(bf16)
        else:
            (xo_ref,) = rest
        xo_ref[...] = xn

    in_specs = [_rows(tm, C), _rows(tm, C), _const((1, C))]
    args = [x, mix, g_post]
    out_shape, out_specs = [S((L, C), f32)], [_rows(tm, C)]
    if want_h:
        in_specs.append(_const((1, C)))
        args.append(g_next)
        out_shape.append(S((L, C), bf16))
        out_specs.append(_rows(tm, C))
    out = pl.pallas_call(body, name=name, out_shape=tuple(out_shape), grid=(L // tm,), in_specs=in_specs,
                         out_specs=tuple(out_specs), compiler_params=_cp("parallel"))(*args)
    return (out[0], out[1]) if want_h else (out[0], None)


def _norm_bwd(x, g, dy, *, name, add=None, out_dtype=f32, after=()):
    L, C = x.shape
    tm = _pick(L, 512)
    has_add = add is not None

    def body(x_ref, g_ref, dy_ref, *rest):
        rest = list(rest)
        add_ref = rest.pop(0) if has_add else None
        dx_ref, dg_ref, cs_ref = rest[-3:]
        i = pl.program_id(0)

        @pl.when(i == 0)
        def _():
            dg_ref[...] = jnp.zeros_like(dg_ref)
            cs_ref[...] = jnp.zeros_like(cs_ref)

        dx, dg = _rms_bwd_val(x_ref[...], g_ref[...], dy_ref[...].astype(f32))
        dg_ref[...] += dg
        cs_ref[...] += jnp.sum(dx, axis=0, keepdims=True)
        if has_add:
            dx = dx + add_ref[...]
        dx_ref[...] = dx.astype(out_dtype)

    in_specs = [_rows(tm, C), _const((1, C)), _rows(tm, C)]
    args = [x, g, dy]
    if has_add:
        in_specs.append(_rows(tm, C))
        args.append(add)
    in_specs += [ANY_SPEC] * len(after)
    args += list(after)
    return pl.pallas_call(body, name=name, out_shape=(S((L, C), out_dtype), S((1, C), f32), S((1, C), f32)),
                          grid=(L // tm,), in_specs=in_specs,
                          out_specs=(_rows(tm, C), _const((1, C)), _const((1, C))),
                          compiler_params=_cp("arbitrary"))(*args)


def _loss_fwd_bwd(y, target, *, name):
    L, C = y.shape
    tm = _pick(L, 512)

    def body(y_ref, t_ref, acc_ref, dy_ref):
        i = pl.program_id(0)

        @pl.when(i == 0)
        def _():
            acc_ref[...] = jnp.zeros_like(acc_ref)

        e = y_ref[...] - t_ref[...]
        rs = jnp.sum(e * e, axis=-1, keepdims=True)
        acc_ref[...] += jnp.broadcast_to(jnp.sum(rs, axis=0, keepdims=True), (1, LANE))
        dy_ref[...] = e * (1.0 / C)

    return pl.pallas_call(body, name=name, out_shape=(S((1, LANE), f32), S((L, C), f32)), grid=(L // tm,),
                          in_specs=[_rows(tm, C), _rows(tm, C)], out_specs=(_const((1, LANE)), _rows(tm, C)),
                          compiler_params=_cp("arbitrary"))(y, target)


def _halo_rows(K):
    return SUBLANE if K - 1 <= SUBLANE else 32


def _prev_halo_spec(tm, H, C):
    return pl.BlockSpec((H, C), lambda i: (jnp.maximum(i * (tm // H) - 1, 0), 0))


def _next_halo_spec(tm, H, C, L):
    return pl.BlockSpec((H, C), lambda i: (jnp.minimum((i + 1) * (tm // H), L // H - 1), 0))


def _down_views(ext, H, n, K):
    bases, views = {}, []
    for s in range(K):
        q, r = divmod(s, SUBLANE)
        if r not in bases:
            bases[r] = ext if r == 0 else pltpu.roll(ext, r, axis=0)
        views.append(bases[r][H - SUBLANE * q:H - SUBLANE * q + n])
    return views


def _up_views(ext, n, K):
    bases, views = {}, []
    for s in range(K):
        q, r = divmod(s, SUBLANE)
        if r not in bases:
            bases[r] = ext if r == 0 else pltpu.roll(ext, ext.shape[0] - r, axis=0)
        views.append(bases[r][SUBLANE * q:SUBLANE * q + n])
    return views


def _causal_conv(ext, H, w_ref, K):
    views = _down_views(ext, H, ext.shape[0] - H, K)
    acc = None
    for k in range(K):
        term = views[K - 1 - k] * w_ref[k:k + 1, :]
        acc = term if acc is None else acc + term
    return acc


def _anticausal_conv(ext, tm, w_ref, K):
    views = _up_views(ext, tm, K)
    acc = None
    for k in range(K):
        term = views[K - 1 - k] * w_ref[k:k + 1, :]
        acc = term if acc is None else acc + term
    return acc


def _tap_grads(dw_ref, d_cur, x_ext, H, K):
    views = _down_views(x_ext, H, d_cur.shape[0], K)
    for k in range(K):
        dw_ref[k:k + 1, :] += jnp.sum(d_cur * views[K - 1 - k], axis=0, keepdims=True)


def _pad_taps(w, K):
    return jnp.pad(w, ((0, _halo_rows(K) - K), (0, 0)))


def _conv_act_fwd(x, w, b, *, K, act, name, out_dtype, tm_pref=256):
    L, C = x.shape
    H = _halo_rows(K)
    tm = _pick(L, tm_pref)
    Co = C if act == "silu" else C // 2

    rs = min(ROW_SUB, tm)

    def body(h_ref, x_ref, w_ref, b_ref, o_ref, ext_scr):
        i = pl.program_id(0)
        ext_scr[0:H] = jnp.where(i > 0, h_ref[...], 0.0)
        ext_scr[H:] = x_ref[...]
        for j in range(Co // LANE):
            lanes = [j] if act == "silu" else [j, j + Co // LANE]
            wb = [(w_ref[:, c * LANE:(c + 1) * LANE], b_ref[:, c * LANE:(c + 1) * LANE]) for c in lanes]
            for r0 in range(0, tm, rs):
                us = [_causal_conv(ext_scr[r0:r0 + rs + H, c * LANE:(c + 1) * LANE], H, wc, K) + bc
                      for c, (wc, bc) in zip(lanes, wb)]
                y = _silu(us[0]) if act == "silu" else _silu(us[0]) * us[1]
                o_ref[r0:r0 + rs, j * LANE:(j + 1) * LANE] = y.astype(out_dtype)

    return pl.pallas_call(body, name=name, out_shape=S((L, Co), out_dtype), grid=(L // tm,),
                          in_specs=[_prev_halo_spec(tm, H, C), _rows(tm, C), _const((H, C)), _const((1, C))],
                          out_specs=_rows(tm, Co), scratch_shapes=[pltpu.VMEM((H + tm, C), f32)],
                          compiler_params=_cp("parallel"))(x, x, w, b)


def _conv_act_bwd(x, dparts, w, b, *, K, act, name, tm_pref=256):
    L, C = x.shape
    H = _halo_rows(K)
    tm = _pick(L, tm_pref)
    nb = L // tm
    Co = C if act == "silu" else C // 2
    nparts = len(dparts)

    rs = min(ROW_SUB, tm)

    def body(hp_ref, x_ref, hn_ref, w_ref, b_ref, *rest):
        d_refs, dn_refs = rest[:nparts], rest[nparts:2 * nparts]
        dx_ref, dw_ref, db_ref, ext_scr, d_scr = rest[2 * nparts:]
        i = pl.program_id(0)

        @pl.when(i == 0)
        def _():
            dw_ref[...] = jnp.zeros_like(dw_ref)
            db_ref[...] = jnp.zeros_like(db_ref)

        ext_scr[0:H] = jnp.where(i > 0, hp_ref[...], 0.0)
        ext_scr[H:H + tm] = x_ref[...]
        ext_scr[H + tm:] = jnp.where(i < nb - 1, hn_ref[...], 0.0)
        off = 0
        for r, rn in zip(d_refs, dn_refs):
            d_scr[0:tm, off:off + r.shape[1]] = r[...].astype(f32)
            d_scr[tm:, off:off + r.shape[1]] = jnp.where(i < nb - 1, rn[...].astype(f32), 0.0)
            off += r.shape[1]
        for j in range(Co // LANE):
            lanes = [j] if act == "silu" else [j, j + Co // LANE]
            wb = [(w_ref[:, c * LANE:(c + 1) * LANE], b_ref[:, c * LANE:(c + 1) * LANE]) for c in lanes]
            db_acc = [jnp.zeros((1, LANE), f32) for _ in lanes]
            dw_acc = [[jnp.zeros((1, LANE), f32) for _ in range(K)] for _ in lanes]
            for r0 in range(0, tm, rs):
                exts = [ext_scr[r0:r0 + rs + 2 * H, c * LANE:(c + 1) * LANE] for c in lanes]
                us = [_causal_conv(e, H, wc, K) + bc for e, (wc, bc) in zip(exts, wb)]
                d = d_scr[r0:r0 + rs + H, j * LANE:(j + 1) * LANE]
                dus = [d * _dsilu(us[0])] if act == "silu" else [d * us[1] * _dsilu(us[0]), d * _silu(us[0])]
                for q, (c, e, du, (wc, _)) in enumerate(zip(lanes, exts, dus, wb)):
                    dx_ref[r0:r0 + rs, c * LANE:(c + 1) * LANE] = _anticausal_conv(du, rs, wc, K).astype(bf16)
                    du_cur = du[:rs]
                    db_acc[q] = db_acc[q] + jnp.sum(du_cur, axis=0, keepdims=True)
                    xv = _down_views(e[:H + rs], H, rs, K)
                    for k in range(K):
                        dw_acc[q][k] = dw_acc[q][k] + jnp.sum(du_cur * xv[K - 1 - k], axis=0, keepdims=True)
            for q, c in enumerate(lanes):
                db_ref[:, c * LANE:(c + 1) * LANE] += db_acc[q]
                for k in range(K):
                    dw_ref[k:k + 1, c * LANE:(c + 1) * LANE] += dw_acc[q][k]

    in_specs = [_prev_halo_spec(tm, H, C), _rows(tm, C), _next_halo_spec(tm, H, C, L), _const((H, C)), _const((1, C))]
    in_specs += [_rows(tm, p.shape[1]) for p in dparts] + [_next_halo_spec(tm, H, p.shape[1], L) for p in dparts]
    return pl.pallas_call(body, name=name, out_shape=(S((L, C), bf16), S((H, C), f32), S((1, C), f32)), grid=(nb,),
                          in_specs=in_specs, out_specs=(_rows(tm, C), _const((H, C)), _const((1, C))),
                          scratch_shapes=[pltpu.VMEM((tm + 2 * H, C), f32), pltpu.VMEM((tm + H, Co), f32)],
                          compiler_params=_cp("arbitrary"))(x, x, x, w, b, *dparts, *dparts)


def _cf_fwd(u, dw_w, dw_b, ln_g, ln_b, *, name):
    L, C2 = u.shape
    C = C2 // 2
    K, H = CF_KERNEL, _halo_rows(CF_KERNEL)
    tm = _pick(L, 256)

    rs = min(ROW_SUB, tm)
    nl = C // LANE

    def body(h_ref, u_ref, w_ref, b_ref, g_ref, lb_ref, c_ref, s_ref, u_scr):
        i = pl.program_id(0)
        u_scr[0:H] = jnp.where(i > 0, h_ref[...], 0.0)
        u_scr[H:] = u_ref[...]
        for j in range(nl):
            wj, bj = w_ref[:, j * LANE:(j + 1) * LANE], b_ref[:, j * LANE:(j + 1) * LANE]
            for r0 in range(0, tm, rs):
                glu = u_scr[r0:r0 + rs + H, j * LANE:(j + 1) * LANE] \
                    * _sigmoid(u_scr[r0:r0 + rs + H, (nl + j) * LANE:(nl + j + 1) * LANE])
                c_ref[r0:r0 + rs, j * LANE:(j + 1) * LANE] = _causal_conv(glu, H, wj, K) + bj
        c = c_ref[...]
        mu = jnp.mean(c, axis=-1, keepdims=True)
        xc = c - mu
        var = jnp.mean(xc * xc, axis=-1, keepdims=True)
        ln = xc * lax.rsqrt(var + EPS) * g_ref[...] + lb_ref[...]
        s_ref[...] = _silu(ln).astype(bf16)

    return pl.pallas_call(body, name=name, out_shape=(S((L, C), f32), S((L, C), bf16)), grid=(L // tm,),
                          in_specs=[_prev_halo_spec(tm, H, C2), _rows(tm, C2), _const((H, C)), _const((1, C)),
                                    _const((1, C)), _const((1, C))],
                          out_specs=(_rows(tm, C), _rows(tm, C)), scratch_shapes=[pltpu.VMEM((H + tm, C2), f32)],
                          compiler_params=_cp("parallel"))(u, u, dw_w, dw_b, ln_g, ln_b)


def _cf_ln_bwd(c, ln_g, ln_b, ds, *, name):
    L, C = c.shape
    tm = _pick(L, 512)

    def body(c_ref, g_ref, lb_ref, ds_ref, dc_ref, dg_ref, db_ref):
        i = pl.program_id(0)

        @pl.when(i == 0)
        def _():
            dg_ref[...] = jnp.zeros_like(dg_ref)
            db_ref[...] = jnp.zeros_like(db_ref)

        c = c_ref[...]
        mu = jnp.mean(c, axis=-1, keepdims=True)
        xc = c - mu
        r = lax.rsqrt(jnp.mean(xc * xc, axis=-1, keepdims=True) + EPS)
        xh = xc * r
        ln = xh * g_ref[...] + lb_ref[...]
        dln = ds_ref[...].astype(f32) * _dsilu(ln)
        dg_ref[...] += jnp.sum(dln * xh, axis=0, keepdims=True)
        db_ref[...] += jnp.sum(dln, axis=0, keepdims=True)
        dxh = dln * g_ref[...]
        dc_ref[...] = r * (dxh - jnp.mean(dxh, axis=-1, keepdims=True) - xh * jnp.mean(dxh * xh, axis=-1, keepdims=True))

    return pl.pallas_call(body, name=name, out_shape=(S((L, C), f32), S((1, C), f32), S((1, C), f32)), grid=(L // tm,),
                          in_specs=[_rows(tm, C), _const((1, C)), _const((1, C)), _rows(tm, C)],
                          out_specs=(_rows(tm, C), _const((1, C)), _const((1, C))),
                          compiler_params=_cp("arbitrary"))(c, ln_g, ln_b, ds)


def _cf_glu_bwd(u, dc, dw_w, *, name):
    L, C2 = u.shape
    C = C2 // 2
    K, H = CF_KERNEL, _halo_rows(CF_KERNEL)
    tm = _pick(L, 256)
    nb = L // tm

    rs = min(ROW_SUB, tm)
    nl = C // LANE

    def fold(x):
        acc = x[0:SUBLANE]
        for r in range(SUBLANE, x.shape[0], SUBLANE):
            acc = acc + x[r:r + SUBLANE]
        return acc

    def body(uh_ref, u_ref, dc_ref, dch_ref, w_ref, du_ref, dw_ref, db_ref, dus_ref, u_scr, dc_scr):
        i = pl.program_id(0)

        @pl.when(i == 0)
        def _():
            dw_ref[...] = jnp.zeros_like(dw_ref)
            db_ref[...] = jnp.zeros_like(db_ref)
            dus_ref[...] = jnp.zeros_like(dus_ref)

        u_scr[0:H] = jnp.where(i > 0, uh_ref[...], 0.0)
        u_scr[H:] = u_ref[...]
        dc_scr[0:tm] = dc_ref[...]
        dc_scr[tm:] = jnp.where(i < nb - 1, dch_ref[...], 0.0)
        for j in range(nl):
            la, lg = slice(j * LANE, (j + 1) * LANE), slice((nl + j) * LANE, (nl + j + 1) * LANE)
            wj = w_ref[:, la]
            zero8 = jnp.zeros((SUBLANE, LANE), f32)
            dw_acc, db_acc, dua_acc, dug_acc = [zero8] * K, zero8, zero8, zero8
            for r0 in range(0, tm, rs):
                a_e = u_scr[r0:r0 + rs + H, la]
                sg = _sigmoid(u_scr[r0:r0 + rs + H, lg])
                dce = dc_scr[r0:r0 + rs + H, la]
                dglu = _anticausal_conv(dce, rs, wj, K)
                a_c, sg_c, dc_c = a_e[H:], sg[H:], dce[:rs]
                du_a = dglu * sg_c
                du_g = dglu * a_c * sg_c * (1.0 - sg_c)
                du_ref[r0:r0 + rs, la] = du_a.astype(bf16)
                du_ref[r0:r0 + rs, lg] = du_g.astype(bf16)
                dua_acc, dug_acc, db_acc = dua_acc + fold(du_a), dug_acc + fold(du_g), db_acc + fold(dc_c)
                views = _down_views(a_e * sg, H, rs, K)
                dw_acc = [dw_acc[k] + fold(dc_c * views[K - 1 - k]) for k in range(K)]
            dus_ref[:, la] += jnp.sum(dua_acc, axis=0, keepdims=True)
            dus_ref[:, lg] += jnp.sum(dug_acc, axis=0, keepdims=True)
            db_ref[:, la] += jnp.sum(db_acc, axis=0, keepdims=True)
            for k in range(K):
                dw_ref[k:k + 1, la] += jnp.sum(dw_acc[k], axis=0, keepdims=True)

    return pl.pallas_call(body, name=name,
                          out_shape=(S((L, C2), bf16), S((H, C), f32), S((1, C), f32), S((1, C2), f32)), grid=(nb,),
                          in_specs=[_prev_halo_spec(tm, H, C2), _rows(tm, C2), _rows(tm, C), _next_halo_spec(tm, H, C, L),
                                    _const((H, C))],
                          out_specs=(_rows(tm, C2), _const((H, C)), _const((1, C)), _const((1, C2))),
                          scratch_shapes=[pltpu.VMEM((H + tm, C2), f32), pltpu.VMEM((tm + H, C), f32)],
                          compiler_params=_cp("arbitrary"))(u, u, dc, dc, dw_w)


def _gated_norm_fwd(y, z, g, *, name):
    L, C = y.shape
    tm = _pick(L, 256)

    def body(y_ref, z_ref, g_ref, o_ref):
        o_ref[...] = _rms_val(y_ref[...] * _silu(z_ref[...]), g_ref[...]).astype(bf16)

    return pl.pallas_call(body, name=name, out_shape=S((L, C), bf16), grid=(L // tm,),
                          in_specs=[_rows(tm, C), _rows(tm, C), _const((1, C))], out_specs=_rows(tm, C),
                          compiler_params=_cp("parallel"))(y, z, g)


def _gated_norm_bwd(y, z, g, dyn, *, name):
    L, C = y.shape
    tm = _pick(L, 256)

    def body(y_ref, z_ref, g_ref, d_ref, dy_ref, dz_ref, dg_ref):
        i = pl.program_id(0)

        @pl.when(i == 0)
        def _():
            dg_ref[...] = jnp.zeros_like(dg_ref)

        y, z = y_ref[...], z_ref[...]
        sz = _silu(z)
        du, dg = _rms_bwd_val(y * sz, g_ref[...], d_ref[...].astype(f32))
        dg_ref[...] += dg
        dy_ref[...] = du * sz
        dz_ref[...] = (du * y * _dsilu(z)).astype(bf16)

    return pl.pallas_call(body, name=name, out_shape=(S((L, C), f32), S((L, C), bf16), S((1, C), f32)), grid=(L // tm,),
                          in_specs=[_rows(tm, C), _rows(tm, C), _const((1, C)), _rows(tm, C)],
                          out_specs=(_rows(tm, C), _rows(tm, C), _const((1, C))),
                          compiler_params=_cp("arbitrary"))(y, z, g, dyn)


_XA_SCALE = XA_HEAD_DIM ** -0.5


def _attn_fwd(q, kv, *, name):
    L, C = q.shape
    tm = _pick(L, 512)
    Dh = XA_HEAD_DIM

    def body(q_ref, kv_ref, o_ref):
        for h in range(XA_HEADS):
            qh = q_ref[:, h * Dh:(h + 1) * Dh]
            kh = kv_ref[:, h * Dh:(h + 1) * Dh]
            vh = kv_ref[:, C + h * Dh:C + (h + 1) * Dh]
            s = lax.dot_general(qh, kh, _DN["nt"], preferred_element_type=f32) * _XA_SCALE
            e = jnp.exp(s - jnp.max(s, axis=-1, keepdims=True))
            p = e / jnp.sum(e, axis=-1, keepdims=True)
            o_ref[:, h * Dh:(h + 1) * Dh] = jnp.dot(p.astype(bf16), vh, preferred_element_type=f32).astype(bf16)

    return pl.pallas_call(body, name=name, out_shape=S((L, C), bf16), grid=(L // tm,),
                          in_specs=[_rows(tm, C), _const((N_MEM, 2 * C))], out_specs=_rows(tm, C),
                          compiler_params=_cp("parallel"))(q, kv)


def _attn_bwd(q, kv, do, *, name):
    L, C = q.shape
    tm = _pick(L, 512)
    Dh = XA_HEAD_DIM

    def body(q_ref, kv_ref, do_ref, dq_ref, dkv_ref):
        i = pl.program_id(0)

        @pl.when(i == 0)
        def _():
            dkv_ref[...] = jnp.zeros_like(dkv_ref)

        for h in range(XA_HEADS):
            qh = q_ref[:, h * Dh:(h + 1) * Dh]
            kh = kv_ref[:, h * Dh:(h + 1) * Dh]
            vh = kv_ref[:, C + h * Dh:C + (h + 1) * Dh]
            doh = do_ref[:, h * Dh:(h + 1) * Dh]
            s = lax.dot_general(qh, kh, _DN["nt"], preferred_element_type=f32) * _XA_SCALE
            e = jnp.exp(s - jnp.max(s, axis=-1, keepdims=True))
            p = e / jnp.sum(e, axis=-1, keepdims=True)
            pb = p.astype(bf16)
            dkv_ref[:, C + h * Dh:C + (h + 1) * Dh] += lax.dot_general(pb, doh, _DN["tn"], preferred_element_type=f32)
            dp = lax.dot_general(doh, vh, _DN["nt"], preferred_element_type=f32)
            ds = (p * (dp - jnp.sum(dp * p, axis=-1, keepdims=True)) * _XA_SCALE).astype(bf16)
            dq_ref[:, h * Dh:(h + 1) * Dh] = jnp.dot(ds, kh, preferred_element_type=f32).astype(bf16)
            dkv_ref[:, h * Dh:(h + 1) * Dh] += lax.dot_general(ds, qh, _DN["tn"], preferred_element_type=f32)

    return pl.pallas_call(body, name=name, out_shape=(S((L, C), bf16), S((N_MEM, 2 * C), f32)), grid=(L // tm,),
                          in_specs=[_rows(tm, C), _const((N_MEM, 2 * C)), _rows(tm, C)],
                          out_specs=(_rows(tm, C), _const((N_MEM, 2 * C))),
                          compiler_params=_cp("arbitrary"))(q, kv, do)


Q = CHUNK
PAIRS = HEADS_PER_GROUP // 2
GW = HEADS_PER_GROUP * HEAD_DIM


def _split(x, pieces):
    out = []
    for _ in range(pieces - 1):
        p = x.astype(bf16)
        out.append(p)
        x = x - p.astype(f32)
    return out + [x.astype(bf16)]


def _sel_right(x, sel, mode="nn", pieces=2):
    return sum(lax.dot_general(p, sel, _DN[mode], preferred_element_type=f32) for p in _split(x, pieces))


def _sel_left(sel, x, pieces=3):
    return sum(lax.dot_general(sel, p, _DN["nn"], preferred_element_type=f32) for p in _split(x, pieces))


def _ssd_common(dt_ref, hp_ref):
    dt_pre = dt_ref[...] + hp_ref[0:1, :]
    dt = _softplus(dt_pre)
    A = -jnp.exp(hp_ref[1:2, :])
    a = dt * A
    row = lax.broadcasted_iota(jnp.int32, (Q, Q), 0)
    col = lax.broadcasted_iota(jnp.int32, (Q, Q), 1)
    tri = row >= col
    cs = _sel_left(tri.astype(bf16), a)
    T = cs[Q - 1:Q, :]
    return dict(dt_pre=dt_pre, dt=dt, A=A, cs=cs, csT=cs.T, T=T, ecs=jnp.exp(cs), eend=jnp.exp(T - cs), eT=jnp.exp(T),
                tri=tri, row=row, col=col)


def _pair_expand(v, jj, lo):
    return jnp.where(lo, v[:, 2 * jj:2 * jj + 1], v[:, 2 * jj + 1:2 * jj + 2])


def _decay(cm, h):
    seg = cm["cs"][:, h:h + 1] - cm["csT"][h:h + 1, :]
    return jnp.where(cm["tri"], jnp.exp(jnp.where(cm["tri"], seg, 0.0)), 0.0)


def _decay_t(cm, h):
    keep = cm["row"] <= cm["col"]
    seg = cm["csT"][h:h + 1, :] - cm["cs"][:, h:h + 1]
    return jnp.where(keep, jnp.exp(jnp.where(keep, seg, 0.0)), 0.0)


def _ssd_fwd(act, dtp, hp, *, name):
    L = act.shape[0]
    nc = L // Q

    def body(xs_ref, b_ref, c_ref, dt_ref, hp_ref, y_ref, hs_ref, h_scr):
        c = pl.program_id(1)

        @pl.when(c == 0)
        def _():
            h_scr[...] = jnp.zeros_like(h_scr)

        cm = _ssd_common(dt_ref, hp_ref)
        Bb, Cb = b_ref[...].astype(bf16), c_ref[...].astype(bf16)
        CB = lax.dot_general(Cb, Bb, _DN["nt"], preferred_element_type=f32)
        lo = lax.broadcasted_iota(jnp.int32, (Q, LANE), 1) < HEAD_DIM
        top = lax.broadcasted_iota(jnp.int32, (LANE, LANE), 0) < HEAD_DIM
        Drow = hp_ref[2:3, :]
        for jj in range(PAIRS):
            hA, hB = 2 * jj, 2 * jj + 1
            dtx, ecsx, eendx = (_pair_expand(cm[k], jj, lo) for k in ("dt", "ecs", "eend"))
            xs_p = xs_ref[:, jj * LANE:(jj + 1) * LANE]
            Xd = xs_p * dtx
            Y = None
            for h, Xm in ((hA, jnp.where(lo, Xd, 0.0)), (hB, jnp.where(lo, 0.0, Xd))):
                W = (CB * _decay(cm, h)).astype(bf16)
                t = jnp.dot(W, Xm.astype(bf16), preferred_element_type=f32)
                Y = t if Y is None else Y + t
            Hp = h_scr[jj]
            hs_ref[0, jj] = Hp
            Yoff = lax.dot_general(Cb, Hp.astype(bf16), _DN["nt"], preferred_element_type=f32) * ecsx
            Dx = jnp.where(lo[0:1, :], Drow[:, hA:hA + 1], Drow[:, hB:hB + 1])
            y_ref[:, jj * LANE:(jj + 1) * LANE] = Y + Yoff + xs_p * Dx
            Snew = lax.dot_general((Xd * eendx).astype(bf16), Bb, _DN["tn"], preferred_element_type=f32)
            eTx = jnp.where(top, cm["eT"][:, hA:hA + 1], cm["eT"][:, hB:hB + 1])
            h_scr[jj] = Hp * eTx + Snew

    return pl.pallas_call(
        body, name=name, out_shape=(S((L, D_INNER), f32), S((nc, N_SSM_HEADS // 2, LANE, D_STATE), f32)),
        grid=(N_GROUPS, nc),
        in_specs=[pl.BlockSpec((Q, GW), lambda g, c: (c, g)),
                  pl.BlockSpec((Q, D_STATE), lambda g, c: (c, D_INNER // D_STATE + g)),
                  pl.BlockSpec((Q, D_STATE), lambda g, c: (c, D_INNER // D_STATE + N_GROUPS + g)),
                  pl.BlockSpec((Q, LANE), lambda g, c: (c, g)),
                  pl.BlockSpec((SUBLANE, LANE), lambda g, c: (0, g))],
        out_specs=(pl.BlockSpec((Q, GW), lambda g, c: (c, g)),
                   pl.BlockSpec((1, PAIRS, LANE, D_STATE), lambda g, c: (c, g, 0, 0))),
        scratch_shapes=[pltpu.VMEM((PAIRS, LANE, D_STATE), f32)],
        compiler_params=_cp("arbitrary", "arbitrary"))(act, act, act, dtp, hp)


def _ssd_bwd(act, dtp, hp, dy, hs, *, name):
    L = act.shape[0]
    nc = L // Q

    def body(xs_ref, b_ref, c_ref, dt_ref, hp_ref, dy_ref, hs_ref, dxs_ref, db_ref, dc_ref, ddt_ref, dhp_ref, dh_scr):
        c = pl.program_id(1)

        @pl.when(c == 0)
        def _():
            dh_scr[...] = jnp.zeros_like(dh_scr)
            dhp_ref[...] = jnp.zeros_like(dhp_ref)

        cm = _ssd_common(dt_ref, hp_ref)
        Bb, Cb = b_ref[...].astype(bf16), c_ref[...].astype(bf16)
        CB = lax.dot_general(Cb, Bb, _DN["nt"], preferred_element_type=f32)
        CBT = lax.dot_general(Bb, Cb, _DN["nt"], preferred_element_type=f32)
        lane = lax.broadcasted_iota(jnp.int32, (Q, LANE), 1)
        sub = lax.broadcasted_iota(jnp.int32, (LANE, LANE), 0)
        lo = lane < HEAD_DIM
        top = sub < HEAD_DIM
        Drow = hp_ref[2:3, :]
        zero = jnp.zeros((Q, LANE), f32)
        dcs, dcsT, ddtx, dC, dB, dCB = zero, zero, zero, zero, zero, jnp.zeros((Q, Q), f32)
        dD_row = jnp.zeros((1, LANE), f32)
        dT_row = jnp.zeros((1, LANE), f32)
        for jj in range(PAIRS):
            hA, hB = 2 * jj, 2 * jj + 1
            Pj = (lane == jnp.where(top, hA, hB)).astype(bf16)
            dtx, ecsx, eendx = (_pair_expand(cm[k], jj, lo) for k in ("dt", "ecs", "eend"))
            xs_p = xs_ref[:, jj * LANE:(jj + 1) * LANE]
            dY_p = dy_ref[:, jj * LANE:(jj + 1) * LANE]
            Xd = xs_p * dtx
            Xdb = Xd.astype(bf16)
            Hp, dHn = hs_ref[0, jj], dh_scr[jj]
            Hb, dHb = Hp.astype(bf16), dHn.astype(bf16)
            EdYb = (dY_p * ecsx).astype(bf16)
            YoffN = lax.dot_general(Cb, Hb, _DN["nt"], preferred_element_type=f32)
            dC = dC + jnp.dot(EdYb, Hb, preferred_element_type=f32)
            dH_off = lax.dot_general(EdYb, Cb, _DN["tn"], preferred_element_type=f32)
            R = lax.dot_general(Bb, dHb, _DN["nt"], preferred_element_type=f32)
            Xe = Xd * eendx
            dB = dB + jnp.dot(Xe.astype(bf16), dHb, preferred_element_type=f32)
            dXd = R * eendx
            RXe = R * Xe
            dcs = dcs + _sel_right(dY_p * YoffN * ecsx - RXe, Pj)
            HH = dHn * Hp
            hh = [jnp.sum(jnp.sum(HH[r0:r0 + HEAD_DIM], axis=0, keepdims=True), axis=1, keepdims=True) for r0 in (0, HEAD_DIM)]
            rxe_cols = jnp.broadcast_to(jnp.sum(RXe, axis=0, keepdims=True), (SUBLANE, LANE))
            dT_row = dT_row + _sel_right(rxe_cols, Pj, pieces=3)[0:1] \
                + (jnp.where(lane[0:1] == hA, hh[0], 0.0) + jnp.where(lane[0:1] == hB, hh[1], 0.0)) * cm["eT"]
            for h, keep in ((hA, lo), (hB, jnp.logical_not(lo))):
                M = _decay(cm, h)
                Wf = CB * M
                dYm = jnp.where(keep, dY_p, 0.0).astype(bf16)
                dW = lax.dot_general(dYm, Xdb, _DN["nt"], preferred_element_type=f32)
                WT = (CBT * _decay_t(cm, h)).astype(bf16)
                dXd = dXd + jnp.dot(WT, dYm, preferred_element_type=f32)
                Z = dW * Wf
                dcs = dcs + _sel_right(Z, (lane == h).astype(bf16))
                dcsT = dcsT + jnp.where(sub == h, jnp.sum(Z, axis=0, keepdims=True), 0.0)
                dCB = dCB + dW * M
            Dx = jnp.where(lo[0:1, :], Drow[:, hA:hA + 1], Drow[:, hB:hB + 1])
            dxs_ref[:, jj * LANE:(jj + 1) * LANE] = dXd * dtx + dY_p * Dx
            ddtx = ddtx + _sel_right(dXd * xs_p, Pj)
            dD_cols = jnp.broadcast_to(jnp.sum(dY_p * xs_p, axis=0, keepdims=True), (SUBLANE, LANE))
            dD_row = dD_row + _sel_right(dD_cols, Pj, pieces=3)[0:1]
            eTx = jnp.where(top, cm["eT"][:, hA:hA + 1], cm["eT"][:, hB:hB + 1])
            dh_scr[jj] = dHn * eTx + dH_off
        dCBb = dCB.astype(bf16)
        dc_ref[...] = dC + jnp.dot(dCBb, Bb, preferred_element_type=f32)
        db_ref[...] = dB + lax.dot_general(dCBb, Cb, _DN["tn"], preferred_element_type=f32)
        dcs = dcs - dcsT.T + jnp.where(lax.broadcasted_iota(jnp.int32, (Q, LANE), 0) == Q - 1, dT_row, 0.0)
        da = _sel_left((cm["row"] <= cm["col"]).astype(bf16), dcs)
        ddt_pre = (da * cm["A"] + ddtx) * _sigmoid(cm["dt_pre"])
        ddt_ref[...] = ddt_pre
        r8 = lax.broadcasted_iota(jnp.int32, (SUBLANE, LANE), 0)
        dhp_ref[...] += jnp.where(r8 == 0, jnp.sum(ddt_pre, axis=0, keepdims=True),
                                  jnp.where(r8 == 1, jnp.sum(da * cm["dt"], axis=0, keepdims=True) * cm["A"],
                                            jnp.where(r8 == 2, dD_row, 0.0)))

    rev = lambda c: nc - 1 - c
    return pl.pallas_call(
        body, name=name,
        out_shape=(S((L, D_INNER), f32), S((L, N_GROUPS * D_STATE), f32), S((L, N_GROUPS * D_STATE), f32),
                   S((L, N_GROUPS * LANE), f32), S((SUBLANE, N_GROUPS * LANE), f32)),
        grid=(N_GROUPS, nc),
        in_specs=[pl.BlockSpec((Q, GW), lambda g, c: (rev(c), g)),
                  pl.BlockSpec((Q, D_STATE), lambda g, c: (rev(c), D_INNER // D_STATE + g)),
                  pl.BlockSpec((Q, D_STATE), lambda g, c: (rev(c), D_INNER // D_STATE + N_GROUPS + g)),
                  pl.BlockSpec((Q, LANE), lambda g, c: (rev(c), g)),
                  pl.BlockSpec((SUBLANE, LANE), lambda g, c: (0, g)),
                  pl.BlockSpec((Q, GW), lambda g, c: (rev(c), g)),
                  pl.BlockSpec((1, PAIRS, LANE, D_STATE), lambda g, c: (rev(c), g, 0, 0))],
        out_specs=(pl.BlockSpec((Q, GW), lambda g, c: (rev(c), g)),
                   pl.BlockSpec((Q, D_STATE), lambda g, c: (rev(c), g)),
                   pl.BlockSpec((Q, D_STATE), lambda g, c: (rev(c), g)),
                   pl.BlockSpec((Q, LANE), lambda g, c: (rev(c), g)),
                   pl.BlockSpec((SUBLANE, LANE), lambda g, c: (0, g))),
        scratch_shapes=[pltpu.VMEM((PAIRS, LANE, D_STATE), f32)],
        compiler_params=_cp("arbitrary", "arbitrary"))(act, act, act, dtp, hp, dy, hs)


def _group_pad_cols(w):
    lead = w.shape[:-1]
    w = w.reshape(lead + (N_GROUPS, HEADS_PER_GROUP))
    w = jnp.pad(w, [(0, 0)] * len(lead) + [(0, 0), (0, LANE - HEADS_PER_GROUP)])
    return w.reshape(lead + (N_GROUPS * LANE,))


def _group_unpad_cols(w):
    lead = w.shape[:-1]
    return w.reshape(lead + (N_GROUPS, LANE))[..., :HEADS_PER_GROUP].reshape(lead + (N_SSM_HEADS,))


def _row(v):
    return v.reshape(1, -1)


ROW_SHARDED = ('ssm_out_w', 'cf_pw2_w', 'xa_q_w', 'xa_o_w', 'ffn_out_w')
COL_SHARDED = ('cf_pw1_w', 'xa_kv_w', 'ffn_in_w')


MIXER_WEIGHTS = ('ssm_in_w', 'ssm_out_w', 'cf_pw1_w', 'cf_pw2_w')


def _layer_matmul_weights(i, part):
    if part == "mixer":
        return ('ssm_in_w', 'ssm_out_w') if i % 2 == 0 else ('cf_pw1_w', 'cf_pw2_w')
    return ('xa_q_w', 'xa_kv_w', 'xa_o_w', 'ffn_in_w', 'ffn_out_w')


def _device_step(x, mem, target, W, layer_weights, layer_grads, start_after=()):
    ng = W['norm_g']
    lw = []
    for i in range(DEPTH):
        j = i // 2
        p = {}
        if i % 2 == 0:
            p['cw'] = _pad_taps(W['ssm_conv_w'][j], SSM_CONV)
            p['cb'] = _row(W['ssm_conv_b'][j])
            hp = jnp.stack([_group_pad_cols(W['ssm_dt_bias'][j]), _group_pad_cols(W['ssm_A_log'][j]),
                            _group_pad_cols(W['ssm_D'][j])])
            p['hp'] = jnp.pad(hp, ((0, SUBLANE - 3), (0, 0)))
            p['sng'] = _row(W['ssm_norm_g'][j])
        else:
            p['pw1b'] = _row(W['cf_pw1_b'][j])
            p['dww'], p['dwb'] = _pad_taps(W['cf_dw_w'][j], CF_KERNEL), _row(W['cf_dw_b'][j])
            p['lng'], p['lnb'] = _row(W['cf_ln_g'][j]), _row(W['cf_ln_b'][j])
            p['pw2b'] = _row(W['cf_pw2_b'][j])
        p['memg'] = _row(W['xa_mem_g'][i])
        p['fcw'], p['fcb'] = _pad_taps(W['ffn_conv_w'][i], FFN_CONV), _row(W['ffn_conv_b'][i])
        p['g'] = [_row(ng[i, s]) for s in range(6)]
        lw.append(p)

    def wmm(a, wl, wname, mode, **kw):
        return _mm(a, wl[wname], mode, b_shards=wname in COL_SHARDED, **kw)

    saved = []
    X = x
    h = _rmsnorm_fwd(X, lw[0]['g'][0], name="norm_in", after=start_after)
    for i in range(DEPTH):
        p, sv = lw[i], {}
        wl = dict(layer_weights(i, "mixer", X))
        sv['X0'], sv['h'], sv['wl'] = X, h, wl
        if i % 2 == 0:
            win = jnp.concatenate([wl['ssm_in_w'][s] for s in range(N_CHIPS)], axis=1)
            wl['wz'], wl['wx'] = win[:, :D_INNER], win[:, D_INNER:D_INNER + CONV_DIM]
            wl['wdt'] = _group_pad_cols(win[:, D_INNER + CONV_DIM:])
            z = _mm(h, wl['wz'], "nn", name="ssm_z")
            xbc = _mm(h, wl['wx'], "nn", name="ssm_xbc")
            dtp = _mm(h, wl['wdt'], "nn", name="ssm_dt")
            act = _conv_act_fwd(xbc, p['cw'], p['cb'], K=SSM_CONV, act="silu", name="ssm_conv_fwd", out_dtype=f32)
            y, hs = _ssd_fwd(act, dtp, p['hp'], name="ssd_fwd")
            yn = _gated_norm_fwd(y, z, p['sng'], name="ssm_gnorm_fwd")
            mix = wmm(yn, wl, 'ssm_out_w', "nn", name="ssm_out")
            sv.update(z=z, xbc=xbc, dtp=dtp, act=act, y=y, hs=hs, yn=yn)
        else:
            u = wmm(h, wl, 'cf_pw1_w', "nn", name="cf_pw1", bias=p['pw1b'])
            c, s = _cf_fwd(u, p['dww'], p['dwb'], p['lng'], p['lnb'], name="cf_conv_fwd")
            mix = wmm(s, wl, 'cf_pw2_w', "nn", name="cf_pw2", bias=p['pw2b'])
            sv.update(u=u, c=c, s=s)
        wl.update(layer_weights(i, "rest", mix))
        X1, h2 = _resid_norm_fwd(X, mix, p['g'][1], p['g'][2], name="resid_norm_a")
        q = wmm(h2, wl, 'xa_q_w', "nn", name="xa_q", out_dtype=bf16)
        m = _rmsnorm_fwd(mem, p['memg'], name="xa_mem_norm")
        kv = wmm(m, wl, 'xa_kv_w', "nn", name="xa_kv", out_dtype=bf16)
        o = _attn_fwd(q, kv, name="xa_attn_fwd")
        a = wmm(o, wl, 'xa_o_w', "nn", name="xa_o")
        X2, h3 = _resid_norm_fwd(X1, a, p['g'][3], p['g'][4], name="resid_norm_b")
        u0 = wmm(h3, wl, 'ffn_in_w', "nn", name="ffn_in")
        fact = _conv_act_fwd(u0, p['fcw'], p['fcb'], K=FFN_CONV, act="swiglu", name="ffn_conv_fwd", out_dtype=bf16)
        f = wmm(fact, wl, 'ffn_out_w', "nn", name="ffn_out")
        g_next = lw[i + 1]['g'][0] if i + 1 < DEPTH else None
        X3, hn = _resid_norm_fwd(X2, f, p['g'][5], g_next, name="resid_norm_c" if g_next is not None else "resid_norm_last")
        sv.update(mix=mix, X1=X1, h2=h2, q=q, m=m, kv=kv, o=o, a=a, X2=X2, h3=h3, u0=u0, fact=fact, f=f)
        saved.append(sv)
        X, h = X3, hn

    sse, G = _loss_fwd_bwd(X, target, name="loss")

    small = [n for n in WEIGHT_NAMES if n not in MATMUL_WEIGHTS]
    gr = {n: [None] * W[n].shape[0] for n in small}

    def dwmm(gl, a, d, wname, *, name):
        if wname in COL_SHARDED:
            gl[wname] = _mm(a, d, "tn", name=name, out_dtype=bf16, out_shards=True)
        else:
            g = _mm(a, d, "tn", name=name, out_dtype=bf16)
            gl[wname] = g.reshape(N_CHIPS, g.shape[0] // N_CHIPS, g.shape[1])

    dng = [[None] * 6 for _ in range(DEPTH)]
    for i in reversed(range(DEPTH)):
        p, sv, j = lw[i], saved[i], i // 2
        wl, gl = sv['wl'], {}
        df, dng[i][5], _ = _norm_bwd(sv['f'], p['g'][5], G, name="nb_f", out_dtype=bf16)
        dwmm(gl, sv['fact'], df, 'ffn_out_w', name="ffn_out_dw")
        dfact = wmm(df, wl, 'ffn_out_w', "nt", name="ffn_out_dx")
        du0, dcw, dcb = _conv_act_bwd(sv['u0'], [dfact], p['fcw'], p['fcb'], K=FFN_CONV, act="swiglu", name="ffn_conv_bwd")
        gr['ffn_conv_w'][i], gr['ffn_conv_b'][i] = dcw[:FFN_CONV], dcb[0]
        dwmm(gl, sv['h3'], du0, 'ffn_in_w', name="ffn_in_dw")
        dh3 = wmm(du0, wl, 'ffn_in_w', "nt", name="ffn_in_dx")
        G, dng[i][4], _ = _norm_bwd(sv['X2'], p['g'][4], dh3, name="nb_x2", add=G)
        da, dng[i][3], _ = _norm_bwd(sv['a'], p['g'][3], G, name="nb_a", out_dtype=bf16)
        dwmm(gl, sv['o'], da, 'xa_o_w', name="xa_o_dw")
        do = wmm(da, wl, 'xa_o_w', "nt", name="xa_o_dx", out_dtype=bf16)
        dq, dkv = _attn_bwd(sv['q'], sv['kv'], do, name="xa_attn_bwd")
        dwmm(gl, sv['h2'], dq, 'xa_q_w', name="xa_q_dw")
        dh2 = wmm(dq, wl, 'xa_q_w', "nt", name="xa_q_dx")
        dwmm(gl, sv['m'], dkv, 'xa_kv_w', name="xa_kv_dw")
        dm = wmm(dkv, wl, 'xa_kv_w', "nt", name="xa_kv_dx")
        _, dmg, _ = _norm_bwd(mem, p['memg'], dm, name="nb_mem")
        gr['xa_mem_g'][i] = dmg[0]
        G, dng[i][2], _ = _norm_bwd(sv['X1'], p['g'][2], dh2, name="nb_x1", add=G)
        behind = tuple(layer_grads(i, "rest", gl))
        dmix, dng[i][1], dmix_sum = _norm_bwd(sv['mix'], p['g'][1], G, name="nb_mix", out_dtype=bf16, after=behind)
        if i % 2 == 0:
            dwmm(gl, sv['yn'], dmix, 'ssm_out_w', name="ssm_out_dw")
            dyn = wmm(dmix, wl, 'ssm_out_w', "nt", name="ssm_out_dx")
            dy, dz, dsng = _gated_norm_bwd(sv['y'], sv['z'], p['sng'], dyn, name="ssm_gnorm_bwd")
            gr['ssm_norm_g'][j] = dsng[0]
            dxs, dB, dC, ddtp, dhp = _ssd_bwd(sv['act'], sv['dtp'], p['hp'], dy, sv['hs'], name="ssd_bwd")
            gr['ssm_dt_bias'][j], gr['ssm_A_log'][j], gr['ssm_D'][j] = (_group_unpad_cols(dhp[r]) for r in range(3))
            dxbc, dcw, dcb = _conv_act_bwd(sv['xbc'], [dxs, dB, dC], p['cw'], p['cb'], K=SSM_CONV, act="silu",
                                           name="ssm_conv_bwd")
            gr['ssm_conv_w'][j], gr['ssm_conv_b'][j] = dcw[:SSM_CONV], dcb[0]
            hh = sv['h']
            dwz = _mm(hh, dz, "tn", name="ssm_z_dw", out_dtype=bf16)
            dwx = _mm(hh, dxbc, "tn", name="ssm_xbc_dw", out_dtype=bf16)
            dwdt = _mm(hh, ddtp, "tn", name="ssm_dt_dw", out_dtype=bf16)
            din = jnp.concatenate([dwz, dwx, _group_unpad_cols(dwdt)], axis=1)
            gl['ssm_in_w'] = jnp.stack(jnp.split(din, N_CHIPS, axis=1))
            behind = tuple(layer_grads(i, "mixer", gl))
            dh = _mm(dz, wl['wz'], "nt", name="ssm_z_dx", after=behind)
            dh = _mm(dxbc, wl['wx'], "nt", name="ssm_xbc_dx", add=dh)
            dh = _mm(ddtp, wl['wdt'], "nt", name="ssm_dt_dx", add=dh)
        else:
            dwmm(gl, sv['s'], dmix, 'cf_pw2_w', name="cf_pw2_dw")
            gr['cf_pw2_b'][j] = dmix_sum[0]
            ds = wmm(dmix, wl, 'cf_pw2_w', "nt", name="cf_pw2_dx")
            dc, dlg, dlb = _cf_ln_bwd(sv['c'], p['lng'], p['lnb'], ds, name="cf_ln_bwd")
            gr['cf_ln_g'][j], gr['cf_ln_b'][j] = dlg[0], dlb[0]
            du, ddw, ddb, dus = _cf_glu_bwd(sv['u'], dc, p['dww'], name="cf_glu_bwd")
            gr['cf_dw_w'][j], gr['cf_dw_b'][j], gr['cf_pw1_b'][j] = ddw[:CF_KERNEL], ddb[0], dus[0]
            dwmm(gl, sv['h'], du, 'cf_pw1_w', name="cf_pw1_dw")
            behind = tuple(layer_grads(i, "mixer", gl))
            dh = wmm(du, wl, 'cf_pw1_w', "nt", name="cf_pw1_dx", after=behind)
        G, dng[i][0], _ = _norm_bwd(sv['X0'], p['g'][0], dh, name="nb_x0", add=G)
    gr['norm_g'] = [jnp.concatenate(dng[i], axis=0) for i in range(DEPTH)]
    gsmall = {n: jnp.stack(gr[n]) for n in small}
    return sse, G, gsmall


MESH = pl.DeviceIdType.MESH
HBM_SPEC = pl.BlockSpec(memory_space=pltpu.HBM)


def _chip_peers(x, y):
    return [(1 - x, y), (x, 1 - y), (1 - x, 1 - y)]


def _all_gather_chips(buf, *, name):
    R, C = buf.shape

    def body(in_ref, out_ref, send_sems, recv_sems, local_sem):
        x, y, c = lax.axis_index("x"), lax.axis_index("y"), lax.axis_index("c")
        me = 2 * x + y
        mine = pltpu.make_async_copy(in_ref, out_ref.at[me], local_sem)
        mine.start()
        peers = _chip_peers(x, y)
        sends = []
        for k, (px, py) in enumerate(peers):
            cp = pltpu.make_async_remote_copy(src_ref=in_ref, dst_ref=out_ref.at[me], send_sem=send_sems.at[k],
                                              recv_sem=recv_sems.at[k], device_id=(px, py, c), device_id_type=MESH)
            cp.start()
            sends.append(cp)
        for k, (px, py) in enumerate(peers):
            pltpu.make_async_remote_copy(src_ref=in_ref, dst_ref=out_ref.at[2 * px + py], send_sem=send_sems.at[k],
                                         recv_sem=recv_sems.at[k], device_id=(px, py, c), device_id_type=MESH).wait_recv()
        for cp in sends:
            cp.wait_send()
        mine.wait()

    return pl.pallas_call(body, name=name, out_shape=S((N_CHIPS, R, C), buf.dtype), in_specs=[HBM_SPEC], out_specs=HBM_SPEC,
                          scratch_shapes=[pltpu.SemaphoreType.DMA((3,)), pltpu.SemaphoreType.DMA((3,)),
                                          pltpu.SemaphoreType.DMA(())])(buf)


def _remote(src, dst, send_sem, recv_sem, device):
    return pltpu.make_async_remote_copy(src_ref=src, dst_ref=dst, send_sem=send_sem, recv_sem=recv_sem,
                                        device_id=device, device_id_type=MESH)


def _gather_matmul_weights(shards, *, name):
    n = len(shards)

    def body(*refs):
        ins, outs = refs[:n], refs[n:2 * n]
        send, recv, fsend, frecv, lsem = refs[2 * n:]
        x, y, c = lax.axis_index("x"), lax.axis_index("y"), lax.axis_index("c")
        me, sib = 2 * x + y, (x, y, 1 - c)
        peers = _chip_peers(x, y)
        started, local = [], []
        for w in range(n):
            cp = pltpu.make_async_copy(ins[w], outs[w].at[:, me], lsem.at[w])
            cp.start()
            local.append(cp)
            for k, (px, py) in enumerate(peers):
                cp = _remote(ins[w].at[:, c], outs[w].at[:, me, c], send.at[w, k], recv.at[w, k], (px, py, c))
                cp.start()
                started.append(cp)
        for w in range(n):
            for k, (px, py) in enumerate(peers):
                landed = outs[w].at[:, 2 * px + py, c]
                _remote(ins[w].at[:, c], landed, send.at[w, k], recv.at[w, k], (px, py, c)).wait_recv()
                cp = _remote(landed, landed, fsend.at[w, k], frecv.at[w, k], sib)
                cp.start()
                started.append(cp)
        for w in range(n):
            for k, (px, py) in enumerate(peers):
                _remote(ins[w].at[:, c], outs[w].at[:, 2 * px + py, 1 - c], fsend.at[w, k], frecv.at[w, k], sib).wait_recv()
        for cp in started:
            cp.wait_send()
        for cp in local:
            cp.wait()

    out_shape = tuple(S((s.shape[0], N_CHIPS) + s.shape[1:], s.dtype) for s in shards)
    sems = [pltpu.SemaphoreType.DMA((n, 3)) for _ in range(4)] + [pltpu.SemaphoreType.DMA((n,))]
    return pl.pallas_call(body, name=name, out_shape=out_shape, in_specs=[HBM_SPEC] * n, out_specs=(HBM_SPEC,) * n,
                          scratch_shapes=sems)(*shards)


SEM_SPEC = pl.BlockSpec(memory_space=pltpu.SEMAPHORE)
VMEM_SPEC = pl.BlockSpec(memory_space=pltpu.VMEM)


def _in_hbm(a):
    return pltpu.with_memory_space_constraint(a, pltpu.HBM)


def _chip_targets(x, y):
    return [(x, y), (1 - x, y), (x, 1 - y), (1 - x, 1 - y)]


def _spread_start(srcs, scatter, *, name, after=()):
    n = len(srcs)
    lands = [lax.empty((N_CHIPS,) + (s.shape[1:] if scatter else s.shape), s.dtype) for s in srcs]

    def body(*refs):
        src, land = refs[:n], refs[n:2 * n]
        send, recv, token = refs[2 * n + len(after)], refs[2 * n + len(after) + 1], refs[-1]
        x, y, c = lax.axis_index("x"), lax.axis_index("y"), lax.axis_index("c")
        me = 2 * x + y
        for w in range(n):
            for k, (px, py) in enumerate(_chip_targets(x, y)):
                block = src[w].at[2 * px + py] if scatter else src[w]
                _remote(block, land[w].at[me], send.at[N_CHIPS * w + k], recv.at[N_CHIPS * w + k], (px, py, c)).start()
        token[...] = jnp.zeros_like(token)

    thru = tuple(pltpu.HBM(a.shape, a.dtype) for a in list(srcs) + lands)
    sems = (pltpu.SemaphoreType.DMA((N_CHIPS * n,)), pltpu.SemaphoreType.DMA((N_CHIPS * n,)))
    out = pl.pallas_call(
        body, name=name, out_shape=sems + thru + (S((SUBLANE, LANE), f32),),
        in_specs=[HBM_SPEC] * (2 * n) + [ANY_SPEC] * len(after),
        out_specs=(SEM_SPEC, SEM_SPEC) + (HBM_SPEC,) * (2 * n) + (VMEM_SPEC,),
        input_output_aliases={i: 2 + i for i in range(2 * n)},
        compiler_params=pltpu.CompilerParams(has_side_effects=pltpu.SideEffectType.DATAFLOW_SIDE_EFFECTING),
    )(*[_in_hbm(a) for a in list(srcs) + lands], *after)
    return out[0], out[1], out[2:2 + n], out[2 + n:2 + 2 * n], out[-1]


def _spread_wait(send, recv, srcs, lands, after, scatter, *, name):
    n = len(srcs)

    def body(*refs):
        src, land, send, recv = refs[:n], refs[n:2 * n], refs[2 * n], refs[2 * n + 1]
        x, y, c = lax.axis_index("x"), lax.axis_index("y"), lax.axis_index("c")
        me = 2 * x + y
        for w in range(n):
            for k, (px, py) in enumerate(_chip_targets(x, y)):
                block = src[w].at[me] if scatter else src[w]
                cp = _remote(block, land[w].at[2 * px + py], send.at[N_CHIPS * w + k], recv.at[N_CHIPS * w + k], (px, py, c))
                cp.wait_send()
                cp.wait_recv()

    thru = tuple(pltpu.HBM(a.shape, a.dtype) for a in list(srcs) + list(lands))
    out = pl.pallas_call(
        body, name=name, out_shape=thru,
        in_specs=[HBM_SPEC] * (2 * n) + [SEM_SPEC, SEM_SPEC] + [ANY_SPEC] * len(after), out_specs=(HBM_SPEC,) * (2 * n),
        input_output_aliases={i: i for i in range(2 * n)},
        compiler_params=pltpu.CompilerParams(has_side_effects=pltpu.SideEffectType.DATAFLOW_SIDE_EFFECTING),
    )(*srcs, *lands, send, recv, *after)
    return out[:n], out[n:]


def _swap_sibling(bufs, *, name):
    n = len(bufs)

    def body(*refs):
        src, out, send, recv = refs[:n], refs[n:2 * n], refs[-2], refs[-1]
        sib = (lax.axis_index("x"), lax.axis_index("y"), 1 - lax.axis_index("c"))
        copies = [_remote(src[w], out[w], send.at[w], recv.at[w], sib) for w in range(n)]
        for cp in copies:
            cp.start()
        for cp in copies:
            cp.wait()

    return pl.pallas_call(body, name=name, out_shape=tuple(S(a.shape, a.dtype) for a in bufs),
                          in_specs=[HBM_SPEC] * n, out_specs=(HBM_SPEC,) * n,
                          scratch_shapes=[pltpu.SemaphoreType.DMA((n,)), pltpu.SemaphoreType.DMA((n,))])(*bufs)


def _swap_start(bufs, *, name):
    n = len(bufs)
    lands = [lax.empty(b.shape, b.dtype) for b in bufs]

    def body(*refs):
        src, land, send, recv, token = refs[:n], refs[n:2 * n], refs[2 * n], refs[2 * n + 1], refs[-1]
        sib = (lax.axis_index("x"), lax.axis_index("y"), 1 - lax.axis_index("c"))
        for w in range(n):
            _remote(src[w], land[w], send.at[w], recv.at[w], sib).start()
        token[...] = jnp.zeros_like(token)

    thru = tuple(pltpu.HBM(a.shape, a.dtype) for a in list(bufs) + lands)
    out = pl.pallas_call(
        body, name=name,
        out_shape=(pltpu.SemaphoreType.DMA((n,)), pltpu.SemaphoreType.DMA((n,))) + thru + (S((SUBLANE, LANE), f32),),
        in_specs=[HBM_SPEC] * (2 * n), out_specs=(SEM_SPEC, SEM_SPEC) + (HBM_SPEC,) * (2 * n) + (VMEM_SPEC,),
        input_output_aliases={i: 2 + i for i in range(2 * n)},
        compiler_params=pltpu.CompilerParams(has_side_effects=pltpu.SideEffectType.DATAFLOW_SIDE_EFFECTING),
    )(*[_in_hbm(a) for a in list(bufs) + lands])
    return out[0], out[1], out[2:2 + n], out[2 + n:2 + 2 * n], out[-1]


def _swap_wait(send, recv, bufs, lands, after, *, name):
    n = len(bufs)

    def body(*refs):
        src, land, send, recv = refs[:n], refs[n:2 * n], refs[2 * n], refs[2 * n + 1]
        sib = (lax.axis_index("x"), lax.axis_index("y"), 1 - lax.axis_index("c"))
        for w in range(n):
            cp = _remote(src[w], land[w], send.at[w], recv.at[w], sib)
            cp.wait_send()
            cp.wait_recv()

    thru = tuple(pltpu.HBM(a.shape, a.dtype) for a in list(bufs) + list(lands))
    out = pl.pallas_call(
        body, name=name, out_shape=thru,
        in_specs=[HBM_SPEC] * (2 * n) + [SEM_SPEC, SEM_SPEC] + [ANY_SPEC] * len(after), out_specs=(HBM_SPEC,) * (2 * n),
        input_output_aliases={i: i for i in range(2 * n)},
        compiler_params=pltpu.CompilerParams(has_side_effects=pltpu.SideEffectType.DATAFLOW_SIDE_EFFECTING),
    )(*bufs, *lands, send, recv, *after)
    return out[:n], out[n:]


N_DEVICES = 8


def _allgather_devices(buf, *, name):
    R, C = buf.shape

    def body(in_ref, out_ref, send, recv, lsem):
        x, y, c = lax.axis_index("x"), lax.axis_index("y"), lax.axis_index("c")
        me = 4 * x + 2 * y + c
        mine = pltpu.make_async_copy(in_ref, out_ref.at[me], lsem)
        mine.start()
        flips = [(d >> 2 & 1, d >> 1 & 1, d & 1) for d in range(1, N_DEVICES)]
        peers = [(1 - x if fx else x, 1 - y if fy else y, 1 - c if fc else c) for fx, fy, fc in flips]
        sends = []
        for k, peer in enumerate(peers):
            cp = _remote(in_ref, out_ref.at[me], send.at[k], recv.at[k], peer)
            cp.start()
            sends.append(cp)
        for k, (px, py, pc) in enumerate(peers):
            _remote(in_ref, out_ref.at[4 * px + 2 * py + pc], send.at[k], recv.at[k], (px, py, pc)).wait_recv()
        for cp in sends:
            cp.wait_send()
        mine.wait()

    return pl.pallas_call(body, name=name, out_shape=S((N_DEVICES, R, C), buf.dtype), in_specs=[HBM_SPEC], out_specs=HBM_SPEC,
                          scratch_shapes=[pltpu.SemaphoreType.DMA((N_DEVICES - 1,)), pltpu.SemaphoreType.DMA((N_DEVICES - 1,)),
                                          pltpu.SemaphoreType.DMA(())])(buf)


def _sum_slots(buf, *, name):
    ns, R, C = buf.shape
    tr = _pick(R, 512)
    assert R % tr == 0

    def body(*refs):
        acc = refs[0][...]
        for r in refs[1:ns]:
            acc = acc + r[...]
        refs[ns][...] = acc

    specs = [pl.BlockSpec((None, tr, C), functools.partial(lambda s, i: (s, i, 0), s)) for s in range(ns)]
    return pl.pallas_call(body, name=name, out_shape=S((R, C), buf.dtype), grid=(R // tr,), in_specs=specs,
                          out_specs=pl.BlockSpec((tr, C), lambda i: (i, 0)), compiler_params=_cp("parallel"))(*([buf] * ns))


ADAMW_BLOCK_BYTES = 1 << 20


def _adamw(w, m, v, groups, *, name, layer=None, prev=None):
    shape = w.shape if layer is None else w.shape[1:]
    C = shape[-1]
    Rr = math.prod(shape[:-1])
    tr = Rr
    if Rr * C * 4 > ADAMW_BLOCK_BYTES:
        tr = max(t for t in range(2 * SUBLANE, Rr + 1, 2 * SUBLANE) if Rr % t == 0 and t * C * 4 <= ADAMW_BLOCK_BYTES)
    c1 = 1.0 / (1.0 - ADAM_B1 ** ADAM_STEP)
    c2 = 1.0 / (1.0 - ADAM_B2 ** ADAM_STEP)
    if layer is None:
        to2 = lambda t: t.reshape(Rr, C)
        spec = pl.BlockSpec((tr, C), lambda i: (i, 0))
        res_shape = S((Rr, C), f32)
    else:
        to2 = lambda t: t.reshape(layer[1], Rr, C)
        spec = pl.BlockSpec((None, tr, C), functools.partial(lambda l, i: (l, i, 0), layer[0]))
        res_shape = S((layer[1], Rr, C), f32)
    wspec, spec = spec, pl.BlockSpec((tr, C), lambda i: (i, 0))
    g_specs, g_args, sizes = [], [], []
    for grp in groups:
        sizes.append(len(grp))
        for term in grp:
            if isinstance(term, tuple):
                arr, slot = term
                g_specs.append(pl.BlockSpec((None, tr, C), functools.partial(lambda s, i: (s, i, 0), slot)))
                g_args.append(arr.reshape(arr.shape[0], Rr, C))
            else:
                g_specs.append(spec)
                g_args.append(term.reshape(Rr, C))
    nterms = len(g_args)
    prev = () if prev is None else tuple(to2(t) for t in prev)

    def body(w_ref, m_ref, v_ref, *rest):
        t_refs, (g_ref, d_ref, mo_ref, vo_ref) = rest[:nterms], rest[-4:]
        g, pos = None, 0
        for size in sizes:
            part = None
            for r in t_refs[pos:pos + size]:
                t = r[...].astype(f32)
                part = t if part is None else part + t
            pos += size
            g = part if g is None else g + part
        mn = ADAM_B1 * m_ref[...] + (1.0 - ADAM_B1) * g
        vn = ADAM_B2 * v_ref[...] + (1.0 - ADAM_B2) * (g * g)
        g_ref[...] = g
        mo_ref[...] = mn
        vo_ref[...] = vn
        d_ref[...] = -ADAM_LR * ((mn * c1) / (jnp.sqrt(vn * c2) + ADAM_EPS) + ADAM_WD * w_ref[...])

    out = pl.pallas_call(body, name=name, out_shape=(res_shape,) * 4, grid=(Rr // tr,),
                         in_specs=[wspec] * 3 + g_specs + [ANY_SPEC] * len(prev), out_specs=(wspec,) * 4,
                         input_output_aliases={3 + nterms + k: k for k in range(len(prev))},
                         compiler_params=_cp("parallel"))(to2(w), to2(m), to2(v), *g_args, *prev)
    return tuple(o.reshape(w.shape) for o in out)


def _pack_rows(parts, dtype):
    flat = jnp.concatenate([p.reshape(-1).astype(dtype) for p in parts])
    n = flat.shape[0]
    unit = PACK_COLS * 2 * SUBLANE
    padded = -(-n // unit) * unit
    return jnp.pad(flat, (0, padded - n)).reshape(padded // PACK_COLS, PACK_COLS)


def _unpack_rows(flat2d, shapes):
    flat = flat2d.reshape(-1)
    out, off = [], 0
    for shp in shapes:
        n = math.prod(shp)
        out.append(flat[off:off + n].reshape(shp))
        off += n
    return out


def _gather_weights(local, names, dtype, *, name):
    shapes = [local[n].shape for n in names]
    got = _all_gather_chips(_pack_rows([local[n] for n in names], dtype), name=name)
    per_chip = [_unpack_rows(got[s], shapes) for s in range(N_CHIPS)]
    return {n: jnp.concatenate([per_chip[s][k] for s in range(N_CHIPS)], axis=SHARD_AXIS[n]) for k, n in enumerate(names)}


def kernel(x, mem, norm_g, ssm_in_w, ssm_conv_w, ssm_conv_b, ssm_dt_bias, ssm_A_log, ssm_D, ssm_norm_g, ssm_out_w, cf_pw1_w, cf_pw1_b, cf_dw_w, cf_dw_b, cf_ln_g, cf_ln_b, cf_pw2_w, cf_pw2_b, xa_mem_g, xa_q_w, xa_kv_w, xa_o_w, ffn_in_w, ffn_conv_w, ffn_conv_b, ffn_out_w, loss_target, m_norm_g, m_ssm_in_w, m_ssm_conv_w, m_ssm_conv_b, m_ssm_dt_bias, m_ssm_A_log, m_ssm_D, m_ssm_norm_g, m_ssm_out_w, m_cf_pw1_w, m_cf_pw1_b, m_cf_dw_w, m_cf_dw_b, m_cf_ln_g, m_cf_ln_b, m_cf_pw2_w, m_cf_pw2_b, m_xa_mem_g, m_xa_q_w, m_xa_kv_w, m_xa_o_w, m_ffn_in_w, m_ffn_conv_w, m_ffn_conv_b, m_ffn_out_w, v_norm_g, v_ssm_in_w, v_ssm_conv_w, v_ssm_conv_b, v_ssm_dt_bias, v_ssm_A_log, v_ssm_D, v_ssm_norm_g, v_ssm_out_w, v_cf_pw1_w, v_cf_pw1_b, v_cf_dw_w, v_cf_dw_b, v_cf_ln_g, v_cf_ln_b, v_cf_pw2_w, v_cf_pw2_b, v_xa_mem_g, v_xa_q_w, v_xa_kv_w, v_xa_o_w, v_ffn_in_w, v_ffn_conv_w, v_ffn_conv_b, v_ffn_out_w):
    w_local = dict(zip(WEIGHT_NAMES, (norm_g, ssm_in_w, ssm_conv_w, ssm_conv_b, ssm_dt_bias, ssm_A_log, ssm_D, ssm_norm_g,
                                      ssm_out_w, cf_pw1_w, cf_pw1_b, cf_dw_w, cf_dw_b, cf_ln_g, cf_ln_b, cf_pw2_w, cf_pw2_b,
                                      xa_mem_g, xa_q_w, xa_kv_w, xa_o_w, ffn_in_w, ffn_conv_w, ffn_conv_b, ffn_out_w)))
    m_local = dict(zip(WEIGHT_NAMES, (m_norm_g, m_ssm_in_w, m_ssm_conv_w, m_ssm_conv_b, m_ssm_dt_bias, m_ssm_A_log, m_ssm_D,
                                      m_ssm_norm_g, m_ssm_out_w, m_cf_pw1_w, m_cf_pw1_b, m_cf_dw_w, m_cf_dw_b, m_cf_ln_g,
                                      m_cf_ln_b, m_cf_pw2_w, m_cf_pw2_b, m_xa_mem_g, m_xa_q_w, m_xa_kv_w, m_xa_o_w,
                                      m_ffn_in_w, m_ffn_conv_w, m_ffn_conv_b, m_ffn_out_w)))
    v_local = dict(zip(WEIGHT_NAMES, (v_norm_g, v_ssm_in_w, v_ssm_conv_w, v_ssm_conv_b, v_ssm_dt_bias, v_ssm_A_log, v_ssm_D,
                                      v_ssm_norm_g, v_ssm_out_w, v_cf_pw1_w, v_cf_pw1_b, v_cf_dw_w, v_cf_dw_b, v_cf_ln_g,
                                      v_cf_ln_b, v_cf_pw2_w, v_cf_pw2_b, v_xa_mem_g, v_xa_q_w, v_xa_kv_w, v_xa_o_w,
                                      v_ffn_in_w, v_ffn_conv_w, v_ffn_conv_b, v_ffn_out_w)))

    small = [n for n in WEIGHT_NAMES if n not in MATMUL_WEIGHTS]
    small_sharded = [n for n in small if SHARD_AXIS[n] is not None]
    W = {n: w_local[n] for n in small if SHARD_AXIS[n] is None}
    W.update(_gather_weights(w_local, small_sharded, f32, name="gather_small_weights"))

    def layer_index(n, i):
        return i // 2 if n in MIXER_WEIGHTS else i

    def keys_of(i, parts):
        return [(n, layer_index(n, i)) for part in parts for n in _layer_matmul_weights(i, part)]

    def shards(keys):
        return [w_local[n][l].astype(bf16) for n, l in keys]

    def usable(n, a):
        return a.reshape(N_CHIPS * a.shape[1], a.shape[2]) if n in ROW_SHARDED else a

    mixer0 = keys_of(0, ("mixer",))
    got0 = _gather_matmul_weights([s.reshape(1, 2, s.shape[0] // 2, s.shape[1]) for s in shards(mixer0)],
                                  name="gather_layer0_mixer")
    gather_groups = {(0, "rest"): keys_of(0, ("rest",))}
    gather_groups.update({(i, "mixer"): keys_of(i, ("mixer", "rest")) for i in range(1, DEPTH)})
    gathers, tokens, landed = {}, [], {}
    for gkey in sorted(gather_groups):
        send, recv, srcs, lands, token = _spread_start(shards(gather_groups[gkey]), False, name="gather_start_%d_%s" % gkey,
                                                       after=(got0[0], W[small_sharded[0]]))
        gathers[gkey] = (send, recv, srcs, lands)
        tokens.append(token)

    def layer_weights(i, part, after):
        if (i, part) == (0, "mixer"):
            landed.update({k: g.reshape((N_CHIPS, 2 * g.shape[3], g.shape[4])) for k, g in zip(mixer0, got0)})
        elif (i, part) in gathers:
            _, lands = _spread_wait(*gathers[i, part], (after,), False, name="gather_wait_%d_%s" % (i, part))
            landed.update(zip(gather_groups[i, part], lands))
        return {n: usable(n, landed[n, layer_index(n, i)]) for n in _layer_matmul_weights(i, part)}

    pending, scatters, swaps, own, sib = {}, {}, {}, {}, {}

    def scatter_start(gkey, keys, gl):
        send, recv, srcs, lands, token = _spread_start([gl[k] for k in keys], True, name="grads_start_%d_%s" % gkey)
        scatters[gkey] = (keys, send, recv, srcs, lands)
        return token

    def layer_grads(i, part, gl):
        grads = {(n, layer_index(n, i)): gl[n] for n in _layer_matmul_weights(i, part)}
        behind = []
        if part == "rest":
            if i + 1 < DEPTH:
                keys, send, recv, srcs, lands = scatters.pop((i + 1, "mixer"))
                _, lands = _spread_wait(send, recv, srcs, lands, (grads['xa_kv_w', i],), True,
                                        name="grads_wait_%d" % (i + 1))
                send, recv, srcs, lands, token = _swap_start(lands, name="grads_swap_start_%d" % (i + 1))
                swaps[i + 1] = (keys, send, recv, srcs, lands)
                behind.append(token)
            if i == 0:
                behind.append(scatter_start((0, "rest"), list(grads), grads))
            else:
                pending.update(grads)
        else:
            pending.update(grads)
            if i == 0:
                keys, send, recv, srcs, lands = scatters.pop((0, "rest"))
                _, lands = _spread_wait(send, recv, srcs, lands, (grads['ssm_in_w', 0],), True, name="grads_wait_0_rest")
                send, recv, srcs, lands, token = _swap_start(lands, name="grads_swap_start_0")
                swaps[0] = (keys, send, recv, srcs, lands)
                behind.append(token)
            behind.append(scatter_start((i, "mixer"), list(pending), dict(pending)))
            pending.clear()
        return behind

    sse, gx, gsmall = _device_step(x[0], mem[0], loss_target[0], W, layer_weights, layer_grads, tuple(tokens))

    loss = lax.psum(0.5 * sse[0, 0] / D_MODEL, ("x", "y", "c"))

    last_keys, last_lands = [], []
    for gkey in sorted(scatters):
        keys, send, recv, srcs, lands = scatters[gkey]
        _, lands = _spread_wait(send, recv, srcs, lands, (gx,), True, name="grads_wait_%d_%s" % gkey)
        last_keys += keys
        last_lands += list(lands)
    own.update(zip(last_keys, last_lands))
    sib.update(zip(last_keys, _swap_sibling(last_lands, name="grads_swap_last")))
    for i in sorted(swaps):
        keys, send, recv, srcs, lands = swaps[i]
        mine, theirs = _swap_wait(send, recv, srcs, lands, (gx,), name="grads_swap_wait_%d" % i)
        own.update(zip(keys, mine))
        sib.update(zip(keys, theirs))
    small_shapes = [gsmall[n].shape for n in small]
    slots = _allgather_devices(_pack_rows([gsmall[n] for n in small], f32), name="allgather_small_grads")
    gsum = dict(zip(small, _unpack_rows(_sum_slots(slots, name="sum_small_grads"), small_shapes)))
    chip = 2 * lax.axis_index("x") + lax.axis_index("y")

    res = {}
    for n in MATMUL_WEIGHTS:
        layers, out = w_local[n].shape[0], None
        for l in range(layers):
            groups = [[(own[n, l], s) for s in range(N_CHIPS)], [(sib[n, l], s) for s in range(N_CHIPS)]]
            out = _adamw(w_local[n], m_local[n], v_local[n], groups, name="adamw_%s_%d" % (n, l), layer=(l, layers), prev=out)
        res[n] = out
    for n in small:
        g, ax = gsum[n], SHARD_AXIS[n]
        if ax is not None:
            width = w_local[n].shape[ax]
            g = lax.dynamic_slice_in_dim(g, chip * width, width, axis=ax)
        res[n] = _adamw(w_local[n], m_local[n], v_local[n], [[g]], name="adamw_" + n)
    return (loss, gx[None], *[res[n][0] for n in WEIGHT_NAMES], *[res[n][1] for n in WEIGHT_NAMES],
            *[res[n][2] for n in WEIGHT_NAMES], *[res[n][3] for n in WEIGHT_NAMES])
```

```python
import functools
import math

import jax
import jax.numpy as jnp
from jax import lax
from jax.experimental import pallas as pl
from jax.experimental.pallas import tpu as pltpu

f32 = jnp.float32
bf16 = jnp.bfloat16
S = jax.ShapeDtypeStruct

D_MODEL = 1024
DEPTH = 4
D_INNER = 2048
HEAD_DIM = 64
N_GROUPS = 4
HEADS_PER_GROUP = 8
N_SSM_HEADS = 32
D_STATE = 128
CHUNK = 128
SSM_CONV = 4
CONV_DIM = 3072
CF_KERNEL = 31
N_MEM = 256
XA_HEADS = 4
XA_HEAD_DIM = 256
D_FF = 2816
FFN_CONV = 3
EPS = 1e-6
ADAM_LR, ADAM_B1, ADAM_B2, ADAM_EPS, ADAM_WD, ADAM_STEP = 0.001, 0.9, 0.999, 1e-08, 0.01, 10

LANE = 128
SUBLANE = 8
ROW_SUB = 64
VMEM_LIMIT = 56 * 1024 * 1024
N_CHIPS = 4
PACK_COLS = 1024

WEIGHT_NAMES = ['norm_g', 'ssm_in_w', 'ssm_conv_w', 'ssm_conv_b', 'ssm_dt_bias', 'ssm_A_log', 'ssm_D', 'ssm_norm_g',
                'ssm_out_w', 'cf_pw1_w', 'cf_pw1_b', 'cf_dw_w', 'cf_dw_b', 'cf_ln_g', 'cf_ln_b', 'cf_pw2_w', 'cf_pw2_b',
                'xa_mem_g', 'xa_q_w', 'xa_kv_w', 'xa_o_w', 'ffn_in_w', 'ffn_conv_w', 'ffn_conv_b', 'ffn_out_w']
SHARD_AXIS = {'norm_g': 2, 'ssm_in_w': 2, 'ssm_conv_w': 2, 'ssm_conv_b': None, 'ssm_dt_bias': None, 'ssm_A_log': None,
              'ssm_D': None, 'ssm_norm_g': None, 'ssm_out_w': 1, 'cf_pw1_w': 2, 'cf_pw1_b': 1, 'cf_dw_w': 2, 'cf_dw_b': 1,
              'cf_ln_g': 1, 'cf_ln_b': 1, 'cf_pw2_w': 1, 'cf_pw2_b': 1, 'xa_mem_g': None, 'xa_q_w': 1, 'xa_kv_w': 2,
              'xa_o_w': 1, 'ffn_in_w': 2, 'ffn_conv_w': 2, 'ffn_conv_b': None, 'ffn_out_w': 1}
MATMUL_WEIGHTS = ('ssm_in_w', 'ssm_out_w', 'cf_pw1_w', 'cf_pw2_w', 'xa_q_w', 'xa_kv_w', 'xa_o_w', 'ffn_in_w', 'ffn_out_w')


def _cp(*sem):
    return pltpu.CompilerParams(dimension_semantics=tuple(sem), vmem_limit_bytes=VMEM_LIMIT)


def _pick(dim, pref):
    if dim <= pref:
        return dim
    best = None
    for t in range(LANE, pref + 1, LANE):
        if dim % t == 0:
            best = t
    assert best is not None, (dim, pref)
    return best


def _sigmoid(x):
    return 1.0 / (1.0 + jnp.exp(-x))


def _silu(x):
    return x * _sigmoid(x)


def _dsilu(x):
    s = _sigmoid(x)
    return s * (1.0 + x * (1.0 - s))


def _softplus(x):
    return jnp.maximum(x, 0.0) + jnp.log(1.0 + jnp.exp(-jnp.abs(x)))


_DN = {"nn": (((1,), (0,)), ((), ())), "nt": (((1,), (1,)), ((), ())), "tn": (((0,), (0,)), ((), ()))}


def _mm(a, b, mode, *, name, out_dtype=f32, bias=None, add=None, b_shards=False, out_shards=False, after=()):
    bshape = (b.shape[1], b.shape[2] * N_CHIPS) if b_shards else b.shape
    if mode == "nn":
        (M, K), (K2, N) = a.shape, bshape
    elif mode == "nt":
        (M, K), (N, K2) = a.shape, bshape
    else:
        (K, M), (K2, N) = a.shape, bshape
    assert K == K2, (a.shape, b.shape, mode)
    n_unit = N // N_CHIPS if ((b_shards and mode == "nn") or out_shards) else N
    k_unit = K // N_CHIPS if (b_shards and mode == "nt") else K
    tm, tn, tk = _pick(M, 1024), _pick(n_unit, 1408), _pick(k_unit, 1408)
    nk, nj_u, nk_u = K // tk, n_unit // tn, k_unit // tk
    a_spec = {"nn": pl.BlockSpec((tm, tk), lambda i, j, k: (i, k)), "nt": pl.BlockSpec((tm, tk), lambda i, j, k: (i, k)),
              "tn": pl.BlockSpec((tk, tm), lambda i, j, k: (k, i))}[mode]
    if not b_shards:
        b_spec = {"nn": pl.BlockSpec((tk, tn), lambda i, j, k: (k, j)), "nt": pl.BlockSpec((tn, tk), lambda i, j, k: (j, k)),
                  "tn": pl.BlockSpec((tk, tn), lambda i, j, k: (k, j))}[mode]
    else:
        b_spec = {"nn": pl.BlockSpec((None, tk, tn), lambda i, j, k: (j // nj_u, k, j % nj_u)),
                  "nt": pl.BlockSpec((None, tn, tk), lambda i, j, k: (k // nk_u, j, k % nk_u))}[mode]
    in_specs, args = [a_spec, b_spec], [a, b]
    if bias is not None:
        in_specs.append(pl.BlockSpec((1, tn), lambda i, j, k: (0, j)))
        args.append(bias)
    if add is not None:
        in_specs.append(pl.BlockSpec((tm, tn), lambda i, j, k: (i, j)))
        args.append(add)
    in_specs += [pl.BlockSpec(memory_space=pl.ANY)] * len(after)
    args += list(after)
    if not out_shards:
        out_shape, out_spec = S((M, N), out_dtype), pl.BlockSpec((tm, tn), lambda i, j, k: (i, j))
    else:
        out_shape = S((N_CHIPS, M, n_unit), out_dtype)
        out_spec = pl.BlockSpec((None, tm, tn), lambda i, j, k: (j // nj_u, i, j % nj_u))
    dn = _DN[mode]
    has_bias, has_add = bias is not None, add is not None

    def body(a_ref, b_ref, *rest):
        rest = list(rest)
        bias_ref = rest.pop(0) if has_bias else None
        add_ref = rest.pop(0) if has_add else None
        rest = rest[len(after):]
        o_ref = rest[0]

        def finish(r):
            if has_bias:
                r = r + bias_ref[...]
            if has_add:
                r = r + add_ref[...].astype(f32)
            o_ref[...] = r.astype(out_dtype)

        part = lax.dot_general(a_ref[...].astype(bf16), b_ref[...].astype(bf16), dn, preferred_element_type=f32)
        if nk == 1:
            finish(part)
            return
        acc_ref = rest[1]
        k = pl.program_id(2)

        @pl.when(k == 0)
        def _():
            acc_ref[...] = part

        @pl.when(k > 0)
        def _():
            acc_ref[...] += part

        @pl.when(k == nk - 1)
        def _():
            finish(acc_ref[...])

    return pl.pallas_call(
        body, name=name, out_shape=out_shape, grid=(M // tm, N // tn, nk),
        in_specs=in_specs, out_specs=out_spec, scratch_shapes=[pltpu.VMEM((tm, tn), f32)] if nk > 1 else [],
        compiler_params=_cp("parallel", "parallel", "arbitrary"))(*args)


def _rows(tm, C):
    return pl.BlockSpec((tm, C), lambda i: (i, 0))


def _const(shape):
    return pl.BlockSpec(shape, lambda i: tuple(0 for _ in shape))


def _rms_val(x, g):
    r = lax.rsqrt(jnp.mean(x * x, axis=-1, keepdims=True) + EPS)
    return x * r * g


def _rms_bwd_val(x, g, dy):
    r = lax.rsqrt(jnp.mean(x * x, axis=-1, keepdims=True) + EPS)
    xn = x * r
    dxh = dy * g
    dx = r * (dxh - xn * jnp.mean(dxh * xn, axis=-1, keepdims=True))
    return dx, jnp.sum(dy * xn, axis=0, keepdims=True)


ANY_SPEC = pl.BlockSpec(memory_space=pl.ANY)


def _rmsnorm_fwd(x, g, *, name, after=()):
    L, C = x.shape
    tm = _pick(L, 512)

    def body(x_ref, g_ref, *rest):
        rest[-1][...] = _rms_val(x_ref[...], g_ref[...]).astype(bf16)

    return pl.pallas_call(body, name=name, out_shape=S((L, C), bf16), grid=(L // tm,),
                          in_specs=[_rows(tm, C), _const((1, C))] + [ANY_SPEC] * len(after), out_specs=_rows(tm, C),
                          compiler_params=_cp("parallel"))(x, g, *after)


def _resid_norm_fwd(x, mix, g_post, g_next, *, name):
    L, C = x.shape
    tm = _pick(L, 512)
    want_h = g_next is not None

    def body(x_ref, m_ref, gp_ref, *rest):
        xn = x_ref[...] + _rms_val(m_ref[...], gp_ref[...])
        if want_h:
            gn_ref, xo_ref, h_ref = rest
            h_ref[...] = _rms_val(xn, gn_ref[...]).astype(bf16)
        else:
            (xo_ref,) = rest
        xo_ref[...] = xn

    in_specs = [_rows(tm, C), _rows(tm, C), _const((1, C))]
    args = [x, mix, g_post]
    out_shape, out_specs = [S((L, C), f32)], [_rows(tm, C)]
    if want_h:
        in_specs.append(_const((1, C)))
        args.append(g_next)
        out_shape.append(S((L, C), bf16))
        out_specs.append(_rows(tm, C))
    out = pl.pallas_call(body, name=name, out_shape=tuple(out_shape), grid=(L // tm,), in_specs=in_specs,
                         out_specs=tuple(out_specs), compiler_params=_cp("parallel"))(*args)
    return (out[0], out[1]) if want_h else (out[0], None)


def _norm_bwd(x, g, dy, *, name, add=None, out_dtype=f32, after=()):
    L, C = x.shape
    tm = _pick(L, 512)
    has_add = add is not None

    def body(x_ref, g_ref, dy_ref, *rest):
        rest = list(rest)
        add_ref = rest.pop(0) if has_add else None
        dx_ref, dg_ref, cs_ref = rest[-3:]
        i = pl.program_id(0)

        @pl.when(i == 0)
        def _():
            dg_ref[...] = jnp.zeros_like(dg_ref)
            cs_ref[...] = jnp.zeros_like(cs_ref)

        dx, dg = _rms_bwd_val(x_ref[...], g_ref[...], dy_ref[...].astype(f32))
        dg_ref[...] += dg
        cs_ref[...] += jnp.sum(dx, axis=0, keepdims=True)
        if has_add:
            dx = dx + add_ref[...]
        dx_ref[...] = dx.astype(out_dtype)

    in_specs = [_rows(tm, C), _const((1, C)), _rows(tm, C)]
    args = [x, g, dy]
    if has_add:
        in_specs.append(_rows(tm, C))
        args.append(add)
    in_specs += [ANY_SPEC] * len(after)
    args += list(after)
    return pl.pallas_call(body, name=name, out_shape=(S((L, C), out_dtype), S((1, C), f32), S((1, C), f32)),
                          grid=(L // tm,), in_specs=in_specs,
                          out_specs=(_rows(tm, C), _const((1, C)), _const((1, C))),
                          compiler_params=_cp("arbitrary"))(*args)


def _loss_fwd_bwd(y, target, *, name):
    L, C = y.shape
    tm = _pick(L, 512)

    def body(y_ref, t_ref, acc_ref, dy_ref):
        i = pl.program_id(0)

        @pl.when(i == 0)
        def _():
            acc_ref[...] = jnp.zeros_like(acc_ref)

        e = y_ref[...] - t_ref[...]
        rs = jnp.sum(e * e, axis=-1, keepdims=True)
        acc_ref[...] += jnp.broadcast_to(jnp.sum(rs, axis=0, keepdims=True), (1, LANE))
        dy_ref[...] = e * (1.0 / C)

    return pl.pallas_call(body, name=name, out_shape=(S((1, LANE), f32), S((L, C), f32)), grid=(L // tm,),
                          in_specs=[_rows(tm, C), _rows(tm, C)], out_specs=(_const((1, LANE)), _rows(tm, C)),
                          compiler_params=_cp("arbitrary"))(y, target)


def _halo_rows(K):
    return SUBLANE if K - 1 <= SUBLANE else 32


def _prev_halo_spec(tm, H, C):
    return pl.BlockSpec((H, C), lambda i: (jnp.maximum(i * (tm // H) - 1, 0), 0))


def _next_halo_spec(tm, H, C, L):
    return pl.BlockSpec((H, C), lambda i: (jnp.minimum((i + 1) * (tm // H), L // H - 1), 0))


def _down_views(ext, H, n, K):
    bases, views = {}, []
    for s in range(K):
        q, r = divmod(s, SUBLANE)
        if r not in bases:
            bases[r] = ext if r == 0 else pltpu.roll(ext, r, axis=0)
        views.append(bases[r][H - SUBLANE * q:H - SUBLANE * q + n])
    return views


def _up_views(ext, n, K):
    bases, views = {}, []
    for s in range(K):
        q, r = divmod(s, SUBLANE)
        if r not in bases:
            bases[r] = ext if r == 0 else pltpu.roll(ext, ext.shape[0] - r, axis=0)
        views.append(bases[r][SUBLANE * q:SUBLANE * q + n])
    return views


def _causal_conv(ext, H, w_ref, K):
    views = _down_views(ext, H, ext.shape[0] - H, K)
    acc = None
    for k in range(K):
        term = views[K - 1 - k] * w_ref[k:k + 1, :]
        acc = term if acc is None else acc + term
    return acc


def _anticausal_conv(ext, tm, w_ref, K):
    views = _up_views(ext, tm, K)
    acc = None
    for k in range(K):
        term = views[K - 1 - k] * w_ref[k:k + 1, :]
        acc = term if acc is None else acc + term
    return acc


def _tap_grads(dw_ref, d_cur, x_ext, H, K):
    views = _down_views(x_ext, H, d_cur.shape[0], K)
    for k in range(K):
        dw_ref[k:k + 1, :] += jnp.sum(d_cur * views[K - 1 - k], axis=0, keepdims=True)


def _fold_rows(x):
    acc = x[0:SUBLANE]
    for r in range(SUBLANE, x.shape[0], SUBLANE):
        acc = acc + x[r:r + SUBLANE]
    return acc


def _pad_taps(w, K):
    return jnp.pad(w, ((0, _halo_rows(K) - K), (0, 0)))


def _conv_act_fwd(x, w, b, *, K, act, name, out_dtype, tm_pref=256):
    L, C = x.shape
    H = _halo_rows(K)
    tm = _pick(L, tm_pref)
    Co = C if act == "silu" else C // 2

    rs = min(ROW_SUB, tm)

    def body(h_ref, x_ref, w_ref, b_ref, o_ref, ext_scr):
        i = pl.program_id(0)
        ext_scr[0:H] = jnp.where(i > 0, h_ref[...], 0.0)
        ext_scr[H:] = x_ref[...]
        for j in range(Co // LANE):
            lanes = [j] if act == "silu" else [j, j + Co // LANE]
            wb = [(w_ref[:, c * LANE:(c + 1) * LANE], b_ref[:, c * LANE:(c + 1) * LANE]) for c in lanes]
            for r0 in range(0, tm, rs):
                us = [_causal_conv(ext_scr[r0:r0 + rs + H, c * LANE:(c + 1) * LANE], H, wc, K) + bc
                      for c, (wc, bc) in zip(lanes, wb)]
                y = _silu(us[0]) if act == "silu" else _silu(us[0]) * us[1]
                o_ref[r0:r0 + rs, j * LANE:(j + 1) * LANE] = y.astype(out_dtype)

    return pl.pallas_call(body, name=name, out_shape=S((L, Co), out_dtype), grid=(L // tm,),
                          in_specs=[_prev_halo_spec(tm, H, C), _rows(tm, C), _const((H, C)), _const((1, C))],
                          out_specs=_rows(tm, Co), scratch_shapes=[pltpu.VMEM((H + tm, C), f32)],
                          compiler_params=_cp("parallel"))(x, x, w, b)


def _conv_act_bwd(x, dparts, w, b, *, K, act, name, tm_pref=256):
    L, C = x.shape
    H = _halo_rows(K)
    tm = _pick(L, tm_pref)
    nb = L // tm
    Co = C if act == "silu" else C // 2
    nparts = len(dparts)

    rs = min(ROW_SUB, tm)

    def body(hp_ref, x_ref, hn_ref, w_ref, b_ref, *rest):
        d_refs, dn_refs = rest[:nparts], rest[nparts:2 * nparts]
        dx_ref, dw_ref, db_ref, ext_scr, d_scr = rest[2 * nparts:]
        i = pl.program_id(0)

        @pl.when(i == 0)
        def _():
            dw_ref[...] = jnp.zeros_like(dw_ref)
            db_ref[...] = jnp.zeros_like(db_ref)

        ext_scr[0:H] = jnp.where(i > 0, hp_ref[...], 0.0)
        ext_scr[H:H + tm] = x_ref[...]
        ext_scr[H + tm:] = jnp.where(i < nb - 1, hn_ref[...], 0.0)
        off = 0
        for r, rn in zip(d_refs, dn_refs):
            d_scr[0:tm, off:off + r.shape[1]] = r[...].astype(f32)
            d_scr[tm:, off:off + r.shape[1]] = jnp.where(i < nb - 1, rn[...].astype(f32), 0.0)
            off += r.shape[1]
        for j in range(Co // LANE):
            lanes = [j] if act == "silu" else [j, j + Co // LANE]
            wb = [(w_ref[:, c * LANE:(c + 1) * LANE], b_ref[:, c * LANE:(c + 1) * LANE]) for c in lanes]
            db_acc = [jnp.zeros((SUBLANE, LANE), f32) for _ in lanes]
            dw_acc = [[jnp.zeros((SUBLANE, LANE), f32) for _ in range(K)] for _ in lanes]
            for r0 in range(0, tm, rs):
                xvs = [_down_views(ext_scr[r0:r0 + rs + 2 * H, c * LANE:(c + 1) * LANE], H, rs + H, K) for c in lanes]
                us = [sum(xv[K - 1 - k] * wc[k:k + 1, :] for k in range(K)) + bc for xv, (wc, bc) in zip(xvs, wb)]
                d = d_scr[r0:r0 + rs + H, j * LANE:(j + 1) * LANE]
                dus = [d * _dsilu(us[0])] if act == "silu" else [d * us[1] * _dsilu(us[0]), d * _silu(us[0])]
                for q, (c, xv, du, (wc, _)) in enumerate(zip(lanes, xvs, dus, wb)):
                    dx_ref[r0:r0 + rs, c * LANE:(c + 1) * LANE] = _anticausal_conv(du, rs, wc, K).astype(bf16)
                    du_cur = du[:rs]
                    db_acc[q] = db_acc[q] + _fold_rows(du_cur)
                    for k in range(K):
                        dw_acc[q][k] = dw_acc[q][k] + _fold_rows(du_cur * xv[K - 1 - k][:rs])
            for q, c in enumerate(lanes):
                db_ref[:, c * LANE:(c + 1) * LANE] += jnp.sum(db_acc[q], axis=0, keepdims=True)
                for k in range(K):
                    dw_ref[k:k + 1, c * LANE:(c + 1) * LANE] += jnp.sum(dw_acc[q][k], axis=0, keepdims=True)

    in_specs = [_prev_halo_spec(tm, H, C), _rows(tm, C), _next_halo_spec(tm, H, C, L), _const((H, C)), _const((1, C))]
    in_specs += [_rows(tm, p.shape[1]) for p in dparts] + [_next_halo_spec(tm, H, p.shape[1], L) for p in dparts]
    return pl.pallas_call(body, name=name, out_shape=(S((L, C), bf16), S((H, C), f32), S((1, C), f32)), grid=(nb,),
                          in_specs=in_specs, out_specs=(_rows(tm, C), _const((H, C)), _const((1, C))),
                          scratch_shapes=[pltpu.VMEM((tm + 2 * H, C), f32), pltpu.VMEM((tm + H, Co), f32)],
                          compiler_params=_cp("arbitrary"))(x, x, x, w, b, *dparts, *dparts)


def _cf_fwd(u, dw_w, dw_b, ln_g, ln_b, *, name):
    L, C2 = u.shape
    C = C2 // 2
    K, H = CF_KERNEL, _halo_rows(CF_KERNEL)
    tm = _pick(L, 256)

    rs = min(ROW_SUB, tm)
    nl = C // LANE

    def body(h_ref, u_ref, w_ref, b_ref, g_ref, lb_ref, c_ref, s_ref, u_scr):
        i = pl.program_id(0)
        u_scr[0:H] = jnp.where(i > 0, h_ref[...], 0.0)
        u_scr[H:] = u_ref[...]
        for j in range(nl):
            wj, bj = w_ref[:, j * LANE:(j + 1) * LANE], b_ref[:, j * LANE:(j + 1) * LANE]
            for r0 in range(0, tm, rs):
                glu = u_scr[r0:r0 + rs + H, j * LANE:(j + 1) * LANE] \
                    * _sigmoid(u_scr[r0:r0 + rs + H, (nl + j) * LANE:(nl + j + 1) * LANE])
                c_ref[r0:r0 + rs, j * LANE:(j + 1) * LANE] = _causal_conv(glu, H, wj, K) + bj
        c = c_ref[...]
        mu = jnp.mean(c, axis=-1, keepdims=True)
        xc = c - mu
        var = jnp.mean(xc * xc, axis=-1, keepdims=True)
        ln = xc * lax.rsqrt(var + EPS) * g_ref[...] + lb_ref[...]
        s_ref[...] = _silu(ln).astype(bf16)

    return pl.pallas_call(body, name=name, out_shape=(S((L, C), f32), S((L, C), bf16)), grid=(L // tm,),
                          in_specs=[_prev_halo_spec(tm, H, C2), _rows(tm, C2), _const((H, C)), _const((1, C)),
                                    _const((1, C)), _const((1, C))],
                          out_specs=(_rows(tm, C), _rows(tm, C)), scratch_shapes=[pltpu.VMEM((H + tm, C2), f32)],
                          compiler_params=_cp("parallel"))(u, u, dw_w, dw_b, ln_g, ln_b)


def _cf_ln_bwd(c, ln_g, ln_b, ds, *, name):
    L, C = c.shape
    tm = _pick(L, 512)

    def body(c_ref, g_ref, lb_ref, ds_ref, dc_ref, dg_ref, db_ref):
        i = pl.program_id(0)

        @pl.when(i == 0)
        def _():
            dg_ref[...] = jnp.zeros_like(dg_ref)
            db_ref[...] = jnp.zeros_like(db_ref)

        c = c_ref[...]
        mu = jnp.mean(c, axis=-1, keepdims=True)
        xc = c - mu
        r = lax.rsqrt(jnp.mean(xc * xc, axis=-1, keepdims=True) + EPS)
        xh = xc * r
        ln = xh * g_ref[...] + lb_ref[...]
        dln = ds_ref[...].astype(f32) * _dsilu(ln)
        dg_ref[...] += jnp.sum(dln * xh, axis=0, keepdims=True)
        db_ref[...] += jnp.sum(dln, axis=0, keepdims=True)
        dxh = dln * g_ref[...]
        dc_ref[...] = r * (dxh - jnp.mean(dxh, axis=-1, keepdims=True) - xh * jnp.mean(dxh * xh, axis=-1, keepdims=True))

    return pl.pallas_call(body, name=name, out_shape=(S((L, C), f32), S((1, C), f32), S((1, C), f32)), grid=(L // tm,),
                          in_specs=[_rows(tm, C), _const((1, C)), _const((1, C)), _rows(tm, C)],
                          out_specs=(_rows(tm, C), _const((1, C)), _const((1, C))),
                          compiler_params=_cp("arbitrary"))(c, ln_g, ln_b, ds)


def _cf_glu_bwd(u, dc, dw_w, *, name):
    L, C2 = u.shape
    C = C2 // 2
    K, H = CF_KERNEL, _halo_rows(CF_KERNEL)
    tm = _pick(L, 256)
    nb = L // tm

    rs = min(ROW_SUB, tm)
    nl = C // LANE

    fold = _fold_rows

    def body(uh_ref, u_ref, dc_ref, dch_ref, w_ref, du_ref, dw_ref, db_ref, dus_ref, u_scr, dc_scr):
        i = pl.program_id(0)

        @pl.when(i == 0)
        def _():
            dw_ref[...] = jnp.zeros_like(dw_ref)
            db_ref[...] = jnp.zeros_like(db_ref)
            dus_ref[...] = jnp.zeros_like(dus_ref)

        u_scr[0:H] = jnp.where(i > 0, uh_ref[...], 0.0)
        u_scr[H:] = u_ref[...]
        dc_scr[0:tm] = dc_ref[...]
        dc_scr[tm:] = jnp.where(i < nb - 1, dch_ref[...], 0.0)
        for j in range(nl):
            la, lg = slice(j * LANE, (j + 1) * LANE), slice((nl + j) * LANE, (nl + j + 1) * LANE)
            wj = w_ref[:, la]
            zero8 = jnp.zeros((SUBLANE, LANE), f32)
            dw_acc, db_acc, dua_acc, dug_acc = [zero8] * K, zero8, zero8, zero8
            for r0 in range(0, tm, rs):
                a_e = u_scr[r0:r0 + rs + H, la]
                sg = _sigmoid(u_scr[r0:r0 + rs + H, lg])
                dce = dc_scr[r0:r0 + rs + H, la]
                dglu = _anticausal_conv(dce, rs, wj, K)
                a_c, sg_c, dc_c = a_e[H:], sg[H:], dce[:rs]
                du_a = dglu * sg_c
                du_g = dglu * a_c * sg_c * (1.0 - sg_c)
                du_ref[r0:r0 + rs, la] = du_a.astype(bf16)
                du_ref[r0:r0 + rs, lg] = du_g.astype(bf16)
                dua_acc, dug_acc, db_acc = dua_acc + fold(du_a), dug_acc + fold(du_g), db_acc + fold(dc_c)
                views = _down_views(a_e * sg, H, rs, K)
                dw_acc = [dw_acc[k] + fold(dc_c * views[K - 1 - k]) for k in range(K)]
            dus_ref[:, la] += jnp.sum(dua_acc, axis=0, keepdims=True)
            dus_ref[:, lg] += jnp.sum(dug_acc, axis=0, keepdims=True)
            db_ref[:, la] += jnp.sum(db_acc, axis=0, keepdims=True)
            for k in range(K):
                dw_ref[k:k + 1, la] += jnp.sum(dw_acc[k], axis=0, keepdims=True)

    return pl.pallas_call(body, name=name,
                          out_shape=(S((L, C2), bf16), S((H, C), f32), S((1, C), f32), S((1, C2), f32)), grid=(nb,),
                          in_specs=[_prev_halo_spec(tm, H, C2), _rows(tm, C2), _rows(tm, C), _next_halo_spec(tm, H, C, L),
                                    _const((H, C))],
                          out_specs=(_rows(tm, C2), _const((H, C)), _const((1, C)), _const((1, C2))),
                          scratch_shapes=[pltpu.VMEM((H + tm, C2), f32), pltpu.VMEM((tm + H, C), f32)],
                          compiler_params=_cp("arbitrary"))(u, u, dc, dc, dw_w)


def _gated_norm_fwd(y, z, g, *, name):
    L, C = y.shape
    tm = _pick(L, 256)

    def body(y_ref, z_ref, g_ref, o_ref):
        o_ref[...] = _rms_val(y_ref[...] * _silu(z_ref[...]), g_ref[...]).astype(bf16)

    return pl.pallas_call(body, name=name, out_shape=S((L, C), bf16), grid=(L // tm,),
                          in_specs=[_rows(tm, C), _rows(tm, C), _const((1, C))], out_specs=_rows(tm, C),
                          compiler_params=_cp("parallel"))(y, z, g)


def _gated_norm_bwd(y, z, g, dyn, *, name):
    L, C = y.shape
    tm = _pick(L, 256)

    def body(y_ref, z_ref, g_ref, d_ref, dy_ref, dz_ref, dg_ref):
        i = pl.program_id(0)

        @pl.when(i == 0)
        def _():
            dg_ref[...] = jnp.zeros_like(dg_ref)

        y, z = y_ref[...], z_ref[...]
        sz = _silu(z)
        du, dg = _rms_bwd_val(y * sz, g_ref[...], d_ref[...].astype(f32))
        dg_ref[...] += dg
        dy_ref[...] = du * sz
        dz_ref[...] = (du * y * _dsilu(z)).astype(bf16)

    return pl.pallas_call(body, name=name, out_shape=(S((L, C), f32), S((L, C), bf16), S((1, C), f32)), grid=(L // tm,),
                          in_specs=[_rows(tm, C), _rows(tm, C), _const((1, C)), _rows(tm, C)],
                          out_specs=(_rows(tm, C), _rows(tm, C), _const((1, C))),
                          compiler_params=_cp("arbitrary"))(y, z, g, dyn)


_XA_SCALE = XA_HEAD_DIM ** -0.5


def _attn_fwd(q, kv, *, name):
    L, C = q.shape
    tm = _pick(L, 512)
    Dh = XA_HEAD_DIM

    def body(q_ref, kv_ref, o_ref):
        for h in range(XA_HEADS):
            qh = q_ref[:, h * Dh:(h + 1) * Dh]
            kh = kv_ref[:, h * Dh:(h + 1) * Dh]
            vh = kv_ref[:, C + h * Dh:C + (h + 1) * Dh]
            s = lax.dot_general(qh, kh, _DN["nt"], preferred_element_type=f32) * _XA_SCALE
            e = jnp.exp(s - jnp.max(s, axis=-1, keepdims=True))
            p = e / jnp.sum(e, axis=-1, keepdims=True)
            o_ref[:, h * Dh:(h + 1) * Dh] = jnp.dot(p.astype(bf16), vh, preferred_element_type=f32).astype(bf16)

    return pl.pallas_call(body, name=name, out_shape=S((L, C), bf16), grid=(L // tm,),
                          in_specs=[_rows(tm, C), _const((N_MEM, 2 * C))], out_specs=_rows(tm, C),
                          compiler_params=_cp("parallel"))(q, kv)


def _attn_bwd(q, kv, do, *, name):
    L, C = q.shape
    tm = _pick(L, 512)
    Dh = XA_HEAD_DIM

    def body(q_ref, kv_ref, do_ref, dq_ref, dkv_ref):
        i = pl.program_id(0)

        @pl.when(i == 0)
        def _():
            dkv_ref[...] = jnp.zeros_like(dkv_ref)

        for h in range(XA_HEADS):
            qh = q_ref[:, h * Dh:(h + 1) * Dh]
            kh = kv_ref[:, h * Dh:(h + 1) * Dh]
            vh = kv_ref[:, C + h * Dh:C + (h + 1) * Dh]
            doh = do_ref[:, h * Dh:(h + 1) * Dh]
            s = lax.dot_general(qh, kh, _DN["nt"], preferred_element_type=f32) * _XA_SCALE
            e = jnp.exp(s - jnp.max(s, axis=-1, keepdims=True))
            p = e / jnp.sum(e, axis=-1, keepdims=True)
            pb = p.astype(bf16)
            dkv_ref[:, C + h * Dh:C + (h + 1) * Dh] += lax.dot_general(pb, doh, _DN["tn"], preferred_element_type=f32)
            dp = lax.dot_general(doh, vh, _DN["nt"], preferred_element_type=f32)
            ds = (p * (dp - jnp.sum(dp * p, axis=-1, keepdims=True)) * _XA_SCALE).astype(bf16)
            dq_ref[:, h * Dh:(h + 1) * Dh] = jnp.dot(ds, kh, preferred_element_type=f32).astype(bf16)
            dkv_ref[:, h * Dh:(h + 1) * Dh] += lax.dot_general(ds, qh, _DN["tn"], preferred_element_type=f32)

    return pl.pallas_call(body, name=name, out_shape=(S((L, C), bf16), S((N_MEM, 2 * C), f32)), grid=(L // tm,),
                          in_specs=[_rows(tm, C), _const((N_MEM, 2 * C)), _rows(tm, C)],
                          out_specs=(_rows(tm, C), _const((N_MEM, 2 * C))),
                          compiler_params=_cp("arbitrary"))(q, kv, do)


Q = CHUNK
PAIRS = HEADS_PER_GROUP // 2
GW = HEADS_PER_GROUP * HEAD_DIM


def _split(x, pieces):
    out = []
    for _ in range(pieces - 1):
        p = x.astype(bf16)
        out.append(p)
        x = x - p.astype(f32)
    return out + [x.astype(bf16)]


def _sel_right(x, sel, mode="nn", pieces=2):
    return sum(lax.dot_general(p, sel, _DN[mode], preferred_element_type=f32) for p in _split(x, pieces))


def _sel_left(sel, x, pieces=3):
    return sum(lax.dot_general(sel, p, _DN["nn"], preferred_element_type=f32) for p in _split(x, pieces))


def _ssd_common(dt_ref, hp_ref):
    dt_pre = dt_ref[...] + hp_ref[0:1, :]
    dt = _softplus(dt_pre)
    A = -jnp.exp(hp_ref[1:2, :])
    a = dt * A
    row = lax.broadcasted_iota(jnp.int32, (Q, Q), 0)
    col = lax.broadcasted_iota(jnp.int32, (Q, Q), 1)
    tri = row >= col
    cs = _sel_left(tri.astype(bf16), a)
    T = cs[Q - 1:Q, :]
    return dict(dt_pre=dt_pre, dt=dt, A=A, cs=cs, csT=cs.T, T=T, ecs=jnp.exp(cs), eend=jnp.exp(T - cs), eT=jnp.exp(T),
                tri=tri, row=row, col=col)


def _pair_expand(v, hA, lo):
    return jnp.where(lo, v[:, hA:hA + 1], v[:, hA + 1:hA + 2])


def _decay(cm, h):
    seg = cm["cs"][:, h:h + 1] - cm["csT"][h:h + 1, :]
    return jnp.where(cm["tri"], jnp.exp(jnp.where(cm["tri"], seg, 0.0)), 0.0)


def _decay_t(cm, h):
    keep = cm["row"] <= cm["col"]
    seg = cm["csT"][h:h + 1, :] - cm["cs"][:, h:h + 1]
    return jnp.where(keep, jnp.exp(jnp.where(keep, seg, 0.0)), 0.0)


ALL_PAIRS = N_SSM_HEADS // 2
GN = N_GROUPS * D_STATE


def _ssd_fwd(act, dtp, hp, *, name):
    L = act.shape[0]
    nc = L // Q

    def body(xs_ref, b_ref, c_ref, dt_ref, hp_ref, y_ref, hs_ref, h_scr):
        c = pl.program_id(0)

        @pl.when(c == 0)
        def _():
            h_scr[...] = jnp.zeros_like(h_scr)

        cm = _ssd_common(dt_ref, hp_ref)
        lo = lax.broadcasted_iota(jnp.int32, (Q, LANE), 1) < HEAD_DIM
        top = lax.broadcasted_iota(jnp.int32, (LANE, LANE), 0) < HEAD_DIM
        Drow = hp_ref[2:3, :]
        for g in range(N_GROUPS):
            Bb = b_ref[:, g * D_STATE:(g + 1) * D_STATE].astype(bf16)
            Cb = c_ref[:, g * D_STATE:(g + 1) * D_STATE].astype(bf16)
            CB = lax.dot_general(Cb, Bb, _DN["nt"], preferred_element_type=f32)
            for jj in range(PAIRS):
                p = g * PAIRS + jj
                hA, hB = 2 * p, 2 * p + 1
                dtx, ecsx, eendx = (_pair_expand(cm[k], hA, lo) for k in ("dt", "ecs", "eend"))
                xs_p = xs_ref[:, p * LANE:(p + 1) * LANE]
                Xd = xs_p * dtx
                Y = None
                for h, Xm in ((hA, jnp.where(lo, Xd, 0.0)), (hB, jnp.where(lo, 0.0, Xd))):
                    W = (CB * _decay(cm, h)).astype(bf16)
                    t = jnp.dot(W, Xm.astype(bf16), preferred_element_type=f32)
                    Y = t if Y is None else Y + t
                Hp = h_scr[p]
                hs_ref[0, p] = Hp
                Yoff = lax.dot_general(Cb, Hp.astype(bf16), _DN["nt"], preferred_element_type=f32) * ecsx
                Dx = jnp.where(lo[0:1, :], Drow[:, hA:hA + 1], Drow[:, hB:hB + 1])
                y_ref[:, p * LANE:(p + 1) * LANE] = Y + Yoff + xs_p * Dx
                Snew = lax.dot_general((Xd * eendx).astype(bf16), Bb, _DN["tn"], preferred_element_type=f32)
                eTx = jnp.where(top, cm["eT"][:, hA:hA + 1], cm["eT"][:, hB:hB + 1])
                h_scr[p] = Hp * eTx + Snew

    return pl.pallas_call(
        body, name=name, out_shape=(S((L, D_INNER), f32), S((nc, ALL_PAIRS, LANE, D_STATE), f32)),
        grid=(nc,),
        in_specs=[pl.BlockSpec((Q, D_INNER), lambda c: (c, 0)),
                  pl.BlockSpec((Q, GN), lambda c: (c, D_INNER // GN)),
                  pl.BlockSpec((Q, GN), lambda c: (c, D_INNER // GN + 1)),
                  pl.BlockSpec((Q, LANE), lambda c: (c, 0)),
                  pl.BlockSpec((SUBLANE, LANE), lambda c: (0, 0))],
        out_specs=(pl.BlockSpec((Q, D_INNER), lambda c: (c, 0)),
                   pl.BlockSpec((1, ALL_PAIRS, LANE, D_STATE), lambda c: (c, 0, 0, 0))),
        scratch_shapes=[pltpu.VMEM((ALL_PAIRS, LANE, D_STATE), f32)],
        compiler_params=_cp("arbitrary"))(act, act, act, dtp, hp)


def _ssd_bwd(act, dtp, hp, dy, hs, *, name):
    L = act.shape[0]
    nc = L // Q

    def body(xs_ref, b_ref, c_ref, dt_ref, hp_ref, dy_ref, hs_ref, dxs_ref, db_ref, dc_ref, ddt_ref, dhp_ref, dh_scr):
        c = pl.program_id(0)

        @pl.when(c == 0)
        def _():
            dh_scr[...] = jnp.zeros_like(dh_scr)
            dhp_ref[...] = jnp.zeros_like(dhp_ref)

        cm = _ssd_common(dt_ref, hp_ref)
        lane = lax.broadcasted_iota(jnp.int32, (Q, LANE), 1)
        sub = lax.broadcasted_iota(jnp.int32, (LANE, LANE), 0)
        lo = lane < HEAD_DIM
        top = sub < HEAD_DIM
        Drow = hp_ref[2:3, :]
        zero = jnp.zeros((Q, LANE), f32)
        dcs, dcsT, ddtx = zero, zero, zero
        dD_row = jnp.zeros((1, LANE), f32)
        dT_row = jnp.zeros((1, LANE), f32)
        for g in range(N_GROUPS):
            Bb = b_ref[:, g * D_STATE:(g + 1) * D_STATE].astype(bf16)
            Cb = c_ref[:, g * D_STATE:(g + 1) * D_STATE].astype(bf16)
            CB = lax.dot_general(Cb, Bb, _DN["nt"], preferred_element_type=f32)
            CBT = lax.dot_general(Bb, Cb, _DN["nt"], preferred_element_type=f32)
            dC, dB, dCB = zero, zero, jnp.zeros((Q, Q), f32)
            for jj in range(PAIRS):
                p = g * PAIRS + jj
                hA, hB = 2 * p, 2 * p + 1
                Pj = (lane == jnp.where(top, hA, hB)).astype(bf16)
                dtx, ecsx, eendx = (_pair_expand(cm[k], hA, lo) for k in ("dt", "ecs", "eend"))
                xs_p = xs_ref[:, p * LANE:(p + 1) * LANE]
                dY_p = dy_ref[:, p * LANE:(p + 1) * LANE]
                Xd = xs_p * dtx
                Xdb = Xd.astype(bf16)
                Hp, dHn = hs_ref[0, p], dh_scr[p]
                Hb, dHb = Hp.astype(bf16), dHn.astype(bf16)
                EdYb = (dY_p * ecsx).astype(bf16)
                YoffN = lax.dot_general(Cb, Hb, _DN["nt"], preferred_element_type=f32)
                dC = dC + jnp.dot(EdYb, Hb, preferred_element_type=f32)
                dH_off = lax.dot_general(EdYb, Cb, _DN["tn"], preferred_element_type=f32)
                R = lax.dot_general(Bb, dHb, _DN["nt"], preferred_element_type=f32)
                Xe = Xd * eendx
                dB = dB + jnp.dot(Xe.astype(bf16), dHb, preferred_element_type=f32)
                dXd = R * eendx
                RXe = R * Xe
                dcs = dcs + _sel_right(dY_p * YoffN * ecsx - RXe, Pj)
                HH = dHn * Hp
                hh = [jnp.sum(jnp.sum(HH[r0:r0 + HEAD_DIM], axis=0, keepdims=True), axis=1, keepdims=True)
                      for r0 in (0, HEAD_DIM)]
                rxe_cols = jnp.broadcast_to(jnp.sum(RXe, axis=0, keepdims=True), (SUBLANE, LANE))
                dT_row = dT_row + _sel_right(rxe_cols, Pj, pieces=3)[0:1] \
                    + (jnp.where(lane[0:1] == hA, hh[0], 0.0) + jnp.where(lane[0:1] == hB, hh[1], 0.0)) * cm["eT"]
                for h, keep in ((hA, lo), (hB, jnp.logical_not(lo))):
                    M = _decay(cm, h)
                    Wf = CB * M
                    dYm = jnp.where(keep, dY_p, 0.0).astype(bf16)
                    dW = lax.dot_general(dYm, Xdb, _DN["nt"], preferred_element_type=f32)
                    WT = (CBT * _decay_t(cm, h)).astype(bf16)
                    dXd = dXd + jnp.dot(WT, dYm, preferred_element_type=f32)
                    Z = dW * Wf
                    dcs = dcs + _sel_right(Z, (lane == h).astype(bf16))
                    dcsT = dcsT + jnp.where(sub == h, jnp.sum(Z, axis=0, keepdims=True), 0.0)
                    dCB = dCB + dW * M
                Dx = jnp.where(lo[0:1, :], Drow[:, hA:hA + 1], Drow[:, hB:hB + 1])
                dxs_ref[:, p * LANE:(p + 1) * LANE] = dXd * dtx + dY_p * Dx
                ddtx = ddtx + _sel_right(dXd * xs_p, Pj)
                dD_cols = jnp.broadcast_to(jnp.sum(dY_p * xs_p, axis=0, keepdims=True), (SUBLANE, LANE))
                dD_row = dD_row + _sel_right(dD_cols, Pj, pieces=3)[0:1]
                eTx = jnp.where(top, cm["eT"][:, hA:hA + 1], cm["eT"][:, hB:hB + 1])
                dh_scr[p] = dHn * eTx + dH_off
            dCBb = dCB.astype(bf16)
            dc_ref[:, g * D_STATE:(g + 1) * D_STATE] = dC + jnp.dot(dCBb, Bb, preferred_element_type=f32)
            db_ref[:, g * D_STATE:(g + 1) * D_STATE] = dB + lax.dot_general(dCBb, Cb, _DN["tn"], preferred_element_type=f32)
        dcs = dcs - dcsT.T + jnp.where(lax.broadcasted_iota(jnp.int32, (Q, LANE), 0) == Q - 1, dT_row, 0.0)
        da = _sel_left((cm["row"] <= cm["col"]).astype(bf16), dcs)
        ddt_pre = (da * cm["A"] + ddtx) * _sigmoid(cm["dt_pre"])
        ddt_ref[...] = ddt_pre
        r8 = lax.broadcasted_iota(jnp.int32, (SUBLANE, LANE), 0)
        dhp_ref[...] += jnp.where(r8 == 0, jnp.sum(ddt_pre, axis=0, keepdims=True),
                                  jnp.where(r8 == 1, jnp.sum(da * cm["dt"], axis=0, keepdims=True) * cm["A"],
                                            jnp.where(r8 == 2, dD_row, 0.0)))

    rev = lambda c: nc - 1 - c
    return pl.pallas_call(
        body, name=name,
        out_shape=(S((L, D_INNER), f32), S((L, GN), f32), S((L, GN), f32), S((L, LANE), f32), S((SUBLANE, LANE), f32)),
        grid=(nc,),
        in_specs=[pl.BlockSpec((Q, D_INNER), lambda c: (rev(c), 0)),
                  pl.BlockSpec((Q, GN), lambda c: (rev(c), D_INNER // GN)),
                  pl.BlockSpec((Q, GN), lambda c: (rev(c), D_INNER // GN + 1)),
                  pl.BlockSpec((Q, LANE), lambda c: (rev(c), 0)),
                  pl.BlockSpec((SUBLANE, LANE), lambda c: (0, 0)),
                  pl.BlockSpec((Q, D_INNER), lambda c: (rev(c), 0)),
                  pl.BlockSpec((1, ALL_PAIRS, LANE, D_STATE), lambda c: (rev(c), 0, 0, 0))],
        out_specs=(pl.BlockSpec((Q, D_INNER), lambda c: (rev(c), 0)),
                   pl.BlockSpec((Q, GN), lambda c: (rev(c), 0)),
                   pl.BlockSpec((Q, GN), lambda c: (rev(c), 0)),
                   pl.BlockSpec((Q, LANE), lambda c: (rev(c), 0)),
                   pl.BlockSpec((SUBLANE, LANE), lambda c: (0, 0))),
        scratch_shapes=[pltpu.VMEM((ALL_PAIRS, LANE, D_STATE), f32)],
        compiler_params=_cp("arbitrary"))(act, act, act, dtp, hp, dy, hs)


def _group_pad_cols(w):
    return jnp.pad(w, [(0, 0)] * (w.ndim - 1) + [(0, LANE - N_SSM_HEADS)])


def _group_unpad_cols(w):
    return w[..., :N_SSM_HEADS]


def _row(v):
    return v.reshape(1, -1)


ROW_SHARDED = ('ssm_out_w', 'cf_pw2_w', 'xa_q_w', 'xa_o_w', 'ffn_out_w')
COL_SHARDED = ('cf_pw1_w', 'xa_kv_w', 'ffn_in_w')


MIXER_WEIGHTS = ('ssm_in_w', 'ssm_out_w', 'cf_pw1_w', 'cf_pw2_w')


def _layer_matmul_weights(i, part):
    if part == "mixer":
        return ('ssm_in_w', 'ssm_out_w') if i % 2 == 0 else ('cf_pw1_w', 'cf_pw2_w')
    return ('xa_q_w', 'xa_kv_w', 'xa_o_w', 'ffn_in_w', 'ffn_out_w')


def _device_step(x, mem, target, W, layer_weights, layer_grads, start_after=()):
    ng = W['norm_g']
    lw = []
    for i in range(DEPTH):
        j = i // 2
        p = {}
        if i % 2 == 0:
            p['cw'] = _pad_taps(W['ssm_conv_w'][j], SSM_CONV)
            p['cb'] = _row(W['ssm_conv_b'][j])
            hp = jnp.stack([_group_pad_cols(W['ssm_dt_bias'][j]), _group_pad_cols(W['ssm_A_log'][j]),
                            _group_pad_cols(W['ssm_D'][j])])
            p['hp'] = jnp.pad(hp, ((0, SUBLANE - 3), (0, 0)))
            p['sng'] = _row(W['ssm_norm_g'][j])
        else:
            p['pw1b'] = _row(W['cf_pw1_b'][j])
            p['dww'], p['dwb'] = _pad_taps(W['cf_dw_w'][j], CF_KERNEL), _row(W['cf_dw_b'][j])
            p['lng'], p['lnb'] = _row(W['cf_ln_g'][j]), _row(W['cf_ln_b'][j])
            p['pw2b'] = _row(W['cf_pw2_b'][j])
        p['memg'] = _row(W['xa_mem_g'][i])
        p['fcw'], p['fcb'] = _pad_taps(W['ffn_conv_w'][i], FFN_CONV), _row(W['ffn_conv_b'][i])
        p['g'] = [_row(ng[i, s]) for s in range(6)]
        lw.append(p)

    def wmm(a, wl, wname, mode, **kw):
        return _mm(a, wl[wname], mode, b_shards=wname in COL_SHARDED, **kw)

    saved = []
    X = x
    h = _rmsnorm_fwd(X, lw[0]['g'][0], name="norm_in", after=start_after)
    for i in range(DEPTH):
        p, sv = lw[i], {}
        wl = dict(layer_weights(i, "mixer", X))
        sv['X0'], sv['h'], sv['wl'] = X, h, wl
        if i % 2 == 0:
            win = jnp.concatenate([wl['ssm_in_w'][s] for s in range(N_CHIPS)], axis=1)
            wl['wz'], wl['wx'] = win[:, :D_INNER], win[:, D_INNER:D_INNER + CONV_DIM]
            wl['wdt'] = _group_pad_cols(win[:, D_INNER + CONV_DIM:])
            z = _mm(h, wl['wz'], "nn", name="ssm_z")
            xbc = _mm(h, wl['wx'], "nn", name="ssm_xbc")
            dtp = _mm(h, wl['wdt'], "nn", name="ssm_dt")
            act = _conv_act_fwd(xbc, p['cw'], p['cb'], K=SSM_CONV, act="silu", name="ssm_conv_fwd", out_dtype=f32)
            y, hs = _ssd_fwd(act, dtp, p['hp'], name="ssd_fwd")
            yn = _gated_norm_fwd(y, z, p['sng'], name="ssm_gnorm_fwd")
            mix = wmm(yn, wl, 'ssm_out_w', "nn", name="ssm_out")
            sv.update(z=z, xbc=xbc, dtp=dtp, act=act, y=y, hs=hs, yn=yn)
        else:
            u = wmm(h, wl, 'cf_pw1_w', "nn", name="cf_pw1", bias=p['pw1b'])
            c, s = _cf_fwd(u, p['dww'], p['dwb'], p['lng'], p['lnb'], name="cf_conv_fwd")
            mix = wmm(s, wl, 'cf_pw2_w', "nn", name="cf_pw2", bias=p['pw2b'])
            sv.update(u=u, c=c, s=s)
        wl.update(layer_weights(i, "rest", mix))
        X1, h2 = _resid_norm_fwd(X, mix, p['g'][1], p['g'][2], name="resid_norm_a")
        q = wmm(h2, wl, 'xa_q_w', "nn", name="xa_q", out_dtype=bf16)
        m = _rmsnorm_fwd(mem, p['memg'], name="xa_mem_norm")
        kv = wmm(m, wl, 'xa_kv_w', "nn", name="xa_kv", out_dtype=bf16)
        o = _attn_fwd(q, kv, name="xa_attn_fwd")
        a = wmm(o, wl, 'xa_o_w', "nn", name="xa_o")
        X2, h3 = _resid_norm_fwd(X1, a, p['g'][3], p['g'][4], name="resid_norm_b")
        u0 = wmm(h3, wl, 'ffn_in_w', "nn", name="ffn_in")
        fact = _conv_act_fwd(u0, p['fcw'], p['fcb'], K=FFN_CONV, act="swiglu", name="ffn_conv_fwd", out_dtype=bf16)
        f = wmm(fact, wl, 'ffn_out_w', "nn", name="ffn_out")
        g_next = lw[i + 1]['g'][0] if i + 1 < DEPTH else None
        X3, hn = _resid_norm_fwd(X2, f, p['g'][5], g_next, name="resid_norm_c" if g_next is not None else "resid_norm_last")
        sv.update(mix=mix, X1=X1, h2=h2, q=q, m=m, kv=kv, o=o, a=a, X2=X2, h3=h3, u0=u0, fact=fact, f=f)
        saved.append(sv)
        X, h = X3, hn

    sse, G = _loss_fwd_bwd(X, target, name="loss")

    small = [n for n in WEIGHT_NAMES if n not in MATMUL_WEIGHTS]
    gr = {n: [None] * W[n].shape[0] for n in small}

    def dwmm(gl, a, d, wname, *, name):
        if wname in COL_SHARDED:
            gl[wname] = _mm(a, d, "tn", name=name, out_dtype=bf16, out_shards=True)
        else:
            g = _mm(a, d, "tn", name=name, out_dtype=bf16)
            gl[wname] = g.reshape(N_CHIPS, g.shape[0] // N_CHIPS, g.shape[1])

    dng = [[None] * 6 for _ in range(DEPTH)]
    for i in reversed(range(DEPTH)):
        p, sv, j = lw[i], saved[i], i // 2
        wl, gl = sv['wl'], {}
        df, dng[i][5], _ = _norm_bwd(sv['f'], p['g'][5], G, name="nb_f", out_dtype=bf16)
        dwmm(gl, sv['fact'], df, 'ffn_out_w', name="ffn_out_dw")
        dfact = wmm(df, wl, 'ffn_out_w', "nt", name="ffn_out_dx")
        du0, dcw, dcb = _conv_act_bwd(sv['u0'], [dfact], p['fcw'], p['fcb'], K=FFN_CONV, act="swiglu", name="ffn_conv_bwd")
        gr['ffn_conv_w'][i], gr['ffn_conv_b'][i] = dcw[:FFN_CONV], dcb[0]
        dwmm(gl, sv['h3'], du0, 'ffn_in_w', name="ffn_in_dw")
        dh3 = wmm(du0, wl, 'ffn_in_w', "nt", name="ffn_in_dx")
        G, dng[i][4], _ = _norm_bwd(sv['X2'], p['g'][4], dh3, name="nb_x2", add=G)
        da, dng[i][3], _ = _norm_bwd(sv['a'], p['g'][3], G, name="nb_a", out_dtype=bf16)
        dwmm(gl, sv['o'], da, 'xa_o_w', name="xa_o_dw")
        do = wmm(da, wl, 'xa_o_w', "nt", name="xa_o_dx", out_dtype=bf16)
        dq, dkv = _attn_bwd(sv['q'], sv['kv'], do, name="xa_attn_bwd")
        dwmm(gl, sv['h2'], dq, 'xa_q_w', name="xa_q_dw")
        dh2 = wmm(dq, wl, 'xa_q_w', "nt", name="xa_q_dx")
        dwmm(gl, sv['m'], dkv, 'xa_kv_w', name="xa_kv_dw")
        dm = wmm(dkv, wl, 'xa_kv_w', "nt", name="xa_kv_dx")
        _, dmg, _ = _norm_bwd(mem, p['memg'], dm, name="nb_mem")
        gr['xa_mem_g'][i] = dmg[0]
        G, dng[i][2], _ = _norm_bwd(sv['X1'], p['g'][2], dh2, name="nb_x1", add=G)
        behind = tuple(layer_grads(i, "rest", gl))
        dmix, dng[i][1], dmix_sum = _norm_bwd(sv['mix'], p['g'][1], G, name="nb_mix", out_dtype=bf16, after=behind)
        if i % 2 == 0:
            dwmm(gl, sv['yn'], dmix, 'ssm_out_w', name="ssm_out_dw")
            dyn = wmm(dmix, wl, 'ssm_out_w', "nt", name="ssm_out_dx")
            dy, dz, dsng = _gated_norm_bwd(sv['y'], sv['z'], p['sng'], dyn, name="ssm_gnorm_bwd")
            gr['ssm_norm_g'][j] = dsng[0]
            dxs, dB, dC, ddtp, dhp = _ssd_bwd(sv['act'], sv['dtp'], p['hp'], dy, sv['hs'], name="ssd_bwd")
            gr['ssm_dt_bias'][j], gr['ssm_A_log'][j], gr['ssm_D'][j] = (_group_unpad_cols(dhp[r]) for r in range(3))
            dxbc, dcw, dcb = _conv_act_bwd(sv['xbc'], [dxs, dB, dC], p['cw'], p['cb'], K=SSM_CONV, act="silu",
                                           name="ssm_conv_bwd")
            gr['ssm_conv_w'][j], gr['ssm_conv_b'][j] = dcw[:SSM_CONV], dcb[0]
            hh = sv['h']
            dwz = _mm(hh, dz, "tn", name="ssm_z_dw", out_dtype=bf16)
            dwx = _mm(hh, dxbc, "tn", name="ssm_xbc_dw", out_dtype=bf16)
            dwdt = _mm(hh, ddtp, "tn", name="ssm_dt_dw", out_dtype=bf16)
            din = jnp.concatenate([dwz, dwx, _group_unpad_cols(dwdt)], axis=1)
            gl['ssm_in_w'] = jnp.stack(jnp.split(din, N_CHIPS, axis=1))
            behind = tuple(layer_grads(i, "mixer", gl))
            dh = _mm(dz, wl['wz'], "nt", name="ssm_z_dx", after=behind)
            dh = _mm(dxbc, wl['wx'], "nt", name="ssm_xbc_dx", add=dh)
            dh = _mm(ddtp, wl['wdt'], "nt", name="ssm_dt_dx", add=dh)
        else:
            dwmm(gl, sv['s'], dmix, 'cf_pw2_w', name="cf_pw2_dw")
            gr['cf_pw2_b'][j] = dmix_sum[0]
            ds = wmm(dmix, wl, 'cf_pw2_w', "nt", name="cf_pw2_dx")
            dc, dlg, dlb = _cf_ln_bwd(sv['c'], p['lng'], p['lnb'], ds, name="cf_ln_bwd")
            gr['cf_ln_g'][j], gr['cf_ln_b'][j] = dlg[0], dlb[0]
            du, ddw, ddb, dus = _cf_glu_bwd(sv['u'], dc, p['dww'], name="cf_glu_bwd")
            gr['cf_dw_w'][j], gr['cf_dw_b'][j], gr['cf_pw1_b'][j] = ddw[:CF_KERNEL], ddb[0], dus[0]
            dwmm(gl, sv['h'], du, 'cf_pw1_w', name="cf_pw1_dw")
            behind = tuple(layer_grads(i, "mixer", gl))
            dh = wmm(du, wl, 'cf_pw1_w', "nt", name="cf_pw1_dx", after=behind)
        G, dng[i][0], _ = _norm_bwd(sv['X0'], p['g'][0], dh, name="nb_x0", add=G)
    gr['norm_g'] = [jnp.concatenate(dng[i], axis=0) for i in range(DEPTH)]
    gsmall = {n: jnp.stack(gr[n]) for n in small}
    return sse, G, gsmall


MESH = pl.DeviceIdType.MESH
HBM_SPEC = pl.BlockSpec(memory_space=pltpu.HBM)


def _chip_peers(x, y):
    return [(1 - x, y), (x, 1 - y), (1 - x, 1 - y)]


def _all_gather_chips(buf, *, name):
    R, C = buf.shape

    def body(in_ref, out_ref, send_sems, recv_sems, local_sem):
        x, y, c = lax.axis_index("x"), lax.axis_index("y"), lax.axis_index("c")
        me = 2 * x + y
        mine = pltpu.make_async_copy(in_ref, out_ref.at[me], local_sem)
        mine.start()
        peers = _chip_peers(x, y)
        sends = []
        for k, (px, py) in enumerate(peers):
            cp = pltpu.make_async_remote_copy(src_ref=in_ref, dst_ref=out_ref.at[me], send_sem=send_sems.at[k],
                                              recv_sem=recv_sems.at[k], device_id=(px, py, c), device_id_type=MESH)
            cp.start()
            sends.append(cp)
        for k, (px, py) in enumerate(peers):
            pltpu.make_async_remote_copy(src_ref=in_ref, dst_ref=out_ref.at[2 * px + py], send_sem=send_sems.at[k],
                                         recv_sem=recv_sems.at[k], device_id=(px, py, c), device_id_type=MESH).wait_recv()
        for cp in sends:
            cp.wait_send()
        mine.wait()

    return pl.pallas_call(body, name=name, out_shape=S((N_CHIPS, R, C), buf.dtype), in_specs=[HBM_SPEC], out_specs=HBM_SPEC,
                          scratch_shapes=[pltpu.SemaphoreType.DMA((3,)), pltpu.SemaphoreType.DMA((3,)),
                                          pltpu.SemaphoreType.DMA(())])(buf)


def _remote(src, dst, send_sem, recv_sem, device):
    return pltpu.make_async_remote_copy(src_ref=src, dst_ref=dst, send_sem=send_sem, recv_sem=recv_sem,
                                        device_id=device, device_id_type=MESH)


def _gather_matmul_weights(shards, *, name):
    n = len(shards)

    def body(*refs):
        ins, outs = refs[:n], refs[n:2 * n]
        send, recv, fsend, frecv, lsem = refs[2 * n:]
        x, y, c = lax.axis_index("x"), lax.axis_index("y"), lax.axis_index("c")
        me, sib = 2 * x + y, (x, y, 1 - c)
        peers = _chip_peers(x, y)
        started, local = [], []
        for w in range(n):
            cp = pltpu.make_async_copy(ins[w], outs[w].at[:, me], lsem.at[w])
            cp.start()
            local.append(cp)
            for k, (px, py) in enumerate(peers):
                cp = _remote(ins[w].at[:, c], outs[w].at[:, me, c], send.at[w, k], recv.at[w, k], (px, py, c))
                cp.start()
                started.append(cp)
        for w in range(n):
            for k, (px, py) in enumerate(peers):
                landed = outs[w].at[:, 2 * px + py, c]
                _remote(ins[w].at[:, c], landed, send.at[w, k], recv.at[w, k], (px, py, c)).wait_recv()
                cp = _remote(landed, landed, fsend.at[w, k], frecv.at[w, k], sib)
                cp.start()
                started.append(cp)
        for w in range(n):
            for k, (px, py) in enumerate(peers):
                _remote(ins[w].at[:, c], outs[w].at[:, 2 * px + py, 1 - c], fsend.at[w, k], frecv.at[w, k], sib).wait_recv()
        for cp in started:
            cp.wait_send()
        for cp in local:
            cp.wait()

    out_shape = tuple(S((s.shape[0], N_CHIPS) + s.shape[1:], s.dtype) for s in shards)
    sems = [pltpu.SemaphoreType.DMA((n, 3)) for _ in range(4)] + [pltpu.SemaphoreType.DMA((n,))]
    return pl.pallas_call(body, name=name, out_shape=out_shape, in_specs=[HBM_SPEC] * n, out_specs=(HBM_SPEC,) * n,
                          scratch_shapes=sems)(*shards)


SEM_SPEC = pl.BlockSpec(memory_space=pltpu.SEMAPHORE)
VMEM_SPEC = pl.BlockSpec(memory_space=pltpu.VMEM)


def _in_hbm(a):
    return pltpu.with_memory_space_constraint(a, pltpu.HBM)


def _chip_targets(x, y):
    return [(x, y), (1 - x, y), (x, 1 - y), (1 - x, 1 - y)]


def _spread_start(srcs, scatter, *, name, after=()):
    n = len(srcs)
    lands = [lax.empty((N_CHIPS,) + (s.shape[1:] if scatter else s.shape), s.dtype) for s in srcs]

    def body(*refs):
        src, land = refs[:n], refs[n:2 * n]
        send, recv, token = refs[2 * n + len(after)], refs[2 * n + len(after) + 1], refs[-1]
        x, y, c = lax.axis_index("x"), lax.axis_index("y"), lax.axis_index("c")
        me = 2 * x + y
        for w in range(n):
            for k, (px, py) in enumerate(_chip_targets(x, y)):
                block = src[w].at[2 * px + py] if scatter else src[w]
                _remote(block, land[w].at[me], send.at[N_CHIPS * w + k], recv.at[N_CHIPS * w + k], (px, py, c)).start()
        token[...] = jnp.zeros_like(token)

    thru = tuple(pltpu.HBM(a.shape, a.dtype) for a in list(srcs) + lands)
    sems = (pltpu.SemaphoreType.DMA((N_CHIPS * n,)), pltpu.SemaphoreType.DMA((N_CHIPS * n,)))
    out = pl.pallas_call(
        body, name=name, out_shape=sems + thru + (S((SUBLANE, LANE), f32),),
        in_specs=[HBM_SPEC] * (2 * n) + [ANY_SPEC] * len(after),
        out_specs=(SEM_SPEC, SEM_SPEC) + (HBM_SPEC,) * (2 * n) + (VMEM_SPEC,),
        input_output_aliases={i: 2 + i for i in range(2 * n)},
        compiler_params=pltpu.CompilerParams(has_side_effects=pltpu.SideEffectType.DATAFLOW_SIDE_EFFECTING),
    )(*[_in_hbm(a) for a in list(srcs) + lands], *after)
    return out[0], out[1], out[2:2 + n], out[2 + n:2 + 2 * n], out[-1]


def _spread_wait(send, recv, srcs, lands, after, scatter, *, name):
    n = len(srcs)

    def body(*refs):
        src, land, send, recv = refs[:n], refs[n:2 * n], refs[2 * n], refs[2 * n + 1]
        x, y, c = lax.axis_index("x"), lax.axis_index("y"), lax.axis_index("c")
        me = 2 * x + y
        for w in range(n):
            for k, (px, py) in enumerate(_chip_targets(x, y)):
                block = src[w].at[me] if scatter else src[w]
                cp = _remote(block, land[w].at[2 * px + py], send.at[N_CHIPS * w + k], recv.at[N_CHIPS * w + k], (px, py, c))
                cp.wait_send()
                cp.wait_recv()

    thru = tuple(pltpu.HBM(a.shape, a.dtype) for a in list(srcs) + list(lands))
    out = pl.pallas_call(
        body, name=name, out_shape=thru,
        in_specs=[HBM_SPEC] * (2 * n) + [SEM_SPEC, SEM_SPEC] + [ANY_SPEC] * len(after), out_specs=(HBM_SPEC,) * (2 * n),
        input_output_aliases={i: i for i in range(2 * n)},
        compiler_params=pltpu.CompilerParams(has_side_effects=pltpu.SideEffectType.DATAFLOW_SIDE_EFFECTING),
    )(*srcs, *lands, send, recv, *after)
    return out[:n], out[n:]


def _swap_sibling(bufs, *, name):
    n = len(bufs)

    def body(*refs):
        src, out, send, recv = refs[:n], refs[n:2 * n], refs[-2], refs[-1]
        sib = (lax.axis_index("x"), lax.axis_index("y"), 1 - lax.axis_index("c"))
        copies = [_remote(src[w], out[w], send.at[w], recv.at[w], sib) for w in range(n)]
        for cp in copies:
            cp.start()
        for cp in copies:
            cp.wait()

    return pl.pallas_call(body, name=name, out_shape=tuple(S(a.shape, a.dtype) for a in bufs),
                          in_specs=[HBM_SPEC] * n, out_specs=(HBM_SPEC,) * n,
                          scratch_shapes=[pltpu.SemaphoreType.DMA((n,)), pltpu.SemaphoreType.DMA((n,))])(*bufs)


def _swap_start(bufs, *, name):
    n = len(bufs)
    lands = [lax.empty(b.shape, b.dtype) for b in bufs]

    def body(*refs):
        src, land, send, recv, token = refs[:n], refs[n:2 * n], refs[2 * n], refs[2 * n + 1], refs[-1]
        sib = (lax.axis_index("x"), lax.axis_index("y"), 1 - lax.axis_index("c"))
        for w in range(n):
            _remote(src[w], land[w], send.at[w], recv.at[w], sib).start()
        token[...] = jnp.zeros_like(token)

    thru = tuple(pltpu.HBM(a.shape, a.dtype) for a in list(bufs) + lands)
    out = pl.pallas_call(
        body, name=name,
        out_shape=(pltpu.SemaphoreType.DMA((n,)), pltpu.SemaphoreType.DMA((n,))) + thru + (S((SUBLANE, LANE), f32),),
        in_specs=[HBM_SPEC] * (2 * n), out_specs=(SEM_SPEC, SEM_SPEC) + (HBM_SPEC,) * (2 * n) + (VMEM_SPEC,),
        input_output_aliases={i: 2 + i for i in range(2 * n)},
        compiler_params=pltpu.CompilerParams(has_side_effects=pltpu.SideEffectType.DATAFLOW_SIDE_EFFECTING),
    )(*[_in_hbm(a) for a in list(bufs) + lands])
    return out[0], out[1], out[2:2 + n], out[2 + n:2 + 2 * n], out[-1]


def _swap_wait(send, recv, bufs, lands, after, *, name):
    n = len(bufs)

    def body(*refs):
        src, land, send, recv = refs[:n], refs[n:2 * n], refs[2 * n], refs[2 * n + 1]
        sib = (lax.axis_index("x"), lax.axis_index("y"), 1 - lax.axis_index("c"))
        for w in range(n):
            cp = _remote(src[w], land[w], send.at[w], recv.at[w], sib)
            cp.wait_send()
            cp.wait_recv()

    thru = tuple(pltpu.HBM(a.shape, a.dtype) for a in list(bufs) + list(lands))
    out = pl.pallas_call(
        body, name=name, out_shape=thru,
        in_specs=[HBM_SPEC] * (2 * n) + [SEM_SPEC, SEM_SPEC] + [ANY_SPEC] * len(after), out_specs=(HBM_SPEC,) * (2 * n),
        input_output_aliases={i: i for i in range(2 * n)},
        compiler_params=pltpu.CompilerParams(has_side_effects=pltpu.SideEffectType.DATAFLOW_SIDE_EFFECTING),
    )(*bufs, *lands, send, recv, *after)
    return out[:n], out[n:]


N_DEVICES = 8


def _allgather_devices(buf, *, name):
    R, C = buf.shape

    def body(in_ref, out_ref, send, recv, lsem):
        x, y, c = lax.axis_index("x"), lax.axis_index("y"), lax.axis_index("c")
        me = 4 * x + 2 * y + c
        mine = pltpu.make_async_copy(in_ref, out_ref.at[me], lsem)
        mine.start()
        flips = [(d >> 2 & 1, d >> 1 & 1, d & 1) for d in range(1, N_DEVICES)]
        peers = [(1 - x if fx else x, 1 - y if fy else y, 1 - c if fc else c) for fx, fy, fc in flips]
        sends = []
        for k, peer in enumerate(peers):
            cp = _remote(in_ref, out_ref.at[me], send.at[k], recv.at[k], peer)
            cp.start()
            sends.append(cp)
        for k, (px, py, pc) in enumerate(peers):
            _remote(in_ref, out_ref.at[4 * px + 2 * py + pc], send.at[k], recv.at[k], (px, py, pc)).wait_recv()
        for cp in sends:
            cp.wait_send()
        mine.wait()

    return pl.pallas_call(body, name=name, out_shape=S((N_DEVICES, R, C), buf.dtype), in_specs=[HBM_SPEC], out_specs=HBM_SPEC,
                          scratch_shapes=[pltpu.SemaphoreType.DMA((N_DEVICES - 1,)), pltpu.SemaphoreType.DMA((N_DEVICES - 1,)),
                                          pltpu.SemaphoreType.DMA(())])(buf)


def _sum_slots(buf, *, name):
    ns, R, C = buf.shape
    tr = _pick(R, 512)
    assert R % tr == 0

    def body(*refs):
        acc = refs[0][...]
        for r in refs[1:ns]:
            acc = acc + r[...]
        refs[ns][...] = acc

    specs = [pl.BlockSpec((None, tr, C), functools.partial(lambda s, i: (s, i, 0), s)) for s in range(ns)]
    return pl.pallas_call(body, name=name, out_shape=S((R, C), buf.dtype), grid=(R // tr,), in_specs=specs,
                          out_specs=pl.BlockSpec((tr, C), lambda i: (i, 0)), compiler_params=_cp("parallel"))(*([buf] * ns))


ADAMW_BLOCK_BYTES = 1 << 20


def _adamw(w, m, v, groups, *, name, layer=None, prev=None):
    shape = w.shape if layer is None else w.shape[1:]
    C = shape[-1]
    Rr = math.prod(shape[:-1])
    tr = Rr
    if Rr * C * 4 > ADAMW_BLOCK_BYTES:
        tr = max(t for t in range(2 * SUBLANE, Rr + 1, 2 * SUBLANE) if Rr % t == 0 and t * C * 4 <= ADAMW_BLOCK_BYTES)
    c1 = 1.0 / (1.0 - ADAM_B1 ** ADAM_STEP)
    c2 = 1.0 / (1.0 - ADAM_B2 ** ADAM_STEP)
    if layer is None:
        to2 = lambda t: t.reshape(Rr, C)
        spec = pl.BlockSpec((tr, C), lambda i: (i, 0))
        res_shape = S((Rr, C), f32)
    else:
        to2 = lambda t: t.reshape(layer[1], Rr, C)
        spec = pl.BlockSpec((None, tr, C), functools.partial(lambda l, i: (l, i, 0), layer[0]))
        res_shape = S((layer[1], Rr, C), f32)
    wspec, spec = spec, pl.BlockSpec((tr, C), lambda i: (i, 0))
    g_specs, g_args, sizes = [], [], []
    for grp in groups:
        sizes.append(len(grp))
        for term in grp:
            if isinstance(term, tuple):
                arr, slot = term
                g_specs.append(pl.BlockSpec((None, tr, C), functools.partial(lambda s, i: (s, i, 0), slot)))
                g_args.append(arr.reshape(arr.shape[0], Rr, C))
            else:
                g_specs.append(spec)
                g_args.append(term.reshape(Rr, C))
    nterms = len(g_args)
    prev = () if prev is None else tuple(to2(t) for t in prev)

    def body(w_ref, m_ref, v_ref, *rest):
        t_refs, (g_ref, d_ref, mo_ref, vo_ref) = rest[:nterms], rest[-4:]
        g, pos = None, 0
        for size in sizes:
            part = None
            for r in t_refs[pos:pos + size]:
                t = r[...].astype(f32)
                part = t if part is None else part + t
            pos += size
            g = part if g is None else g + part
        mn = ADAM_B1 * m_ref[...] + (1.0 - ADAM_B1) * g
        vn = ADAM_B2 * v_ref[...] + (1.0 - ADAM_B2) * (g * g)
        g_ref[...] = g
        mo_ref[...] = mn
        vo_ref[...] = vn
        d_ref[...] = -ADAM_LR * ((mn * c1) / (jnp.sqrt(vn * c2) + ADAM_EPS) + ADAM_WD * w_ref[...])

    out = pl.pallas_call(body, name=name, out_shape=(res_shape,) * 4, grid=(Rr // tr,),
                         in_specs=[wspec] * 3 + g_specs + [ANY_SPEC] * len(prev), out_specs=(wspec,) * 4,
                         input_output_aliases={3 + nterms + k: k for k in range(len(prev))},
                         compiler_params=_cp("parallel"))(to2(w), to2(m), to2(v), *g_args, *prev)
    return tuple(o.reshape(w.shape) for o in out)


def _pack_rows(parts, dtype):
    flat = jnp.concatenate([p.reshape(-1).astype(dtype) for p in parts])
    n = flat.shape[0]
    unit = PACK_COLS * 2 * SUBLANE
    padded = -(-n // unit) * unit
    return jnp.pad(flat, (0, padded - n)).reshape(padded // PACK_COLS, PACK_COLS)


def _unpack_rows(flat2d, shapes):
    flat = flat2d.reshape(-1)
    out, off = [], 0
    for shp in shapes:
        n = math.prod(shp)
        out.append(flat[off:off + n].reshape(shp))
        off += n
    return out


def _gather_weights(local, names, dtype, *, name):
    shapes = [local[n].shape for n in names]
    got = _all_gather_chips(_pack_rows([local[n] for n in names], dtype), name=name)
    per_chip = [_unpack_rows(got[s], shapes) for s in range(N_CHIPS)]
    return {n: jnp.concatenate([per_chip[s][k] for s in range(N_CHIPS)], axis=SHARD_AXIS[n]) for k, n in enumerate(names)}


def kernel(x, mem, norm_g, ssm_in_w, ssm_conv_w, ssm_conv_b, ssm_dt_bias, ssm_A_log, ssm_D, ssm_norm_g, ssm_out_w, cf_pw1_w, cf_pw1_b, cf_dw_w, cf_dw_b, cf_ln_g, cf_ln_b, cf_pw2_w, cf_pw2_b, xa_mem_g, xa_q_w, xa_kv_w, xa_o_w, ffn_in_w, ffn_conv_w, ffn_conv_b, ffn_out_w, loss_target, m_norm_g, m_ssm_in_w, m_ssm_conv_w, m_ssm_conv_b, m_ssm_dt_bias, m_ssm_A_log, m_ssm_D, m_ssm_norm_g, m_ssm_out_w, m_cf_pw1_w, m_cf_pw1_b, m_cf_dw_w, m_cf_dw_b, m_cf_ln_g, m_cf_ln_b, m_cf_pw2_w, m_cf_pw2_b, m_xa_mem_g, m_xa_q_w, m_xa_kv_w, m_xa_o_w, m_ffn_in_w, m_ffn_conv_w, m_ffn_conv_b, m_ffn_out_w, v_norm_g, v_ssm_in_w, v_ssm_conv_w, v_ssm_conv_b, v_ssm_dt_bias, v_ssm_A_log, v_ssm_D, v_ssm_norm_g, v_ssm_out_w, v_cf_pw1_w, v_cf_pw1_b, v_cf_dw_w, v_cf_dw_b, v_cf_ln_g, v_cf_ln_b, v_cf_pw2_w, v_cf_pw2_b, v_xa_mem_g, v_xa_q_w, v_xa_kv_w, v_xa_o_w, v_ffn_in_w, v_ffn_conv_w, v_ffn_conv_b, v_ffn_out_w):
    w_local = dict(zip(WEIGHT_NAMES, (norm_g, ssm_in_w, ssm_conv_w, ssm_conv_b, ssm_dt_bias, ssm_A_log, ssm_D, ssm_norm_g,
                                      ssm_out_w, cf_pw1_w, cf_pw1_b, cf_dw_w, cf_dw_b, cf_ln_g, cf_ln_b, cf_pw2_w, cf_pw2_b,
                                      xa_mem_g, xa_q_w, xa_kv_w, xa_o_w, ffn_in_w, ffn_conv_w, ffn_conv_b, ffn_out_w)))
    m_local = dict(zip(WEIGHT_NAMES, (m_norm_g, m_ssm_in_w, m_ssm_conv_w, m_ssm_conv_b, m_ssm_dt_bias, m_ssm_A_log, m_ssm_D,
                                      m_ssm_norm_g, m_ssm_out_w, m_cf_pw1_w, m_cf_pw1_b, m_cf_dw_w, m_cf_dw_b, m_cf_ln_g,
                                      m_cf_ln_b, m_cf_pw2_w, m_cf_pw2_b, m_xa_mem_g, m_xa_q_w, m_xa_kv_w, m_xa_o_w,
                                      m_ffn_in_w, m_ffn_conv_w, m_ffn_conv_b, m_ffn_out_w)))
    v_local = dict(zip(WEIGHT_NAMES, (v_norm_g, v_ssm_in_w, v_ssm_conv_w, v_ssm_conv_b, v_ssm_dt_bias, v_ssm_A_log, v_ssm_D,
                                      v_ssm_norm_g, v_ssm_out_w, v_cf_pw1_w, v_cf_pw1_b, v_cf_dw_w, v_cf_dw_b, v_cf_ln_g,
                                      v_cf_ln_b, v_cf_pw2_w, v_cf_pw2_b, v_xa_mem_g, v_xa_q_w, v_xa_kv_w, v_xa_o_w,
                                      v_ffn_in_w, v_ffn_conv_w, v_ffn_conv_b, v_ffn_out_w)))

    small = [n for n in WEIGHT_NAMES if n not in MATMUL_WEIGHTS]
    small_sharded = [n for n in small if SHARD_AXIS[n] is not None]
    W = {n: w_local[n] for n in small if SHARD_AXIS[n] is None}
    W.update(_gather_weights(w_local, small_sharded, f32, name="gather_small_weights"))

    def layer_index(n, i):
        return i // 2 if n in MIXER_WEIGHTS else i

    def keys_of(i, parts):
        return [(n, layer_index(n, i)) for part in parts for n in _layer_matmul_weights(i, part)]

    def shards(keys):
        return [w_local[n][l].astype(bf16) for n, l in keys]

    def usable(n, a):
        return a.reshape(N_CHIPS * a.shape[1], a.shape[2]) if n in ROW_SHARDED else a

    mixer0 = keys_of(0, ("mixer",))
    got0 = _gather_matmul_weights([s.reshape(1, 2, s.shape[0] // 2, s.shape[1]) for s in shards(mixer0)],
                                  name="gather_layer0_mixer")
    gather_groups = {(0, "rest"): keys_of(0, ("rest",))}
    gather_groups.update({(i, "mixer"): keys_of(i, ("mixer", "rest")) for i in range(1, DEPTH)})
    gathers, tokens, landed = {}, [], {}
    for gkey in sorted(gather_groups):
        send, recv, srcs, lands, token = _spread_start(shards(gather_groups[gkey]), False, name="gather_start_%d_%s" % gkey,
                                                       after=(got0[0], W[small_sharded[0]]))
        gathers[gkey] = (send, recv, srcs, lands)
        tokens.append(token)

    def layer_weights(i, part, after):
        if (i, part) == (0, "mixer"):
            landed.update({k: g.reshape((N_CHIPS, 2 * g.shape[3], g.shape[4])) for k, g in zip(mixer0, got0)})
        elif (i, part) in gathers:
            _, lands = _spread_wait(*gathers[i, part], (after,), False, name="gather_wait_%d_%s" % (i, part))
            landed.update(zip(gather_groups[i, part], lands))
        return {n: usable(n, landed[n, layer_index(n, i)]) for n in _layer_matmul_weights(i, part)}

    pending, scatters, swaps, own, sib = {}, {}, {}, {}, {}

    def scatter_start(gkey, keys, gl):
        send, recv, srcs, lands, token = _spread_start([gl[k] for k in keys], True, name="grads_start_%d_%s" % gkey)
        scatters[gkey] = (keys, send, recv, srcs, lands)
        return token

    def layer_grads(i, part, gl):
        grads = {(n, layer_index(n, i)): gl[n] for n in _layer_matmul_weights(i, part)}
        behind = []
        if part == "rest":
            if i + 1 < DEPTH:
                keys, send, recv, srcs, lands = scatters.pop((i + 1, "mixer"))
                _, lands = _spread_wait(send, recv, srcs, lands, (grads['xa_kv_w', i],), True,
                                        name="grads_wait_%d" % (i + 1))
                send, recv, srcs, lands, token = _swap_start(lands, name="grads_swap_start_%d" % (i + 1))
                swaps[i + 1] = (keys, send, recv, srcs, lands)
                behind.append(token)
            if i == 0:
                behind.append(scatter_start((0, "rest"), list(grads), grads))
            else:
                pending.update(grads)
        else:
            pending.update(grads)
            if i == 0:
                keys, send, recv, srcs, lands = scatters.pop((0, "rest"))
                _, lands = _spread_wait(send, recv, srcs, lands, (grads['ssm_in_w', 0],), True, name="grads_wait_0_rest")
                send, recv, srcs, lands, token = _swap_start(lands, name="grads_swap_start_0")
                swaps[0] = (keys, send, recv, srcs, lands)
                behind.append(token)
            behind.append(scatter_start((i, "mixer"), list(pending), dict(pending)))
            pending.clear()
        return behind

    sse, gx, gsmall = _device_step(x[0], mem[0], loss_target[0], W, layer_weights, layer_grads, tuple(tokens))

    loss = lax.psum(0.5 * sse[0, 0] / D_MODEL, ("x", "y", "c"))

    last_keys, last_lands = [], []
    for gkey in sorted(scatters):
        keys, send, recv, srcs, lands = scatters[gkey]
        _, lands = _spread_wait(send, recv, srcs, lands, (gx,), True, name="grads_wait_%d_%s" % gkey)
        last_keys += keys
        last_lands += list(lands)
    own.update(zip(last_keys, last_lands))
    sib.update(zip(last_keys, _swap_sibling(last_lands, name="grads_swap_last")))
    for i in sorted(swaps):
        keys, send, recv, srcs, lands = swaps[i]
        mine, theirs = _swap_wait(send, recv, srcs, lands, (gx,), name="grads_swap_wait_%d" % i)
        own.update(zip(keys, mine))
        sib.update(zip(keys, theirs))
    small_shapes = [gsmall[n].shape for n in small]
    slots = _allgather_devices(_pack_rows([gsmall[n] for n in small], f32), name="allgather_small_grads")
    gsum = dict(zip(small, _unpack_rows(_sum_slots(slots, name="sum_small_grads"), small_shapes)))
    chip = 2 * lax.axis_index("x") + lax.axis_index("y")

    res = {}
    for n in MATMUL_WEIGHTS:
        layers, out = w_local[n].shape[0], None
        for l in range(layers):
            groups = [[(own[n, l], s) for s in range(N_CHIPS)], [(sib[n, l], s) for s in range(N_CHIPS)]]
            out = _adamw(w_local[n], m_local[n], v_local[n], groups, name="adamw_%s_%d" % (n, l), layer=(l, layers), prev=out)
        res[n] = out
    for n in small:
        g, ax = gsum[n], SHARD_AXIS[n]
        if ax is not None:
            width = w_local[n].shape[ax]
            g = lax.dynamic_slice_in_dim(g, chip * width, width, axis=ax)
        res[n] = _adamw(w_local[n], m_local[n], v_local[n], [[g]], name="adamw_" + n)
    return (loss, gx[None], *[res[n][0] for n in WEIGHT_NAMES], *[res[n][1] for n in WEIGHT_NAMES],
            *[res[n][2] for n in WEIGHT_NAMES], *[res[n][3] for n in WEIGHT_NAMES])
```

```python
import functools
import math

import jax
import jax.numpy as jnp
from jax import lax
from jax.experimental import pallas as pl
from jax.experimental.pallas import tpu as pltpu

f32 = jnp.float32
bf16 = jnp.bfloat16
S = jax.ShapeDtypeStruct

D_MODEL = 1024
DEPTH = 4
D_INNER = 2048
HEAD_DIM = 64
N_GROUPS = 4
HEADS_PER_GROUP = 8
N_SSM_HEADS = 32
D_STATE = 128
CHUNK = 128
SSM_CONV = 4
CONV_DIM = 3072
CF_KERNEL = 31
N_MEM = 256
XA_HEADS = 4
XA_HEAD_DIM = 256
D_FF = 2816
FFN_CONV = 3
EPS = 1e-6
ADAM_LR, ADAM_B1, ADAM_B2, ADAM_EPS, ADAM_WD, ADAM_STEP = 0.001, 0.9, 0.999, 1e-08, 0.01, 10

LANE = 128
SUBLANE = 8
ROW_SUB = 64
VMEM_LIMIT = 56 * 1024 * 1024
N_CHIPS = 4
PACK_COLS = 1024

WEIGHT_NAMES = ['norm_g', 'ssm_in_w', 'ssm_conv_w', 'ssm_conv_b', 'ssm_dt_bias', 'ssm_A_log', 'ssm_D', 'ssm_norm_g',
                'ssm_out_w', 'cf_pw1_w', 'cf_pw1_b', 'cf_dw_w', 'cf_dw_b', 'cf_ln_g', 'cf_ln_b', 'cf_pw2_w', 'cf_pw2_b',
                'xa_mem_g', 'xa_q_w', 'xa_kv_w', 'xa_o_w', 'ffn_in_w', 'ffn_conv_w', 'ffn_conv_b', 'ffn_out_w']
SHARD_AXIS = {'norm_g': 2, 'ssm_in_w': 2, 'ssm_conv_w': 2, 'ssm_conv_b': None, 'ssm_dt_bias': None, 'ssm_A_log': None,
              'ssm_D': None, 'ssm_norm_g': None, 'ssm_out_w': 1, 'cf_pw1_w': 2, 'cf_pw1_b': 1, 'cf_dw_w': 2, 'cf_dw_b': 1,
              'cf_ln_g': 1, 'cf_ln_b': 1, 'cf_pw2_w': 1, 'cf_pw2_b': 1, 'xa_mem_g': None, 'xa_q_w': 1, 'xa_kv_w': 2,
              'xa_o_w': 1, 'ffn_in_w': 2, 'ffn_conv_w': 2, 'ffn_conv_b': None, 'ffn_out_w': 1}
MATMUL_WEIGHTS = ('ssm_in_w', 'ssm_out_w', 'cf_pw1_w', 'cf_pw2_w', 'xa_q_w', 'xa_kv_w', 'xa_o_w', 'ffn_in_w', 'ffn_out_w')


def _cp(*sem):
    return pltpu.CompilerParams(dimension_semantics=tuple(sem), vmem_limit_bytes=VMEM_LIMIT)


def _pick(dim, pref):
    if dim <= pref:
        return dim
    best = None
    for t in range(LANE, pref + 1, LANE):
        if dim % t == 0:
            best = t
    assert best is not None, (dim, pref)
    return best


def _sigmoid(x):
    return 1.0 / (1.0 + jnp.exp(-x))


def _silu(x):
    return x * _sigmoid(x)


def _dsilu(x):
    s = _sigmoid(x)
    return s * (1.0 + x * (1.0 - s))


def _softplus(x):
    return jnp.maximum(x, 0.0) + jnp.log(1.0 + jnp.exp(-jnp.abs(x)))


_DN = {"nn": (((1,), (0,)), ((), ())), "nt": (((1,), (1,)), ((), ())), "tn": (((0,), (0,)), ((), ()))}


def _mm(a, b, mode, *, name, out_dtype=f32, bias=None, add=None, b_shards=False, out_shards=False, after=()):
    bshape = (b.shape[1], b.shape[2] * N_CHIPS) if b_shards else b.shape
    if mode == "nn":
        (M, K), (K2, N) = a.shape, bshape
    elif mode == "nt":
        (M, K), (N, K2) = a.shape, bshape
    else:
        (K, M), (K2, N) = a.shape, bshape
    assert K == K2, (a.shape, b.shape, mode)
    n_unit = N // N_CHIPS if ((b_shards and mode == "nn") or out_shards) else N
    k_unit = K // N_CHIPS if (b_shards and mode == "nt") else K
    tm, tn, tk = _pick(M, 1024), _pick(n_unit, 1408), _pick(k_unit, 1408)
    nk, nj_u, nk_u = K // tk, n_unit // tn, k_unit // tk
    a_spec = {"nn": pl.BlockSpec((tm, tk), lambda i, j, k: (i, k)), "nt": pl.BlockSpec((tm, tk), lambda i, j, k: (i, k)),
              "tn": pl.BlockSpec((tk, tm), lambda i, j, k: (k, i))}[mode]
    if not b_shards:
        b_spec = {"nn": pl.BlockSpec((tk, tn), lambda i, j, k: (k, j)), "nt": pl.BlockSpec((tn, tk), lambda i, j, k: (j, k)),
                  "tn": pl.BlockSpec((tk, tn), lambda i, j, k: (k, j))}[mode]
    else:
        b_spec = {"nn": pl.BlockSpec((None, tk, tn), lambda i, j, k: (j // nj_u, k, j % nj_u)),
                  "nt": pl.BlockSpec((None, tn, tk), lambda i, j, k: (k // nk_u, j, k % nk_u))}[mode]
    in_specs, args = [a_spec, b_spec], [a, b]
    if bias is not None:
        in_specs.append(pl.BlockSpec((1, tn), lambda i, j, k: (0, j)))
        args.append(bias)
    if add is not None:
        in_specs.append(pl.BlockSpec((tm, tn), lambda i, j, k: (i, j)))
        args.append(add)
    in_specs += [pl.BlockSpec(memory_space=pl.ANY)] * len(after)
    args += list(after)
    if not out_shards:
        out_shape, out_spec = S((M, N), out_dtype), pl.BlockSpec((tm, tn), lambda i, j, k: (i, j))
    else:
        out_shape = S((N_CHIPS, M, n_unit), out_dtype)
        out_spec = pl.BlockSpec((None, tm, tn), lambda i, j, k: (j // nj_u, i, j % nj_u))
    dn = _DN[mode]
    has_bias, has_add = bias is not None, add is not None

    def body(a_ref, b_ref, *rest):
        rest = list(rest)
        bias_ref = rest.pop(0) if has_bias else None
        add_ref = rest.pop(0) if has_add else None
        rest = rest[len(after):]
        o_ref = rest[0]

        def finish(r):
            if has_bias:
                r = r + bias_ref[...]
            if has_add:
                r = r + add_ref[...].astype(f32)
            o_ref[...] = r.astype(out_dtype)

        part = lax.dot_general(a_ref[...].astype(bf16), b_ref[...].astype(bf16), dn, preferred_element_type=f32)
        if nk == 1:
            finish(part)
            return
        acc_ref = rest[1]
        k = pl.program_id(2)

        @pl.when(k == 0)
        def _():
            acc_ref[...] = part

        @pl.when(k > 0)
        def _():
            acc_ref[...] += part

        @pl.when(k == nk - 1)
        def _():
            finish(acc_ref[...])

    return pl.pallas_call(
        body, name=name, out_shape=out_shape, grid=(M // tm, N // tn, nk),
        in_specs=in_specs, out_specs=out_spec, scratch_shapes=[pltpu.VMEM((tm, tn), f32)] if nk > 1 else [],
        compiler_params=_cp("parallel", "parallel", "arbitrary"))(*args)


def _rows(tm, C):
    return pl.BlockSpec((tm, C), lambda i: (i, 0))


def _const(shape):
    return pl.BlockSpec(shape, lambda i: tuple(0 for _ in shape))


def _rms_val(x, g):
    r = lax.rsqrt(jnp.mean(x * x, axis=-1, keepdims=True) + EPS)
    return x * r * g


def _rms_bwd_val(x, g, dy):
    r = lax.rsqrt(jnp.mean(x * x, axis=-1, keepdims=True) + EPS)
    xn = x * r
    dxh = dy * g
    dx = r * (dxh - xn * jnp.mean(dxh * xn, axis=-1, keepdims=True))
    return dx, jnp.sum(dy * xn, axis=0, keepdims=True)


ANY_SPEC = pl.BlockSpec(memory_space=pl.ANY)


def _rmsnorm_fwd(x, g, *, name, after=()):
    L, C = x.shape
    tm = _pick(L, 512)

    def body(x_ref, g_ref, *rest):
        rest[-1][...] = _rms_val(x_ref[...], g_ref[...]).astype(bf16)

    return pl.pallas_call(body, name=name, out_shape=S((L, C), bf16), grid=(L // tm,),
                          in_specs=[_rows(tm, C), _const((1, C))] + [ANY_SPEC] * len(after), out_specs=_rows(tm, C),
                          compiler_params=_cp("parallel"))(x, g, *after)


def _resid_norm_fwd(x, mix, g_post, g_next, *, name):
    L, C = x.shape
    tm = _pick(L, 512)
    want_h = g_next is not None

    def body(x_ref, m_ref, gp_ref, *rest):
        xn = x_ref[...] + _rms_val(m_ref[...], gp_ref[...])
        if want_h:
            gn_ref, xo_ref, h_ref = rest
            h_ref[...] = _rms_val(xn, gn_ref[...]).astype(bf16)
        else:
            (xo_ref,) = rest
        xo_ref[...] = xn

    in_specs = [_rows(tm, C), _rows(tm, C), _const((1, C))]
    args = [x, mix, g_post]
    out_shape, out_specs = [S((L, C), f32)], [_rows(tm, C)]
    if want_h:
        in_specs.append(_const((1, C)))
        args.append(g_next)
        out_shape.append(S((L, C), bf16))
        out_specs.append(_rows(tm, C))
    out = pl.pallas_call(body, name=name, out_shape=tuple(out_shape), grid=(L // tm,), in_specs=in_specs,
                         out_specs=tuple(out_specs), compiler_params=_cp("parallel"))(*args)
    return (out[0], out[1]) if want_h else (out[0], None)


def _norm_bwd(x, g, dy, *, name, add=None, out_dtype=f32, after=()):
    L, C = x.shape
    tm = _pick(L, 512)
    has_add = add is not None

    def body(x_ref, g_ref, dy_ref, *rest):
        rest = list(rest)
        add_ref = rest.pop(0) if has_add else None
        dx_ref, dg_ref, cs_ref = rest[-3:]
        i = pl.program_id(0)

        @pl.when(i == 0)
        def _():
            dg_ref[...] = jnp.zeros_like(dg_ref)
            cs_ref[...] = jnp.zeros_like(cs_ref)

        dx, dg = _rms_bwd_val(x_ref[...], g_ref[...], dy_ref[...].astype(f32))
        dg_ref[...] += dg
        cs_ref[...] += jnp.sum(dx, axis=0, keepdims=True)
        if has_add:
            dx = dx + add_ref[...]
        dx_ref[...] = dx.astype(out_dtype)

    in_specs = [_rows(tm, C), _const((1, C)), _rows(tm, C)]
    args = [x, g, dy]
    if has_add:
        in_specs.append(_rows(tm, C))
        args.append(add)
    in_specs += [ANY_SPEC] * len(after)
    args += list(after)
    return pl.pallas_call(body, name=name, out_shape=(S((L, C), out_dtype), S((1, C), f32), S((1, C), f32)),
                          grid=(L // tm,), in_specs=in_specs,
                          out_specs=(_rows(tm, C), _const((1, C)), _const((1, C))),
                          compiler_params=_cp("arbitrary"))(*args)


def _norm_bwd_chain(x, g, dy, add, x2, g2, *, name, after=()):
    L, C = x.shape
    tm = _pick(L, 512)

    def body(x_ref, g_ref, dy_ref, add_ref, x2_ref, g2_ref, *rest):
        G_ref, dg_ref, d2_ref, dg2_ref, cs2_ref = rest[-5:]
        i = pl.program_id(0)

        @pl.when(i == 0)
        def _():
            dg_ref[...] = jnp.zeros_like(dg_ref)
            dg2_ref[...] = jnp.zeros_like(dg2_ref)
            cs2_ref[...] = jnp.zeros_like(cs2_ref)

        dx, dg = _rms_bwd_val(x_ref[...], g_ref[...], dy_ref[...].astype(f32))
        G = dx + add_ref[...]
        dg_ref[...] += dg
        G_ref[...] = G
        d2, dg2 = _rms_bwd_val(x2_ref[...], g2_ref[...], G)
        dg2_ref[...] += dg2
        cs2_ref[...] += jnp.sum(d2, axis=0, keepdims=True)
        d2_ref[...] = d2.astype(bf16)

    row, vec = _rows(tm, C), _const((1, C))
    return pl.pallas_call(body, name=name,
                          out_shape=(S((L, C), f32), S((1, C), f32), S((L, C), bf16), S((1, C), f32), S((1, C), f32)),
                          grid=(L // tm,), in_specs=[row, vec, row, row, row, vec] + [ANY_SPEC] * len(after),
                          out_specs=(row, vec, row, vec, vec),
                          compiler_params=_cp("arbitrary"))(x, g, dy, add, x2, g2, *after)


def _loss_fwd_bwd(y, target, *, name):
    L, C = y.shape
    tm = _pick(L, 512)

    def body(y_ref, t_ref, acc_ref, dy_ref):
        i = pl.program_id(0)

        @pl.when(i == 0)
        def _():
            acc_ref[...] = jnp.zeros_like(acc_ref)

        e = y_ref[...] - t_ref[...]
        rs = jnp.sum(e * e, axis=-1, keepdims=True)
        acc_ref[...] += jnp.broadcast_to(jnp.sum(rs, axis=0, keepdims=True), (1, LANE))
        dy_ref[...] = e * (1.0 / C)

    return pl.pallas_call(body, name=name, out_shape=(S((1, LANE), f32), S((L, C), f32)), grid=(L // tm,),
                          in_specs=[_rows(tm, C), _rows(tm, C)], out_specs=(_const((1, LANE)), _rows(tm, C)),
                          compiler_params=_cp("arbitrary"))(y, target)


def _halo_rows(K):
    return SUBLANE if K - 1 <= SUBLANE else 32


def _prev_halo_spec(tm, H, C):
    return pl.BlockSpec((H, C), lambda i: (jnp.maximum(i * (tm // H) - 1, 0), 0))


def _next_halo_spec(tm, H, C, L):
    return pl.BlockSpec((H, C), lambda i: (jnp.minimum((i + 1) * (tm // H), L // H - 1), 0))


def _down_views(ext, H, n, K):
    bases, views = {}, []
    for s in range(K):
        q, r = divmod(s, SUBLANE)
        if r not in bases:
            bases[r] = ext if r == 0 else pltpu.roll(ext, r, axis=0)
        views.append(bases[r][H - SUBLANE * q:H - SUBLANE * q + n])
    return views


def _up_views(ext, n, K):
    bases, views = {}, []
    for s in range(K):
        q, r = divmod(s, SUBLANE)
        if r not in bases:
            bases[r] = ext if r == 0 else pltpu.roll(ext, ext.shape[0] - r, axis=0)
        views.append(bases[r][SUBLANE * q:SUBLANE * q + n])
    return views


def _causal_conv(ext, H, w_ref, K):
    views = _down_views(ext, H, ext.shape[0] - H, K)
    acc = None
    for k in range(K):
        term = views[K - 1 - k] * w_ref[k:k + 1, :]
        acc = term if acc is None else acc + term
    return acc


def _anticausal_conv(ext, tm, w_ref, K):
    views = _up_views(ext, tm, K)
    acc = None
    for k in range(K):
        term = views[K - 1 - k] * w_ref[k:k + 1, :]
        acc = term if acc is None else acc + term
    return acc


def _tap_grads(dw_ref, d_cur, x_ext, H, K):
    views = _down_views(x_ext, H, d_cur.shape[0], K)
    for k in range(K):
        dw_ref[k:k + 1, :] += jnp.sum(d_cur * views[K - 1 - k], axis=0, keepdims=True)


def _fold_rows(x):
    acc = x[0:SUBLANE]
    for r in range(SUBLANE, x.shape[0], SUBLANE):
        acc = acc + x[r:r + SUBLANE]
    return acc


def _pad_taps(w, K):
    return jnp.pad(w, ((0, _halo_rows(K) - K), (0, 0)))


def _conv_act_fwd(x, w, b, *, K, act, name, out_dtype, tm_pref=256):
    L, C = x.shape
    H = _halo_rows(K)
    tm = _pick(L, tm_pref)
    Co = C if act == "silu" else C // 2

    rs = min(ROW_SUB, tm)

    def body(h_ref, x_ref, w_ref, b_ref, o_ref, ext_scr):
        i = pl.program_id(0)
        ext_scr[0:H] = jnp.where(i > 0, h_ref[...], 0.0)
        ext_scr[H:] = x_ref[...]
        for j in range(Co // LANE):
            lanes = [j] if act == "silu" else [j, j + Co // LANE]
            wb = [(w_ref[:, c * LANE:(c + 1) * LANE], b_ref[:, c * LANE:(c + 1) * LANE]) for c in lanes]
            for r0 in range(0, tm, rs):
                us = [_causal_conv(ext_scr[r0:r0 + rs + H, c * LANE:(c + 1) * LANE], H, wc, K) + bc
                      for c, (wc, bc) in zip(lanes, wb)]
                y = _silu(us[0]) if act == "silu" else _silu(us[0]) * us[1]
                o_ref[r0:r0 + rs, j * LANE:(j + 1) * LANE] = y.astype(out_dtype)

    return pl.pallas_call(body, name=name, out_shape=S((L, Co), out_dtype), grid=(L // tm,),
                          in_specs=[_prev_halo_spec(tm, H, C), _rows(tm, C), _const((H, C)), _const((1, C))],
                          out_specs=_rows(tm, Co), scratch_shapes=[pltpu.VMEM((H + tm, C), f32)],
                          compiler_params=_cp("parallel"))(x, x, w, b)


def _conv_act_bwd(x, dparts, w, b, *, K, act, name, tm_pref=256):
    L, C = x.shape
    H = _halo_rows(K)
    tm = _pick(L, tm_pref)
    nb = L // tm
    Co = C if act == "silu" else C // 2
    nparts = len(dparts)

    rs = min(ROW_SUB, tm)

    def body(hp_ref, x_ref, hn_ref, w_ref, b_ref, *rest):
        d_refs, dn_refs = rest[:nparts], rest[nparts:2 * nparts]
        dx_ref, dw_ref, db_ref, ext_scr, d_scr = rest[2 * nparts:]
        i = pl.program_id(0)

        @pl.when(i == 0)
        def _():
            dw_ref[...] = jnp.zeros_like(dw_ref)
            db_ref[...] = jnp.zeros_like(db_ref)

        ext_scr[0:H] = jnp.where(i > 0, hp_ref[...], 0.0)
        ext_scr[H:H + tm] = x_ref[...]
        ext_scr[H + tm:] = jnp.where(i < nb - 1, hn_ref[...], 0.0)
        off = 0
        for r, rn in zip(d_refs, dn_refs):
            d_scr[0:tm, off:off + r.shape[1]] = r[...].astype(f32)
            d_scr[tm:, off:off + r.shape[1]] = jnp.where(i < nb - 1, rn[...].astype(f32), 0.0)
            off += r.shape[1]
        for j in range(Co // LANE):
            lanes = [j] if act == "silu" else [j, j + Co // LANE]
            wb = [(w_ref[:, c * LANE:(c + 1) * LANE], b_ref[:, c * LANE:(c + 1) * LANE]) for c in lanes]
            db_acc = [jnp.zeros((SUBLANE, LANE), f32) for _ in lanes]
            dw_acc = [[jnp.zeros((SUBLANE, LANE), f32) for _ in range(K)] for _ in lanes]
            for r0 in range(0, tm, rs):
                xvs = [_down_views(ext_scr[r0:r0 + rs + 2 * H, c * LANE:(c + 1) * LANE], H, rs + H, K) for c in lanes]
                us = [sum(xv[K - 1 - k] * wc[k:k + 1, :] for k in range(K)) + bc for xv, (wc, bc) in zip(xvs, wb)]
                d = d_scr[r0:r0 + rs + H, j * LANE:(j + 1) * LANE]
                dus = [d * _dsilu(us[0])] if act == "silu" else [d * us[1] * _dsilu(us[0]), d * _silu(us[0])]
                for q, (c, xv, du, (wc, _)) in enumerate(zip(lanes, xvs, dus, wb)):
                    dx_ref[r0:r0 + rs, c * LANE:(c + 1) * LANE] = _anticausal_conv(du, rs, wc, K).astype(bf16)
                    du_cur = du[:rs]
                    db_acc[q] = db_acc[q] + _fold_rows(du_cur)
                    for k in range(K):
                        dw_acc[q][k] = dw_acc[q][k] + _fold_rows(du_cur * xv[K - 1 - k][:rs])
            for q, c in enumerate(lanes):
                db_ref[:, c * LANE:(c + 1) * LANE] += jnp.sum(db_acc[q], axis=0, keepdims=True)
                for k in range(K):
                    dw_ref[k:k + 1, c * LANE:(c + 1) * LANE] += jnp.sum(dw_acc[q][k], axis=0, keepdims=True)

    in_specs = [_prev_halo_spec(tm, H, C), _rows(tm, C), _next_halo_spec(tm, H, C, L), _const((H, C)), _const((1, C))]
    in_specs += [_rows(tm, p.shape[1]) for p in dparts] + [_next_halo_spec(tm, H, p.shape[1], L) for p in dparts]
    return pl.pallas_call(body, name=name, out_shape=(S((L, C), bf16), S((H, C), f32), S((1, C), f32)), grid=(nb,),
                          in_specs=in_specs, out_specs=(_rows(tm, C), _const((H, C)), _const((1, C))),
                          scratch_shapes=[pltpu.VMEM((tm + 2 * H, C), f32), pltpu.VMEM((tm + H, Co), f32)],
                          compiler_params=_cp("arbitrary"))(x, x, x, w, b, *dparts, *dparts)


def _cf_fwd(u, dw_w, dw_b, ln_g, ln_b, *, name):
    L, C2 = u.shape
    C = C2 // 2
    K, H = CF_KERNEL, _halo_rows(CF_KERNEL)
    tm = _pick(L, 256)

    rs = min(ROW_SUB, tm)
    nl = C // LANE

    def body(h_ref, u_ref, w_ref, b_ref, g_ref, lb_ref, c_ref, s_ref, u_scr):
        i = pl.program_id(0)
        u_scr[0:H] = jnp.where(i > 0, h_ref[...], 0.0)
        u_scr[H:] = u_ref[...]
        for j in range(nl):
            wj, bj = w_ref[:, j * LANE:(j + 1) * LANE], b_ref[:, j * LANE:(j + 1) * LANE]
            for r0 in range(0, tm, rs):
                glu = u_scr[r0:r0 + rs + H, j * LANE:(j + 1) * LANE] \
                    * _sigmoid(u_scr[r0:r0 + rs + H, (nl + j) * LANE:(nl + j + 1) * LANE])
                c_ref[r0:r0 + rs, j * LANE:(j + 1) * LANE] = _causal_conv(glu, H, wj, K) + bj
        c = c_ref[...]
        mu = jnp.mean(c, axis=-1, keepdims=True)
        xc = c - mu
        var = jnp.mean(xc * xc, axis=-1, keepdims=True)
        ln = xc * lax.rsqrt(var + EPS) * g_ref[...] + lb_ref[...]
        s_ref[...] = _silu(ln).astype(bf16)

    return pl.pallas_call(body, name=name, out_shape=(S((L, C), f32), S((L, C), bf16)), grid=(L // tm,),
                          in_specs=[_prev_halo_spec(tm, H, C2), _rows(tm, C2), _const((H, C)), _const((1, C)),
                                    _const((1, C)), _const((1, C))],
                          out_specs=(_rows(tm, C), _rows(tm, C)), scratch_shapes=[pltpu.VMEM((H + tm, C2), f32)],
                          compiler_params=_cp("parallel"))(u, u, dw_w, dw_b, ln_g, ln_b)


def _cf_ln_bwd(c, ln_g, ln_b, ds, *, name):
    L, C = c.shape
    tm = _pick(L, 512)

    def body(c_ref, g_ref, lb_ref, ds_ref, dc_ref, dg_ref, db_ref):
        i = pl.program_id(0)

        @pl.when(i == 0)
        def _():
            dg_ref[...] = jnp.zeros_like(dg_ref)
            db_ref[...] = jnp.zeros_like(db_ref)

        c = c_ref[...]
        mu = jnp.mean(c, axis=-1, keepdims=True)
        xc = c - mu
        r = lax.rsqrt(jnp.mean(xc * xc, axis=-1, keepdims=True) + EPS)
        xh = xc * r
        ln = xh * g_ref[...] + lb_ref[...]
        dln = ds_ref[...].astype(f32) * _dsilu(ln)
        dg_ref[...] += jnp.sum(dln * xh, axis=0, keepdims=True)
        db_ref[...] += jnp.sum(dln, axis=0, keepdims=True)
        dxh = dln * g_ref[...]
        dc_ref[...] = r * (dxh - jnp.mean(dxh, axis=-1, keepdims=True) - xh * jnp.mean(dxh * xh, axis=-1, keepdims=True))

    return pl.pallas_call(body, name=name, out_shape=(S((L, C), f32), S((1, C), f32), S((1, C), f32)), grid=(L // tm,),
                          in_specs=[_rows(tm, C), _const((1, C)), _const((1, C)), _rows(tm, C)],
                          out_specs=(_rows(tm, C), _const((1, C)), _const((1, C))),
                          compiler_params=_cp("arbitrary"))(c, ln_g, ln_b, ds)


def _cf_glu_bwd(u, dc, dw_w, *, name):
    L, C2 = u.shape
    C = C2 // 2
    K, H = CF_KERNEL, _halo_rows(CF_KERNEL)
    tm = _pick(L, 256)
    nb = L // tm

    rs = min(ROW_SUB, tm)
    nl = C // LANE

    fold = _fold_rows

    def body(uh_ref, u_ref, dc_ref, dch_ref, w_ref, du_ref, dw_ref, db_ref, dus_ref, u_scr, dc_scr):
        i = pl.program_id(0)

        @pl.when(i == 0)
        def _():
            dw_ref[...] = jnp.zeros_like(dw_ref)
            db_ref[...] = jnp.zeros_like(db_ref)
            dus_ref[...] = jnp.zeros_like(dus_ref)

        u_scr[0:H] = jnp.where(i > 0, uh_ref[...], 0.0)
        u_scr[H:] = u_ref[...]
        dc_scr[0:tm] = dc_ref[...]
        dc_scr[tm:] = jnp.where(i < nb - 1, dch_ref[...], 0.0)
        for j in range(nl):
            la, lg = slice(j * LANE, (j + 1) * LANE), slice((nl + j) * LANE, (nl + j + 1) * LANE)
            wj = w_ref[:, la]
            zero8 = jnp.zeros((SUBLANE, LANE), f32)
            dw_acc, db_acc, dua_acc, dug_acc = [zero8] * K, zero8, zero8, zero8
            for r0 in range(0, tm, rs):
                a_e = u_scr[r0:r0 + rs + H, la]
                sg = _sigmoid(u_scr[r0:r0 + rs + H, lg])
                dce = dc_scr[r0:r0 + rs + H, la]
                dglu = _anticausal_conv(dce, rs, wj, K)
                a_c, sg_c, dc_c = a_e[H:], sg[H:], dce[:rs]
                du_a = dglu * sg_c
                du_g = dglu * a_c * sg_c * (1.0 - sg_c)
                du_ref[r0:r0 + rs, la] = du_a.astype(bf16)
                du_ref[r0:r0 + rs, lg] = du_g.astype(bf16)
                dua_acc, dug_acc, db_acc = dua_acc + fold(du_a), dug_acc + fold(du_g), db_acc + fold(dc_c)
                views = _down_views(a_e * sg, H, rs, K)
                dw_acc = [dw_acc[k] + fold(dc_c * views[K - 1 - k]) for k in range(K)]
            dus_ref[:, la] += jnp.sum(dua_acc, axis=0, keepdims=True)
            dus_ref[:, lg] += jnp.sum(dug_acc, axis=0, keepdims=True)
            db_ref[:, la] += jnp.sum(db_acc, axis=0, keepdims=True)
            for k in range(K):
                dw_ref[k:k + 1, la] += jnp.sum(dw_acc[k], axis=0, keepdims=True)

    return pl.pallas_call(body, name=name,
                          out_shape=(S((L, C2), bf16), S((H, C), f32), S((1, C), f32), S((1, C2), f32)), grid=(nb,),
                          in_specs=[_prev_halo_spec(tm, H, C2), _rows(tm, C2), _rows(tm, C), _next_halo_spec(tm, H, C, L),
                                    _const((H, C))],
                          out_specs=(_rows(tm, C2), _const((H, C)), _const((1, C)), _const((1, C2))),
                          scratch_shapes=[pltpu.VMEM((H + tm, C2), f32), pltpu.VMEM((tm + H, C), f32)],
                          compiler_params=_cp("arbitrary"))(u, u, dc, dc, dw_w)


def _gated_norm_fwd(y, z, g, *, name):
    L, C = y.shape
    tm = _pick(L, 256)

    def body(y_ref, z_ref, g_ref, o_ref):
        o_ref[...] = _rms_val(y_ref[...] * _silu(z_ref[...]), g_ref[...]).astype(bf16)

    return pl.pallas_call(body, name=name, out_shape=S((L, C), bf16), grid=(L // tm,),
                          in_specs=[_rows(tm, C), _rows(tm, C), _const((1, C))], out_specs=_rows(tm, C),
                          compiler_params=_cp("parallel"))(y, z, g)


def _gated_norm_bwd(y, z, g, dyn, *, name):
    L, C = y.shape
    tm = _pick(L, 256)

    def body(y_ref, z_ref, g_ref, d_ref, dy_ref, dz_ref, dg_ref):
        i = pl.program_id(0)

        @pl.when(i == 0)
        def _():
            dg_ref[...] = jnp.zeros_like(dg_ref)

        y, z = y_ref[...], z_ref[...]
        sz = _silu(z)
        du, dg = _rms_bwd_val(y * sz, g_ref[...], d_ref[...].astype(f32))
        dg_ref[...] += dg
        dy_ref[...] = du * sz
        dz_ref[...] = (du * y * _dsilu(z)).astype(bf16)

    return pl.pallas_call(body, name=name, out_shape=(S((L, C), f32), S((L, C), bf16), S((1, C), f32)), grid=(L // tm,),
                          in_specs=[_rows(tm, C), _rows(tm, C), _const((1, C)), _rows(tm, C)],
                          out_specs=(_rows(tm, C), _rows(tm, C), _const((1, C))),
                          compiler_params=_cp("arbitrary"))(y, z, g, dyn)


_XA_SCALE = XA_HEAD_DIM ** -0.5


def _attn_fwd(q, kv, *, name):
    L, C = q.shape
    tm = _pick(L, 512)
    Dh = XA_HEAD_DIM

    def body(q_ref, kv_ref, o_ref):
        for h in range(XA_HEADS):
            qh = q_ref[:, h * Dh:(h + 1) * Dh]
            kh = kv_ref[:, h * Dh:(h + 1) * Dh]
            vh = kv_ref[:, C + h * Dh:C + (h + 1) * Dh]
            s = lax.dot_general(qh, kh, _DN["nt"], preferred_element_type=f32) * _XA_SCALE
            e = jnp.exp(s - jnp.max(s, axis=-1, keepdims=True))
            p = e / jnp.sum(e, axis=-1, keepdims=True)
            o_ref[:, h * Dh:(h + 1) * Dh] = jnp.dot(p.astype(bf16), vh, preferred_element_type=f32).astype(bf16)

    return pl.pallas_call(body, name=name, out_shape=S((L, C), bf16), grid=(L // tm,),
                          in_specs=[_rows(tm, C), _const((N_MEM, 2 * C))], out_specs=_rows(tm, C),
                          compiler_params=_cp("parallel"))(q, kv)


def _attn_bwd(q, kv, do, *, name):
    L, C = q.shape
    tm = _pick(L, 512)
    Dh = XA_HEAD_DIM

    def body(q_ref, kv_ref, do_ref, dq_ref, dkv_ref):
        i = pl.program_id(0)

        @pl.when(i == 0)
        def _():
            dkv_ref[...] = jnp.zeros_like(dkv_ref)

        for h in range(XA_HEADS):
            qh = q_ref[:, h * Dh:(h + 1) * Dh]
            kh = kv_ref[:, h * Dh:(h + 1) * Dh]
            vh = kv_ref[:, C + h * Dh:C + (h + 1) * Dh]
            doh = do_ref[:, h * Dh:(h + 1) * Dh]
            s = lax.dot_general(qh, kh, _DN["nt"], preferred_element_type=f32) * _XA_SCALE
            e = jnp.exp(s - jnp.max(s, axis=-1, keepdims=True))
            p = e / jnp.sum(e, axis=-1, keepdims=True)
            pb = p.astype(bf16)
            dkv_ref[:, C + h * Dh:C + (h + 1) * Dh] += lax.dot_general(pb, doh, _DN["tn"], preferred_element_type=f32)
            dp = lax.dot_general(doh, vh, _DN["nt"], preferred_element_type=f32)
            ds = (p * (dp - jnp.sum(dp * p, axis=-1, keepdims=True)) * _XA_SCALE).astype(bf16)
            dq_ref[:, h * Dh:(h + 1) * Dh] = jnp.dot(ds, kh, preferred_element_type=f32).astype(bf16)
            dkv_ref[:, h * Dh:(h + 1) * Dh] += lax.dot_general(ds, qh, _DN["tn"], preferred_element_type=f32)

    return pl.pallas_call(body, name=name, out_shape=(S((L, C), bf16), S((N_MEM, 2 * C), f32)), grid=(L // tm,),
                          in_specs=[_rows(tm, C), _const((N_MEM, 2 * C)), _rows(tm, C)],
                          out_specs=(_rows(tm, C), _const((N_MEM, 2 * C))),
                          compiler_params=_cp("arbitrary"))(q, kv, do)


Q = CHUNK
PAIRS = HEADS_PER_GROUP // 2
GW = HEADS_PER_GROUP * HEAD_DIM


def _split(x, pieces):
    out = []
    for _ in range(pieces - 1):
        p = x.astype(bf16)
        out.append(p)
        x = x - p.astype(f32)
    return out + [x.astype(bf16)]


def _sel_right(x, sel, mode="nn", pieces=2):
    return sum(lax.dot_general(p, sel, _DN[mode], preferred_element_type=f32) for p in _split(x, pieces))


def _sel_left(sel, x, pieces=3):
    return sum(lax.dot_general(sel, p, _DN["nn"], preferred_element_type=f32) for p in _split(x, pieces))


def _ssd_common(dt_ref, hp_ref):
    dt_pre = dt_ref[...] + hp_ref[0:1, :]
    dt = _softplus(dt_pre)
    A = -jnp.exp(hp_ref[1:2, :])
    a = dt * A
    row = lax.broadcasted_iota(jnp.int32, (Q, Q), 0)
    col = lax.broadcasted_iota(jnp.int32, (Q, Q), 1)
    tri = row >= col
    cs = _sel_left(tri.astype(bf16), a)
    T = cs[Q - 1:Q, :]
    return dict(dt_pre=dt_pre, dt=dt, A=A, cs=cs, csT=cs.T, T=T, ecs=jnp.exp(cs), eend=jnp.exp(T - cs), eT=jnp.exp(T),
                tri=tri, row=row, col=col)


def _pair_expand(v, hA, lo):
    return jnp.where(lo, v[:, hA:hA + 1], v[:, hA + 1:hA + 2])


def _decay(cm, h):
    seg = cm["cs"][:, h:h + 1] - cm["csT"][h:h + 1, :]
    return jnp.where(cm["tri"], jnp.exp(jnp.where(cm["tri"], seg, 0.0)), 0.0)


def _decay_t(cm, h):
    keep = cm["row"] <= cm["col"]
    seg = cm["csT"][h:h + 1, :] - cm["cs"][:, h:h + 1]
    return jnp.where(keep, jnp.exp(jnp.where(keep, seg, 0.0)), 0.0)


ALL_PAIRS = N_SSM_HEADS // 2
GN = N_GROUPS * D_STATE


def _ssd_fwd(act, dtp, hp, *, name):
    L = act.shape[0]
    nc = L // Q

    def body(xs_ref, b_ref, c_ref, dt_ref, hp_ref, y_ref, hs_ref, h_scr):
        c = pl.program_id(0)

        @pl.when(c == 0)
        def _():
            h_scr[...] = jnp.zeros_like(h_scr)

        cm = _ssd_common(dt_ref, hp_ref)
        lo = lax.broadcasted_iota(jnp.int32, (Q, LANE), 1) < HEAD_DIM
        top = lax.broadcasted_iota(jnp.int32, (LANE, LANE), 0) < HEAD_DIM
        Drow = hp_ref[2:3, :]
        for g in range(N_GROUPS):
            Bb = b_ref[:, g * D_STATE:(g + 1) * D_STATE].astype(bf16)
            Cb = c_ref[:, g * D_STATE:(g + 1) * D_STATE].astype(bf16)
            CB = lax.dot_general(Cb, Bb, _DN["nt"], preferred_element_type=f32)
            for jj in range(PAIRS):
                p = g * PAIRS + jj
                hA, hB = 2 * p, 2 * p + 1
                dtx, ecsx, eendx = (_pair_expand(cm[k], hA, lo) for k in ("dt", "ecs", "eend"))
                xs_p = xs_ref[:, p * LANE:(p + 1) * LANE]
                Xd = xs_p * dtx
                Y = None
                for h, Xm in ((hA, jnp.where(lo, Xd, 0.0)), (hB, jnp.where(lo, 0.0, Xd))):
                    W = (CB * _decay(cm, h)).astype(bf16)
                    t = jnp.dot(W, Xm.astype(bf16), preferred_element_type=f32)
                    Y = t if Y is None else Y + t
                Hp = h_scr[p]
                hs_ref[0, p] = Hp
                Yoff = lax.dot_general(Cb, Hp.astype(bf16), _DN["nt"], preferred_element_type=f32) * ecsx
                Dx = jnp.where(lo[0:1, :], Drow[:, hA:hA + 1], Drow[:, hB:hB + 1])
                y_ref[:, p * LANE:(p + 1) * LANE] = Y + Yoff + xs_p * Dx
                Snew = lax.dot_general((Xd * eendx).astype(bf16), Bb, _DN["tn"], preferred_element_type=f32)
                eTx = jnp.where(top, cm["eT"][:, hA:hA + 1], cm["eT"][:, hB:hB + 1])
                h_scr[p] = Hp * eTx + Snew

    return pl.pallas_call(
        body, name=name, out_shape=(S((L, D_INNER), f32), S((nc, ALL_PAIRS, LANE, D_STATE), f32)),
        grid=(nc,),
        in_specs=[pl.BlockSpec((Q, D_INNER), lambda c: (c, 0)),
                  pl.BlockSpec((Q, GN), lambda c: (c, D_INNER // GN)),
                  pl.BlockSpec((Q, GN), lambda c: (c, D_INNER // GN + 1)),
                  pl.BlockSpec((Q, LANE), lambda c: (c, 0)),
                  pl.BlockSpec((SUBLANE, LANE), lambda c: (0, 0))],
        out_specs=(pl.BlockSpec((Q, D_INNER), lambda c: (c, 0)),
                   pl.BlockSpec((1, ALL_PAIRS, LANE, D_STATE), lambda c: (c, 0, 0, 0))),
        scratch_shapes=[pltpu.VMEM((ALL_PAIRS, LANE, D_STATE), f32)],
        compiler_params=_cp("arbitrary"))(act, act, act, dtp, hp)


def _ssd_bwd(act, dtp, hp, dy, hs, *, name):
    L = act.shape[0]
    nc = L // Q

    def body(xs_ref, b_ref, c_ref, dt_ref, hp_ref, dy_ref, hs_ref, dxs_ref, db_ref, dc_ref, ddt_ref, dhp_ref, dh_scr):
        c = pl.program_id(0)

        @pl.when(c == 0)
        def _():
            dh_scr[...] = jnp.zeros_like(dh_scr)
            dhp_ref[...] = jnp.zeros_like(dhp_ref)

        cm = _ssd_common(dt_ref, hp_ref)
        lane = lax.broadcasted_iota(jnp.int32, (Q, LANE), 1)
        sub = lax.broadcasted_iota(jnp.int32, (LANE, LANE), 0)
        lo = lane < HEAD_DIM
        top = sub < HEAD_DIM
        Drow = hp_ref[2:3, :]
        zero = jnp.zeros((Q, LANE), f32)
        dcs, dcsT, ddtx = zero, zero, zero
        dD_row = jnp.zeros((1, LANE), f32)
        dT_row = jnp.zeros((1, LANE), f32)
        for g in range(N_GROUPS):
            Bb = b_ref[:, g * D_STATE:(g + 1) * D_STATE].astype(bf16)
            Cb = c_ref[:, g * D_STATE:(g + 1) * D_STATE].astype(bf16)
            CB = lax.dot_general(Cb, Bb, _DN["nt"], preferred_element_type=f32)
            CBT = lax.dot_general(Bb, Cb, _DN["nt"], preferred_element_type=f32)
            dC, dB, dCB = zero, zero, jnp.zeros((Q, Q), f32)
            for jj in range(PAIRS):
                p = g * PAIRS + jj
                hA, hB = 2 * p, 2 * p + 1
                Pj = (lane == jnp.where(top, hA, hB)).astype(bf16)
                dtx, ecsx, eendx = (_pair_expand(cm[k], hA, lo) for k in ("dt", "ecs", "eend"))
                xs_p = xs_ref[:, p * LANE:(p + 1) * LANE]
                dY_p = dy_ref[:, p * LANE:(p + 1) * LANE]
                Xd = xs_p * dtx
                Xdb = Xd.astype(bf16)
                Hp, dHn = hs_ref[0, p], dh_scr[p]
                Hb, dHb = Hp.astype(bf16), dHn.astype(bf16)
                EdYb = (dY_p * ecsx).astype(bf16)
                YoffN = lax.dot_general(Cb, Hb, _DN["nt"], preferred_element_type=f32)
                dC = dC + jnp.dot(EdYb, Hb, preferred_element_type=f32)
                dH_off = lax.dot_general(EdYb, Cb, _DN["tn"], preferred_element_type=f32)
                R = lax.dot_general(Bb, dHb, _DN["nt"], preferred_element_type=f32)
                Xe = Xd * eendx
                dB = dB + jnp.dot(Xe.astype(bf16), dHb, preferred_element_type=f32)
                dXd = R * eendx
                RXe = R * Xe
                dcs = dcs + _sel_right(dY_p * YoffN * ecsx - RXe, Pj)
                HH = dHn * Hp
                hh = [jnp.sum(jnp.sum(HH[r0:r0 + HEAD_DIM], axis=0, keepdims=True), axis=1, keepdims=True)
                      for r0 in (0, HEAD_DIM)]
                rxe_cols = jnp.broadcast_to(jnp.sum(RXe, axis=0, keepdims=True), (SUBLANE, LANE))
                dT_row = dT_row + _sel_right(rxe_cols, Pj, pieces=3)[0:1] \
                    + (jnp.where(lane[0:1] == hA, hh[0], 0.0) + jnp.where(lane[0:1] == hB, hh[1], 0.0)) * cm["eT"]
                for h, keep in ((hA, lo), (hB, jnp.logical_not(lo))):
                    M = _decay(cm, h)
                    Wf = CB * M
                    dYm = jnp.where(keep, dY_p, 0.0).astype(bf16)
                    dW = lax.dot_general(dYm, Xdb, _DN["nt"], preferred_element_type=f32)
                    WT = (CBT * _decay_t(cm, h)).astype(bf16)
                    dXd = dXd + jnp.dot(WT, dYm, preferred_element_type=f32)
                    Z = dW * Wf
                    dcs = dcs + _sel_right(Z, (lane == h).astype(bf16))
                    dcsT = dcsT + jnp.where(sub == h, jnp.sum(Z, axis=0, keepdims=True), 0.0)
                    dCB = dCB + dW * M
                Dx = jnp.where(lo[0:1, :], Drow[:, hA:hA + 1], Drow[:, hB:hB + 1])
                dxs_ref[:, p * LANE:(p + 1) * LANE] = dXd * dtx + dY_p * Dx
                ddtx = ddtx + _sel_right(dXd * xs_p, Pj)
                dD_cols = jnp.broadcast_to(jnp.sum(dY_p * xs_p, axis=0, keepdims=True), (SUBLANE, LANE))
                dD_row = dD_row + _sel_right(dD_cols, Pj, pieces=3)[0:1]
                eTx = jnp.where(top, cm["eT"][:, hA:hA + 1], cm["eT"][:, hB:hB + 1])
                dh_scr[p] = dHn * eTx + dH_off
            dCBb = dCB.astype(bf16)
            dc_ref[:, g * D_STATE:(g + 1) * D_STATE] = dC + jnp.dot(dCBb, Bb, preferred_element_type=f32)
            db_ref[:, g * D_STATE:(g + 1) * D_STATE] = dB + lax.dot_general(dCBb, Cb, _DN["tn"], preferred_element_type=f32)
        dcs = dcs - dcsT.T + jnp.where(lax.broadcasted_iota(jnp.int32, (Q, LANE), 0) == Q - 1, dT_row, 0.0)
        da = _sel_left((cm["row"] <= cm["col"]).astype(bf16), dcs)
        ddt_pre = (da * cm["A"] + ddtx) * _sigmoid(cm["dt_pre"])
        ddt_ref[...] = ddt_pre
        r8 = lax.broadcasted_iota(jnp.int32, (SUBLANE, LANE), 0)
        dhp_ref[...] += jnp.where(r8 == 0, jnp.sum(ddt_pre, axis=0, keepdims=True),
                                  jnp.where(r8 == 1, jnp.sum(da * cm["dt"], axis=0, keepdims=True) * cm["A"],
                                            jnp.where(r8 == 2, dD_row, 0.0)))

    rev = lambda c: nc - 1 - c
    return pl.pallas_call(
        body, name=name,
        out_shape=(S((L, D_INNER), f32), S((L, GN), f32), S((L, GN), f32), S((L, LANE), f32), S((SUBLANE, LANE), f32)),
        grid=(nc,),
        in_specs=[pl.BlockSpec((Q, D_INNER), lambda c: (rev(c), 0)),
                  pl.BlockSpec((Q, GN), lambda c: (rev(c), D_INNER // GN)),
                  pl.BlockSpec((Q, GN), lambda c: (rev(c), D_INNER // GN + 1)),
                  pl.BlockSpec((Q, LANE), lambda c: (rev(c), 0)),
                  pl.BlockSpec((SUBLANE, LANE), lambda c: (0, 0)),
                  pl.BlockSpec((Q, D_INNER), lambda c: (rev(c), 0)),
                  pl.BlockSpec((1, ALL_PAIRS, LANE, D_STATE), lambda c: (rev(c), 0, 0, 0))],
        out_specs=(pl.BlockSpec((Q, D_INNER), lambda c: (rev(c), 0)),
                   pl.BlockSpec((Q, GN), lambda c: (rev(c), 0)),
                   pl.BlockSpec((Q, GN), lambda c: (rev(c), 0)),
                   pl.BlockSpec((Q, LANE), lambda c: (rev(c), 0)),
                   pl.BlockSpec((SUBLANE, LANE), lambda c: (0, 0))),
        scratch_shapes=[pltpu.VMEM((ALL_PAIRS, LANE, D_STATE), f32)],
        compiler_params=_cp("arbitrary"))(act, act, act, dtp, hp, dy, hs)


def _group_pad_cols(w):
    return jnp.pad(w, [(0, 0)] * (w.ndim - 1) + [(0, LANE - N_SSM_HEADS)])


def _group_unpad_cols(w):
    return w[..., :N_SSM_HEADS]


def _row(v):
    return v.reshape(1, -1)


ROW_SHARDED = ('ssm_out_w', 'cf_pw2_w', 'xa_q_w', 'xa_o_w', 'ffn_out_w')
COL_SHARDED = ('cf_pw1_w', 'xa_kv_w', 'ffn_in_w')


MIXER_WEIGHTS = ('ssm_in_w', 'ssm_out_w', 'cf_pw1_w', 'cf_pw2_w')


def _layer_matmul_weights(i, part):
    if part == "mixer":
        return ('ssm_in_w', 'ssm_out_w') if i % 2 == 0 else ('cf_pw1_w', 'cf_pw2_w')
    return ('xa_q_w', 'xa_kv_w', 'xa_o_w', 'ffn_in_w', 'ffn_out_w')


def _device_step(x, mem, target, W, layer_weights, layer_grads, start_after=()):
    ng = W['norm_g']
    lw = []
    for i in range(DEPTH):
        j = i // 2
        p = {}
        if i % 2 == 0:
            p['cw'] = _pad_taps(W['ssm_conv_w'][j], SSM_CONV)
            p['cb'] = _row(W['ssm_conv_b'][j])
            hp = jnp.stack([_group_pad_cols(W['ssm_dt_bias'][j]), _group_pad_cols(W['ssm_A_log'][j]),
                            _group_pad_cols(W['ssm_D'][j])])
            p['hp'] = jnp.pad(hp, ((0, SUBLANE - 3), (0, 0)))
            p['sng'] = _row(W['ssm_norm_g'][j])
        else:
            p['pw1b'] = _row(W['cf_pw1_b'][j])
            p['dww'], p['dwb'] = _pad_taps(W['cf_dw_w'][j], CF_KERNEL), _row(W['cf_dw_b'][j])
            p['lng'], p['lnb'] = _row(W['cf_ln_g'][j]), _row(W['cf_ln_b'][j])
            p['pw2b'] = _row(W['cf_pw2_b'][j])
        p['memg'] = _row(W['xa_mem_g'][i])
        p['fcw'], p['fcb'] = _pad_taps(W['ffn_conv_w'][i], FFN_CONV), _row(W['ffn_conv_b'][i])
        p['g'] = [_row(ng[i, s]) for s in range(6)]
        lw.append(p)

    def wmm(a, wl, wname, mode, **kw):
        return _mm(a, wl[wname], mode, b_shards=wname in COL_SHARDED, **kw)

    saved = []
    X = x
    h = _rmsnorm_fwd(X, lw[0]['g'][0], name="norm_in", after=start_after)
    for i in range(DEPTH):
        p, sv = lw[i], {}
        wl = dict(layer_weights(i, "mixer", X))
        sv['X0'], sv['h'], sv['wl'] = X, h, wl
        if i % 2 == 0:
            win = jnp.concatenate([wl['ssm_in_w'][s] for s in range(N_CHIPS)], axis=1)
            wl['wz'], wl['wx'] = win[:, :D_INNER], win[:, D_INNER:D_INNER + CONV_DIM]
            wl['wdt'] = _group_pad_cols(win[:, D_INNER + CONV_DIM:])
            z = _mm(h, wl['wz'], "nn", name="ssm_z")
            xbc = _mm(h, wl['wx'], "nn", name="ssm_xbc")
            dtp = _mm(h, wl['wdt'], "nn", name="ssm_dt")
            act = _conv_act_fwd(xbc, p['cw'], p['cb'], K=SSM_CONV, act="silu", name="ssm_conv_fwd", out_dtype=f32)
            y, hs = _ssd_fwd(act, dtp, p['hp'], name="ssd_fwd")
            yn = _gated_norm_fwd(y, z, p['sng'], name="ssm_gnorm_fwd")
            mix = wmm(yn, wl, 'ssm_out_w', "nn", name="ssm_out")
            sv.update(z=z, xbc=xbc, dtp=dtp, act=act, y=y, hs=hs, yn=yn)
        else:
            u = wmm(h, wl, 'cf_pw1_w', "nn", name="cf_pw1", bias=p['pw1b'])
            c, s = _cf_fwd(u, p['dww'], p['dwb'], p['lng'], p['lnb'], name="cf_conv_fwd")
            mix = wmm(s, wl, 'cf_pw2_w', "nn", name="cf_pw2", bias=p['pw2b'])
            sv.update(u=u, c=c, s=s)
        wl.update(layer_weights(i, "rest", mix))
        X1, h2 = _resid_norm_fwd(X, mix, p['g'][1], p['g'][2], name="resid_norm_a")
        q = wmm(h2, wl, 'xa_q_w', "nn", name="xa_q", out_dtype=bf16)
        m = _rmsnorm_fwd(mem, p['memg'], name="xa_mem_norm")
        kv = wmm(m, wl, 'xa_kv_w', "nn", name="xa_kv", out_dtype=bf16)
        o = _attn_fwd(q, kv, name="xa_attn_fwd")
        a = wmm(o, wl, 'xa_o_w', "nn", name="xa_o")
        X2, h3 = _resid_norm_fwd(X1, a, p['g'][3], p['g'][4], name="resid_norm_b")
        u0 = wmm(h3, wl, 'ffn_in_w', "nn", name="ffn_in")
        fact = _conv_act_fwd(u0, p['fcw'], p['fcb'], K=FFN_CONV, act="swiglu", name="ffn_conv_fwd", out_dtype=bf16)
        f = wmm(fact, wl, 'ffn_out_w', "nn", name="ffn_out")
        g_next = lw[i + 1]['g'][0] if i + 1 < DEPTH else None
        X3, hn = _resid_norm_fwd(X2, f, p['g'][5], g_next, name="resid_norm_c" if g_next is not None else "resid_norm_last")
        sv.update(mix=mix, X1=X1, h2=h2, q=q, m=m, kv=kv, o=o, a=a, X2=X2, h3=h3, u0=u0, fact=fact, f=f)
        saved.append(sv)
        X, h = X3, hn

    sse, G = _loss_fwd_bwd(X, target, name="loss")

    small = [n for n in WEIGHT_NAMES if n not in MATMUL_WEIGHTS]
    gr = {n: [None] * W[n].shape[0] for n in small}

    def dwmm(gl, a, d, wname, *, name):
        if wname in COL_SHARDED:
            gl[wname] = _mm(a, d, "tn", name=name, out_dtype=bf16, out_shards=True)
        else:
            g = _mm(a, d, "tn", name=name, out_dtype=bf16)
            gl[wname] = g.reshape(N_CHIPS, g.shape[0] // N_CHIPS, g.shape[1])

    dng = [[None] * 6 for _ in range(DEPTH)]
    df = None
    for i in reversed(range(DEPTH)):
        p, sv, j = lw[i], saved[i], i // 2
        wl, gl = sv['wl'], {}
        if df is None:
            df, dng[i][5], _ = _norm_bwd(sv['f'], p['g'][5], G, name="nb_f", out_dtype=bf16)
        dwmm(gl, sv['fact'], df, 'ffn_out_w', name="ffn_out_dw")
        dfact = wmm(df, wl, 'ffn_out_w', "nt", name="ffn_out_dx")
        du0, dcw, dcb = _conv_act_bwd(sv['u0'], [dfact], p['fcw'], p['fcb'], K=FFN_CONV, act="swiglu", name="ffn_conv_bwd")
        gr['ffn_conv_w'][i], gr['ffn_conv_b'][i] = dcw[:FFN_CONV], dcb[0]
        dwmm(gl, sv['h3'], du0, 'ffn_in_w', name="ffn_in_dw")
        dh3 = wmm(du0, wl, 'ffn_in_w', "nt", name="ffn_in_dx")
        G, dng[i][4], da, dng[i][3], _ = _norm_bwd_chain(sv['X2'], p['g'][4], dh3, G, sv['a'], p['g'][3], name="nb_x2_a")
        dwmm(gl, sv['o'], da, 'xa_o_w', name="xa_o_dw")
        do = wmm(da, wl, 'xa_o_w', "nt", name="xa_o_dx", out_dtype=bf16)
        dq, dkv = _attn_bwd(sv['q'], sv['kv'], do, name="xa_attn_bwd")
        dwmm(gl, sv['h2'], dq, 'xa_q_w', name="xa_q_dw")
        dh2 = wmm(dq, wl, 'xa_q_w', "nt", name="xa_q_dx")
        dwmm(gl, sv['m'], dkv, 'xa_kv_w', name="xa_kv_dw")
        dm = wmm(dkv, wl, 'xa_kv_w', "nt", name="xa_kv_dx")
        _, dmg, _ = _norm_bwd(mem, p['memg'], dm, name="nb_mem")
        gr['xa_mem_g'][i] = dmg[0]
        behind = tuple(layer_grads(i, "rest", gl))
        G, dng[i][2], dmix, dng[i][1], dmix_sum = _norm_bwd_chain(sv['X1'], p['g'][2], dh2, G, sv['mix'], p['g'][1],
                                                                  name="nb_x1_mix", after=behind)
        if i % 2 == 0:
            dwmm(gl, sv['yn'], dmix, 'ssm_out_w', name="ssm_out_dw")
            dyn = wmm(dmix, wl, 'ssm_out_w', "nt", name="ssm_out_dx")
            dy, dz, dsng = _gated_norm_bwd(sv['y'], sv['z'], p['sng'], dyn, name="ssm_gnorm_bwd")
            gr['ssm_norm_g'][j] = dsng[0]
            dxs, dB, dC, ddtp, dhp = _ssd_bwd(sv['act'], sv['dtp'], p['hp'], dy, sv['hs'], name="ssd_bwd")
            gr['ssm_dt_bias'][j], gr['ssm_A_log'][j], gr['ssm_D'][j] = (_group_unpad_cols(dhp[r]) for r in range(3))
            dxbc, dcw, dcb = _conv_act_bwd(sv['xbc'], [dxs, dB, dC], p['cw'], p['cb'], K=SSM_CONV, act="silu",
                                           name="ssm_conv_bwd")
            gr['ssm_conv_w'][j], gr['ssm_conv_b'][j] = dcw[:SSM_CONV], dcb[0]
            hh = sv['h']
            dwz = _mm(hh, dz, "tn", name="ssm_z_dw", out_dtype=bf16)
            dwx = _mm(hh, dxbc, "tn", name="ssm_xbc_dw", out_dtype=bf16)
            dwdt = _mm(hh, ddtp, "tn", name="ssm_dt_dw", out_dtype=bf16)
            din = jnp.concatenate([dwz, dwx, _group_unpad_cols(dwdt)], axis=1)
            gl['ssm_in_w'] = jnp.stack(jnp.split(din, N_CHIPS, axis=1))
            behind = tuple(layer_grads(i, "mixer", gl))
            dh = _mm(dz, wl['wz'], "nt", name="ssm_z_dx", after=behind)
            dh = _mm(dxbc, wl['wx'], "nt", name="ssm_xbc_dx", add=dh)
            dh = _mm(ddtp, wl['wdt'], "nt", name="ssm_dt_dx", add=dh)
        else:
            dwmm(gl, sv['s'], dmix, 'cf_pw2_w', name="cf_pw2_dw")
            gr['cf_pw2_b'][j] = dmix_sum[0]
            ds = wmm(dmix, wl, 'cf_pw2_w', "nt", name="cf_pw2_dx")
            dc, dlg, dlb = _cf_ln_bwd(sv['c'], p['lng'], p['lnb'], ds, name="cf_ln_bwd")
            gr['cf_ln_g'][j], gr['cf_ln_b'][j] = dlg[0], dlb[0]
            du, ddw, ddb, dus = _cf_glu_bwd(sv['u'], dc, p['dww'], name="cf_glu_bwd")
            gr['cf_dw_w'][j], gr['cf_dw_b'][j], gr['cf_pw1_b'][j] = ddw[:CF_KERNEL], ddb[0], dus[0]
            dwmm(gl, sv['h'], du, 'cf_pw1_w', name="cf_pw1_dw")
            behind = tuple(layer_grads(i, "mixer", gl))
            dh = wmm(du, wl, 'cf_pw1_w', "nt", name="cf_pw1_dx", after=behind)
        if i > 0:
            G, dng[i][0], df, dng[i - 1][5], _ = _norm_bwd_chain(sv['X0'], p['g'][0], dh, G, saved[i - 1]['f'],
                                                                 lw[i - 1]['g'][5], name="nb_x0_f")
        else:
            G, dng[i][0], _ = _norm_bwd(sv['X0'], p['g'][0], dh, name="nb_x0", add=G)
    gr['norm_g'] = [jnp.concatenate(dng[i], axis=0) for i in range(DEPTH)]
    gsmall = {n: jnp.stack(gr[n]) for n in small}
    return sse, G, gsmall


MESH = pl.DeviceIdType.MESH
HBM_SPEC = pl.BlockSpec(memory_space=pltpu.HBM)


def _chip_peers(x, y):
    return [(1 - x, y), (x, 1 - y), (1 - x, 1 - y)]


def _all_gather_chips(buf, *, name):
    R, C = buf.shape

    def body(in_ref, out_ref, send_sems, recv_sems, local_sem):
        x, y, c = lax.axis_index("x"), lax.axis_index("y"), lax.axis_index("c")
        me = 2 * x + y
        mine = pltpu.make_async_copy(in_ref, out_ref.at[me], local_sem)
        mine.start()
        peers = _chip_peers(x, y)
        sends = []
        for k, (px, py) in enumerate(peers):
            cp = pltpu.make_async_remote_copy(src_ref=in_ref, dst_ref=out_ref.at[me], send_sem=send_sems.at[k],
                                              recv_sem=recv_sems.at[k], device_id=(px, py, c), device_id_type=MESH)
            cp.start()
            sends.append(cp)
        for k, (px, py) in enumerate(peers):
            pltpu.make_async_remote_copy(src_ref=in_ref, dst_ref=out_ref.at[2 * px + py], send_sem=send_sems.at[k],
                                         recv_sem=recv_sems.at[k], device_id=(px, py, c), device_id_type=MESH).wait_recv()
        for cp in sends:
            cp.wait_send()
        mine.wait()

    return pl.pallas_call(body, name=name, out_shape=S((N_CHIPS, R, C), buf.dtype), in_specs=[HBM_SPEC], out_specs=HBM_SPEC,
                          scratch_shapes=[pltpu.SemaphoreType.DMA((3,)), pltpu.SemaphoreType.DMA((3,)),
                                          pltpu.SemaphoreType.DMA(())])(buf)


def _remote(src, dst, send_sem, recv_sem, device):
    return pltpu.make_async_remote_copy(src_ref=src, dst_ref=dst, send_sem=send_sem, recv_sem=recv_sem,
                                        device_id=device, device_id_type=MESH)


def _gather_matmul_weights(shards, *, name):
    n = len(shards)

    def body(*refs):
        ins, outs = refs[:n], refs[n:2 * n]
        send, recv, fsend, frecv, lsem = refs[2 * n:]
        x, y, c = lax.axis_index("x"), lax.axis_index("y"), lax.axis_index("c")
        me, sib = 2 * x + y, (x, y, 1 - c)
        peers = _chip_peers(x, y)
        started, local = [], []
        for w in range(n):
            cp = pltpu.make_async_copy(ins[w], outs[w].at[:, me], lsem.at[w])
            cp.start()
            local.append(cp)
            for k, (px, py) in enumerate(peers):
                cp = _remote(ins[w].at[:, c], outs[w].at[:, me, c], send.at[w, k], recv.at[w, k], (px, py, c))
                cp.start()
                started.append(cp)
        for w in range(n):
            for k, (px, py) in enumerate(peers):
                landed = outs[w].at[:, 2 * px + py, c]
                _remote(ins[w].at[:, c], landed, send.at[w, k], recv.at[w, k], (px, py, c)).wait_recv()
                cp = _remote(landed, landed, fsend.at[w, k], frecv.at[w, k], sib)
                cp.start()
                started.append(cp)
        for w in range(n):
            for k, (px, py) in enumerate(peers):
                _remote(ins[w].at[:, c], outs[w].at[:, 2 * px + py, 1 - c], fsend.at[w, k], frecv.at[w, k], sib).wait_recv()
        for cp in started:
            cp.wait_send()
        for cp in local:
            cp.wait()

    out_shape = tuple(S((s.shape[0], N_CHIPS) + s.shape[1:], s.dtype) for s in shards)
    sems = [pltpu.SemaphoreType.DMA((n, 3)) for _ in range(4)] + [pltpu.SemaphoreType.DMA((n,))]
    return pl.pallas_call(body, name=name, out_shape=out_shape, in_specs=[HBM_SPEC] * n, out_specs=(HBM_SPEC,) * n,
                          scratch_shapes=sems)(*shards)


SEM_SPEC = pl.BlockSpec(memory_space=pltpu.SEMAPHORE)
VMEM_SPEC = pl.BlockSpec(memory_space=pltpu.VMEM)


def _in_hbm(a):
    return pltpu.with_memory_space_constraint(a, pltpu.HBM)


def _chip_targets(x, y):
    return [(x, y), (1 - x, y), (x, 1 - y), (1 - x, 1 - y)]


def _spread_start(srcs, scatter, *, name, after=()):
    n = len(srcs)
    lands = [lax.empty((N_CHIPS,) + (s.shape[1:] if scatter else s.shape), s.dtype) for s in srcs]

    def body(*refs):
        src, land = refs[:n], refs[n:2 * n]
        send, recv, token = refs[2 * n + len(after)], refs[2 * n + len(after) + 1], refs[-1]
        x, y, c = lax.axis_index("x"), lax.axis_index("y"), lax.axis_index("c")
        me = 2 * x + y
        for w in range(n):
            for k, (px, py) in enumerate(_chip_targets(x, y)):
                block = src[w].at[2 * px + py] if scatter else src[w]
                _remote(block, land[w].at[me], send.at[N_CHIPS * w + k], recv.at[N_CHIPS * w + k], (px, py, c)).start()
        token[...] = jnp.zeros_like(token)

    thru = tuple(pltpu.HBM(a.shape, a.dtype) for a in list(srcs) + lands)
    sems = (pltpu.SemaphoreType.DMA((N_CHIPS * n,)), pltpu.SemaphoreType.DMA((N_CHIPS * n,)))
    out = pl.pallas_call(
        body, name=name, out_shape=sems + thru + (S((SUBLANE, LANE), f32),),
        in_specs=[HBM_SPEC] * (2 * n) + [ANY_SPEC] * len(after),
        out_specs=(SEM_SPEC, SEM_SPEC) + (HBM_SPEC,) * (2 * n) + (VMEM_SPEC,),
        input_output_aliases={i: 2 + i for i in range(2 * n)},
        compiler_params=pltpu.CompilerParams(has_side_effects=pltpu.SideEffectType.DATAFLOW_SIDE_EFFECTING),
    )(*[_in_hbm(a) for a in list(srcs) + lands], *after)
    return out[0], out[1], out[2:2 + n], out[2 + n:2 + 2 * n], out[-1]


def _spread_wait(send, recv, srcs, lands, after, scatter, *, name):
    n = len(srcs)

    def body(*refs):
        src, land, send, recv = refs[:n], refs[n:2 * n], refs[2 * n], refs[2 * n + 1]
        x, y, c = lax.axis_index("x"), lax.axis_index("y"), lax.axis_index("c")
        me = 2 * x + y
        for w in range(n):
            for k, (px, py) in enumerate(_chip_targets(x, y)):
                block = src[w].at[me] if scatter else src[w]
                cp = _remote(block, land[w].at[2 * px + py], send.at[N_CHIPS * w + k], recv.at[N_CHIPS * w + k], (px, py, c))
                cp.wait_send()
                cp.wait_recv()

    thru = tuple(pltpu.HBM(a.shape, a.dtype) for a in list(srcs) + list(lands))
    out = pl.pallas_call(
        body, name=name, out_shape=thru,
        in_specs=[HBM_SPEC] * (2 * n) + [SEM_SPEC, SEM_SPEC] + [ANY_SPEC] * len(after), out_specs=(HBM_SPEC,) * (2 * n),
        input_output_aliases={i: i for i in range(2 * n)},
        compiler_params=pltpu.CompilerParams(has_side_effects=pltpu.SideEffectType.DATAFLOW_SIDE_EFFECTING),
    )(*srcs, *lands, send, recv, *after)
    return out[:n], out[n:]


def _swap_sibling(bufs, *, name):
    n = len(bufs)

    def body(*refs):
        src, out, send, recv = refs[:n], refs[n:2 * n], refs[-2], refs[-1]
        sib = (lax.axis_index("x"), lax.axis_index("y"), 1 - lax.axis_index("c"))
        copies = [_remote(src[w], out[w], send.at[w], recv.at[w], sib) for w in range(n)]
        for cp in copies:
            cp.start()
        for cp in copies:
            cp.wait()

    return pl.pallas_call(body, name=name, out_shape=tuple(S(a.shape, a.dtype) for a in bufs),
                          in_specs=[HBM_SPEC] * n, out_specs=(HBM_SPEC,) * n,
                          scratch_shapes=[pltpu.SemaphoreType.DMA((n,)), pltpu.SemaphoreType.DMA((n,))])(*bufs)


def _swap_start(bufs, *, name):
    n = len(bufs)
    lands = [lax.empty(b.shape, b.dtype) for b in bufs]

    def body(*refs):
        src, land, send, recv, token = refs[:n], refs[n:2 * n], refs[2 * n], refs[2 * n + 1], refs[-1]
        sib = (lax.axis_index("x"), lax.axis_index("y"), 1 - lax.axis_index("c"))
        for w in range(n):
            _remote(src[w], land[w], send.at[w], recv.at[w], sib).start()
        token[...] = jnp.zeros_like(token)

    thru = tuple(pltpu.HBM(a.shape, a.dtype) for a in list(bufs) + lands)
    out = pl.pallas_call(
        body, name=name,
        out_shape=(pltpu.SemaphoreType.DMA((n,)), pltpu.SemaphoreType.DMA((n,))) + thru + (S((SUBLANE, LANE), f32),),
        in_specs=[HBM_SPEC] * (2 * n), out_specs=(SEM_SPEC, SEM_SPEC) + (HBM_SPEC,) * (2 * n) + (VMEM_SPEC,),
        input_output_aliases={i: 2 + i for i in range(2 * n)},
        compiler_params=pltpu.CompilerParams(has_side_effects=pltpu.SideEffectType.DATAFLOW_SIDE_EFFECTING),
    )(*[_in_hbm(a) for a in list(bufs) + lands])
    return out[0], out[1], out[2:2 + n], out[2 + n:2 + 2 * n], out[-1]


def _swap_wait(send, recv, bufs, lands, after, *, name):
    n = len(bufs)

    def body(*refs):
        src, land, send, recv = refs[:n], refs[n:2 * n], refs[2 * n], refs[2 * n + 1]
        sib = (lax.axis_index("x"), lax.axis_index("y"), 1 - lax.axis_index("c"))
        for w in range(n):
            cp = _remote(src[w], land[w], send.at[w], recv.at[w], sib)
            cp.wait_send()
            cp.wait_recv()

    thru = tuple(pltpu.HBM(a.shape, a.dtype) for a in list(bufs) + list(lands))
    out = pl.pallas_call(
        body, name=name, out_shape=thru,
        in_specs=[HBM_SPEC] * (2 * n) + [SEM_SPEC, SEM_SPEC] + [ANY_SPEC] * len(after), out_specs=(HBM_SPEC,) * (2 * n),
        input_output_aliases={i: i for i in range(2 * n)},
        compiler_params=pltpu.CompilerParams(has_side_effects=pltpu.SideEffectType.DATAFLOW_SIDE_EFFECTING),
    )(*bufs, *lands, send, recv, *after)
    return out[:n], out[n:]


N_DEVICES = 8


def _allgather_devices(buf, *, name):
    R, C = buf.shape

    def body(in_ref, out_ref, send, recv, lsem):
        x, y, c = lax.axis_index("x"), lax.axis_index("y"), lax.axis_index("c")
        me = 4 * x + 2 * y + c
        mine = pltpu.make_async_copy(in_ref, out_ref.at[me], lsem)
        mine.start()
        flips = [(d >> 2 & 1, d >> 1 & 1, d & 1) for d in range(1, N_DEVICES)]
        peers = [(1 - x if fx else x, 1 - y if fy else y, 1 - c if fc else c) for fx, fy, fc in flips]
        sends = []
        for k, peer in enumerate(peers):
            cp = _remote(in_ref, out_ref.at[me], send.at[k], recv.at[k], peer)
            cp.start()
            sends.append(cp)
        for k, (px, py, pc) in enumerate(peers):
            _remote(in_ref, out_ref.at[4 * px + 2 * py + pc], send.at[k], recv.at[k], (px, py, pc)).wait_recv()
        for cp in sends:
            cp.wait_send()
        mine.wait()

    return pl.pallas_call(body, name=name, out_shape=S((N_DEVICES, R, C), buf.dtype), in_specs=[HBM_SPEC], out_specs=HBM_SPEC,
                          scratch_shapes=[pltpu.SemaphoreType.DMA((N_DEVICES - 1,)), pltpu.SemaphoreType.DMA((N_DEVICES - 1,)),
                                          pltpu.SemaphoreType.DMA(())])(buf)


def _sum_slots(buf, *, name):
    ns, R, C = buf.shape
    tr = _pick(R, 512)
    assert R % tr == 0

    def body(*refs):
        acc = refs[0][...]
        for r in refs[1:ns]:
            acc = acc + r[...]
        refs[ns][...] = acc

    specs = [pl.BlockSpec((None, tr, C), functools.partial(lambda s, i: (s, i, 0), s)) for s in range(ns)]
    return pl.pallas_call(body, name=name, out_shape=S((R, C), buf.dtype), grid=(R // tr,), in_specs=specs,
                          out_specs=pl.BlockSpec((tr, C), lambda i: (i, 0)), compiler_params=_cp("parallel"))(*([buf] * ns))


ADAMW_BLOCK_BYTES = 1 << 20


def _adamw(w, m, v, groups, *, name, layer=None, prev=None):
    shape = w.shape if layer is None else w.shape[1:]
    C = shape[-1]
    Rr = math.prod(shape[:-1])
    tr = Rr
    if Rr * C * 4 > ADAMW_BLOCK_BYTES:
        tr = max(t for t in range(2 * SUBLANE, Rr + 1, 2 * SUBLANE) if Rr % t == 0 and t * C * 4 <= ADAMW_BLOCK_BYTES)
    c1 = 1.0 / (1.0 - ADAM_B1 ** ADAM_STEP)
    c2 = 1.0 / (1.0 - ADAM_B2 ** ADAM_STEP)
    if layer is None:
        to2 = lambda t: t.reshape(Rr, C)
        spec = pl.BlockSpec((tr, C), lambda i: (i, 0))
        res_shape = S((Rr, C), f32)
    else:
        to2 = lambda t: t.reshape(layer[1], Rr, C)
        spec = pl.BlockSpec((None, tr, C), functools.partial(lambda l, i: (l, i, 0), layer[0]))
        res_shape = S((layer[1], Rr, C), f32)
    wspec, spec = spec, pl.BlockSpec((tr, C), lambda i: (i, 0))
    g_specs, g_args, sizes = [], [], []
    for grp in groups:
        sizes.append(len(grp))
        for term in grp:
            if isinstance(term, tuple):
                arr, slot = term
                g_specs.append(pl.BlockSpec((None, tr, C), functools.partial(lambda s, i: (s, i, 0), slot)))
                g_args.append(arr.reshape(arr.shape[0], Rr, C))
            else:
                g_specs.append(spec)
                g_args.append(term.reshape(Rr, C))
    nterms = len(g_args)
    prev = () if prev is None else tuple(to2(t) for t in prev)

    def body(w_ref, m_ref, v_ref, *rest):
        t_refs, (g_ref, d_ref, mo_ref, vo_ref) = rest[:nterms], rest[-4:]
        g, pos = None, 0
        for size in sizes:
            part = None
            for r in t_refs[pos:pos + size]:
                t = r[...].astype(f32)
                part = t if part is None else part + t
            pos += size
            g = part if g is None else g + part
        mn = ADAM_B1 * m_ref[...] + (1.0 - ADAM_B1) * g
        vn = ADAM_B2 * v_ref[...] + (1.0 - ADAM_B2) * (g * g)
        g_ref[...] = g
        mo_ref[...] = mn
        vo_ref[...] = vn
        d_ref[...] = -ADAM_LR * ((mn * c1) / (jnp.sqrt(vn * c2) + ADAM_EPS) + ADAM_WD * w_ref[...])

    out = pl.pallas_call(body, name=name, out_shape=(res_shape,) * 4, grid=(Rr // tr,),
                         in_specs=[wspec] * 3 + g_specs + [ANY_SPEC] * len(prev), out_specs=(wspec,) * 4,
                         input_output_aliases={3 + nterms + k: k for k in range(len(prev))},
                         compiler_params=_cp("parallel"))(to2(w), to2(m), to2(v), *g_args, *prev)
    return tuple(o.reshape(w.shape) for o in out)


def _pack_rows(parts, dtype):
    flat = jnp.concatenate([p.reshape(-1).astype(dtype) for p in parts])
    n = flat.shape[0]
    unit = PACK_COLS * 2 * SUBLANE
    padded = -(-n // unit) * unit
    return jnp.pad(flat, (0, padded - n)).reshape(padded // PACK_COLS, PACK_COLS)


def _unpack_rows(flat2d, shapes):
    flat = flat2d.reshape(-1)
    out, off = [], 0
    for shp in shapes:
        n = math.prod(shp)
        out.append(flat[off:off + n].reshape(shp))
        off += n
    return out


def _gather_weights(local, names, dtype, *, name):
    shapes = [local[n].shape for n in names]
    got = _all_gather_chips(_pack_rows([local[n] for n in names], dtype), name=name)
    per_chip = [_unpack_rows(got[s], shapes) for s in range(N_CHIPS)]
    return {n: jnp.concatenate([per_chip[s][k] for s in range(N_CHIPS)], axis=SHARD_AXIS[n]) for k, n in enumerate(names)}


def kernel(x, mem, norm_g, ssm_in_w, ssm_conv_w, ssm_conv_b, ssm_dt_bias, ssm_A_log, ssm_D, ssm_norm_g, ssm_out_w, cf_pw1_w, cf_pw1_b, cf_dw_w, cf_dw_b, cf_ln_g, cf_ln_b, cf_pw2_w, cf_pw2_b, xa_mem_g, xa_q_w, xa_kv_w, xa_o_w, ffn_in_w, ffn_conv_w, ffn_conv_b, ffn_out_w, loss_target, m_norm_g, m_ssm_in_w, m_ssm_conv_w, m_ssm_conv_b, m_ssm_dt_bias, m_ssm_A_log, m_ssm_D, m_ssm_norm_g, m_ssm_out_w, m_cf_pw1_w, m_cf_pw1_b, m_cf_dw_w, m_cf_dw_b, m_cf_ln_g, m_cf_ln_b, m_cf_pw2_w, m_cf_pw2_b, m_xa_mem_g, m_xa_q_w, m_xa_kv_w, m_xa_o_w, m_ffn_in_w, m_ffn_conv_w, m_ffn_conv_b, m_ffn_out_w, v_norm_g, v_ssm_in_w, v_ssm_conv_w, v_ssm_conv_b, v_ssm_dt_bias, v_ssm_A_log, v_ssm_D, v_ssm_norm_g, v_ssm_out_w, v_cf_pw1_w, v_cf_pw1_b, v_cf_dw_w, v_cf_dw_b, v_cf_ln_g, v_cf_ln_b, v_cf_pw2_w, v_cf_pw2_b, v_xa_mem_g, v_xa_q_w, v_xa_kv_w, v_xa_o_w, v_ffn_in_w, v_ffn_conv_w, v_ffn_conv_b, v_ffn_out_w):
    w_local = dict(zip(WEIGHT_NAMES, (norm_g, ssm_in_w, ssm_conv_w, ssm_conv_b, ssm_dt_bias, ssm_A_log, ssm_D, ssm_norm_g,
                                      ssm_out_w, cf_pw1_w, cf_pw1_b, cf_dw_w, cf_dw_b, cf_ln_g, cf_ln_b, cf_pw2_w, cf_pw2_b,
                                      xa_mem_g, xa_q_w, xa_kv_w, xa_o_w, ffn_in_w, ffn_conv_w, ffn_conv_b, ffn_out_w)))
    m_local = dict(zip(WEIGHT_NAMES, (m_norm_g, m_ssm_in_w, m_ssm_conv_w, m_ssm_conv_b, m_ssm_dt_bias, m_ssm_A_log, m_ssm_D,
                                      m_ssm_norm_g, m_ssm_out_w, m_cf_pw1_w, m_cf_pw1_b, m_cf_dw_w, m_cf_dw_b, m_cf_ln_g,
                                      m_cf_ln_b, m_cf_pw2_w, m_cf_pw2_b, m_xa_mem_g, m_xa_q_w, m_xa_kv_w, m_xa_o_w,
                                      m_ffn_in_w, m_ffn_conv_w, m_ffn_conv_b, m_ffn_out_w)))
    v_local = dict(zip(WEIGHT_NAMES, (v_norm_g, v_ssm_in_w, v_ssm_conv_w, v_ssm_conv_b, v_ssm_dt_bias, v_ssm_A_log, v_ssm_D,
                                      v_ssm_norm_g, v_ssm_out_w, v_cf_pw1_w, v_cf_pw1_b, v_cf_dw_w, v_cf_dw_b, v_cf_ln_g,
                                      v_cf_ln_b, v_cf_pw2_w, v_cf_pw2_b, v_xa_mem_g, v_xa_q_w, v_xa_kv_w, v_xa_o_w,
                                      v_ffn_in_w, v_ffn_conv_w, v_ffn_conv_b, v_ffn_out_w)))

    small = [n for n in WEIGHT_NAMES if n not in MATMUL_WEIGHTS]
    small_sharded = [n for n in small if SHARD_AXIS[n] is not None]
    W = {n: w_local[n] for n in small if SHARD_AXIS[n] is None}
    W.update(_gather_weights(w_local, small_sharded, f32, name="gather_small_weights"))

    def layer_index(n, i):
        return i // 2 if n in MIXER_WEIGHTS else i

    def keys_of(i, parts):
        return [(n, layer_index(n, i)) for part in parts for n in _layer_matmul_weights(i, part)]

    def shards(keys):
        return [w_local[n][l].astype(bf16) for n, l in keys]

    def usable(n, a):
        return a.reshape(N_CHIPS * a.shape[1], a.shape[2]) if n in ROW_SHARDED else a

    mixer0 = keys_of(0, ("mixer",))
    got0 = _gather_matmul_weights([s.reshape(1, 2, s.shape[0] // 2, s.shape[1]) for s in shards(mixer0)],
                                  name="gather_layer0_mixer")
    gather_groups = {(0, "rest"): keys_of(0, ("rest",))}
    gather_groups.update({(i, "mixer"): keys_of(i, ("mixer", "rest")) for i in range(1, DEPTH)})
    gathers, tokens, landed = {}, [], {}
    for gkey in sorted(gather_groups):
        send, recv, srcs, lands, token = _spread_start(shards(gather_groups[gkey]), False, name="gather_start_%d_%s" % gkey,
                                                       after=(got0[0], W[small_sharded[0]]))
        gathers[gkey] = (send, recv, srcs, lands)
        tokens.append(token)

    def layer_weights(i, part, after):
        if (i, part) == (0, "mixer"):
            landed.update({k: g.reshape((N_CHIPS, 2 * g.shape[3], g.shape[4])) for k, g in zip(mixer0, got0)})
        elif (i, part) in gathers:
            _, lands = _spread_wait(*gathers[i, part], (after,), False, name="gather_wait_%d_%s" % (i, part))
            landed.update(zip(gather_groups[i, part], lands))
        return {n: usable(n, landed[n, layer_index(n, i)]) for n in _layer_matmul_weights(i, part)}

    pending, scatters, swaps, own, sib = {}, {}, {}, {}, {}

    def scatter_start(gkey, keys, gl):
        send, recv, srcs, lands, token = _spread_start([gl[k] for k in keys], True, name="grads_start_%d_%s" % gkey)
        scatters[gkey] = (keys, send, recv, srcs, lands)
        return token

    def layer_grads(i, part, gl):
        grads = {(n, layer_index(n, i)): gl[n] for n in _layer_matmul_weights(i, part)}
        behind = []
        if part == "rest":
            if i + 1 < DEPTH:
                keys, send, recv, srcs, lands = scatters.pop((i + 1, "mixer"))
                _, lands = _spread_wait(send, recv, srcs, lands, (grads['xa_kv_w', i],), True,
                                        name="grads_wait_%d" % (i + 1))
                send, recv, srcs, lands, token = _swap_start(lands, name="grads_swap_start_%d" % (i + 1))
                swaps[i + 1] = (keys, send, recv, srcs, lands)
                behind.append(token)
            if i == 0:
                behind.append(scatter_start((0, "rest"), list(grads), grads))
            else:
                pending.update(grads)
        else:
            pending.update(grads)
            if i == 0:
                keys, send, recv, srcs, lands = scatters.pop((0, "rest"))
                _, lands = _spread_wait(send, recv, srcs, lands, (grads['ssm_in_w', 0],), True, name="grads_wait_0_rest")
                send, recv, srcs, lands, token = _swap_start(lands, name="grads_swap_start_0")
                swaps[0] = (keys, send, recv, srcs, lands)
                behind.append(token)
            behind.append(scatter_start((i, "mixer"), list(pending), dict(pending)))
            pending.clear()
        return behind

    sse, gx, gsmall = _device_step(x[0], mem[0], loss_target[0], W, layer_weights, layer_grads, tuple(tokens))

    loss = lax.psum(0.5 * sse[0, 0] / D_MODEL, ("x", "y", "c"))

    last_keys, last_lands = [], []
    for gkey in sorted(scatters):
        keys, send, recv, srcs, lands = scatters[gkey]
        _, lands = _spread_wait(send, recv, srcs, lands, (gx,), True, name="grads_wait_%d_%s" % gkey)
        last_keys += keys
        last_lands += list(lands)
    own.update(zip(last_keys, last_lands))
    sib.update(zip(last_keys, _swap_sibling(last_lands, name="grads_swap_last")))
    for i in sorted(swaps):
        keys, send, recv, srcs, lands = swaps[i]
        mine, theirs = _swap_wait(send, recv, srcs, lands, (gx,), name="grads_swap_wait_%d" % i)
        own.update(zip(keys, mine))
        sib.update(zip(keys, theirs))
    small_shapes = [gsmall[n].shape for n in small]
    slots = _allgather_devices(_pack_rows([gsmall[n] for n in small], f32), name="allgather_small_grads")
    gsum = dict(zip(small, _unpack_rows(_sum_slots(slots, name="sum_small_grads"), small_shapes)))
    chip = 2 * lax.axis_index("x") + lax.axis_index("y")

    res = {}
    for n in MATMUL_WEIGHTS:
        layers, out = w_local[n].shape[0], None
        for l in range(layers):
            groups = [[(own[n, l], s) for s in range(N_CHIPS)], [(sib[n, l], s) for s in range(N_CHIPS)]]
            out = _adamw(w_local[n], m_local[n], v_local[n], groups, name="adamw_%s_%d" % (n, l), layer=(l, layers), prev=out)
        res[n] = out
    for n in small:
        g, ax = gsum[n], SHARD_AXIS[n]
        if ax is not None:
            width = w_local[n].shape[ax]
            g = lax.dynamic_slice_in_dim(g, chip * width, width, axis=ax)
        res[n] = _adamw(w_local[n], m_local[n], v_local[n], [[g]], name="adamw_" + n)
    return (loss, gx[None], *[res[n][0] for n in WEIGHT_NAMES], *[res[n][1] for n in WEIGHT_NAMES],
            *[res[n][2] for n in WEIGHT_NAMES], *[res[n][3] for n in WEIGHT_NAMES])
```

```python
import functools
import math

import jax
import jax.numpy as jnp
from jax import lax
from jax.experimental import pallas as pl
from jax.experimental.pallas import tpu as pltpu

f32 = jnp.float32
bf16 = jnp.bfloat16
S = jax.ShapeDtypeStruct

D_MODEL = 1024
DEPTH = 4
D_INNER = 2048
HEAD_DIM = 64
N_GROUPS = 4
HEADS_PER_GROUP = 8
N_SSM_HEADS = 32
D_STATE = 128
CHUNK = 128
SSM_CONV = 4
CONV_DIM = 3072
CF_KERNEL = 31
N_MEM = 256
XA_HEADS = 4
XA_HEAD_DIM = 256
D_FF = 2816
FFN_CONV = 3
EPS = 1e-6
ADAM_LR, ADAM_B1, ADAM_B2, ADAM_EPS, ADAM_WD, ADAM_STEP = 0.001, 0.9, 0.999, 1e-08, 0.01, 10

LANE = 128
SUBLANE = 8
ROW_SUB = 64
VMEM_LIMIT = 56 * 1024 * 1024
N_CHIPS = 4
PACK_COLS = 1024

WEIGHT_NAMES = ['norm_g', 'ssm_in_w', 'ssm_conv_w', 'ssm_conv_b', 'ssm_dt_bias', 'ssm_A_log', 'ssm_D', 'ssm_norm_g',
                'ssm_out_w', 'cf_pw1_w', 'cf_pw1_b', 'cf_dw_w', 'cf_dw_b', 'cf_ln_g', 'cf_ln_b', 'cf_pw2_w', 'cf_pw2_b',
                'xa_mem_g', 'xa_q_w', 'xa_kv_w', 'xa_o_w', 'ffn_in_w', 'ffn_conv_w', 'ffn_conv_b', 'ffn_out_w']
SHARD_AXIS = {'norm_g': 2, 'ssm_in_w': 2, 'ssm_conv_w': 2, 'ssm_conv_b': None, 'ssm_dt_bias': None, 'ssm_A_log': None,
              'ssm_D': None, 'ssm_norm_g': None, 'ssm_out_w': 1, 'cf_pw1_w': 2, 'cf_pw1_b': 1, 'cf_dw_w': 2, 'cf_dw_b': 1,
              'cf_ln_g': 1, 'cf_ln_b': 1, 'cf_pw2_w': 1, 'cf_pw2_b': 1, 'xa_mem_g': None, 'xa_q_w': 1, 'xa_kv_w': 2,
              'xa_o_w': 1, 'ffn_in_w': 2, 'ffn_conv_w': 2, 'ffn_conv_b': None, 'ffn_out_w': 1}
MATMUL_WEIGHTS = ('ssm_in_w', 'ssm_out_w', 'cf_pw1_w', 'cf_pw2_w', 'xa_q_w', 'xa_kv_w', 'xa_o_w', 'ffn_in_w', 'ffn_out_w')


def _cp(*sem):
    return pltpu.CompilerParams(dimension_semantics=tuple(sem), vmem_limit_bytes=VMEM_LIMIT)


def _pick(dim, pref):
    if dim <= pref:
        return dim
    best = None
    for t in range(LANE, pref + 1, LANE):
        if dim % t == 0:
            best = t
    assert best is not None, (dim, pref)
    return best


def _sigmoid(x):
    return 1.0 / (1.0 + jnp.exp(-x))


def _silu(x):
    return x * _sigmoid(x)


def _dsilu(x):
    s = _sigmoid(x)
    return s * (1.0 + x * (1.0 - s))


def _softplus(x):
    return jnp.maximum(x, 0.0) + jnp.log(1.0 + jnp.exp(-jnp.abs(x)))


_DN = {"nn": (((1,), (0,)), ((), ())), "nt": (((1,), (1,)), ((), ())), "tn": (((0,), (0,)), ((), ()))}


def _mm(a, b, mode, *, name, out_dtype=f32, bias=None, add=None, b_shards=False, out_shards=False, after=()):
    bshape = (b.shape[1], b.shape[2] * N_CHIPS) if b_shards else b.shape
    if mode == "nn":
        (M, K), (K2, N) = a.shape, bshape
    elif mode == "nt":
        (M, K), (N, K2) = a.shape, bshape
    else:
        (K, M), (K2, N) = a.shape, bshape
    assert K == K2, (a.shape, b.shape, mode)
    n_unit = N // N_CHIPS if ((b_shards and mode == "nn") or out_shards) else N
    k_unit = K // N_CHIPS if (b_shards and mode == "nt") else K
    tm, tn, tk = _pick(M, 1024), _pick(n_unit, 1408), _pick(k_unit, 1408)
    nk, nj_u, nk_u = K // tk, n_unit // tn, k_unit // tk
    a_spec = {"nn": pl.BlockSpec((tm, tk), lambda i, j, k: (i, k)), "nt": pl.BlockSpec((tm, tk), lambda i, j, k: (i, k)),
              "tn": pl.BlockSpec((tk, tm), lambda i, j, k: (k, i))}[mode]
    if not b_shards:
        b_spec = {"nn": pl.BlockSpec((tk, tn), lambda i, j, k: (k, j)), "nt": pl.BlockSpec((tn, tk), lambda i, j, k: (j, k)),
                  "tn": pl.BlockSpec((tk, tn), lambda i, j, k: (k, j))}[mode]
    else:
        b_spec = {"nn": pl.BlockSpec((None, tk, tn), lambda i, j, k: (j // nj_u, k, j % nj_u)),
                  "nt": pl.BlockSpec((None, tn, tk), lambda i, j, k: (k // nk_u, j, k % nk_u))}[mode]
    in_specs, args = [a_spec, b_spec], [a, b]
    if bias is not None:
        in_specs.append(pl.BlockSpec((1, tn), lambda i, j, k: (0, j)))
        args.append(bias)
    if add is not None:
        in_specs.append(pl.BlockSpec((tm, tn), lambda i, j, k: (i, j)))
        args.append(add)
    in_specs += [pl.BlockSpec(memory_space=pl.ANY)] * len(after)
    args += list(after)
    if not out_shards:
        out_shape, out_spec = S((M, N), out_dtype), pl.BlockSpec((tm, tn), lambda i, j, k: (i, j))
    else:
        out_shape = S((N_CHIPS, M, n_unit), out_dtype)
        out_spec = pl.BlockSpec((None, tm, tn), lambda i, j, k: (j // nj_u, i, j % nj_u))
    dn = _DN[mode]
    has_bias, has_add = bias is not None, add is not None

    def body(a_ref, b_ref, *rest):
        rest = list(rest)
        bias_ref = rest.pop(0) if has_bias else None
        add_ref = rest.pop(0) if has_add else None
        rest = rest[len(after):]
        o_ref = rest[0]

        def finish(r):
            if has_bias:
                r = r + bias_ref[...]
            if has_add:
                r = r + add_ref[...].astype(f32)
            o_ref[...] = r.astype(out_dtype)

        part = lax.dot_general(a_ref[...].astype(bf16), b_ref[...].astype(bf16), dn, preferred_element_type=f32)
        if nk == 1:
            finish(part)
            return
        acc_ref = rest[1]
        k = pl.program_id(2)

        @pl.when(k == 0)
        def _():
            acc_ref[...] = part

        @pl.when(k > 0)
        def _():
            acc_ref[...] += part

        @pl.when(k == nk - 1)
        def _():
            finish(acc_ref[...])

    return pl.pallas_call(
        body, name=name, out_shape=out_shape, grid=(M // tm, N // tn, nk),
        in_specs=in_specs, out_specs=out_spec, scratch_shapes=[pltpu.VMEM((tm, tn), f32)] if nk > 1 else [],
        compiler_params=_cp("parallel", "parallel", "arbitrary"))(*args)


def _rows(tm, C):
    return pl.BlockSpec((tm, C), lambda i: (i, 0))


def _const(shape):
    return pl.BlockSpec(shape, lambda i: tuple(0 for _ in shape))


def _rms_val(x, g):
    r = lax.rsqrt(jnp.mean(x * x, axis=-1, keepdims=True) + EPS)
    return x * r * g


def _rms_bwd_val(x, g, dy):
    r = lax.rsqrt(jnp.mean(x * x, axis=-1, keepdims=True) + EPS)
    xn = x * r
    dxh = dy * g
    dx = r * (dxh - xn * jnp.mean(dxh * xn, axis=-1, keepdims=True))
    return dx, jnp.sum(dy * xn, axis=0, keepdims=True)


ANY_SPEC = pl.BlockSpec(memory_space=pl.ANY)


def _rmsnorm_fwd(x, g, *, name, after=()):
    L, C = x.shape
    tm = _pick(L, 512)

    def body(x_ref, g_ref, *rest):
        rest[-1][...] = _rms_val(x_ref[...], g_ref[...]).astype(bf16)

    return pl.pallas_call(body, name=name, out_shape=S((L, C), bf16), grid=(L // tm,),
                          in_specs=[_rows(tm, C), _const((1, C))] + [ANY_SPEC] * len(after), out_specs=_rows(tm, C),
                          compiler_params=_cp("parallel"))(x, g, *after)


def _resid_norm_fwd(x, mix, g_post, g_next, *, name):
    L, C = x.shape
    tm = _pick(L, 512)
    want_h = g_next is not None

    def body(x_ref, m_ref, gp_ref, *rest):
        xn = x_ref[...] + _rms_val(m_ref[...], gp_ref[...])
        if want_h:
            gn_ref, xo_ref, h_ref = rest
            h_ref[...] = _rms_val(xn, gn_ref[...]).astype(bf16)
        else:
            (xo_ref,) = rest
        xo_ref[...] = xn

    in_specs = [_rows(tm, C), _rows(tm, C), _const((1, C))]
    args = [x, mix, g_post]
    out_shape, out_specs = [S((L, C), f32)], [_rows(tm, C)]
    if want_h:
        in_specs.append(_const((1, C)))
        args.append(g_next)
        out_shape.append(S((L, C), bf16))
        out_specs.append(_rows(tm, C))
    out = pl.pallas_call(body, name=name, out_shape=tuple(out_shape), grid=(L // tm,), in_specs=in_specs,
                         out_specs=tuple(out_specs), compiler_params=_cp("parallel"))(*args)
    return (out[0], out[1]) if want_h else (out[0], None)


def _norm_bwd(x, g, dy, *, name, add=None, out_dtype=f32, after=()):
    L, C = x.shape
    tm = _pick(L, 512)
    has_add = add is not None

    def body(x_ref, g_ref, dy_ref, *rest):
        rest = list(rest)
        add_ref = rest.pop(0) if has_add else None
        dx_ref, dg_ref, cs_ref = rest[-3:]
        i = pl.program_id(0)

        @pl.when(i == 0)
        def _():
            dg_ref[...] = jnp.zeros_like(dg_ref)
            cs_ref[...] = jnp.zeros_like(cs_ref)

        dx, dg = _rms_bwd_val(x_ref[...], g_ref[...], dy_ref[...].astype(f32))
        dg_ref[...] += dg
        cs_ref[...] += jnp.sum(dx, axis=0, keepdims=True)
        if has_add:
            dx = dx + add_ref[...]
        dx_ref[...] = dx.astype(out_dtype)

    in_specs = [_rows(tm, C), _const((1, C)), _rows(tm, C)]
    args = [x, g, dy]
    if has_add:
        in_specs.append(_rows(tm, C))
        args.append(add)
    in_specs += [ANY_SPEC] * len(after)
    args += list(after)
    return pl.pallas_call(body, name=name, out_shape=(S((L, C), out_dtype), S((1, C), f32), S((1, C), f32)),
                          grid=(L // tm,), in_specs=in_specs,
                          out_specs=(_rows(tm, C), _const((1, C)), _const((1, C))),
                          compiler_params=_cp("arbitrary"))(*args)


def _norm_bwd_chain(x, g, dy, add, x2, g2, *, name, after=()):
    L, C = x.shape
    tm = _pick(L, 512)

    def body(x_ref, g_ref, dy_ref, add_ref, x2_ref, g2_ref, *rest):
        G_ref, dg_ref, d2_ref, dg2_ref, cs2_ref = rest[-5:]
        i = pl.program_id(0)

        @pl.when(i == 0)
        def _():
            dg_ref[...] = jnp.zeros_like(dg_ref)
            dg2_ref[...] = jnp.zeros_like(dg2_ref)
            cs2_ref[...] = jnp.zeros_like(cs2_ref)

        dx, dg = _rms_bwd_val(x_ref[...], g_ref[...], dy_ref[...].astype(f32))
        G = dx + add_ref[...]
        dg_ref[...] += dg
        G_ref[...] = G
        d2, dg2 = _rms_bwd_val(x2_ref[...], g2_ref[...], G)
        dg2_ref[...] += dg2
        cs2_ref[...] += jnp.sum(d2, axis=0, keepdims=True)
        d2_ref[...] = d2.astype(bf16)

    row, vec = _rows(tm, C), _const((1, C))
    return pl.pallas_call(body, name=name,
                          out_shape=(S((L, C), f32), S((1, C), f32), S((L, C), bf16), S((1, C), f32), S((1, C), f32)),
                          grid=(L // tm,), in_specs=[row, vec, row, row, row, vec] + [ANY_SPEC] * len(after),
                          out_specs=(row, vec, row, vec, vec),
                          compiler_params=_cp("arbitrary"))(x, g, dy, add, x2, g2, *after)


def _loss_fwd_bwd(y, target, *, name):
    L, C = y.shape
    tm = _pick(L, 512)

    def body(y_ref, t_ref, acc_ref, dy_ref):
        i = pl.program_id(0)

        @pl.when(i == 0)
        def _():
            acc_ref[...] = jnp.zeros_like(acc_ref)

        e = y_ref[...] - t_ref[...]
        rs = jnp.sum(e * e, axis=-1, keepdims=True)
        acc_ref[...] += jnp.broadcast_to(jnp.sum(rs, axis=0, keepdims=True), (1, LANE))
        dy_ref[...] = e * (1.0 / C)

    return pl.pallas_call(body, name=name, out_shape=(S((1, LANE), f32), S((L, C), f32)), grid=(L // tm,),
                          in_specs=[_rows(tm, C), _rows(tm, C)], out_specs=(_const((1, LANE)), _rows(tm, C)),
                          compiler_params=_cp("arbitrary"))(y, target)


def _halo_rows(K):
    return SUBLANE if K - 1 <= SUBLANE else 32


def _prev_halo_spec(tm, H, C):
    return pl.BlockSpec((H, C), lambda i: (jnp.maximum(i * (tm // H) - 1, 0), 0))


def _next_halo_spec(tm, H, C, L):
    return pl.BlockSpec((H, C), lambda i: (jnp.minimum((i + 1) * (tm // H), L // H - 1), 0))


def _down_views(ext, H, n, K):
    bases, views = {}, []
    for s in range(K):
        q, r = divmod(s, SUBLANE)
        if r not in bases:
            bases[r] = ext if r == 0 else pltpu.roll(ext, r, axis=0)
        views.append(bases[r][H - SUBLANE * q:H - SUBLANE * q + n])
    return views


def _up_views(ext, n, K):
    bases, views = {}, []
    for s in range(K):
        q, r = divmod(s, SUBLANE)
        if r not in bases:
            bases[r] = ext if r == 0 else pltpu.roll(ext, ext.shape[0] - r, axis=0)
        views.append(bases[r][SUBLANE * q:SUBLANE * q + n])
    return views


def _causal_conv(ext, H, w_ref, K):
    views = _down_views(ext, H, ext.shape[0] - H, K)
    acc = None
    for k in range(K):
        term = views[K - 1 - k] * w_ref[k:k + 1, :]
        acc = term if acc is None else acc + term
    return acc


def _anticausal_conv(ext, tm, w_ref, K):
    views = _up_views(ext, tm, K)
    acc = None
    for k in range(K):
        term = views[K - 1 - k] * w_ref[k:k + 1, :]
        acc = term if acc is None else acc + term
    return acc


def _tap_grads(dw_ref, d_cur, x_ext, H, K):
    views = _down_views(x_ext, H, d_cur.shape[0], K)
    for k in range(K):
        dw_ref[k:k + 1, :] += jnp.sum(d_cur * views[K - 1 - k], axis=0, keepdims=True)


def _fold_rows(x):
    acc = x[0:SUBLANE]
    for r in range(SUBLANE, x.shape[0], SUBLANE):
        acc = acc + x[r:r + SUBLANE]
    return acc


def _pad_taps(w, K):
    return jnp.pad(w, ((0, _halo_rows(K) - K), (0, 0)))


def _conv_act_fwd(x, w, b, *, K, act, name, out_dtype, tm_pref=256):
    L, C = x.shape
    H = _halo_rows(K)
    tm = _pick(L, tm_pref)
    Co = C if act == "silu" else C // 2

    rs = min(ROW_SUB, tm)

    def body(h_ref, x_ref, w_ref, b_ref, o_ref, ext_scr):
        i = pl.program_id(0)
        ext_scr[0:H] = jnp.where(i > 0, h_ref[...], 0.0)
        ext_scr[H:] = x_ref[...]
        for j in range(Co // LANE):
            lanes = [j] if act == "silu" else [j, j + Co // LANE]
            wb = [(w_ref[:, c * LANE:(c + 1) * LANE], b_ref[:, c * LANE:(c + 1) * LANE]) for c in lanes]
            for r0 in range(0, tm, rs):
                us = [_causal_conv(ext_scr[r0:r0 + rs + H, c * LANE:(c + 1) * LANE], H, wc, K) + bc
                      for c, (wc, bc) in zip(lanes, wb)]
                y = _silu(us[0]) if act == "silu" else _silu(us[0]) * us[1]
                o_ref[r0:r0 + rs, j * LANE:(j + 1) * LANE] = y.astype(out_dtype)

    return pl.pallas_call(body, name=name, out_shape=S((L, Co), out_dtype), grid=(L // tm,),
                          in_specs=[_prev_halo_spec(tm, H, C), _rows(tm, C), _const((H, C)), _const((1, C))],
                          out_specs=_rows(tm, Co), scratch_shapes=[pltpu.VMEM((H + tm, C), f32)],
                          compiler_params=_cp("parallel"))(x, x, w, b)


def _conv_act_bwd(x, dparts, w, b, *, K, act, name, tm_pref=256):
    L, C = x.shape
    H = _halo_rows(K)
    tm = _pick(L, tm_pref)
    nb = L // tm
    Co = C if act == "silu" else C // 2
    nparts = len(dparts)

    rs = min(ROW_SUB, tm)

    def body(hp_ref, x_ref, hn_ref, w_ref, b_ref, *rest):
        d_refs, dn_refs = rest[:nparts], rest[nparts:2 * nparts]
        dx_ref, dw_ref, db_ref, ext_scr, d_scr = rest[2 * nparts:]
        i = pl.program_id(0)

        @pl.when(i == 0)
        def _():
            dw_ref[...] = jnp.zeros_like(dw_ref)
            db_ref[...] = jnp.zeros_like(db_ref)

        ext_scr[0:H] = jnp.where(i > 0, hp_ref[...], 0.0)
        ext_scr[H:H + tm] = x_ref[...]
        ext_scr[H + tm:] = jnp.where(i < nb - 1, hn_ref[...], 0.0)
        off = 0
        for r, rn in zip(d_refs, dn_refs):
            d_scr[0:tm, off:off + r.shape[1]] = r[...].astype(f32)
            d_scr[tm:, off:off + r.shape[1]] = jnp.where(i < nb - 1, rn[...].astype(f32), 0.0)
            off += r.shape[1]
        for j in range(Co // LANE):
            lanes = [j] if act == "silu" else [j, j + Co // LANE]
            wb = [(w_ref[:, c * LANE:(c + 1) * LANE], b_ref[:, c * LANE:(c + 1) * LANE]) for c in lanes]
            db_acc = [jnp.zeros((SUBLANE, LANE), f32) for _ in lanes]
            dw_acc = [[jnp.zeros((SUBLANE, LANE), f32) for _ in range(K)] for _ in lanes]
            for r0 in range(0, tm, rs):
                xvs = [_down_views(ext_scr[r0:r0 + rs + 2 * H, c * LANE:(c + 1) * LANE], H, rs + H, K) for c in lanes]
                us = [sum(xv[K - 1 - k] * wc[k:k + 1, :] for k in range(K)) + bc for xv, (wc, bc) in zip(xvs, wb)]
                d = d_scr[r0:r0 + rs + H, j * LANE:(j + 1) * LANE]
                dus = [d * _dsilu(us[0])] if act == "silu" else [d * us[1] * _dsilu(us[0]), d * _silu(us[0])]
                for q, (c, xv, du, (wc, _)) in enumerate(zip(lanes, xvs, dus, wb)):
                    dx_ref[r0:r0 + rs, c * LANE:(c + 1) * LANE] = _anticausal_conv(du, rs, wc, K).astype(bf16)
                    du_cur = du[:rs]
                    db_acc[q] = db_acc[q] + _fold_rows(du_cur)
                    for k in range(K):
                        dw_acc[q][k] = dw_acc[q][k] + _fold_rows(du_cur * xv[K - 1 - k][:rs])
            for q, c in enumerate(lanes):
                db_ref[:, c * LANE:(c + 1) * LANE] += jnp.sum(db_acc[q], axis=0, keepdims=True)
                for k in range(K):
                    dw_ref[k:k + 1, c * LANE:(c + 1) * LANE] += jnp.sum(dw_acc[q][k], axis=0, keepdims=True)

    in_specs = [_prev_halo_spec(tm, H, C), _rows(tm, C), _next_halo_spec(tm, H, C, L), _const((H, C)), _const((1, C))]
    in_specs += [_rows(tm, p.shape[1]) for p in dparts] + [_next_halo_spec(tm, H, p.shape[1], L) for p in dparts]
    return pl.pallas_call(body, name=name, out_shape=(S((L, C), bf16), S((H, C), f32), S((1, C), f32)), grid=(nb,),
                          in_specs=in_specs, out_specs=(_rows(tm, C), _const((H, C)), _const((1, C))),
                          scratch_shapes=[pltpu.VMEM((tm + 2 * H, C), f32), pltpu.VMEM((tm + H, Co), f32)],
                          compiler_params=_cp("arbitrary"))(x, x, x, w, b, *dparts, *dparts)


def _cf_fwd(u, dw_w, dw_b, ln_g, ln_b, *, name):
    L, C2 = u.shape
    C = C2 // 2
    K, H = CF_KERNEL, _halo_rows(CF_KERNEL)
    tm = _pick(L, 256)

    rs = min(ROW_SUB, tm)
    nl = C // LANE

    def body(h_ref, u_ref, w_ref, b_ref, g_ref, lb_ref, c_ref, s_ref, u_scr):
        i = pl.program_id(0)
        u_scr[0:H] = jnp.where(i > 0, h_ref[...], 0.0)
        u_scr[H:] = u_ref[...]
        for j in range(nl):
            wj, bj = w_ref[:, j * LANE:(j + 1) * LANE], b_ref[:, j * LANE:(j + 1) * LANE]
            for r0 in range(0, tm, rs):
                glu = u_scr[r0:r0 + rs + H, j * LANE:(j + 1) * LANE] \
                    * _sigmoid(u_scr[r0:r0 + rs + H, (nl + j) * LANE:(nl + j + 1) * LANE])
                c_ref[r0:r0 + rs, j * LANE:(j + 1) * LANE] = _causal_conv(glu, H, wj, K) + bj
        c = c_ref[...]
        mu = jnp.mean(c, axis=-1, keepdims=True)
        xc = c - mu
        var = jnp.mean(xc * xc, axis=-1, keepdims=True)
        ln = xc * lax.rsqrt(var + EPS) * g_ref[...] + lb_ref[...]
        s_ref[...] = _silu(ln).astype(bf16)

    return pl.pallas_call(body, name=name, out_shape=(S((L, C), f32), S((L, C), bf16)), grid=(L // tm,),
                          in_specs=[_prev_halo_spec(tm, H, C2), _rows(tm, C2), _const((H, C)), _const((1, C)),
                                    _const((1, C)), _const((1, C))],
                          out_specs=(_rows(tm, C), _rows(tm, C)), scratch_shapes=[pltpu.VMEM((H + tm, C2), f32)],
                          compiler_params=_cp("parallel"))(u, u, dw_w, dw_b, ln_g, ln_b)


def _cf_ln_bwd(c, ln_g, ln_b, ds, *, name):
    L, C = c.shape
    tm = _pick(L, 512)

    def body(c_ref, g_ref, lb_ref, ds_ref, dc_ref, dg_ref, db_ref):
        i = pl.program_id(0)

        @pl.when(i == 0)
        def _():
            dg_ref[...] = jnp.zeros_like(dg_ref)
            db_ref[...] = jnp.zeros_like(db_ref)

        c = c_ref[...]
        mu = jnp.mean(c, axis=-1, keepdims=True)
        xc = c - mu
        r = lax.rsqrt(jnp.mean(xc * xc, axis=-1, keepdims=True) + EPS)
        xh = xc * r
        ln = xh * g_ref[...] + lb_ref[...]
        dln = ds_ref[...].astype(f32) * _dsilu(ln)
        dg_ref[...] += jnp.sum(dln * xh, axis=0, keepdims=True)
        db_ref[...] += jnp.sum(dln, axis=0, keepdims=True)
        dxh = dln * g_ref[...]
        dc_ref[...] = r * (dxh - jnp.mean(dxh, axis=-1, keepdims=True) - xh * jnp.mean(dxh * xh, axis=-1, keepdims=True))

    return pl.pallas_call(body, name=name, out_shape=(S((L, C), f32), S((1, C), f32), S((1, C), f32)), grid=(L // tm,),
                          in_specs=[_rows(tm, C), _const((1, C)), _const((1, C)), _rows(tm, C)],
                          out_specs=(_rows(tm, C), _const((1, C)), _const((1, C))),
                          compiler_params=_cp("arbitrary"))(c, ln_g, ln_b, ds)


def _cf_glu_bwd(u, dc, dw_w, *, name):
    L, C2 = u.shape
    C = C2 // 2
    K, H = CF_KERNEL, _halo_rows(CF_KERNEL)
    tm = _pick(L, 256)
    nb = L // tm

    rs = min(ROW_SUB, tm)
    nl = C // LANE

    fold = _fold_rows

    def body(uh_ref, u_ref, dc_ref, dch_ref, w_ref, du_ref, dw_ref, db_ref, dus_ref, u_scr, dc_scr):
        i = pl.program_id(0)

        @pl.when(i == 0)
        def _():
            dw_ref[...] = jnp.zeros_like(dw_ref)
            db_ref[...] = jnp.zeros_like(db_ref)
            dus_ref[...] = jnp.zeros_like(dus_ref)

        u_scr[0:H] = jnp.where(i > 0, uh_ref[...], 0.0)
        u_scr[H:] = u_ref[...]
        dc_scr[0:tm] = dc_ref[...]
        dc_scr[tm:] = jnp.where(i < nb - 1, dch_ref[...], 0.0)
        for j in range(nl):
            la, lg = slice(j * LANE, (j + 1) * LANE), slice((nl + j) * LANE, (nl + j + 1) * LANE)
            wj = w_ref[:, la]
            zero8 = jnp.zeros((SUBLANE, LANE), f32)
            dw_acc, db_acc, dua_acc, dug_acc = [zero8] * K, zero8, zero8, zero8
            for r0 in range(0, tm, rs):
                a_e = u_scr[r0:r0 + rs + H, la]
                sg = _sigmoid(u_scr[r0:r0 + rs + H, lg])
                dce = dc_scr[r0:r0 + rs + H, la]
                dglu = _anticausal_conv(dce, rs, wj, K)
                a_c, sg_c, dc_c = a_e[H:], sg[H:], dce[:rs]
                du_a = dglu * sg_c
                du_g = dglu * a_c * sg_c * (1.0 - sg_c)
                du_ref[r0:r0 + rs, la] = du_a.astype(bf16)
                du_ref[r0:r0 + rs, lg] = du_g.astype(bf16)
                dua_acc, dug_acc, db_acc = dua_acc + fold(du_a), dug_acc + fold(du_g), db_acc + fold(dc_c)
                views = _down_views(a_e * sg, H, rs, K)
                dw_acc = [dw_acc[k] + fold(dc_c * views[K - 1 - k]) for k in range(K)]
            dus_ref[:, la] += jnp.sum(dua_acc, axis=0, keepdims=True)
            dus_ref[:, lg] += jnp.sum(dug_acc, axis=0, keepdims=True)
            db_ref[:, la] += jnp.sum(db_acc, axis=0, keepdims=True)
            for k in range(K):
                dw_ref[k:k + 1, la] += jnp.sum(dw_acc[k], axis=0, keepdims=True)

    return pl.pallas_call(body, name=name,
                          out_shape=(S((L, C2), bf16), S((H, C), f32), S((1, C), f32), S((1, C2), f32)), grid=(nb,),
                          in_specs=[_prev_halo_spec(tm, H, C2), _rows(tm, C2), _rows(tm, C), _next_halo_spec(tm, H, C, L),
                                    _const((H, C))],
                          out_specs=(_rows(tm, C2), _const((H, C)), _const((1, C)), _const((1, C2))),
                          scratch_shapes=[pltpu.VMEM((H + tm, C2), f32), pltpu.VMEM((tm + H, C), f32)],
                          compiler_params=_cp("arbitrary"))(u, u, dc, dc, dw_w)


def _gated_norm_fwd(y, z, g, *, name):
    L, C = y.shape
    tm = _pick(L, 256)

    def body(y_ref, z_ref, g_ref, o_ref):
        o_ref[...] = _rms_val(y_ref[...] * _silu(z_ref[...]), g_ref[...]).astype(bf16)

    return pl.pallas_call(body, name=name, out_shape=S((L, C), bf16), grid=(L // tm,),
                          in_specs=[_rows(tm, C), _rows(tm, C), _const((1, C))], out_specs=_rows(tm, C),
                          compiler_params=_cp("parallel"))(y, z, g)


def _gated_norm_bwd(y, z, g, dyn, *, name):
    L, C = y.shape
    tm = _pick(L, 256)

    def body(y_ref, z_ref, g_ref, d_ref, dy_ref, dz_ref, dg_ref):
        i = pl.program_id(0)

        @pl.when(i == 0)
        def _():
            dg_ref[...] = jnp.zeros_like(dg_ref)

        y, z = y_ref[...], z_ref[...]
        sz = _silu(z)
        du, dg = _rms_bwd_val(y * sz, g_ref[...], d_ref[...].astype(f32))
        dg_ref[...] += dg
        dy_ref[...] = du * sz
        dz_ref[...] = (du * y * _dsilu(z)).astype(bf16)

    return pl.pallas_call(body, name=name, out_shape=(S((L, C), f32), S((L, C), bf16), S((1, C), f32)), grid=(L // tm,),
                          in_specs=[_rows(tm, C), _rows(tm, C), _const((1, C)), _rows(tm, C)],
                          out_specs=(_rows(tm, C), _rows(tm, C), _const((1, C))),
                          compiler_params=_cp("arbitrary"))(y, z, g, dyn)


_XA_SCALE = XA_HEAD_DIM ** -0.5


def _attn_fwd(q, kv, *, name):
    L, C = q.shape
    tm = _pick(L, 512)
    Dh = XA_HEAD_DIM

    def body(q_ref, kv_ref, o_ref):
        for h in range(XA_HEADS):
            qh = q_ref[:, h * Dh:(h + 1) * Dh]
            kh = kv_ref[:, h * Dh:(h + 1) * Dh]
            vh = kv_ref[:, C + h * Dh:C + (h + 1) * Dh]
            s = lax.dot_general(qh, kh, _DN["nt"], preferred_element_type=f32) * _XA_SCALE
            e = jnp.exp(s - jnp.max(s, axis=-1, keepdims=True))
            p = e / jnp.sum(e, axis=-1, keepdims=True)
            o_ref[:, h * Dh:(h + 1) * Dh] = jnp.dot(p.astype(bf16), vh, preferred_element_type=f32).astype(bf16)

    return pl.pallas_call(body, name=name, out_shape=S((L, C), bf16), grid=(L // tm,),
                          in_specs=[_rows(tm, C), _const((N_MEM, 2 * C))], out_specs=_rows(tm, C),
                          compiler_params=_cp("parallel"))(q, kv)


def _attn_bwd(q, kv, do, *, name):
    L, C = q.shape
    tm = _pick(L, 512)
    Dh = XA_HEAD_DIM

    def body(q_ref, kv_ref, do_ref, dq_ref, dkv_ref):
        i = pl.program_id(0)

        @pl.when(i == 0)
        def _():
            dkv_ref[...] = jnp.zeros_like(dkv_ref)

        for h in range(XA_HEADS):
            qh = q_ref[:, h * Dh:(h + 1) * Dh]
            kh = kv_ref[:, h * Dh:(h + 1) * Dh]
            vh = kv_ref[:, C + h * Dh:C + (h + 1) * Dh]
            doh = do_ref[:, h * Dh:(h + 1) * Dh]
            s = lax.dot_general(qh, kh, _DN["nt"], preferred_element_type=f32) * _XA_SCALE
            e = jnp.exp(s - jnp.max(s, axis=-1, keepdims=True))
            p = e / jnp.sum(e, axis=-1, keepdims=True)
            pb = p.astype(bf16)
            dkv_ref[:, C + h * Dh:C + (h + 1) * Dh] += lax.dot_general(pb, doh, _DN["tn"], preferred_element_type=f32)
            dp = lax.dot_general(doh, vh, _DN["nt"], preferred_element_type=f32)
            ds = (p * (dp - jnp.sum(dp * p, axis=-1, keepdims=True)) * _XA_SCALE).astype(bf16)
            dq_ref[:, h * Dh:(h + 1) * Dh] = jnp.dot(ds, kh, preferred_element_type=f32).astype(bf16)
            dkv_ref[:, h * Dh:(h + 1) * Dh] += lax.dot_general(ds, qh, _DN["tn"], preferred_element_type=f32)

    return pl.pallas_call(body, name=name, out_shape=(S((L, C), bf16), S((N_MEM, 2 * C), f32)), grid=(L // tm,),
                          in_specs=[_rows(tm, C), _const((N_MEM, 2 * C)), _rows(tm, C)],
                          out_specs=(_rows(tm, C), _const((N_MEM, 2 * C))),
                          compiler_params=_cp("arbitrary"))(q, kv, do)


Q = CHUNK
PAIRS = HEADS_PER_GROUP // 2


def _split(x, pieces):
    out = []
    for _ in range(pieces - 1):
        p = x.astype(bf16)
        out.append(p)
        x = x - p.astype(f32)
    return out + [x.astype(bf16)]


def _sel_right(x, sel, mode="nn", pieces=2):
    return sum(lax.dot_general(p, sel, _DN[mode], preferred_element_type=f32) for p in _split(x, pieces))


def _sel_left(sel, x, pieces=3):
    return sum(lax.dot_general(sel, p, _DN["nn"], preferred_element_type=f32) for p in _split(x, pieces))


def _ssd_common(dt_ref, hp_ref):
    dt_pre = dt_ref[...] + hp_ref[0:1, :]
    dt = _softplus(dt_pre)
    A = -jnp.exp(hp_ref[1:2, :])
    a = dt * A
    row = lax.broadcasted_iota(jnp.int32, (Q, Q), 0)
    col = lax.broadcasted_iota(jnp.int32, (Q, Q), 1)
    tri = row >= col
    cs = _sel_left(tri.astype(bf16), a)
    T = cs[Q - 1:Q, :]
    return dict(dt_pre=dt_pre, dt=dt, A=A, cs=cs, csT=cs.T, T=T, ecs=jnp.exp(cs), eend=jnp.exp(T - cs), eT=jnp.exp(T),
                tri=tri, row=row, col=col)


def _pair_expand(v, hA, lo):
    return jnp.where(lo, v[:, hA:hA + 1], v[:, hA + 1:hA + 2])


def _decay(cm, h):
    seg = cm["cs"][:, h:h + 1] - cm["csT"][h:h + 1, :]
    return jnp.where(cm["tri"], jnp.exp(jnp.where(cm["tri"], seg, 0.0)), 0.0)


def _decay_t(cm, h):
    keep = cm["row"] <= cm["col"]
    seg = cm["csT"][h:h + 1, :] - cm["cs"][:, h:h + 1]
    return jnp.where(keep, jnp.exp(jnp.where(keep, seg, 0.0)), 0.0)


ALL_PAIRS = N_SSM_HEADS // 2
GN = N_GROUPS * D_STATE


def _ssd_fwd(act, dtp, hp, *, name):
    L = act.shape[0]
    nc = L // Q

    def body(xs_ref, b_ref, c_ref, dt_ref, hp_ref, y_ref, hs_ref, h_scr):
        c = pl.program_id(0)

        @pl.when(c == 0)
        def _():
            h_scr[...] = jnp.zeros_like(h_scr)

        cm = _ssd_common(dt_ref, hp_ref)
        lo = lax.broadcasted_iota(jnp.int32, (Q, LANE), 1) < HEAD_DIM
        top = lax.broadcasted_iota(jnp.int32, (LANE, LANE), 0) < HEAD_DIM
        Drow = hp_ref[2:3, :]
        for g in range(N_GROUPS):
            Bb = b_ref[:, g * D_STATE:(g + 1) * D_STATE].astype(bf16)
            Cb = c_ref[:, g * D_STATE:(g + 1) * D_STATE].astype(bf16)
            CB = lax.dot_general(Cb, Bb, _DN["nt"], preferred_element_type=f32)
            for jj in range(PAIRS):
                p = g * PAIRS + jj
                hA, hB = 2 * p, 2 * p + 1
                dtx, ecsx, eendx = (_pair_expand(cm[k], hA, lo) for k in ("dt", "ecs", "eend"))
                xs_p = xs_ref[:, p * LANE:(p + 1) * LANE]
                Xd = xs_p * dtx
                Y = None
                for h, Xm in ((hA, jnp.where(lo, Xd, 0.0)), (hB, jnp.where(lo, 0.0, Xd))):
                    W = (CB * _decay(cm, h)).astype(bf16)
                    t = jnp.dot(W, Xm.astype(bf16), preferred_element_type=f32)
                    Y = t if Y is None else Y + t
                Hp = h_scr[p]
                hs_ref[0, p] = Hp
                Yoff = lax.dot_general(Cb, Hp.astype(bf16), _DN["nt"], preferred_element_type=f32) * ecsx
                Dx = jnp.where(lo[0:1, :], Drow[:, hA:hA + 1], Drow[:, hB:hB + 1])
                y_ref[:, p * LANE:(p + 1) * LANE] = Y + Yoff + xs_p * Dx
                Snew = lax.dot_general((Xd * eendx).astype(bf16), Bb, _DN["tn"], preferred_element_type=f32)
                eTx = jnp.where(top, cm["eT"][:, hA:hA + 1], cm["eT"][:, hB:hB + 1])
                h_scr[p] = Hp * eTx + Snew

    return pl.pallas_call(
        body, name=name, out_shape=(S((L, D_INNER), f32), S((nc, ALL_PAIRS, LANE, D_STATE), f32)),
        grid=(nc,),
        in_specs=[pl.BlockSpec((Q, D_INNER), lambda c: (c, 0)),
                  pl.BlockSpec((Q, GN), lambda c: (c, D_INNER // GN)),
                  pl.BlockSpec((Q, GN), lambda c: (c, D_INNER // GN + 1)),
                  pl.BlockSpec((Q, LANE), lambda c: (c, 0)),
                  pl.BlockSpec((SUBLANE, LANE), lambda c: (0, 0))],
        out_specs=(pl.BlockSpec((Q, D_INNER), lambda c: (c, 0)),
                   pl.BlockSpec((1, ALL_PAIRS, LANE, D_STATE), lambda c: (c, 0, 0, 0))),
        scratch_shapes=[pltpu.VMEM((ALL_PAIRS, LANE, D_STATE), f32)],
        compiler_params=_cp("arbitrary"))(act, act, act, dtp, hp)


def _ssd_bwd(act, dtp, hp, dy, hs, *, name):
    L = act.shape[0]
    nc = L // Q

    def body(xs_ref, b_ref, c_ref, dt_ref, hp_ref, dy_ref, hs_ref, dxs_ref, db_ref, dc_ref, ddt_ref, dhp_ref, dh_scr):
        c = pl.program_id(0)

        @pl.when(c == 0)
        def _():
            dh_scr[...] = jnp.zeros_like(dh_scr)
            dhp_ref[...] = jnp.zeros_like(dhp_ref)

        cm = _ssd_common(dt_ref, hp_ref)
        lane = lax.broadcasted_iota(jnp.int32, (Q, LANE), 1)
        sub = lax.broadcasted_iota(jnp.int32, (LANE, LANE), 0)
        lo = lane < HEAD_DIM
        top = sub < HEAD_DIM
        Drow = hp_ref[2:3, :]
        zero = jnp.zeros((Q, LANE), f32)
        dcs, dcsT, ddtx = zero, zero, zero
        dD_row = jnp.zeros((1, LANE), f32)
        dT_row = jnp.zeros((1, LANE), f32)
        for g in range(N_GROUPS):
            Bb = b_ref[:, g * D_STATE:(g + 1) * D_STATE].astype(bf16)
            Cb = c_ref[:, g * D_STATE:(g + 1) * D_STATE].astype(bf16)
            CB = lax.dot_general(Cb, Bb, _DN["nt"], preferred_element_type=f32)
            CBT = lax.dot_general(Bb, Cb, _DN["nt"], preferred_element_type=f32)
            dC, dB, dCB = zero, zero, jnp.zeros((Q, Q), f32)
            for jj in range(PAIRS):
                p = g * PAIRS + jj
                hA, hB = 2 * p, 2 * p + 1
                Pj = (lane == jnp.where(top, hA, hB)).astype(bf16)
                dtx, ecsx, eendx = (_pair_expand(cm[k], hA, lo) for k in ("dt", "ecs", "eend"))
                xs_p = xs_ref[:, p * LANE:(p + 1) * LANE]
                dY_p = dy_ref[:, p * LANE:(p + 1) * LANE]
                Xd = xs_p * dtx
                Xdb = Xd.astype(bf16)
                Hp, dHn = hs_ref[0, p], dh_scr[p]
                Hb, dHb = Hp.astype(bf16), dHn.astype(bf16)
                EdYb = (dY_p * ecsx).astype(bf16)
                YoffN = lax.dot_general(Cb, Hb, _DN["nt"], preferred_element_type=f32)
                dC = dC + jnp.dot(EdYb, Hb, preferred_element_type=f32)
                dH_off = lax.dot_general(EdYb, Cb, _DN["tn"], preferred_element_type=f32)
                R = lax.dot_general(Bb, dHb, _DN["nt"], preferred_element_type=f32)
                Xe = Xd * eendx
                dB = dB + jnp.dot(Xe.astype(bf16), dHb, preferred_element_type=f32)
                dXd = R * eendx
                RXe = R * Xe
                dcs = dcs + _sel_right(dY_p * YoffN * ecsx - RXe, Pj)
                HH = dHn * Hp
                hh = [jnp.sum(jnp.sum(HH[r0:r0 + HEAD_DIM], axis=0, keepdims=True), axis=1, keepdims=True)
                      for r0 in (0, HEAD_DIM)]
                rxe_cols = jnp.broadcast_to(jnp.sum(RXe, axis=0, keepdims=True), (SUBLANE, LANE))
                dT_row = dT_row + _sel_right(rxe_cols, Pj, pieces=3)[0:1] \
                    + (jnp.where(lane[0:1] == hA, hh[0], 0.0) + jnp.where(lane[0:1] == hB, hh[1], 0.0)) * cm["eT"]
                for h, keep in ((hA, lo), (hB, jnp.logical_not(lo))):
                    M = _decay(cm, h)
                    Wf = CB * M
                    dYm = jnp.where(keep, dY_p, 0.0).astype(bf16)
                    dW = lax.dot_general(dYm, Xdb, _DN["nt"], preferred_element_type=f32)
                    WT = (CBT * _decay_t(cm, h)).astype(bf16)
                    dXd = dXd + jnp.dot(WT, dYm, preferred_element_type=f32)
                    Z = dW * Wf
                    dcs = dcs + _sel_right(Z, (lane == h).astype(bf16))
                    dcsT = dcsT + jnp.where(sub == h, jnp.sum(Z, axis=0, keepdims=True), 0.0)
                    dCB = dCB + dW * M
                Dx = jnp.where(lo[0:1, :], Drow[:, hA:hA + 1], Drow[:, hB:hB + 1])
                dxs_ref[:, p * LANE:(p + 1) * LANE] = dXd * dtx + dY_p * Dx
                ddtx = ddtx + _sel_right(dXd * xs_p, Pj)
                dD_cols = jnp.broadcast_to(jnp.sum(dY_p * xs_p, axis=0, keepdims=True), (SUBLANE, LANE))
                dD_row = dD_row + _sel_right(dD_cols, Pj, pieces=3)[0:1]
                eTx = jnp.where(top, cm["eT"][:, hA:hA + 1], cm["eT"][:, hB:hB + 1])
                dh_scr[p] = dHn * eTx + dH_off
            dCBb = dCB.astype(bf16)
            dc_ref[:, g * D_STATE:(g + 1) * D_STATE] = dC + jnp.dot(dCBb, Bb, preferred_element_type=f32)
            db_ref[:, g * D_STATE:(g + 1) * D_STATE] = dB + lax.dot_general(dCBb, Cb, _DN["tn"], preferred_element_type=f32)
        dcs = dcs - dcsT.T + jnp.where(lax.broadcasted_iota(jnp.int32, (Q, LANE), 0) == Q - 1, dT_row, 0.0)
        da = _sel_left((cm["row"] <= cm["col"]).astype(bf16), dcs)
        ddt_pre = (da * cm["A"] + ddtx) * _sigmoid(cm["dt_pre"])
        ddt_ref[...] = ddt_pre
        r8 = lax.broadcasted_iota(jnp.int32, (SUBLANE, LANE), 0)
        dhp_ref[...] += jnp.where(r8 == 0, jnp.sum(ddt_pre, axis=0, keepdims=True),
                                  jnp.where(r8 == 1, jnp.sum(da * cm["dt"], axis=0, keepdims=True) * cm["A"],
                                            jnp.where(r8 == 2, dD_row, 0.0)))

    rev = lambda c: nc - 1 - c
    return pl.pallas_call(
        body, name=name,
        out_shape=(S((L, D_INNER), f32), S((L, GN), f32), S((L, GN), f32), S((L, LANE), f32), S((SUBLANE, LANE), f32)),
        grid=(nc,),
        in_specs=[pl.BlockSpec((Q, D_INNER), lambda c: (rev(c), 0)),
                  pl.BlockSpec((Q, GN), lambda c: (rev(c), D_INNER // GN)),
                  pl.BlockSpec((Q, GN), lambda c: (rev(c), D_INNER // GN + 1)),
                  pl.BlockSpec((Q, LANE), lambda c: (rev(c), 0)),
                  pl.BlockSpec((SUBLANE, LANE), lambda c: (0, 0)),
                  pl.BlockSpec((Q, D_INNER), lambda c: (rev(c), 0)),
                  pl.BlockSpec((1, ALL_PAIRS, LANE, D_STATE), lambda c: (rev(c), 0, 0, 0))],
        out_specs=(pl.BlockSpec((Q, D_INNER), lambda c: (rev(c), 0)),
                   pl.BlockSpec((Q, GN), lambda c: (rev(c), 0)),
                   pl.BlockSpec((Q, GN), lambda c: (rev(c), 0)),
                   pl.BlockSpec((Q, LANE), lambda c: (rev(c), 0)),
                   pl.BlockSpec((SUBLANE, LANE), lambda c: (0, 0))),
        scratch_shapes=[pltpu.VMEM((ALL_PAIRS, LANE, D_STATE), f32)],
        compiler_params=_cp("arbitrary"))(act, act, act, dtp, hp, dy, hs)


def _group_pad_cols(w):
    return jnp.pad(w, [(0, 0)] * (w.ndim - 1) + [(0, LANE - N_SSM_HEADS)])


def _group_unpad_cols(w):
    return w[..., :N_SSM_HEADS]


def _row(v):
    return v.reshape(1, -1)


ROW_SHARDED = ('ssm_out_w', 'cf_pw2_w', 'xa_q_w', 'xa_o_w', 'ffn_out_w')
COL_SHARDED = ('cf_pw1_w', 'xa_kv_w', 'ffn_in_w')


MIXER_WEIGHTS = ('ssm_in_w', 'ssm_out_w', 'cf_pw1_w', 'cf_pw2_w')


def _layer_matmul_weights(i, part):
    if part == "mixer":
        return ('ssm_in_w', 'ssm_out_w') if i % 2 == 0 else ('cf_pw1_w', 'cf_pw2_w')
    return ('xa_q_w', 'xa_kv_w', 'xa_o_w', 'ffn_in_w', 'ffn_out_w')


def _device_step(x, mem, target, W, layer_weights, layer_grads, start_after=()):
    ng = W['norm_g']
    lw = []
    for i in range(DEPTH):
        j = i // 2
        p = {}
        if i % 2 == 0:
            p['cw'] = _pad_taps(W['ssm_conv_w'][j], SSM_CONV)
            p['cb'] = _row(W['ssm_conv_b'][j])
            hp = jnp.stack([_group_pad_cols(W['ssm_dt_bias'][j]), _group_pad_cols(W['ssm_A_log'][j]),
                            _group_pad_cols(W['ssm_D'][j])])
            p['hp'] = jnp.pad(hp, ((0, SUBLANE - 3), (0, 0)))
            p['sng'] = _row(W['ssm_norm_g'][j])
        else:
            p['pw1b'] = _row(W['cf_pw1_b'][j])
            p['dww'], p['dwb'] = _pad_taps(W['cf_dw_w'][j], CF_KERNEL), _row(W['cf_dw_b'][j])
            p['lng'], p['lnb'] = _row(W['cf_ln_g'][j]), _row(W['cf_ln_b'][j])
            p['pw2b'] = _row(W['cf_pw2_b'][j])
        p['memg'] = _row(W['xa_mem_g'][i])
        p['fcw'], p['fcb'] = _pad_taps(W['ffn_conv_w'][i], FFN_CONV), _row(W['ffn_conv_b'][i])
        p['g'] = [_row(ng[i, s]) for s in range(6)]
        lw.append(p)

    def wmm(a, wl, wname, mode, **kw):
        return _mm(a, wl[wname], mode, b_shards=wname in COL_SHARDED, **kw)

    saved = []
    X = x
    h = _rmsnorm_fwd(X, lw[0]['g'][0], name="norm_in", after=start_after)
    for i in range(DEPTH):
        p, sv = lw[i], {}
        wl = dict(layer_weights(i, "mixer", X))
        sv['X0'], sv['h'], sv['wl'] = X, h, wl
        if i % 2 == 0:
            win = jnp.concatenate([wl['ssm_in_w'][s] for s in range(N_CHIPS)], axis=1)
            wl['wz'], wl['wx'] = win[:, :D_INNER], win[:, D_INNER:D_INNER + CONV_DIM]
            wl['wdt'] = _group_pad_cols(win[:, D_INNER + CONV_DIM:])
            z = _mm(h, wl['wz'], "nn", name="ssm_z")
            xbc = _mm(h, wl['wx'], "nn", name="ssm_xbc")
            dtp = _mm(h, wl['wdt'], "nn", name="ssm_dt")
            act = _conv_act_fwd(xbc, p['cw'], p['cb'], K=SSM_CONV, act="silu", name="ssm_conv_fwd", out_dtype=f32)
            y, hs = _ssd_fwd(act, dtp, p['hp'], name="ssd_fwd")
            yn = _gated_norm_fwd(y, z, p['sng'], name="ssm_gnorm_fwd")
            mix = wmm(yn, wl, 'ssm_out_w', "nn", name="ssm_out")
            sv.update(z=z, xbc=xbc, dtp=dtp, act=act, y=y, hs=hs, yn=yn)
        else:
            u = wmm(h, wl, 'cf_pw1_w', "nn", name="cf_pw1", bias=p['pw1b'])
            c, s = _cf_fwd(u, p['dww'], p['dwb'], p['lng'], p['lnb'], name="cf_conv_fwd")
            mix = wmm(s, wl, 'cf_pw2_w', "nn", name="cf_pw2", bias=p['pw2b'])
            sv.update(u=u, c=c, s=s)
        wl.update(layer_weights(i, "rest", mix))
        X1, h2 = _resid_norm_fwd(X, mix, p['g'][1], p['g'][2], name="resid_norm_a")
        q = wmm(h2, wl, 'xa_q_w', "nn", name="xa_q", out_dtype=bf16)
        m = _rmsnorm_fwd(mem, p['memg'], name="xa_mem_norm")
        kv = wmm(m, wl, 'xa_kv_w', "nn", name="xa_kv", out_dtype=bf16)
        o = _attn_fwd(q, kv, name="xa_attn_fwd")
        a = wmm(o, wl, 'xa_o_w', "nn", name="xa_o")
        X2, h3 = _resid_norm_fwd(X1, a, p['g'][3], p['g'][4], name="resid_norm_b")
        u0 = wmm(h3, wl, 'ffn_in_w', "nn", name="ffn_in")
        fact = _conv_act_fwd(u0, p['fcw'], p['fcb'], K=FFN_CONV, act="swiglu", name="ffn_conv_fwd", out_dtype=bf16)
        f = wmm(fact, wl, 'ffn_out_w', "nn", name="ffn_out")
        g_next = lw[i + 1]['g'][0] if i + 1 < DEPTH else None
        X3, hn = _resid_norm_fwd(X2, f, p['g'][5], g_next, name="resid_norm_c" if g_next is not None else "resid_norm_last")
        sv.update(mix=mix, X1=X1, h2=h2, q=q, m=m, kv=kv, o=o, a=a, X2=X2, h3=h3, u0=u0, fact=fact, f=f)
        saved.append(sv)
        X, h = X3, hn

    sse, G = _loss_fwd_bwd(X, target, name="loss")

    small = [n for n in WEIGHT_NAMES if n not in MATMUL_WEIGHTS]
    gr = {n: [None] * W[n].shape[0] for n in small}

    def dwmm(gl, a, d, wname, *, name):
        if wname in COL_SHARDED:
            gl[wname] = _mm(a, d, "tn", name=name, out_dtype=bf16, out_shards=True)
        else:
            g = _mm(a, d, "tn", name=name, out_dtype=bf16)
            gl[wname] = g.reshape(N_CHIPS, g.shape[0] // N_CHIPS, g.shape[1])

    dng = [[None] * 6 for _ in range(DEPTH)]
    df = None
    for i in reversed(range(DEPTH)):
        p, sv, j = lw[i], saved[i], i // 2
        wl, gl = sv['wl'], {}
        if df is None:
            df, dng[i][5], _ = _norm_bwd(sv['f'], p['g'][5], G, name="nb_f", out_dtype=bf16)
        dwmm(gl, sv['fact'], df, 'ffn_out_w', name="ffn_out_dw")
        dfact = wmm(df, wl, 'ffn_out_w', "nt", name="ffn_out_dx")
        du0, dcw, dcb = _conv_act_bwd(sv['u0'], [dfact], p['fcw'], p['fcb'], K=FFN_CONV, act="swiglu", name="ffn_conv_bwd")
        gr['ffn_conv_w'][i], gr['ffn_conv_b'][i] = dcw[:FFN_CONV], dcb[0]
        dwmm(gl, sv['h3'], du0, 'ffn_in_w', name="ffn_in_dw")
        dh3 = wmm(du0, wl, 'ffn_in_w', "nt", name="ffn_in_dx")
        G, dng[i][4], da, dng[i][3], _ = _norm_bwd_chain(sv['X2'], p['g'][4], dh3, G, sv['a'], p['g'][3], name="nb_x2_a")
        dwmm(gl, sv['o'], da, 'xa_o_w', name="xa_o_dw")
        do = wmm(da, wl, 'xa_o_w', "nt", name="xa_o_dx", out_dtype=bf16)
        dq, dkv = _attn_bwd(sv['q'], sv['kv'], do, name="xa_attn_bwd")
        dwmm(gl, sv['h2'], dq, 'xa_q_w', name="xa_q_dw")
        dh2 = wmm(dq, wl, 'xa_q_w', "nt", name="xa_q_dx")
        dwmm(gl, sv['m'], dkv, 'xa_kv_w', name="xa_kv_dw")
        dm = wmm(dkv, wl, 'xa_kv_w', "nt", name="xa_kv_dx")
        _, dmg, _ = _norm_bwd(mem, p['memg'], dm, name="nb_mem")
        gr['xa_mem_g'][i] = dmg[0]
        behind = tuple(layer_grads(i, "rest", gl))
        G, dng[i][2], dmix, dng[i][1], dmix_sum = _norm_bwd_chain(sv['X1'], p['g'][2], dh2, G, sv['mix'], p['g'][1],
                                                                  name="nb_x1_mix", after=behind)
        if i % 2 == 0:
            dwmm(gl, sv['yn'], dmix, 'ssm_out_w', name="ssm_out_dw")
            dyn = wmm(dmix, wl, 'ssm_out_w', "nt", name="ssm_out_dx")
            dy, dz, dsng = _gated_norm_bwd(sv['y'], sv['z'], p['sng'], dyn, name="ssm_gnorm_bwd")
            gr['ssm_norm_g'][j] = dsng[0]
            dxs, dB, dC, ddtp, dhp = _ssd_bwd(sv['act'], sv['dtp'], p['hp'], dy, sv['hs'], name="ssd_bwd")
            gr['ssm_dt_bias'][j], gr['ssm_A_log'][j], gr['ssm_D'][j] = (_group_unpad_cols(dhp[r]) for r in range(3))
            dxbc, dcw, dcb = _conv_act_bwd(sv['xbc'], [dxs, dB, dC], p['cw'], p['cb'], K=SSM_CONV, act="silu",
                                           name="ssm_conv_bwd")
            gr['ssm_conv_w'][j], gr['ssm_conv_b'][j] = dcw[:SSM_CONV], dcb[0]
            hh = sv['h']
            dwz = _mm(hh, dz, "tn", name="ssm_z_dw", out_dtype=bf16)
            dwx = _mm(hh, dxbc, "tn", name="ssm_xbc_dw", out_dtype=bf16)
            dwdt = _mm(hh, ddtp, "tn", name="ssm_dt_dw", out_dtype=bf16)
            din = jnp.concatenate([dwz, dwx, _group_unpad_cols(dwdt)], axis=1)
            gl['ssm_in_w'] = jnp.stack(jnp.split(din, N_CHIPS, axis=1))
            behind = tuple(layer_grads(i, "mixer", gl))
            dh = _mm(dz, wl['wz'], "nt", name="ssm_z_dx", after=behind)
            dh = _mm(dxbc, wl['wx'], "nt", name="ssm_xbc_dx", add=dh)
            dh = _mm(ddtp, wl['wdt'], "nt", name="ssm_dt_dx", add=dh)
        else:
            dwmm(gl, sv['s'], dmix, 'cf_pw2_w', name="cf_pw2_dw")
            gr['cf_pw2_b'][j] = dmix_sum[0]
            ds = wmm(dmix, wl, 'cf_pw2_w', "nt", name="cf_pw2_dx")
            dc, dlg, dlb = _cf_ln_bwd(sv['c'], p['lng'], p['lnb'], ds, name="cf_ln_bwd")
            gr['cf_ln_g'][j], gr['cf_ln_b'][j] = dlg[0], dlb[0]
            du, ddw, ddb, dus = _cf_glu_bwd(sv['u'], dc, p['dww'], name="cf_glu_bwd")
            gr['cf_dw_w'][j], gr['cf_dw_b'][j], gr['cf_pw1_b'][j] = ddw[:CF_KERNEL], ddb[0], dus[0]
            dwmm(gl, sv['h'], du, 'cf_pw1_w', name="cf_pw1_dw")
            behind = tuple(layer_grads(i, "mixer", gl))
            dh = wmm(du, wl, 'cf_pw1_w', "nt", name="cf_pw1_dx", after=behind)
        if i > 0:
            G, dng[i][0], df, dng[i - 1][5], _ = _norm_bwd_chain(sv['X0'], p['g'][0], dh, G, saved[i - 1]['f'],
                                                                 lw[i - 1]['g'][5], name="nb_x0_f")
        else:
            G, dng[i][0], _ = _norm_bwd(sv['X0'], p['g'][0], dh, name="nb_x0", add=G)
    gr['norm_g'] = [jnp.concatenate(dng[i], axis=0) for i in range(DEPTH)]
    gsmall = {n: jnp.stack(gr[n]) for n in small}
    return sse, G, gsmall


MESH = pl.DeviceIdType.MESH
HBM_SPEC = pl.BlockSpec(memory_space=pltpu.HBM)


def _chip_peers(x, y):
    return [(1 - x, y), (x, 1 - y), (1 - x, 1 - y)]


def _all_gather_chips(buf, *, name):
    R, C = buf.shape

    def body(in_ref, out_ref, send_sems, recv_sems, local_sem):
        x, y, c = lax.axis_index("x"), lax.axis_index("y"), lax.axis_index("c")
        me = 2 * x + y
        mine = pltpu.make_async_copy(in_ref, out_ref.at[me], local_sem)
        mine.start()
        peers = _chip_peers(x, y)
        sends = []
        for k, (px, py) in enumerate(peers):
            cp = pltpu.make_async_remote_copy(src_ref=in_ref, dst_ref=out_ref.at[me], send_sem=send_sems.at[k],
                                              recv_sem=recv_sems.at[k], device_id=(px, py, c), device_id_type=MESH)
            cp.start()
            sends.append(cp)
        for k, (px, py) in enumerate(peers):
            pltpu.make_async_remote_copy(src_ref=in_ref, dst_ref=out_ref.at[2 * px + py], send_sem=send_sems.at[k],
                                         recv_sem=recv_sems.at[k], device_id=(px, py, c), device_id_type=MESH).wait_recv()
        for cp in sends:
            cp.wait_send()
        mine.wait()

    return pl.pallas_call(body, name=name, out_shape=S((N_CHIPS, R, C), buf.dtype), in_specs=[HBM_SPEC], out_specs=HBM_SPEC,
                          scratch_shapes=[pltpu.SemaphoreType.DMA((3,)), pltpu.SemaphoreType.DMA((3,)),
                                          pltpu.SemaphoreType.DMA(())])(buf)


def _remote(src, dst, send_sem, recv_sem, device):
    return pltpu.make_async_remote_copy(src_ref=src, dst_ref=dst, send_sem=send_sem, recv_sem=recv_sem,
                                        device_id=device, device_id_type=MESH)


def _gather_matmul_weights(shards, *, name):
    n = len(shards)

    def body(*refs):
        ins, outs = refs[:n], refs[n:2 * n]
        send, recv, fsend, frecv, lsem = refs[2 * n:]
        x, y, c = lax.axis_index("x"), lax.axis_index("y"), lax.axis_index("c")
        me, sib = 2 * x + y, (x, y, 1 - c)
        peers = _chip_peers(x, y)
        started, local = [], []
        for w in range(n):
            cp = pltpu.make_async_copy(ins[w], outs[w].at[:, me], lsem.at[w])
            cp.start()
            local.append(cp)
            for k, (px, py) in enumerate(peers):
                cp = _remote(ins[w].at[:, c], outs[w].at[:, me, c], send.at[w, k], recv.at[w, k], (px, py, c))
                cp.start()
                started.append(cp)
        for w in range(n):
            for k, (px, py) in enumerate(peers):
                landed = outs[w].at[:, 2 * px + py, c]
                _remote(ins[w].at[:, c], landed, send.at[w, k], recv.at[w, k], (px, py, c)).wait_recv()
                cp = _remote(landed, landed, fsend.at[w, k], frecv.at[w, k], sib)
                cp.start()
                started.append(cp)
        for w in range(n):
            for k, (px, py) in enumerate(peers):
                _remote(ins[w].at[:, c], outs[w].at[:, 2 * px + py, 1 - c], fsend.at[w, k], frecv.at[w, k], sib).wait_recv()
        for cp in started:
            cp.wait_send()
        for cp in local:
            cp.wait()

    out_shape = tuple(S((s.shape[0], N_CHIPS) + s.shape[1:], s.dtype) for s in shards)
    sems = [pltpu.SemaphoreType.DMA((n, 3)) for _ in range(4)] + [pltpu.SemaphoreType.DMA((n,))]
    return pl.pallas_call(body, name=name, out_shape=out_shape, in_specs=[HBM_SPEC] * n, out_specs=(HBM_SPEC,) * n,
                          scratch_shapes=sems)(*shards)


SEM_SPEC = pl.BlockSpec(memory_space=pltpu.SEMAPHORE)
VMEM_SPEC = pl.BlockSpec(memory_space=pltpu.VMEM)


def _in_hbm(a):
    return pltpu.with_memory_space_constraint(a, pltpu.HBM)


def _chip_targets(x, y):
    return [(x, y), (1 - x, y), (x, 1 - y), (1 - x, 1 - y)]


def _spread_start(srcs, scatter, *, name, after=()):
    n = len(srcs)
    lands = [lax.empty((N_CHIPS,) + (s.shape[1:] if scatter else s.shape), s.dtype) for s in srcs]

    def body(*refs):
        src, land = refs[:n], refs[n:2 * n]
        send, recv, token = refs[2 * n + len(after)], refs[2 * n + len(after) + 1], refs[-1]
        x, y, c = lax.axis_index("x"), lax.axis_index("y"), lax.axis_index("c")
        me = 2 * x + y
        for w in range(n):
            for k, (px, py) in enumerate(_chip_targets(x, y)):
                block = src[w].at[2 * px + py] if scatter else src[w]
                _remote(block, land[w].at[me], send.at[N_CHIPS * w + k], recv.at[N_CHIPS * w + k], (px, py, c)).start()
        token[...] = jnp.zeros_like(token)

    thru = tuple(pltpu.HBM(a.shape, a.dtype) for a in list(srcs) + lands)
    sems = (pltpu.SemaphoreType.DMA((N_CHIPS * n,)), pltpu.SemaphoreType.DMA((N_CHIPS * n,)))
    out = pl.pallas_call(
        body, name=name, out_shape=sems + thru + (S((SUBLANE, LANE), f32),),
        in_specs=[HBM_SPEC] * (2 * n) + [ANY_SPEC] * len(after),
        out_specs=(SEM_SPEC, SEM_SPEC) + (HBM_SPEC,) * (2 * n) + (VMEM_SPEC,),
        input_output_aliases={i: 2 + i for i in range(2 * n)},
        compiler_params=pltpu.CompilerParams(has_side_effects=pltpu.SideEffectType.DATAFLOW_SIDE_EFFECTING),
    )(*[_in_hbm(a) for a in list(srcs) + lands], *after)
    return out[0], out[1], out[2:2 + n], out[2 + n:2 + 2 * n], out[-1]


def _spread_wait(send, recv, srcs, lands, after, scatter, *, name):
    n = len(srcs)

    def body(*refs):
        src, land, send, recv = refs[:n], refs[n:2 * n], refs[2 * n], refs[2 * n + 1]
        x, y, c = lax.axis_index("x"), lax.axis_index("y"), lax.axis_index("c")
        me = 2 * x + y
        for w in range(n):
            for k, (px, py) in enumerate(_chip_targets(x, y)):
                block = src[w].at[me] if scatter else src[w]
                cp = _remote(block, land[w].at[2 * px + py], send.at[N_CHIPS * w + k], recv.at[N_CHIPS * w + k], (px, py, c))
                cp.wait_send()
                cp.wait_recv()

    thru = tuple(pltpu.HBM(a.shape, a.dtype) for a in list(srcs) + list(lands))
    out = pl.pallas_call(
        body, name=name, out_shape=thru,
        in_specs=[HBM_SPEC] * (2 * n) + [SEM_SPEC, SEM_SPEC] + [ANY_SPEC] * len(after), out_specs=(HBM_SPEC,) * (2 * n),
        input_output_aliases={i: i for i in range(2 * n)},
        compiler_params=pltpu.CompilerParams(has_side_effects=pltpu.SideEffectType.DATAFLOW_SIDE_EFFECTING),
    )(*srcs, *lands, send, recv, *after)
    return out[:n], out[n:]


def _swap_start(bufs, *, name):
    n = len(bufs)
    lands = [lax.empty(b.shape, b.dtype) for b in bufs]

    def body(*refs):
        src, land, send, recv, token = refs[:n], refs[n:2 * n], refs[2 * n], refs[2 * n + 1], refs[-1]
        sib = (lax.axis_index("x"), lax.axis_index("y"), 1 - lax.axis_index("c"))
        for w in range(n):
            _remote(src[w], land[w], send.at[w], recv.at[w], sib).start()
        token[...] = jnp.zeros_like(token)

    thru = tuple(pltpu.HBM(a.shape, a.dtype) for a in list(bufs) + lands)
    out = pl.pallas_call(
        body, name=name,
        out_shape=(pltpu.SemaphoreType.DMA((n,)), pltpu.SemaphoreType.DMA((n,))) + thru + (S((SUBLANE, LANE), f32),),
        in_specs=[HBM_SPEC] * (2 * n), out_specs=(SEM_SPEC, SEM_SPEC) + (HBM_SPEC,) * (2 * n) + (VMEM_SPEC,),
        input_output_aliases={i: 2 + i for i in range(2 * n)},
        compiler_params=pltpu.CompilerParams(has_side_effects=pltpu.SideEffectType.DATAFLOW_SIDE_EFFECTING),
    )(*[_in_hbm(a) for a in list(bufs) + lands])
    return out[0], out[1], out[2:2 + n], out[2 + n:2 + 2 * n], out[-1]


def _swap_wait(send, recv, bufs, lands, after, *, name):
    n = len(bufs)

    def body(*refs):
        src, land, send, recv = refs[:n], refs[n:2 * n], refs[2 * n], refs[2 * n + 1]
        sib = (lax.axis_index("x"), lax.axis_index("y"), 1 - lax.axis_index("c"))
        for w in range(n):
            cp = _remote(src[w], land[w], send.at[w], recv.at[w], sib)
            cp.wait_send()
            cp.wait_recv()

    thru = tuple(pltpu.HBM(a.shape, a.dtype) for a in list(bufs) + list(lands))
    out = pl.pallas_call(
        body, name=name, out_shape=thru,
        in_specs=[HBM_SPEC] * (2 * n) + [SEM_SPEC, SEM_SPEC] + [ANY_SPEC] * len(after), out_specs=(HBM_SPEC,) * (2 * n),
        input_output_aliases={i: i for i in range(2 * n)},
        compiler_params=pltpu.CompilerParams(has_side_effects=pltpu.SideEffectType.DATAFLOW_SIDE_EFFECTING),
    )(*bufs, *lands, send, recv, *after)
    return out[:n], out[n:]


N_DEVICES = 8


def _device_targets(x, y, c):
    flips = [(d >> 2 & 1, d >> 1 & 1, d & 1) for d in range(N_DEVICES)]
    return [(1 - x if fx else x, 1 - y if fy else y, 1 - c if fc else c) for fx, fy, fc in flips]


def _allgather_devices_start(buf, *, name):
    land = lax.empty((N_DEVICES,) + buf.shape, buf.dtype)

    def body(src, land, send, recv, src_thru, land_thru, token):
        x, y, c = lax.axis_index("x"), lax.axis_index("y"), lax.axis_index("c")
        me = 4 * x + 2 * y + c
        for k, peer in enumerate(_device_targets(x, y, c)):
            _remote(src, land.at[me], send.at[k], recv.at[k], peer).start()
        token[...] = jnp.zeros_like(token)

    return pl.pallas_call(
        body, name=name,
        out_shape=(pltpu.SemaphoreType.DMA((N_DEVICES,)), pltpu.SemaphoreType.DMA((N_DEVICES,)), pltpu.HBM(buf.shape, buf.dtype),
                   pltpu.HBM(land.shape, land.dtype), S((SUBLANE, LANE), f32)),
        in_specs=[HBM_SPEC, HBM_SPEC], out_specs=(SEM_SPEC, SEM_SPEC, HBM_SPEC, HBM_SPEC, VMEM_SPEC),
        input_output_aliases={0: 2, 1: 3},
        compiler_params=pltpu.CompilerParams(has_side_effects=pltpu.SideEffectType.DATAFLOW_SIDE_EFFECTING),
    )(_in_hbm(buf), _in_hbm(land))


def _allgather_devices_wait(send, recv, buf, land, after, *, name):
    def body(src, land, send, recv, *rest):
        x, y, c = lax.axis_index("x"), lax.axis_index("y"), lax.axis_index("c")
        for k, (px, py, pc) in enumerate(_device_targets(x, y, c)):
            cp = _remote(src, land.at[4 * px + 2 * py + pc], send.at[k], recv.at[k], (px, py, pc))
            cp.wait_send()
            cp.wait_recv()

    out = pl.pallas_call(
        body, name=name, out_shape=(pltpu.HBM(buf.shape, buf.dtype), pltpu.HBM(land.shape, land.dtype)),
        in_specs=[HBM_SPEC, HBM_SPEC, SEM_SPEC, SEM_SPEC] + [ANY_SPEC] * len(after), out_specs=(HBM_SPEC, HBM_SPEC),
        input_output_aliases={0: 0, 1: 1},
        compiler_params=pltpu.CompilerParams(has_side_effects=pltpu.SideEffectType.DATAFLOW_SIDE_EFFECTING),
    )(buf, land, send, recv, *after)
    return out[1]


def _sum_slots(buf, *, name):
    ns, R, C = buf.shape
    tr = _pick(R, 512)
    assert R % tr == 0

    def body(*refs):
        acc = refs[0][...]
        for r in refs[1:ns]:
            acc = acc + r[...]
        refs[ns][...] = acc

    specs = [pl.BlockSpec((None, tr, C), functools.partial(lambda s, i: (s, i, 0), s)) for s in range(ns)]
    return pl.pallas_call(body, name=name, out_shape=S((R, C), buf.dtype), grid=(R // tr,), in_specs=specs,
                          out_specs=pl.BlockSpec((tr, C), lambda i: (i, 0)), compiler_params=_cp("parallel"))(*([buf] * ns))


ADAMW_BLOCK_BYTES = 1 << 20


def _adamw(w, m, v, groups, *, name, layer=None, prev=None):
    shape = w.shape if layer is None else w.shape[1:]
    C = shape[-1]
    Rr = math.prod(shape[:-1])
    tr = Rr
    if Rr * C * 4 > ADAMW_BLOCK_BYTES:
        tr = max(t for t in range(2 * SUBLANE, Rr + 1, 2 * SUBLANE) if Rr % t == 0 and t * C * 4 <= ADAMW_BLOCK_BYTES)
    c1 = 1.0 / (1.0 - ADAM_B1 ** ADAM_STEP)
    c2 = 1.0 / (1.0 - ADAM_B2 ** ADAM_STEP)
    if layer is None:
        to2 = lambda t: t.reshape(Rr, C)
        spec = pl.BlockSpec((tr, C), lambda i: (i, 0))
        res_shape = S((Rr, C), f32)
    else:
        to2 = lambda t: t.reshape(layer[1], Rr, C)
        spec = pl.BlockSpec((None, tr, C), functools.partial(lambda l, i: (l, i, 0), layer[0]))
        res_shape = S((layer[1], Rr, C), f32)
    wspec, spec = spec, pl.BlockSpec((tr, C), lambda i: (i, 0))
    g_specs, g_args, sizes = [], [], []
    for grp in groups:
        sizes.append(len(grp))
        for term in grp:
            if isinstance(term, tuple):
                arr, slot = term
                g_specs.append(pl.BlockSpec((None, tr, C), functools.partial(lambda s, i: (s, i, 0), slot)))
                g_args.append(arr.reshape(arr.shape[0], Rr, C))
            else:
                g_specs.append(spec)
                g_args.append(term.reshape(Rr, C))
    nterms = len(g_args)
    prev = () if prev is None else tuple(to2(t) for t in prev)

    def body(w_ref, m_ref, v_ref, *rest):
        t_refs, (g_ref, d_ref, mo_ref, vo_ref) = rest[:nterms], rest[-4:]
        g, pos = None, 0
        for size in sizes:
            part = None
            for r in t_refs[pos:pos + size]:
                t = r[...].astype(f32)
                part = t if part is None else part + t
            pos += size
            g = part if g is None else g + part
        mn = ADAM_B1 * m_ref[...] + (1.0 - ADAM_B1) * g
        vn = ADAM_B2 * v_ref[...] + (1.0 - ADAM_B2) * (g * g)
        g_ref[...] = g
        mo_ref[...] = mn
        vo_ref[...] = vn
        d_ref[...] = -ADAM_LR * ((mn * c1) / (jnp.sqrt(vn * c2) + ADAM_EPS) + ADAM_WD * w_ref[...])

    out = pl.pallas_call(body, name=name, out_shape=(res_shape,) * 4, grid=(Rr // tr,),
                         in_specs=[wspec] * 3 + g_specs + [ANY_SPEC] * len(prev), out_specs=(wspec,) * 4,
                         input_output_aliases={3 + nterms + k: k for k in range(len(prev))},
                         compiler_params=_cp("parallel"))(to2(w), to2(m), to2(v), *g_args, *prev)
    return tuple(o.reshape(w.shape) for o in out)


def _pack_rows(parts, dtype):
    flat = jnp.concatenate([p.reshape(-1).astype(dtype) for p in parts])
    n = flat.shape[0]
    unit = PACK_COLS * 2 * SUBLANE
    padded = -(-n // unit) * unit
    return jnp.pad(flat, (0, padded - n)).reshape(padded // PACK_COLS, PACK_COLS)


def _unpack_rows(flat2d, shapes):
    flat = flat2d.reshape(-1)
    out, off = [], 0
    for shp in shapes:
        n = math.prod(shp)
        out.append(flat[off:off + n].reshape(shp))
        off += n
    return out


def _gather_weights(local, names, dtype, *, name):
    shapes = [local[n].shape for n in names]
    got = _all_gather_chips(_pack_rows([local[n] for n in names], dtype), name=name)
    per_chip = [_unpack_rows(got[s], shapes) for s in range(N_CHIPS)]
    return {n: jnp.concatenate([per_chip[s][k] for s in range(N_CHIPS)], axis=SHARD_AXIS[n]) for k, n in enumerate(names)}


def kernel(x, mem, norm_g, ssm_in_w, ssm_conv_w, ssm_conv_b, ssm_dt_bias, ssm_A_log, ssm_D, ssm_norm_g, ssm_out_w, cf_pw1_w, cf_pw1_b, cf_dw_w, cf_dw_b, cf_ln_g, cf_ln_b, cf_pw2_w, cf_pw2_b, xa_mem_g, xa_q_w, xa_kv_w, xa_o_w, ffn_in_w, ffn_conv_w, ffn_conv_b, ffn_out_w, loss_target, m_norm_g, m_ssm_in_w, m_ssm_conv_w, m_ssm_conv_b, m_ssm_dt_bias, m_ssm_A_log, m_ssm_D, m_ssm_norm_g, m_ssm_out_w, m_cf_pw1_w, m_cf_pw1_b, m_cf_dw_w, m_cf_dw_b, m_cf_ln_g, m_cf_ln_b, m_cf_pw2_w, m_cf_pw2_b, m_xa_mem_g, m_xa_q_w, m_xa_kv_w, m_xa_o_w, m_ffn_in_w, m_ffn_conv_w, m_ffn_conv_b, m_ffn_out_w, v_norm_g, v_ssm_in_w, v_ssm_conv_w, v_ssm_conv_b, v_ssm_dt_bias, v_ssm_A_log, v_ssm_D, v_ssm_norm_g, v_ssm_out_w, v_cf_pw1_w, v_cf_pw1_b, v_cf_dw_w, v_cf_dw_b, v_cf_ln_g, v_cf_ln_b, v_cf_pw2_w, v_cf_pw2_b, v_xa_mem_g, v_xa_q_w, v_xa_kv_w, v_xa_o_w, v_ffn_in_w, v_ffn_conv_w, v_ffn_conv_b, v_ffn_out_w):
    w_local = dict(zip(WEIGHT_NAMES, (norm_g, ssm_in_w, ssm_conv_w, ssm_conv_b, ssm_dt_bias, ssm_A_log, ssm_D, ssm_norm_g,
                                      ssm_out_w, cf_pw1_w, cf_pw1_b, cf_dw_w, cf_dw_b, cf_ln_g, cf_ln_b, cf_pw2_w, cf_pw2_b,
                                      xa_mem_g, xa_q_w, xa_kv_w, xa_o_w, ffn_in_w, ffn_conv_w, ffn_conv_b, ffn_out_w)))
    m_local = dict(zip(WEIGHT_NAMES, (m_norm_g, m_ssm_in_w, m_ssm_conv_w, m_ssm_conv_b, m_ssm_dt_bias, m_ssm_A_log, m_ssm_D,
                                      m_ssm_norm_g, m_ssm_out_w, m_cf_pw1_w, m_cf_pw1_b, m_cf_dw_w, m_cf_dw_b, m_cf_ln_g,
                                      m_cf_ln_b, m_cf_pw2_w, m_cf_pw2_b, m_xa_mem_g, m_xa_q_w, m_xa_kv_w, m_xa_o_w,
                                      m_ffn_in_w, m_ffn_conv_w, m_ffn_conv_b, m_ffn_out_w)))
    v_local = dict(zip(WEIGHT_NAMES, (v_norm_g, v_ssm_in_w, v_ssm_conv_w, v_ssm_conv_b, v_ssm_dt_bias, v_ssm_A_log, v_ssm_D,
                                      v_ssm_norm_g, v_ssm_out_w, v_cf_pw1_w, v_cf_pw1_b, v_cf_dw_w, v_cf_dw_b, v_cf_ln_g,
                                      v_cf_ln_b, v_cf_pw2_w, v_cf_pw2_b, v_xa_mem_g, v_xa_q_w, v_xa_kv_w, v_xa_o_w,
                                      v_ffn_in_w, v_ffn_conv_w, v_ffn_conv_b, v_ffn_out_w)))

    small = [n for n in WEIGHT_NAMES if n not in MATMUL_WEIGHTS]
    small_sharded = [n for n in small if SHARD_AXIS[n] is not None]
    W = {n: w_local[n] for n in small if SHARD_AXIS[n] is None}
    W.update(_gather_weights(w_local, small_sharded, f32, name="gather_small_weights"))

    def layer_index(n, i):
        return i // 2 if n in MIXER_WEIGHTS else i

    def keys_of(i, parts):
        return [(n, layer_index(n, i)) for part in parts for n in _layer_matmul_weights(i, part)]

    def shards(keys):
        return [w_local[n][l].astype(bf16) for n, l in keys]

    def usable(n, a):
        return a.reshape(N_CHIPS * a.shape[1], a.shape[2]) if n in ROW_SHARDED else a

    mixer0 = keys_of(0, ("mixer",))
    got0 = _gather_matmul_weights([s.reshape(1, 2, s.shape[0] // 2, s.shape[1]) for s in shards(mixer0)],
                                  name="gather_layer0_mixer")
    gather_groups = {(0, "rest"): keys_of(0, ("rest",))}
    gather_groups.update({(i, "mixer"): keys_of(i, ("mixer", "rest")) for i in range(1, DEPTH)})
    gathers, tokens, landed = {}, [], {}
    for gkey in sorted(gather_groups):
        send, recv, srcs, lands, token = _spread_start(shards(gather_groups[gkey]), False, name="gather_start_%d_%s" % gkey,
                                                       after=(got0[0], W[small_sharded[0]]))
        gathers[gkey] = (send, recv, srcs, lands)
        tokens.append(token)

    def layer_weights(i, part, after):
        if (i, part) == (0, "mixer"):
            landed.update({k: g.reshape((N_CHIPS, 2 * g.shape[3], g.shape[4])) for k, g in zip(mixer0, got0)})
        elif (i, part) in gathers:
            _, lands = _spread_wait(*gathers[i, part], (after,), False, name="gather_wait_%d_%s" % (i, part))
            landed.update(zip(gather_groups[i, part], lands))
        return {n: usable(n, landed[n, layer_index(n, i)]) for n in _layer_matmul_weights(i, part)}

    pending, scatters, swaps, own, sib = {}, {}, {}, {}, {}

    def scatter_start(gkey, keys, gl):
        send, recv, srcs, lands, token = _spread_start([gl[k] for k in keys], True, name="grads_start_%d_%s" % gkey)
        scatters[gkey] = (keys, send, recv, srcs, lands)
        return token

    def layer_grads(i, part, gl):
        grads = {(n, layer_index(n, i)): gl[n] for n in _layer_matmul_weights(i, part)}
        behind = []
        if part == "rest":
            if i + 1 < DEPTH:
                keys, send, recv, srcs, lands = scatters.pop((i + 1, "mixer"))
                _, lands = _spread_wait(send, recv, srcs, lands, (grads['xa_kv_w', i],), True,
                                        name="grads_wait_%d" % (i + 1))
                send, recv, srcs, lands, token = _swap_start(lands, name="grads_swap_start_%d" % (i + 1))
                swaps[i + 1] = (keys, send, recv, srcs, lands)
                behind.append(token)
            if i == 0:
                behind.append(scatter_start((0, "rest"), list(grads), grads))
            else:
                pending.update(grads)
        else:
            pending.update(grads)
            if i == 0:
                keys, send, recv, srcs, lands = scatters.pop((0, "rest"))
                _, lands = _spread_wait(send, recv, srcs, lands, (grads['ssm_in_w', 0],), True, name="grads_wait_0_rest")
                send, recv, srcs, lands, token = _swap_start(lands, name="grads_swap_start_0")
                swaps[0] = (keys, send, recv, srcs, lands)
                behind.append(token)
            behind.append(scatter_start((i, "mixer"), list(pending), dict(pending)))
            pending.clear()
        return behind

    sse, gx, gsmall = _device_step(x[0], mem[0], loss_target[0], W, layer_weights, layer_grads, tuple(tokens))

    loss = lax.psum(0.5 * sse[0, 0] / D_MODEL, ("x", "y", "c"))

    small_shapes = [gsmall[n].shape for n in small]
    ag_send, ag_recv, ag_src, ag_land, ag_token = _allgather_devices_start(_pack_rows([gsmall[n] for n in small], f32),
                                                                           name="allgather_small_start")
    last_keys, last_lands = [], []
    for gkey in sorted(scatters):
        keys, send, recv, srcs, lands = scatters[gkey]
        _, lands = _spread_wait(send, recv, srcs, lands, (gx, ag_token), True, name="grads_wait_%d_%s" % gkey)
        last_keys += keys
        last_lands += list(lands)
    last_swap = _swap_start(last_lands, name="grads_swap_start_last")
    for i in sorted(swaps):
        keys, send, recv, srcs, lands = swaps[i]
        mine, theirs = _swap_wait(send, recv, srcs, lands, (gx, last_swap[4]), name="grads_swap_wait_%d" % i)
        own.update(zip(keys, mine))
        sib.update(zip(keys, theirs))

    res = {}

    def adamw_matmul_weight(n, l):
        layers = w_local[n].shape[0]
        groups = [[(own[n, l], s) for s in range(N_CHIPS)], [(sib[n, l], s) for s in range(N_CHIPS)]]
        res[n] = _adamw(w_local[n], m_local[n], v_local[n], groups, name="adamw_%s_%d" % (n, l), layer=(l, layers),
                        prev=res.get(n))

    for n in MATMUL_WEIGHTS:
        for l in range(w_local[n].shape[0]):
            if (n, l) not in last_keys:
                adamw_matmul_weight(n, l)
    done = res[MATMUL_WEIGHTS[-1]][0]
    mine, theirs = _swap_wait(*last_swap[:4], (done,), name="grads_swap_wait_last")
    own.update(zip(last_keys, mine))
    sib.update(zip(last_keys, theirs))
    for n, l in last_keys:
        adamw_matmul_weight(n, l)
    slots = _allgather_devices_wait(ag_send, ag_recv, ag_src, ag_land, (res[last_keys[-1][0]][0],), name="allgather_small_wait")
    gsum = dict(zip(small, _unpack_rows(_sum_slots(slots, name="sum_small_grads"), small_shapes)))
    chip = 2 * lax.axis_index("x") + lax.axis_index("y")

    for n in small:
        g, ax = gsum[n], SHARD_AXIS[n]
        if ax is not None:
            width = w_local[n].shape[ax]
            g = lax.dynamic_slice_in_dim(g, chip * width, width, axis=ax)
        res[n] = _adamw(w_local[n], m_local[n], v_local[n], [[g]], name="adamw_" + n)
    return (loss, gx[None], *[res[n][0] for n in WEIGHT_NAMES], *[res[n][1] for n in WEIGHT_NAMES],
            *[res[n][2] for n in WEIGHT_NAMES], *[res[n][3] for n in WEIGHT_NAMES])
```

```python
import functools
import math

import jax
import jax.numpy as jnp
from jax import lax
from jax.experimental import pallas as pl
from jax.experimental.pallas import tpu as pltpu

f32 = jnp.float32
bf16 = jnp.bfloat16
S = jax.ShapeDtypeStruct

D_MODEL = 1024
DEPTH = 4
D_INNER = 2048
HEAD_DIM = 64
N_GROUPS = 4
HEADS_PER_GROUP = 8
N_SSM_HEADS = 32
D_STATE = 128
CHUNK = 128
SSM_CONV = 4
CONV_DIM = 3072
CF_KERNEL = 31
N_MEM = 256
XA_HEADS = 4
XA_HEAD_DIM = 256
D_FF = 2816
FFN_CONV = 3
EPS = 1e-6
ADAM_LR, ADAM_B1, ADAM_B2, ADAM_EPS, ADAM_WD, ADAM_STEP = 0.001, 0.9, 0.999, 1e-08, 0.01, 10

LANE = 128
SUBLANE = 8
ROW_SUB = 64
VMEM_LIMIT = 56 * 1024 * 1024
N_CHIPS = 4
PACK_COLS = 1024

WEIGHT_NAMES = ['norm_g', 'ssm_in_w', 'ssm_conv_w', 'ssm_conv_b', 'ssm_dt_bias', 'ssm_A_log', 'ssm_D', 'ssm_norm_g',
                'ssm_out_w', 'cf_pw1_w', 'cf_pw1_b', 'cf_dw_w', 'cf_dw_b', 'cf_ln_g', 'cf_ln_b', 'cf_pw2_w', 'cf_pw2_b',
                'xa_mem_g', 'xa_q_w', 'xa_kv_w', 'xa_o_w', 'ffn_in_w', 'ffn_conv_w', 'ffn_conv_b', 'ffn_out_w']
SHARD_AXIS = {'norm_g': 2, 'ssm_in_w': 2, 'ssm_conv_w': 2, 'ssm_conv_b': None, 'ssm_dt_bias': None, 'ssm_A_log': None,
              'ssm_D': None, 'ssm_norm_g': None, 'ssm_out_w': 1, 'cf_pw1_w': 2, 'cf_pw1_b': 1, 'cf_dw_w': 2, 'cf_dw_b': 1,
              'cf_ln_g': 1, 'cf_ln_b': 1, 'cf_pw2_w': 1, 'cf_pw2_b': 1, 'xa_mem_g': None, 'xa_q_w': 1, 'xa_kv_w': 2,
              'xa_o_w': 1, 'ffn_in_w': 2, 'ffn_conv_w': 2, 'ffn_conv_b': None, 'ffn_out_w': 1}
MATMUL_WEIGHTS = ('ssm_in_w', 'ssm_out_w', 'cf_pw1_w', 'cf_pw2_w', 'xa_q_w', 'xa_kv_w', 'xa_o_w', 'ffn_in_w', 'ffn_out_w')


def _cp(*sem):
    return pltpu.CompilerParams(dimension_semantics=tuple(sem), vmem_limit_bytes=VMEM_LIMIT)


def _pick(dim, pref):
    if dim <= pref:
        return dim
    best = None
    for t in range(LANE, pref + 1, LANE):
        if dim % t == 0:
            best = t
    assert best is not None, (dim, pref)
    return best


def _sigmoid(x):
    return 1.0 / (1.0 + jnp.exp(-x))


def _silu(x):
    return x * _sigmoid(x)


def _dsilu(x):
    s = _sigmoid(x)
    return s * (1.0 + x * (1.0 - s))


def _softplus(x):
    return jnp.maximum(x, 0.0) + jnp.log(1.0 + jnp.exp(-jnp.abs(x)))


_DN = {"nn": (((1,), (0,)), ((), ())), "nt": (((1,), (1,)), ((), ())), "tn": (((0,), (0,)), ((), ()))}


def _mm(a, b, mode, *, name, out_dtype=f32, bias=None, add=None, b_shards=False, out_shards=False, after=()):
    bshape = (b.shape[1], b.shape[2] * N_CHIPS) if b_shards else b.shape
    if mode == "nn":
        (M, K), (K2, N) = a.shape, bshape
    elif mode == "nt":
        (M, K), (N, K2) = a.shape, bshape
    else:
        (K, M), (K2, N) = a.shape, bshape
    assert K == K2, (a.shape, b.shape, mode)
    n_unit = N // N_CHIPS if ((b_shards and mode == "nn") or out_shards) else N
    k_unit = K // N_CHIPS if (b_shards and mode == "nt") else K
    tm, tn, tk = _pick(M, 1024), _pick(n_unit, 1408), _pick(k_unit, 1408)
    nk, nj_u, nk_u = K // tk, n_unit // tn, k_unit // tk
    a_spec = {"nn": pl.BlockSpec((tm, tk), lambda i, j, k: (i, k)), "nt": pl.BlockSpec((tm, tk), lambda i, j, k: (i, k)),
              "tn": pl.BlockSpec((tk, tm), lambda i, j, k: (k, i))}[mode]
    if not b_shards:
        b_spec = {"nn": pl.BlockSpec((tk, tn), lambda i, j, k: (k, j)), "nt": pl.BlockSpec((tn, tk), lambda i, j, k: (j, k)),
                  "tn": pl.BlockSpec((tk, tn), lambda i, j, k: (k, j))}[mode]
    else:
        b_spec = {"nn": pl.BlockSpec((None, tk, tn), lambda i, j, k: (j // nj_u, k, j % nj_u)),
                  "nt": pl.BlockSpec((None, tn, tk), lambda i, j, k: (k // nk_u, j, k % nk_u))}[mode]
    in_specs, args = [a_spec, b_spec], [a, b]
    if bias is not None:
        in_specs.append(pl.BlockSpec((1, tn), lambda i, j, k: (0, j)))
        args.append(bias)
    if add is not None:
        in_specs.append(pl.BlockSpec((tm, tn), lambda i, j, k: (i, j)))
        args.append(add)
    in_specs += [pl.BlockSpec(memory_space=pl.ANY)] * len(after)
    args += list(after)
    if not out_shards:
        out_shape, out_spec = S((M, N), out_dtype), pl.BlockSpec((tm, tn), lambda i, j, k: (i, j))
    else:
        out_shape = S((N_CHIPS, M, n_unit), out_dtype)
        out_spec = pl.BlockSpec((None, tm, tn), lambda i, j, k: (j // nj_u, i, j % nj_u))
    dn = _DN[mode]
    has_bias, has_add = bias is not None, add is not None

    def body(a_ref, b_ref, *rest):
        rest = list(rest)
        bias_ref = rest.pop(0) if has_bias else None
        add_ref = rest.pop(0) if has_add else None
        rest = rest[len(after):]
        o_ref = rest[0]

        def finish(r):
            if has_bias:
                r = r + bias_ref[...]
            if has_add:
                r = r + add_ref[...].astype(f32)
            o_ref[...] = r.astype(out_dtype)

        part = lax.dot_general(a_ref[...].astype(bf16), b_ref[...].astype(bf16), dn, preferred_element_type=f32)
        if nk == 1:
            finish(part)
            return
        acc_ref = rest[1]
        k = pl.program_id(2)

        @pl.when(k == 0)
        def _():
            acc_ref[...] = part

        @pl.when(k > 0)
        def _():
            acc_ref[...] += part

        @pl.when(k == nk - 1)
        def _():
            finish(acc_ref[...])

    return pl.pallas_call(
        body, name=name, out_shape=out_shape, grid=(M // tm, N // tn, nk),
        in_specs=in_specs, out_specs=out_spec, scratch_shapes=[pltpu.VMEM((tm, tn), f32)] if nk > 1 else [],
        compiler_params=_cp("parallel", "parallel", "arbitrary"))(*args)


def _rows(tm, C):
    return pl.BlockSpec((tm, C), lambda i: (i, 0))


def _const(shape):
    return pl.BlockSpec(shape, lambda i: tuple(0 for _ in shape))


def _rms_val(x, g):
    r = lax.rsqrt(jnp.mean(x * x, axis=-1, keepdims=True) + EPS)
    return x * r * g


def _rms_bwd_val(x, g, dy):
    r = lax.rsqrt(jnp.mean(x * x, axis=-1, keepdims=True) + EPS)
    xn = x * r
    dxh = dy * g
    dx = r * (dxh - xn * jnp.mean(dxh * xn, axis=-1, keepdims=True))
    return dx, jnp.sum(dy * xn, axis=0, keepdims=True)


ANY_SPEC = pl.BlockSpec(memory_space=pl.ANY)


def _rmsnorm_fwd(x, g, *, name, after=()):
    L, C = x.shape
    tm = _pick(L, 512)

    def body(x_ref, g_ref, *rest):
        rest[-1][...] = _rms_val(x_ref[...], g_ref[...]).astype(bf16)

    return pl.pallas_call(body, name=name, out_shape=S((L, C), bf16), grid=(L // tm,),
                          in_specs=[_rows(tm, C), _const((1, C))] + [ANY_SPEC] * len(after), out_specs=_rows(tm, C),
                          compiler_params=_cp("parallel"))(x, g, *after)


def _resid_norm_fwd(x, mix, g_post, g_next, *, name):
    L, C = x.shape
    tm = _pick(L, 512)
    want_h = g_next is not None

    def body(x_ref, m_ref, gp_ref, *rest):
        xn = x_ref[...] + _rms_val(m_ref[...], gp_ref[...])
        if want_h:
            gn_ref, xo_ref, h_ref = rest
            h_ref[...] = _rms_val(xn, gn_ref[...]).astype(bf16)
        else:
            (xo_ref,) = rest
        xo_ref[...] = xn

    in_specs = [_rows(tm, C), _rows(tm, C), _const((1, C))]
    args = [x, mix, g_post]
    out_shape, out_specs = [S((L, C), f32)], [_rows(tm, C)]
    if want_h:
        in_specs.append(_const((1, C)))
        args.append(g_next)
        out_shape.append(S((L, C), bf16))
        out_specs.append(_rows(tm, C))
    out = pl.pallas_call(body, name=name, out_shape=tuple(out_shape), grid=(L // tm,), in_specs=in_specs,
                         out_specs=tuple(out_specs), compiler_params=_cp("parallel"))(*args)
    return (out[0], out[1]) if want_h else (out[0], None)


def _norm_bwd(x, g, dy, *, name, add=None, out_dtype=f32, after=()):
    L, C = x.shape
    tm = _pick(L, 512)
    has_add = add is not None

    def body(x_ref, g_ref, dy_ref, *rest):
        rest = list(rest)
        add_ref = rest.pop(0) if has_add else None
        dx_ref, dg_ref, cs_ref = rest[-3:]
        i = pl.program_id(0)

        @pl.when(i == 0)
        def _():
            dg_ref[...] = jnp.zeros_like(dg_ref)
            cs_ref[...] = jnp.zeros_like(cs_ref)

        dx, dg = _rms_bwd_val(x_ref[...], g_ref[...], dy_ref[...].astype(f32))
        dg_ref[...] += dg
        cs_ref[...] += jnp.sum(dx, axis=0, keepdims=True)
        if has_add:
            dx = dx + add_ref[...]
        dx_ref[...] = dx.astype(out_dtype)

    in_specs = [_rows(tm, C), _const((1, C)), _rows(tm, C)]
    args = [x, g, dy]
    if has_add:
        in_specs.append(_rows(tm, C))
        args.append(add)
    in_specs += [ANY_SPEC] * len(after)
    args += list(after)
    return pl.pallas_call(body, name=name, out_shape=(S((L, C), out_dtype), S((1, C), f32), S((1, C), f32)),
                          grid=(L // tm,), in_specs=in_specs,
                          out_specs=(_rows(tm, C), _const((1, C)), _const((1, C))),
                          compiler_params=_cp("arbitrary"))(*args)


def _norm_bwd_chain(x, g, dy, add, x2, g2, *, name, after=()):
    L, C = x.shape
    tm = _pick(L, 512)

    def body(x_ref, g_ref, dy_ref, add_ref, x2_ref, g2_ref, *rest):
        G_ref, dg_ref, d2_ref, dg2_ref, cs2_ref = rest[-5:]
        i = pl.program_id(0)

        @pl.when(i == 0)
        def _():
            dg_ref[...] = jnp.zeros_like(dg_ref)
            dg2_ref[...] = jnp.zeros_like(dg2_ref)
            cs2_ref[...] = jnp.zeros_like(cs2_ref)

        dx, dg = _rms_bwd_val(x_ref[...], g_ref[...], dy_ref[...].astype(f32))
        G = dx + add_ref[...]
        dg_ref[...] += dg
        G_ref[...] = G
        d2, dg2 = _rms_bwd_val(x2_ref[...], g2_ref[...], G)
        dg2_ref[...] += dg2
        cs2_ref[...] += jnp.sum(d2, axis=0, keepdims=True)
        d2_ref[...] = d2.astype(bf16)

    row, vec = _rows(tm, C), _const((1, C))
    return pl.pallas_call(body, name=name,
                          out_shape=(S((L, C), f32), S((1, C), f32), S((L, C), bf16), S((1, C), f32), S((1, C), f32)),
                          grid=(L // tm,), in_specs=[row, vec, row, row, row, vec] + [ANY_SPEC] * len(after),
                          out_specs=(row, vec, row, vec, vec),
                          compiler_params=_cp("arbitrary"))(x, g, dy, add, x2, g2, *after)


def _loss_fwd_bwd(y, target, *, name):
    L, C = y.shape
    tm = _pick(L, 512)

    def body(y_ref, t_ref, acc_ref, dy_ref):
        i = pl.program_id(0)

        @pl.when(i == 0)
        def _():
            acc_ref[...] = jnp.zeros_like(acc_ref)

        e = y_ref[...] - t_ref[...]
        rs = jnp.sum(e * e, axis=-1, keepdims=True)
        acc_ref[...] += jnp.broadcast_to(jnp.sum(rs, axis=0, keepdims=True), (1, LANE))
        dy_ref[...] = e * (1.0 / C)

    return pl.pallas_call(body, name=name, out_shape=(S((1, LANE), f32), S((L, C), f32)), grid=(L // tm,),
                          in_specs=[_rows(tm, C), _rows(tm, C)], out_specs=(_const((1, LANE)), _rows(tm, C)),
                          compiler_params=_cp("arbitrary"))(y, target)


def _halo_rows(K):
    return SUBLANE if K - 1 <= SUBLANE else 32


def _prev_halo_spec(tm, H, C):
    return pl.BlockSpec((H, C), lambda i: (jnp.maximum(i * (tm // H) - 1, 0), 0))


def _next_halo_spec(tm, H, C, L):
    return pl.BlockSpec((H, C), lambda i: (jnp.minimum((i + 1) * (tm // H), L // H - 1), 0))


def _down_views(ext, H, n, K):
    bases, views = {}, []
    for s in range(K):
        q, r = divmod(s, SUBLANE)
        if r not in bases:
            bases[r] = ext if r == 0 else pltpu.roll(ext, r, axis=0)
        views.append(bases[r][H - SUBLANE * q:H - SUBLANE * q + n])
    return views


def _up_views(ext, n, K):
    bases, views = {}, []
    for s in range(K):
        q, r = divmod(s, SUBLANE)
        if r not in bases:
            bases[r] = ext if r == 0 else pltpu.roll(ext, ext.shape[0] - r, axis=0)
        views.append(bases[r][SUBLANE * q:SUBLANE * q + n])
    return views


def _causal_conv(ext, H, w_ref, K):
    views = _down_views(ext, H, ext.shape[0] - H, K)
    acc = None
    for k in range(K):
        term = views[K - 1 - k] * w_ref[k:k + 1, :]
        acc = term if acc is None else acc + term
    return acc


def _anticausal_conv(ext, tm, w_ref, K):
    views = _up_views(ext, tm, K)
    acc = None
    for k in range(K):
        term = views[K - 1 - k] * w_ref[k:k + 1, :]
        acc = term if acc is None else acc + term
    return acc


def _tap_grads(dw_ref, d_cur, x_ext, H, K):
    views = _down_views(x_ext, H, d_cur.shape[0], K)
    for k in range(K):
        dw_ref[k:k + 1, :] += jnp.sum(d_cur * views[K - 1 - k], axis=0, keepdims=True)


def _fold_rows(x):
    acc = x[0:SUBLANE]
    for r in range(SUBLANE, x.shape[0], SUBLANE):
        acc = acc + x[r:r + SUBLANE]
    return acc


def _pad_taps(w, K):
    return jnp.pad(w, ((0, _halo_rows(K) - K), (0, 0)))


def _conv_act_fwd(x, w, b, *, K, act, name, out_dtype, tm_pref=256):
    L, C = x.shape
    H = _halo_rows(K)
    tm = _pick(L, tm_pref)
    Co = C if act == "silu" else C // 2

    rs = min(ROW_SUB, tm)

    def body(h_ref, x_ref, w_ref, b_ref, o_ref, ext_scr):
        i = pl.program_id(0)
        ext_scr[0:H] = jnp.where(i > 0, h_ref[...], 0.0)
        ext_scr[H:] = x_ref[...]
        for j in range(Co // LANE):
            lanes = [j] if act == "silu" else [j, j + Co // LANE]
            wb = [(w_ref[:, c * LANE:(c + 1) * LANE], b_ref[:, c * LANE:(c + 1) * LANE]) for c in lanes]
            for r0 in range(0, tm, rs):
                us = [_causal_conv(ext_scr[r0:r0 + rs + H, c * LANE:(c + 1) * LANE], H, wc, K) + bc
                      for c, (wc, bc) in zip(lanes, wb)]
                y = _silu(us[0]) if act == "silu" else _silu(us[0]) * us[1]
                o_ref[r0:r0 + rs, j * LANE:(j + 1) * LANE] = y.astype(out_dtype)

    return pl.pallas_call(body, name=name, out_shape=S((L, Co), out_dtype), grid=(L // tm,),
                          in_specs=[_prev_halo_spec(tm, H, C), _rows(tm, C), _const((H, C)), _const((1, C))],
                          out_specs=_rows(tm, Co), scratch_shapes=[pltpu.VMEM((H + tm, C), f32)],
                          compiler_params=_cp("parallel"))(x, x, w, b)


def _conv_act_bwd(x, dparts, w, b, *, K, act, name, tm_pref=256):
    L, C = x.shape
    H = _halo_rows(K)
    tm = _pick(L, tm_pref)
    nb = L // tm
    Co = C if act == "silu" else C // 2
    nparts = len(dparts)

    rs = min(ROW_SUB, tm)

    def body(hp_ref, x_ref, hn_ref, w_ref, b_ref, *rest):
        d_refs, dn_refs = rest[:nparts], rest[nparts:2 * nparts]
        dx_ref, dw_ref, db_ref, ext_scr, d_scr = rest[2 * nparts:]
        i = pl.program_id(0)

        @pl.when(i == 0)
        def _():
            dw_ref[...] = jnp.zeros_like(dw_ref)
            db_ref[...] = jnp.zeros_like(db_ref)

        ext_scr[0:H] = jnp.where(i > 0, hp_ref[...], 0.0)
        ext_scr[H:H + tm] = x_ref[...]
        ext_scr[H + tm:] = jnp.where(i < nb - 1, hn_ref[...], 0.0)
        off = 0
        for r, rn in zip(d_refs, dn_refs):
            d_scr[0:tm, off:off + r.shape[1]] = r[...].astype(f32)
            d_scr[tm:, off:off + r.shape[1]] = jnp.where(i < nb - 1, rn[...].astype(f32), 0.0)
            off += r.shape[1]
        for j in range(Co // LANE):
            lanes = [j] if act == "silu" else [j, j + Co // LANE]
            wb = [(w_ref[:, c * LANE:(c + 1) * LANE], b_ref[:, c * LANE:(c + 1) * LANE]) for c in lanes]
            db_acc = [jnp.zeros((SUBLANE, LANE), f32) for _ in lanes]
            dw_acc = [[jnp.zeros((SUBLANE, LANE), f32) for _ in range(K)] for _ in lanes]
            for r0 in range(0, tm, rs):
                xvs = [_down_views(ext_scr[r0:r0 + rs + 2 * H, c * LANE:(c + 1) * LANE], H, rs + H, K) for c in lanes]
                us = [sum(xv[K - 1 - k] * wc[k:k + 1, :] for k in range(K)) + bc for xv, (wc, bc) in zip(xvs, wb)]
                d = d_scr[r0:r0 + rs + H, j * LANE:(j + 1) * LANE]
                dus = [d * _dsilu(us[0])] if act == "silu" else [d * us[1] * _dsilu(us[0]), d * _silu(us[0])]
                for q, (c, xv, du, (wc, _)) in enumerate(zip(lanes, xvs, dus, wb)):
                    dx_ref[r0:r0 + rs, c * LANE:(c + 1) * LANE] = _anticausal_conv(du, rs, wc, K).astype(bf16)
                    du_cur = du[:rs]
                    db_acc[q] = db_acc[q] + _fold_rows(du_cur)
                    for k in range(K):
                        dw_acc[q][k] = dw_acc[q][k] + _fold_rows(du_cur * xv[K - 1 - k][:rs])
            for q, c in enumerate(lanes):
                db_ref[:, c * LANE:(c + 1) * LANE] += jnp.sum(db_acc[q], axis=0, keepdims=True)
                for k in range(K):
                    dw_ref[k:k + 1, c * LANE:(c + 1) * LANE] += jnp.sum(dw_acc[q][k], axis=0, keepdims=True)

    in_specs = [_prev_halo_spec(tm, H, C), _rows(tm, C), _next_halo_spec(tm, H, C, L), _const((H, C)), _const((1, C))]
    in_specs += [_rows(tm, p.shape[1]) for p in dparts] + [_next_halo_spec(tm, H, p.shape[1], L) for p in dparts]
    return pl.pallas_call(body, name=name, out_shape=(S((L, C), bf16), S((H, C), f32), S((1, C), f32)), grid=(nb,),
                          in_specs=in_specs, out_specs=(_rows(tm, C), _const((H, C)), _const((1, C))),
                          scratch_shapes=[pltpu.VMEM((tm + 2 * H, C), f32), pltpu.VMEM((tm + H, Co), f32)],
                          compiler_params=_cp("arbitrary"))(x, x, x, w, b, *dparts, *dparts)


def _cf_fwd(u, dw_w, dw_b, ln_g, ln_b, *, name):
    L, C2 = u.shape
    C = C2 // 2
    K, H = CF_KERNEL, _halo_rows(CF_KERNEL)
    tm = _pick(L, 256)

    rs = min(ROW_SUB, tm)
    nl = C // LANE

    def body(h_ref, u_ref, w_ref, b_ref, g_ref, lb_ref, c_ref, s_ref, u_scr):
        i = pl.program_id(0)
        u_scr[0:H] = jnp.where(i > 0, h_ref[...], 0.0)
        u_scr[H:] = u_ref[...]
        for j in range(nl):
            wj, bj = w_ref[:, j * LANE:(j + 1) * LANE], b_ref[:, j * LANE:(j + 1) * LANE]
            for r0 in range(0, tm, rs):
                glu = u_scr[r0:r0 + rs + H, j * LANE:(j + 1) * LANE] \
                    * _sigmoid(u_scr[r0:r0 + rs + H, (nl + j) * LANE:(nl + j + 1) * LANE])
                c_ref[r0:r0 + rs, j * LANE:(j + 1) * LANE] = _causal_conv(glu, H, wj, K) + bj
        c = c_ref[...]
        mu = jnp.mean(c, axis=-1, keepdims=True)
        xc = c - mu
        var = jnp.mean(xc * xc, axis=-1, keepdims=True)
        ln = xc * lax.rsqrt(var + EPS) * g_ref[...] + lb_ref[...]
        s_ref[...] = _silu(ln).astype(bf16)

    return pl.pallas_call(body, name=name, out_shape=(S((L, C), f32), S((L, C), bf16)), grid=(L // tm,),
                          in_specs=[_prev_halo_spec(tm, H, C2), _rows(tm, C2), _const((H, C)), _const((1, C)),
                                    _const((1, C)), _const((1, C))],
                          out_specs=(_rows(tm, C), _rows(tm, C)), scratch_shapes=[pltpu.VMEM((H + tm, C2), f32)],
                          compiler_params=_cp("parallel"))(u, u, dw_w, dw_b, ln_g, ln_b)


def _cf_ln_bwd(c, ln_g, ln_b, ds, *, name):
    L, C = c.shape
    tm = _pick(L, 512)

    def body(c_ref, g_ref, lb_ref, ds_ref, dc_ref, dg_ref, db_ref):
        i = pl.program_id(0)

        @pl.when(i == 0)
        def _():
            dg_ref[...] = jnp.zeros_like(dg_ref)
            db_ref[...] = jnp.zeros_like(db_ref)

        c = c_ref[...]
        mu = jnp.mean(c, axis=-1, keepdims=True)
        xc = c - mu
        r = lax.rsqrt(jnp.mean(xc * xc, axis=-1, keepdims=True) + EPS)
        xh = xc * r
        ln = xh * g_ref[...] + lb_ref[...]
        dln = ds_ref[...].astype(f32) * _dsilu(ln)
        dg_ref[...] += jnp.sum(dln * xh, axis=0, keepdims=True)
        db_ref[...] += jnp.sum(dln, axis=0, keepdims=True)
        dxh = dln * g_ref[...]
        dc_ref[...] = r * (dxh - jnp.mean(dxh, axis=-1, keepdims=True) - xh * jnp.mean(dxh * xh, axis=-1, keepdims=True))

    return pl.pallas_call(body, name=name, out_shape=(S((L, C), f32), S((1, C), f32), S((1, C), f32)), grid=(L // tm,),
                          in_specs=[_rows(tm, C), _const((1, C)), _const((1, C)), _rows(tm, C)],
                          out_specs=(_rows(tm, C), _const((1, C)), _const((1, C))),
                          compiler_params=_cp("arbitrary"))(c, ln_g, ln_b, ds)


def _cf_glu_bwd(u, dc, dw_w, *, name):
    L, C2 = u.shape
    C = C2 // 2
    K, H = CF_KERNEL, _halo_rows(CF_KERNEL)
    tm = _pick(L, 256)
    nb = L // tm

    rs = min(ROW_SUB, tm)
    nl = C // LANE

    fold = _fold_rows

    def body(uh_ref, u_ref, dc_ref, dch_ref, w_ref, du_ref, dw_ref, db_ref, dus_ref, u_scr, dc_scr):
        i = pl.program_id(0)

        @pl.when(i == 0)
        def _():
            dw_ref[...] = jnp.zeros_like(dw_ref)
            db_ref[...] = jnp.zeros_like(db_ref)
            dus_ref[...] = jnp.zeros_like(dus_ref)

        u_scr[0:H] = jnp.where(i > 0, uh_ref[...], 0.0)
        u_scr[H:] = u_ref[...]
        dc_scr[0:tm] = dc_ref[...]
        dc_scr[tm:] = jnp.where(i < nb - 1, dch_ref[...], 0.0)
        for j in range(nl):
            la, lg = slice(j * LANE, (j + 1) * LANE), slice((nl + j) * LANE, (nl + j + 1) * LANE)
            wj = w_ref[:, la]
            zero8 = jnp.zeros((SUBLANE, LANE), f32)
            dw_acc, db_acc, dua_acc, dug_acc = [zero8] * K, zero8, zero8, zero8
            for r0 in range(0, tm, rs):
                a_e = u_scr[r0:r0 + rs + H, la]
                sg = _sigmoid(u_scr[r0:r0 + rs + H, lg])
                dce = dc_scr[r0:r0 + rs + H, la]
                dglu = _anticausal_conv(dce, rs, wj, K)
                a_c, sg_c, dc_c = a_e[H:], sg[H:], dce[:rs]
                du_a = dglu * sg_c
                du_g = dglu * a_c * sg_c * (1.0 - sg_c)
                du_ref[r0:r0 + rs, la] = du_a.astype(bf16)
                du_ref[r0:r0 + rs, lg] = du_g.astype(bf16)
                dua_acc, dug_acc, db_acc = dua_acc + fold(du_a), dug_acc + fold(du_g), db_acc + fold(dc_c)
                views = _down_views(a_e * sg, H, rs, K)
                dw_acc = [dw_acc[k] + fold(dc_c * views[K - 1 - k]) for k in range(K)]
            dus_ref[:, la] += jnp.sum(dua_acc, axis=0, keepdims=True)
            dus_ref[:, lg] += jnp.sum(dug_acc, axis=0, keepdims=True)
            db_ref[:, la] += jnp.sum(db_acc, axis=0, keepdims=True)
            for k in range(K):
                dw_ref[k:k + 1, la] += jnp.sum(dw_acc[k], axis=0, keepdims=True)

    return pl.pallas_call(body, name=name,
                          out_shape=(S((L, C2), bf16), S((H, C), f32), S((1, C), f32), S((1, C2), f32)), grid=(nb,),
                          in_specs=[_prev_halo_spec(tm, H, C2), _rows(tm, C2), _rows(tm, C), _next_halo_spec(tm, H, C, L),
                                    _const((H, C))],
                          out_specs=(_rows(tm, C2), _const((H, C)), _const((1, C)), _const((1, C2))),
                          scratch_shapes=[pltpu.VMEM((H + tm, C2), f32), pltpu.VMEM((tm + H, C), f32)],
                          compiler_params=_cp("arbitrary"))(u, u, dc, dc, dw_w)


def _gated_norm_fwd(y, z, g, *, name):
    L, C = y.shape
    tm = _pick(L, 256)

    def body(y_ref, z_ref, g_ref, o_ref):
        o_ref[...] = _rms_val(y_ref[...] * _silu(z_ref[...]), g_ref[...]).astype(bf16)

    return pl.pallas_call(body, name=name, out_shape=S((L, C), bf16), grid=(L // tm,),
                          in_specs=[_rows(tm, C), _rows(tm, C), _const((1, C))], out_specs=_rows(tm, C),
                          compiler_params=_cp("parallel"))(y, z, g)


def _gated_norm_bwd(y, z, g, dyn, *, name):
    L, C = y.shape
    tm = _pick(L, 256)

    def body(y_ref, z_ref, g_ref, d_ref, dy_ref, dz_ref, dg_ref):
        i = pl.program_id(0)

        @pl.when(i == 0)
        def _():
            dg_ref[...] = jnp.zeros_like(dg_ref)

        y, z = y_ref[...], z_ref[...]
        sz = _silu(z)
        du, dg = _rms_bwd_val(y * sz, g_ref[...], d_ref[...].astype(f32))
        dg_ref[...] += dg
        dy_ref[...] = du * sz
        dz_ref[...] = (du * y * _dsilu(z)).astype(bf16)

    return pl.pallas_call(body, name=name, out_shape=(S((L, C), f32), S((L, C), bf16), S((1, C), f32)), grid=(L // tm,),
                          in_specs=[_rows(tm, C), _rows(tm, C), _const((1, C)), _rows(tm, C)],
                          out_specs=(_rows(tm, C), _rows(tm, C), _const((1, C))),
                          compiler_params=_cp("arbitrary"))(y, z, g, dyn)


_XA_SCALE = XA_HEAD_DIM ** -0.5


def _attn_fwd(q, kv, *, name):
    L, C = q.shape
    tm = _pick(L, 512)
    Dh = XA_HEAD_DIM

    def body(q_ref, kv_ref, o_ref):
        for h in range(XA_HEADS):
            qh = q_ref[:, h * Dh:(h + 1) * Dh]
            kh = kv_ref[:, h * Dh:(h + 1) * Dh]
            vh = kv_ref[:, C + h * Dh:C + (h + 1) * Dh]
            s = lax.dot_general(qh, kh, _DN["nt"], preferred_element_type=f32) * _XA_SCALE
            e = jnp.exp(s - jnp.max(s, axis=-1, keepdims=True))
            p = e / jnp.sum(e, axis=-1, keepdims=True)
            o_ref[:, h * Dh:(h + 1) * Dh] = jnp.dot(p.astype(bf16), vh, preferred_element_type=f32).astype(bf16)

    return pl.pallas_call(body, name=name, out_shape=S((L, C), bf16), grid=(L // tm,),
                          in_specs=[_rows(tm, C), _const((N_MEM, 2 * C))], out_specs=_rows(tm, C),
                          compiler_params=_cp("parallel"))(q, kv)


def _attn_bwd(q, kv, do, *, name):
    L, C = q.shape
    tm = _pick(L, 512)
    Dh = XA_HEAD_DIM

    def body(q_ref, kv_ref, do_ref, dq_ref, dkv_ref):
        i = pl.program_id(0)

        @pl.when(i == 0)
        def _():
            dkv_ref[...] = jnp.zeros_like(dkv_ref)

        for h in range(XA_HEADS):
            qh = q_ref[:, h * Dh:(h + 1) * Dh]
            kh = kv_ref[:, h * Dh:(h + 1) * Dh]
            vh = kv_ref[:, C + h * Dh:C + (h + 1) * Dh]
            doh = do_ref[:, h * Dh:(h + 1) * Dh]
            s = lax.dot_general(qh, kh, _DN["nt"], preferred_element_type=f32) * _XA_SCALE
            e = jnp.exp(s - jnp.max(s, axis=-1, keepdims=True))
            p = e / jnp.sum(e, axis=-1, keepdims=True)
            pb = p.astype(bf16)
            dkv_ref[:, C + h * Dh:C + (h + 1) * Dh] += lax.dot_general(pb, doh, _DN["tn"], preferred_element_type=f32)
            dp = lax.dot_general(doh, vh, _DN["nt"], preferred_element_type=f32)
            ds = (p * (dp - jnp.sum(dp * p, axis=-1, keepdims=True)) * _XA_SCALE).astype(bf16)
            dq_ref[:, h * Dh:(h + 1) * Dh] = jnp.dot(ds, kh, preferred_element_type=f32).astype(bf16)
            dkv_ref[:, h * Dh:(h + 1) * Dh] += lax.dot_general(ds, qh, _DN["tn"], preferred_element_type=f32)

    return pl.pallas_call(body, name=name, out_shape=(S((L, C), bf16), S((N_MEM, 2 * C), f32)), grid=(L // tm,),
                          in_specs=[_rows(tm, C), _const((N_MEM, 2 * C)), _rows(tm, C)],
                          out_specs=(_rows(tm, C), _const((N_MEM, 2 * C))),
                          compiler_params=_cp("arbitrary"))(q, kv, do)


Q = CHUNK
PAIRS = HEADS_PER_GROUP // 2


def _split(x, pieces):
    out = []
    for _ in range(pieces - 1):
        p = x.astype(bf16)
        out.append(p)
        x = x - p.astype(f32)
    return out + [x.astype(bf16)]


def _sel_right(x, sel, mode="nn", pieces=2):
    return sum(lax.dot_general(p, sel, _DN[mode], preferred_element_type=f32) for p in _split(x, pieces))


def _sel_left(sel, x, pieces=3):
    return sum(lax.dot_general(sel, p, _DN["nn"], preferred_element_type=f32) for p in _split(x, pieces))


def _ssd_common(dt_ref, hp_ref):
    dt_pre = dt_ref[...] + hp_ref[0:1, :]
    dt = _softplus(dt_pre)
    A = -jnp.exp(hp_ref[1:2, :])
    a = dt * A
    row = lax.broadcasted_iota(jnp.int32, (Q, Q), 0)
    col = lax.broadcasted_iota(jnp.int32, (Q, Q), 1)
    tri = row >= col
    cs = _sel_left(tri.astype(bf16), a)
    T = cs[Q - 1:Q, :]
    return dict(dt_pre=dt_pre, dt=dt, A=A, cs=cs, csT=cs.T, T=T, ecs=jnp.exp(cs), eend=jnp.exp(T - cs), eT=jnp.exp(T),
                tri=tri, row=row, col=col)


def _pair_expand(v, hA, lo):
    return jnp.where(lo, v[:, hA:hA + 1], v[:, hA + 1:hA + 2])


def _decay(cm, h):
    seg = cm["cs"][:, h:h + 1] - cm["csT"][h:h + 1, :]
    return jnp.where(cm["tri"], jnp.exp(jnp.where(cm["tri"], seg, 0.0)), 0.0)


def _decay_t(cm, h):
    keep = cm["row"] <= cm["col"]
    seg = cm["csT"][h:h + 1, :] - cm["cs"][:, h:h + 1]
    return jnp.where(keep, jnp.exp(jnp.where(keep, seg, 0.0)), 0.0)


ALL_PAIRS = N_SSM_HEADS // 2
GN = N_GROUPS * D_STATE


def _ssd_fwd(act, dtp, hp, *, name):
    L = act.shape[0]
    nc = L // Q

    def body(xs_ref, b_ref, c_ref, dt_ref, hp_ref, y_ref, hs_ref, h_scr):
        c = pl.program_id(0)

        @pl.when(c == 0)
        def _():
            h_scr[...] = jnp.zeros_like(h_scr)

        cm = _ssd_common(dt_ref, hp_ref)
        lo = lax.broadcasted_iota(jnp.int32, (Q, LANE), 1) < HEAD_DIM
        top = lax.broadcasted_iota(jnp.int32, (LANE, LANE), 0) < HEAD_DIM
        Drow = hp_ref[2:3, :]
        for g in range(N_GROUPS):
            Bb = b_ref[:, g * D_STATE:(g + 1) * D_STATE].astype(bf16)
            Cb = c_ref[:, g * D_STATE:(g + 1) * D_STATE].astype(bf16)
            CB = lax.dot_general(Cb, Bb, _DN["nt"], preferred_element_type=f32)
            for jj in range(PAIRS):
                p = g * PAIRS + jj
                hA, hB = 2 * p, 2 * p + 1
                dtx, ecsx, eendx = (_pair_expand(cm[k], hA, lo) for k in ("dt", "ecs", "eend"))
                xs_p = xs_ref[:, p * LANE:(p + 1) * LANE]
                Xd = xs_p * dtx
                Y = None
                for h, Xm in ((hA, jnp.where(lo, Xd, 0.0)), (hB, jnp.where(lo, 0.0, Xd))):
                    W = (CB * _decay(cm, h)).astype(bf16)
                    t = jnp.dot(W, Xm.astype(bf16), preferred_element_type=f32)
                    Y = t if Y is None else Y + t
                Hp = h_scr[p]
                hs_ref[0, p] = Hp
                Yoff = lax.dot_general(Cb, Hp.astype(bf16), _DN["nt"], preferred_element_type=f32) * ecsx
                Dx = jnp.where(lo[0:1, :], Drow[:, hA:hA + 1], Drow[:, hB:hB + 1])
                y_ref[:, p * LANE:(p + 1) * LANE] = Y + Yoff + xs_p * Dx
                Snew = lax.dot_general((Xd * eendx).astype(bf16), Bb, _DN["tn"], preferred_element_type=f32)
                eTx = jnp.where(top, cm["eT"][:, hA:hA + 1], cm["eT"][:, hB:hB + 1])
                h_scr[p] = Hp * eTx + Snew

    return pl.pallas_call(
        body, name=name, out_shape=(S((L, D_INNER), f32), S((nc, ALL_PAIRS, LANE, D_STATE), f32)),
        grid=(nc,),
        in_specs=[pl.BlockSpec((Q, D_INNER), lambda c: (c, 0)),
                  pl.BlockSpec((Q, GN), lambda c: (c, D_INNER // GN)),
                  pl.BlockSpec((Q, GN), lambda c: (c, D_INNER // GN + 1)),
                  pl.BlockSpec((Q, LANE), lambda c: (c, 0)),
                  pl.BlockSpec((SUBLANE, LANE), lambda c: (0, 0))],
        out_specs=(pl.BlockSpec((Q, D_INNER), lambda c: (c, 0)),
                   pl.BlockSpec((1, ALL_PAIRS, LANE, D_STATE), lambda c: (c, 0, 0, 0))),
        scratch_shapes=[pltpu.VMEM((ALL_PAIRS, LANE, D_STATE), f32)],
        compiler_params=_cp("arbitrary"))(act, act, act, dtp, hp)


def _ssd_bwd(act, dtp, hp, dy, hs, *, name):
    L = act.shape[0]
    nc = L // Q

    def body(xs_ref, b_ref, c_ref, dt_ref, hp_ref, dy_ref, hs_ref, dxs_ref, db_ref, dc_ref, ddt_ref, dhp_ref, dh_scr):
        c = pl.program_id(0)

        @pl.when(c == 0)
        def _():
            dh_scr[...] = jnp.zeros_like(dh_scr)
            dhp_ref[...] = jnp.zeros_like(dhp_ref)

        cm = _ssd_common(dt_ref, hp_ref)
        lane = lax.broadcasted_iota(jnp.int32, (Q, LANE), 1)
        sub = lax.broadcasted_iota(jnp.int32, (LANE, LANE), 0)
        lo = lane < HEAD_DIM
        top = sub < HEAD_DIM
        Drow = hp_ref[2:3, :]
        zero = jnp.zeros((Q, LANE), f32)
        dcs, dcsT, ddtx = zero, zero, zero
        dD_row = jnp.zeros((1, LANE), f32)
        dT_row = jnp.zeros((1, LANE), f32)
        for g in range(N_GROUPS):
            Bb = b_ref[:, g * D_STATE:(g + 1) * D_STATE].astype(bf16)
            Cb = c_ref[:, g * D_STATE:(g + 1) * D_STATE].astype(bf16)
            CB = lax.dot_general(Cb, Bb, _DN["nt"], preferred_element_type=f32)
            CBT = lax.dot_general(Bb, Cb, _DN["nt"], preferred_element_type=f32)
            dC, dB, dCB = zero, zero, jnp.zeros((Q, Q), f32)
            for jj in range(PAIRS):
                p = g * PAIRS + jj
                hA, hB = 2 * p, 2 * p + 1
                Pj = (lane == jnp.where(top, hA, hB)).astype(bf16)
                dtx, ecsx, eendx = (_pair_expand(cm[k], hA, lo) for k in ("dt", "ecs", "eend"))
                xs_p = xs_ref[:, p * LANE:(p + 1) * LANE]
                dY_p = dy_ref[:, p * LANE:(p + 1) * LANE]
                Xd = xs_p * dtx
                Xdb = Xd.astype(bf16)
                Hp, dHn = hs_ref[0, p], dh_scr[p]
                Hb, dHb = Hp.astype(bf16), dHn.astype(bf16)
                EdYb = (dY_p * ecsx).astype(bf16)
                YoffN = lax.dot_general(Cb, Hb, _DN["nt"], preferred_element_type=f32)
                dC = dC + jnp.dot(EdYb, Hb, preferred_element_type=f32)
                dH_off = lax.dot_general(EdYb, Cb, _DN["tn"], preferred_element_type=f32)
                R = lax.dot_general(Bb, dHb, _DN["nt"], preferred_element_type=f32)
                Xe = Xd * eendx
                dB = dB + jnp.dot(Xe.astype(bf16), dHb, preferred_element_type=f32)
                dXd = R * eendx
                RXe = R * Xe
                dcs = dcs + _sel_right(dY_p * YoffN * ecsx - RXe, Pj)
                HH = dHn * Hp
                hh = [jnp.sum(jnp.sum(HH[r0:r0 + HEAD_DIM], axis=0, keepdims=True), axis=1, keepdims=True)
                      for r0 in (0, HEAD_DIM)]
                rxe_cols = jnp.broadcast_to(jnp.sum(RXe, axis=0, keepdims=True), (SUBLANE, LANE))
                dT_row = dT_row + _sel_right(rxe_cols, Pj, pieces=3)[0:1] \
                    + (jnp.where(lane[0:1] == hA, hh[0], 0.0) + jnp.where(lane[0:1] == hB, hh[1], 0.0)) * cm["eT"]
                for h, keep in ((hA, lo), (hB, jnp.logical_not(lo))):
                    M = _decay(cm, h)
                    Wf = CB * M
                    dYm = jnp.where(keep, dY_p, 0.0).astype(bf16)
                    dW = lax.dot_general(dYm, Xdb, _DN["nt"], preferred_element_type=f32)
                    WT = (CBT * _decay_t(cm, h)).astype(bf16)
                    dXd = dXd + jnp.dot(WT, dYm, preferred_element_type=f32)
                    Z = dW * Wf
                    dcs = dcs + _sel_right(Z, (lane == h).astype(bf16))
                    dcsT = dcsT + jnp.where(sub == h, jnp.sum(Z, axis=0, keepdims=True), 0.0)
                    dCB = dCB + dW * M
                Dx = jnp.where(lo[0:1, :], Drow[:, hA:hA + 1], Drow[:, hB:hB + 1])
                dxs_ref[:, p * LANE:(p + 1) * LANE] = dXd * dtx + dY_p * Dx
                ddtx = ddtx + _sel_right(dXd * xs_p, Pj)
                dD_cols = jnp.broadcast_to(jnp.sum(dY_p * xs_p, axis=0, keepdims=True), (SUBLANE, LANE))
                dD_row = dD_row + _sel_right(dD_cols, Pj, pieces=3)[0:1]
                eTx = jnp.where(top, cm["eT"][:, hA:hA + 1], cm["eT"][:, hB:hB + 1])
                dh_scr[p] = dHn * eTx + dH_off
            dCBb = dCB.astype(bf16)
            dc_ref[:, g * D_STATE:(g + 1) * D_STATE] = dC + jnp.dot(dCBb, Bb, preferred_element_type=f32)
            db_ref[:, g * D_STATE:(g + 1) * D_STATE] = dB + lax.dot_general(dCBb, Cb, _DN["tn"], preferred_element_type=f32)
        dcs = dcs - dcsT.T + jnp.where(lax.broadcasted_iota(jnp.int32, (Q, LANE), 0) == Q - 1, dT_row, 0.0)
        da = _sel_left((cm["row"] <= cm["col"]).astype(bf16), dcs)
        ddt_pre = (da * cm["A"] + ddtx) * _sigmoid(cm["dt_pre"])
        ddt_ref[...] = ddt_pre
        r8 = lax.broadcasted_iota(jnp.int32, (SUBLANE, LANE), 0)
        dhp_ref[...] += jnp.where(r8 == 0, jnp.sum(ddt_pre, axis=0, keepdims=True),
                                  jnp.where(r8 == 1, jnp.sum(da * cm["dt"], axis=0, keepdims=True) * cm["A"],
                                            jnp.where(r8 == 2, dD_row, 0.0)))

    rev = lambda c: nc - 1 - c
    return pl.pallas_call(
        body, name=name,
        out_shape=(S((L, D_INNER), f32), S((L, GN), f32), S((L, GN), f32), S((L, LANE), f32), S((SUBLANE, LANE), f32)),
        grid=(nc,),
        in_specs=[pl.BlockSpec((Q, D_INNER), lambda c: (rev(c), 0)),
                  pl.BlockSpec((Q, GN), lambda c: (rev(c), D_INNER // GN)),
                  pl.BlockSpec((Q, GN), lambda c: (rev(c), D_INNER // GN + 1)),
                  pl.BlockSpec((Q, LANE), lambda c: (rev(c), 0)),
                  pl.BlockSpec((SUBLANE, LANE), lambda c: (0, 0)),
                  pl.BlockSpec((Q, D_INNER), lambda c: (rev(c), 0)),
                  pl.BlockSpec((1, ALL_PAIRS, LANE, D_STATE), lambda c: (rev(c), 0, 0, 0))],
        out_specs=(pl.BlockSpec((Q, D_INNER), lambda c: (rev(c), 0)),
                   pl.BlockSpec((Q, GN), lambda c: (rev(c), 0)),
                   pl.BlockSpec((Q, GN), lambda c: (rev(c), 0)),
                   pl.BlockSpec((Q, LANE), lambda c: (rev(c), 0)),
                   pl.BlockSpec((SUBLANE, LANE), lambda c: (0, 0))),
        scratch_shapes=[pltpu.VMEM((ALL_PAIRS, LANE, D_STATE), f32)],
        compiler_params=_cp("arbitrary"))(act, act, act, dtp, hp, dy, hs)


def _group_pad_cols(w):
    return jnp.pad(w, [(0, 0)] * (w.ndim - 1) + [(0, LANE - N_SSM_HEADS)])


def _group_unpad_cols(w):
    return w[..., :N_SSM_HEADS]


def _row(v):
    return v.reshape(1, -1)


ROW_SHARDED = ('ssm_out_w', 'cf_pw2_w', 'xa_q_w', 'xa_o_w', 'ffn_out_w')
COL_SHARDED = ('cf_pw1_w', 'xa_kv_w', 'ffn_in_w')


MIXER_WEIGHTS = ('ssm_in_w', 'ssm_out_w', 'cf_pw1_w', 'cf_pw2_w')


def _layer_matmul_weights(i, part):
    if part == "rest":
        return ('xa_q_w', 'xa_kv_w', 'xa_o_w', 'ffn_in_w', 'ffn_out_w')
    mixer = ('ssm_in_w', 'ssm_out_w') if i % 2 == 0 else ('cf_pw1_w', 'cf_pw2_w')
    return {"mixer": mixer, "mixer_in": mixer[:1], "mixer_out": mixer[1:]}[part]


def _device_step(x, mem, target, W, layer_weights, layer_grads, start_after=()):
    ng = W['norm_g']
    lw = []
    for i in range(DEPTH):
        j = i // 2
        p = {}
        if i % 2 == 0:
            p['cw'] = _pad_taps(W['ssm_conv_w'][j], SSM_CONV)
            p['cb'] = _row(W['ssm_conv_b'][j])
            hp = jnp.stack([_group_pad_cols(W['ssm_dt_bias'][j]), _group_pad_cols(W['ssm_A_log'][j]),
                            _group_pad_cols(W['ssm_D'][j])])
            p['hp'] = jnp.pad(hp, ((0, SUBLANE - 3), (0, 0)))
            p['sng'] = _row(W['ssm_norm_g'][j])
        else:
            p['pw1b'] = _row(W['cf_pw1_b'][j])
            p['dww'], p['dwb'] = _pad_taps(W['cf_dw_w'][j], CF_KERNEL), _row(W['cf_dw_b'][j])
            p['lng'], p['lnb'] = _row(W['cf_ln_g'][j]), _row(W['cf_ln_b'][j])
            p['pw2b'] = _row(W['cf_pw2_b'][j])
        p['memg'] = _row(W['xa_mem_g'][i])
        p['fcw'], p['fcb'] = _pad_taps(W['ffn_conv_w'][i], FFN_CONV), _row(W['ffn_conv_b'][i])
        p['g'] = [_row(ng[i, s]) for s in range(6)]
        lw.append(p)

    def wmm(a, wl, wname, mode, **kw):
        return _mm(a, wl[wname], mode, b_shards=wname in COL_SHARDED, **kw)

    saved = []
    X = x
    h = _rmsnorm_fwd(X, lw[0]['g'][0], name="norm_in", after=start_after)
    for i in range(DEPTH):
        p, sv = lw[i], {}
        wl = dict(layer_weights(i, "mixer_in", X))
        sv['X0'], sv['h'], sv['wl'] = X, h, wl
        if i % 2 == 0:
            win = jnp.concatenate([wl['ssm_in_w'][s] for s in range(N_CHIPS)], axis=1)
            wl['wz'], wl['wx'] = win[:, :D_INNER], win[:, D_INNER:D_INNER + CONV_DIM]
            wl['wdt'] = _group_pad_cols(win[:, D_INNER + CONV_DIM:])
            z = _mm(h, wl['wz'], "nn", name="ssm_z")
            xbc = _mm(h, wl['wx'], "nn", name="ssm_xbc")
            dtp = _mm(h, wl['wdt'], "nn", name="ssm_dt")
            act = _conv_act_fwd(xbc, p['cw'], p['cb'], K=SSM_CONV, act="silu", name="ssm_conv_fwd", out_dtype=f32)
            y, hs = _ssd_fwd(act, dtp, p['hp'], name="ssd_fwd")
            yn = _gated_norm_fwd(y, z, p['sng'], name="ssm_gnorm_fwd")
            wl.update(layer_weights(i, "mixer_out", yn))
            mix = wmm(yn, wl, 'ssm_out_w', "nn", name="ssm_out")
            sv.update(z=z, xbc=xbc, dtp=dtp, act=act, y=y, hs=hs, yn=yn)
        else:
            u = wmm(h, wl, 'cf_pw1_w', "nn", name="cf_pw1", bias=p['pw1b'])
            c, s = _cf_fwd(u, p['dww'], p['dwb'], p['lng'], p['lnb'], name="cf_conv_fwd")
            wl.update(layer_weights(i, "mixer_out", s))
            mix = wmm(s, wl, 'cf_pw2_w', "nn", name="cf_pw2", bias=p['pw2b'])
            sv.update(u=u, c=c, s=s)
        wl.update(layer_weights(i, "rest", mix))
        X1, h2 = _resid_norm_fwd(X, mix, p['g'][1], p['g'][2], name="resid_norm_a")
        q = wmm(h2, wl, 'xa_q_w', "nn", name="xa_q", out_dtype=bf16)
        m = _rmsnorm_fwd(mem, p['memg'], name="xa_mem_norm")
        kv = wmm(m, wl, 'xa_kv_w', "nn", name="xa_kv", out_dtype=bf16)
        o = _attn_fwd(q, kv, name="xa_attn_fwd")
        a = wmm(o, wl, 'xa_o_w', "nn", name="xa_o")
        X2, h3 = _resid_norm_fwd(X1, a, p['g'][3], p['g'][4], name="resid_norm_b")
        u0 = wmm(h3, wl, 'ffn_in_w', "nn", name="ffn_in")
        fact = _conv_act_fwd(u0, p['fcw'], p['fcb'], K=FFN_CONV, act="swiglu", name="ffn_conv_fwd", out_dtype=bf16)
        f = wmm(fact, wl, 'ffn_out_w', "nn", name="ffn_out")
        g_next = lw[i + 1]['g'][0] if i + 1 < DEPTH else None
        X3, hn = _resid_norm_fwd(X2, f, p['g'][5], g_next, name="resid_norm_c" if g_next is not None else "resid_norm_last")
        sv.update(mix=mix, X1=X1, h2=h2, q=q, m=m, kv=kv, o=o, a=a, X2=X2, h3=h3, u0=u0, fact=fact, f=f)
        saved.append(sv)
        X, h = X3, hn

    sse, G = _loss_fwd_bwd(X, target, name="loss")

    small = [n for n in WEIGHT_NAMES if n not in MATMUL_WEIGHTS]
    gr = {n: [None] * W[n].shape[0] for n in small}

    def dwmm(gl, a, d, wname, *, name):
        if wname in COL_SHARDED:
            gl[wname] = _mm(a, d, "tn", name=name, out_dtype=bf16, out_shards=True)
        else:
            g = _mm(a, d, "tn", name=name, out_dtype=bf16)
            gl[wname] = g.reshape(N_CHIPS, g.shape[0] // N_CHIPS, g.shape[1])

    dng = [[None] * 6 for _ in range(DEPTH)]
    df = None
    for i in reversed(range(DEPTH)):
        p, sv, j = lw[i], saved[i], i // 2
        wl, gl = sv['wl'], {}
        if df is None:
            df, dng[i][5], _ = _norm_bwd(sv['f'], p['g'][5], G, name="nb_f", out_dtype=bf16)
        dwmm(gl, sv['fact'], df, 'ffn_out_w', name="ffn_out_dw")
        dfact = wmm(df, wl, 'ffn_out_w', "nt", name="ffn_out_dx")
        du0, dcw, dcb = _conv_act_bwd(sv['u0'], [dfact], p['fcw'], p['fcb'], K=FFN_CONV, act="swiglu", name="ffn_conv_bwd")
        gr['ffn_conv_w'][i], gr['ffn_conv_b'][i] = dcw[:FFN_CONV], dcb[0]
        dwmm(gl, sv['h3'], du0, 'ffn_in_w', name="ffn_in_dw")
        dh3 = wmm(du0, wl, 'ffn_in_w', "nt", name="ffn_in_dx")
        G, dng[i][4], da, dng[i][3], _ = _norm_bwd_chain(sv['X2'], p['g'][4], dh3, G, sv['a'], p['g'][3], name="nb_x2_a")
        dwmm(gl, sv['o'], da, 'xa_o_w', name="xa_o_dw")
        do = wmm(da, wl, 'xa_o_w', "nt", name="xa_o_dx", out_dtype=bf16)
        dq, dkv = _attn_bwd(sv['q'], sv['kv'], do, name="xa_attn_bwd")
        dwmm(gl, sv['h2'], dq, 'xa_q_w', name="xa_q_dw")
        dh2 = wmm(dq, wl, 'xa_q_w', "nt", name="xa_q_dx")
        dwmm(gl, sv['m'], dkv, 'xa_kv_w', name="xa_kv_dw")
        dm = wmm(dkv, wl, 'xa_kv_w', "nt", name="xa_kv_dx")
        _, dmg, _ = _norm_bwd(mem, p['memg'], dm, name="nb_mem")
        gr['xa_mem_g'][i] = dmg[0]
        behind = tuple(layer_grads(i, "rest", gl))
        G, dng[i][2], dmix, dng[i][1], dmix_sum = _norm_bwd_chain(sv['X1'], p['g'][2], dh2, G, sv['mix'], p['g'][1],
                                                                  name="nb_x1_mix", after=behind)
        if i % 2 == 0:
            dwmm(gl, sv['yn'], dmix, 'ssm_out_w', name="ssm_out_dw")
            dyn = wmm(dmix, wl, 'ssm_out_w', "nt", name="ssm_out_dx")
            dy, dz, dsng = _gated_norm_bwd(sv['y'], sv['z'], p['sng'], dyn, name="ssm_gnorm_bwd")
            gr['ssm_norm_g'][j] = dsng[0]
            dxs, dB, dC, ddtp, dhp = _ssd_bwd(sv['act'], sv['dtp'], p['hp'], dy, sv['hs'], name="ssd_bwd")
            gr['ssm_dt_bias'][j], gr['ssm_A_log'][j], gr['ssm_D'][j] = (_group_unpad_cols(dhp[r]) for r in range(3))
            dxbc, dcw, dcb = _conv_act_bwd(sv['xbc'], [dxs, dB, dC], p['cw'], p['cb'], K=SSM_CONV, act="silu",
                                           name="ssm_conv_bwd")
            gr['ssm_conv_w'][j], gr['ssm_conv_b'][j] = dcw[:SSM_CONV], dcb[0]
            hh = sv['h']
            dwz = _mm(dz, hh, "tn", name="ssm_z_dw", out_dtype=bf16)
            dwx = _mm(dxbc, hh, "tn", name="ssm_xbc_dw", out_dtype=bf16)
            dwdt = _mm(ddtp, hh, "tn", name="ssm_dt_dw", out_dtype=bf16)
            din = jnp.concatenate([dwz, dwx, dwdt[:N_SSM_HEADS]], axis=0)
            gl['ssm_in_w'] = din.reshape(N_CHIPS, din.shape[0] // N_CHIPS, din.shape[1])
            behind = tuple(layer_grads(i, "mixer", gl))
            dh = _mm(dz, wl['wz'], "nt", name="ssm_z_dx", after=behind)
            dh = _mm(dxbc, wl['wx'], "nt", name="ssm_xbc_dx", add=dh)
            dh = _mm(ddtp, wl['wdt'], "nt", name="ssm_dt_dx", add=dh)
        else:
            dwmm(gl, sv['s'], dmix, 'cf_pw2_w', name="cf_pw2_dw")
            gr['cf_pw2_b'][j] = dmix_sum[0]
            ds = wmm(dmix, wl, 'cf_pw2_w', "nt", name="cf_pw2_dx")
            dc, dlg, dlb = _cf_ln_bwd(sv['c'], p['lng'], p['lnb'], ds, name="cf_ln_bwd")
            gr['cf_ln_g'][j], gr['cf_ln_b'][j] = dlg[0], dlb[0]
            du, ddw, ddb, dus = _cf_glu_bwd(sv['u'], dc, p['dww'], name="cf_glu_bwd")
            gr['cf_dw_w'][j], gr['cf_dw_b'][j], gr['cf_pw1_b'][j] = ddw[:CF_KERNEL], ddb[0], dus[0]
            dwmm(gl, sv['h'], du, 'cf_pw1_w', name="cf_pw1_dw")
            behind = tuple(layer_grads(i, "mixer", gl))
            dh = wmm(du, wl, 'cf_pw1_w', "nt", name="cf_pw1_dx", after=behind)
        if i > 0:
            G, dng[i][0], df, dng[i - 1][5], _ = _norm_bwd_chain(sv['X0'], p['g'][0], dh, G, saved[i - 1]['f'],
                                                                 lw[i - 1]['g'][5], name="nb_x0_f")
        else:
            G, dng[i][0], _ = _norm_bwd(sv['X0'], p['g'][0], dh, name="nb_x0", add=G)
    gr['norm_g'] = [jnp.concatenate(dng[i], axis=0) for i in range(DEPTH)]
    gsmall = {n: jnp.stack(gr[n]) for n in small}
    return sse, G, gsmall


MESH = pl.DeviceIdType.MESH
HBM_SPEC = pl.BlockSpec(memory_space=pltpu.HBM)


def _chip_peers(x, y):
    return [(1 - x, y), (x, 1 - y), (1 - x, 1 - y)]


def _all_gather_chips(buf, *, name):
    R, C = buf.shape

    def body(in_ref, out_ref, send_sems, recv_sems, local_sem):
        x, y, c = lax.axis_index("x"), lax.axis_index("y"), lax.axis_index("c")
        me = 2 * x + y
        mine = pltpu.make_async_copy(in_ref, out_ref.at[me], local_sem)
        mine.start()
        peers = _chip_peers(x, y)
        sends = []
        for k, (px, py) in enumerate(peers):
            cp = pltpu.make_async_remote_copy(src_ref=in_ref, dst_ref=out_ref.at[me], send_sem=send_sems.at[k],
                                              recv_sem=recv_sems.at[k], device_id=(px, py, c), device_id_type=MESH)
            cp.start()
            sends.append(cp)
        for k, (px, py) in enumerate(peers):
            pltpu.make_async_remote_copy(src_ref=in_ref, dst_ref=out_ref.at[2 * px + py], send_sem=send_sems.at[k],
                                         recv_sem=recv_sems.at[k], device_id=(px, py, c), device_id_type=MESH).wait_recv()
        for cp in sends:
            cp.wait_send()
        mine.wait()

    return pl.pallas_call(body, name=name, out_shape=S((N_CHIPS, R, C), buf.dtype), in_specs=[HBM_SPEC], out_specs=HBM_SPEC,
                          scratch_shapes=[pltpu.SemaphoreType.DMA((3,)), pltpu.SemaphoreType.DMA((3,)),
                                          pltpu.SemaphoreType.DMA(())])(buf)


def _remote(src, dst, send_sem, recv_sem, device):
    return pltpu.make_async_remote_copy(src_ref=src, dst_ref=dst, send_sem=send_sem, recv_sem=recv_sem,
                                        device_id=device, device_id_type=MESH)


def _gather_matmul_weights(shards, *, name):
    n = len(shards)

    def body(*refs):
        ins, outs = refs[:n], refs[n:2 * n]
        send, recv, fsend, frecv, lsem = refs[2 * n:]
        x, y, c = lax.axis_index("x"), lax.axis_index("y"), lax.axis_index("c")
        me, sib = 2 * x + y, (x, y, 1 - c)
        peers = _chip_peers(x, y)
        started, local = [], []
        for w in range(n):
            cp = pltpu.make_async_copy(ins[w], outs[w].at[:, me], lsem.at[w])
            cp.start()
            local.append(cp)
            for k, (px, py) in enumerate(peers):
                cp = _remote(ins[w].at[:, c], outs[w].at[:, me, c], send.at[w, k], recv.at[w, k], (px, py, c))
                cp.start()
                started.append(cp)
        for w in range(n):
            for k, (px, py) in enumerate(peers):
                landed = outs[w].at[:, 2 * px + py, c]
                _remote(ins[w].at[:, c], landed, send.at[w, k], recv.at[w, k], (px, py, c)).wait_recv()
                cp = _remote(landed, landed, fsend.at[w, k], frecv.at[w, k], sib)
                cp.start()
                started.append(cp)
        for w in range(n):
            for k, (px, py) in enumerate(peers):
                _remote(ins[w].at[:, c], outs[w].at[:, 2 * px + py, 1 - c], fsend.at[w, k], frecv.at[w, k], sib).wait_recv()
        for cp in started:
            cp.wait_send()
        for cp in local:
            cp.wait()

    out_shape = tuple(S((s.shape[0], N_CHIPS) + s.shape[1:], s.dtype) for s in shards)
    sems = [pltpu.SemaphoreType.DMA((n, 3)) for _ in range(4)] + [pltpu.SemaphoreType.DMA((n,))]
    return pl.pallas_call(body, name=name, out_shape=out_shape, in_specs=[HBM_SPEC] * n, out_specs=(HBM_SPEC,) * n,
                          scratch_shapes=sems)(*shards)


SEM_SPEC = pl.BlockSpec(memory_space=pltpu.SEMAPHORE)
VMEM_SPEC = pl.BlockSpec(memory_space=pltpu.VMEM)


def _in_hbm(a):
    return pltpu.with_memory_space_constraint(a, pltpu.HBM)


def _chip_targets(x, y):
    return [(x, y), (1 - x, y), (x, 1 - y), (1 - x, 1 - y)]


def _spread_start(srcs, scatter, *, name, after=()):
    n = len(srcs)
    lands = [lax.empty((N_CHIPS,) + (s.shape[1:] if scatter else s.shape), s.dtype) for s in srcs]

    def body(*refs):
        src, land = refs[:n], refs[n:2 * n]
        send, recv, token = refs[2 * n + len(after)], refs[2 * n + len(after) + 1], refs[-1]
        x, y, c = lax.axis_index("x"), lax.axis_index("y"), lax.axis_index("c")
        me = 2 * x + y
        for w in range(n):
            for k, (px, py) in enumerate(_chip_targets(x, y)):
                block = src[w].at[2 * px + py] if scatter else src[w]
                _remote(block, land[w].at[me], send.at[N_CHIPS * w + k], recv.at[N_CHIPS * w + k], (px, py, c)).start()
        token[...] = jnp.zeros_like(token)

    thru = tuple(pltpu.HBM(a.shape, a.dtype) for a in list(srcs) + lands)
    sems = (pltpu.SemaphoreType.DMA((N_CHIPS * n,)), pltpu.SemaphoreType.DMA((N_CHIPS * n,)))
    out = pl.pallas_call(
        body, name=name, out_shape=sems + thru + (S((SUBLANE, LANE), f32),),
        in_specs=[HBM_SPEC] * (2 * n) + [ANY_SPEC] * len(after),
        out_specs=(SEM_SPEC, SEM_SPEC) + (HBM_SPEC,) * (2 * n) + (VMEM_SPEC,),
        input_output_aliases={i: 2 + i for i in range(2 * n)},
        compiler_params=pltpu.CompilerParams(has_side_effects=pltpu.SideEffectType.DATAFLOW_SIDE_EFFECTING),
    )(*[_in_hbm(a) for a in list(srcs) + lands], *after)
    return out[0], out[1], out[2:2 + n], out[2 + n:2 + 2 * n], out[-1]


def _spread_wait(send, recv, srcs, lands, after, scatter, *, name):
    n = len(srcs)

    def body(*refs):
        src, land, send, recv = refs[:n], refs[n:2 * n], refs[2 * n], refs[2 * n + 1]
        x, y, c = lax.axis_index("x"), lax.axis_index("y"), lax.axis_index("c")
        me = 2 * x + y
        for w in range(n):
            for k, (px, py) in enumerate(_chip_targets(x, y)):
                block = src[w].at[me] if scatter else src[w]
                cp = _remote(block, land[w].at[2 * px + py], send.at[N_CHIPS * w + k], recv.at[N_CHIPS * w + k], (px, py, c))
                cp.wait_send()
                cp.wait_recv()

    thru = tuple(pltpu.HBM(a.shape, a.dtype) for a in list(srcs) + list(lands))
    out = pl.pallas_call(
        body, name=name, out_shape=thru,
        in_specs=[HBM_SPEC] * (2 * n) + [SEM_SPEC, SEM_SPEC] + [ANY_SPEC] * len(after), out_specs=(HBM_SPEC,) * (2 * n),
        input_output_aliases={i: i for i in range(2 * n)},
        compiler_params=pltpu.CompilerParams(has_side_effects=pltpu.SideEffectType.DATAFLOW_SIDE_EFFECTING),
    )(*srcs, *lands, send, recv, *after)
    return out[:n], out[n:]


def _swap_start(bufs, *, name):
    n = len(bufs)
    lands = [lax.empty(b.shape, b.dtype) for b in bufs]

    def body(*refs):
        src, land, send, recv, token = refs[:n], refs[n:2 * n], refs[2 * n], refs[2 * n + 1], refs[-1]
        sib = (lax.axis_index("x"), lax.axis_index("y"), 1 - lax.axis_index("c"))
        for w in range(n):
            _remote(src[w], land[w], send.at[w], recv.at[w], sib).start()
        token[...] = jnp.zeros_like(token)

    thru = tuple(pltpu.HBM(a.shape, a.dtype) for a in list(bufs) + lands)
    out = pl.pallas_call(
        body, name=name,
        out_shape=(pltpu.SemaphoreType.DMA((n,)), pltpu.SemaphoreType.DMA((n,))) + thru + (S((SUBLANE, LANE), f32),),
        in_specs=[HBM_SPEC] * (2 * n), out_specs=(SEM_SPEC, SEM_SPEC) + (HBM_SPEC,) * (2 * n) + (VMEM_SPEC,),
        input_output_aliases={i: 2 + i for i in range(2 * n)},
        compiler_params=pltpu.CompilerParams(has_side_effects=pltpu.SideEffectType.DATAFLOW_SIDE_EFFECTING),
    )(*[_in_hbm(a) for a in list(bufs) + lands])
    return out[0], out[1], out[2:2 + n], out[2 + n:2 + 2 * n], out[-1]


def _swap_wait(send, recv, bufs, lands, after, *, name):
    n = len(bufs)

    def body(*refs):
        src, land, send, recv = refs[:n], refs[n:2 * n], refs[2 * n], refs[2 * n + 1]
        sib = (lax.axis_index("x"), lax.axis_index("y"), 1 - lax.axis_index("c"))
        for w in range(n):
            cp = _remote(src[w], land[w], send.at[w], recv.at[w], sib)
            cp.wait_send()
            cp.wait_recv()

    thru = tuple(pltpu.HBM(a.shape, a.dtype) for a in list(bufs) + list(lands))
    out = pl.pallas_call(
        body, name=name, out_shape=thru,
        in_specs=[HBM_SPEC] * (2 * n) + [SEM_SPEC, SEM_SPEC] + [ANY_SPEC] * len(after), out_specs=(HBM_SPEC,) * (2 * n),
        input_output_aliases={i: i for i in range(2 * n)},
        compiler_params=pltpu.CompilerParams(has_side_effects=pltpu.SideEffectType.DATAFLOW_SIDE_EFFECTING),
    )(*bufs, *lands, send, recv, *after)
    return out[:n], out[n:]


N_DEVICES = 8


def _device_targets(x, y, c):
    flips = [(d >> 2 & 1, d >> 1 & 1, d & 1) for d in range(N_DEVICES)]
    return [(1 - x if fx else x, 1 - y if fy else y, 1 - c if fc else c) for fx, fy, fc in flips]


def _allgather_devices_start(buf, *, name):
    land = lax.empty((N_DEVICES,) + buf.shape, buf.dtype)

    def body(src, land, send, recv, src_thru, land_thru, token):
        x, y, c = lax.axis_index("x"), lax.axis_index("y"), lax.axis_index("c")
        me = 4 * x + 2 * y + c
        for k, peer in enumerate(_device_targets(x, y, c)):
            _remote(src, land.at[me], send.at[k], recv.at[k], peer).start()
        token[...] = jnp.zeros_like(token)

    return pl.pallas_call(
        body, name=name,
        out_shape=(pltpu.SemaphoreType.DMA((N_DEVICES,)), pltpu.SemaphoreType.DMA((N_DEVICES,)), pltpu.HBM(buf.shape, buf.dtype),
                   pltpu.HBM(land.shape, land.dtype), S((SUBLANE, LANE), f32)),
        in_specs=[HBM_SPEC, HBM_SPEC], out_specs=(SEM_SPEC, SEM_SPEC, HBM_SPEC, HBM_SPEC, VMEM_SPEC),
        input_output_aliases={0: 2, 1: 3},
        compiler_params=pltpu.CompilerParams(has_side_effects=pltpu.SideEffectType.DATAFLOW_SIDE_EFFECTING),
    )(_in_hbm(buf), _in_hbm(land))


def _allgather_devices_wait(send, recv, buf, land, after, *, name):
    def body(src, land, send, recv, *rest):
        x, y, c = lax.axis_index("x"), lax.axis_index("y"), lax.axis_index("c")
        for k, (px, py, pc) in enumerate(_device_targets(x, y, c)):
            cp = _remote(src, land.at[4 * px + 2 * py + pc], send.at[k], recv.at[k], (px, py, pc))
            cp.wait_send()
            cp.wait_recv()

    out = pl.pallas_call(
        body, name=name, out_shape=(pltpu.HBM(buf.shape, buf.dtype), pltpu.HBM(land.shape, land.dtype)),
        in_specs=[HBM_SPEC, HBM_SPEC, SEM_SPEC, SEM_SPEC] + [ANY_SPEC] * len(after), out_specs=(HBM_SPEC, HBM_SPEC),
        input_output_aliases={0: 0, 1: 1},
        compiler_params=pltpu.CompilerParams(has_side_effects=pltpu.SideEffectType.DATAFLOW_SIDE_EFFECTING),
    )(buf, land, send, recv, *after)
    return out[1]


def _sum_slots(buf, *, name):
    ns, R, C = buf.shape
    tr = _pick(R, 512)
    assert R % tr == 0

    def body(*refs):
        acc = refs[0][...]
        for r in refs[1:ns]:
            acc = acc + r[...]
        refs[ns][...] = acc

    specs = [pl.BlockSpec((None, tr, C), functools.partial(lambda s, i: (s, i, 0), s)) for s in range(ns)]
    return pl.pallas_call(body, name=name, out_shape=S((R, C), buf.dtype), grid=(R // tr,), in_specs=specs,
                          out_specs=pl.BlockSpec((tr, C), lambda i: (i, 0)), compiler_params=_cp("parallel"))(*([buf] * ns))


ADAMW_BLOCK_BYTES = 1 << 20


def _adamw(w, m, v, groups, *, name, layer=None, prev=None):
    shape = w.shape if layer is None else w.shape[1:]
    C = shape[-1]
    Rr = math.prod(shape[:-1])
    tr, tc = Rr, C
    if Rr * C * 4 > ADAMW_BLOCK_BYTES:
        rows = [t for t in range(2 * SUBLANE, Rr + 1, 2 * SUBLANE) if Rr % t == 0 and t * C * 4 <= ADAMW_BLOCK_BYTES]
        if rows:
            tr = max(rows)
        else:
            tc = max(t for t in range(LANE, C + 1, LANE) if C % t == 0 and Rr * t * 4 <= ADAMW_BLOCK_BYTES)
    c1 = 1.0 / (1.0 - ADAM_B1 ** ADAM_STEP)
    c2 = 1.0 / (1.0 - ADAM_B2 ** ADAM_STEP)
    if layer is None:
        to2 = lambda t: t.reshape(Rr, C)
        spec = pl.BlockSpec((tr, tc), lambda i, j: (i, j))
        res_shape = S((Rr, C), f32)
    else:
        to2 = lambda t: t.reshape(layer[1], Rr, C)
        spec = pl.BlockSpec((None, tr, tc), functools.partial(lambda l, i, j: (l, i, j), layer[0]))
        res_shape = S((layer[1], Rr, C), f32)
    wspec, spec = spec, pl.BlockSpec((tr, tc), lambda i, j: (i, j))
    g_specs, g_args, sizes = [], [], []
    for grp in groups:
        sizes.append(len(grp))
        for term in grp:
            if isinstance(term, tuple):
                arr, slot = term
                g_specs.append(pl.BlockSpec((None, tr, tc), functools.partial(lambda s, i, j: (s, i, j), slot)))
                g_args.append(arr.reshape(arr.shape[0], Rr, C))
            else:
                g_specs.append(spec)
                g_args.append(term.reshape(Rr, C))
    nterms = len(g_args)
    prev = () if prev is None else tuple(to2(t) for t in prev)

    def body(w_ref, m_ref, v_ref, *rest):
        t_refs, (g_ref, d_ref, mo_ref, vo_ref) = rest[:nterms], rest[-4:]
        g, pos = None, 0
        for size in sizes:
            part = None
            for r in t_refs[pos:pos + size]:
                t = r[...].astype(f32)
                part = t if part is None else part + t
            pos += size
            g = part if g is None else g + part
        mn = ADAM_B1 * m_ref[...] + (1.0 - ADAM_B1) * g
        vn = ADAM_B2 * v_ref[...] + (1.0 - ADAM_B2) * (g * g)
        g_ref[...] = g
        mo_ref[...] = mn
        vo_ref[...] = vn
        d_ref[...] = -ADAM_LR * ((mn * c1) / (jnp.sqrt(vn * c2) + ADAM_EPS) + ADAM_WD * w_ref[...])

    out = pl.pallas_call(body, name=name, out_shape=(res_shape,) * 4, grid=(Rr // tr, C // tc),
                         in_specs=[wspec] * 3 + g_specs + [ANY_SPEC] * len(prev), out_specs=(wspec,) * 4,
                         input_output_aliases={3 + nterms + k: k for k in range(len(prev))},
                         compiler_params=_cp("parallel", "parallel"))(to2(w), to2(m), to2(v), *g_args, *prev)
    return tuple(o.reshape(w.shape) for o in out)


def _pack_rows(parts, dtype):
    flat = jnp.concatenate([p.reshape(-1).astype(dtype) for p in parts])
    n = flat.shape[0]
    unit = PACK_COLS * 2 * SUBLANE
    padded = -(-n // unit) * unit
    return jnp.pad(flat, (0, padded - n)).reshape(padded // PACK_COLS, PACK_COLS)


def _unpack_rows(flat2d, shapes):
    flat = flat2d.reshape(-1)
    out, off = [], 0
    for shp in shapes:
        n = math.prod(shp)
        out.append(flat[off:off + n].reshape(shp))
        off += n
    return out


def _gather_weights(local, names, dtype, *, name):
    shapes = [local[n].shape for n in names]
    got = _all_gather_chips(_pack_rows([local[n] for n in names], dtype), name=name)
    per_chip = [_unpack_rows(got[s], shapes) for s in range(N_CHIPS)]
    return {n: jnp.concatenate([per_chip[s][k] for s in range(N_CHIPS)], axis=SHARD_AXIS[n]) for k, n in enumerate(names)}


def kernel(x, mem, norm_g, ssm_in_w, ssm_conv_w, ssm_conv_b, ssm_dt_bias, ssm_A_log, ssm_D, ssm_norm_g, ssm_out_w, cf_pw1_w, cf_pw1_b, cf_dw_w, cf_dw_b, cf_ln_g, cf_ln_b, cf_pw2_w, cf_pw2_b, xa_mem_g, xa_q_w, xa_kv_w, xa_o_w, ffn_in_w, ffn_conv_w, ffn_conv_b, ffn_out_w, loss_target, m_norm_g, m_ssm_in_w, m_ssm_conv_w, m_ssm_conv_b, m_ssm_dt_bias, m_ssm_A_log, m_ssm_D, m_ssm_norm_g, m_ssm_out_w, m_cf_pw1_w, m_cf_pw1_b, m_cf_dw_w, m_cf_dw_b, m_cf_ln_g, m_cf_ln_b, m_cf_pw2_w, m_cf_pw2_b, m_xa_mem_g, m_xa_q_w, m_xa_kv_w, m_xa_o_w, m_ffn_in_w, m_ffn_conv_w, m_ffn_conv_b, m_ffn_out_w, v_norm_g, v_ssm_in_w, v_ssm_conv_w, v_ssm_conv_b, v_ssm_dt_bias, v_ssm_A_log, v_ssm_D, v_ssm_norm_g, v_ssm_out_w, v_cf_pw1_w, v_cf_pw1_b, v_cf_dw_w, v_cf_dw_b, v_cf_ln_g, v_cf_ln_b, v_cf_pw2_w, v_cf_pw2_b, v_xa_mem_g, v_xa_q_w, v_xa_kv_w, v_xa_o_w, v_ffn_in_w, v_ffn_conv_w, v_ffn_conv_b, v_ffn_out_w):
    w_local = dict(zip(WEIGHT_NAMES, (norm_g, ssm_in_w, ssm_conv_w, ssm_conv_b, ssm_dt_bias, ssm_A_log, ssm_D, ssm_norm_g,
                                      ssm_out_w, cf_pw1_w, cf_pw1_b, cf_dw_w, cf_dw_b, cf_ln_g, cf_ln_b, cf_pw2_w, cf_pw2_b,
                                      xa_mem_g, xa_q_w, xa_kv_w, xa_o_w, ffn_in_w, ffn_conv_w, ffn_conv_b, ffn_out_w)))
    m_local = dict(zip(WEIGHT_NAMES, (m_norm_g, m_ssm_in_w, m_ssm_conv_w, m_ssm_conv_b, m_ssm_dt_bias, m_ssm_A_log, m_ssm_D,
                                      m_ssm_norm_g, m_ssm_out_w, m_cf_pw1_w, m_cf_pw1_b, m_cf_dw_w, m_cf_dw_b, m_cf_ln_g,
                                      m_cf_ln_b, m_cf_pw2_w, m_cf_pw2_b, m_xa_mem_g, m_xa_q_w, m_xa_kv_w, m_xa_o_w,
                                      m_ffn_in_w, m_ffn_conv_w, m_ffn_conv_b, m_ffn_out_w)))
    v_local = dict(zip(WEIGHT_NAMES, (v_norm_g, v_ssm_in_w, v_ssm_conv_w, v_ssm_conv_b, v_ssm_dt_bias, v_ssm_A_log, v_ssm_D,
                                      v_ssm_norm_g, v_ssm_out_w, v_cf_pw1_w, v_cf_pw1_b, v_cf_dw_w, v_cf_dw_b, v_cf_ln_g,
                                      v_cf_ln_b, v_cf_pw2_w, v_cf_pw2_b, v_xa_mem_g, v_xa_q_w, v_xa_kv_w, v_xa_o_w,
                                      v_ffn_in_w, v_ffn_conv_w, v_ffn_conv_b, v_ffn_out_w)))

    small = [n for n in WEIGHT_NAMES if n not in MATMUL_WEIGHTS]
    small_sharded = [n for n in small if SHARD_AXIS[n] is not None]
    W = {n: w_local[n] for n in small if SHARD_AXIS[n] is None}
    W.update(_gather_weights(w_local, small_sharded, f32, name="gather_small_weights"))

    def layer_index(n, i):
        return i // 2 if n in MIXER_WEIGHTS else i

    def keys_of(i, parts):
        return [(n, layer_index(n, i)) for part in parts for n in _layer_matmul_weights(i, part)]

    def shards(keys):
        return [w_local[n][l].astype(bf16) for n, l in keys]

    def usable(n, a):
        return a.reshape(N_CHIPS * a.shape[1], a.shape[2]) if n in ROW_SHARDED else a

    mixer0 = keys_of(0, ("mixer_in",))
    got0 = _gather_matmul_weights([s.reshape(1, 2, s.shape[0] // 2, s.shape[1]) for s in shards(mixer0)],
                                  name="gather_layer0_mixer")
    gather_groups = {(0, "mixer_out"): keys_of(0, ("mixer_out",)), (0, "rest"): keys_of(0, ("rest",))}
    gather_groups.update({(i, "mixer_in"): keys_of(i, ("mixer", "rest")) for i in range(1, DEPTH)})
    gathers, tokens, landed = {}, [], {}
    for gkey in sorted(gather_groups):
        send, recv, srcs, lands, token = _spread_start(shards(gather_groups[gkey]), False, name="gather_start_%d_%s" % gkey,
                                                       after=(got0[0], W[small_sharded[0]]))
        gathers[gkey] = (send, recv, srcs, lands)
        tokens.append(token)

    def layer_weights(i, part, after):
        if (i, part) == (0, "mixer_in"):
            landed.update({k: g.reshape((N_CHIPS, 2 * g.shape[3], g.shape[4])) for k, g in zip(mixer0, got0)})
        elif (i, part) in gathers:
            _, lands = _spread_wait(*gathers[i, part], (after,), False, name="gather_wait_%d_%s" % (i, part))
            landed.update(zip(gather_groups[i, part], lands))
        return {n: usable(n, landed[n, layer_index(n, i)]) for n in _layer_matmul_weights(i, part)}

    pending, scatters, swaps, own, sib = {}, {}, {}, {}, {}

    def scatter_start(gkey, keys, gl):
        send, recv, srcs, lands, token = _spread_start([gl[k] for k in keys], True, name="grads_start_%d_%s" % gkey)
        scatters[gkey] = (keys, send, recv, srcs, lands)
        return token

    def layer_grads(i, part, gl):
        grads = {(n, layer_index(n, i)): gl[n] for n in _layer_matmul_weights(i, part)}
        behind = []
        if part == "rest":
            if i + 1 < DEPTH:
                keys, send, recv, srcs, lands = scatters.pop((i + 1, "mixer"))
                _, lands = _spread_wait(send, recv, srcs, lands, (grads['xa_kv_w', i],), True,
                                        name="grads_wait_%d" % (i + 1))
                send, recv, srcs, lands, token = _swap_start(lands, name="grads_swap_start_%d" % (i + 1))
                swaps[i + 1] = (keys, send, recv, srcs, lands)
                behind.append(token)
            if i == 0:
                behind.append(scatter_start((0, "rest"), list(grads), grads))
            else:
                pending.update(grads)
        else:
            pending.update(grads)
            if i == 0:
                keys, send, recv, srcs, lands = scatters.pop((0, "rest"))
                _, lands = _spread_wait(send, recv, srcs, lands, (grads['ssm_in_w', 0],), True, name="grads_wait_0_rest")
                send, recv, srcs, lands, token = _swap_start(lands, name="grads_swap_start_0")
                swaps[0] = (keys, send, recv, srcs, lands)
                behind.append(token)
            behind.append(scatter_start((i, "mixer"), list(pending), dict(pending)))
            pending.clear()
        return behind

    sse, gx, gsmall = _device_step(x[0], mem[0], loss_target[0], W, layer_weights, layer_grads, tuple(tokens))

    loss = lax.psum(0.5 * sse[0, 0] / D_MODEL, ("x", "y", "c"))

    small_shapes = [gsmall[n].shape for n in small]
    ag_send, ag_recv, ag_src, ag_land, ag_token = _allgather_devices_start(_pack_rows([gsmall[n] for n in small], f32),
                                                                           name="allgather_small_start")
    last_keys, last_lands = [], []
    for gkey in sorted(scatters):
        keys, send, recv, srcs, lands = scatters[gkey]
        _, lands = _spread_wait(send, recv, srcs, lands, (gx, ag_token), True, name="grads_wait_%d_%s" % gkey)
        last_keys += keys
        last_lands += list(lands)
    last_swap = _swap_start(last_lands, name="grads_swap_start_last")
    for i in sorted(swaps):
        keys, send, recv, srcs, lands = swaps[i]
        mine, theirs = _swap_wait(send, recv, srcs, lands, (gx, last_swap[4]), name="grads_swap_wait_%d" % i)
        own.update(zip(keys, mine))
        sib.update(zip(keys, theirs))

    res = {}

    def adamw_matmul_weight(n, l):
        layers = w_local[n].shape[0]
        groups = [[(own[n, l], s) for s in range(N_CHIPS)], [(sib[n, l], s) for s in range(N_CHIPS)]]
        view = (lambda t: jnp.swapaxes(t, 1, 2)) if n == 'ssm_in_w' else (lambda t: t)
        res[n] = _adamw(view(w_local[n]), view(m_local[n]), view(v_local[n]), groups, name="adamw_%s_%d" % (n, l),
                        layer=(l, layers), prev=res.get(n))

    for n in MATMUL_WEIGHTS:
        for l in range(w_local[n].shape[0]):
            if (n, l) not in last_keys:
                adamw_matmul_weight(n, l)
    done = res[MATMUL_WEIGHTS[-1]][0]
    mine, theirs = _swap_wait(*last_swap[:4], (done,), name="grads_swap_wait_last")
    own.update(zip(last_keys, mine))
    sib.update(zip(last_keys, theirs))
    for n, l in last_keys:
        adamw_matmul_weight(n, l)
    slots = _allgather_devices_wait(ag_send, ag_recv, ag_src, ag_land, (res[last_keys[-1][0]][0],), name="allgather_small_wait")
    gsum = dict(zip(small, _unpack_rows(_sum_slots(slots, name="sum_small_grads"), small_shapes)))
    chip = 2 * lax.axis_index("x") + lax.axis_index("y")

    for n in small:
        g, ax = gsum[n], SHARD_AXIS[n]
        if ax is not None:
            width = w_local[n].shape[ax]
            g = lax.dynamic_slice_in_dim(g, chip * width, width, axis=ax)
        res[n] = _adamw(w_local[n], m_local[n], v_local[n], [[g]], name="adamw_" + n)
    res['ssm_in_w'] = tuple(jnp.swapaxes(t, 1, 2) for t in res['ssm_in_w'])
    return (loss, gx[None], *[res[n][0] for n in WEIGHT_NAMES], *[res[n][1] for n in WEIGHT_NAMES],
            *[res[n][2] for n in WEIGHT_NAMES], *[res[n][3] for n in WEIGHT_NAMES])
```

```python
import functools
import math

import jax
import jax.numpy as jnp
from jax import lax
from jax.experimental import pallas as pl
from jax.experimental.pallas import tpu as pltpu

f32 = jnp.float32
bf16 = jnp.bfloat16
S = jax.ShapeDtypeStruct

D_MODEL = 1024
DEPTH = 4
D_INNER = 2048
HEAD_DIM = 64
N_GROUPS = 4
HEADS_PER_GROUP = 8
N_SSM_HEADS = 32
D_STATE = 128
CHUNK = 128
SSM_CONV = 4
CONV_DIM = 3072
CF_KERNEL = 31
N_MEM = 256
XA_HEADS = 4
XA_HEAD_DIM = 256
D_FF = 2816
FFN_CONV = 3
EPS = 1e-6
ADAM_LR, ADAM_B1, ADAM_B2, ADAM_EPS, ADAM_WD, ADAM_STEP = 0.001, 0.9, 0.999, 1e-08, 0.01, 10

LANE = 128
SUBLANE = 8
ROW_SUB = 64
VMEM_LIMIT = 56 * 1024 * 1024
N_CHIPS = 4
PACK_COLS = 1024

WEIGHT_NAMES = ['norm_g', 'ssm_in_w', 'ssm_conv_w', 'ssm_conv_b', 'ssm_dt_bias', 'ssm_A_log', 'ssm_D', 'ssm_norm_g',
                'ssm_out_w', 'cf_pw1_w', 'cf_pw1_b', 'cf_dw_w', 'cf_dw_b', 'cf_ln_g', 'cf_ln_b', 'cf_pw2_w', 'cf_pw2_b',
                'xa_mem_g', 'xa_q_w', 'xa_kv_w', 'xa_o_w', 'ffn_in_w', 'ffn_conv_w', 'ffn_conv_b', 'ffn_out_w']
SHARD_AXIS = {'norm_g': 2, 'ssm_in_w': 2, 'ssm_conv_w': 2, 'ssm_conv_b': None, 'ssm_dt_bias': None, 'ssm_A_log': None,
              'ssm_D': None, 'ssm_norm_g': None, 'ssm_out_w': 1, 'cf_pw1_w': 2, 'cf_pw1_b': 1, 'cf_dw_w': 2, 'cf_dw_b': 1,
              'cf_ln_g': 1, 'cf_ln_b': 1, 'cf_pw2_w': 1, 'cf_pw2_b': 1, 'xa_mem_g': None, 'xa_q_w': 1, 'xa_kv_w': 2,
              'xa_o_w': 1, 'ffn_in_w': 2, 'ffn_conv_w': 2, 'ffn_conv_b': None, 'ffn_out_w': 1}
MATMUL_WEIGHTS = ('ssm_in_w', 'ssm_out_w', 'cf_pw1_w', 'cf_pw2_w', 'xa_q_w', 'xa_kv_w', 'xa_o_w', 'ffn_in_w', 'ffn_out_w')


def _cp(*sem):
    return pltpu.CompilerParams(dimension_semantics=tuple(sem), vmem_limit_bytes=VMEM_LIMIT)


def _pick(dim, pref):
    if dim <= pref:
        return dim
    best = None
    for t in range(LANE, pref + 1, LANE):
        if dim % t == 0:
            best = t
    assert best is not None, (dim, pref)
    return best


def _sigmoid(x):
    return 1.0 / (1.0 + jnp.exp(-x))


def _silu(x):
    return x * _sigmoid(x)


def _dsilu(x):
    s = _sigmoid(x)
    return s * (1.0 + x * (1.0 - s))


def _softplus(x):
    return jnp.maximum(x, 0.0) + jnp.log(1.0 + jnp.exp(-jnp.abs(x)))


_DN = {"nn": (((1,), (0,)), ((), ())), "nt": (((1,), (1,)), ((), ())), "tn": (((0,), (0,)), ((), ()))}


def _mm(a, b, mode, *, name, out_dtype=f32, bias=None, add=None, b_shards=False, out_shards=False, after=()):
    bshape = (b.shape[1], b.shape[2] * N_CHIPS) if b_shards else b.shape
    if mode == "nn":
        (M, K), (K2, N) = a.shape, bshape
    elif mode == "nt":
        (M, K), (N, K2) = a.shape, bshape
    else:
        (K, M), (K2, N) = a.shape, bshape
    assert K == K2, (a.shape, b.shape, mode)
    n_unit = N // N_CHIPS if ((b_shards and mode == "nn") or out_shards) else N
    k_unit = K // N_CHIPS if (b_shards and mode == "nt") else K
    tm, tn, tk = _pick(M, 1024), _pick(n_unit, 1408), _pick(k_unit, 1408)
    nk, nj_u, nk_u = K // tk, n_unit // tn, k_unit // tk
    a_spec = {"nn": pl.BlockSpec((tm, tk), lambda i, j, k: (i, k)), "nt": pl.BlockSpec((tm, tk), lambda i, j, k: (i, k)),
              "tn": pl.BlockSpec((tk, tm), lambda i, j, k: (k, i))}[mode]
    if not b_shards:
        b_spec = {"nn": pl.BlockSpec((tk, tn), lambda i, j, k: (k, j)), "nt": pl.BlockSpec((tn, tk), lambda i, j, k: (j, k)),
                  "tn": pl.BlockSpec((tk, tn), lambda i, j, k: (k, j))}[mode]
    else:
        b_spec = {"nn": pl.BlockSpec((None, tk, tn), lambda i, j, k: (j // nj_u, k, j % nj_u)),
                  "nt": pl.BlockSpec((None, tn, tk), lambda i, j, k: (k // nk_u, j, k % nk_u))}[mode]
    in_specs, args = [a_spec, b_spec], [a, b]
    if bias is not None:
        in_specs.append(pl.BlockSpec((1, tn), lambda i, j, k: (0, j)))
        args.append(bias)
    if add is not None:
        in_specs.append(pl.BlockSpec((tm, tn), lambda i, j, k: (i, j)))
        args.append(add)
    in_specs += [pl.BlockSpec(memory_space=pl.ANY)] * len(after)
    args += list(after)
    if not out_shards:
        out_shape, out_spec = S((M, N), out_dtype), pl.BlockSpec((tm, tn), lambda i, j, k: (i, j))
    else:
        out_shape = S((N_CHIPS, M, n_unit), out_dtype)
        out_spec = pl.BlockSpec((None, tm, tn), lambda i, j, k: (j // nj_u, i, j % nj_u))
    dn = _DN[mode]
    has_bias, has_add = bias is not None, add is not None

    def body(a_ref, b_ref, *rest):
        rest = list(rest)
        bias_ref = rest.pop(0) if has_bias else None
        add_ref = rest.pop(0) if has_add else None
        rest = rest[len(after):]
        o_ref = rest[0]

        def finish(r):
            if has_bias:
                r = r + bias_ref[...]
            if has_add:
                r = r + add_ref[...].astype(f32)
            o_ref[...] = r.astype(out_dtype)

        part = lax.dot_general(a_ref[...].astype(bf16), b_ref[...].astype(bf16), dn, preferred_element_type=f32)
        if nk == 1:
            finish(part)
            return
        acc_ref = rest[1]
        k = pl.program_id(2)

        @pl.when(k == 0)
        def _():
            acc_ref[...] = part

        @pl.when(k > 0)
        def _():
            acc_ref[...] += part

        @pl.when(k == nk - 1)
        def _():
            finish(acc_ref[...])

    return pl.pallas_call(
        body, name=name, out_shape=out_shape, grid=(M // tm, N // tn, nk),
        in_specs=in_specs, out_specs=out_spec, scratch_shapes=[pltpu.VMEM((tm, tn), f32)] if nk > 1 else [],
        compiler_params=_cp("parallel", "parallel", "arbitrary"))(*args)


def _rows(tm, C):
    return pl.BlockSpec((tm, C), lambda i: (i, 0))


def _const(shape):
    return pl.BlockSpec(shape, lambda i: tuple(0 for _ in shape))


def _rms_val(x, g):
    r = lax.rsqrt(jnp.mean(x * x, axis=-1, keepdims=True) + EPS)
    return x * r * g


def _rms_bwd_val(x, g, dy):
    r = lax.rsqrt(jnp.mean(x * x, axis=-1, keepdims=True) + EPS)
    xn = x * r
    dxh = dy * g
    dx = r * (dxh - xn * jnp.mean(dxh * xn, axis=-1, keepdims=True))
    return dx, jnp.sum(dy * xn, axis=0, keepdims=True)


ANY_SPEC = pl.BlockSpec(memory_space=pl.ANY)


def _rmsnorm_fwd(x, g, *, name, after=()):
    L, C = x.shape
    tm = _pick(L, 512)

    def body(x_ref, g_ref, *rest):
        rest[-1][...] = _rms_val(x_ref[...], g_ref[...]).astype(bf16)

    return pl.pallas_call(body, name=name, out_shape=S((L, C), bf16), grid=(L // tm,),
                          in_specs=[_rows(tm, C), _const((1, C))] + [ANY_SPEC] * len(after), out_specs=_rows(tm, C),
                          compiler_params=_cp("parallel"))(x, g, *after)


def _resid_norm_fwd(x, mix, g_post, g_next, *, name):
    L, C = x.shape
    tm = _pick(L, 512)
    want_h = g_next is not None

    def body(x_ref, m_ref, gp_ref, *rest):
        xn = x_ref[...] + _rms_val(m_ref[...], gp_ref[...])
        if want_h:
            gn_ref, xo_ref, h_ref = rest
            h_ref[...] = _rms_val(xn, gn_ref[...]).astype(bf16)
        else:
            (xo_ref,) = rest
        xo_ref[...] = xn

    in_specs = [_rows(tm, C), _rows(tm, C), _const((1, C))]
    args = [x, mix, g_post]
    out_shape, out_specs = [S((L, C), f32)], [_rows(tm, C)]
    if want_h:
        in_specs.append(_const((1, C)))
        args.append(g_next)
        out_shape.append(S((L, C), bf16))
        out_specs.append(_rows(tm, C))
    out = pl.pallas_call(body, name=name, out_shape=tuple(out_shape), grid=(L // tm,), in_specs=in_specs,
                         out_specs=tuple(out_specs), compiler_params=_cp("parallel"))(*args)
    return (out[0], out[1]) if want_h else (out[0], None)


def _norm_bwd(x, g, dy, *, name, add=None, out_dtype=f32, after=()):
    L, C = x.shape
    tm = _pick(L, 512)
    has_add = add is not None

    def body(x_ref, g_ref, dy_ref, *rest):
        rest = list(rest)
        add_ref = rest.pop(0) if has_add else None
        dx_ref, dg_ref, cs_ref = rest[-3:]
        i = pl.program_id(0)

        @pl.when(i == 0)
        def _():
            dg_ref[...] = jnp.zeros_like(dg_ref)
            cs_ref[...] = jnp.zeros_like(cs_ref)

        dx, dg = _rms_bwd_val(x_ref[...], g_ref[...], dy_ref[...].astype(f32))
        dg_ref[...] += dg
        cs_ref[...] += jnp.sum(dx, axis=0, keepdims=True)
        if has_add:
            dx = dx + add_ref[...]
        dx_ref[...] = dx.astype(out_dtype)

    in_specs = [_rows(tm, C), _const((1, C)), _rows(tm, C)]
    args = [x, g, dy]
    if has_add:
        in_specs.append(_rows(tm, C))
        args.append(add)
    in_specs += [ANY_SPEC] * len(after)
    args += list(after)
    return pl.pallas_call(body, name=name, out_shape=(S((L, C), out_dtype), S((1, C), f32), S((1, C), f32)),
                          grid=(L // tm,), in_specs=in_specs,
                          out_specs=(_rows(tm, C), _const((1, C)), _const((1, C))),
                          compiler_params=_cp("arbitrary"))(*args)


def _norm_bwd_chain(x, g, dy, add, x2, g2, *, name, after=()):
    L, C = x.shape
    tm = _pick(L, 512)

    def body(x_ref, g_ref, dy_ref, add_ref, x2_ref, g2_ref, *rest):
        G_ref, dg_ref, d2_ref, dg2_ref, cs2_ref = rest[-5:]
        i = pl.program_id(0)

        @pl.when(i == 0)
        def _():
            dg_ref[...] = jnp.zeros_like(dg_ref)
            dg2_ref[...] = jnp.zeros_like(dg2_ref)
            cs2_ref[...] = jnp.zeros_like(cs2_ref)

        dx, dg = _rms_bwd_val(x_ref[...], g_ref[...], dy_ref[...].astype(f32))
        G = dx + add_ref[...]
        dg_ref[...] += dg
        G_ref[...] = G
        d2, dg2 = _rms_bwd_val(x2_ref[...], g2_ref[...], G)
        dg2_ref[...] += dg2
        cs2_ref[...] += jnp.sum(d2, axis=0, keepdims=True)
        d2_ref[...] = d2.astype(bf16)

    row, vec = _rows(tm, C), _const((1, C))
    return pl.pallas_call(body, name=name,
                          out_shape=(S((L, C), f32), S((1, C), f32), S((L, C), bf16), S((1, C), f32), S((1, C), f32)),
                          grid=(L // tm,), in_specs=[row, vec, row, row, row, vec] + [ANY_SPEC] * len(after),
                          out_specs=(row, vec, row, vec, vec),
                          compiler_params=_cp("arbitrary"))(x, g, dy, add, x2, g2, *after)


def _loss_fwd_bwd(y, target, *, name):
    L, C = y.shape
    tm = _pick(L, 512)

    def body(y_ref, t_ref, acc_ref, dy_ref):
        i = pl.program_id(0)

        @pl.when(i == 0)
        def _():
            acc_ref[...] = jnp.zeros_like(acc_ref)

        e = y_ref[...] - t_ref[...]
        rs = jnp.sum(e * e, axis=-1, keepdims=True)
        acc_ref[...] += jnp.broadcast_to(jnp.sum(rs, axis=0, keepdims=True), (1, LANE))
        dy_ref[...] = e * (1.0 / C)

    return pl.pallas_call(body, name=name, out_shape=(S((1, LANE), f32), S((L, C), f32)), grid=(L // tm,),
                          in_specs=[_rows(tm, C), _rows(tm, C)], out_specs=(_const((1, LANE)), _rows(tm, C)),
                          compiler_params=_cp("arbitrary"))(y, target)


def _halo_rows(K):
    return SUBLANE if K - 1 <= SUBLANE else 32


def _prev_halo_spec(tm, H, C):
    return pl.BlockSpec((H, C), lambda i: (jnp.maximum(i * (tm // H) - 1, 0), 0))


def _next_halo_spec(tm, H, C, L):
    return pl.BlockSpec((H, C), lambda i: (jnp.minimum((i + 1) * (tm // H), L // H - 1), 0))


def _down_views(ext, H, n, K):
    bases, views = {}, []
    for s in range(K):
        q, r = divmod(s, SUBLANE)
        if r not in bases:
            bases[r] = ext if r == 0 else pltpu.roll(ext, r, axis=0)
        views.append(bases[r][H - SUBLANE * q:H - SUBLANE * q + n])
    return views


def _up_views(ext, n, K):
    bases, views = {}, []
    for s in range(K):
        q, r = divmod(s, SUBLANE)
        if r not in bases:
            bases[r] = ext if r == 0 else pltpu.roll(ext, ext.shape[0] - r, axis=0)
        views.append(bases[r][SUBLANE * q:SUBLANE * q + n])
    return views


def _causal_conv(ext, H, w_ref, K):
    views = _down_views(ext, H, ext.shape[0] - H, K)
    acc = None
    for k in range(K):
        term = views[K - 1 - k] * w_ref[k:k + 1, :]
        acc = term if acc is None else acc + term
    return acc


def _anticausal_conv(ext, tm, w_ref, K):
    views = _up_views(ext, tm, K)
    acc = None
    for k in range(K):
        term = views[K - 1 - k] * w_ref[k:k + 1, :]
        acc = term if acc is None else acc + term
    return acc


def _tap_grads(dw_ref, d_cur, x_ext, H, K):
    views = _down_views(x_ext, H, d_cur.shape[0], K)
    for k in range(K):
        dw_ref[k:k + 1, :] += jnp.sum(d_cur * views[K - 1 - k], axis=0, keepdims=True)


def _fold_rows(x):
    acc = x[0:SUBLANE]
    for r in range(SUBLANE, x.shape[0], SUBLANE):
        acc = acc + x[r:r + SUBLANE]
    return acc


def _pad_taps(w, K):
    return jnp.pad(w, ((0, _halo_rows(K) - K), (0, 0)))


def _conv_act_fwd(x, w, b, *, K, act, name, out_dtype, tm_pref=256):
    L, C = x.shape
    H = _halo_rows(K)
    tm = _pick(L, tm_pref)
    Co = C if act == "silu" else C // 2

    rs = min(ROW_SUB, tm)

    def body(h_ref, x_ref, w_ref, b_ref, o_ref, ext_scr):
        i = pl.program_id(0)
        ext_scr[0:H] = jnp.where(i > 0, h_ref[...], 0.0)
        ext_scr[H:] = x_ref[...]
        for j in range(Co // LANE):
            lanes = [j] if act == "silu" else [j, j + Co // LANE]
            wb = [(w_ref[:, c * LANE:(c + 1) * LANE], b_ref[:, c * LANE:(c + 1) * LANE]) for c in lanes]
            for r0 in range(0, tm, rs):
                us = [_causal_conv(ext_scr[r0:r0 + rs + H, c * LANE:(c + 1) * LANE], H, wc, K) + bc
                      for c, (wc, bc) in zip(lanes, wb)]
                y = _silu(us[0]) if act == "silu" else _silu(us[0]) * us[1]
                o_ref[r0:r0 + rs, j * LANE:(j + 1) * LANE] = y.astype(out_dtype)

    return pl.pallas_call(body, name=name, out_shape=S((L, Co), out_dtype), grid=(L // tm,),
                          in_specs=[_prev_halo_spec(tm, H, C), _rows(tm, C), _const((H, C)), _const((1, C))],
                          out_specs=_rows(tm, Co), scratch_shapes=[pltpu.VMEM((H + tm, C), f32)],
                          compiler_params=_cp("parallel"))(x, x, w, b)


def _conv_act_bwd(x, dparts, w, b, *, K, act, name, tm_pref=256):
    L, C = x.shape
    H = _halo_rows(K)
    tm = _pick(L, tm_pref)
    nb = L // tm
    Co = C if act == "silu" else C // 2
    nparts = len(dparts)

    rs = min(ROW_SUB, tm)

    def body(hp_ref, x_ref, hn_ref, w_ref, b_ref, *rest):
        d_refs, dn_refs = rest[:nparts], rest[nparts:2 * nparts]
        dx_ref, dw_ref, db_ref, ext_scr, d_scr = rest[2 * nparts:]
        i = pl.program_id(0)

        @pl.when(i == 0)
        def _():
            dw_ref[...] = jnp.zeros_like(dw_ref)
            db_ref[...] = jnp.zeros_like(db_ref)

        ext_scr[0:H] = jnp.where(i > 0, hp_ref[...], 0.0)
        ext_scr[H:H + tm] = x_ref[...]
        ext_scr[H + tm:] = jnp.where(i < nb - 1, hn_ref[...], 0.0)
        off = 0
        for r, rn in zip(d_refs, dn_refs):
            d_scr[0:tm, off:off + r.shape[1]] = r[...].astype(f32)
            d_scr[tm:, off:off + r.shape[1]] = jnp.where(i < nb - 1, rn[...].astype(f32), 0.0)
            off += r.shape[1]
        for j in range(Co // LANE):
            lanes = [j] if act == "silu" else [j, j + Co // LANE]
            wb = [(w_ref[:, c * LANE:(c + 1) * LANE], b_ref[:, c * LANE:(c + 1) * LANE]) for c in lanes]
            db_acc = [jnp.zeros((SUBLANE, LANE), f32) for _ in lanes]
            dw_acc = [[jnp.zeros((SUBLANE, LANE), f32) for _ in range(K)] for _ in lanes]
            for r0 in range(0, tm, rs):
                xvs = [_down_views(ext_scr[r0:r0 + rs + 2 * H, c * LANE:(c + 1) * LANE], H, rs + H, K) for c in lanes]
                us = [sum(xv[K - 1 - k] * wc[k:k + 1, :] for k in range(K)) + bc for xv, (wc, bc) in zip(xvs, wb)]
                d = d_scr[r0:r0 + rs + H, j * LANE:(j + 1) * LANE]
                dus = [d * _dsilu(us[0])] if act == "silu" else [d * us[1] * _dsilu(us[0]), d * _silu(us[0])]
                for q, (c, xv, du, (wc, _)) in enumerate(zip(lanes, xvs, dus, wb)):
                    dx_ref[r0:r0 + rs, c * LANE:(c + 1) * LANE] = _anticausal_conv(du, rs, wc, K).astype(bf16)
                    du_cur = du[:rs]
                    db_acc[q] = db_acc[q] + _fold_rows(du_cur)
                    for k in range(K):
                        dw_acc[q][k] = dw_acc[q][k] + _fold_rows(du_cur * xv[K - 1 - k][:rs])
            for q, c in enumerate(lanes):
                db_ref[:, c * LANE:(c + 1) * LANE] += jnp.sum(db_acc[q], axis=0, keepdims=True)
                for k in range(K):
                    dw_ref[k:k + 1, c * LANE:(c + 1) * LANE] += jnp.sum(dw_acc[q][k], axis=0, keepdims=True)

    in_specs = [_prev_halo_spec(tm, H, C), _rows(tm, C), _next_halo_spec(tm, H, C, L), _const((H, C)), _const((1, C))]
    in_specs += [_rows(tm, p.shape[1]) for p in dparts] + [_next_halo_spec(tm, H, p.shape[1], L) for p in dparts]
    return pl.pallas_call(body, name=name, out_shape=(S((L, C), bf16), S((H, C), f32), S((1, C), f32)), grid=(nb,),
                          in_specs=in_specs, out_specs=(_rows(tm, C), _const((H, C)), _const((1, C))),
                          scratch_shapes=[pltpu.VMEM((tm + 2 * H, C), f32), pltpu.VMEM((tm + H, Co), f32)],
                          compiler_params=_cp("arbitrary"))(x, x, x, w, b, *dparts, *dparts)


def _cf_fwd(u, dw_w, dw_b, ln_g, ln_b, *, name):
    L, C2 = u.shape
    C = C2 // 2
    K, H = CF_KERNEL, _halo_rows(CF_KERNEL)
    tm = _pick(L, 256)

    rs = min(ROW_SUB, tm)
    nl = C // LANE

    def body(h_ref, u_ref, w_ref, b_ref, g_ref, lb_ref, c_ref, s_ref, u_scr):
        i = pl.program_id(0)
        u_scr[0:H] = jnp.where(i > 0, h_ref[...], 0.0)
        u_scr[H:] = u_ref[...]
        for j in range(nl):
            wj, bj = w_ref[:, j * LANE:(j + 1) * LANE], b_ref[:, j * LANE:(j + 1) * LANE]
            for r0 in range(0, tm, rs):
                glu = u_scr[r0:r0 + rs + H, j * LANE:(j + 1) * LANE] \
                    * _sigmoid(u_scr[r0:r0 + rs + H, (nl + j) * LANE:(nl + j + 1) * LANE])
                c_ref[r0:r0 + rs, j * LANE:(j + 1) * LANE] = _causal_conv(glu, H, wj, K) + bj
        c = c_ref[...]
        mu = jnp.mean(c, axis=-1, keepdims=True)
        xc = c - mu
        var = jnp.mean(xc * xc, axis=-1, keepdims=True)
        ln = xc * lax.rsqrt(var + EPS) * g_ref[...] + lb_ref[...]
        s_ref[...] = _silu(ln).astype(bf16)

    return pl.pallas_call(body, name=name, out_shape=(S((L, C), f32), S((L, C), bf16)), grid=(L // tm,),
                          in_specs=[_prev_halo_spec(tm, H, C2), _rows(tm, C2), _const((H, C)), _const((1, C)),
                                    _const((1, C)), _const((1, C))],
                          out_specs=(_rows(tm, C), _rows(tm, C)), scratch_shapes=[pltpu.VMEM((H + tm, C2), f32)],
                          compiler_params=_cp("parallel"))(u, u, dw_w, dw_b, ln_g, ln_b)


def _cf_ln_bwd(c, ln_g, ln_b, ds, *, name):
    L, C = c.shape
    tm = _pick(L, 512)

    def body(c_ref, g_ref, lb_ref, ds_ref, dc_ref, dg_ref, db_ref):
        i = pl.program_id(0)

        @pl.when(i == 0)
        def _():
            dg_ref[...] = jnp.zeros_like(dg_ref)
            db_ref[...] = jnp.zeros_like(db_ref)

        c = c_ref[...]
        mu = jnp.mean(c, axis=-1, keepdims=True)
        xc = c - mu
        r = lax.rsqrt(jnp.mean(xc * xc, axis=-1, keepdims=True) + EPS)
        xh = xc * r
        ln = xh * g_ref[...] + lb_ref[...]
        dln = ds_ref[...].astype(f32) * _dsilu(ln)
        dg_ref[...] += jnp.sum(dln * xh, axis=0, keepdims=True)
        db_ref[...] += jnp.sum(dln, axis=0, keepdims=True)
        dxh = dln * g_ref[...]
        dc_ref[...] = r * (dxh - jnp.mean(dxh, axis=-1, keepdims=True) - xh * jnp.mean(dxh * xh, axis=-1, keepdims=True))

    return pl.pallas_call(body, name=name, out_shape=(S((L, C), f32), S((1, C), f32), S((1, C), f32)), grid=(L // tm,),
                          in_specs=[_rows(tm, C), _const((1, C)), _const((1, C)), _rows(tm, C)],
                          out_specs=(_rows(tm, C), _const((1, C)), _const((1, C))),
                          compiler_params=_cp("arbitrary"))(c, ln_g, ln_b, ds)


def _cf_glu_bwd(u, dc, dw_w, *, name):
    L, C2 = u.shape
    C = C2 // 2
    K, H = CF_KERNEL, _halo_rows(CF_KERNEL)
    tm = _pick(L, 256)
    nb = L // tm

    rs = min(ROW_SUB, tm)
    nl = C // LANE

    fold = _fold_rows

    def body(uh_ref, u_ref, dc_ref, dch_ref, w_ref, du_ref, dw_ref, db_ref, dus_ref, u_scr, dc_scr):
        i = pl.program_id(0)

        @pl.when(i == 0)
        def _():
            dw_ref[...] = jnp.zeros_like(dw_ref)
            db_ref[...] = jnp.zeros_like(db_ref)
            dus_ref[...] = jnp.zeros_like(dus_ref)

        u_scr[0:H] = jnp.where(i > 0, uh_ref[...], 0.0)
        u_scr[H:] = u_ref[...]
        dc_scr[0:tm] = dc_ref[...]
        dc_scr[tm:] = jnp.where(i < nb - 1, dch_ref[...], 0.0)
        for j in range(nl):
            la, lg = slice(j * LANE, (j + 1) * LANE), slice((nl + j) * LANE, (nl + j + 1) * LANE)
            wj = w_ref[:, la]
            zero8 = jnp.zeros((SUBLANE, LANE), f32)
            dw_acc, db_acc, dua_acc, dug_acc = [zero8] * K, zero8, zero8, zero8
            for r0 in range(0, tm, rs):
                a_e = u_scr[r0:r0 + rs + H, la]
                sg = _sigmoid(u_scr[r0:r0 + rs + H, lg])
                dce = dc_scr[r0:r0 + rs + H, la]
                dglu = _anticausal_conv(dce, rs, wj, K)
                a_c, sg_c, dc_c = a_e[H:], sg[H:], dce[:rs]
                du_a = dglu * sg_c
                du_g = dglu * a_c * sg_c * (1.0 - sg_c)
                du_ref[r0:r0 + rs, la] = du_a.astype(bf16)
                du_ref[r0:r0 + rs, lg] = du_g.astype(bf16)
                dua_acc, dug_acc, db_acc = dua_acc + fold(du_a), dug_acc + fold(du_g), db_acc + fold(dc_c)
                views = _down_views(a_e * sg, H, rs, K)
                dw_acc = [dw_acc[k] + fold(dc_c * views[K - 1 - k]) for k in range(K)]
            dus_ref[:, la] += jnp.sum(dua_acc, axis=0, keepdims=True)
            dus_ref[:, lg] += jnp.sum(dug_acc, axis=0, keepdims=True)
            db_ref[:, la] += jnp.sum(db_acc, axis=0, keepdims=True)
            for k in range(K):
                dw_ref[k:k + 1, la] += jnp.sum(dw_acc[k], axis=0, keepdims=True)

    return pl.pallas_call(body, name=name,
                          out_shape=(S((L, C2), bf16), S((H, C), f32), S((1, C), f32), S((1, C2), f32)), grid=(nb,),
                          in_specs=[_prev_halo_spec(tm, H, C2), _rows(tm, C2), _rows(tm, C), _next_halo_spec(tm, H, C, L),
                                    _const((H, C))],
                          out_specs=(_rows(tm, C2), _const((H, C)), _const((1, C)), _const((1, C2))),
                          scratch_shapes=[pltpu.VMEM((H + tm, C2), f32), pltpu.VMEM((tm + H, C), f32)],
                          compiler_params=_cp("arbitrary"))(u, u, dc, dc, dw_w)


def _gated_norm_fwd(y, z, g, *, name):
    L, C = y.shape
    tm = _pick(L, 256)

    def body(y_ref, z_ref, g_ref, o_ref):
        o_ref[...] = _rms_val(y_ref[...] * _silu(z_ref[...]), g_ref[...]).astype(bf16)

    return pl.pallas_call(body, name=name, out_shape=S((L, C), bf16), grid=(L // tm,),
                          in_specs=[_rows(tm, C), _rows(tm, C), _const((1, C))], out_specs=_rows(tm, C),
                          compiler_params=_cp("parallel"))(y, z, g)


def _gated_norm_bwd(y, z, g, dyn, *, name):
    L, C = y.shape
    tm = _pick(L, 256)

    def body(y_ref, z_ref, g_ref, d_ref, dy_ref, dz_ref, dg_ref):
        i = pl.program_id(0)

        @pl.when(i == 0)
        def _():
            dg_ref[...] = jnp.zeros_like(dg_ref)

        y, z = y_ref[...], z_ref[...]
        sz = _silu(z)
        du, dg = _rms_bwd_val(y * sz, g_ref[...], d_ref[...].astype(f32))
        dg_ref[...] += dg
        dy_ref[...] = du * sz
        dz_ref[...] = (du * y * _dsilu(z)).astype(bf16)

    return pl.pallas_call(body, name=name, out_shape=(S((L, C), f32), S((L, C), bf16), S((1, C), f32)), grid=(L // tm,),
                          in_specs=[_rows(tm, C), _rows(tm, C), _const((1, C)), _rows(tm, C)],
                          out_specs=(_rows(tm, C), _rows(tm, C), _const((1, C))),
                          compiler_params=_cp("arbitrary"))(y, z, g, dyn)


_XA_SCALE = XA_HEAD_DIM ** -0.5


def _attn_fwd(q, kv, *, name):
    L, C = q.shape
    tm = _pick(L, 1024)
    Dh = XA_HEAD_DIM

    def body(q_ref, kv_ref, o_ref):
        for h in range(XA_HEADS):
            qh = q_ref[:, h * Dh:(h + 1) * Dh]
            kh = kv_ref[:, h * Dh:(h + 1) * Dh]
            vh = kv_ref[:, C + h * Dh:C + (h + 1) * Dh]
            s = lax.dot_general(qh, kh, _DN["nt"], preferred_element_type=f32) * _XA_SCALE
            e = jnp.exp(s - jnp.max(s, axis=-1, keepdims=True))
            p = e / jnp.sum(e, axis=-1, keepdims=True)
            o_ref[:, h * Dh:(h + 1) * Dh] = jnp.dot(p.astype(bf16), vh, preferred_element_type=f32).astype(bf16)

    return pl.pallas_call(body, name=name, out_shape=S((L, C), bf16), grid=(L // tm,),
                          in_specs=[_rows(tm, C), _const((N_MEM, 2 * C))], out_specs=_rows(tm, C),
                          compiler_params=_cp("parallel"))(q, kv)


def _attn_bwd(q, kv, do, *, name):
    L, C = q.shape
    tm = _pick(L, 1024)
    Dh = XA_HEAD_DIM

    def body(q_ref, kv_ref, do_ref, dq_ref, dkv_ref):
        i = pl.program_id(0)

        @pl.when(i == 0)
        def _():
            dkv_ref[...] = jnp.zeros_like(dkv_ref)

        for h in range(XA_HEADS):
            qh = q_ref[:, h * Dh:(h + 1) * Dh]
            kh = kv_ref[:, h * Dh:(h + 1) * Dh]
            vh = kv_ref[:, C + h * Dh:C + (h + 1) * Dh]
            doh = do_ref[:, h * Dh:(h + 1) * Dh]
            s = lax.dot_general(qh, kh, _DN["nt"], preferred_element_type=f32) * _XA_SCALE
            e = jnp.exp(s - jnp.max(s, axis=-1, keepdims=True))
            p = e / jnp.sum(e, axis=-1, keepdims=True)
            pb = p.astype(bf16)
            dkv_ref[:, C + h * Dh:C + (h + 1) * Dh] += lax.dot_general(pb, doh, _DN["tn"], preferred_element_type=f32)
            dp = lax.dot_general(doh, vh, _DN["nt"], preferred_element_type=f32)
            ds = (p * (dp - jnp.sum(dp * p, axis=-1, keepdims=True)) * _XA_SCALE).astype(bf16)
            dq_ref[:, h * Dh:(h + 1) * Dh] = jnp.dot(ds, kh, preferred_element_type=f32).astype(bf16)
            dkv_ref[:, h * Dh:(h + 1) * Dh] += lax.dot_general(ds, qh, _DN["tn"], preferred_element_type=f32)

    return pl.pallas_call(body, name=name, out_shape=(S((L, C), bf16), S((N_MEM, 2 * C), f32)), grid=(L // tm,),
                          in_specs=[_rows(tm, C), _const((N_MEM, 2 * C)), _rows(tm, C)],
                          out_specs=(_rows(tm, C), _const((N_MEM, 2 * C))),
                          compiler_params=_cp("arbitrary"))(q, kv, do)


Q = CHUNK
PAIRS = HEADS_PER_GROUP // 2


def _split(x, pieces):
    out = []
    for _ in range(pieces - 1):
        p = x.astype(bf16)
        out.append(p)
        x = x - p.astype(f32)
    return out + [x.astype(bf16)]


def _sel_right(x, sel, mode="nn", pieces=2):
    return sum(lax.dot_general(p, sel, _DN[mode], preferred_element_type=f32) for p in _split(x, pieces))


def _sel_left(sel, x, pieces=3):
    return sum(lax.dot_general(sel, p, _DN["nn"], preferred_element_type=f32) for p in _split(x, pieces))


def _ssd_common(dt_ref, hp_ref):
    dt_pre = dt_ref[...] + hp_ref[0:1, :]
    dt = _softplus(dt_pre)
    A = -jnp.exp(hp_ref[1:2, :])
    a = dt * A
    row = lax.broadcasted_iota(jnp.int32, (Q, Q), 0)
    col = lax.broadcasted_iota(jnp.int32, (Q, Q), 1)
    tri = row >= col
    cs = _sel_left(tri.astype(bf16), a)
    T = cs[Q - 1:Q, :]
    return dict(dt_pre=dt_pre, dt=dt, A=A, cs=cs, csT=cs.T, T=T, ecs=jnp.exp(cs), eend=jnp.exp(T - cs), eT=jnp.exp(T),
                tri=tri, row=row, col=col)


def _pair_expand(v, hA, lo):
    return jnp.where(lo, v[:, hA:hA + 1], v[:, hA + 1:hA + 2])


def _decay(cm, h):
    seg = cm["cs"][:, h:h + 1] - cm["csT"][h:h + 1, :]
    return jnp.where(cm["tri"], jnp.exp(jnp.where(cm["tri"], seg, 0.0)), 0.0)


def _decay_t(cm, h):
    keep = cm["row"] <= cm["col"]
    seg = cm["csT"][h:h + 1, :] - cm["cs"][:, h:h + 1]
    return jnp.where(keep, jnp.exp(jnp.where(keep, seg, 0.0)), 0.0)


ALL_PAIRS = N_SSM_HEADS // 2
GN = N_GROUPS * D_STATE


def _ssd_fwd(act, dtp, hp, *, name):
    L = act.shape[0]
    nc = L // Q

    def body(xs_ref, b_ref, c_ref, dt_ref, hp_ref, y_ref, hs_ref, h_scr):
        c = pl.program_id(0)

        @pl.when(c == 0)
        def _():
            h_scr[...] = jnp.zeros_like(h_scr)

        cm = _ssd_common(dt_ref, hp_ref)
        lo = lax.broadcasted_iota(jnp.int32, (Q, LANE), 1) < HEAD_DIM
        top = lax.broadcasted_iota(jnp.int32, (LANE, LANE), 0) < HEAD_DIM
        Drow = hp_ref[2:3, :]
        for g in range(N_GROUPS):
            Bb = b_ref[:, g * D_STATE:(g + 1) * D_STATE].astype(bf16)
            Cb = c_ref[:, g * D_STATE:(g + 1) * D_STATE].astype(bf16)
            CB = lax.dot_general(Cb, Bb, _DN["nt"], preferred_element_type=f32)
            for jj in range(PAIRS):
                p = g * PAIRS + jj
                hA, hB = 2 * p, 2 * p + 1
                dtx, ecsx, eendx = (_pair_expand(cm[k], hA, lo) for k in ("dt", "ecs", "eend"))
                xs_p = xs_ref[:, p * LANE:(p + 1) * LANE]
                Xd = xs_p * dtx
                Y = None
                for h, Xm in ((hA, jnp.where(lo, Xd, 0.0)), (hB, jnp.where(lo, 0.0, Xd))):
                    W = (CB * _decay(cm, h)).astype(bf16)
                    t = jnp.dot(W, Xm.astype(bf16), preferred_element_type=f32)
                    Y = t if Y is None else Y + t
                Hp = h_scr[p]
                hs_ref[0, p] = Hp
                Yoff = lax.dot_general(Cb, Hp.astype(bf16), _DN["nt"], preferred_element_type=f32) * ecsx
                Dx = jnp.where(lo[0:1, :], Drow[:, hA:hA + 1], Drow[:, hB:hB + 1])
                y_ref[:, p * LANE:(p + 1) * LANE] = Y + Yoff + xs_p * Dx
                Snew = lax.dot_general((Xd * eendx).astype(bf16), Bb, _DN["tn"], preferred_element_type=f32)
                eTx = jnp.where(top, cm["eT"][:, hA:hA + 1], cm["eT"][:, hB:hB + 1])
                h_scr[p] = Hp * eTx + Snew

    return pl.pallas_call(
        body, name=name, out_shape=(S((L, D_INNER), f32), S((nc, ALL_PAIRS, LANE, D_STATE), f32)),
        grid=(nc,),
        in_specs=[pl.BlockSpec((Q, D_INNER), lambda c: (c, 0)),
                  pl.BlockSpec((Q, GN), lambda c: (c, D_INNER // GN)),
                  pl.BlockSpec((Q, GN), lambda c: (c, D_INNER // GN + 1)),
                  pl.BlockSpec((Q, LANE), lambda c: (c, 0)),
                  pl.BlockSpec((SUBLANE, LANE), lambda c: (0, 0))],
        out_specs=(pl.BlockSpec((Q, D_INNER), lambda c: (c, 0)),
                   pl.BlockSpec((1, ALL_PAIRS, LANE, D_STATE), lambda c: (c, 0, 0, 0))),
        scratch_shapes=[pltpu.VMEM((ALL_PAIRS, LANE, D_STATE), f32)],
        compiler_params=_cp("arbitrary"))(act, act, act, dtp, hp)


def _ssd_bwd(act, dtp, hp, dy, hs, *, name):
    L = act.shape[0]
    nc = L // Q

    def body(xs_ref, b_ref, c_ref, dt_ref, hp_ref, dy_ref, hs_ref, dxs_ref, db_ref, dc_ref, ddt_ref, dhp_ref, dh_scr):
        c = pl.program_id(0)

        @pl.when(c == 0)
        def _():
            dh_scr[...] = jnp.zeros_like(dh_scr)
            dhp_ref[...] = jnp.zeros_like(dhp_ref)

        cm = _ssd_common(dt_ref, hp_ref)
        lane = lax.broadcasted_iota(jnp.int32, (Q, LANE), 1)
        sub = lax.broadcasted_iota(jnp.int32, (LANE, LANE), 0)
        lo = lane < HEAD_DIM
        top = sub < HEAD_DIM
        Drow = hp_ref[2:3, :]
        zero = jnp.zeros((Q, LANE), f32)
        dcs, dcsT, ddtx = zero, zero, zero
        dD_row = jnp.zeros((1, LANE), f32)
        dT_row = jnp.zeros((1, LANE), f32)
        for g in range(N_GROUPS):
            Bb = b_ref[:, g * D_STATE:(g + 1) * D_STATE].astype(bf16)
            Cb = c_ref[:, g * D_STATE:(g + 1) * D_STATE].astype(bf16)
            CB = lax.dot_general(Cb, Bb, _DN["nt"], preferred_element_type=f32)
            CBT = lax.dot_general(Bb, Cb, _DN["nt"], preferred_element_type=f32)
            dC, dB, dCB = zero, zero, jnp.zeros((Q, Q), f32)
            for jj in range(PAIRS):
                p = g * PAIRS + jj
                hA, hB = 2 * p, 2 * p + 1
                Pj = (lane == jnp.where(top, hA, hB)).astype(bf16)
                dtx, ecsx, eendx = (_pair_expand(cm[k], hA, lo) for k in ("dt", "ecs", "eend"))
                xs_p = xs_ref[:, p * LANE:(p + 1) * LANE]
                dY_p = dy_ref[:, p * LANE:(p + 1) * LANE]
                Xd = xs_p * dtx
                Xdb = Xd.astype(bf16)
                Hp, dHn = hs_ref[0, p], dh_scr[p]
                Hb, dHb = Hp.astype(bf16), dHn.astype(bf16)
                EdYb = (dY_p * ecsx).astype(bf16)
                YoffN = lax.dot_general(Cb, Hb, _DN["nt"], preferred_element_type=f32)
                dC = dC + jnp.dot(EdYb, Hb, preferred_element_type=f32)
                dH_off = lax.dot_general(EdYb, Cb, _DN["tn"], preferred_element_type=f32)
                R = lax.dot_general(Bb, dHb, _DN["nt"], preferred_element_type=f32)
                Xe = Xd * eendx
                dB = dB + jnp.dot(Xe.astype(bf16), dHb, preferred_element_type=f32)
                dXd = R * eendx
                RXe = R * Xe
                dcs = dcs + _sel_right(dY_p * YoffN * ecsx - RXe, Pj)
                HH = dHn * Hp
                hh = [jnp.sum(jnp.sum(HH[r0:r0 + HEAD_DIM], axis=0, keepdims=True), axis=1, keepdims=True)
                      for r0 in (0, HEAD_DIM)]
                rxe_cols = jnp.broadcast_to(jnp.sum(RXe, axis=0, keepdims=True), (SUBLANE, LANE))
                dT_row = dT_row + _sel_right(rxe_cols, Pj, pieces=3)[0:1] \
                    + (jnp.where(lane[0:1] == hA, hh[0], 0.0) + jnp.where(lane[0:1] == hB, hh[1], 0.0)) * cm["eT"]
                for h, keep in ((hA, lo), (hB, jnp.logical_not(lo))):
                    M = _decay(cm, h)
                    Wf = CB * M
                    dYm = jnp.where(keep, dY_p, 0.0).astype(bf16)
                    dW = lax.dot_general(dYm, Xdb, _DN["nt"], preferred_element_type=f32)
                    WT = (CBT * _decay_t(cm, h)).astype(bf16)
                    dXd = dXd + jnp.dot(WT, dYm, preferred_element_type=f32)
                    Z = dW * Wf
                    dcs = dcs + _sel_right(Z, (lane == h).astype(bf16))
                    dcsT = dcsT + jnp.where(sub == h, jnp.sum(Z, axis=0, keepdims=True), 0.0)
                    dCB = dCB + dW * M
                Dx = jnp.where(lo[0:1, :], Drow[:, hA:hA + 1], Drow[:, hB:hB + 1])
                dxs_ref[:, p * LANE:(p + 1) * LANE] = dXd * dtx + dY_p * Dx
                ddtx = ddtx + _sel_right(dXd * xs_p, Pj)
                dD_cols = jnp.broadcast_to(jnp.sum(dY_p * xs_p, axis=0, keepdims=True), (SUBLANE, LANE))
                dD_row = dD_row + _sel_right(dD_cols, Pj, pieces=3)[0:1]
                eTx = jnp.where(top, cm["eT"][:, hA:hA + 1], cm["eT"][:, hB:hB + 1])
                dh_scr[p] = dHn * eTx + dH_off
            dCBb = dCB.astype(bf16)
            dc_ref[:, g * D_STATE:(g + 1) * D_STATE] = dC + jnp.dot(dCBb, Bb, preferred_element_type=f32)
            db_ref[:, g * D_STATE:(g + 1) * D_STATE] = dB + lax.dot_general(dCBb, Cb, _DN["tn"], preferred_element_type=f32)
        dcs = dcs - dcsT.T + jnp.where(lax.broadcasted_iota(jnp.int32, (Q, LANE), 0) == Q - 1, dT_row, 0.0)
        da = _sel_left((cm["row"] <= cm["col"]).astype(bf16), dcs)
        ddt_pre = (da * cm["A"] + ddtx) * _sigmoid(cm["dt_pre"])
        ddt_ref[...] = ddt_pre
        r8 = lax.broadcasted_iota(jnp.int32, (SUBLANE, LANE), 0)
        dhp_ref[...] += jnp.where(r8 == 0, jnp.sum(ddt_pre, axis=0, keepdims=True),
                                  jnp.where(r8 == 1, jnp.sum(da * cm["dt"], axis=0, keepdims=True) * cm["A"],
                                            jnp.where(r8 == 2, dD_row, 0.0)))

    rev = lambda c: nc - 1 - c
    return pl.pallas_call(
        body, name=name,
        out_shape=(S((L, D_INNER), f32), S((L, GN), f32), S((L, GN), f32), S((L, LANE), f32), S((SUBLANE, LANE), f32)),
        grid=(nc,),
        in_specs=[pl.BlockSpec((Q, D_INNER), lambda c: (rev(c), 0)),
                  pl.BlockSpec((Q, GN), lambda c: (rev(c), D_INNER // GN)),
                  pl.BlockSpec((Q, GN), lambda c: (rev(c), D_INNER // GN + 1)),
                  pl.BlockSpec((Q, LANE), lambda c: (rev(c), 0)),
                  pl.BlockSpec((SUBLANE, LANE), lambda c: (0, 0)),
                  pl.BlockSpec((Q, D_INNER), lambda c: (rev(c), 0)),
                  pl.BlockSpec((1, ALL_PAIRS, LANE, D_STATE), lambda c: (rev(c), 0, 0, 0))],
        out_specs=(pl.BlockSpec((Q, D_INNER), lambda c: (rev(c), 0)),
                   pl.BlockSpec((Q, GN), lambda c: (rev(c), 0)),
                   pl.BlockSpec((Q, GN), lambda c: (rev(c), 0)),
                   pl.BlockSpec((Q, LANE), lambda c: (rev(c), 0)),
                   pl.BlockSpec((SUBLANE, LANE), lambda c: (0, 0))),
        scratch_shapes=[pltpu.VMEM((ALL_PAIRS, LANE, D_STATE), f32)],
        compiler_params=_cp("arbitrary"))(act, act, act, dtp, hp, dy, hs)


def _group_pad_cols(w):
    return jnp.pad(w, [(0, 0)] * (w.ndim - 1) + [(0, LANE - N_SSM_HEADS)])


def _group_unpad_cols(w):
    return w[..., :N_SSM_HEADS]


def _row(v):
    return v.reshape(1, -1)


ROW_SHARDED = ('ssm_out_w', 'cf_pw2_w', 'xa_q_w', 'xa_o_w', 'ffn_out_w')
COL_SHARDED = ('cf_pw1_w', 'xa_kv_w', 'ffn_in_w')


MIXER_WEIGHTS = ('ssm_in_w', 'ssm_out_w', 'cf_pw1_w', 'cf_pw2_w')


def _layer_matmul_weights(i, part):
    if part == "rest":
        return ('xa_q_w', 'xa_kv_w', 'xa_o_w', 'ffn_in_w', 'ffn_out_w')
    mixer = ('ssm_in_w', 'ssm_out_w') if i % 2 == 0 else ('cf_pw1_w', 'cf_pw2_w')
    return {"mixer": mixer, "mixer_in": mixer[:1], "mixer_out": mixer[1:]}[part]


def _device_step(x, mem, target, W, layer_weights, layer_grads, start_after=()):
    ng = W['norm_g']
    lw = []
    for i in range(DEPTH):
        j = i // 2
        p = {}
        if i % 2 == 0:
            p['cw'] = _pad_taps(W['ssm_conv_w'][j], SSM_CONV)
            p['cb'] = _row(W['ssm_conv_b'][j])
            hp = jnp.stack([_group_pad_cols(W['ssm_dt_bias'][j]), _group_pad_cols(W['ssm_A_log'][j]),
                            _group_pad_cols(W['ssm_D'][j])])
            p['hp'] = jnp.pad(hp, ((0, SUBLANE - 3), (0, 0)))
            p['sng'] = _row(W['ssm_norm_g'][j])
        else:
            p['pw1b'] = _row(W['cf_pw1_b'][j])
            p['dww'], p['dwb'] = _pad_taps(W['cf_dw_w'][j], CF_KERNEL), _row(W['cf_dw_b'][j])
            p['lng'], p['lnb'] = _row(W['cf_ln_g'][j]), _row(W['cf_ln_b'][j])
            p['pw2b'] = _row(W['cf_pw2_b'][j])
        p['memg'] = _row(W['xa_mem_g'][i])
        p['fcw'], p['fcb'] = _pad_taps(W['ffn_conv_w'][i], FFN_CONV), _row(W['ffn_conv_b'][i])
        p['g'] = [_row(ng[i, s]) for s in range(6)]
        lw.append(p)

    def wmm(a, wl, wname, mode, **kw):
        return _mm(a, wl[wname], mode, b_shards=wname in COL_SHARDED, **kw)

    saved = []
    X = x
    h = _rmsnorm_fwd(X, lw[0]['g'][0], name="norm_in", after=start_after)
    for i in range(DEPTH):
        p, sv = lw[i], {}
        wl = dict(layer_weights(i, "mixer_in", X))
        sv['X0'], sv['h'], sv['wl'] = X, h, wl
        if i % 2 == 0:
            win = jnp.concatenate([wl['ssm_in_w'][s] for s in range(N_CHIPS)], axis=1)
            wl['wz'], wl['wx'] = win[:, :D_INNER], win[:, D_INNER:D_INNER + CONV_DIM]
            wl['wdt'] = _group_pad_cols(win[:, D_INNER + CONV_DIM:])
            z = _mm(h, wl['wz'], "nn", name="ssm_z")
            xbc = _mm(h, wl['wx'], "nn", name="ssm_xbc")
            dtp = _mm(h, wl['wdt'], "nn", name="ssm_dt")
            act = _conv_act_fwd(xbc, p['cw'], p['cb'], K=SSM_CONV, act="silu", name="ssm_conv_fwd", out_dtype=f32)
            y, hs = _ssd_fwd(act, dtp, p['hp'], name="ssd_fwd")
            yn = _gated_norm_fwd(y, z, p['sng'], name="ssm_gnorm_fwd")
            wl.update(layer_weights(i, "mixer_out", yn))
            mix = wmm(yn, wl, 'ssm_out_w', "nn", name="ssm_out")
            sv.update(z=z, xbc=xbc, dtp=dtp, act=act, y=y, hs=hs, yn=yn)
        else:
            u = wmm(h, wl, 'cf_pw1_w', "nn", name="cf_pw1", bias=p['pw1b'])
            c, s = _cf_fwd(u, p['dww'], p['dwb'], p['lng'], p['lnb'], name="cf_conv_fwd")
            wl.update(layer_weights(i, "mixer_out", s))
            mix = wmm(s, wl, 'cf_pw2_w', "nn", name="cf_pw2", bias=p['pw2b'])
            sv.update(u=u, c=c, s=s)
        wl.update(layer_weights(i, "rest", mix))
        X1, h2 = _resid_norm_fwd(X, mix, p['g'][1], p['g'][2], name="resid_norm_a")
        q = wmm(h2, wl, 'xa_q_w', "nn", name="xa_q", out_dtype=bf16)
        m = _rmsnorm_fwd(mem, p['memg'], name="xa_mem_norm")
        kv = wmm(m, wl, 'xa_kv_w', "nn", name="xa_kv", out_dtype=bf16)
        o = _attn_fwd(q, kv, name="xa_attn_fwd")
        a = wmm(o, wl, 'xa_o_w', "nn", name="xa_o")
        X2, h3 = _resid_norm_fwd(X1, a, p['g'][3], p['g'][4], name="resid_norm_b")
        u0 = wmm(h3, wl, 'ffn_in_w', "nn", name="ffn_in")
        fact = _conv_act_fwd(u0, p['fcw'], p['fcb'], K=FFN_CONV, act="swiglu", name="ffn_conv_fwd", out_dtype=bf16)
        f = wmm(fact, wl, 'ffn_out_w', "nn", name="ffn_out")
        g_next = lw[i + 1]['g'][0] if i + 1 < DEPTH else None
        X3, hn = _resid_norm_fwd(X2, f, p['g'][5], g_next, name="resid_norm_c" if g_next is not None else "resid_norm_last")
        sv.update(mix=mix, X1=X1, h2=h2, q=q, m=m, kv=kv, o=o, a=a, X2=X2, h3=h3, u0=u0, fact=fact, f=f)
        saved.append(sv)
        X, h = X3, hn

    sse, G = _loss_fwd_bwd(X, target, name="loss")

    small = [n for n in WEIGHT_NAMES if n not in MATMUL_WEIGHTS]
    gr = {n: [None] * W[n].shape[0] for n in small}

    def dwmm(gl, a, d, wname, *, name):
        if wname in COL_SHARDED:
            gl[wname] = _mm(a, d, "tn", name=name, out_dtype=bf16, out_shards=True)
        else:
            g = _mm(a, d, "tn", name=name, out_dtype=bf16)
            gl[wname] = g.reshape(N_CHIPS, g.shape[0] // N_CHIPS, g.shape[1])

    dng = [[None] * 6 for _ in range(DEPTH)]
    df = None
    for i in reversed(range(DEPTH)):
        p, sv, j = lw[i], saved[i], i // 2
        wl, gl = sv['wl'], {}
        if df is None:
            df, dng[i][5], _ = _norm_bwd(sv['f'], p['g'][5], G, name="nb_f", out_dtype=bf16)
        dwmm(gl, sv['fact'], df, 'ffn_out_w', name="ffn_out_dw")
        dfact = wmm(df, wl, 'ffn_out_w', "nt", name="ffn_out_dx")
        du0, dcw, dcb = _conv_act_bwd(sv['u0'], [dfact], p['fcw'], p['fcb'], K=FFN_CONV, act="swiglu", name="ffn_conv_bwd")
        gr['ffn_conv_w'][i], gr['ffn_conv_b'][i] = dcw[:FFN_CONV], dcb[0]
        dwmm(gl, sv['h3'], du0, 'ffn_in_w', name="ffn_in_dw")
        dh3 = wmm(du0, wl, 'ffn_in_w', "nt", name="ffn_in_dx")
        G, dng[i][4], da, dng[i][3], _ = _norm_bwd_chain(sv['X2'], p['g'][4], dh3, G, sv['a'], p['g'][3], name="nb_x2_a")
        dwmm(gl, sv['o'], da, 'xa_o_w', name="xa_o_dw")
        do = wmm(da, wl, 'xa_o_w', "nt", name="xa_o_dx", out_dtype=bf16)
        dq, dkv = _attn_bwd(sv['q'], sv['kv'], do, name="xa_attn_bwd")
        dwmm(gl, sv['h2'], dq, 'xa_q_w', name="xa_q_dw")
        dh2 = wmm(dq, wl, 'xa_q_w', "nt", name="xa_q_dx")
        dwmm(gl, sv['m'], dkv, 'xa_kv_w', name="xa_kv_dw")
        dm = wmm(dkv, wl, 'xa_kv_w', "nt", name="xa_kv_dx")
        _, dmg, _ = _norm_bwd(mem, p['memg'], dm, name="nb_mem")
        gr['xa_mem_g'][i] = dmg[0]
        behind = tuple(layer_grads(i, "rest", gl))
        G, dng[i][2], dmix, dng[i][1], dmix_sum = _norm_bwd_chain(sv['X1'], p['g'][2], dh2, G, sv['mix'], p['g'][1],
                                                                  name="nb_x1_mix", after=behind)
        if i % 2 == 0:
            dwmm(gl, sv['yn'], dmix, 'ssm_out_w', name="ssm_out_dw")
            dyn = wmm(dmix, wl, 'ssm_out_w', "nt", name="ssm_out_dx")
            dy, dz, dsng = _gated_norm_bwd(sv['y'], sv['z'], p['sng'], dyn, name="ssm_gnorm_bwd")
            gr['ssm_norm_g'][j] = dsng[0]
            dxs, dB, dC, ddtp, dhp = _ssd_bwd(sv['act'], sv['dtp'], p['hp'], dy, sv['hs'], name="ssd_bwd")
            gr['ssm_dt_bias'][j], gr['ssm_A_log'][j], gr['ssm_D'][j] = (_group_unpad_cols(dhp[r]) for r in range(3))
            dxbc, dcw, dcb = _conv_act_bwd(sv['xbc'], [dxs, dB, dC], p['cw'], p['cb'], K=SSM_CONV, act="silu",
                                           name="ssm_conv_bwd")
            gr['ssm_conv_w'][j], gr['ssm_conv_b'][j] = dcw[:SSM_CONV], dcb[0]
            hh = sv['h']
            dwz = _mm(dz, hh, "tn", name="ssm_z_dw", out_dtype=bf16)
            dwx = _mm(dxbc, hh, "tn", name="ssm_xbc_dw", out_dtype=bf16)
            dwdt = _mm(ddtp, hh, "tn", name="ssm_dt_dw", out_dtype=bf16)
            din = jnp.concatenate([dwz, dwx, dwdt[:N_SSM_HEADS]], axis=0)
            gl['ssm_in_w'] = din.reshape(N_CHIPS, din.shape[0] // N_CHIPS, din.shape[1])
            behind = tuple(layer_grads(i, "mixer", gl))
            dh = _mm(dz, wl['wz'], "nt", name="ssm_z_dx", after=behind)
            dh = _mm(dxbc, wl['wx'], "nt", name="ssm_xbc_dx", add=dh)
            dh = _mm(ddtp, wl['wdt'], "nt", name="ssm_dt_dx", add=dh)
        else:
            dwmm(gl, sv['s'], dmix, 'cf_pw2_w', name="cf_pw2_dw")
            gr['cf_pw2_b'][j] = dmix_sum[0]
            ds = wmm(dmix, wl, 'cf_pw2_w', "nt", name="cf_pw2_dx")
            dc, dlg, dlb = _cf_ln_bwd(sv['c'], p['lng'], p['lnb'], ds, name="cf_ln_bwd")
            gr['cf_ln_g'][j], gr['cf_ln_b'][j] = dlg[0], dlb[0]
            du, ddw, ddb, dus = _cf_glu_bwd(sv['u'], dc, p['dww'], name="cf_glu_bwd")
            gr['cf_dw_w'][j], gr['cf_dw_b'][j], gr['cf_pw1_b'][j] = ddw[:CF_KERNEL], ddb[0], dus[0]
            dwmm(gl, sv['h'], du, 'cf_pw1_w', name="cf_pw1_dw")
            behind = tuple(layer_grads(i, "mixer", gl))
            dh = wmm(du, wl, 'cf_pw1_w', "nt", name="cf_pw1_dx", after=behind)
        if i > 0:
            G, dng[i][0], df, dng[i - 1][5], _ = _norm_bwd_chain(sv['X0'], p['g'][0], dh, G, saved[i - 1]['f'],
                                                                 lw[i - 1]['g'][5], name="nb_x0_f")
        else:
            G, dng[i][0], _ = _norm_bwd(sv['X0'], p['g'][0], dh, name="nb_x0", add=G)
    gr['norm_g'] = [jnp.concatenate(dng[i], axis=0) for i in range(DEPTH)]
    gsmall = {n: jnp.stack(gr[n]) for n in small}
    return sse, G, gsmall


MESH = pl.DeviceIdType.MESH
HBM_SPEC = pl.BlockSpec(memory_space=pltpu.HBM)


def _chip_peers(x, y):
    return [(1 - x, y), (x, 1 - y), (1 - x, 1 - y)]


def _all_gather_chips(buf, *, name):
    R, C = buf.shape

    def body(in_ref, out_ref, send_sems, recv_sems, local_sem):
        x, y, c = lax.axis_index("x"), lax.axis_index("y"), lax.axis_index("c")
        me = 2 * x + y
        mine = pltpu.make_async_copy(in_ref, out_ref.at[me], local_sem)
        mine.start()
        peers = _chip_peers(x, y)
        sends = []
        for k, (px, py) in enumerate(peers):
            cp = pltpu.make_async_remote_copy(src_ref=in_ref, dst_ref=out_ref.at[me], send_sem=send_sems.at[k],
                                              recv_sem=recv_sems.at[k], device_id=(px, py, c), device_id_type=MESH)
            cp.start()
            sends.append(cp)
        for k, (px, py) in enumerate(peers):
            pltpu.make_async_remote_copy(src_ref=in_ref, dst_ref=out_ref.at[2 * px + py], send_sem=send_sems.at[k],
                                         recv_sem=recv_sems.at[k], device_id=(px, py, c), device_id_type=MESH).wait_recv()
        for cp in sends:
            cp.wait_send()
        mine.wait()

    return pl.pallas_call(body, name=name, out_shape=S((N_CHIPS, R, C), buf.dtype), in_specs=[HBM_SPEC], out_specs=HBM_SPEC,
                          scratch_shapes=[pltpu.SemaphoreType.DMA((3,)), pltpu.SemaphoreType.DMA((3,)),
                                          pltpu.SemaphoreType.DMA(())])(buf)


def _remote(src, dst, send_sem, recv_sem, device):
    return pltpu.make_async_remote_copy(src_ref=src, dst_ref=dst, send_sem=send_sem, recv_sem=recv_sem,
                                        device_id=device, device_id_type=MESH)


def _gather_matmul_weights(shards, *, name):
    n = len(shards)

    def body(*refs):
        ins, outs = refs[:n], refs[n:2 * n]
        send, recv, fsend, frecv, lsem = refs[2 * n:]
        x, y, c = lax.axis_index("x"), lax.axis_index("y"), lax.axis_index("c")
        me, sib = 2 * x + y, (x, y, 1 - c)
        peers = _chip_peers(x, y)
        started, local = [], []
        for w in range(n):
            cp = pltpu.make_async_copy(ins[w], outs[w].at[:, me], lsem.at[w])
            cp.start()
            local.append(cp)
            for k, (px, py) in enumerate(peers):
                cp = _remote(ins[w].at[:, c], outs[w].at[:, me, c], send.at[w, k], recv.at[w, k], (px, py, c))
                cp.start()
                started.append(cp)
        for w in range(n):
            for k, (px, py) in enumerate(peers):
                landed = outs[w].at[:, 2 * px + py, c]
                _remote(ins[w].at[:, c], landed, send.at[w, k], recv.at[w, k], (px, py, c)).wait_recv()
                cp = _remote(landed, landed, fsend.at[w, k], frecv.at[w, k], sib)
                cp.start()
                started.append(cp)
        for w in range(n):
            for k, (px, py) in enumerate(peers):
                _remote(ins[w].at[:, c], outs[w].at[:, 2 * px + py, 1 - c], fsend.at[w, k], frecv.at[w, k], sib).wait_recv()
        for cp in started:
            cp.wait_send()
        for cp in local:
            cp.wait()

    out_shape = tuple(S((s.shape[0], N_CHIPS) + s.shape[1:], s.dtype) for s in shards)
    sems = [pltpu.SemaphoreType.DMA((n, 3)) for _ in range(4)] + [pltpu.SemaphoreType.DMA((n,))]
    return pl.pallas_call(body, name=name, out_shape=out_shape, in_specs=[HBM_SPEC] * n, out_specs=(HBM_SPEC,) * n,
                          scratch_shapes=sems)(*shards)


SEM_SPEC = pl.BlockSpec(memory_space=pltpu.SEMAPHORE)
VMEM_SPEC = pl.BlockSpec(memory_space=pltpu.VMEM)


def _in_hbm(a):
    return pltpu.with_memory_space_constraint(a, pltpu.HBM)


def _chip_targets(x, y):
    return [(x, y), (1 - x, y), (x, 1 - y), (1 - x, 1 - y)]


def _spread_start(srcs, scatter, *, name, after=()):
    n = len(srcs)
    lands = [lax.empty((N_CHIPS,) + (s.shape[1:] if scatter else s.shape), s.dtype) for s in srcs]

    def body(*refs):
        src, land = refs[:n], refs[n:2 * n]
        send, recv, token = refs[2 * n + len(after)], refs[2 * n + len(after) + 1], refs[-1]
        x, y, c = lax.axis_index("x"), lax.axis_index("y"), lax.axis_index("c")
        me = 2 * x + y
        for w in range(n):
            for k, (px, py) in enumerate(_chip_targets(x, y)):
                block = src[w].at[2 * px + py] if scatter else src[w]
                _remote(block, land[w].at[me], send.at[N_CHIPS * w + k], recv.at[N_CHIPS * w + k], (px, py, c)).start()
        token[...] = jnp.zeros_like(token)

    thru = tuple(pltpu.HBM(a.shape, a.dtype) for a in list(srcs) + lands)
    sems = (pltpu.SemaphoreType.DMA((N_CHIPS * n,)), pltpu.SemaphoreType.DMA((N_CHIPS * n,)))
    out = pl.pallas_call(
        body, name=name, out_shape=sems + thru + (S((SUBLANE, LANE), f32),),
        in_specs=[HBM_SPEC] * (2 * n) + [ANY_SPEC] * len(after),
        out_specs=(SEM_SPEC, SEM_SPEC) + (HBM_SPEC,) * (2 * n) + (VMEM_SPEC,),
        input_output_aliases={i: 2 + i for i in range(2 * n)},
        compiler_params=pltpu.CompilerParams(has_side_effects=pltpu.SideEffectType.DATAFLOW_SIDE_EFFECTING),
    )(*[_in_hbm(a) for a in list(srcs) + lands], *after)
    return out[0], out[1], out[2:2 + n], out[2 + n:2 + 2 * n], out[-1]


def _spread_wait(send, recv, srcs, lands, after, scatter, *, name):
    n = len(srcs)

    def body(*refs):
        src, land, send, recv = refs[:n], refs[n:2 * n], refs[2 * n], refs[2 * n + 1]
        x, y, c = lax.axis_index("x"), lax.axis_index("y"), lax.axis_index("c")
        me = 2 * x + y
        for w in range(n):
            for k, (px, py) in enumerate(_chip_targets(x, y)):
                block = src[w].at[me] if scatter else src[w]
                cp = _remote(block, land[w].at[2 * px + py], send.at[N_CHIPS * w + k], recv.at[N_CHIPS * w + k], (px, py, c))
                cp.wait_send()
                cp.wait_recv()

    thru = tuple(pltpu.HBM(a.shape, a.dtype) for a in list(srcs) + list(lands))
    out = pl.pallas_call(
        body, name=name, out_shape=thru,
        in_specs=[HBM_SPEC] * (2 * n) + [SEM_SPEC, SEM_SPEC] + [ANY_SPEC] * len(after), out_specs=(HBM_SPEC,) * (2 * n),
        input_output_aliases={i: i for i in range(2 * n)},
        compiler_params=pltpu.CompilerParams(has_side_effects=pltpu.SideEffectType.DATAFLOW_SIDE_EFFECTING),
    )(*srcs, *lands, send, recv, *after)
    return out[:n], out[n:]


def _swap_start(bufs, *, name):
    n = len(bufs)
    lands = [lax.empty(b.shape, b.dtype) for b in bufs]

    def body(*refs):
        src, land, send, recv, token = refs[:n], refs[n:2 * n], refs[2 * n], refs[2 * n + 1], refs[-1]
        sib = (lax.axis_index("x"), lax.axis_index("y"), 1 - lax.axis_index("c"))
        for w in range(n):
            _remote(src[w], land[w], send.at[w], recv.at[w], sib).start()
        token[...] = jnp.zeros_like(token)

    thru = tuple(pltpu.HBM(a.shape, a.dtype) for a in list(bufs) + lands)
    out = pl.pallas_call(
        body, name=name,
        out_shape=(pltpu.SemaphoreType.DMA((n,)), pltpu.SemaphoreType.DMA((n,))) + thru + (S((SUBLANE, LANE), f32),),
        in_specs=[HBM_SPEC] * (2 * n), out_specs=(SEM_SPEC, SEM_SPEC) + (HBM_SPEC,) * (2 * n) + (VMEM_SPEC,),
        input_output_aliases={i: 2 + i for i in range(2 * n)},
        compiler_params=pltpu.CompilerParams(has_side_effects=pltpu.SideEffectType.DATAFLOW_SIDE_EFFECTING),
    )(*[_in_hbm(a) for a in list(bufs) + lands])
    return out[0], out[1], out[2:2 + n], out[2 + n:2 + 2 * n], out[-1]


def _swap_wait(send, recv, bufs, lands, after, *, name):
    n = len(bufs)

    def body(*refs):
        src, land, send, recv = refs[:n], refs[n:2 * n], refs[2 * n], refs[2 * n + 1]
        sib = (lax.axis_index("x"), lax.axis_index("y"), 1 - lax.axis_index("c"))
        for w in range(n):
            cp = _remote(src[w], land[w], send.at[w], recv.at[w], sib)
            cp.wait_send()
            cp.wait_recv()

    thru = tuple(pltpu.HBM(a.shape, a.dtype) for a in list(bufs) + list(lands))
    out = pl.pallas_call(
        body, name=name, out_shape=thru,
        in_specs=[HBM_SPEC] * (2 * n) + [SEM_SPEC, SEM_SPEC] + [ANY_SPEC] * len(after), out_specs=(HBM_SPEC,) * (2 * n),
        input_output_aliases={i: i for i in range(2 * n)},
        compiler_params=pltpu.CompilerParams(has_side_effects=pltpu.SideEffectType.DATAFLOW_SIDE_EFFECTING),
    )(*bufs, *lands, send, recv, *after)
    return out[:n], out[n:]


N_DEVICES = 8


def _device_targets(x, y, c):
    flips = [(d >> 2 & 1, d >> 1 & 1, d & 1) for d in range(N_DEVICES)]
    return [(1 - x if fx else x, 1 - y if fy else y, 1 - c if fc else c) for fx, fy, fc in flips]


def _allgather_devices_start(buf, *, name):
    land = lax.empty((N_DEVICES,) + buf.shape, buf.dtype)

    def body(src, land, send, recv, src_thru, land_thru, token):
        x, y, c = lax.axis_index("x"), lax.axis_index("y"), lax.axis_index("c")
        me = 4 * x + 2 * y + c
        for k, peer in enumerate(_device_targets(x, y, c)):
            _remote(src, land.at[me], send.at[k], recv.at[k], peer).start()
        token[...] = jnp.zeros_like(token)

    return pl.pallas_call(
        body, name=name,
        out_shape=(pltpu.SemaphoreType.DMA((N_DEVICES,)), pltpu.SemaphoreType.DMA((N_DEVICES,)), pltpu.HBM(buf.shape, buf.dtype),
                   pltpu.HBM(land.shape, land.dtype), S((SUBLANE, LANE), f32)),
        in_specs=[HBM_SPEC, HBM_SPEC], out_specs=(SEM_SPEC, SEM_SPEC, HBM_SPEC, HBM_SPEC, VMEM_SPEC),
        input_output_aliases={0: 2, 1: 3},
        compiler_params=pltpu.CompilerParams(has_side_effects=pltpu.SideEffectType.DATAFLOW_SIDE_EFFECTING),
    )(_in_hbm(buf), _in_hbm(land))


def _allgather_devices_wait(send, recv, buf, land, after, *, name):
    def body(src, land, send, recv, *rest):
        x, y, c = lax.axis_index("x"), lax.axis_index("y"), lax.axis_index("c")
        for k, (px, py, pc) in enumerate(_device_targets(x, y, c)):
            cp = _remote(src, land.at[4 * px + 2 * py + pc], send.at[k], recv.at[k], (px, py, pc))
            cp.wait_send()
            cp.wait_recv()

    out = pl.pallas_call(
        body, name=name, out_shape=(pltpu.HBM(buf.shape, buf.dtype), pltpu.HBM(land.shape, land.dtype)),
        in_specs=[HBM_SPEC, HBM_SPEC, SEM_SPEC, SEM_SPEC] + [ANY_SPEC] * len(after), out_specs=(HBM_SPEC, HBM_SPEC),
        input_output_aliases={0: 0, 1: 1},
        compiler_params=pltpu.CompilerParams(has_side_effects=pltpu.SideEffectType.DATAFLOW_SIDE_EFFECTING),
    )(buf, land, send, recv, *after)
    return out[1]


def _sum_slots(buf, *, name):
    ns, R, C = buf.shape
    tr = _pick(R, 512)
    assert R % tr == 0

    def body(*refs):
        acc = refs[0][...]
        for r in refs[1:ns]:
            acc = acc + r[...]
        refs[ns][...] = acc

    specs = [pl.BlockSpec((None, tr, C), functools.partial(lambda s, i: (s, i, 0), s)) for s in range(ns)]
    return pl.pallas_call(body, name=name, out_shape=S((R, C), buf.dtype), grid=(R // tr,), in_specs=specs,
                          out_specs=pl.BlockSpec((tr, C), lambda i: (i, 0)), compiler_params=_cp("parallel"))(*([buf] * ns))


ADAMW_BLOCK_BYTES = 1 << 20


def _adamw(w, m, v, groups, *, name, layer=None, prev=None):
    shape = w.shape if layer is None else w.shape[1:]
    C = shape[-1]
    Rr = math.prod(shape[:-1])
    tr, tc = Rr, C
    if Rr * C * 4 > ADAMW_BLOCK_BYTES:
        rows = [t for t in range(2 * SUBLANE, Rr + 1, 2 * SUBLANE) if Rr % t == 0 and t * C * 4 <= ADAMW_BLOCK_BYTES]
        if rows:
            tr = max(rows)
        else:
            tc = max(t for t in range(LANE, C + 1, LANE) if C % t == 0 and Rr * t * 4 <= ADAMW_BLOCK_BYTES)
    c1 = 1.0 / (1.0 - ADAM_B1 ** ADAM_STEP)
    c2 = 1.0 / (1.0 - ADAM_B2 ** ADAM_STEP)
    if layer is None:
        to2 = lambda t: t.reshape(Rr, C)
        spec = pl.BlockSpec((tr, tc), lambda i, j: (i, j))
        res_shape = S((Rr, C), f32)
    else:
        to2 = lambda t: t.reshape(layer[1], Rr, C)
        spec = pl.BlockSpec((None, tr, tc), functools.partial(lambda l, i, j: (l, i, j), layer[0]))
        res_shape = S((layer[1], Rr, C), f32)
    wspec, spec = spec, pl.BlockSpec((tr, tc), lambda i, j: (i, j))
    g_specs, g_args, sizes = [], [], []
    for grp in groups:
        sizes.append(len(grp))
        for term in grp:
            if isinstance(term, tuple):
                arr, slot = term
                g_specs.append(pl.BlockSpec((None, tr, tc), functools.partial(lambda s, i, j: (s, i, j), slot)))
                g_args.append(arr.reshape(arr.shape[0], Rr, C))
            else:
                g_specs.append(spec)
                g_args.append(term.reshape(Rr, C))
    nterms = len(g_args)
    prev = () if prev is None else tuple(to2(t) for t in prev)

    def body(w_ref, m_ref, v_ref, *rest):
        t_refs, (g_ref, d_ref, mo_ref, vo_ref) = rest[:nterms], rest[-4:]
        g, pos = None, 0
        for size in sizes:
            part = None
            for r in t_refs[pos:pos + size]:
                t = r[...].astype(f32)
                part = t if part is None else part + t
            pos += size
            g = part if g is None else g + part
        mn = ADAM_B1 * m_ref[...] + (1.0 - ADAM_B1) * g
        vn = ADAM_B2 * v_ref[...] + (1.0 - ADAM_B2) * (g * g)
        g_ref[...] = g
        mo_ref[...] = mn
        vo_ref[...] = vn
        d_ref[...] = -ADAM_LR * ((mn * c1) / (jnp.sqrt(vn * c2) + ADAM_EPS) + ADAM_WD * w_ref[...])

    out = pl.pallas_call(body, name=name, out_shape=(res_shape,) * 4, grid=(Rr // tr, C // tc),
                         in_specs=[wspec] * 3 + g_specs + [ANY_SPEC] * len(prev), out_specs=(wspec,) * 4,
                         input_output_aliases={3 + nterms + k: k for k in range(len(prev))},
                         compiler_params=_cp("parallel", "parallel"))(to2(w), to2(m), to2(v), *g_args, *prev)
    return tuple(o.reshape(w.shape) for o in out)


def _pack_rows(parts, dtype):
    flat = jnp.concatenate([p.reshape(-1).astype(dtype) for p in parts])
    n = flat.shape[0]
    unit = PACK_COLS * 2 * SUBLANE
    padded = -(-n // unit) * unit
    return jnp.pad(flat, (0, padded - n)).reshape(padded // PACK_COLS, PACK_COLS)


def _unpack_rows(flat2d, shapes):
    flat = flat2d.reshape(-1)
    out, off = [], 0
    for shp in shapes:
        n = math.prod(shp)
        out.append(flat[off:off + n].reshape(shp))
        off += n
    return out


def _gather_weights(local, names, dtype, *, name):
    shapes = [local[n].shape for n in names]
    got = _all_gather_chips(_pack_rows([local[n] for n in names], dtype), name=name)
    per_chip = [_unpack_rows(got[s], shapes) for s in range(N_CHIPS)]
    return {n: jnp.concatenate([per_chip[s][k] for s in range(N_CHIPS)], axis=SHARD_AXIS[n]) for k, n in enumerate(names)}


def kernel(x, mem, norm_g, ssm_in_w, ssm_conv_w, ssm_conv_b, ssm_dt_bias, ssm_A_log, ssm_D, ssm_norm_g, ssm_out_w, cf_pw1_w, cf_pw1_b, cf_dw_w, cf_dw_b, cf_ln_g, cf_ln_b, cf_pw2_w, cf_pw2_b, xa_mem_g, xa_q_w, xa_kv_w, xa_o_w, ffn_in_w, ffn_conv_w, ffn_conv_b, ffn_out_w, loss_target, m_norm_g, m_ssm_in_w, m_ssm_conv_w, m_ssm_conv_b, m_ssm_dt_bias, m_ssm_A_log, m_ssm_D, m_ssm_norm_g, m_ssm_out_w, m_cf_pw1_w, m_cf_pw1_b, m_cf_dw_w, m_cf_dw_b, m_cf_ln_g, m_cf_ln_b, m_cf_pw2_w, m_cf_pw2_b, m_xa_mem_g, m_xa_q_w, m_xa_kv_w, m_xa_o_w, m_ffn_in_w, m_ffn_conv_w, m_ffn_conv_b, m_ffn_out_w, v_norm_g, v_ssm_in_w, v_ssm_conv_w, v_ssm_conv_b, v_ssm_dt_bias, v_ssm_A_log, v_ssm_D, v_ssm_norm_g, v_ssm_out_w, v_cf_pw1_w, v_cf_pw1_b, v_cf_dw_w, v_cf_dw_b, v_cf_ln_g, v_cf_ln_b, v_cf_pw2_w, v_cf_pw2_b, v_xa_mem_g, v_xa_q_w, v_xa_kv_w, v_xa_o_w, v_ffn_in_w, v_ffn_conv_w, v_ffn_conv_b, v_ffn_out_w):
    w_local = dict(zip(WEIGHT_NAMES, (norm_g, ssm_in_w, ssm_conv_w, ssm_conv_b, ssm_dt_bias, ssm_A_log, ssm_D, ssm_norm_g,
                                      ssm_out_w, cf_pw1_w, cf_pw1_b, cf_dw_w, cf_dw_b, cf_ln_g, cf_ln_b, cf_pw2_w, cf_pw2_b,
                                      xa_mem_g, xa_q_w, xa_kv_w, xa_o_w, ffn_in_w, ffn_conv_w, ffn_conv_b, ffn_out_w)))
    m_local = dict(zip(WEIGHT_NAMES, (m_norm_g, m_ssm_in_w, m_ssm_conv_w, m_ssm_conv_b, m_ssm_dt_bias, m_ssm_A_log, m_ssm_D,
                                      m_ssm_norm_g, m_ssm_out_w, m_cf_pw1_w, m_cf_pw1_b, m_cf_dw_w, m_cf_dw_b, m_cf_ln_g,
                                      m_cf_ln_b, m_cf_pw2_w, m_cf_pw2_b, m_xa_mem_g, m_xa_q_w, m_xa_kv_w, m_xa_o_w,
                                      m_ffn_in_w, m_ffn_conv_w, m_ffn_conv_b, m_ffn_out_w)))
    v_local = dict(zip(WEIGHT_NAMES, (v_norm_g, v_ssm_in_w, v_ssm_conv_w, v_ssm_conv_b, v_ssm_dt_bias, v_ssm_A_log, v_ssm_D,
                                      v_ssm_norm_g, v_ssm_out_w, v_cf_pw1_w, v_cf_pw1_b, v_cf_dw_w, v_cf_dw_b, v_cf_ln_g,
                                      v_cf_ln_b, v_cf_pw2_w, v_cf_pw2_b, v_xa_mem_g, v_xa_q_w, v_xa_kv_w, v_xa_o_w,
                                      v_ffn_in_w, v_ffn_conv_w, v_ffn_conv_b, v_ffn_out_w)))

    small = [n for n in WEIGHT_NAMES if n not in MATMUL_WEIGHTS]
    small_sharded = [n for n in small if SHARD_AXIS[n] is not None]
    W = {n: w_local[n] for n in small if SHARD_AXIS[n] is None}
    W.update(_gather_weights(w_local, small_sharded, f32, name="gather_small_weights"))

    def layer_index(n, i):
        return i // 2 if n in MIXER_WEIGHTS else i

    def keys_of(i, parts):
        return [(n, layer_index(n, i)) for part in parts for n in _layer_matmul_weights(i, part)]

    def shards(keys):
        return [w_local[n][l].astype(bf16) for n, l in keys]

    def usable(n, a):
        return a.reshape(N_CHIPS * a.shape[1], a.shape[2]) if n in ROW_SHARDED else a

    mixer0 = keys_of(0, ("mixer_in",))
    got0 = _gather_matmul_weights([s.reshape(1, 2, s.shape[0] // 2, s.shape[1]) for s in shards(mixer0)],
                                  name="gather_layer0_mixer")
    gather_groups = {(0, "mixer_out"): keys_of(0, ("mixer_out",)), (0, "rest"): keys_of(0, ("rest",))}
    gather_groups.update({(i, "mixer_in"): keys_of(i, ("mixer", "rest")) for i in range(1, DEPTH)})
    gathers, tokens, landed = {}, [], {}
    for gkey in sorted(gather_groups):
        send, recv, srcs, lands, token = _spread_start(shards(gather_groups[gkey]), False, name="gather_start_%d_%s" % gkey,
                                                       after=(got0[0], W[small_sharded[0]]))
        gathers[gkey] = (send, recv, srcs, lands)
        tokens.append(token)

    def layer_weights(i, part, after):
        if (i, part) == (0, "mixer_in"):
            landed.update({k: g.reshape((N_CHIPS, 2 * g.shape[3], g.shape[4])) for k, g in zip(mixer0, got0)})
        elif (i, part) in gathers:
            _, lands = _spread_wait(*gathers[i, part], (after,), False, name="gather_wait_%d_%s" % (i, part))
            landed.update(zip(gather_groups[i, part], lands))
        return {n: usable(n, landed[n, layer_index(n, i)]) for n in _layer_matmul_weights(i, part)}

    pending, scatters, swaps, own, sib = {}, {}, {}, {}, {}

    def scatter_start(gkey, keys, gl):
        send, recv, srcs, lands, token = _spread_start([gl[k] for k in keys], True, name="grads_start_%d_%s" % gkey)
        scatters[gkey] = (keys, send, recv, srcs, lands)
        return token

    def layer_grads(i, part, gl):
        grads = {(n, layer_index(n, i)): gl[n] for n in _layer_matmul_weights(i, part)}
        behind = []
        if part == "rest":
            if i + 1 < DEPTH:
                keys, send, recv, srcs, lands = scatters.pop((i + 1, "mixer"))
                _, lands = _spread_wait(send, recv, srcs, lands, (grads['xa_kv_w', i],), True,
                                        name="grads_wait_%d" % (i + 1))
                send, recv, srcs, lands, token = _swap_start(lands, name="grads_swap_start_%d" % (i + 1))
                swaps[i + 1] = (keys, send, recv, srcs, lands)
                behind.append(token)
            if i == 0:
                behind.append(scatter_start((0, "rest"), list(grads), grads))
            else:
                pending.update(grads)
        else:
            pending.update(grads)
            if i == 0:
                keys, send, recv, srcs, lands = scatters.pop((0, "rest"))
                _, lands = _spread_wait(send, recv, srcs, lands, (grads['ssm_in_w', 0],), True, name="grads_wait_0_rest")
                send, recv, srcs, lands, token = _swap_start(lands, name="grads_swap_start_0")
                swaps[0] = (keys, send, recv, srcs, lands)
                behind.append(token)
            behind.append(scatter_start((i, "mixer"), list(pending), dict(pending)))
            pending.clear()
        return behind

    sse, gx, gsmall = _device_step(x[0], mem[0], loss_target[0], W, layer_weights, layer_grads, tuple(tokens))

    loss = lax.psum(0.5 * sse[0, 0] / D_MODEL, ("x", "y", "c"))

    small_shapes = [gsmall[n].shape for n in small]
    ag_send, ag_recv, ag_src, ag_land, ag_token = _allgather_devices_start(_pack_rows([gsmall[n] for n in small], f32),
                                                                           name="allgather_small_start")
    last_keys, last_lands = [], []
    for gkey in sorted(scatters):
        keys, send, recv, srcs, lands = scatters[gkey]
        _, lands = _spread_wait(send, recv, srcs, lands, (gx, ag_token), True, name="grads_wait_%d_%s" % gkey)
        last_keys += keys
        last_lands += list(lands)
    last_swap = _swap_start(last_lands, name="grads_swap_start_last")
    for i in sorted(swaps):
        keys, send, recv, srcs, lands = swaps[i]
        mine, theirs = _swap_wait(send, recv, srcs, lands, (gx, last_swap[4]), name="grads_swap_wait_%d" % i)
        own.update(zip(keys, mine))
        sib.update(zip(keys, theirs))

    res = {}

    def adamw_matmul_weight(n, l):
        layers = w_local[n].shape[0]
        groups = [[(own[n, l], s) for s in range(N_CHIPS)], [(sib[n, l], s) for s in range(N_CHIPS)]]
        view = (lambda t: jnp.swapaxes(t, 1, 2)) if n == 'ssm_in_w' else (lambda t: t)
        res[n] = _adamw(view(w_local[n]), view(m_local[n]), view(v_local[n]), groups, name="adamw_%s_%d" % (n, l),
                        layer=(l, layers), prev=res.get(n))

    for n in MATMUL_WEIGHTS:
        for l in range(w_local[n].shape[0]):
            if (n, l) not in last_keys:
                adamw_matmul_weight(n, l)
    done = res[MATMUL_WEIGHTS[-1]][0]
    mine, theirs = _swap_wait(*last_swap[:4], (done,), name="grads_swap_wait_last")
    own.update(zip(last_keys, mine))
    sib.update(zip(last_keys, theirs))
    for n, l in last_keys:
        adamw_matmul_weight(n, l)
    slots = _allgather_devices_wait(ag_send, ag_recv, ag_src, ag_land, (res[last_keys[-1][0]][0],), name="allgather_small_wait")
    gsum = dict(zip(small, _unpack_rows(_sum_slots(slots, name="sum_small_grads"), small_shapes)))
    chip = 2 * lax.axis_index("x") + lax.axis_index("y")

    for n in small:
        g, ax = gsum[n], SHARD_AXIS[n]
        if ax is not None:
            width = w_local[n].shape[ax]
            g = lax.dynamic_slice_in_dim(g, chip * width, width, axis=ax)
        res[n] = _adamw(w_local[n], m_local[n], v_local[n], [[g]], name="adamw_" + n)
    res['ssm_in_w'] = tuple(jnp.swapaxes(t, 1, 2) for t in res['ssm_in_w'])
    return (loss, gx[None], *[res[n][0] for n in WEIGHT_NAMES], *[res[n][1] for n in WEIGHT_NAMES],
            *[res[n][2] for n in WEIGHT_NAMES], *[res[n][3] for n in WEIGHT_NAMES])
```

```python
import functools
import math

import jax
import jax.numpy as jnp
from jax import lax
from jax.experimental import pallas as pl
from jax.experimental.pallas import tpu as pltpu

f32 = jnp.float32
bf16 = jnp.bfloat16
S = jax.ShapeDtypeStruct

D_MODEL = 1024
DEPTH = 4
D_INNER = 2048
HEAD_DIM = 64
N_GROUPS = 4
HEADS_PER_GROUP = 8
N_SSM_HEADS = 32
D_STATE = 128
CHUNK = 128
SSM_CONV = 4
CONV_DIM = 3072
CF_KERNEL = 31
N_MEM = 256
XA_HEADS = 4
XA_HEAD_DIM = 256
D_FF = 2816
FFN_CONV = 3
EPS = 1e-6
ADAM_LR, ADAM_B1, ADAM_B2, ADAM_EPS, ADAM_WD, ADAM_STEP = 0.001, 0.9, 0.999, 1e-08, 0.01, 10

LANE = 128
SUBLANE = 8
ROW_SUB = 64
LN_ROWS = 16
VMEM_LIMIT = 56 * 1024 * 1024
N_CHIPS = 4
PACK_COLS = 1024

WEIGHT_NAMES = ['norm_g', 'ssm_in_w', 'ssm_conv_w', 'ssm_conv_b', 'ssm_dt_bias', 'ssm_A_log', 'ssm_D', 'ssm_norm_g',
                'ssm_out_w', 'cf_pw1_w', 'cf_pw1_b', 'cf_dw_w', 'cf_dw_b', 'cf_ln_g', 'cf_ln_b', 'cf_pw2_w', 'cf_pw2_b',
                'xa_mem_g', 'xa_q_w', 'xa_kv_w', 'xa_o_w', 'ffn_in_w', 'ffn_conv_w', 'ffn_conv_b', 'ffn_out_w']
SHARD_AXIS = {'norm_g': 2, 'ssm_in_w': 2, 'ssm_conv_w': 2, 'ssm_conv_b': None, 'ssm_dt_bias': None, 'ssm_A_log': None,
              'ssm_D': None, 'ssm_norm_g': None, 'ssm_out_w': 1, 'cf_pw1_w': 2, 'cf_pw1_b': 1, 'cf_dw_w': 2, 'cf_dw_b': 1,
              'cf_ln_g': 1, 'cf_ln_b': 1, 'cf_pw2_w': 1, 'cf_pw2_b': 1, 'xa_mem_g': None, 'xa_q_w': 1, 'xa_kv_w': 2,
              'xa_o_w': 1, 'ffn_in_w': 2, 'ffn_conv_w': 2, 'ffn_conv_b': None, 'ffn_out_w': 1}
MATMUL_WEIGHTS = ('ssm_in_w', 'ssm_out_w', 'cf_pw1_w', 'cf_pw2_w', 'xa_q_w', 'xa_kv_w', 'xa_o_w', 'ffn_in_w', 'ffn_out_w')


def _cp(*sem):
    return pltpu.CompilerParams(dimension_semantics=tuple(sem), vmem_limit_bytes=VMEM_LIMIT)


def _pick(dim, pref):
    if dim <= pref:
        return dim
    best = None
    for t in range(LANE, pref + 1, LANE):
        if dim % t == 0:
            best = t
    assert best is not None, (dim, pref)
    return best


def _sigmoid(x):
    return 1.0 / (1.0 + jnp.exp(-x))


def _silu(x):
    return x * _sigmoid(x)


def _dsilu(x):
    s = _sigmoid(x)
    return s * (1.0 + x * (1.0 - s))


def _softplus(x):
    return jnp.maximum(x, 0.0) + jnp.log(1.0 + jnp.exp(-jnp.abs(x)))


_DN = {"nn": (((1,), (0,)), ((), ())), "nt": (((1,), (1,)), ((), ())), "tn": (((0,), (0,)), ((), ()))}


def _mm(a, b, mode, *, name, out_dtype=f32, bias=None, add=None, b_shards=False, out_shards=False, after=()):
    bshape = (b.shape[1], b.shape[2] * N_CHIPS) if b_shards else b.shape
    if mode == "nn":
        (M, K), (K2, N) = a.shape, bshape
    elif mode == "nt":
        (M, K), (N, K2) = a.shape, bshape
    else:
        (K, M), (K2, N) = a.shape, bshape
    assert K == K2, (a.shape, b.shape, mode)
    n_unit = N // N_CHIPS if ((b_shards and mode == "nn") or out_shards) else N
    k_unit = K // N_CHIPS if (b_shards and mode == "nt") else K
    tm, tn, tk = _pick(M, 1024), _pick(n_unit, 1408), _pick(k_unit, 1408)
    nk, nj_u, nk_u = K // tk, n_unit // tn, k_unit // tk
    a_spec = {"nn": pl.BlockSpec((tm, tk), lambda i, j, k: (i, k)), "nt": pl.BlockSpec((tm, tk), lambda i, j, k: (i, k)),
              "tn": pl.BlockSpec((tk, tm), lambda i, j, k: (k, i))}[mode]
    if not b_shards:
        b_spec = {"nn": pl.BlockSpec((tk, tn), lambda i, j, k: (k, j)), "nt": pl.BlockSpec((tn, tk), lambda i, j, k: (j, k)),
                  "tn": pl.BlockSpec((tk, tn), lambda i, j, k: (k, j))}[mode]
    else:
        b_spec = {"nn": pl.BlockSpec((None, tk, tn), lambda i, j, k: (j // nj_u, k, j % nj_u)),
                  "nt": pl.BlockSpec((None, tn, tk), lambda i, j, k: (k // nk_u, j, k % nk_u))}[mode]
    in_specs, args = [a_spec, b_spec], [a, b]
    if bias is not None:
        in_specs.append(pl.BlockSpec((1, tn), lambda i, j, k: (0, j)))
        args.append(bias)
    if add is not None:
        in_specs.append(pl.BlockSpec((tm, tn), lambda i, j, k: (i, j)))
        args.append(add)
    in_specs += [pl.BlockSpec(memory_space=pl.ANY)] * len(after)
    args += list(after)
    if not out_shards:
        out_shape, out_spec = S((M, N), out_dtype), pl.BlockSpec((tm, tn), lambda i, j, k: (i, j))
    else:
        out_shape = S((N_CHIPS, M, n_unit), out_dtype)
        out_spec = pl.BlockSpec((None, tm, tn), lambda i, j, k: (j // nj_u, i, j % nj_u))
    dn = _DN[mode]
    has_bias, has_add = bias is not None, add is not None

    def body(a_ref, b_ref, *rest):
        rest = list(rest)
        bias_ref = rest.pop(0) if has_bias else None
        add_ref = rest.pop(0) if has_add else None
        rest = rest[len(after):]
        o_ref = rest[0]

        def finish(r):
            if has_bias:
                r = r + bias_ref[...]
            if has_add:
                r = r + add_ref[...].astype(f32)
            o_ref[...] = r.astype(out_dtype)

        part = lax.dot_general(a_ref[...].astype(bf16), b_ref[...].astype(bf16), dn, preferred_element_type=f32)
        if nk == 1:
            finish(part)
            return
        acc_ref = rest[1]
        k = pl.program_id(2)

        @pl.when(k == 0)
        def _():
            acc_ref[...] = part

        @pl.when(k > 0)
        def _():
            acc_ref[...] += part

        @pl.when(k == nk - 1)
        def _():
            finish(acc_ref[...])

    return pl.pallas_call(
        body, name=name, out_shape=out_shape, grid=(M // tm, N // tn, nk),
        in_specs=in_specs, out_specs=out_spec, scratch_shapes=[pltpu.VMEM((tm, tn), f32)] if nk > 1 else [],
        compiler_params=_cp("parallel", "parallel", "arbitrary"))(*args)


def _rows(tm, C):
    return pl.BlockSpec((tm, C), lambda i: (i, 0))


def _const(shape):
    return pl.BlockSpec(shape, lambda i: tuple(0 for _ in shape))


def _rms_val(x, g):
    r = lax.rsqrt(jnp.mean(x * x, axis=-1, keepdims=True) + EPS)
    return x * r * g


def _rms_bwd_val(x, g, dy):
    r = lax.rsqrt(jnp.mean(x * x, axis=-1, keepdims=True) + EPS)
    xn = x * r
    dxh = dy * g
    dx = r * (dxh - xn * jnp.mean(dxh * xn, axis=-1, keepdims=True))
    return dx, jnp.sum(dy * xn, axis=0, keepdims=True)


ANY_SPEC = pl.BlockSpec(memory_space=pl.ANY)


def _rmsnorm_fwd(x, g, *, name, after=()):
    L, C = x.shape
    tm = _pick(L, 512)

    def body(x_ref, g_ref, *rest):
        rest[-1][...] = _rms_val(x_ref[...], g_ref[...]).astype(bf16)

    return pl.pallas_call(body, name=name, out_shape=S((L, C), bf16), grid=(L // tm,),
                          in_specs=[_rows(tm, C), _const((1, C))] + [ANY_SPEC] * len(after), out_specs=_rows(tm, C),
                          compiler_params=_cp("parallel"))(x, g, *after)


def _resid_norm_fwd(x, mix, g_post, g_next, *, name):
    L, C = x.shape
    tm = _pick(L, 512)
    want_h = g_next is not None

    def body(x_ref, m_ref, gp_ref, *rest):
        xn = x_ref[...] + _rms_val(m_ref[...], gp_ref[...])
        if want_h:
            gn_ref, xo_ref, h_ref = rest
            h_ref[...] = _rms_val(xn, gn_ref[...]).astype(bf16)
        else:
            (xo_ref,) = rest
        xo_ref[...] = xn

    in_specs = [_rows(tm, C), _rows(tm, C), _const((1, C))]
    args = [x, mix, g_post]
    out_shape, out_specs = [S((L, C), f32)], [_rows(tm, C)]
    if want_h:
        in_specs.append(_const((1, C)))
        args.append(g_next)
        out_shape.append(S((L, C), bf16))
        out_specs.append(_rows(tm, C))
    out = pl.pallas_call(body, name=name, out_shape=tuple(out_shape), grid=(L // tm,), in_specs=in_specs,
                         out_specs=tuple(out_specs), compiler_params=_cp("parallel"))(*args)
    return (out[0], out[1]) if want_h else (out[0], None)


def _norm_bwd(x, g, dy, *, name, add=None, out_dtype=f32, after=()):
    L, C = x.shape
    tm = _pick(L, 512)
    has_add = add is not None

    def body(x_ref, g_ref, dy_ref, *rest):
        rest = list(rest)
        add_ref = rest.pop(0) if has_add else None
        dx_ref, dg_ref, cs_ref = rest[-3:]
        i = pl.program_id(0)

        @pl.when(i == 0)
        def _():
            dg_ref[...] = jnp.zeros_like(dg_ref)
            cs_ref[...] = jnp.zeros_like(cs_ref)

        dx, dg = _rms_bwd_val(x_ref[...], g_ref[...], dy_ref[...].astype(f32))
        dg_ref[...] += dg
        cs_ref[...] += jnp.sum(dx, axis=0, keepdims=True)
        if has_add:
            dx = dx + add_ref[...]
        dx_ref[...] = dx.astype(out_dtype)

    in_specs = [_rows(tm, C), _const((1, C)), _rows(tm, C)]
    args = [x, g, dy]
    if has_add:
        in_specs.append(_rows(tm, C))
        args.append(add)
    in_specs += [ANY_SPEC] * len(after)
    args += list(after)
    return pl.pallas_call(body, name=name, out_shape=(S((L, C), out_dtype), S((1, C), f32), S((1, C), f32)),
                          grid=(L // tm,), in_specs=in_specs,
                          out_specs=(_rows(tm, C), _const((1, C)), _const((1, C))),
                          compiler_params=_cp("arbitrary"))(*args)


def _norm_bwd_chain(x, g, dy, add, x2, g2, *, name, after=()):
    L, C = x.shape
    tm = _pick(L, 512)

    def body(x_ref, g_ref, dy_ref, add_ref, x2_ref, g2_ref, *rest):
        G_ref, dg_ref, d2_ref, dg2_ref, cs2_ref = rest[-5:]
        i = pl.program_id(0)

        @pl.when(i == 0)
        def _():
            dg_ref[...] = jnp.zeros_like(dg_ref)
            dg2_ref[...] = jnp.zeros_like(dg2_ref)
            cs2_ref[...] = jnp.zeros_like(cs2_ref)

        dx, dg = _rms_bwd_val(x_ref[...], g_ref[...], dy_ref[...].astype(f32))
        G = dx + add_ref[...]
        dg_ref[...] += dg
        G_ref[...] = G
        d2, dg2 = _rms_bwd_val(x2_ref[...], g2_ref[...], G)
        dg2_ref[...] += dg2
        cs2_ref[...] += jnp.sum(d2, axis=0, keepdims=True)
        d2_ref[...] = d2.astype(bf16)

    row, vec = _rows(tm, C), _const((1, C))
    return pl.pallas_call(body, name=name,
                          out_shape=(S((L, C), f32), S((1, C), f32), S((L, C), bf16), S((1, C), f32), S((1, C), f32)),
                          grid=(L // tm,), in_specs=[row, vec, row, row, row, vec] + [ANY_SPEC] * len(after),
                          out_specs=(row, vec, row, vec, vec),
                          compiler_params=_cp("arbitrary"))(x, g, dy, add, x2, g2, *after)


def _loss_fwd_bwd(y, target, *, name):
    L, C = y.shape
    tm = _pick(L, 512)

    def body(y_ref, t_ref, acc_ref, dy_ref):
        i = pl.program_id(0)

        @pl.when(i == 0)
        def _():
            acc_ref[...] = jnp.zeros_like(acc_ref)

        e = y_ref[...] - t_ref[...]
        rs = jnp.sum(e * e, axis=-1, keepdims=True)
        acc_ref[...] += jnp.broadcast_to(jnp.sum(rs, axis=0, keepdims=True), (1, LANE))
        dy_ref[...] = e * (1.0 / C)

    return pl.pallas_call(body, name=name, out_shape=(S((1, LANE), f32), S((L, C), f32)), grid=(L // tm,),
                          in_specs=[_rows(tm, C), _rows(tm, C)], out_specs=(_const((1, LANE)), _rows(tm, C)),
                          compiler_params=_cp("arbitrary"))(y, target)


def _halo_rows(K):
    return SUBLANE if K - 1 <= SUBLANE else 32


def _prev_halo_spec(tm, H, C):
    return pl.BlockSpec((H, C), lambda i: (jnp.maximum(i * (tm // H) - 1, 0), 0))


def _next_halo_spec(tm, H, C, L):
    return pl.BlockSpec((H, C), lambda i: (jnp.minimum((i + 1) * (tm // H), L // H - 1), 0))


def _down_views(ext, H, n, K):
    bases, views = {}, []
    for s in range(K):
        q, r = divmod(s, SUBLANE)
        if r not in bases:
            bases[r] = ext if r == 0 else pltpu.roll(ext, r, axis=0)
        views.append(bases[r][H - SUBLANE * q:H - SUBLANE * q + n])
    return views


def _up_views(ext, n, K):
    bases, views = {}, []
    for s in range(K):
        q, r = divmod(s, SUBLANE)
        if r not in bases:
            bases[r] = ext if r == 0 else pltpu.roll(ext, ext.shape[0] - r, axis=0)
        views.append(bases[r][SUBLANE * q:SUBLANE * q + n])
    return views


def _causal_conv(ext, H, w_ref, K):
    views = _down_views(ext, H, ext.shape[0] - H, K)
    acc = None
    for k in range(K):
        term = views[K - 1 - k] * w_ref[k:k + 1, :]
        acc = term if acc is None else acc + term
    return acc


def _anticausal_conv(ext, tm, w_ref, K):
    views = _up_views(ext, tm, K)
    acc = None
    for k in range(K):
        term = views[K - 1 - k] * w_ref[k:k + 1, :]
        acc = term if acc is None else acc + term
    return acc


def _tap_grads(dw_ref, d_cur, x_ext, H, K):
    views = _down_views(x_ext, H, d_cur.shape[0], K)
    for k in range(K):
        dw_ref[k:k + 1, :] += jnp.sum(d_cur * views[K - 1 - k], axis=0, keepdims=True)


def _fold_rows(x):
    acc = x[0:SUBLANE]
    for r in range(SUBLANE, x.shape[0], SUBLANE):
        acc = acc + x[r:r + SUBLANE]
    return acc


def _pad_taps(w, K):
    return jnp.pad(w, ((0, _halo_rows(K) - K), (0, 0)))


def _conv_act_fwd(x, w, b, *, K, act, name, out_dtype, tm_pref=256):
    L, C = x.shape
    H = _halo_rows(K)
    tm = _pick(L, tm_pref)
    Co = C if act == "silu" else C // 2

    rs = min(ROW_SUB, tm)

    def body(h_ref, x_ref, w_ref, b_ref, o_ref, ext_scr):
        i = pl.program_id(0)
        ext_scr[0:H] = jnp.where(i > 0, h_ref[...], 0.0)
        ext_scr[H:] = x_ref[...]
        for j in range(Co // LANE):
            lanes = [j] if act == "silu" else [j, j + Co // LANE]
            wb = [(w_ref[:, c * LANE:(c + 1) * LANE], b_ref[:, c * LANE:(c + 1) * LANE]) for c in lanes]
            for r0 in range(0, tm, rs):
                us = [_causal_conv(ext_scr[r0:r0 + rs + H, c * LANE:(c + 1) * LANE], H, wc, K) + bc
                      for c, (wc, bc) in zip(lanes, wb)]
                y = _silu(us[0]) if act == "silu" else _silu(us[0]) * us[1]
                o_ref[r0:r0 + rs, j * LANE:(j + 1) * LANE] = y.astype(out_dtype)

    return pl.pallas_call(body, name=name, out_shape=S((L, Co), out_dtype), grid=(L // tm,),
                          in_specs=[_prev_halo_spec(tm, H, C), _rows(tm, C), _const((H, C)), _const((1, C))],
                          out_specs=_rows(tm, Co), scratch_shapes=[pltpu.VMEM((H + tm, C), f32)],
                          compiler_params=_cp("parallel"))(x, x, w, b)


def _conv_act_bwd(x, dparts, w, b, *, K, act, name, tm_pref=256):
    L, C = x.shape
    H = _halo_rows(K)
    tm = _pick(L, tm_pref)
    nb = L // tm
    Co = C if act == "silu" else C // 2
    nparts = len(dparts)

    rs = min(ROW_SUB, tm)

    def body(hp_ref, x_ref, hn_ref, w_ref, b_ref, *rest):
        d_refs, dn_refs = rest[:nparts], rest[nparts:2 * nparts]
        dx_ref, dw_ref, db_ref, ext_scr, d_scr = rest[2 * nparts:]
        i = pl.program_id(0)

        @pl.when(i == 0)
        def _():
            dw_ref[...] = jnp.zeros_like(dw_ref)
            db_ref[...] = jnp.zeros_like(db_ref)

        ext_scr[0:H] = jnp.where(i > 0, hp_ref[...], 0.0)
        ext_scr[H:H + tm] = x_ref[...]
        ext_scr[H + tm:] = jnp.where(i < nb - 1, hn_ref[...], 0.0)
        off = 0
        for r, rn in zip(d_refs, dn_refs):
            d_scr[0:tm, off:off + r.shape[1]] = r[...].astype(f32)
            d_scr[tm:, off:off + r.shape[1]] = jnp.where(i < nb - 1, rn[...].astype(f32), 0.0)
            off += r.shape[1]
        for j in range(Co // LANE):
            lanes = [j] if act == "silu" else [j, j + Co // LANE]
            wb = [(w_ref[:, c * LANE:(c + 1) * LANE], b_ref[:, c * LANE:(c + 1) * LANE]) for c in lanes]
            db_acc = [jnp.zeros((SUBLANE, LANE), f32) for _ in lanes]
            dw_acc = [[jnp.zeros((SUBLANE, LANE), f32) for _ in range(K)] for _ in lanes]
            for r0 in range(0, tm, rs):
                xvs = [_down_views(ext_scr[r0:r0 + rs + 2 * H, c * LANE:(c + 1) * LANE], H, rs + H, K) for c in lanes]
                us = [sum(xv[K - 1 - k] * wc[k:k + 1, :] for k in range(K)) + bc for xv, (wc, bc) in zip(xvs, wb)]
                d = d_scr[r0:r0 + rs + H, j * LANE:(j + 1) * LANE]
                dus = [d * _dsilu(us[0])] if act == "silu" else [d * us[1] * _dsilu(us[0]), d * _silu(us[0])]
                for q, (c, xv, du, (wc, _)) in enumerate(zip(lanes, xvs, dus, wb)):
                    dx_ref[r0:r0 + rs, c * LANE:(c + 1) * LANE] = _anticausal_conv(du, rs, wc, K).astype(bf16)
                    du_cur = du[:rs]
                    db_acc[q] = db_acc[q] + _fold_rows(du_cur)
                    for k in range(K):
                        dw_acc[q][k] = dw_acc[q][k] + _fold_rows(du_cur * xv[K - 1 - k][:rs])
            for q, c in enumerate(lanes):
                db_ref[:, c * LANE:(c + 1) * LANE] += jnp.sum(db_acc[q], axis=0, keepdims=True)
                for k in range(K):
                    dw_ref[k:k + 1, c * LANE:(c + 1) * LANE] += jnp.sum(dw_acc[q][k], axis=0, keepdims=True)

    in_specs = [_prev_halo_spec(tm, H, C), _rows(tm, C), _next_halo_spec(tm, H, C, L), _const((H, C)), _const((1, C))]
    in_specs += [_rows(tm, p.shape[1]) for p in dparts] + [_next_halo_spec(tm, H, p.shape[1], L) for p in dparts]
    return pl.pallas_call(body, name=name, out_shape=(S((L, C), bf16), S((H, C), f32), S((1, C), f32)), grid=(nb,),
                          in_specs=in_specs, out_specs=(_rows(tm, C), _const((H, C)), _const((1, C))),
                          scratch_shapes=[pltpu.VMEM((tm + 2 * H, C), f32), pltpu.VMEM((tm + H, Co), f32)],
                          compiler_params=_cp("arbitrary"))(x, x, x, w, b, *dparts, *dparts)


def _cf_fwd(u, dw_w, dw_b, ln_g, ln_b, *, name):
    L, C2 = u.shape
    C = C2 // 2
    K, H = CF_KERNEL, _halo_rows(CF_KERNEL)
    tm = _pick(L, 256)

    rs = min(ROW_SUB, tm)
    nl = C // LANE

    def body(h_ref, u_ref, w_ref, b_ref, g_ref, lb_ref, c_ref, s_ref, u_scr):
        i = pl.program_id(0)
        u_scr[0:H] = jnp.where(i > 0, h_ref[...], 0.0)
        u_scr[H:] = u_ref[...]
        for j in range(nl):
            wj, bj = w_ref[:, j * LANE:(j + 1) * LANE], b_ref[:, j * LANE:(j + 1) * LANE]
            for r0 in range(0, tm, rs):
                glu = u_scr[r0:r0 + rs + H, j * LANE:(j + 1) * LANE] \
                    * _sigmoid(u_scr[r0:r0 + rs + H, (nl + j) * LANE:(nl + j + 1) * LANE])
                c_ref[r0:r0 + rs, j * LANE:(j + 1) * LANE] = _causal_conv(glu, H, wj, K) + bj
        for r0 in range(0, tm, LN_ROWS):
            c = c_ref[r0:r0 + LN_ROWS, :]
            mu = jnp.mean(c, axis=-1, keepdims=True)
            xc = c - mu
            var = jnp.mean(xc * xc, axis=-1, keepdims=True)
            ln = xc * lax.rsqrt(var + EPS) * g_ref[...] + lb_ref[...]
            s_ref[r0:r0 + LN_ROWS, :] = _silu(ln).astype(bf16)

    return pl.pallas_call(body, name=name, out_shape=(S((L, C), f32), S((L, C), bf16)), grid=(L // tm,),
                          in_specs=[_prev_halo_spec(tm, H, C2), _rows(tm, C2), _const((H, C)), _const((1, C)),
                                    _const((1, C)), _const((1, C))],
                          out_specs=(_rows(tm, C), _rows(tm, C)), scratch_shapes=[pltpu.VMEM((H + tm, C2), f32)],
                          compiler_params=_cp("parallel"))(u, u, dw_w, dw_b, ln_g, ln_b)


def _cf_ln_bwd(c, ln_g, ln_b, ds, *, name):
    L, C = c.shape
    tm = _pick(L, 512)

    def body(c_ref, g_ref, lb_ref, ds_ref, dc_ref, dg_ref, db_ref):
        i = pl.program_id(0)

        @pl.when(i == 0)
        def _():
            dg_ref[...] = jnp.zeros_like(dg_ref)
            db_ref[...] = jnp.zeros_like(db_ref)

        dg_acc = jnp.zeros((SUBLANE, C), f32)
        db_acc = jnp.zeros((SUBLANE, C), f32)
        for r0 in range(0, tm, LN_ROWS):
            c = c_ref[r0:r0 + LN_ROWS, :]
            mu = jnp.mean(c, axis=-1, keepdims=True)
            xc = c - mu
            r = lax.rsqrt(jnp.mean(xc * xc, axis=-1, keepdims=True) + EPS)
            xh = xc * r
            ln = xh * g_ref[...] + lb_ref[...]
            dln = ds_ref[r0:r0 + LN_ROWS, :].astype(f32) * _dsilu(ln)
            dg_acc = dg_acc + _fold_rows(dln * xh)
            db_acc = db_acc + _fold_rows(dln)
            dxh = dln * g_ref[...]
            dc_ref[r0:r0 + LN_ROWS, :] = r * (dxh - jnp.mean(dxh, axis=-1, keepdims=True)
                                              - xh * jnp.mean(dxh * xh, axis=-1, keepdims=True))
        dg_ref[...] += jnp.sum(dg_acc, axis=0, keepdims=True)
        db_ref[...] += jnp.sum(db_acc, axis=0, keepdims=True)

    return pl.pallas_call(body, name=name, out_shape=(S((L, C), f32), S((1, C), f32), S((1, C), f32)), grid=(L // tm,),
                          in_specs=[_rows(tm, C), _const((1, C)), _const((1, C)), _rows(tm, C)],
                          out_specs=(_rows(tm, C), _const((1, C)), _const((1, C))),
                          compiler_params=_cp("arbitrary"))(c, ln_g, ln_b, ds)


def _cf_glu_bwd(u, dc, dw_w, *, name):
    L, C2 = u.shape
    C = C2 // 2
    K, H = CF_KERNEL, _halo_rows(CF_KERNEL)
    tm = _pick(L, 256)
    nb = L // tm

    rs = min(ROW_SUB, tm)
    nl = C // LANE

    fold = _fold_rows

    def body(uh_ref, u_ref, dc_ref, dch_ref, w_ref, du_ref, dw_ref, db_ref, dus_ref, u_scr, dc_scr):
        i = pl.program_id(0)

        @pl.when(i == 0)
        def _():
            dw_ref[...] = jnp.zeros_like(dw_ref)
            db_ref[...] = jnp.zeros_like(db_ref)
            dus_ref[...] = jnp.zeros_like(dus_ref)

        u_scr[0:H] = jnp.where(i > 0, uh_ref[...], 0.0)
        u_scr[H:] = u_ref[...]
        dc_scr[0:tm] = dc_ref[...]
        dc_scr[tm:] = jnp.where(i < nb - 1, dch_ref[...], 0.0)
        for j in range(nl):
            la, lg = slice(j * LANE, (j + 1) * LANE), slice((nl + j) * LANE, (nl + j + 1) * LANE)
            wj = w_ref[:, la]
            zero8 = jnp.zeros((SUBLANE, LANE), f32)
            dw_acc, db_acc, dua_acc, dug_acc = [zero8] * K, zero8, zero8, zero8
            for r0 in range(0, tm, rs):
                a_e = u_scr[r0:r0 + rs + H, la]
                sg = _sigmoid(u_scr[r0:r0 + rs + H, lg])
                dce = dc_scr[r0:r0 + rs + H, la]
                dglu = _anticausal_conv(dce, rs, wj, K)
                a_c, sg_c, dc_c = a_e[H:], sg[H:], dce[:rs]
                du_a = dglu * sg_c
                du_g = dglu * a_c * sg_c * (1.0 - sg_c)
                du_ref[r0:r0 + rs, la] = du_a.astype(bf16)
                du_ref[r0:r0 + rs, lg] = du_g.astype(bf16)
                dua_acc, dug_acc, db_acc = dua_acc + fold(du_a), dug_acc + fold(du_g), db_acc + fold(dc_c)
                views = _down_views(a_e * sg, H, rs, K)
                dw_acc = [dw_acc[k] + fold(dc_c * views[K - 1 - k]) for k in range(K)]
            dus_ref[:, la] += jnp.sum(dua_acc, axis=0, keepdims=True)
            dus_ref[:, lg] += jnp.sum(dug_acc, axis=0, keepdims=True)
            db_ref[:, la] += jnp.sum(db_acc, axis=0, keepdims=True)
            for k in range(K):
                dw_ref[k:k + 1, la] += jnp.sum(dw_acc[k], axis=0, keepdims=True)

    return pl.pallas_call(body, name=name,
                          out_shape=(S((L, C2), bf16), S((H, C), f32), S((1, C), f32), S((1, C2), f32)), grid=(nb,),
                          in_specs=[_prev_halo_spec(tm, H, C2), _rows(tm, C2), _rows(tm, C), _next_halo_spec(tm, H, C, L),
                                    _const((H, C))],
                          out_specs=(_rows(tm, C2), _const((H, C)), _const((1, C)), _const((1, C2))),
                          scratch_shapes=[pltpu.VMEM((H + tm, C2), f32), pltpu.VMEM((tm + H, C), f32)],
                          compiler_params=_cp("arbitrary"))(u, u, dc, dc, dw_w)


def _gated_norm_fwd(y, z, g, *, name):
    L, C = y.shape
    tm = _pick(L, 256)

    def body(y_ref, z_ref, g_ref, o_ref):
        o_ref[...] = _rms_val(y_ref[...] * _silu(z_ref[...]), g_ref[...]).astype(bf16)

    return pl.pallas_call(body, name=name, out_shape=S((L, C), bf16), grid=(L // tm,),
                          in_specs=[_rows(tm, C), _rows(tm, C), _const((1, C))], out_specs=_rows(tm, C),
                          compiler_params=_cp("parallel"))(y, z, g)


def _gated_norm_bwd(y, z, g, dyn, *, name):
    L, C = y.shape
    tm = _pick(L, 256)

    def body(y_ref, z_ref, g_ref, d_ref, dy_ref, dz_ref, dg_ref):
        i = pl.program_id(0)

        @pl.when(i == 0)
        def _():
            dg_ref[...] = jnp.zeros_like(dg_ref)

        y, z = y_ref[...], z_ref[...]
        sz = _silu(z)
        du, dg = _rms_bwd_val(y * sz, g_ref[...], d_ref[...].astype(f32))
        dg_ref[...] += dg
        dy_ref[...] = du * sz
        dz_ref[...] = (du * y * _dsilu(z)).astype(bf16)

    return pl.pallas_call(body, name=name, out_shape=(S((L, C), f32), S((L, C), bf16), S((1, C), f32)), grid=(L // tm,),
                          in_specs=[_rows(tm, C), _rows(tm, C), _const((1, C)), _rows(tm, C)],
                          out_specs=(_rows(tm, C), _rows(tm, C), _const((1, C))),
                          compiler_params=_cp("arbitrary"))(y, z, g, dyn)


_XA_SCALE = XA_HEAD_DIM ** -0.5


def _attn_fwd(q, kv, *, name):
    L, C = q.shape
    tm = _pick(L, 1024)
    Dh = XA_HEAD_DIM

    def body(q_ref, kv_ref, o_ref):
        for h in range(XA_HEADS):
            qh = q_ref[:, h * Dh:(h + 1) * Dh]
            kh = kv_ref[:, h * Dh:(h + 1) * Dh]
            vh = kv_ref[:, C + h * Dh:C + (h + 1) * Dh]
            s = lax.dot_general(qh, kh, _DN["nt"], preferred_element_type=f32) * _XA_SCALE
            e = jnp.exp(s - jnp.max(s, axis=-1, keepdims=True))
            p = e / jnp.sum(e, axis=-1, keepdims=True)
            o_ref[:, h * Dh:(h + 1) * Dh] = jnp.dot(p.astype(bf16), vh, preferred_element_type=f32).astype(bf16)

    return pl.pallas_call(body, name=name, out_shape=S((L, C), bf16), grid=(L // tm,),
                          in_specs=[_rows(tm, C), _const((N_MEM, 2 * C))], out_specs=_rows(tm, C),
                          compiler_params=_cp("parallel"))(q, kv)


def _attn_bwd(q, kv, do, *, name):
    L, C = q.shape
    tm = _pick(L, 1024)
    Dh = XA_HEAD_DIM

    def body(q_ref, kv_ref, do_ref, dq_ref, dkv_ref):
        i = pl.program_id(0)

        @pl.when(i == 0)
        def _():
            dkv_ref[...] = jnp.zeros_like(dkv_ref)

        for h in range(XA_HEADS):
            qh = q_ref[:, h * Dh:(h + 1) * Dh]
            kh = kv_ref[:, h * Dh:(h + 1) * Dh]
            vh = kv_ref[:, C + h * Dh:C + (h + 1) * Dh]
            doh = do_ref[:, h * Dh:(h + 1) * Dh]
            s = lax.dot_general(qh, kh, _DN["nt"], preferred_element_type=f32) * _XA_SCALE
            e = jnp.exp(s - jnp.max(s, axis=-1, keepdims=True))
            p = e / jnp.sum(e, axis=-1, keepdims=True)
            pb = p.astype(bf16)
            dkv_ref[:, C + h * Dh:C + (h + 1) * Dh] += lax.dot_general(pb, doh, _DN["tn"], preferred_element_type=f32)
            dp = lax.dot_general(doh, vh, _DN["nt"], preferred_element_type=f32)
            ds = (p * (dp - jnp.sum(dp * p, axis=-1, keepdims=True)) * _XA_SCALE).astype(bf16)
            dq_ref[:, h * Dh:(h + 1) * Dh] = jnp.dot(ds, kh, preferred_element_type=f32).astype(bf16)
            dkv_ref[:, h * Dh:(h + 1) * Dh] += lax.dot_general(ds, qh, _DN["tn"], preferred_element_type=f32)

    return pl.pallas_call(body, name=name, out_shape=(S((L, C), bf16), S((N_MEM, 2 * C), f32)), grid=(L // tm,),
                          in_specs=[_rows(tm, C), _const((N_MEM, 2 * C)), _rows(tm, C)],
                          out_specs=(_rows(tm, C), _const((N_MEM, 2 * C))),
                          compiler_params=_cp("arbitrary"))(q, kv, do)


Q = CHUNK
PAIRS = HEADS_PER_GROUP // 2


def _split(x, pieces):
    out = []
    for _ in range(pieces - 1):
        p = x.astype(bf16)
        out.append(p)
        x = x - p.astype(f32)
    return out + [x.astype(bf16)]


def _sel_right(x, sel, mode="nn", pieces=2):
    return sum(lax.dot_general(p, sel, _DN[mode], preferred_element_type=f32) for p in _split(x, pieces))


def _sel_left(sel, x, pieces=3):
    return sum(lax.dot_general(sel, p, _DN["nn"], preferred_element_type=f32) for p in _split(x, pieces))


def _ssd_common(dt_ref, hp_ref):
    dt_pre = dt_ref[...] + hp_ref[0:1, :]
    dt = _softplus(dt_pre)
    A = -jnp.exp(hp_ref[1:2, :])
    a = dt * A
    row = lax.broadcasted_iota(jnp.int32, (Q, Q), 0)
    col = lax.broadcasted_iota(jnp.int32, (Q, Q), 1)
    tri = row >= col
    cs = _sel_left(tri.astype(bf16), a)
    T = cs[Q - 1:Q, :]
    return dict(dt_pre=dt_pre, dt=dt, A=A, cs=cs, csT=cs.T, T=T, ecs=jnp.exp(cs), eend=jnp.exp(T - cs), eT=jnp.exp(T),
                tri=tri, row=row, col=col)


def _pair_expand(v, hA, lo):
    return jnp.where(lo, v[:, hA:hA + 1], v[:, hA + 1:hA + 2])


def _decay(cm, h):
    seg = cm["cs"][:, h:h + 1] - cm["csT"][h:h + 1, :]
    return jnp.where(cm["tri"], jnp.exp(jnp.where(cm["tri"], seg, 0.0)), 0.0)


def _decay_t(cm, h):
    keep = cm["row"] <= cm["col"]
    seg = cm["csT"][h:h + 1, :] - cm["cs"][:, h:h + 1]
    return jnp.where(keep, jnp.exp(jnp.where(keep, seg, 0.0)), 0.0)


ALL_PAIRS = N_SSM_HEADS // 2
GN = N_GROUPS * D_STATE


def _ssd_fwd(act, dtp, hp, *, name):
    L = act.shape[0]
    nc = L // Q

    def body(xs_ref, b_ref, c_ref, dt_ref, hp_ref, y_ref, hs_ref, h_scr):
        c = pl.program_id(0)

        @pl.when(c == 0)
        def _():
            h_scr[...] = jnp.zeros_like(h_scr)

        cm = _ssd_common(dt_ref, hp_ref)
        lo = lax.broadcasted_iota(jnp.int32, (Q, LANE), 1) < HEAD_DIM
        top = lax.broadcasted_iota(jnp.int32, (LANE, LANE), 0) < HEAD_DIM
        Drow = hp_ref[2:3, :]
        for g in range(N_GROUPS):
            Bb = b_ref[:, g * D_STATE:(g + 1) * D_STATE].astype(bf16)
            Cb = c_ref[:, g * D_STATE:(g + 1) * D_STATE].astype(bf16)
            CB = lax.dot_general(Cb, Bb, _DN["nt"], preferred_element_type=f32)
            for jj in range(PAIRS):
                p = g * PAIRS + jj
                hA, hB = 2 * p, 2 * p + 1
                dtx, ecsx, eendx = (_pair_expand(cm[k], hA, lo) for k in ("dt", "ecs", "eend"))
                xs_p = xs_ref[:, p * LANE:(p + 1) * LANE]
                Xd = xs_p * dtx
                Y = None
                for h, Xm in ((hA, jnp.where(lo, Xd, 0.0)), (hB, jnp.where(lo, 0.0, Xd))):
                    W = (CB * _decay(cm, h)).astype(bf16)
                    t = jnp.dot(W, Xm.astype(bf16), preferred_element_type=f32)
                    Y = t if Y is None else Y + t
                Hp = h_scr[p]
                hs_ref[0, p] = Hp
                Yoff = lax.dot_general(Cb, Hp.astype(bf16), _DN["nt"], preferred_element_type=f32) * ecsx
                Dx = jnp.where(lo[0:1, :], Drow[:, hA:hA + 1], Drow[:, hB:hB + 1])
                y_ref[:, p * LANE:(p + 1) * LANE] = Y + Yoff + xs_p * Dx
                Snew = lax.dot_general((Xd * eendx).astype(bf16), Bb, _DN["tn"], preferred_element_type=f32)
                eTx = jnp.where(top, cm["eT"][:, hA:hA + 1], cm["eT"][:, hB:hB + 1])
                h_scr[p] = Hp * eTx + Snew

    return pl.pallas_call(
        body, name=name, out_shape=(S((L, D_INNER), f32), S((nc, ALL_PAIRS, LANE, D_STATE), f32)),
        grid=(nc,),
        in_specs=[pl.BlockSpec((Q, D_INNER), lambda c: (c, 0)),
                  pl.BlockSpec((Q, GN), lambda c: (c, D_INNER // GN)),
                  pl.BlockSpec((Q, GN), lambda c: (c, D_INNER // GN + 1)),
                  pl.BlockSpec((Q, LANE), lambda c: (c, 0)),
                  pl.BlockSpec((SUBLANE, LANE), lambda c: (0, 0))],
        out_specs=(pl.BlockSpec((Q, D_INNER), lambda c: (c, 0)),
                   pl.BlockSpec((1, ALL_PAIRS, LANE, D_STATE), lambda c: (c, 0, 0, 0))),
        scratch_shapes=[pltpu.VMEM((ALL_PAIRS, LANE, D_STATE), f32)],
        compiler_params=_cp("arbitrary"))(act, act, act, dtp, hp)


def _ssd_bwd(act, dtp, hp, dy, hs, *, name):
    L = act.shape[0]
    nc = L // Q

    def body(xs_ref, b_ref, c_ref, dt_ref, hp_ref, dy_ref, hs_ref, dxs_ref, db_ref, dc_ref, ddt_ref, dhp_ref, dh_scr):
        c = pl.program_id(0)

        @pl.when(c == 0)
        def _():
            dh_scr[...] = jnp.zeros_like(dh_scr)
            dhp_ref[...] = jnp.zeros_like(dhp_ref)

        cm = _ssd_common(dt_ref, hp_ref)
        lane = lax.broadcasted_iota(jnp.int32, (Q, LANE), 1)
        sub = lax.broadcasted_iota(jnp.int32, (LANE, LANE), 0)
        lo = lane < HEAD_DIM
        top = sub < HEAD_DIM
        Drow = hp_ref[2:3, :]
        zero = jnp.zeros((Q, LANE), f32)
        dcs, dcsT, ddtx = zero, zero, zero
        dD_row = jnp.zeros((1, LANE), f32)
        dT_row = jnp.zeros((1, LANE), f32)
        for g in range(N_GROUPS):
            Bb = b_ref[:, g * D_STATE:(g + 1) * D_STATE].astype(bf16)
            Cb = c_ref[:, g * D_STATE:(g + 1) * D_STATE].astype(bf16)
            CB = lax.dot_general(Cb, Bb, _DN["nt"], preferred_element_type=f32)
            CBT = lax.dot_general(Bb, Cb, _DN["nt"], preferred_element_type=f32)
            dC, dB, dCB = zero, zero, jnp.zeros((Q, Q), f32)
            for jj in range(PAIRS):
                p = g * PAIRS + jj
                hA, hB = 2 * p, 2 * p + 1
                Pj = (lane == jnp.where(top, hA, hB)).astype(bf16)
                dtx, ecsx, eendx = (_pair_expand(cm[k], hA, lo) for k in ("dt", "ecs", "eend"))
                xs_p = xs_ref[:, p * LANE:(p + 1) * LANE]
                dY_p = dy_ref[:, p * LANE:(p + 1) * LANE]
                Xd = xs_p * dtx
                Xdb = Xd.astype(bf16)
                Hp, dHn = hs_ref[0, p], dh_scr[p]
                Hb, dHb = Hp.astype(bf16), dHn.astype(bf16)
                EdYb = (dY_p * ecsx).astype(bf16)
                YoffN = lax.dot_general(Cb, Hb, _DN["nt"], preferred_element_type=f32)
                dC = dC + jnp.dot(EdYb, Hb, preferred_element_type=f32)
                dH_off = lax.dot_general(EdYb, Cb, _DN["tn"], preferred_element_type=f32)
                R = lax.dot_general(Bb, dHb, _DN["nt"], preferred_element_type=f32)
                Xe = Xd * eendx
                dB = dB + jnp.dot(Xe.astype(bf16), dHb, preferred_element_type=f32)
                dXd = R * eendx
                RXe = R * Xe
                dcs = dcs + _sel_right(dY_p * YoffN * ecsx - RXe, Pj)
                HH = dHn * Hp
                hh = [jnp.sum(jnp.sum(HH[r0:r0 + HEAD_DIM], axis=0, keepdims=True), axis=1, keepdims=True)
                      for r0 in (0, HEAD_DIM)]
                rxe_cols = jnp.broadcast_to(jnp.sum(RXe, axis=0, keepdims=True), (SUBLANE, LANE))
                dT_row = dT_row + _sel_right(rxe_cols, Pj, pieces=3)[0:1] \
                    + (jnp.where(lane[0:1] == hA, hh[0], 0.0) + jnp.where(lane[0:1] == hB, hh[1], 0.0)) * cm["eT"]
                for h, keep in ((hA, lo), (hB, jnp.logical_not(lo))):
                    M = _decay(cm, h)
                    Wf = CB * M
                    dYm = jnp.where(keep, dY_p, 0.0).astype(bf16)
                    dW = lax.dot_general(dYm, Xdb, _DN["nt"], preferred_element_type=f32)
                    WT = (CBT * _decay_t(cm, h)).astype(bf16)
                    dXd = dXd + jnp.dot(WT, dYm, preferred_element_type=f32)
                    Z = dW * Wf
                    dcs = dcs + _sel_right(Z, (lane == h).astype(bf16))
                    dcsT = dcsT + jnp.where(sub == h, jnp.sum(Z, axis=0, keepdims=True), 0.0)
                    dCB = dCB + dW * M
                Dx = jnp.where(lo[0:1, :], Drow[:, hA:hA + 1], Drow[:, hB:hB + 1])
                dxs_ref[:, p * LANE:(p + 1) * LANE] = dXd * dtx + dY_p * Dx
                ddtx = ddtx + _sel_right(dXd * xs_p, Pj)
                dD_cols = jnp.broadcast_to(jnp.sum(dY_p * xs_p, axis=0, keepdims=True), (SUBLANE, LANE))
                dD_row = dD_row + _sel_right(dD_cols, Pj, pieces=3)[0:1]
                eTx = jnp.where(top, cm["eT"][:, hA:hA + 1], cm["eT"][:, hB:hB + 1])
                dh_scr[p] = dHn * eTx + dH_off
            dCBb = dCB.astype(bf16)
            dc_ref[:, g * D_STATE:(g + 1) * D_STATE] = dC + jnp.dot(dCBb, Bb, preferred_element_type=f32)
            db_ref[:, g * D_STATE:(g + 1) * D_STATE] = dB + lax.dot_general(dCBb, Cb, _DN["tn"], preferred_element_type=f32)
        dcs = dcs - dcsT.T + jnp.where(lax.broadcasted_iota(jnp.int32, (Q, LANE), 0) == Q - 1, dT_row, 0.0)
        da = _sel_left((cm["row"] <= cm["col"]).astype(bf16), dcs)
        ddt_pre = (da * cm["A"] + ddtx) * _sigmoid(cm["dt_pre"])
        ddt_ref[...] = ddt_pre
        r8 = lax.broadcasted_iota(jnp.int32, (SUBLANE, LANE), 0)
        dhp_ref[...] += jnp.where(r8 == 0, jnp.sum(ddt_pre, axis=0, keepdims=True),
                                  jnp.where(r8 == 1, jnp.sum(da * cm["dt"], axis=0, keepdims=True) * cm["A"],
                                            jnp.where(r8 == 2, dD_row, 0.0)))

    rev = lambda c: nc - 1 - c
    return pl.pallas_call(
        body, name=name,
        out_shape=(S((L, D_INNER), f32), S((L, GN), f32), S((L, GN), f32), S((L, LANE), f32), S((SUBLANE, LANE), f32)),
        grid=(nc,),
        in_specs=[pl.BlockSpec((Q, D_INNER), lambda c: (rev(c), 0)),
                  pl.BlockSpec((Q, GN), lambda c: (rev(c), D_INNER // GN)),
                  pl.BlockSpec((Q, GN), lambda c: (rev(c), D_INNER // GN + 1)),
                  pl.BlockSpec((Q, LANE), lambda c: (rev(c), 0)),
                  pl.BlockSpec((SUBLANE, LANE), lambda c: (0, 0)),
                  pl.BlockSpec((Q, D_INNER), lambda c: (rev(c), 0)),
                  pl.BlockSpec((1, ALL_PAIRS, LANE, D_STATE), lambda c: (rev(c), 0, 0, 0))],
        out_specs=(pl.BlockSpec((Q, D_INNER), lambda c: (rev(c), 0)),
                   pl.BlockSpec((Q, GN), lambda c: (rev(c), 0)),
                   pl.BlockSpec((Q, GN), lambda c: (rev(c), 0)),
                   pl.BlockSpec((Q, LANE), lambda c: (rev(c), 0)),
                   pl.BlockSpec((SUBLANE, LANE), lambda c: (0, 0))),
        scratch_shapes=[pltpu.VMEM((ALL_PAIRS, LANE, D_STATE), f32)],
        compiler_params=_cp("arbitrary"))(act, act, act, dtp, hp, dy, hs)


def _group_pad_cols(w):
    return jnp.pad(w, [(0, 0)] * (w.ndim - 1) + [(0, LANE - N_SSM_HEADS)])


def _group_unpad_cols(w):
    return w[..., :N_SSM_HEADS]


def _row(v):
    return v.reshape(1, -1)


ROW_SHARDED = ('ssm_out_w', 'cf_pw2_w', 'xa_q_w', 'xa_o_w', 'ffn_out_w')
COL_SHARDED = ('cf_pw1_w', 'xa_kv_w', 'ffn_in_w')


MIXER_WEIGHTS = ('ssm_in_w', 'ssm_out_w', 'cf_pw1_w', 'cf_pw2_w')


def _layer_matmul_weights(i, part):
    if part == "rest":
        return ('xa_q_w', 'xa_kv_w', 'xa_o_w', 'ffn_in_w', 'ffn_out_w')
    mixer = ('ssm_in_w', 'ssm_out_w') if i % 2 == 0 else ('cf_pw1_w', 'cf_pw2_w')
    return {"mixer": mixer, "mixer_in": mixer[:1], "mixer_out": mixer[1:]}[part]


def _device_step(x, mem, target, W, layer_weights, layer_grads, start_after=()):
    ng = W['norm_g']
    lw = []
    for i in range(DEPTH):
        j = i // 2
        p = {}
        if i % 2 == 0:
            p['cw'] = _pad_taps(W['ssm_conv_w'][j], SSM_CONV)
            p['cb'] = _row(W['ssm_conv_b'][j])
            hp = jnp.stack([_group_pad_cols(W['ssm_dt_bias'][j]), _group_pad_cols(W['ssm_A_log'][j]),
                            _group_pad_cols(W['ssm_D'][j])])
            p['hp'] = jnp.pad(hp, ((0, SUBLANE - 3), (0, 0)))
            p['sng'] = _row(W['ssm_norm_g'][j])
        else:
            p['pw1b'] = _row(W['cf_pw1_b'][j])
            p['dww'], p['dwb'] = _pad_taps(W['cf_dw_w'][j], CF_KERNEL), _row(W['cf_dw_b'][j])
            p['lng'], p['lnb'] = _row(W['cf_ln_g'][j]), _row(W['cf_ln_b'][j])
            p['pw2b'] = _row(W['cf_pw2_b'][j])
        p['memg'] = _row(W['xa_mem_g'][i])
        p['fcw'], p['fcb'] = _pad_taps(W['ffn_conv_w'][i], FFN_CONV), _row(W['ffn_conv_b'][i])
        p['g'] = [_row(ng[i, s]) for s in range(6)]
        lw.append(p)

    def wmm(a, wl, wname, mode, **kw):
        return _mm(a, wl[wname], mode, b_shards=wname in COL_SHARDED, **kw)

    saved = []
    X = x
    h = _rmsnorm_fwd(X, lw[0]['g'][0], name="norm_in", after=start_after)
    for i in range(DEPTH):
        p, sv = lw[i], {}
        wl = dict(layer_weights(i, "mixer_in", X))
        sv['X0'], sv['h'], sv['wl'] = X, h, wl
        if i % 2 == 0:
            win = jnp.concatenate([wl['ssm_in_w'][s] for s in range(N_CHIPS)], axis=1)
            wl['wz'], wl['wx'] = win[:, :D_INNER], win[:, D_INNER:D_INNER + CONV_DIM]
            wl['wdt'] = _group_pad_cols(win[:, D_INNER + CONV_DIM:])
            z = _mm(h, wl['wz'], "nn", name="ssm_z")
            xbc = _mm(h, wl['wx'], "nn", name="ssm_xbc")
            dtp = _mm(h, wl['wdt'], "nn", name="ssm_dt")
            act = _conv_act_fwd(xbc, p['cw'], p['cb'], K=SSM_CONV, act="silu", name="ssm_conv_fwd", out_dtype=f32)
            y, hs = _ssd_fwd(act, dtp, p['hp'], name="ssd_fwd")
            yn = _gated_norm_fwd(y, z, p['sng'], name="ssm_gnorm_fwd")
            wl.update(layer_weights(i, "mixer_out", yn))
            mix = wmm(yn, wl, 'ssm_out_w', "nn", name="ssm_out")
            sv.update(z=z, xbc=xbc, dtp=dtp, act=act, y=y, hs=hs, yn=yn)
        else:
            u = wmm(h, wl, 'cf_pw1_w', "nn", name="cf_pw1", bias=p['pw1b'])
            c, s = _cf_fwd(u, p['dww'], p['dwb'], p['lng'], p['lnb'], name="cf_conv_fwd")
            wl.update(layer_weights(i, "mixer_out", s))
            mix = wmm(s, wl, 'cf_pw2_w', "nn", name="cf_pw2", bias=p['pw2b'])
            sv.update(u=u, c=c, s=s)
        wl.update(layer_weights(i, "rest", mix))
        X1, h2 = _resid_norm_fwd(X, mix, p['g'][1], p['g'][2], name="resid_norm_a")
        q = wmm(h2, wl, 'xa_q_w', "nn", name="xa_q", out_dtype=bf16)
        m = _rmsnorm_fwd(mem, p['memg'], name="xa_mem_norm")
        kv = wmm(m, wl, 'xa_kv_w', "nn", name="xa_kv", out_dtype=bf16)
        o = _attn_fwd(q, kv, name="xa_attn_fwd")
        a = wmm(o, wl, 'xa_o_w', "nn", name="xa_o")
        X2, h3 = _resid_norm_fwd(X1, a, p['g'][3], p['g'][4], name="resid_norm_b")
        u0 = wmm(h3, wl, 'ffn_in_w', "nn", name="ffn_in")
        fact = _conv_act_fwd(u0, p['fcw'], p['fcb'], K=FFN_CONV, act="swiglu", name="ffn_conv_fwd", out_dtype=bf16)
        f = wmm(fact, wl, 'ffn_out_w', "nn", name="ffn_out")
        g_next = lw[i + 1]['g'][0] if i + 1 < DEPTH else None
        X3, hn = _resid_norm_fwd(X2, f, p['g'][5], g_next, name="resid_norm_c" if g_next is not None else "resid_norm_last")
        sv.update(mix=mix, X1=X1, h2=h2, q=q, m=m, kv=kv, o=o, a=a, X2=X2, h3=h3, u0=u0, fact=fact, f=f)
        saved.append(sv)
        X, h = X3, hn

    sse, G = _loss_fwd_bwd(X, target, name="loss")

    small = [n for n in WEIGHT_NAMES if n not in MATMUL_WEIGHTS]
    gr = {n: [None] * W[n].shape[0] for n in small}

    def dwmm(gl, a, d, wname, *, name):
        if wname in COL_SHARDED:
            gl[wname] = _mm(a, d, "tn", name=name, out_dtype=bf16, out_shards=True)
        else:
            g = _mm(a, d, "tn", name=name, out_dtype=bf16)
            gl[wname] = g.reshape(N_CHIPS, g.shape[0] // N_CHIPS, g.shape[1])

    dng = [[None] * 6 for _ in range(DEPTH)]
    df = None
    for i in reversed(range(DEPTH)):
        p, sv, j = lw[i], saved[i], i // 2
        wl, gl = sv['wl'], {}
        if df is None:
            df, dng[i][5], _ = _norm_bwd(sv['f'], p['g'][5], G, name="nb_f", out_dtype=bf16)
        dwmm(gl, sv['fact'], df, 'ffn_out_w', name="ffn_out_dw")
        dfact = wmm(df, wl, 'ffn_out_w', "nt", name="ffn_out_dx")
        du0, dcw, dcb = _conv_act_bwd(sv['u0'], [dfact], p['fcw'], p['fcb'], K=FFN_CONV, act="swiglu", name="ffn_conv_bwd")
        gr['ffn_conv_w'][i], gr['ffn_conv_b'][i] = dcw[:FFN_CONV], dcb[0]
        dwmm(gl, sv['h3'], du0, 'ffn_in_w', name="ffn_in_dw")
        dh3 = wmm(du0, wl, 'ffn_in_w', "nt", name="ffn_in_dx")
        G, dng[i][4], da, dng[i][3], _ = _norm_bwd_chain(sv['X2'], p['g'][4], dh3, G, sv['a'], p['g'][3], name="nb_x2_a")
        dwmm(gl, sv['o'], da, 'xa_o_w', name="xa_o_dw")
        do = wmm(da, wl, 'xa_o_w', "nt", name="xa_o_dx", out_dtype=bf16)
        dq, dkv = _attn_bwd(sv['q'], sv['kv'], do, name="xa_attn_bwd")
        dwmm(gl, sv['h2'], dq, 'xa_q_w', name="xa_q_dw")
        dh2 = wmm(dq, wl, 'xa_q_w', "nt", name="xa_q_dx")
        dwmm(gl, sv['m'], dkv, 'xa_kv_w', name="xa_kv_dw")
        dm = wmm(dkv, wl, 'xa_kv_w', "nt", name="xa_kv_dx")
        _, dmg, _ = _norm_bwd(mem, p['memg'], dm, name="nb_mem")
        gr['xa_mem_g'][i] = dmg[0]
        behind = tuple(layer_grads(i, "rest", gl))
        G, dng[i][2], dmix, dng[i][1], dmix_sum = _norm_bwd_chain(sv['X1'], p['g'][2], dh2, G, sv['mix'], p['g'][1],
                                                                  name="nb_x1_mix", after=behind)
        if i % 2 == 0:
            dwmm(gl, sv['yn'], dmix, 'ssm_out_w', name="ssm_out_dw")
            dyn = wmm(dmix, wl, 'ssm_out_w', "nt", name="ssm_out_dx")
            dy, dz, dsng = _gated_norm_bwd(sv['y'], sv['z'], p['sng'], dyn, name="ssm_gnorm_bwd")
            gr['ssm_norm_g'][j] = dsng[0]
            dxs, dB, dC, ddtp, dhp = _ssd_bwd(sv['act'], sv['dtp'], p['hp'], dy, sv['hs'], name="ssd_bwd")
            gr['ssm_dt_bias'][j], gr['ssm_A_log'][j], gr['ssm_D'][j] = (_group_unpad_cols(dhp[r]) for r in range(3))
            dxbc, dcw, dcb = _conv_act_bwd(sv['xbc'], [dxs, dB, dC], p['cw'], p['cb'], K=SSM_CONV, act="silu",
                                           name="ssm_conv_bwd")
            gr['ssm_conv_w'][j], gr['ssm_conv_b'][j] = dcw[:SSM_CONV], dcb[0]
            hh = sv['h']
            dwz = _mm(dz, hh, "tn", name="ssm_z_dw", out_dtype=bf16)
            dwx = _mm(dxbc, hh, "tn", name="ssm_xbc_dw", out_dtype=bf16)
            dwdt = _mm(ddtp, hh, "tn", name="ssm_dt_dw", out_dtype=bf16)
            din = jnp.concatenate([dwz, dwx, dwdt[:N_SSM_HEADS]], axis=0)
            gl['ssm_in_w'] = din.reshape(N_CHIPS, din.shape[0] // N_CHIPS, din.shape[1])
            behind = tuple(layer_grads(i, "mixer", gl))
            dh = _mm(dz, wl['wz'], "nt", name="ssm_z_dx", after=behind)
            dh = _mm(dxbc, wl['wx'], "nt", name="ssm_xbc_dx", add=dh)
            dh = _mm(ddtp, wl['wdt'], "nt", name="ssm_dt_dx", add=dh)
        else:
            dwmm(gl, sv['s'], dmix, 'cf_pw2_w', name="cf_pw2_dw")
            gr['cf_pw2_b'][j] = dmix_sum[0]
            ds = wmm(dmix, wl, 'cf_pw2_w', "nt", name="cf_pw2_dx")
            dc, dlg, dlb = _cf_ln_bwd(sv['c'], p['lng'], p['lnb'], ds, name="cf_ln_bwd")
            gr['cf_ln_g'][j], gr['cf_ln_b'][j] = dlg[0], dlb[0]
            du, ddw, ddb, dus = _cf_glu_bwd(sv['u'], dc, p['dww'], name="cf_glu_bwd")
            gr['cf_dw_w'][j], gr['cf_dw_b'][j], gr['cf_pw1_b'][j] = ddw[:CF_KERNEL], ddb[0], dus[0]
            dwmm(gl, sv['h'], du, 'cf_pw1_w', name="cf_pw1_dw")
            behind = tuple(layer_grads(i, "mixer", gl))
            dh = wmm(du, wl, 'cf_pw1_w', "nt", name="cf_pw1_dx", after=behind)
        if i > 0:
            G, dng[i][0], df, dng[i - 1][5], _ = _norm_bwd_chain(sv['X0'], p['g'][0], dh, G, saved[i - 1]['f'],
                                                                 lw[i - 1]['g'][5], name="nb_x0_f")
        else:
            G, dng[i][0], _ = _norm_bwd(sv['X0'], p['g'][0], dh, name="nb_x0", add=G)
    gr['norm_g'] = [jnp.concatenate(dng[i], axis=0) for i in range(DEPTH)]
    gsmall = {n: jnp.stack(gr[n]) for n in small}
    return sse, G, gsmall


MESH = pl.DeviceIdType.MESH
HBM_SPEC = pl.BlockSpec(memory_space=pltpu.HBM)


def _chip_peers(x, y):
    return [(1 - x, y), (x, 1 - y), (1 - x, 1 - y)]


def _all_gather_chips(buf, *, name):
    R, C = buf.shape

    def body(in_ref, out_ref, send_sems, recv_sems, local_sem):
        x, y, c = lax.axis_index("x"), lax.axis_index("y"), lax.axis_index("c")
        me = 2 * x + y
        mine = pltpu.make_async_copy(in_ref, out_ref.at[me], local_sem)
        mine.start()
        peers = _chip_peers(x, y)
        sends = []
        for k, (px, py) in enumerate(peers):
            cp = pltpu.make_async_remote_copy(src_ref=in_ref, dst_ref=out_ref.at[me], send_sem=send_sems.at[k],
                                              recv_sem=recv_sems.at[k], device_id=(px, py, c), device_id_type=MESH)
            cp.start()
            sends.append(cp)
        for k, (px, py) in enumerate(peers):
            pltpu.make_async_remote_copy(src_ref=in_ref, dst_ref=out_ref.at[2 * px + py], send_sem=send_sems.at[k],
                                         recv_sem=recv_sems.at[k], device_id=(px, py, c), device_id_type=MESH).wait_recv()
        for cp in sends:
            cp.wait_send()
        mine.wait()

    return pl.pallas_call(body, name=name, out_shape=S((N_CHIPS, R, C), buf.dtype), in_specs=[HBM_SPEC], out_specs=HBM_SPEC,
                          scratch_shapes=[pltpu.SemaphoreType.DMA((3,)), pltpu.SemaphoreType.DMA((3,)),
                                          pltpu.SemaphoreType.DMA(())])(buf)


def _remote(src, dst, send_sem, recv_sem, device):
    return pltpu.make_async_remote_copy(src_ref=src, dst_ref=dst, send_sem=send_sem, recv_sem=recv_sem,
                                        device_id=device, device_id_type=MESH)


def _gather_matmul_weights(shards, *, name):
    n = len(shards)

    def body(*refs):
        ins, outs = refs[:n], refs[n:2 * n]
        send, recv, fsend, frecv, lsem = refs[2 * n:]
        x, y, c = lax.axis_index("x"), lax.axis_index("y"), lax.axis_index("c")
        me, sib = 2 * x + y, (x, y, 1 - c)
        peers = _chip_peers(x, y)
        started, local = [], []
        for w in range(n):
            cp = pltpu.make_async_copy(ins[w], outs[w].at[:, me], lsem.at[w])
            cp.start()
            local.append(cp)
            for k, (px, py) in enumerate(peers):
                cp = _remote(ins[w].at[:, c], outs[w].at[:, me, c], send.at[w, k], recv.at[w, k], (px, py, c))
                cp.start()
                started.append(cp)
        for w in range(n):
            for k, (px, py) in enumerate(peers):
                landed = outs[w].at[:, 2 * px + py, c]
                _remote(ins[w].at[:, c], landed, send.at[w, k], recv.at[w, k], (px, py, c)).wait_recv()
                cp = _remote(landed, landed, fsend.at[w, k], frecv.at[w, k], sib)
                cp.start()
                started.append(cp)
        for w in range(n):
            for k, (px, py) in enumerate(peers):
                _remote(ins[w].at[:, c], outs[w].at[:, 2 * px + py, 1 - c], fsend.at[w, k], frecv.at[w, k], sib).wait_recv()
        for cp in started:
            cp.wait_send()
        for cp in local:
            cp.wait()

    out_shape = tuple(S((s.shape[0], N_CHIPS) + s.shape[1:], s.dtype) for s in shards)
    sems = [pltpu.SemaphoreType.DMA((n, 3)) for _ in range(4)] + [pltpu.SemaphoreType.DMA((n,))]
    return pl.pallas_call(body, name=name, out_shape=out_shape, in_specs=[HBM_SPEC] * n, out_specs=(HBM_SPEC,) * n,
                          scratch_shapes=sems)(*shards)


SEM_SPEC = pl.BlockSpec(memory_space=pltpu.SEMAPHORE)
VMEM_SPEC = pl.BlockSpec(memory_space=pltpu.VMEM)


def _in_hbm(a):
    return pltpu.with_memory_space_constraint(a, pltpu.HBM)


def _chip_targets(x, y):
    return [(x, y), (1 - x, y), (x, 1 - y), (1 - x, 1 - y)]


def _spread_start(srcs, scatter, *, name, after=()):
    n = len(srcs)
    lands = [lax.empty((N_CHIPS,) + (s.shape[1:] if scatter else s.shape), s.dtype) for s in srcs]

    def body(*refs):
        src, land = refs[:n], refs[n:2 * n]
        send, recv, token = refs[2 * n + len(after)], refs[2 * n + len(after) + 1], refs[-1]
        x, y, c = lax.axis_index("x"), lax.axis_index("y"), lax.axis_index("c")
        me = 2 * x + y
        for w in range(n):
            for k, (px, py) in enumerate(_chip_targets(x, y)):
                block = src[w].at[2 * px + py] if scatter else src[w]
                _remote(block, land[w].at[me], send.at[N_CHIPS * w + k], recv.at[N_CHIPS * w + k], (px, py, c)).start()
        token[...] = jnp.zeros_like(token)

    thru = tuple(pltpu.HBM(a.shape, a.dtype) for a in list(srcs) + lands)
    sems = (pltpu.SemaphoreType.DMA((N_CHIPS * n,)), pltpu.SemaphoreType.DMA((N_CHIPS * n,)))
    out = pl.pallas_call(
        body, name=name, out_shape=sems + thru + (S((SUBLANE, LANE), f32),),
        in_specs=[HBM_SPEC] * (2 * n) + [ANY_SPEC] * len(after),
        out_specs=(SEM_SPEC, SEM_SPEC) + (HBM_SPEC,) * (2 * n) + (VMEM_SPEC,),
        input_output_aliases={i: 2 + i for i in range(2 * n)},
        compiler_params=pltpu.CompilerParams(has_side_effects=pltpu.SideEffectType.DATAFLOW_SIDE_EFFECTING),
    )(*[_in_hbm(a) for a in list(srcs) + lands], *after)
    return out[0], out[1], out[2:2 + n], out[2 + n:2 + 2 * n], out[-1]


def _spread_wait(send, recv, srcs, lands, after, scatter, *, name):
    n = len(srcs)

    def body(*refs):
        src, land, send, recv = refs[:n], refs[n:2 * n], refs[2 * n], refs[2 * n + 1]
        x, y, c = lax.axis_index("x"), lax.axis_index("y"), lax.axis_index("c")
        me = 2 * x + y
        for w in range(n):
            for k, (px, py) in enumerate(_chip_targets(x, y)):
                block = src[w].at[me] if scatter else src[w]
                cp = _remote(block, land[w].at[2 * px + py], send.at[N_CHIPS * w + k], recv.at[N_CHIPS * w + k], (px, py, c))
                cp.wait_send()
                cp.wait_recv()

    thru = tuple(pltpu.HBM(a.shape, a.dtype) for a in list(srcs) + list(lands))
    out = pl.pallas_call(
        body, name=name, out_shape=thru,
        in_specs=[HBM_SPEC] * (2 * n) + [SEM_SPEC, SEM_SPEC] + [ANY_SPEC] * len(after), out_specs=(HBM_SPEC,) * (2 * n),
        input_output_aliases={i: i for i in range(2 * n)},
        compiler_params=pltpu.CompilerParams(has_side_effects=pltpu.SideEffectType.DATAFLOW_SIDE_EFFECTING),
    )(*srcs, *lands, send, recv, *after)
    return out[:n], out[n:]


def _swap_start(bufs, *, name):
    n = len(bufs)
    lands = [lax.empty(b.shape, b.dtype) for b in bufs]

    def body(*refs):
        src, land, send, recv, token = refs[:n], refs[n:2 * n], refs[2 * n], refs[2 * n + 1], refs[-1]
        sib = (lax.axis_index("x"), lax.axis_index("y"), 1 - lax.axis_index("c"))
        for w in range(n):
            _remote(src[w], land[w], send.at[w], recv.at[w], sib).start()
        token[...] = jnp.zeros_like(token)

    thru = tuple(pltpu.HBM(a.shape, a.dtype) for a in list(bufs) + lands)
    out = pl.pallas_call(
        body, name=name,
        out_shape=(pltpu.SemaphoreType.DMA((n,)), pltpu.SemaphoreType.DMA((n,))) + thru + (S((SUBLANE, LANE), f32),),
        in_specs=[HBM_SPEC] * (2 * n), out_specs=(SEM_SPEC, SEM_SPEC) + (HBM_SPEC,) * (2 * n) + (VMEM_SPEC,),
        input_output_aliases={i: 2 + i for i in range(2 * n)},
        compiler_params=pltpu.CompilerParams(has_side_effects=pltpu.SideEffectType.DATAFLOW_SIDE_EFFECTING),
    )(*[_in_hbm(a) for a in list(bufs) + lands])
    return out[0], out[1], out[2:2 + n], out[2 + n:2 + 2 * n], out[-1]


def _swap_wait(send, recv, bufs, lands, after, *, name):
    n = len(bufs)

    def body(*refs):
        src, land, send, recv = refs[:n], refs[n:2 * n], refs[2 * n], refs[2 * n + 1]
        sib = (lax.axis_index("x"), lax.axis_index("y"), 1 - lax.axis_index("c"))
        for w in range(n):
            cp = _remote(src[w], land[w], send.at[w], recv.at[w], sib)
            cp.wait_send()
            cp.wait_recv()

    thru = tuple(pltpu.HBM(a.shape, a.dtype) for a in list(bufs) + list(lands))
    out = pl.pallas_call(
        body, name=name, out_shape=thru,
        in_specs=[HBM_SPEC] * (2 * n) + [SEM_SPEC, SEM_SPEC] + [ANY_SPEC] * len(after), out_specs=(HBM_SPEC,) * (2 * n),
        input_output_aliases={i: i for i in range(2 * n)},
        compiler_params=pltpu.CompilerParams(has_side_effects=pltpu.SideEffectType.DATAFLOW_SIDE_EFFECTING),
    )(*bufs, *lands, send, recv, *after)
    return out[:n], out[n:]


N_DEVICES = 8


def _device_targets(x, y, c):
    flips = [(d >> 2 & 1, d >> 1 & 1, d & 1) for d in range(N_DEVICES)]
    return [(1 - x if fx else x, 1 - y if fy else y, 1 - c if fc else c) for fx, fy, fc in flips]


def _allgather_devices_start(buf, *, name):
    land = lax.empty((N_DEVICES,) + buf.shape, buf.dtype)

    def body(src, land, send, recv, src_thru, land_thru, token):
        x, y, c = lax.axis_index("x"), lax.axis_index("y"), lax.axis_index("c")
        me = 4 * x + 2 * y + c
        for k, peer in enumerate(_device_targets(x, y, c)):
            _remote(src, land.at[me], send.at[k], recv.at[k], peer).start()
        token[...] = jnp.zeros_like(token)

    return pl.pallas_call(
        body, name=name,
        out_shape=(pltpu.SemaphoreType.DMA((N_DEVICES,)), pltpu.SemaphoreType.DMA((N_DEVICES,)), pltpu.HBM(buf.shape, buf.dtype),
                   pltpu.HBM(land.shape, land.dtype), S((SUBLANE, LANE), f32)),
        in_specs=[HBM_SPEC, HBM_SPEC], out_specs=(SEM_SPEC, SEM_SPEC, HBM_SPEC, HBM_SPEC, VMEM_SPEC),
        input_output_aliases={0: 2, 1: 3},
        compiler_params=pltpu.CompilerParams(has_side_effects=pltpu.SideEffectType.DATAFLOW_SIDE_EFFECTING),
    )(_in_hbm(buf), _in_hbm(land))


def _allgather_devices_wait(send, recv, buf, land, after, *, name):
    def body(src, land, send, recv, *rest):
        x, y, c = lax.axis_index("x"), lax.axis_index("y"), lax.axis_index("c")
        for k, (px, py, pc) in enumerate(_device_targets(x, y, c)):
            cp = _remote(src, land.at[4 * px + 2 * py + pc], send.at[k], recv.at[k], (px, py, pc))
            cp.wait_send()
            cp.wait_recv()

    out = pl.pallas_call(
        body, name=name, out_shape=(pltpu.HBM(buf.shape, buf.dtype), pltpu.HBM(land.shape, land.dtype)),
        in_specs=[HBM_SPEC, HBM_SPEC, SEM_SPEC, SEM_SPEC] + [ANY_SPEC] * len(after), out_specs=(HBM_SPEC, HBM_SPEC),
        input_output_aliases={0: 0, 1: 1},
        compiler_params=pltpu.CompilerParams(has_side_effects=pltpu.SideEffectType.DATAFLOW_SIDE_EFFECTING),
    )(buf, land, send, recv, *after)
    return out[1]


def _sum_slots(buf, *, name):
    ns, R, C = buf.shape
    tr = _pick(R, 512)
    assert R % tr == 0

    def body(*refs):
        acc = refs[0][...]
        for r in refs[1:ns]:
            acc = acc + r[...]
        refs[ns][...] = acc

    specs = [pl.BlockSpec((None, tr, C), functools.partial(lambda s, i: (s, i, 0), s)) for s in range(ns)]
    return pl.pallas_call(body, name=name, out_shape=S((R, C), buf.dtype), grid=(R // tr,), in_specs=specs,
                          out_specs=pl.BlockSpec((tr, C), lambda i: (i, 0)), compiler_params=_cp("parallel"))(*([buf] * ns))


ADAMW_BLOCK_BYTES = 1 << 20


def _adamw(w, m, v, groups, *, name, layer=None, prev=None):
    shape = w.shape if layer is None else w.shape[1:]
    C = shape[-1]
    Rr = math.prod(shape[:-1])
    tr, tc = Rr, C
    if Rr * C * 4 > ADAMW_BLOCK_BYTES:
        rows = [t for t in range(2 * SUBLANE, Rr + 1, 2 * SUBLANE) if Rr % t == 0 and t * C * 4 <= ADAMW_BLOCK_BYTES]
        if rows:
            tr = max(rows)
        else:
            tc = max(t for t in range(LANE, C + 1, LANE) if C % t == 0 and Rr * t * 4 <= ADAMW_BLOCK_BYTES)
    c1 = 1.0 / (1.0 - ADAM_B1 ** ADAM_STEP)
    c2 = 1.0 / (1.0 - ADAM_B2 ** ADAM_STEP)
    if layer is None:
        to2 = lambda t: t.reshape(Rr, C)
        spec = pl.BlockSpec((tr, tc), lambda i, j: (i, j))
        res_shape = S((Rr, C), f32)
    else:
        to2 = lambda t: t.reshape(layer[1], Rr, C)
        spec = pl.BlockSpec((None, tr, tc), functools.partial(lambda l, i, j: (l, i, j), layer[0]))
        res_shape = S((layer[1], Rr, C), f32)
    wspec, spec = spec, pl.BlockSpec((tr, tc), lambda i, j: (i, j))
    g_specs, g_args, sizes = [], [], []
    for grp in groups:
        sizes.append(len(grp))
        for term in grp:
            if isinstance(term, tuple):
                arr, slot = term
                g_specs.append(pl.BlockSpec((None, tr, tc), functools.partial(lambda s, i, j: (s, i, j), slot)))
                g_args.append(arr.reshape(arr.shape[0], Rr, C))
            else:
                g_specs.append(spec)
                g_args.append(term.reshape(Rr, C))
    nterms = len(g_args)
    prev = () if prev is None else tuple(to2(t) for t in prev)

    def body(w_ref, m_ref, v_ref, *rest):
        t_refs, (g_ref, d_ref, mo_ref, vo_ref) = rest[:nterms], rest[-4:]
        g, pos = None, 0
        for size in sizes:
            part = None
            for r in t_refs[pos:pos + size]:
                t = r[...].astype(f32)
                part = t if part is None else part + t
            pos += size
            g = part if g is None else g + part
        mn = ADAM_B1 * m_ref[...] + (1.0 - ADAM_B1) * g
        vn = ADAM_B2 * v_ref[...] + (1.0 - ADAM_B2) * (g * g)
        g_ref[...] = g
        mo_ref[...] = mn
        vo_ref[...] = vn
        d_ref[...] = -ADAM_LR * ((mn * c1) / (jnp.sqrt(vn * c2) + ADAM_EPS) + ADAM_WD * w_ref[...])

    out = pl.pallas_call(body, name=name, out_shape=(res_shape,) * 4, grid=(Rr // tr, C // tc),
                         in_specs=[wspec] * 3 + g_specs + [ANY_SPEC] * len(prev), out_specs=(wspec,) * 4,
                         input_output_aliases={3 + nterms + k: k for k in range(len(prev))},
                         compiler_params=_cp("parallel", "parallel"))(to2(w), to2(m), to2(v), *g_args, *prev)
    return tuple(o.reshape(w.shape) for o in out)


def _pack_rows(parts, dtype):
    flat = jnp.concatenate([p.reshape(-1).astype(dtype) for p in parts])
    n = flat.shape[0]
    unit = PACK_COLS * 2 * SUBLANE
    padded = -(-n // unit) * unit
    return jnp.pad(flat, (0, padded - n)).reshape(padded // PACK_COLS, PACK_COLS)


def _unpack_rows(flat2d, shapes):
    flat = flat2d.reshape(-1)
    out, off = [], 0
    for shp in shapes:
        n = math.prod(shp)
        out.append(flat[off:off + n].reshape(shp))
        off += n
    return out


def _gather_weights(local, names, dtype, *, name):
    shapes = [local[n].shape for n in names]
    got = _all_gather_chips(_pack_rows([local[n] for n in names], dtype), name=name)
    per_chip = [_unpack_rows(got[s], shapes) for s in range(N_CHIPS)]
    return {n: jnp.concatenate([per_chip[s][k] for s in range(N_CHIPS)], axis=SHARD_AXIS[n]) for k, n in enumerate(names)}


def kernel(x, mem, norm_g, ssm_in_w, ssm_conv_w, ssm_conv_b, ssm_dt_bias, ssm_A_log, ssm_D, ssm_norm_g, ssm_out_w, cf_pw1_w, cf_pw1_b, cf_dw_w, cf_dw_b, cf_ln_g, cf_ln_b, cf_pw2_w, cf_pw2_b, xa_mem_g, xa_q_w, xa_kv_w, xa_o_w, ffn_in_w, ffn_conv_w, ffn_conv_b, ffn_out_w, loss_target, m_norm_g, m_ssm_in_w, m_ssm_conv_w, m_ssm_conv_b, m_ssm_dt_bias, m_ssm_A_log, m_ssm_D, m_ssm_norm_g, m_ssm_out_w, m_cf_pw1_w, m_cf_pw1_b, m_cf_dw_w, m_cf_dw_b, m_cf_ln_g, m_cf_ln_b, m_cf_pw2_w, m_cf_pw2_b, m_xa_mem_g, m_xa_q_w, m_xa_kv_w, m_xa_o_w, m_ffn_in_w, m_ffn_conv_w, m_ffn_conv_b, m_ffn_out_w, v_norm_g, v_ssm_in_w, v_ssm_conv_w, v_ssm_conv_b, v_ssm_dt_bias, v_ssm_A_log, v_ssm_D, v_ssm_norm_g, v_ssm_out_w, v_cf_pw1_w, v_cf_pw1_b, v_cf_dw_w, v_cf_dw_b, v_cf_ln_g, v_cf_ln_b, v_cf_pw2_w, v_cf_pw2_b, v_xa_mem_g, v_xa_q_w, v_xa_kv_w, v_xa_o_w, v_ffn_in_w, v_ffn_conv_w, v_ffn_conv_b, v_ffn_out_w):
    w_local = dict(zip(WEIGHT_NAMES, (norm_g, ssm_in_w, ssm_conv_w, ssm_conv_b, ssm_dt_bias, ssm_A_log, ssm_D, ssm_norm_g,
                                      ssm_out_w, cf_pw1_w, cf_pw1_b, cf_dw_w, cf_dw_b, cf_ln_g, cf_ln_b, cf_pw2_w, cf_pw2_b,
                                      xa_mem_g, xa_q_w, xa_kv_w, xa_o_w, ffn_in_w, ffn_conv_w, ffn_conv_b, ffn_out_w)))
    m_local = dict(zip(WEIGHT_NAMES, (m_norm_g, m_ssm_in_w, m_ssm_conv_w, m_ssm_conv_b, m_ssm_dt_bias, m_ssm_A_log, m_ssm_D,
                                      m_ssm_norm_g, m_ssm_out_w, m_cf_pw1_w, m_cf_pw1_b, m_cf_dw_w, m_cf_dw_b, m_cf_ln_g,
                                      m_cf_ln_b, m_cf_pw2_w, m_cf_pw2_b, m_xa_mem_g, m_xa_q_w, m_xa_kv_w, m_xa_o_w,
                                      m_ffn_in_w, m_ffn_conv_w, m_ffn_conv_b, m_ffn_out_w)))
    v_local = dict(zip(WEIGHT_NAMES, (v_norm_g, v_ssm_in_w, v_ssm_conv_w, v_ssm_conv_b, v_ssm_dt_bias, v_ssm_A_log, v_ssm_D,
                                      v_ssm_norm_g, v_ssm_out_w, v_cf_pw1_w, v_cf_pw1_b, v_cf_dw_w, v_cf_dw_b, v_cf_ln_g,
                                      v_cf_ln_b, v_cf_pw2_w, v_cf_pw2_b, v_xa_mem_g, v_xa_q_w, v_xa_kv_w, v_xa_o_w,
                                      v_ffn_in_w, v_ffn_conv_w, v_ffn_conv_b, v_ffn_out_w)))

    small = [n for n in WEIGHT_NAMES if n not in MATMUL_WEIGHTS]
    small_sharded = [n for n in small if SHARD_AXIS[n] is not None]
    W = {n: w_local[n] for n in small if SHARD_AXIS[n] is None}
    W.update(_gather_weights(w_local, small_sharded, f32, name="gather_small_weights"))

    def layer_index(n, i):
        return i // 2 if n in MIXER_WEIGHTS else i

    def keys_of(i, parts):
        return [(n, layer_index(n, i)) for part in parts for n in _layer_matmul_weights(i, part)]

    def shards(keys):
        return [w_local[n][l].astype(bf16) for n, l in keys]

    def usable(n, a):
        return a.reshape(N_CHIPS * a.shape[1], a.shape[2]) if n in ROW_SHARDED else a

    mixer0 = keys_of(0, ("mixer_in",))
    got0 = _gather_matmul_weights([s.reshape(1, 2, s.shape[0] // 2, s.shape[1]) for s in shards(mixer0)],
                                  name="gather_layer0_mixer")
    gather_groups = {(0, "mixer_out"): keys_of(0, ("mixer_out",)), (0, "rest"): keys_of(0, ("rest",))}
    gather_groups.update({(i, "mixer_in"): keys_of(i, ("mixer", "rest")) for i in range(1, DEPTH)})
    gathers, tokens, landed = {}, [], {}
    for gkey in sorted(gather_groups):
        send, recv, srcs, lands, token = _spread_start(shards(gather_groups[gkey]), False, name="gather_start_%d_%s" % gkey,
                                                       after=(got0[0], W[small_sharded[0]]))
        gathers[gkey] = (send, recv, srcs, lands)
        tokens.append(token)

    def layer_weights(i, part, after):
        if (i, part) == (0, "mixer_in"):
            landed.update({k: g.reshape((N_CHIPS, 2 * g.shape[3], g.shape[4])) for k, g in zip(mixer0, got0)})
        elif (i, part) in gathers:
            _, lands = _spread_wait(*gathers[i, part], (after,), False, name="gather_wait_%d_%s" % (i, part))
            landed.update(zip(gather_groups[i, part], lands))
        return {n: usable(n, landed[n, layer_index(n, i)]) for n in _layer_matmul_weights(i, part)}

    pending, scatters, swaps, own, sib = {}, {}, {}, {}, {}

    def scatter_start(gkey, keys, gl):
        send, recv, srcs, lands, token = _spread_start([gl[k] for k in keys], True, name="grads_start_%d_%s" % gkey)
        scatters[gkey] = (keys, send, recv, srcs, lands)
        return token

    def layer_grads(i, part, gl):
        grads = {(n, layer_index(n, i)): gl[n] for n in _layer_matmul_weights(i, part)}
        behind = []
        if part == "rest":
            if i + 1 < DEPTH:
                keys, send, recv, srcs, lands = scatters.pop((i + 1, "mixer"))
                _, lands = _spread_wait(send, recv, srcs, lands, (grads['xa_kv_w', i],), True,
                                        name="grads_wait_%d" % (i + 1))
                send, recv, srcs, lands, token = _swap_start(lands, name="grads_swap_start_%d" % (i + 1))
                swaps[i + 1] = (keys, send, recv, srcs, lands)
                behind.append(token)
            if i == 0:
                behind.append(scatter_start((0, "rest"), list(grads), grads))
            else:
                pending.update(grads)
        else:
            pending.update(grads)
            if i == 0:
                keys, send, recv, srcs, lands = scatters.pop((0, "rest"))
                _, lands = _spread_wait(send, recv, srcs, lands, (grads['ssm_in_w', 0],), True, name="grads_wait_0_rest")
                send, recv, srcs, lands, token = _swap_start(lands, name="grads_swap_start_0")
                swaps[0] = (keys, send, recv, srcs, lands)
                behind.append(token)
            behind.append(scatter_start((i, "mixer"), list(pending), dict(pending)))
            pending.clear()
        return behind

    sse, gx, gsmall = _device_step(x[0], mem[0], loss_target[0], W, layer_weights, layer_grads, tuple(tokens))

    loss = lax.psum(0.5 * sse[0, 0] / D_MODEL, ("x", "y", "c"))

    small_shapes = [gsmall[n].shape for n in small]
    ag_send, ag_recv, ag_src, ag_land, ag_token = _allgather_devices_start(_pack_rows([gsmall[n] for n in small], f32),
                                                                           name="allgather_small_start")
    last_keys, last_lands = [], []
    for gkey in sorted(scatters):
        keys, send, recv, srcs, lands = scatters[gkey]
        _, lands = _spread_wait(send, recv, srcs, lands, (gx, ag_token), True, name="grads_wait_%d_%s" % gkey)
        last_keys += keys
        last_lands += list(lands)
    last_swap = _swap_start(last_lands, name="grads_swap_start_last")
    for i in sorted(swaps):
        keys, send, recv, srcs, lands = swaps[i]
        mine, theirs = _swap_wait(send, recv, srcs, lands, (gx, last_swap[4]), name="grads_swap_wait_%d" % i)
        own.update(zip(keys, mine))
        sib.update(zip(keys, theirs))

    res = {}

    def adamw_matmul_weight(n, l):
        layers = w_local[n].shape[0]
        groups = [[(own[n, l], s) for s in range(N_CHIPS)], [(sib[n, l], s) for s in range(N_CHIPS)]]
        view = (lambda t: jnp.swapaxes(t, 1, 2)) if n == 'ssm_in_w' else (lambda t: t)
        res[n] = _adamw(view(w_local[n]), view(m_local[n]), view(v_local[n]), groups, name="adamw_%s_%d" % (n, l),
                        layer=(l, layers), prev=res.get(n))

    for n in MATMUL_WEIGHTS:
        for l in range(w_local[n].shape[0]):
            if (n, l) not in last_keys:
                adamw_matmul_weight(n, l)
    done = res[MATMUL_WEIGHTS[-1]][0]
    mine, theirs = _swap_wait(*last_swap[:4], (done,), name="grads_swap_wait_last")
    own.update(zip(last_keys, mine))
    sib.update(zip(last_keys, theirs))
    for n, l in last_keys:
        adamw_matmul_weight(n, l)
    slots = _allgather_devices_wait(ag_send, ag_recv, ag_src, ag_land, (res[last_keys[-1][0]][0],), name="allgather_small_wait")
    gsum = dict(zip(small, _unpack_rows(_sum_slots(slots, name="sum_small_grads"), small_shapes)))
    chip = 2 * lax.axis_index("x") + lax.axis_index("y")

    for n in small:
        g, ax = gsum[n], SHARD_AXIS[n]
        if ax is not None:
            width = w_local[n].shape[ax]
            g = lax.dynamic_slice_in_dim(g, chip * width, width, axis=ax)
        res[n] = _adamw(w_local[n], m_local[n], v_local[n], [[g]], name="adamw_" + n)
    res['ssm_in_w'] = tuple(jnp.swapaxes(t, 1, 2) for t in res['ssm_in_w'])
    return (loss, gx[None], *[res[n][0] for n in WEIGHT_NAMES], *[res[n][1] for n in WEIGHT_NAMES],
            *[res[n][2] for n in WEIGHT_NAMES], *[res[n][3] for n in WEIGHT_NAMES])
```
